```python
import math
import jax, jax.numpy as jnp
from jax import lax
import numpy as np

D_MODEL = 1024
BATCH = 8
SEQ = 4096
DEPTH = 2

N_META = 16
BLOCK = 128
PAD_FRONT = BLOCK - N_META
HEAD_DIM = 64
FOX_HEADS = 8
SWA_Q_HEADS = 8
SWA_KV_HEADS = 2
WINDOW = 128
N_BUCKETS = 32
MAX_DISTANCE = 128
D_FF = 2816
EPS = 1e-6
NEG = -1e30
FORGET_BIAS_INIT = 2.0
FOX_W = FOX_HEADS * HEAD_DIM
SWA_QW = SWA_Q_HEADS * HEAD_DIM
SWA_KVW = SWA_KV_HEADS * HEAD_DIM
D_IN = 3 * FOX_W + FOX_HEADS + SWA_QW + 2 * SWA_KVW + 2 * D_MODEL

kernel_name = "hybrid_fox_swa_sink_macaron_meta"


def rms_norm(x, g):
    xf = x.astype(jnp.float32)
    ms = jnp.mean(xf * xf, axis=-1, keepdims=True)
    return (xf * lax.rsqrt(ms + EPS) * g.astype(jnp.float32)).astype(x.dtype)


def swiglu(h, w_in, w_out):
    gu = h @ w_in
    gate, up = jnp.split(gu, 2, axis=-1)
    return (jax.nn.silu(gate) * up) @ w_out


def t5_bucket(dist):
    n = jnp.maximum(dist, 0)
    max_exact = N_BUCKETS // 2
    nf = jnp.maximum(n, 1).astype(jnp.float32)
    large = max_exact + (jnp.log(nf / max_exact) / math.log(MAX_DISTANCE / max_exact)
                         * (N_BUCKETS - max_exact)).astype(jnp.int32)
    large = jnp.minimum(large, N_BUCKETS - 1)
    return jnp.where(n < max_exact, n, large)


def fox_attention(q, k, v, log_f):
    b, p, h, dh = q.shape
    nb = p // BLOCK
    c_k = jnp.cumsum(log_f, axis=1).transpose(0, 2, 1)
    k_pos = jnp.arange(p)
    k_valid = k_pos >= PAD_FRONT
    qb = q.reshape(b, nb, BLOCK, h, dh).transpose(1, 0, 2, 3, 4)
    cqb = c_k.reshape(b, h, nb, BLOCK).transpose(2, 0, 1, 3)
    scale = dh ** -0.5

    def one_block(args):
        qi, cqi, i = args
        q_pos = i * BLOCK + jnp.arange(BLOCK)
        s = jnp.einsum('bqhd,bkhd->bhqk', qi, k, preferred_element_type=jnp.float32) * scale
        s = s + cqi[..., :, None] - c_k[:, :, None, :]
        mask = (k_pos[None, :] <= q_pos[:, None]) & k_valid[None, :]
        s = jnp.where(mask[None, None], s, NEG)
        pr = jax.nn.softmax(s, axis=-1)
        return jnp.einsum('bhqk,bkhd->bqhd', pr.astype(v.dtype), v)

    out = lax.map(one_block, (qb, cqb, jnp.arange(nb)))
    return out.transpose(1, 0, 2, 3, 4).reshape(b, p, h * dh)


def swa_attention(q, k, v, sinks, rel_bias_table):
    b, p, hq, dh = q.shape
    kv = k.shape[2]
    r = hq // kv
    nb = p // BLOCK
    scale = dh ** -0.5
    qb = q.reshape(b, nb, BLOCK, kv, r, dh)

    def band(a):
        a_ext = jnp.concatenate([jnp.zeros_like(a[:, :BLOCK]), a], axis=1)
        a_ext = a_ext.reshape(b, nb + 1, BLOCK, kv, dh)
        return jnp.concatenate([a_ext[:, :-1], a_ext[:, 1:]], axis=2)

    kb, vb = band(k), band(v)
    km, vm = k[:, PAD_FRONT:BLOCK], v[:, PAD_FRONT:BLOCK]
    q_pos = jnp.arange(nb)[:, None] * BLOCK + jnp.arange(BLOCK)[None, :]
    band_pos = (jnp.arange(nb)[:, None] - 1) * BLOCK + jnp.arange(2 * BLOCK)[None, :]
    meta_pos = PAD_FRONT + jnp.arange(N_META)
    d_band = q_pos[:, :, None] - band_pos[:, None, :]
    m_band = (d_band >= 0) & (d_band < WINDOW) & (band_pos[:, None, :] >= PAD_FRONT)
    d_meta = q_pos[:, :, None] - meta_pos[None, None, :]
    m_meta = d_meta >= WINDOW
    table = rel_bias_table.astype(jnp.float32)
    bias_band = table[t5_bucket(d_band)].transpose(0, 3, 1, 2).reshape(nb, kv, r, BLOCK, 2 * BLOCK)
    bias_meta = table[t5_bucket(d_meta)].transpose(0, 3, 1, 2).reshape(nb, kv, r, BLOCK, N_META)

    s_band = jnp.einsum('bnqgrd,bnkgd->bngrqk', qb, kb, preferred_element_type=jnp.float32) * scale + bias_band[None]
    s_band = jnp.where(m_band[None, :, None, None], s_band, NEG)
    s_meta = jnp.einsum('bnqgrd,bmgd->bngrqm', qb, km, preferred_element_type=jnp.float32) * scale + bias_meta[None]
    s_meta = jnp.where(m_meta[None, :, None, None], s_meta, NEG)
    s = jnp.concatenate([s_band, s_meta], axis=-1)
    sink = sinks.astype(jnp.float32).reshape(kv, r)[None, None, :, :, None, None]
    mx = jnp.maximum(jnp.max(s, axis=-1, keepdims=True), sink)
    pr = jnp.exp(s - mx)
    pr = pr / (jnp.sum(pr, axis=-1, keepdims=True) + jnp.exp(sink - mx))
    pr = pr.astype(v.dtype)
    out = (jnp.einsum('bngrqk,bnkgd->bnqgrd', pr[..., :2 * BLOCK], vb)
           + jnp.einsum('bngrqm,bmgd->bnqgrd', pr[..., 2 * BLOCK:], vm))
    return out.reshape(b, p, hq * dh)


def token_mixer(h, rel_bias_table, w_in, forget_bias, fox_q_norm, fox_k_norm,
                swa_q_norm, swa_k_norm, swa_sinks, w_branch_fox, w_branch_swa, w_out):
    b, p, _ = h.shape
    proj = h @ w_in
    widths = [FOX_W, FOX_W, FOX_W, FOX_HEADS, SWA_QW, SWA_KVW, SWA_KVW, D_MODEL]
    idx = list(np.cumsum(widths))
    qa, ka, va, fa, qb, kb, vb, ga, gb = jnp.split(proj, idx, axis=-1)
    qa = rms_norm(qa.reshape(b, p, FOX_HEADS, HEAD_DIM), fox_q_norm)
    ka = rms_norm(ka.reshape(b, p, FOX_HEADS, HEAD_DIM), fox_k_norm)
    va = va.reshape(b, p, FOX_HEADS, HEAD_DIM)
    log_f = jax.nn.log_sigmoid(fa.astype(jnp.float32) + forget_bias.astype(jnp.float32))
    qb = rms_norm(qb.reshape(b, p, SWA_Q_HEADS, HEAD_DIM), swa_q_norm)
    kb = rms_norm(kb.reshape(b, p, SWA_KV_HEADS, HEAD_DIM), swa_k_norm)
    vb = vb.reshape(b, p, SWA_KV_HEADS, HEAD_DIM)
    o_fox = fox_attention(qa, ka, va, log_f)
    o_swa = swa_attention(qb, kb, vb, swa_sinks, rel_bias_table)
    y = jax.nn.sigmoid(ga) * (o_fox @ w_branch_fox) + jax.nn.sigmoid(gb) * (o_swa @ w_branch_swa)
    return y @ w_out


def _fwd_setup_inputs(seed: int = 0) -> dict:
    key = jax.random.key(seed)
    ks = jax.random.split(key, 24)
    f32 = jnp.float32

    def nrm(k, shape, scale):
        return jax.random.normal(k, shape, f32) * scale

    def gain(k, shape):
        return 1.0 + 0.1 * jax.random.normal(k, shape, f32)

    return {
        "x": nrm(ks[0], (BATCH, SEQ, D_MODEL), 1.0),
        "meta_tokens": nrm(ks[1], (N_META, D_MODEL), 1.0),
        "rel_bias_table": nrm(ks[2], (N_BUCKETS, SWA_Q_HEADS), 0.5),
        "ffn1_norm": gain(ks[3], (DEPTH, D_MODEL)),
        "ffn1_w_in": nrm(ks[4], (DEPTH, D_MODEL, 2 * D_FF), D_MODEL ** -0.5),
        "ffn1_w_out": nrm(ks[5], (DEPTH, D_FF, D_MODEL), D_FF ** -0.5),
        "mix_norm": gain(ks[6], (DEPTH, D_MODEL)),
        "w_in": nrm(ks[7], (DEPTH, D_MODEL, D_IN), D_MODEL ** -0.5),
        "forget_bias": FORGET_BIAS_INIT + 0.1 * jax.random.normal(ks[8], (DEPTH, FOX_HEADS), f32),
        "fox_q_norm": gain(ks[9], (DEPTH, HEAD_DIM)),
        "fox_k_norm": gain(ks[10], (DEPTH, HEAD_DIM)),
        "swa_q_norm": gain(ks[11], (DEPTH, HEAD_DIM)),
        "swa_k_norm": gain(ks[12], (DEPTH, HEAD_DIM)),
        "swa_sinks": nrm(ks[13], (DEPTH, SWA_Q_HEADS), 0.5),
        "w_branch_fox": nrm(ks[14], (DEPTH, FOX_W, D_MODEL), FOX_W ** -0.5),
        "w_branch_swa": nrm(ks[15], (DEPTH, SWA_QW, D_MODEL), SWA_QW ** -0.5),
        "w_out": nrm(ks[16], (DEPTH, D_MODEL, D_MODEL), D_MODEL ** -0.5),
        "ffn2_norm": gain(ks[17], (DEPTH, D_MODEL)),
        "ffn2_w_in": nrm(ks[18], (DEPTH, D_MODEL, 2 * D_FF), D_MODEL ** -0.5),
        "ffn2_w_out": nrm(ks[19], (DEPTH, D_FF, D_MODEL), D_FF ** -0.5),
    }


def _fwd_reference(x, meta_tokens, rel_bias_table, ffn1_norm, ffn1_w_in, ffn1_w_out, mix_norm,
              w_in, forget_bias, fox_q_norm, fox_k_norm, swa_q_norm, swa_k_norm, swa_sinks,
              w_branch_fox, w_branch_swa, w_out, ffn2_norm, ffn2_w_in, ffn2_w_out):
    b = x.shape[0]
    pad = jnp.zeros((b, PAD_FRONT, D_MODEL), x.dtype)
    meta = jnp.broadcast_to(meta_tokens.astype(x.dtype)[None], (b, N_META, D_MODEL))
    h = jnp.concatenate([pad, meta, x], axis=1)
    for l in range(DEPTH):
        h = h + 0.5 * swiglu(rms_norm(h, ffn1_norm[l]), ffn1_w_in[l], ffn1_w_out[l])
        h = h + token_mixer(rms_norm(h, mix_norm[l]), rel_bias_table, w_in[l], forget_bias[l],
                            fox_q_norm[l], fox_k_norm[l], swa_q_norm[l], swa_k_norm[l],
                            swa_sinks[l], w_branch_fox[l], w_branch_swa[l], w_out[l])
        h = h + 0.5 * swiglu(rms_norm(h, ffn2_norm[l]), ffn2_w_in[l], ffn2_w_out[l])
    return h[:, BLOCK:]


import jax as _jax
import jax.numpy as _jnp

TWIN_FORMAT = 'train_step'
FWD_PARAMS = ['x', 'meta_tokens', 'rel_bias_table', 'ffn1_norm', 'ffn1_w_in', 'ffn1_w_out', 'mix_norm', 'w_in', 'forget_bias', 'fox_q_norm', 'fox_k_norm', 'swa_q_norm', 'swa_k_norm', 'swa_sinks', 'w_branch_fox', 'w_branch_swa', 'w_out', 'ffn2_norm', 'ffn2_w_in', 'ffn2_w_out']
TWIN_WEIGHTS = ['meta_tokens', 'rel_bias_table', 'ffn1_norm', 'ffn1_w_in', 'ffn1_w_out', 'mix_norm', 'w_in', 'forget_bias', 'fox_q_norm', 'fox_k_norm', 'swa_q_norm', 'swa_k_norm', 'swa_sinks', 'w_branch_fox', 'w_branch_swa', 'w_out', 'ffn2_norm', 'ffn2_w_in', 'ffn2_w_out']
TWIN_DIFF_INPUT = 'x'
TWIN_INPUTS = ['x', 'meta_tokens', 'rel_bias_table', 'ffn1_norm', 'ffn1_w_in', 'ffn1_w_out', 'mix_norm', 'w_in', 'forget_bias', 'fox_q_norm', 'fox_k_norm', 'swa_q_norm', 'swa_k_norm', 'swa_sinks', 'w_branch_fox', 'w_branch_swa', 'w_out', 'ffn2_norm', 'ffn2_w_in', 'ffn2_w_out', 'loss_target', 'm_meta_tokens', 'm_rel_bias_table', 'm_ffn1_norm', 'm_ffn1_w_in', 'm_ffn1_w_out', 'm_mix_norm', 'm_w_in', 'm_forget_bias', 'm_fox_q_norm', 'm_fox_k_norm', 'm_swa_q_norm', 'm_swa_k_norm', 'm_swa_sinks', 'm_w_branch_fox', 'm_w_branch_swa', 'm_w_out', 'm_ffn2_norm', 'm_ffn2_w_in', 'm_ffn2_w_out', 'v_meta_tokens', 'v_rel_bias_table', 'v_ffn1_norm', 'v_ffn1_w_in', 'v_ffn1_w_out', 'v_mix_norm', 'v_w_in', 'v_forget_bias', 'v_fox_q_norm', 'v_fox_k_norm', 'v_swa_q_norm', 'v_swa_k_norm', 'v_swa_sinks', 'v_w_branch_fox', 'v_w_branch_swa', 'v_w_out', 'v_ffn2_norm', 'v_ffn2_w_in', 'v_ffn2_w_out']
TWIN_OUTPUTS = ['loss', 'grad_x', 'grad_meta_tokens', 'grad_rel_bias_table', 'grad_ffn1_norm', 'grad_ffn1_w_in', 'grad_ffn1_w_out', 'grad_mix_norm', 'grad_w_in', 'grad_forget_bias', 'grad_fox_q_norm', 'grad_fox_k_norm', 'grad_swa_q_norm', 'grad_swa_k_norm', 'grad_swa_sinks', 'grad_w_branch_fox', 'grad_w_branch_swa', 'grad_w_out', 'grad_ffn2_norm', 'grad_ffn2_w_in', 'grad_ffn2_w_out', 'delta_meta_tokens', 'delta_rel_bias_table', 'delta_ffn1_norm', 'delta_ffn1_w_in', 'delta_ffn1_w_out', 'delta_mix_norm', 'delta_w_in', 'delta_forget_bias', 'delta_fox_q_norm', 'delta_fox_k_norm', 'delta_swa_q_norm', 'delta_swa_k_norm', 'delta_swa_sinks', 'delta_w_branch_fox', 'delta_w_branch_swa', 'delta_w_out', 'delta_ffn2_norm', 'delta_ffn2_w_in', 'delta_ffn2_w_out', 'new_m_meta_tokens', 'new_m_rel_bias_table', 'new_m_ffn1_norm', 'new_m_ffn1_w_in', 'new_m_ffn1_w_out', 'new_m_mix_norm', 'new_m_w_in', 'new_m_forget_bias', 'new_m_fox_q_norm', 'new_m_fox_k_norm', 'new_m_swa_q_norm', 'new_m_swa_k_norm', 'new_m_swa_sinks', 'new_m_w_branch_fox', 'new_m_w_branch_swa', 'new_m_w_out', 'new_m_ffn2_norm', 'new_m_ffn2_w_in', 'new_m_ffn2_w_out', 'new_v_meta_tokens', 'new_v_rel_bias_table', 'new_v_ffn1_norm', 'new_v_ffn1_w_in', 'new_v_ffn1_w_out', 'new_v_mix_norm', 'new_v_w_in', 'new_v_forget_bias', 'new_v_fox_q_norm', 'new_v_fox_k_norm', 'new_v_swa_q_norm', 'new_v_swa_k_norm', 'new_v_swa_sinks', 'new_v_w_branch_fox', 'new_v_w_branch_swa', 'new_v_w_out', 'new_v_ffn2_norm', 'new_v_ffn2_w_in', 'new_v_ffn2_w_out']
TWIN_LEAF_KINDS = {'loss': 'loss', 'grad_x': 'grad_x', 'grad_meta_tokens': 'grad_w', 'grad_rel_bias_table': 'grad_w', 'grad_ffn1_norm': 'grad_w', 'grad_ffn1_w_in': 'grad_w', 'grad_ffn1_w_out': 'grad_w', 'grad_mix_norm': 'grad_w', 'grad_w_in': 'grad_w', 'grad_forget_bias': 'grad_w', 'grad_fox_q_norm': 'grad_w', 'grad_fox_k_norm': 'grad_w', 'grad_swa_q_norm': 'grad_w', 'grad_swa_k_norm': 'grad_w', 'grad_swa_sinks': 'grad_w', 'grad_w_branch_fox': 'grad_w', 'grad_w_branch_swa': 'grad_w', 'grad_w_out': 'grad_w', 'grad_ffn2_norm': 'grad_w', 'grad_ffn2_w_in': 'grad_w', 'grad_ffn2_w_out': 'grad_w', 'delta_meta_tokens': 'delta_w', 'delta_rel_bias_table': 'delta_w', 'delta_ffn1_norm': 'delta_w', 'delta_ffn1_w_in': 'delta_w', 'delta_ffn1_w_out': 'delta_w', 'delta_mix_norm': 'delta_w', 'delta_w_in': 'delta_w', 'delta_forget_bias': 'delta_w', 'delta_fox_q_norm': 'delta_w', 'delta_fox_k_norm': 'delta_w', 'delta_swa_q_norm': 'delta_w', 'delta_swa_k_norm': 'delta_w', 'delta_swa_sinks': 'delta_w', 'delta_w_branch_fox': 'delta_w', 'delta_w_branch_swa': 'delta_w', 'delta_w_out': 'delta_w', 'delta_ffn2_norm': 'delta_w', 'delta_ffn2_w_in': 'delta_w', 'delta_ffn2_w_out': 'delta_w', 'new_m_meta_tokens': 'new_m', 'new_m_rel_bias_table': 'new_m', 'new_m_ffn1_norm': 'new_m', 'new_m_ffn1_w_in': 'new_m', 'new_m_ffn1_w_out': 'new_m', 'new_m_mix_norm': 'new_m', 'new_m_w_in': 'new_m', 'new_m_forget_bias': 'new_m', 'new_m_fox_q_norm': 'new_m', 'new_m_fox_k_norm': 'new_m', 'new_m_swa_q_norm': 'new_m', 'new_m_swa_k_norm': 'new_m', 'new_m_swa_sinks': 'new_m', 'new_m_w_branch_fox': 'new_m', 'new_m_w_branch_swa': 'new_m', 'new_m_w_out': 'new_m', 'new_m_ffn2_norm': 'new_m', 'new_m_ffn2_w_in': 'new_m', 'new_m_ffn2_w_out': 'new_m', 'new_v_meta_tokens': 'new_v', 'new_v_rel_bias_table': 'new_v', 'new_v_ffn1_norm': 'new_v', 'new_v_ffn1_w_in': 'new_v', 'new_v_ffn1_w_out': 'new_v', 'new_v_mix_norm': 'new_v', 'new_v_w_in': 'new_v', 'new_v_forget_bias': 'new_v', 'new_v_fox_q_norm': 'new_v', 'new_v_fox_k_norm': 'new_v', 'new_v_swa_q_norm': 'new_v', 'new_v_swa_k_norm': 'new_v', 'new_v_swa_sinks': 'new_v', 'new_v_w_branch_fox': 'new_v', 'new_v_w_branch_swa': 'new_v', 'new_v_w_out': 'new_v', 'new_v_ffn2_norm': 'new_v', 'new_v_ffn2_w_in': 'new_v', 'new_v_ffn2_w_out': 'new_v'}


def _forward(args):
    return _fwd_reference(*[args[k] for k in FWD_PARAMS])


def _output_shape():
    def fwd():
        inp = _fwd_setup_inputs(0)
        return _fwd_reference(*[inp[k] for k in FWD_PARAMS])
    out = _jax.eval_shape(fwd)
    return out.shape, out.dtype

N_MICROBATCH = 1
ADAM_LR = 0.001
ADAM_B1 = 0.9
ADAM_B2 = 0.999
ADAM_EPS = 1e-08
ADAM_WD = 0.01
ADAM_STEP = 10
PER_EXAMPLE_BATCH_AXIS = {'x': 0, 'loss_target': 0}
SHARED_INPUTS = []
_WEIGHT_DTYPES = {'meta_tokens': _jnp.float32, 'rel_bias_table': _jnp.float32, 'ffn1_norm': _jnp.float32, 'ffn1_w_in': _jnp.float32, 'ffn1_w_out': _jnp.float32, 'mix_norm': _jnp.float32, 'w_in': _jnp.float32, 'forget_bias': _jnp.float32, 'fox_q_norm': _jnp.float32, 'fox_k_norm': _jnp.float32, 'swa_q_norm': _jnp.float32, 'swa_k_norm': _jnp.float32, 'swa_sinks': _jnp.float32, 'w_branch_fox': _jnp.float32, 'w_branch_swa': _jnp.float32, 'w_out': _jnp.float32, 'ffn2_norm': _jnp.float32, 'ffn2_w_in': _jnp.float32, 'ffn2_w_out': _jnp.float32}
MOMENT_SCALE = {'meta_tokens': 2.655929e-02, 'rel_bias_table': 1.200326e+00, 'ffn1_norm': 6.236753e+00, 'ffn1_w_in': 8.569380e-02, 'ffn1_w_out': 1.494275e-01, 'mix_norm': 2.373683e+00, 'w_in': 1.081207e-01, 'forget_bias': 6.639441e+01, 'fox_q_norm': 9.647685e+00, 'fox_k_norm': 9.720255e+00, 'swa_q_norm': 2.629588e+00, 'swa_k_norm': 2.584952e+00, 'swa_sinks': 1.930326e-01, 'w_branch_fox': 1.349266e-01, 'w_branch_swa': 6.026414e-02, 'w_out': 1.371211e-01, 'ffn2_norm': 6.334948e+00, 'ffn2_w_in': 7.770290e-02, 'ffn2_w_out': 1.384032e-01}


def _to_microbatches(a, axis):
    t = _jnp.moveaxis(a, axis, 0)
    t = t.reshape((N_MICROBATCH, t.shape[0] // N_MICROBATCH) + t.shape[1:])
    return _jnp.moveaxis(t, 1, axis + 1)


def setup_inputs(seed: int = 0) -> dict:
    inp = _fwd_setup_inputs(seed)
    key = _jax.random.fold_in(_jax.random.key(seed), 7919)
    shape, _ = _output_shape()
    out = dict(inp)
    out["loss_target"] = _jax.random.normal(_jax.random.fold_in(key, 0), shape, _jnp.float32)
    for i, name in enumerate(TWIN_WEIGHTS):
        w = inp[name].astype(_jnp.float32)
        if MOMENT_SCALE is None:
            s = _jnp.sqrt(_jnp.mean(_jnp.square(w)) + 1e-30)
        else:
            s = MOMENT_SCALE[name]
        km, kv = _jax.random.split(_jax.random.fold_in(key, i + 1))
        out[name] = w
        out["m_" + name] = s * _jax.random.normal(km, w.shape, _jnp.float32)
        out["v_" + name] = (s * s) * _jax.random.uniform(kv, w.shape, _jnp.float32, 0.5, 1.5)
    if N_MICROBATCH > 1:
        for name, axis in PER_EXAMPLE_BATCH_AXIS.items():
            out[name] = _to_microbatches(out[name], axis)
    return {'x': out['x'], 'meta_tokens': out['meta_tokens'], 'rel_bias_table': out['rel_bias_table'], 'ffn1_norm': out['ffn1_norm'], 'ffn1_w_in': out['ffn1_w_in'], 'ffn1_w_out': out['ffn1_w_out'], 'mix_norm': out['mix_norm'], 'w_in': out['w_in'], 'forget_bias': out['forget_bias'], 'fox_q_norm': out['fox_q_norm'], 'fox_k_norm': out['fox_k_norm'], 'swa_q_norm': out['swa_q_norm'], 'swa_k_norm': out['swa_k_norm'], 'swa_sinks': out['swa_sinks'], 'w_branch_fox': out['w_branch_fox'], 'w_branch_swa': out['w_branch_swa'], 'w_out': out['w_out'], 'ffn2_norm': out['ffn2_norm'], 'ffn2_w_in': out['ffn2_w_in'], 'ffn2_w_out': out['ffn2_w_out'], 'loss_target': out['loss_target'], 'm_meta_tokens': out['m_meta_tokens'], 'm_rel_bias_table': out['m_rel_bias_table'], 'm_ffn1_norm': out['m_ffn1_norm'], 'm_ffn1_w_in': out['m_ffn1_w_in'], 'm_ffn1_w_out': out['m_ffn1_w_out'], 'm_mix_norm': out['m_mix_norm'], 'm_w_in': out['m_w_in'], 'm_forget_bias': out['m_forget_bias'], 'm_fox_q_norm': out['m_fox_q_norm'], 'm_fox_k_norm': out['m_fox_k_norm'], 'm_swa_q_norm': out['m_swa_q_norm'], 'm_swa_k_norm': out['m_swa_k_norm'], 'm_swa_sinks': out['m_swa_sinks'], 'm_w_branch_fox': out['m_w_branch_fox'], 'm_w_branch_swa': out['m_w_branch_swa'], 'm_w_out': out['m_w_out'], 'm_ffn2_norm': out['m_ffn2_norm'], 'm_ffn2_w_in': out['m_ffn2_w_in'], 'm_ffn2_w_out': out['m_ffn2_w_out'], 'v_meta_tokens': out['v_meta_tokens'], 'v_rel_bias_table': out['v_rel_bias_table'], 'v_ffn1_norm': out['v_ffn1_norm'], 'v_ffn1_w_in': out['v_ffn1_w_in'], 'v_ffn1_w_out': out['v_ffn1_w_out'], 'v_mix_norm': out['v_mix_norm'], 'v_w_in': out['v_w_in'], 'v_forget_bias': out['v_forget_bias'], 'v_fox_q_norm': out['v_fox_q_norm'], 'v_fox_k_norm': out['v_fox_k_norm'], 'v_swa_q_norm': out['v_swa_q_norm'], 'v_swa_k_norm': out['v_swa_k_norm'], 'v_swa_sinks': out['v_swa_sinks'], 'v_w_branch_fox': out['v_w_branch_fox'], 'v_w_branch_swa': out['v_w_branch_swa'], 'v_w_out': out['v_w_out'], 'v_ffn2_norm': out['v_ffn2_norm'], 'v_ffn2_w_in': out['v_ffn2_w_in'], 'v_ffn2_w_out': out['v_ffn2_w_out']}


def _loss(weights, diff, rest, loss_target):
    with _jax.named_scope("forward"):
        args = {**rest, TWIN_DIFF_INPUT: diff, **{k: w.astype(_WEIGHT_DTYPES[k]) for k, w in weights.items()}}
        y = _forward(args)
    with _jax.named_scope("loss_head"):
        err = _jnp.square(y.astype(_jnp.float32) - loss_target)
        return 0.5 * _jnp.sum(_jnp.mean(err, axis=-1)) if err.ndim else 0.5 * err


def _adamw(w, g, m, v):
    m = ADAM_B1 * m + (1.0 - ADAM_B1) * g
    v = ADAM_B2 * v + (1.0 - ADAM_B2) * _jnp.square(g)
    m_hat = m / (1.0 - ADAM_B1 ** ADAM_STEP)
    v_hat = v / (1.0 - ADAM_B2 ** ADAM_STEP)
    delta = -ADAM_LR * (m_hat / (_jnp.sqrt(v_hat) + ADAM_EPS) + ADAM_WD * w)
    return delta, m, v


def reference(x, meta_tokens, rel_bias_table, ffn1_norm, ffn1_w_in, ffn1_w_out, mix_norm, w_in, forget_bias, fox_q_norm, fox_k_norm, swa_q_norm, swa_k_norm, swa_sinks, w_branch_fox, w_branch_swa, w_out, ffn2_norm, ffn2_w_in, ffn2_w_out, loss_target, m_meta_tokens, m_rel_bias_table, m_ffn1_norm, m_ffn1_w_in, m_ffn1_w_out, m_mix_norm, m_w_in, m_forget_bias, m_fox_q_norm, m_fox_k_norm, m_swa_q_norm, m_swa_k_norm, m_swa_sinks, m_w_branch_fox, m_w_branch_swa, m_w_out, m_ffn2_norm, m_ffn2_w_in, m_ffn2_w_out, v_meta_tokens, v_rel_bias_table, v_ffn1_norm, v_ffn1_w_in, v_ffn1_w_out, v_mix_norm, v_w_in, v_forget_bias, v_fox_q_norm, v_fox_k_norm, v_swa_q_norm, v_swa_k_norm, v_swa_sinks, v_w_branch_fox, v_w_branch_swa, v_w_out, v_ffn2_norm, v_ffn2_w_in, v_ffn2_w_out):
    given = dict(x=x, meta_tokens=meta_tokens, rel_bias_table=rel_bias_table, ffn1_norm=ffn1_norm, ffn1_w_in=ffn1_w_in, ffn1_w_out=ffn1_w_out, mix_norm=mix_norm, w_in=w_in, forget_bias=forget_bias, fox_q_norm=fox_q_norm, fox_k_norm=fox_k_norm, swa_q_norm=swa_q_norm, swa_k_norm=swa_k_norm, swa_sinks=swa_sinks, w_branch_fox=w_branch_fox, w_branch_swa=w_branch_swa, w_out=w_out, ffn2_norm=ffn2_norm, ffn2_w_in=ffn2_w_in, ffn2_w_out=ffn2_w_out, loss_target=loss_target, m_meta_tokens=m_meta_tokens, m_rel_bias_table=m_rel_bias_table, m_ffn1_norm=m_ffn1_norm, m_ffn1_w_in=m_ffn1_w_in, m_ffn1_w_out=m_ffn1_w_out, m_mix_norm=m_mix_norm, m_w_in=m_w_in, m_forget_bias=m_forget_bias, m_fox_q_norm=m_fox_q_norm, m_fox_k_norm=m_fox_k_norm, m_swa_q_norm=m_swa_q_norm, m_swa_k_norm=m_swa_k_norm, m_swa_sinks=m_swa_sinks, m_w_branch_fox=m_w_branch_fox, m_w_branch_swa=m_w_branch_swa, m_w_out=m_w_out, m_ffn2_norm=m_ffn2_norm, m_ffn2_w_in=m_ffn2_w_in, m_ffn2_w_out=m_ffn2_w_out, v_meta_tokens=v_meta_tokens, v_rel_bias_table=v_rel_bias_table, v_ffn1_norm=v_ffn1_norm, v_ffn1_w_in=v_ffn1_w_in, v_ffn1_w_out=v_ffn1_w_out, v_mix_norm=v_mix_norm, v_w_in=v_w_in, v_forget_bias=v_forget_bias, v_fox_q_norm=v_fox_q_norm, v_fox_k_norm=v_fox_k_norm, v_swa_q_norm=v_swa_q_norm, v_swa_k_norm=v_swa_k_norm, v_swa_sinks=v_swa_sinks, v_w_branch_fox=v_w_branch_fox, v_w_branch_swa=v_w_branch_swa, v_w_out=v_w_out, v_ffn2_norm=v_ffn2_norm, v_ffn2_w_in=v_ffn2_w_in, v_ffn2_w_out=v_ffn2_w_out)
    weights = {n: given[n] for n in TWIN_WEIGHTS}
    shared = {n: given[n] for n in SHARED_INPUTS}
    per_example = {n: given[n] for n in ['x']}
    grad_fn = _jax.value_and_grad(_loss, argnums=(0, 1))

    def one_microbatch(ex, loss_target):
        ex = dict(ex)
        diff = ex.pop(TWIN_DIFF_INPUT)
        return grad_fn(weights, diff, {**shared, **ex}, loss_target)

    if N_MICROBATCH == 1:
        loss, (grad_w, grad_x) = one_microbatch(per_example, given["loss_target"])
    else:
        def body(carry, xs):
            loss_sum, grad_sum = carry
            l_k, (gw_k, gx_k) = one_microbatch(xs[0], xs[1])
            with _jax.named_scope("update"):
                return (loss_sum + l_k, _jax.tree.map(_jnp.add, grad_sum, gw_k)), gx_k

        init = (_jnp.zeros((), _jnp.float32), _jax.tree.map(_jnp.zeros_like, weights))
        (loss, grad_w), grad_x = _jax.lax.scan(body, init, (per_example, given["loss_target"]))
    with _jax.named_scope("update"):
        delta_w, new_m, new_v = {}, {}, {}
        for n in TWIN_WEIGHTS:
            delta_w[n], new_m[n], new_v[n] = _adamw(weights[n], grad_w[n], given["m_" + n], given["v_" + n])
    return (loss, grad_x, *[grad_w[n] for n in TWIN_WEIGHTS], *[delta_w[n] for n in TWIN_WEIGHTS],
            *[new_m[n] for n in TWIN_WEIGHTS], *[new_v[n] for n in TWIN_WEIGHTS])
```

```python
import functools
import math

import numpy as np
import jax
import jax.numpy as jnp
from jax import lax
from jax.experimental import pallas as pl
from jax.experimental.pallas import tpu as pltpu

F32 = jnp.float32
BF16 = jnp.bfloat16
EPS = 1e-6
NEG = -1e30
HEAD_DIM = 64
LANES = 128
N_META = 16
PAD_FRONT = LANES - N_META
N_BUCKETS = 32
MAX_DISTANCE = 128
N_DEV = 8
ADAM_LR, ADAM_B1, ADAM_B2, ADAM_EPS, ADAM_WD, ADAM_STEP = 0.001, 0.9, 0.999, 1e-08, 0.01, 10
VMEM_LIMIT = 56 * 1024 * 1024
MESH = pl.DeviceIdType.MESH


def _params(n_grid):
    return pltpu.CompilerParams(dimension_semantics=("arbitrary",) * n_grid,
                                vmem_limit_bytes=VMEM_LIMIT)


def _dot(a, b):
    return jnp.dot(a, b, preferred_element_type=F32)


def _dot_nt(a, b):
    return lax.dot_general(a, b, (((1,), (1,)), ((), ())), preferred_element_type=F32)


def _dot_tn(a, b):
    return lax.dot_general(a, b, (((0,), (0,)), ((), ())), preferred_element_type=F32)


def _rms(x):
    r = lax.rsqrt(jnp.mean(x * x, axis=-1, keepdims=True) + EPS)
    return x * r, r


def _rms_bwd(x, g, dn):
    xh, r = _rms(x)
    dxh = dn * g
    dx = r * (dxh - xh * jnp.mean(dxh * xh, axis=-1, keepdims=True))
    return dx, jnp.sum(dn * xh, axis=0, keepdims=True)


def _split2(v):
    hi = v.astype(BF16)
    return hi, (v - hi.astype(F32)).astype(BF16)


def _split3(v):
    hi = v.astype(BF16)
    r1 = v - hi.astype(F32)
    mid = r1.astype(BF16)
    return hi, mid, (r1 - mid.astype(F32)).astype(BF16)


def _group_ones():
    r = lax.broadcasted_iota(jnp.int32, (LANES, LANES), 0) // HEAD_DIM
    c = lax.broadcasted_iota(jnp.int32, (LANES, LANES), 1) // HEAD_DIM
    return jnp.where(r == c, 1.0, 0.0).astype(BF16)


def _group_mean(v, ones):
    hi, lo = _split2(v)
    return (_dot(hi, ones) + _dot(lo, ones)) * (1.0 / HEAD_DIM)


def _row_tile(m):
    return 384 if m % 384 == 0 else LANES


def _peer(k):
    x, y, c = lax.axis_index("x"), lax.axis_index("y"), lax.axis_index("c")
    px = 1 - x if k & 4 else x
    py = 1 - y if k & 2 else y
    pc = 1 - c if k & 1 else c
    return (px, py, pc), 4 * px + 2 * py + pc


def _exchange_body(src_ref, dst_ref, send_sems, recv_sems, local_sem, bcast):
    x, y, c = lax.axis_index("x"), lax.axis_index("y"), lax.axis_index("c")
    me = 4 * x + 2 * y + c
    mine = pltpu.make_async_copy(src_ref.at[0 if bcast else me], dst_ref.at[me], local_sem)
    mine.start()
    sends = []
    for k in range(1, N_DEV):
        dev, idx = _peer(k)
        cp = pltpu.make_async_remote_copy(
            src_ref=src_ref.at[0 if bcast else idx], dst_ref=dst_ref.at[me],
            send_sem=send_sems.at[k - 1], recv_sem=recv_sems.at[k - 1],
            device_id=dev, device_id_type=MESH)
        cp.start()
        sends.append(cp)
    for k in range(1, N_DEV):
        dev, idx = _peer(k)
        pltpu.make_async_remote_copy(
            src_ref=src_ref.at[0], dst_ref=dst_ref.at[idx],
            send_sem=send_sems.at[k - 1], recv_sem=recv_sems.at[k - 1],
            device_id=dev, device_id_type=MESH).wait_recv()
    for cp in sends:
        cp.wait_send()
    mine.wait()


def exchange_hbm(srcs, bcast, name):
    n = len(srcs)

    def body(*refs):
        src_refs, dst_refs = refs[:n], refs[n:2 * n]
        send_sems, recv_sems, local_sems = refs[2 * n:]
        x, y, c = lax.axis_index("x"), lax.axis_index("y"), lax.axis_index("c")
        me = 4 * x + 2 * y + c
        own, sends = [], []
        for a, (s, d) in enumerate(zip(src_refs, dst_refs)):
            mine = pltpu.make_async_copy(s if bcast else s.at[me], d.at[me], local_sems.at[a])
            mine.start()
            own.append(mine)
            for k in range(1, N_DEV):
                dev, idx = _peer(k)
                cp = pltpu.make_async_remote_copy(
                    src_ref=s if bcast else s.at[idx], dst_ref=d.at[me],
                    send_sem=send_sems.at[a * (N_DEV - 1) + k - 1], recv_sem=recv_sems.at[a * (N_DEV - 1) + k - 1],
                    device_id=dev, device_id_type=MESH)
                cp.start()
                sends.append(cp)
        for a, (s, d) in enumerate(zip(src_refs, dst_refs)):
            for k in range(1, N_DEV):
                dev, idx = _peer(k)
                pltpu.make_async_remote_copy(
                    src_ref=d.at[idx], dst_ref=d.at[idx],
                    send_sem=send_sems.at[a * (N_DEV - 1) + k - 1], recv_sem=recv_sems.at[a * (N_DEV - 1) + k - 1],
                    device_id=dev, device_id_type=MESH).wait_recv()
        for cp in sends:
            cp.wait_send()
        for cp in own:
            cp.wait()

    shape = lambda s: ((N_DEV,) + s.shape) if bcast else s.shape
    return pl.pallas_call(
        body, name=name,
        out_shape=[jax.ShapeDtypeStruct(shape(s), s.dtype) for s in srcs],
        in_specs=[pl.BlockSpec(memory_space=pl.ANY)] * n,
        out_specs=[pl.BlockSpec(memory_space=pl.ANY)] * n,
        scratch_shapes=[pltpu.SemaphoreType.DMA((n * (N_DEV - 1),)), pltpu.SemaphoreType.DMA((n * (N_DEV - 1),)),
                        pltpu.SemaphoreType.DMA((n,))],
    )(*srcs)


def allsum_small(vec, name):
    def body(src_ref, out_ref, dst_ref, send_sems, recv_sems, local_sem):
        _exchange_body(src_ref, dst_ref, send_sems, recv_sems, local_sem, True)
        acc = dst_ref[0]
        for j in range(1, N_DEV):
            acc = acc + dst_ref[j]
        out_ref[...] = acc

    return pl.pallas_call(
        body, name=name,
        out_shape=jax.ShapeDtypeStruct(vec.shape[1:], F32),
        in_specs=[pl.BlockSpec(memory_space=pltpu.VMEM)],
        out_specs=pl.BlockSpec(memory_space=pltpu.VMEM),
        scratch_shapes=[pltpu.VMEM((N_DEV,) + vec.shape[1:], F32),
                        pltpu.SemaphoreType.DMA((N_DEV - 1,)), pltpu.SemaphoreType.DMA((N_DEV - 1,)),
                        pltpu.SemaphoreType.DMA],
    )(vec)


def gather_small(vec, name):
    def body(src_ref, dst_ref, send_sems, recv_sems, local_sem):
        _exchange_body(src_ref, dst_ref, send_sems, recv_sems, local_sem, True)

    return pl.pallas_call(
        body, name=name,
        out_shape=jax.ShapeDtypeStruct((N_DEV,) + vec.shape[1:], F32),
        in_specs=[pl.BlockSpec(memory_space=pltpu.VMEM)],
        out_specs=pl.BlockSpec(memory_space=pltpu.VMEM),
        scratch_shapes=[pltpu.SemaphoreType.DMA((N_DEV - 1,)), pltpu.SemaphoreType.DMA((N_DEV - 1,)),
                        pltpu.SemaphoreType.DMA],
    )(vec)


def ffn_fwd(h, g, w_in8, w_out4, name):
    m, d = h.shape
    fb = w_in8.shape[2]
    tm = _row_tile(m)

    def body(h_ref, g_ref, wg_ref, wu_ref, wo_ref, hn_ref, n_ref, gate_ref, up_ref, acc_ref):
        i = pl.program_id(1)

        @pl.when(i == 0)
        def _():
            xh, _ = _rms(h_ref[...])
            n_ref[...] = (xh * g_ref[...]).astype(BF16)
            acc_ref[...] = jnp.zeros_like(acc_ref)

        n = n_ref[...]
        gate = _dot(n, wg_ref[0])
        up = _dot(n, wu_ref[0])
        gate_ref[0] = gate
        up_ref[0] = up
        a = (gate * jax.nn.sigmoid(gate) * up).astype(BF16)
        acc_ref[...] += _dot(a, wo_ref[0])

        @pl.when(i == 3)
        def _():
            hn_ref[...] = h_ref[...] + 0.5 * acc_ref[...]

    return pl.pallas_call(
        body, name=name, grid=(m // tm, 4),
        in_specs=[pl.BlockSpec((tm, d), lambda r, i: (r, 0)),
                  pl.BlockSpec((1, d), lambda r, i: (0, 0)),
                  pl.BlockSpec((1, d, fb), lambda r, i: (i, 0, 0)),
                  pl.BlockSpec((1, d, fb), lambda r, i: (i + 4, 0, 0)),
                  pl.BlockSpec((1, fb, d), lambda r, i: (i, 0, 0))],
        out_specs=[pl.BlockSpec((tm, d), lambda r, i: (r, 0)),
                   pl.BlockSpec((tm, d), lambda r, i: (r, 0)),
                   pl.BlockSpec((1, tm, fb), lambda r, i: (i, r, 0)),
                   pl.BlockSpec((1, tm, fb), lambda r, i: (i, r, 0))],
        out_shape=[jax.ShapeDtypeStruct((m, d), F32), jax.ShapeDtypeStruct((m, d), BF16),
                   jax.ShapeDtypeStruct((4, m, fb), F32), jax.ShapeDtypeStruct((4, m, fb), F32)],
        scratch_shapes=[pltpu.VMEM((tm, d), F32)],
        compiler_params=_params(2),
    )(h, g, w_in8, w_in8, w_out4)


def ffn_bwd(dh, h, g, gate, up, w_in8, w_out4, name):
    m, d = h.shape
    fb = w_in8.shape[2]
    tm = _row_tile(m)

    def body(dh_ref, h_ref, g_ref, gate_ref, up_ref, wg_ref, wu_ref, wo_ref,
             dhin_ref, a_ref, dg_ref, du_ref, dgn_ref, dhs_ref, acc_ref):
        r = pl.program_id(0)
        i = pl.program_id(1)

        @pl.when(i == 0)
        def _():
            dhs_ref[...] = (0.5 * dh_ref[...]).astype(BF16)
            acc_ref[...] = jnp.zeros_like(acc_ref)

        @pl.when((r == 0) & (i == 0))
        def _():
            dgn_ref[...] = jnp.zeros_like(dgn_ref)

        da = _dot_nt(dhs_ref[...], wo_ref[0])
        gt = gate_ref[0]
        u = up_ref[0]
        sg = jax.nn.sigmoid(gt)
        sl = gt * sg
        a_ref[0] = (sl * u).astype(BF16)
        dub = (da * sl).astype(BF16)
        dgb = (da * u * (sg * (1.0 + gt * (1.0 - sg)))).astype(BF16)
        dg_ref[0] = dgb
        du_ref[0] = dub
        acc_ref[...] += _dot_nt(dgb, wg_ref[0]) + _dot_nt(dub, wu_ref[0])

        @pl.when(i == 3)
        def _():
            dx, dgain = _rms_bwd(h_ref[...], g_ref[...], acc_ref[...])
            dgn_ref[...] += dgain
            dhin_ref[...] = dh_ref[...] + dx

    row = lambda r, i: (r, 0)
    blk = lambda r, i: (i, r, 0)
    return pl.pallas_call(
        body, name=name, grid=(m // tm, 4),
        in_specs=[pl.BlockSpec((tm, d), row), pl.BlockSpec((tm, d), row),
                  pl.BlockSpec((1, d), lambda r, i: (0, 0)),
                  pl.BlockSpec((1, tm, fb), blk), pl.BlockSpec((1, tm, fb), blk),
                  pl.BlockSpec((1, d, fb), lambda r, i: (i, 0, 0)),
                  pl.BlockSpec((1, d, fb), lambda r, i: (i + 4, 0, 0)),
                  pl.BlockSpec((1, fb, d), lambda r, i: (i, 0, 0))],
        out_specs=[pl.BlockSpec((tm, d), row),
                   pl.BlockSpec((1, tm, fb), blk), pl.BlockSpec((1, tm, fb), blk),
                   pl.BlockSpec((1, tm, fb), blk),
                   pl.BlockSpec((1, d), lambda r, i: (0, 0)),
                   pl.BlockSpec((tm, d), row)],
        out_shape=[jax.ShapeDtypeStruct((m, d), F32),
                   jax.ShapeDtypeStruct((4, m, fb), BF16), jax.ShapeDtypeStruct((4, m, fb), BF16),
                   jax.ShapeDtypeStruct((4, m, fb), BF16),
                   jax.ShapeDtypeStruct((1, d), F32), jax.ShapeDtypeStruct((m, d), BF16)],
        scratch_shapes=[pltpu.VMEM((tm, d), F32)],
        compiler_params=_params(2),
    )(dh, h, g, gate, up, w_in8, w_in8, w_out4)


def matmul_tn(x, y, name, tn=None, blocked=False, y2=None):
    bx, m, k = x.shape
    by, _, n = y.shape
    b = max(bx, by) * (2 if y2 is not None else 1)
    tm = _row_tile(m)
    tn = n if tn is None else tn
    nt = n // tn
    nr = m // tm

    def body(*refs):
        x_ref, y_ref = refs[0], refs[1]
        o_ref, acc_ref = refs[-2], refs[-1]
        r = pl.program_id(2)

        @pl.when(r == 0)
        def _():
            acc_ref[...] = jnp.zeros_like(acc_ref)

        if y2 is None:
            acc_ref[...] += _dot_tn(x_ref[0].astype(BF16), y_ref[0].astype(BF16))
        else:
            @pl.when(pl.program_id(0) < by)
            def _():
                acc_ref[...] += _dot_tn(x_ref[0].astype(BF16), y_ref[0].astype(BF16))

            @pl.when(pl.program_id(0) >= by)
            def _():
                acc_ref[...] += _dot_tn(x_ref[0].astype(BF16), refs[2][0].astype(BF16))

        @pl.when(r == nr - 1)
        def _():
            o_ref[0] = acc_ref[...].astype(BF16)

    x_map = (lambda i, j, r: (i, r, 0)) if bx > 1 else (lambda i, j, r: (0, r, 0))
    if y2 is None:
        y_specs = [pl.BlockSpec((1, tm, tn), (lambda i, j, r: (i, r, j)) if by > 1 else (lambda i, j, r: (0, r, j)))]
    else:
        y_specs = [pl.BlockSpec((1, tm, tn), lambda i, j, r: (jnp.minimum(i, by - 1), jnp.where(i < by, r, nr - 1), j)),
                   pl.BlockSpec((1, tm, tn), lambda i, j, r: (jnp.maximum(i - by, 0), jnp.where(i < by, 0, r), j))]
    if blocked:
        out_spec = pl.BlockSpec((1, k, tn), lambda i, j, r: (i * nt + j, 0, 0))
        out_shape = jax.ShapeDtypeStruct((b * nt, k, tn), BF16)
    else:
        out_spec = pl.BlockSpec((1, k, tn), lambda i, j, r: (i, 0, j))
        out_shape = jax.ShapeDtypeStruct((b, k, n), BF16)
    return pl.pallas_call(
        body, name=name, grid=(b, nt, nr),
        in_specs=[pl.BlockSpec((1, tm, k), x_map)] + y_specs,
        out_specs=out_spec, out_shape=out_shape,
        scratch_shapes=[pltpu.VMEM((k, tn), F32)],
        compiler_params=_params(3),
    )(*([x, y] + ([y2] if y2 is not None else [])))


class _Cols:
    def __init__(self, d):
        self.d = d
        self.ga, self.gb = 0, d
        self.qa, self.ka, self.va, self.qb = 2 * d, 2 * d + 512, 2 * d + 1024, 2 * d + 1536
        self.kb, self.vb, self.fa = 2 * d + 2048, 2 * d + 2176, 2 * d + 2304
        self.np = 2 * d + 2432


def mixer_proj(h, g, wp, name):
    m, d = h.shape
    npad = wp.shape[1]
    tm = _row_tile(m)

    def body(h_ref, g_ref, w_ref, n_ref, p_ref):
        xh, _ = _rms(h_ref[...])
        n = (xh * g_ref[...]).astype(BF16)
        n_ref[...] = n
        p_ref[...] = _dot(n, w_ref[...])

    return pl.pallas_call(
        body, name=name, grid=(m // tm,),
        in_specs=[pl.BlockSpec((tm, d), lambda r: (r, 0)), pl.BlockSpec((1, d), lambda r: (0, 0)),
                  pl.BlockSpec((d, npad), lambda r: (0, 0))],
        out_specs=[pl.BlockSpec((tm, d), lambda r: (r, 0)), pl.BlockSpec((tm, npad), lambda r: (r, 0))],
        out_shape=[jax.ShapeDtypeStruct((m, d), BF16), jax.ShapeDtypeStruct((m, npad), F32)],
        compiler_params=_params(1),
    )(h, g, wp)


def _head_norm(x, gain, ones):
    outs = []
    for b in range(x.shape[1] // LANES):
        xb = x[:, b * LANES:(b + 1) * LANES]
        r = lax.rsqrt(_group_mean(xb * xb, ones) + EPS)
        outs.append(xb * r * gain[:, b * LANES:(b + 1) * LANES])
    return outs


def _head_norm_bwd(x, gain, dn, ones):
    dxs, dgs = [], []
    for b in range(x.shape[1] // LANES):
        sl = slice(b * LANES, (b + 1) * LANES)
        xb, dnb = x[:, sl], dn[:, sl]
        r = lax.rsqrt(_group_mean(xb * xb, ones) + EPS)
        xh = xb * r
        dxh = dnb * gain[:, sl]
        dxs.append(r * (dxh - xh * _group_mean(dxh * xh, ones)))
        dgs.append(jnp.sum(dnb * xh, axis=0, keepdims=True))
    return dxs, dgs


def qk_post(proj, gains, fbias, cols, name):
    m = proj.shape[0]
    tm = _row_tile(m)
    gqa, gka, gqb, gkb = gains

    def body(qa_ref, ka_ref, va_ref, qb_ref, kb_ref, vb_ref, fa_ref, gqa_ref, gka_ref, gqb_ref, gkb_ref, fb_ref,
             qa_o, ka_o, va_o, qb_o, kb_o, vb_o, c_o, carry_ref):
        @pl.when(pl.program_id(0) == 0)
        def _():
            carry_ref[...] = jnp.zeros_like(carry_ref)

        ones = _group_ones()
        for src, gn, dst in ((qa_ref, gqa_ref, qa_o), (ka_ref, gka_ref, ka_o), (qb_ref, gqb_ref, qb_o),
                             (kb_ref, gkb_ref, kb_o)):
            for b, blk in enumerate(_head_norm(src[...], gn[...], ones)):
                dst[:, b * LANES:(b + 1) * LANES] = blk.astype(BF16)
        va_o[...] = va_ref[...].astype(BF16)
        vb_o[...] = vb_ref[...].astype(BF16)
        z = fa_ref[...] + fb_ref[...]
        logf = jnp.minimum(z, 0.0) - jnp.log(1.0 + jnp.exp(-jnp.abs(z)))
        rr = lax.broadcasted_iota(jnp.int32, (tm, tm), 0)
        cc = lax.broadcasted_iota(jnp.int32, (tm, tm), 1)
        tril = jnp.where(cc <= rr, 1.0, 0.0).astype(BF16)
        p0, p1, p2 = _split3(logf)
        c_o[...] = _dot(tril, p0) + _dot(tril, p1) + _dot(tril, p2) + carry_ref[...]
        carry_ref[...] += jnp.sum(logf, axis=0, keepdims=True)

    w512 = lambda off: pl.BlockSpec((tm, 512), lambda r, o=off // 512: (r, o))
    w128 = lambda off: pl.BlockSpec((tm, LANES), lambda r, o=off // LANES: (r, o))
    vec = lambda w: pl.BlockSpec((1, w), lambda r: (0, 0))
    row = lambda w: pl.BlockSpec((tm, w), lambda r: (r, 0))
    return pl.pallas_call(
        body, name=name, grid=(m // tm,),
        in_specs=[w512(cols.qa), w512(cols.ka), w512(cols.va), w512(cols.qb), w128(cols.kb), w128(cols.vb),
                  w128(cols.fa), vec(512), vec(512), vec(512), vec(LANES), vec(LANES)],
        out_specs=[row(512), row(512), row(512), row(512), row(LANES), row(LANES), row(LANES)],
        out_shape=[jax.ShapeDtypeStruct((m, 512), BF16)] * 4 + [jax.ShapeDtypeStruct((m, LANES), BF16)] * 2
                  + [jax.ShapeDtypeStruct((m, LANES), F32)],
        scratch_shapes=[pltpu.VMEM((1, LANES), F32)],
        compiler_params=_params(1),
    )(proj, proj, proj, proj, proj, proj, proj, gqa, gka, gqb, gkb, fbias)


def qk_post_bwd(proj, gains, fbias, dqa, dka, dva, dqb, dkb, dvb, dc, dga, dgb, cols, name):
    m = proj.shape[0]
    d = cols.d
    tm = _row_tile(m)
    nt = m // tm
    gqa, gka, gqb, gkb = gains

    def body(qa_ref, ka_ref, qb_ref, kb_ref, fa_ref, gqa_ref, gka_ref, gqb_ref, gkb_ref, fb_ref,
             dqa_ref, dka_ref, dva_ref, dqb_ref, dkb_ref, dvb_ref, dc_ref, dga_ref, dgb_ref,
             dp_o, ggqa_o, ggka_o, ggqb_o, ggkb_o, gfb_o, carry_ref):
        @pl.when(pl.program_id(0) == 0)
        def _():
            carry_ref[...] = jnp.zeros_like(carry_ref)
            for o in (ggqa_o, ggka_o, ggqb_o, ggkb_o, gfb_o):
                o[...] = jnp.zeros_like(o)

        ones = _group_ones()
        dp_o[:, cols.ga:cols.ga + d] = dga_ref[...].astype(BF16)
        dp_o[:, cols.gb:cols.gb + d] = dgb_ref[...].astype(BF16)
        dp_o[:, cols.va:cols.va + 512] = dva_ref[...].astype(BF16)
        dp_o[:, cols.vb:cols.vb + LANES] = dvb_ref[...].astype(BF16)
        for src, gn, dn, off, gout in ((qa_ref, gqa_ref, dqa_ref, cols.qa, ggqa_o),
                                       (ka_ref, gka_ref, dka_ref, cols.ka, ggka_o),
                                       (qb_ref, gqb_ref, dqb_ref, cols.qb, ggqb_o),
                                       (kb_ref, gkb_ref, dkb_ref, cols.kb, ggkb_o)):
            dxs, dgs = _head_norm_bwd(src[...], gn[...], dn[...], ones)
            for b, (dx, dg) in enumerate(zip(dxs, dgs)):
                dp_o[:, off + b * LANES:off + (b + 1) * LANES] = dx.astype(BF16)
                gout[:, b * LANES:(b + 1) * LANES] += dg
        dcv = dc_ref[...]
        rr = lax.broadcasted_iota(jnp.int32, (tm, tm), 0)
        cc = lax.broadcasted_iota(jnp.int32, (tm, tm), 1)
        triu = jnp.where(cc >= rr, 1.0, 0.0).astype(BF16)
        p0, p1, p2 = _split3(dcv)
        dlogf = _dot(triu, p0) + _dot(triu, p1) + _dot(triu, p2) + carry_ref[...]
        carry_ref[...] += jnp.sum(dcv, axis=0, keepdims=True)
        z = fa_ref[...] + fb_ref[...]
        row = (nt - 1 - pl.program_id(0)) * tm + lax.broadcasted_iota(jnp.int32, (tm, LANES), 0)
        dfa = jnp.where(row >= PAD_FRONT, dlogf * jax.nn.sigmoid(-z), 0.0)
        dp_o[:, cols.fa:cols.fa + LANES] = dfa.astype(BF16)
        gfb_o[...] += jnp.sum(dfa, axis=0, keepdims=True)

    rev = lambda r: nt - 1 - r
    w512 = lambda off: pl.BlockSpec((tm, 512), lambda r, o=off // 512: (rev(r), o))
    w128 = lambda off: pl.BlockSpec((tm, LANES), lambda r, o=off // LANES: (rev(r), o))
    vec = lambda w: pl.BlockSpec((1, w), lambda r: (0, 0))
    row = lambda w: pl.BlockSpec((tm, w), lambda r: (rev(r), 0))
    return pl.pallas_call(
        body, name=name, grid=(nt,),
        in_specs=[w512(cols.qa), w512(cols.ka), w512(cols.qb), w128(cols.kb), w128(cols.fa),
                  vec(512), vec(512), vec(512), vec(LANES), vec(LANES),
                  row(512), row(512), row(512), row(512), row(LANES), row(LANES), row(LANES), row(d), row(d)],
        out_specs=[row(cols.np), vec(512), vec(512), vec(512), vec(LANES), vec(LANES)],
        out_shape=[jax.ShapeDtypeStruct((m, cols.np), BF16)] + [jax.ShapeDtypeStruct((1, 512), F32)] * 3
                  + [jax.ShapeDtypeStruct((1, LANES), F32)] * 2,
        scratch_shapes=[pltpu.VMEM((1, LANES), F32)],
        compiler_params=_params(1),
    )(proj, proj, proj, proj, proj, gqa, gka, gqb, gkb, fbias, dqa, dka, dva, dqb, dkb, dvb, dc, dga, dgb)


def dproj_bwd(dh, h, g, dproj, wp, name):
    m, d = h.shape
    npad = wp.shape[1]
    tm = _row_tile(m)

    def body(dh_ref, h_ref, g_ref, dp_ref, w_ref, dhin_ref, dgn_ref):
        @pl.when(pl.program_id(0) == 0)
        def _():
            dgn_ref[...] = jnp.zeros_like(dgn_ref)

        dn = _dot_nt(dp_ref[...], w_ref[...])
        dx, dgain = _rms_bwd(h_ref[...], g_ref[...], dn)
        dgn_ref[...] += dgain
        dhin_ref[...] = dh_ref[...] + dx

    row = lambda w: pl.BlockSpec((tm, w), lambda r: (r, 0))
    return pl.pallas_call(
        body, name=name, grid=(m // tm,),
        in_specs=[row(d), row(d), pl.BlockSpec((1, d), lambda r: (0, 0)), row(npad),
                  pl.BlockSpec((d, npad), lambda r: (0, 0))],
        out_specs=[row(d), pl.BlockSpec((1, d), lambda r: (0, 0))],
        out_shape=[jax.ShapeDtypeStruct((m, d), F32), jax.ShapeDtypeStruct((1, d), F32)],
        compiler_params=_params(1),
    )(dh, h, g, dproj, wp)


def _lane_col(v, lane_iota, idx):
    return jnp.sum(jnp.where(lane_iota == idx, v, 0.0), axis=1, keepdims=True)


def fox_fwd(q, k, v, c, c_row, name):
    m = q.shape[0]
    t = _row_tile(m)
    nq = m // t

    def body(q_ref, k_ref, v_ref, c_ref, crow_ref, o_ref, lse_ref, acc_ref, m_ref, l_ref):
        pair = pl.program_id(0)
        qi = pl.program_id(1)
        lane1 = lax.broadcasted_iota(jnp.int32, (1, LANES), 1)
        lane_t = lax.broadcasted_iota(jnp.int32, (t, LANES), 1)
        in_head = [lane1 < HEAD_DIM, lane1 >= HEAD_DIM]
        qt = q_ref[...]
        qs = [jnp.where(in_head[e], qt, jnp.zeros_like(qt)) for e in (0, 1)]
        cfull = c_ref[...]
        ccol = [_lane_col(cfull, lane_t, 2 * pair + e) for e in (0, 1)]
        row_pos = qi * t + lax.broadcasted_iota(jnp.int32, (t, t), 0)
        col_loc = lax.broadcasted_iota(jnp.int32, (t, t), 1)
        acc_ref[...] = jnp.zeros_like(acc_ref)
        m_ref[...] = jnp.full_like(m_ref, NEG)
        l_ref[...] = jnp.zeros_like(l_ref)

        def step(ki, carry):
            off = pl.multiple_of(ki * t, t)
            kt = k_ref[pl.ds(off, t), :]
            vt = v_ref[pl.ds(off, t), :]
            col_pos = ki * t + col_loc
            valid = (col_pos <= row_pos) & (col_pos >= PAD_FRONT)
            for e in (0, 1):
                s = _dot_nt(qs[e], kt) + ccol[e] - crow_ref[e, :, pl.ds(off, t)]
                s = jnp.where(valid, s, NEG)
                m_old = m_ref[e]
                m_new = jnp.maximum(m_old, jnp.max(s, axis=1, keepdims=True))
                alpha = jnp.exp(m_old - m_new)
                p = jnp.exp(s - m_new)
                l_ref[e] = alpha * l_ref[e] + jnp.sum(p, axis=1, keepdims=True)
                m_ref[e] = m_new
                ve = jnp.where(in_head[e], vt, jnp.zeros_like(vt))
                acc_ref[...] = acc_ref[...] * jnp.where(in_head[e], alpha, 1.0) + _dot(p.astype(BF16), ve)
            return carry

        lax.fori_loop(0, qi + 1, step, 0)
        o_ref[...] = acc_ref[...] * jnp.where(in_head[0], 1.0 / l_ref[0], 1.0 / l_ref[1])
        lse_ref[0] = jnp.where(in_head[0], m_ref[0] + jnp.log(l_ref[0]), m_ref[1] + jnp.log(l_ref[1]))

    return pl.pallas_call(
        body, name=name, grid=(4, nq),
        in_specs=[pl.BlockSpec((t, LANES), lambda p, i: (i, p)),
                  pl.BlockSpec((m, LANES), lambda p, i: (0, p)),
                  pl.BlockSpec((m, LANES), lambda p, i: (0, p)),
                  pl.BlockSpec((t, LANES), lambda p, i: (i, 0)),
                  pl.BlockSpec((2, 1, m), lambda p, i: (p, 0, 0))],
        out_specs=[pl.BlockSpec((t, LANES), lambda p, i: (i, p)),
                   pl.BlockSpec((1, t, LANES), lambda p, i: (p, i, 0))],
        out_shape=[jax.ShapeDtypeStruct((m, 512), F32), jax.ShapeDtypeStruct((4, m, LANES), F32)],
        scratch_shapes=[pltpu.VMEM((t, LANES), F32), pltpu.VMEM((2, t, 1), F32), pltpu.VMEM((2, t, 1), F32)],
        compiler_params=_params(2),
    )(q, k, v, c, c_row)


def fox_bwd(q, k, v, c, c_row, o, lse, do, name):
    m = q.shape[0]
    t = _row_tile(m)
    nq = m // t

    def body(k_ref, v_ref, crow_ref, q_ref, do_ref, o_ref, lse_ref, c_ref,
             dk_ref, dv_ref, dq_ref, dcr_ref, dcq_ref, dka_ref, dva_ref, dca_ref):
        pair = pl.program_id(0)
        ki = pl.program_id(1)

        @pl.when(ki == 0)
        def _():
            dq_ref[...] = jnp.zeros_like(dq_ref)
            dcq_ref[...] = jnp.zeros_like(dcq_ref)

        lane1 = lax.broadcasted_iota(jnp.int32, (1, LANES), 1)
        lane_t = lax.broadcasted_iota(jnp.int32, (t, LANES), 1)
        in_head = [lane1 < HEAD_DIM, lane1 >= HEAD_DIM]
        kt = k_ref[...]
        vt = v_ref[...]
        ks = [jnp.where(in_head[e], kt, jnp.zeros_like(kt)) for e in (0, 1)]
        crow = [crow_ref[e] for e in (0, 1)]
        col_pos = ki * t + lax.broadcasted_iota(jnp.int32, (t, t), 1)
        row_loc = lax.broadcasted_iota(jnp.int32, (t, t), 0)
        dka_ref[...] = jnp.zeros_like(dka_ref)
        dva_ref[...] = jnp.zeros_like(dva_ref)
        dca_ref[...] = jnp.zeros_like(dca_ref)

        def step(qi, carry):
            off = pl.multiple_of(qi * t, t)
            qt = q_ref[pl.ds(off, t), :]
            dot = do_ref[pl.ds(off, t), :]
            prod = dot * o_ref[pl.ds(off, t), :]
            lset = lse_ref[0, pl.ds(off, t), :]
            ct = c_ref[pl.ds(off, t), :]
            valid = (col_pos <= off + row_loc) & (col_pos >= PAD_FRONT)
            row_sums = []
            for e in (0, 1):
                qe = jnp.where(in_head[e], qt, jnp.zeros_like(qt))
                doe = jnp.where(in_head[e], dot, 0.0).astype(BF16)
                delta = jnp.sum(jnp.where(in_head[e], prod, 0.0), axis=1, keepdims=True)
                lse_e = _lane_col(lset, lane_t, HEAD_DIM * e)
                ccol = _lane_col(ct, lane_t, 2 * pair + e)
                s = _dot_nt(qe, kt) + ccol - crow[e]
                s = jnp.where(valid, s, NEG)
                p = jnp.exp(s - lse_e)
                dp = _dot_nt(doe, vt)
                ds = p * (dp - delta)
                dsb = ds.astype(BF16)
                dva_ref[...] += _dot_tn(p.astype(BF16), doe)
                dka_ref[...] += _dot_tn(dsb, qe)
                dq_ref[pl.ds(off, t), :] += _dot(dsb, ks[e])
                dca_ref[e] -= jnp.sum(ds, axis=0, keepdims=True)
                row_sums.append(jnp.sum(ds, axis=1, keepdims=True))
            dcq_ref[0, pl.ds(off, t), :] += jnp.where(in_head[0], row_sums[0], row_sums[1])
            return carry

        lax.fori_loop(ki, nq, step, 0)
        dk_ref[...] = dka_ref[...]
        dv_ref[...] = dva_ref[...]
        dcr_ref[...] = dca_ref[...]

    tile = pl.BlockSpec((t, LANES), lambda p, i: (i, p))
    full = pl.BlockSpec((m, LANES), lambda p, i: (0, p))
    dk, dv, dq, dc_row, dc_col = pl.pallas_call(
        body, name=name, grid=(4, nq),
        in_specs=[tile, tile, pl.BlockSpec((2, 1, t), lambda p, i: (p, 0, i)),
                  full, full, full, pl.BlockSpec((1, m, LANES), lambda p, i: (p, 0, 0)),
                  pl.BlockSpec((m, LANES), lambda p, i: (0, 0))],
        out_specs=[tile, tile, full, pl.BlockSpec((2, 1, t), lambda p, i: (p, 0, i)),
                   pl.BlockSpec((1, m, LANES), lambda p, i: (p, 0, 0))],
        out_shape=[jax.ShapeDtypeStruct((m, 512), F32), jax.ShapeDtypeStruct((m, 512), F32),
                   jax.ShapeDtypeStruct((m, 512), F32), jax.ShapeDtypeStruct((8, 1, m), F32),
                   jax.ShapeDtypeStruct((4, m, LANES), F32)],
        scratch_shapes=[pltpu.VMEM((t, LANES), F32), pltpu.VMEM((t, LANES), F32), pltpu.VMEM((2, 1, t), F32)],
        compiler_params=_params(2),
    )(k, v, c_row, q, do, o, lse, c)
    dc_query = jnp.stack([dc_col[:, :, 0], dc_col[:, :, HEAD_DIM]], axis=1).reshape(8, m)
    return dq, dk, dv, dc_row.reshape(8, m) + dc_query


def _bucket_ids():
    def bucket(dist):
        n = np.maximum(dist, 0)
        max_exact = N_BUCKETS // 2
        nf = np.maximum(n, 1).astype(np.float32)
        large = max_exact + (np.log(nf / max_exact) / math.log(MAX_DISTANCE / max_exact)
                             * (N_BUCKETS - max_exact)).astype(np.int32)
        return np.where(n < max_exact, n, np.minimum(large, N_BUCKETS - 1))

    tl = np.arange(LANES)[:, None]
    sl = np.arange(LANES)[None, :]
    prev = bucket(LANES + tl - sl)
    cur = bucket(tl - sl)
    meta = np.full((LANES, LANES), N_BUCKETS - 1)
    return np.concatenate([prev, cur, meta], axis=1).astype(np.int32)


def bias_build(table, name):
    ids = jnp.asarray(_bucket_ids())

    def body(t_ref, id_ref, o_ref):
        idv = id_ref[...]
        for h in range(8):
            acc = jnp.zeros((LANES, 3 * LANES), F32)
            for b in range(N_BUCKETS):
                acc = jnp.where(idv == b, t_ref[b, h], acc)
            o_ref[h] = acc

    return pl.pallas_call(
        body, name=name,
        in_specs=[pl.BlockSpec(memory_space=pltpu.SMEM), pl.BlockSpec(memory_space=pltpu.VMEM)],
        out_specs=pl.BlockSpec(memory_space=pltpu.VMEM),
        out_shape=jax.ShapeDtypeStruct((8, LANES, 3 * LANES), F32),
    )(table, ids)


def bias_reduce(dbias, name):
    ids = jnp.asarray(_bucket_ids())

    def body(d_ref, id_ref, o_ref):
        idv = id_ref[...]
        rr = lax.broadcasted_iota(jnp.int32, (N_BUCKETS, LANES), 0)
        cc = lax.broadcasted_iota(jnp.int32, (N_BUCKETS, LANES), 1)
        acc = jnp.zeros((N_BUCKETS, LANES), F32)
        for h in range(8):
            dv = d_ref[h]
            for b in range(N_BUCKETS):
                val = jnp.sum(jnp.where(idv == b, dv, 0.0), keepdims=True)
                acc = jnp.where((rr == b) & (cc == h), val, acc)
        o_ref[...] = acc

    return pl.pallas_call(
        body, name=name,
        in_specs=[pl.BlockSpec(memory_space=pltpu.VMEM), pl.BlockSpec(memory_space=pltpu.VMEM)],
        out_specs=pl.BlockSpec(memory_space=pltpu.VMEM),
        out_shape=jax.ShapeDtypeStruct((N_BUCKETS, LANES), F32),
    )(dbias, ids)


def _swa_valid(n):
    shape = (LANES, 3 * LANES)
    tl = lax.broadcasted_iota(jnp.int32, shape, 0)
    col = lax.broadcasted_iota(jnp.int32, shape, 1)
    sl = col & (LANES - 1)
    nv = jnp.full(shape, n, jnp.int32)
    is_meta = sl >= PAD_FRONT
    prev = (col < LANES) & (sl > tl) & (nv >= 1) & ((nv >= 2) | is_meta)
    cur = (col >= LANES) & (col < 2 * LANES) & (sl <= tl) & ((nv >= 1) | is_meta)
    meta = (col >= 2 * LANES) & is_meta & ((nv >= 2) | ((nv == 1) & (sl <= tl)))
    return prev | cur | meta


def _swa_keys(ref, n):
    off_prev = pl.multiple_of(jnp.maximum(n - 1, 0) * LANES, LANES)
    off_cur = pl.multiple_of(n * LANES, LANES)
    return jnp.concatenate([ref[pl.ds(off_prev, LANES), :], ref[pl.ds(off_cur, LANES), :], ref[0:LANES, :]], axis=0)


def swa_fwd(q, k, v, bias, sinks, name):
    m = q.shape[0]

    def body(q_ref, k_ref, v_ref, bias_ref, sink_ref, o_ref, lse_ref):
        n = pl.program_id(0)
        lane1 = lax.broadcasted_iota(jnp.int32, (1, LANES), 1)
        lane_t = lax.broadcasted_iota(jnp.int32, (LANES, LANES), 1)
        in_head = [lane1 < HEAD_DIM, lane1 >= HEAD_DIM]
        kall = _swa_keys(k_ref, n)
        vall = _swa_keys(v_ref, n)
        vs = [jnp.where(in_head[g], vall, jnp.zeros_like(vall)) for g in (0, 1)]
        valid = _swa_valid(n)
        lse = jnp.zeros((LANES, LANES), F32)
        for b in range(4):
            qb = q_ref[:, b * LANES:(b + 1) * LANES]
            ob = jnp.zeros((LANES, LANES), F32)
            for g in (0, 1):
                h = 4 * g + b
                qe = jnp.where(in_head[g], qb, jnp.zeros_like(qb))
                s = jnp.where(valid, _dot_nt(qe, kall) + bias_ref[h], NEG)
                sink = sink_ref[h]
                mx = jnp.maximum(jnp.max(s, axis=1, keepdims=True), sink)
                p = jnp.exp(s - mx)
                den = jnp.sum(p, axis=1, keepdims=True) + jnp.exp(sink - mx)
                ob = ob + _dot((p / den).astype(BF16), vs[g])
                lse = jnp.where(lane_t == h, mx + jnp.log(den), lse)
            o_ref[:, b * LANES:(b + 1) * LANES] = ob
        lse_ref[...] = lse

    return pl.pallas_call(
        body, name=name, grid=(m // LANES,),
        in_specs=[pl.BlockSpec((LANES, 512), lambda n: (n, 0)),
                  pl.BlockSpec((m, LANES), lambda n: (0, 0)), pl.BlockSpec((m, LANES), lambda n: (0, 0)),
                  pl.BlockSpec((8, LANES, 3 * LANES), lambda n: (0, 0, 0)),
                  pl.BlockSpec(memory_space=pltpu.SMEM)],
        out_specs=[pl.BlockSpec((LANES, 512), lambda n: (n, 0)), pl.BlockSpec((LANES, LANES), lambda n: (n, 0))],
        out_shape=[jax.ShapeDtypeStruct((m, 512), F32), jax.ShapeDtypeStruct((m, LANES), F32)],
        compiler_params=_params(1),
    )(q, k, v, bias, sinks)


def swa_bwd(q, k, v, bias, sinks, o, lse, do, name):
    m = q.shape[0]

    def body(q_ref, do_ref, o_ref, lse_ref, k_ref, v_ref, bias_ref, sink_ref,
             dq_ref, dk_ref, dv_ref, dbias_ref, dsink_ref):
        n = pl.program_id(0)

        @pl.when(n == 0)
        def _():
            for r in (dk_ref, dv_ref, dbias_ref, dsink_ref):
                r[...] = jnp.zeros_like(r)

        lane1 = lax.broadcasted_iota(jnp.int32, (1, LANES), 1)
        lane_t = lax.broadcasted_iota(jnp.int32, (LANES, LANES), 1)
        in_head = [lane1 < HEAD_DIM, lane1 >= HEAD_DIM]
        off_prev = pl.multiple_of(jnp.maximum(n - 1, 0) * LANES, LANES)
        off_cur = pl.multiple_of(n * LANES, LANES)
        kall = _swa_keys(k_ref, n)
        vall = _swa_keys(v_ref, n)
        ks = [jnp.where(in_head[g], kall, jnp.zeros_like(kall)) for g in (0, 1)]
        valid = _swa_valid(n)
        lsev = lse_ref[...]
        dsink = dsink_ref[...]
        dkall = jnp.zeros((3 * LANES, LANES), F32)
        dvall = jnp.zeros((3 * LANES, LANES), F32)
        for b in range(4):
            sl = slice(b * LANES, (b + 1) * LANES)
            qb = q_ref[:, sl]
            dob = do_ref[:, sl]
            prod = dob * o_ref[:, sl]
            dqb = jnp.zeros((LANES, LANES), F32)
            for g in (0, 1):
                h = 4 * g + b
                qe = jnp.where(in_head[g], qb, jnp.zeros_like(qb))
                doe = jnp.where(in_head[g], dob, 0.0).astype(BF16)
                delta = jnp.sum(jnp.where(in_head[g], prod, 0.0), axis=1, keepdims=True)
                lse_h = _lane_col(lsev, lane_t, h)
                s = jnp.where(valid, _dot_nt(qe, kall) + bias_ref[h], NEG)
                p = jnp.exp(s - lse_h)
                ds = p * (_dot_nt(doe, vall) - delta)
                dbias_ref[h] += ds
                sink_part = jnp.sum(-jnp.exp(sink_ref[h] - lse_h) * delta, keepdims=True)
                dsink = jnp.where(lane1 == h, dsink + sink_part, dsink)
                dsb = ds.astype(BF16)
                dqb = dqb + _dot(dsb, ks[g])
                dkall = dkall + _dot_tn(dsb, qe)
                dvall = dvall + _dot_tn(p.astype(BF16), doe)
            dq_ref[:, sl] = dqb
        dsink_ref[...] = dsink
        for ref, val in ((dk_ref, dkall), (dv_ref, dvall)):
            ref[pl.ds(off_prev, LANES), :] += val[0:LANES]
            ref[pl.ds(off_cur, LANES), :] += val[LANES:2 * LANES]
            ref[0:LANES, :] += val[2 * LANES:3 * LANES]

    blk = pl.BlockSpec((LANES, 512), lambda n: (n, 0))
    full = pl.BlockSpec((m, LANES), lambda n: (0, 0))
    return pl.pallas_call(
        body, name=name, grid=(m // LANES,),
        in_specs=[blk, blk, blk, pl.BlockSpec((LANES, LANES), lambda n: (n, 0)), full, full,
                  pl.BlockSpec((8, LANES, 3 * LANES), lambda n: (0, 0, 0)),
                  pl.BlockSpec(memory_space=pltpu.SMEM)],
        out_specs=[blk, full, full, pl.BlockSpec((8, LANES, 3 * LANES), lambda n: (0, 0, 0)),
                   pl.BlockSpec((1, LANES), lambda n: (0, 0))],
        out_shape=[jax.ShapeDtypeStruct((m, 512), F32), jax.ShapeDtypeStruct((m, LANES), F32),
                   jax.ShapeDtypeStruct((m, LANES), F32), jax.ShapeDtypeStruct((8, LANES, 3 * LANES), F32),
                   jax.ShapeDtypeStruct((1, LANES), F32)],
        compiler_params=_params(1),
    )(q, do, o, lse, k, v, bias, sinks)


def branch_out(h, o_fox, o_swa, proj, wbf, wbs, wo, cols, name):
    m, d = h.shape
    tm = _row_tile(m)

    def body(h_ref, of_ref, os_ref, ga_ref, gb_ref, wbf_ref, wbs_ref, wo_ref, hn_ref):
        tf = _dot(of_ref[...].astype(BF16), wbf_ref[...])
        ts = _dot(os_ref[...].astype(BF16), wbs_ref[...])
        y = jax.nn.sigmoid(ga_ref[...]) * tf + jax.nn.sigmoid(gb_ref[...]) * ts
        hn_ref[...] = h_ref[...] + _dot(y.astype(BF16), wo_ref[...])

    row = lambda w, o=0: pl.BlockSpec((tm, w), lambda r, o=o: (r, o))
    res = lambda a: pl.BlockSpec(a.shape, lambda r: (0, 0))
    return pl.pallas_call(
        body, name=name, grid=(m // tm,),
        in_specs=[row(d), row(512), row(512), row(d, cols.ga // d), row(d, cols.gb // d), res(wbf), res(wbs), res(wo)],
        out_specs=row(d),
        out_shape=jax.ShapeDtypeStruct((m, d), F32),
        compiler_params=_params(1),
    )(h, o_fox, o_swa, proj, proj, wbf, wbs, wo)


def branch_out_bwd(dh, o_fox, o_swa, proj, wbf, wbs, wo, cols, name):
    m, d = dh.shape
    tm = _row_tile(m)

    def body(dh_ref, of_ref, os_ref, ga_ref, gb_ref, wbf_ref, wbs_ref, wo_ref,
             y_ref, dtf_ref, dts_ref, dga_ref, dgb_ref, dof_ref, dos_ref):
        dy = _dot_nt(dh_ref[...].astype(BF16), wo_ref[...])
        tf = _dot(of_ref[...].astype(BF16), wbf_ref[...])
        ts = _dot(os_ref[...].astype(BF16), wbs_ref[...])
        sa = jax.nn.sigmoid(ga_ref[...])
        sb = jax.nn.sigmoid(gb_ref[...])
        y_ref[...] = (sa * tf + sb * ts).astype(BF16)
        dtf = (dy * sa).astype(BF16)
        dts = (dy * sb).astype(BF16)
        dtf_ref[...] = dtf
        dts_ref[...] = dts
        dga_ref[...] = (dy * tf * sa * (1.0 - sa)).astype(BF16)
        dgb_ref[...] = (dy * ts * sb * (1.0 - sb)).astype(BF16)
        dof_ref[...] = _dot_nt(dtf, wbf_ref[...])
        dos_ref[...] = _dot_nt(dts, wbs_ref[...])

    row = lambda w, o=0: pl.BlockSpec((tm, w), lambda r, o=o: (r, o))
    res = lambda a: pl.BlockSpec(a.shape, lambda r: (0, 0))
    return pl.pallas_call(
        body, name=name, grid=(m // tm,),
        in_specs=[row(d), row(512), row(512), row(d, cols.ga // d), row(d, cols.gb // d), res(wbf), res(wbs), res(wo)],
        out_specs=[row(d)] * 5 + [row(512)] * 2,
        out_shape=[jax.ShapeDtypeStruct((m, d), BF16)] * 5 + [jax.ShapeDtypeStruct((m, 512), F32)] * 2,
        compiler_params=_params(1),
    )(dh, o_fox, o_swa, proj, proj, wbf, wbs, wo)


def loss_head(h, target, name):
    m, d = h.shape

    def body(h_ref, t_ref, dh_ref, loss_ref):
        n = pl.program_id(0)

        @pl.when(n == 0)
        def _():
            loss_ref[...] = jnp.zeros_like(loss_ref)
            dh_ref[...] = jnp.zeros_like(dh_ref)

        @pl.when(n > 0)
        def _():
            err = h_ref[...] - t_ref[...]
            dh_ref[...] = err * (1.0 / d)
            loss_ref[...] += jnp.sum(err * err, keepdims=True) * (0.5 / d)

    return pl.pallas_call(
        body, name=name, grid=(m // LANES,),
        in_specs=[pl.BlockSpec((LANES, d), lambda n: (n, 0)),
                  pl.BlockSpec((LANES, d), lambda n: (jnp.maximum(n - 1, 0), 0))],
        out_specs=[pl.BlockSpec((LANES, d), lambda n: (n, 0)), pl.BlockSpec((8, LANES), lambda n: (0, 0))],
        out_shape=[jax.ShapeDtypeStruct((m, d), F32), jax.ShapeDtypeStruct((8, LANES), F32)],
        compiler_params=_params(1),
    )(h, target)


def _adamw_math(w, g, m, v):
    m = ADAM_B1 * m + (1.0 - ADAM_B1) * g
    v = ADAM_B2 * v + (1.0 - ADAM_B2) * (g * g)
    m_hat = m / (1.0 - ADAM_B1 ** ADAM_STEP)
    v_hat = v / (1.0 - ADAM_B2 ** ADAM_STEP)
    delta = -ADAM_LR * (m_hat / (jnp.sqrt(v_hat) + ADAM_EPS) + ADAM_WD * w)
    return delta, m, v


def adamw_sum(parts, w, m, v, name):
    n_layers, a, b = w.shape
    ta = next(t for t in (256, 176, 128, a) if a % t == 0)
    nr = a // ta

    def body(*refs):
        p_refs = refs[:n_layers]
        w_ref, m_ref, v_ref, g_o, d_o, m_o, v_o = refs[n_layers:]
        for l in range(n_layers):
            @pl.when(pl.program_id(0) == l)
            def _(l=l):
                g = p_refs[l][0].astype(F32)
                for j in range(1, N_DEV):
                    g = g + p_refs[l][j].astype(F32)
                g_o[0] = g
                d_o[0], m_o[0], v_o[0] = _adamw_math(w_ref[0], g, m_ref[0], v_ref[0])

    def part_spec(l):
        return pl.BlockSpec((N_DEV, ta, b), lambda i, r, l=l: (0, jnp.where(i == l, r, jnp.where(i < l, 0, nr - 1)), 0))

    row = pl.BlockSpec((1, ta, b), lambda i, r: (i, r, 0))
    return pl.pallas_call(
        body, name=name, grid=(n_layers, nr),
        in_specs=[part_spec(l) for l in range(n_layers)] + [row, row, row],
        out_specs=[row] * 4,
        out_shape=[jax.ShapeDtypeStruct(w.shape, F32)] * 4,
        compiler_params=_params(2),
    )(*parts, w, m, v)


def adamw_small(g, w, m, v, name):
    def body(g_ref, w_ref, m_ref, v_ref, d_o, m_o, v_o):
        d_o[...], m_o[...], v_o[...] = _adamw_math(w_ref[...], g_ref[...], m_ref[...], v_ref[...])

    spec = pl.BlockSpec(memory_space=pltpu.VMEM)
    return pl.pallas_call(
        body, name=name, in_specs=[spec] * 4, out_specs=[spec] * 3,
        out_shape=[jax.ShapeDtypeStruct(w.shape, F32)] * 3,
    )(g, w, m, v)


BIG = ("ffn1_w_in", "ffn1_w_out", "w_in", "w_branch_fox", "w_branch_swa", "w_out", "ffn2_w_in", "ffn2_w_out")
SMALL = ("rel_bias_table", "ffn1_norm", "mix_norm", "forget_bias", "fox_q_norm", "fox_k_norm",
         "swa_q_norm", "swa_k_norm", "swa_sinks", "ffn2_norm")
WEIGHTS = ("meta_tokens", "rel_bias_table", "ffn1_norm", "ffn1_w_in", "ffn1_w_out", "mix_norm", "w_in",
           "forget_bias", "fox_q_norm", "fox_k_norm", "swa_q_norm", "swa_k_norm", "swa_sinks", "w_branch_fox",
           "w_branch_swa", "w_out", "ffn2_norm", "ffn2_w_in", "ffn2_w_out")


def _pack(arrs, width, row_multiple, dtype):
    lead = arrs[0].shape[:-1]
    flat = jnp.concatenate([a.astype(dtype) for a in arrs], axis=-1)
    n = flat.shape[-1]
    rows = -(-n // width)
    rows = -(-rows // row_multiple) * row_multiple
    flat = jnp.pad(flat, [(0, 0)] * len(lead) + [(0, rows * width - n)])
    return flat.reshape(lead + (rows, width))


def _unpack(flat, shapes):
    flat = flat.reshape(-1)
    out, off = [], 0
    for s in shapes:
        n = int(np.prod(s))
        out.append(flat[off:off + n].reshape(s))
        off += n
    return out


def _swa_head_order():
    return [4 * (j % 2) + j // 2 for j in range(8)]


def _permute_heads(a, axis, inverse=False):
    order = _swa_head_order()
    if inverse:
        order = [order.index(hd) for hd in range(8)]
    parts = [lax.slice_in_dim(a, hd * HEAD_DIM, (hd + 1) * HEAD_DIM, axis=axis) for hd in order]
    return jnp.concatenate(parts, axis=axis)


def _w_in_segments(cols):
    d = cols.d
    segs = [(0, 512, cols.qa), (512, 512, cols.ka), (1024, 512, cols.va), (1536, 8, cols.fa)]
    order = _swa_head_order()
    segs += [(1544 + HEAD_DIM * hd, HEAD_DIM, cols.qb + HEAD_DIM * order.index(hd)) for hd in range(8)]
    segs += [(2056, 128, cols.kb), (2184, 128, cols.vb), (2312, d, cols.ga), (2312 + d, d, cols.gb)]
    return segs


def _reorder_w_in(blocks, cols):
    width = blocks[0].shape[1]
    parts = []
    for old, length, _ in sorted(_w_in_segments(cols), key=lambda s: s[2]):
        while length:
            j, off = divmod(old, width)
            take = min(length, width - off)
            parts.append(blocks[j][:, off:off + take])
            old, length = old + take, length - take
    parts.append(jnp.zeros((blocks[0].shape[0], LANES - 8), blocks[0].dtype))
    return jnp.concatenate(parts, axis=1)


def _restore_w_in(wp, cols, width):
    segs = sorted(_w_in_segments(cols))
    blocks = []
    for j in range(N_DEV):
        lo, hi = j * width, (j + 1) * width
        parts = []
        for old, length, new in segs:
            a, b = max(old, lo), min(old + length, hi)
            if a < b:
                parts.append(wp[:, new + a - old:new + b - old])
        blocks.append(jnp.concatenate(parts, axis=1))
    return jnp.stack(blocks)


def _lane_pad(v):
    return jnp.pad(v, ((0, 0), (0, LANES - v.shape[1])))


def kernel(x, meta_tokens, rel_bias_table, ffn1_norm, ffn1_w_in, ffn1_w_out, mix_norm, w_in, forget_bias, fox_q_norm, fox_k_norm, swa_q_norm, swa_k_norm, swa_sinks, w_branch_fox, w_branch_swa, w_out, ffn2_norm, ffn2_w_in, ffn2_w_out, loss_target, m_meta_tokens, m_rel_bias_table, m_ffn1_norm, m_ffn1_w_in, m_ffn1_w_out, m_mix_norm, m_w_in, m_forget_bias, m_fox_q_norm, m_fox_k_norm, m_swa_q_norm, m_swa_k_norm, m_swa_sinks, m_w_branch_fox, m_w_branch_swa, m_w_out, m_ffn2_norm, m_ffn2_w_in, m_ffn2_w_out, v_meta_tokens, v_rel_bias_table, v_ffn1_norm, v_ffn1_w_in, v_ffn1_w_out, v_mix_norm, v_w_in, v_forget_bias, v_fox_q_norm, v_fox_k_norm, v_swa_q_norm, v_swa_k_norm, v_swa_sinks, v_w_branch_fox, v_w_branch_swa, v_w_out, v_ffn2_norm, v_ffn2_w_in, v_ffn2_w_out):
    args = dict(locals())
    wts = {n: args[n] for n in WEIGHTS}
    mom1 = {n: args["m_" + n] for n in WEIGHTS}
    mom2 = {n: args["v_" + n] for n in WEIGHTS}

    seq, d = x.shape[1], x.shape[2]
    m_rows = seq + LANES
    depth = ffn1_norm.shape[0]
    fb = ffn1_w_in.shape[2]
    fo = ffn1_w_out.shape[1]
    din_shard = w_in.shape[2]
    cols = _Cols(d)
    scale = HEAD_DIM ** -0.5
    dev = 4 * lax.axis_index("x") + 2 * lax.axis_index("y") + lax.axis_index("c")

    full = dict(zip(BIG, exchange_hbm([wts[n].astype(BF16) for n in BIG], True, "gather_weights")))
    meta_all = gather_small(meta_tokens.reshape(1, N_META, -1), "gather_meta")
    meta_full = meta_all.transpose(1, 0, 2).reshape(N_META, d)

    def layer_weights(l):
        w = {}
        for nm in ("ffn1", "ffn2"):
            w[nm + "_in"] = full[nm + "_w_in"][:, l]
            w[nm + "_out"] = full[nm + "_w_out"][:, l].reshape(4, fb, d)
        w["wp"] = _reorder_w_in([full["w_in"][j, l] for j in range(N_DEV)], cols)
        w["wbf"] = jnp.concatenate([full["w_branch_fox"][j, l] for j in range(N_DEV)], axis=1)
        w["wbs"] = _permute_heads(jnp.concatenate([full["w_branch_swa"][j, l] for j in range(N_DEV)], axis=1), 0)
        w["wo"] = full["w_out"][:, l].reshape(d, d)
        return w

    lw = [layer_weights(l) for l in range(depth)]
    tile8 = lambda g, s=1.0: jnp.tile(g.reshape(1, HEAD_DIM) * s, (1, 8))
    tile2 = lambda g: jnp.tile(g.reshape(1, HEAD_DIM), (1, 2))
    bias = bias_build(rel_bias_table, "swa_bias")

    first = jnp.concatenate([jnp.zeros((PAD_FRONT, d), F32), meta_full], axis=0)
    h = jnp.concatenate([first, x[0]], axis=0)
    saved = []
    for l in range(depth):
        w = lw[l]
        s = {"h0": h}
        h, s["n1"], s["gate1"], s["up1"] = ffn_fwd(h, ffn1_norm[l:l + 1], w["ffn1_in"], w["ffn1_out"], f"ffn1_fwd_{l}")
        s["h1"] = h
        s["nm"], s["proj"] = mixer_proj(h, mix_norm[l:l + 1], w["wp"], f"mixer_proj_{l}")
        s["gains"] = (tile8(fox_q_norm[l], scale), tile8(fox_k_norm[l]), tile8(swa_q_norm[l], scale),
                      tile2(swa_k_norm[l]))
        s["fbias"] = _lane_pad(forget_bias[l:l + 1])
        qa, ka, va, qb, kb, vb, c = qk_post(s["proj"], s["gains"], s["fbias"], cols, f"qk_post_{l}")
        c_row = c[:, :8].T.reshape(8, 1, m_rows)
        s.update(qa=qa, ka=ka, va=va, qb=qb, kb=kb, vb=vb, c=c, c_row=c_row)
        s["o_fox"], s["lse_fox"] = fox_fwd(qa, ka, va, c, c_row, f"fox_fwd_{l}")
        s["o_swa"], s["lse_swa"] = swa_fwd(qb, kb, vb, bias, swa_sinks[l], f"swa_fwd_{l}")
        h = branch_out(h, s["o_fox"], s["o_swa"], s["proj"], w["wbf"], w["wbs"], w["wo"], cols, f"branch_out_{l}")
        s["h2"] = h
        h, s["n2"], s["gate2"], s["up2"] = ffn_fwd(h, ffn2_norm[l:l + 1], w["ffn2_in"], w["ffn2_out"], f"ffn2_fwd_{l}")
        saved.append(s)

    dh, loss_part = loss_head(h, loss_target[0], "loss_head")

    gw = {n: [None] * depth for n in BIG}
    gs = {n: [None] * depth for n in SMALL}
    dbias_total = None
    for l in reversed(range(depth)):
        w, s = lw[l], saved[l]

        def ffn_back(dh, tag, hin, norm, n_in, gate, up):
            dh_in, a, dg, du, dgn, dhs = ffn_bwd(dh, hin, norm, gate, up, w[tag + "_in"], w[tag + "_out"],
                                                 f"{tag}_bwd_{l}")
            d_in = matmul_tn(n_in[None], dg, f"{tag}_dwi_{l}", y2=du)
            d_out = matmul_tn(a, dhs[None], f"{tag}_dwo_{l}").reshape(N_DEV, fo, d)
            return dh_in, d_in, d_out, dgn

        dh, gw["ffn2_w_in"][l], gw["ffn2_w_out"][l], gs["ffn2_norm"][l] = ffn_back(
            dh, "ffn2", s["h2"], ffn2_norm[l:l + 1], s["n2"], s["gate2"], s["up2"])

        y, dtf, dts, dga, dgb, dof, dos = branch_out_bwd(dh, s["o_fox"], s["o_swa"], s["proj"], w["wbf"], w["wbs"],
                                                         w["wo"], cols, f"branch_out_bwd_{l}")
        gw["w_out"][l] = matmul_tn(y[None], dh[None], f"dw_out_{l}").reshape(N_DEV, d // N_DEV, d)
        gw["w_branch_fox"][l] = matmul_tn(s["o_fox"][None], dtf[None], f"dw_branch_fox_{l}", tn=d // N_DEV,
                                          blocked=True)
        gw["w_branch_swa"][l] = _permute_heads(
            matmul_tn(s["o_swa"][None], dts[None], f"dw_branch_swa_{l}", tn=d // N_DEV, blocked=True), 1, inverse=True)

        dqa, dka, dva, dc_row = fox_bwd(s["qa"], s["ka"], s["va"], s["c"], s["c_row"], s["o_fox"], s["lse_fox"], dof,
                                        f"fox_bwd_{l}")
        dqb, dkb, dvb, dbias, dsink = swa_bwd(s["qb"], s["kb"], s["vb"], bias, swa_sinks[l], s["o_swa"], s["lse_swa"],
                                              dos, f"swa_bwd_{l}")
        dbias_total = dbias if dbias_total is None else dbias_total + dbias
        gs["swa_sinks"][l] = dsink[0, :8]
        dc = _lane_pad(dc_row.T)
        dproj, ggqa, ggka, ggqb, ggkb, gfb = qk_post_bwd(s["proj"], s["gains"], s["fbias"], dqa, dka, dva, dqb, dkb,
                                                         dvb, dc, dga, dgb, cols, f"qk_post_bwd_{l}")
        gs["fox_q_norm"][l] = ggqa.reshape(8, HEAD_DIM).sum(0) * scale
        gs["fox_k_norm"][l] = ggka.reshape(8, HEAD_DIM).sum(0)
        gs["swa_q_norm"][l] = ggqb.reshape(8, HEAD_DIM).sum(0) * scale
        gs["swa_k_norm"][l] = ggkb.reshape(2, HEAD_DIM).sum(0)
        gs["forget_bias"][l] = gfb[0, :8]
        dwp = matmul_tn(s["nm"][None], dproj[None], f"dw_in_{l}", tn=cols.np // 5 if cols.np % 640 == 0 else None)[0]
        gw["w_in"][l] = _restore_w_in(dwp, cols, din_shard)
        dh, gs["mix_norm"][l] = dproj_bwd(dh, s["h1"], mix_norm[l:l + 1], dproj, w["wp"], f"dproj_bwd_{l}")

        dh, gw["ffn1_w_in"][l], gw["ffn1_w_out"][l], gs["ffn1_norm"][l] = ffn_back(
            dh, "ffn1", s["h0"], ffn1_norm[l:l + 1], s["n1"], s["gate1"], s["up1"])

    grad_x = dh[LANES:][None]
    dmeta = dh[PAD_FRONT:LANES]
    dtable = bias_reduce(dbias_total, "swa_dbias")[:, :8]

    parts = exchange_hbm([gw[n][l] for n in BIG for l in range(depth)], False, "scatter_grads")
    big_out = [{}, {}, {}, {}]
    for i, n in enumerate(BIG):
        outs = adamw_sum(parts[i * depth:(i + 1) * depth], wts[n], mom1[n], mom2[n], f"adamw_{n}")
        for k in range(4):
            big_out[k][n] = outs[k]

    small_g = {n: (jnp.stack(gs[n]) if n != "rel_bias_table" else None) for n in SMALL}
    small_g["rel_bias_table"] = dtable
    pieces = [loss_part[0:1, 0:1].reshape(1, 1)] + [small_g[n].reshape(1, -1) for n in SMALL] + [dmeta.reshape(1, -1)]
    small_shapes = [(1,)] + [wts[n].shape for n in SMALL] + [(N_META, d)]
    total = allsum_small(_pack(pieces, LANES, 8, F32), "allsum_small")
    summed = _unpack(total, small_shapes)
    loss = summed[0][0]
    g_small = dict(zip(SMALL, summed[1:1 + len(SMALL)]))
    g_meta = lax.dynamic_slice_in_dim(summed[-1], dev * (d // N_DEV), d // N_DEV, axis=1)
    names = SMALL + ("meta_tokens",)
    g_small["meta_tokens"] = g_meta
    pk = lambda src: _pack([src[n].reshape(1, -1) for n in names], LANES, 8, F32)[0]
    small_out = [dict(zip(names, _unpack(o, [wts[n].shape for n in names])))
                 for o in adamw_small(pk(g_small), pk(wts), pk(mom1), pk(mom2), "adamw_small")]

    grads = {**big_out[0], **g_small}
    delta = {**big_out[1], **small_out[0]}
    new_m = {**big_out[2], **small_out[1]}
    new_v = {**big_out[3], **small_out[2]}
    return (loss, grad_x, *[grads[n] for n in WEIGHTS], *[delta[n] for n in WEIGHTS],
            *[new_m[n] for n in WEIGHTS], *[new_v[n] for n in WEIGHTS])
```

```python
import functools
import math

import numpy as np
import jax
import jax.numpy as jnp
from jax import lax
from jax.experimental import pallas as pl
from jax.experimental.pallas import tpu as pltpu

F32 = jnp.float32
BF16 = jnp.bfloat16
EPS = 1e-6
NEG = -1e30
HEAD_DIM = 64
LANES = 128
N_META = 16
PAD_FRONT = LANES - N_META
N_BUCKETS = 32
MAX_DISTANCE = 128
N_DEV = 8
ADAM_LR, ADAM_B1, ADAM_B2, ADAM_EPS, ADAM_WD, ADAM_STEP = 0.001, 0.9, 0.999, 1e-08, 0.01, 10
VMEM_LIMIT = 56 * 1024 * 1024
MESH = pl.DeviceIdType.MESH


def _params(n_grid):
    return pltpu.CompilerParams(dimension_semantics=("arbitrary",) * n_grid,
                                vmem_limit_bytes=VMEM_LIMIT)


def _dot(a, b):
    return jnp.dot(a, b, preferred_element_type=F32)


def _dot_nt(a, b):
    return lax.dot_general(a, b, (((1,), (1,)), ((), ())), preferred_element_type=F32)


def _dot_tn(a, b):
    return lax.dot_general(a, b, (((0,), (0,)), ((), ())), preferred_element_type=F32)


def _rms(x):
    r = lax.rsqrt(jnp.mean(x * x, axis=-1, keepdims=True) + EPS)
    return x * r, r


def _rms_bwd(x, g, dn):
    xh, r = _rms(x)
    dxh = dn * g
    dx = r * (dxh - xh * jnp.mean(dxh * xh, axis=-1, keepdims=True))
    return dx, jnp.sum(dn * xh, axis=0, keepdims=True)


def _split2(v):
    hi = v.astype(BF16)
    return hi, (v - hi.astype(F32)).astype(BF16)


def _split3(v):
    hi = v.astype(BF16)
    r1 = v - hi.astype(F32)
    mid = r1.astype(BF16)
    return hi, mid, (r1 - mid.astype(F32)).astype(BF16)


def _group_ones():
    r = lax.broadcasted_iota(jnp.int32, (LANES, LANES), 0) // HEAD_DIM
    c = lax.broadcasted_iota(jnp.int32, (LANES, LANES), 1) // HEAD_DIM
    return jnp.where(r == c, 1.0, 0.0).astype(BF16)


def _group_mean(v, ones):
    hi, lo = _split2(v)
    return (_dot(hi, ones) + _dot(lo, ones)) * (1.0 / HEAD_DIM)


def _row_tile(m):
    return 384 if m % 384 == 0 else LANES


def _peer(k):
    x, y, c = lax.axis_index("x"), lax.axis_index("y"), lax.axis_index("c")
    px = 1 - x if k & 4 else x
    py = 1 - y if k & 2 else y
    pc = 1 - c if k & 1 else c
    return (px, py, pc), 4 * px + 2 * py + pc


def _exchange_body(src_ref, dst_ref, send_sems, recv_sems, local_sem, bcast):
    x, y, c = lax.axis_index("x"), lax.axis_index("y"), lax.axis_index("c")
    me = 4 * x + 2 * y + c
    mine = pltpu.make_async_copy(src_ref.at[0 if bcast else me], dst_ref.at[me], local_sem)
    mine.start()
    sends = []
    for k in range(1, N_DEV):
        dev, idx = _peer(k)
        cp = pltpu.make_async_remote_copy(
            src_ref=src_ref.at[0 if bcast else idx], dst_ref=dst_ref.at[me],
            send_sem=send_sems.at[k - 1], recv_sem=recv_sems.at[k - 1],
            device_id=dev, device_id_type=MESH)
        cp.start()
        sends.append(cp)
    for k in range(1, N_DEV):
        dev, idx = _peer(k)
        pltpu.make_async_remote_copy(
            src_ref=src_ref.at[0], dst_ref=dst_ref.at[idx],
            send_sem=send_sems.at[k - 1], recv_sem=recv_sems.at[k - 1],
            device_id=dev, device_id_type=MESH).wait_recv()
    for cp in sends:
        cp.wait_send()
    mine.wait()


class Rider:
    def __init__(self, srcs=(), bcast=True):
        self.srcs, self.bcast, self.n = list(srcs), bcast, len(srcs)

    def out_shapes(self):
        return [jax.ShapeDtypeStruct(((N_DEV,) + s.shape) if self.bcast else s.shape, s.dtype) for s in self.srcs]

    def specs(self):
        return [pl.BlockSpec(memory_space=pl.ANY)] * self.n

    def scratch(self):
        if not self.n:
            return []
        return [pltpu.SemaphoreType.DMA((self.n * (N_DEV - 1),)), pltpu.SemaphoreType.DMA((self.n * (N_DEV - 1),)),
                pltpu.SemaphoreType.DMA((self.n,))]

    def _copies(self, src_refs, dst_refs, send_sems, recv_sems, local_sems):
        x, y, c = lax.axis_index("x"), lax.axis_index("y"), lax.axis_index("c")
        me = 4 * x + 2 * y + c
        own, sends, recvs = [], [], []
        for a, (s, d) in enumerate(zip(src_refs, dst_refs)):
            own.append(pltpu.make_async_copy(s if self.bcast else s.at[me], d.at[me], local_sems.at[a]))
            for k in range(1, N_DEV):
                dev, idx = _peer(k)
                sem = a * (N_DEV - 1) + k - 1
                sends.append(pltpu.make_async_remote_copy(
                    src_ref=s if self.bcast else s.at[idx], dst_ref=d.at[me],
                    send_sem=send_sems.at[sem], recv_sem=recv_sems.at[sem], device_id=dev, device_id_type=MESH))
                recvs.append(pltpu.make_async_remote_copy(
                    src_ref=d.at[idx], dst_ref=d.at[idx],
                    send_sem=send_sems.at[sem], recv_sem=recv_sems.at[sem], device_id=dev, device_id_type=MESH))
        return own, sends, recvs

    def start(self, *refs):
        own, sends, _ = self._copies(*refs)
        for cp in own + sends:
            cp.start()

    def wait(self, *refs):
        own, sends, recvs = self._copies(*refs)
        for cp in recvs:
            cp.wait_recv()
        for cp in sends:
            cp.wait_send()
        for cp in own:
            cp.wait()


def rider_call(core, name, grid, in_specs, out_specs, out_shape, scratch_shapes, args, rider=None):
    rider = rider or Rider()
    n_in, n_out, n_scr, nr = len(in_specs), len(out_specs), len(scratch_shapes), rider.n

    def body(*refs):
        ins, r_src = refs[:n_in], refs[n_in:n_in + nr]
        outs = refs[n_in + nr:n_in + nr + n_out]
        r_dst = refs[n_in + nr + n_out:n_in + 2 * nr + n_out]
        scr = refs[n_in + 2 * nr + n_out:n_in + 2 * nr + n_out + n_scr]
        sems = refs[n_in + 2 * nr + n_out + n_scr:]
        if nr:
            first, last = True, True
            for ax, size in enumerate(grid):
                first = first & (pl.program_id(ax) == 0)
                last = last & (pl.program_id(ax) == size - 1)
            if not grid:
                rider.start(r_src, r_dst, *sems)
            else:
                pl.when(first)(lambda: rider.start(r_src, r_dst, *sems))
        core(*ins, *outs, *scr)
        if nr:
            if not grid:
                rider.wait(r_src, r_dst, *sems)
            else:
                pl.when(last)(lambda: rider.wait(r_src, r_dst, *sems))

    res = pl.pallas_call(
        body, name=name, grid=grid,
        in_specs=list(in_specs) + rider.specs(),
        out_specs=list(out_specs) + rider.specs(),
        out_shape=list(out_shape) + rider.out_shapes(),
        scratch_shapes=list(scratch_shapes) + rider.scratch(),
        compiler_params=_params(len(grid)),
    )(*args, *rider.srcs)
    return res[:n_out], res[n_out:]


def exchange_hbm(srcs, bcast, name):
    return rider_call(lambda: None, name, (), [], [], [], [], [], Rider(srcs, bcast))[1]


def allsum_small(vec, name):
    def body(src_ref, out_ref, dst_ref, send_sems, recv_sems, local_sem):
        _exchange_body(src_ref, dst_ref, send_sems, recv_sems, local_sem, True)
        acc = dst_ref[0]
        for j in range(1, N_DEV):
            acc = acc + dst_ref[j]
        out_ref[...] = acc

    return pl.pallas_call(
        body, name=name,
        out_shape=jax.ShapeDtypeStruct(vec.shape[1:], F32),
        in_specs=[pl.BlockSpec(memory_space=pltpu.VMEM)],
        out_specs=pl.BlockSpec(memory_space=pltpu.VMEM),
        scratch_shapes=[pltpu.VMEM((N_DEV,) + vec.shape[1:], F32),
                        pltpu.SemaphoreType.DMA((N_DEV - 1,)), pltpu.SemaphoreType.DMA((N_DEV - 1,)),
                        pltpu.SemaphoreType.DMA],
    )(vec)


def gather_small(vec, name):
    def body(src_ref, dst_ref, send_sems, recv_sems, local_sem):
        _exchange_body(src_ref, dst_ref, send_sems, recv_sems, local_sem, True)

    return pl.pallas_call(
        body, name=name,
        out_shape=jax.ShapeDtypeStruct((N_DEV,) + vec.shape[1:], F32),
        in_specs=[pl.BlockSpec(memory_space=pltpu.VMEM)],
        out_specs=pl.BlockSpec(memory_space=pltpu.VMEM),
        scratch_shapes=[pltpu.SemaphoreType.DMA((N_DEV - 1,)), pltpu.SemaphoreType.DMA((N_DEV - 1,)),
                        pltpu.SemaphoreType.DMA],
    )(vec)


def ffn_fwd(h, g, w_in8, w_out4, name, rider=None):
    m, d = h.shape
    fb = w_in8.shape[2]
    tm = _row_tile(m)

    def body(h_ref, g_ref, wg_ref, wu_ref, wo_ref, hn_ref, n_ref, gate_ref, up_ref, acc_ref):
        i = pl.program_id(1)

        @pl.when(i == 0)
        def _():
            xh, _ = _rms(h_ref[...])
            n_ref[...] = (xh * g_ref[...]).astype(BF16)
            acc_ref[...] = jnp.zeros_like(acc_ref)

        n = n_ref[...]
        gate = _dot(n, wg_ref[0])
        up = _dot(n, wu_ref[0])
        gate_ref[0] = gate
        up_ref[0] = up
        a = (gate * jax.nn.sigmoid(gate) * up).astype(BF16)
        acc_ref[...] += _dot(a, wo_ref[0])

        @pl.when(i == 3)
        def _():
            hn_ref[...] = h_ref[...] + 0.5 * acc_ref[...]

    return rider_call(
        body, name, (m // tm, 4),
        in_specs=[pl.BlockSpec((tm, d), lambda r, i: (r, 0)),
                  pl.BlockSpec((1, d), lambda r, i: (0, 0)),
                  pl.BlockSpec((1, d, fb), lambda r, i: (i, 0, 0)),
                  pl.BlockSpec((1, d, fb), lambda r, i: (i + 4, 0, 0)),
                  pl.BlockSpec((1, fb, d), lambda r, i: (i, 0, 0))],
        out_specs=[pl.BlockSpec((tm, d), lambda r, i: (r, 0)),
                   pl.BlockSpec((tm, d), lambda r, i: (r, 0)),
                   pl.BlockSpec((1, tm, fb), lambda r, i: (i, r, 0)),
                   pl.BlockSpec((1, tm, fb), lambda r, i: (i, r, 0))],
        out_shape=[jax.ShapeDtypeStruct((m, d), F32), jax.ShapeDtypeStruct((m, d), BF16),
                   jax.ShapeDtypeStruct((4, m, fb), F32), jax.ShapeDtypeStruct((4, m, fb), F32)],
        scratch_shapes=[pltpu.VMEM((tm, d), F32)],
        args=(h, g, w_in8, w_in8, w_out4), rider=rider)


def ffn_bwd(dh, h, g, gate, up, w_in8, w_out4, name, rider=None):
    m, d = h.shape
    fb = w_in8.shape[2]
    tm = _row_tile(m)

    def body(dh_ref, h_ref, g_ref, gate_ref, up_ref, wg_ref, wu_ref, wo_ref,
             dhin_ref, a_ref, dg_ref, du_ref, dgn_ref, dhs_ref, acc_ref):
        r = pl.program_id(0)
        i = pl.program_id(1)

        @pl.when(i == 0)
        def _():
            dhs_ref[...] = (0.5 * dh_ref[...]).astype(BF16)
            acc_ref[...] = jnp.zeros_like(acc_ref)

        @pl.when((r == 0) & (i == 0))
        def _():
            dgn_ref[...] = jnp.zeros_like(dgn_ref)

        da = _dot_nt(dhs_ref[...], wo_ref[0])
        gt = gate_ref[0]
        u = up_ref[0]
        sg = jax.nn.sigmoid(gt)
        sl = gt * sg
        a_ref[0] = (sl * u).astype(BF16)
        dub = (da * sl).astype(BF16)
        dgb = (da * u * (sg * (1.0 + gt * (1.0 - sg)))).astype(BF16)
        dg_ref[0] = dgb
        du_ref[0] = dub
        acc_ref[...] += _dot_nt(dgb, wg_ref[0]) + _dot_nt(dub, wu_ref[0])

        @pl.when(i == 3)
        def _():
            dx, dgain = _rms_bwd(h_ref[...], g_ref[...], acc_ref[...])
            dgn_ref[...] += dgain
            dhin_ref[...] = dh_ref[...] + dx

    row = lambda r, i: (r, 0)
    blk = lambda r, i: (i, r, 0)
    return rider_call(
        body, name, (m // tm, 4),
        in_specs=[pl.BlockSpec((tm, d), row), pl.BlockSpec((tm, d), row),
                  pl.BlockSpec((1, d), lambda r, i: (0, 0)),
                  pl.BlockSpec((1, tm, fb), blk), pl.BlockSpec((1, tm, fb), blk),
                  pl.BlockSpec((1, d, fb), lambda r, i: (i, 0, 0)),
                  pl.BlockSpec((1, d, fb), lambda r, i: (i + 4, 0, 0)),
                  pl.BlockSpec((1, fb, d), lambda r, i: (i, 0, 0))],
        out_specs=[pl.BlockSpec((tm, d), row),
                   pl.BlockSpec((1, tm, fb), blk), pl.BlockSpec((1, tm, fb), blk),
                   pl.BlockSpec((1, tm, fb), blk),
                   pl.BlockSpec((1, d), lambda r, i: (0, 0)),
                   pl.BlockSpec((tm, d), row)],
        out_shape=[jax.ShapeDtypeStruct((m, d), F32),
                   jax.ShapeDtypeStruct((4, m, fb), BF16), jax.ShapeDtypeStruct((4, m, fb), BF16),
                   jax.ShapeDtypeStruct((4, m, fb), BF16),
                   jax.ShapeDtypeStruct((1, d), F32), jax.ShapeDtypeStruct((m, d), BF16)],
        scratch_shapes=[pltpu.VMEM((tm, d), F32)],
        args=(dh, h, g, gate, up, w_in8, w_in8, w_out4), rider=rider)


def matmul_tn(x, y, name, tn=None, blocked=False, y2=None):
    bx, m, k = x.shape
    by, _, n = y.shape
    b = max(bx, by) * (2 if y2 is not None else 1)
    tm = _row_tile(m)
    tn = n if tn is None else tn
    nt = n // tn
    nr = m // tm

    def body(*refs):
        x_ref, y_ref = refs[0], refs[1]
        o_ref, acc_ref = refs[-2], refs[-1]
        r = pl.program_id(2)

        @pl.when(r == 0)
        def _():
            acc_ref[...] = jnp.zeros_like(acc_ref)

        if y2 is None:
            acc_ref[...] += _dot_tn(x_ref[0].astype(BF16), y_ref[0].astype(BF16))
        else:
            @pl.when(pl.program_id(0) < by)
            def _():
                acc_ref[...] += _dot_tn(x_ref[0].astype(BF16), y_ref[0].astype(BF16))

            @pl.when(pl.program_id(0) >= by)
            def _():
                acc_ref[...] += _dot_tn(x_ref[0].astype(BF16), refs[2][0].astype(BF16))

        @pl.when(r == nr - 1)
        def _():
            o_ref[0] = acc_ref[...].astype(BF16)

    x_map = (lambda i, j, r: (i, r, 0)) if bx > 1 else (lambda i, j, r: (0, r, 0))
    if y2 is None:
        y_specs = [pl.BlockSpec((1, tm, tn), (lambda i, j, r: (i, r, j)) if by > 1 else (lambda i, j, r: (0, r, j)))]
    else:
        y_specs = [pl.BlockSpec((1, tm, tn), lambda i, j, r: (jnp.minimum(i, by - 1), jnp.where(i < by, r, nr - 1), j)),
                   pl.BlockSpec((1, tm, tn), lambda i, j, r: (jnp.maximum(i - by, 0), jnp.where(i < by, 0, r), j))]
    if blocked:
        out_spec = pl.BlockSpec((1, k, tn), lambda i, j, r: (i * nt + j, 0, 0))
        out_shape = jax.ShapeDtypeStruct((b * nt, k, tn), BF16)
    else:
        out_spec = pl.BlockSpec((1, k, tn), lambda i, j, r: (i, 0, j))
        out_shape = jax.ShapeDtypeStruct((b, k, n), BF16)
    return pl.pallas_call(
        body, name=name, grid=(b, nt, nr),
        in_specs=[pl.BlockSpec((1, tm, k), x_map)] + y_specs,
        out_specs=out_spec, out_shape=out_shape,
        scratch_shapes=[pltpu.VMEM((k, tn), F32)],
        compiler_params=_params(3),
    )(*([x, y] + ([y2] if y2 is not None else [])))


class _Cols:
    def __init__(self, d):
        self.d = d
        self.ga, self.gb = 0, d
        self.qa, self.ka, self.va, self.qb = 2 * d, 2 * d + 512, 2 * d + 1024, 2 * d + 1536
        self.kb, self.vb, self.fa = 2 * d + 2048, 2 * d + 2176, 2 * d + 2304
        self.np = 2 * d + 2432


def mixer_proj(h, g, wp, name):
    m, d = h.shape
    npad = wp.shape[1]
    tm = _row_tile(m)

    def body(h_ref, g_ref, w_ref, n_ref, p_ref):
        xh, _ = _rms(h_ref[...])
        n = (xh * g_ref[...]).astype(BF16)
        n_ref[...] = n
        p_ref[...] = _dot(n, w_ref[...])

    return pl.pallas_call(
        body, name=name, grid=(m // tm,),
        in_specs=[pl.BlockSpec((tm, d), lambda r: (r, 0)), pl.BlockSpec((1, d), lambda r: (0, 0)),
                  pl.BlockSpec((d, npad), lambda r: (0, 0))],
        out_specs=[pl.BlockSpec((tm, d), lambda r: (r, 0)), pl.BlockSpec((tm, npad), lambda r: (r, 0))],
        out_shape=[jax.ShapeDtypeStruct((m, d), BF16), jax.ShapeDtypeStruct((m, npad), F32)],
        compiler_params=_params(1),
    )(h, g, wp)


def _head_norm(x, gain, ones):
    outs = []
    for b in range(x.shape[1] // LANES):
        xb = x[:, b * LANES:(b + 1) * LANES]
        r = lax.rsqrt(_group_mean(xb * xb, ones) + EPS)
        outs.append(xb * r * gain[:, b * LANES:(b + 1) * LANES])
    return outs


def _head_norm_bwd(x, gain, dn, ones):
    dxs, dgs = [], []
    for b in range(x.shape[1] // LANES):
        sl = slice(b * LANES, (b + 1) * LANES)
        xb, dnb = x[:, sl], dn[:, sl]
        r = lax.rsqrt(_group_mean(xb * xb, ones) + EPS)
        xh = xb * r
        dxh = dnb * gain[:, sl]
        dxs.append(r * (dxh - xh * _group_mean(dxh * xh, ones)))
        dgs.append(jnp.sum(dnb * xh, axis=0, keepdims=True))
    return dxs, dgs


def qk_post(proj, gains, fbias, cols, name):
    m = proj.shape[0]
    tm = _row_tile(m)
    gqa, gka, gqb, gkb = gains

    def body(qa_ref, ka_ref, va_ref, qb_ref, kb_ref, vb_ref, fa_ref, gqa_ref, gka_ref, gqb_ref, gkb_ref, fb_ref,
             qa_o, ka_o, va_o, qb_o, kb_o, vb_o, c_o, carry_ref):
        @pl.when(pl.program_id(0) == 0)
        def _():
            carry_ref[...] = jnp.zeros_like(carry_ref)

        ones = _group_ones()
        for src, gn, dst in ((qa_ref, gqa_ref, qa_o), (ka_ref, gka_ref, ka_o), (qb_ref, gqb_ref, qb_o),
                             (kb_ref, gkb_ref, kb_o)):
            for b, blk in enumerate(_head_norm(src[...], gn[...], ones)):
                dst[:, b * LANES:(b + 1) * LANES] = blk.astype(BF16)
        va_o[...] = va_ref[...].astype(BF16)
        vb_o[...] = vb_ref[...].astype(BF16)
        z = fa_ref[...] + fb_ref[...]
        logf = jnp.minimum(z, 0.0) - jnp.log(1.0 + jnp.exp(-jnp.abs(z)))
        rr = lax.broadcasted_iota(jnp.int32, (tm, tm), 0)
        cc = lax.broadcasted_iota(jnp.int32, (tm, tm), 1)
        tril = jnp.where(cc <= rr, 1.0, 0.0).astype(BF16)
        p0, p1, p2 = _split3(logf)
        c_o[...] = _dot(tril, p0) + _dot(tril, p1) + _dot(tril, p2) + carry_ref[...]
        carry_ref[...] += jnp.sum(logf, axis=0, keepdims=True)

    w512 = lambda off: pl.BlockSpec((tm, 512), lambda r, o=off // 512: (r, o))
    w128 = lambda off: pl.BlockSpec((tm, LANES), lambda r, o=off // LANES: (r, o))
    vec = lambda w: pl.BlockSpec((1, w), lambda r: (0, 0))
    row = lambda w: pl.BlockSpec((tm, w), lambda r: (r, 0))
    return pl.pallas_call(
        body, name=name, grid=(m // tm,),
        in_specs=[w512(cols.qa), w512(cols.ka), w512(cols.va), w512(cols.qb), w128(cols.kb), w128(cols.vb),
                  w128(cols.fa), vec(512), vec(512), vec(512), vec(LANES), vec(LANES)],
        out_specs=[row(512), row(512), row(512), row(512), row(LANES), row(LANES), row(LANES)],
        out_shape=[jax.ShapeDtypeStruct((m, 512), BF16)] * 4 + [jax.ShapeDtypeStruct((m, LANES), BF16)] * 2
                  + [jax.ShapeDtypeStruct((m, LANES), F32)],
        scratch_shapes=[pltpu.VMEM((1, LANES), F32)],
        compiler_params=_params(1),
    )(proj, proj, proj, proj, proj, proj, proj, gqa, gka, gqb, gkb, fbias)


def qk_post_bwd(proj, gains, fbias, dqa, dka, dva, dqb, dkb, dvb, dc, dga, dgb, cols, name):
    m = proj.shape[0]
    d = cols.d
    tm = _row_tile(m)
    nt = m // tm
    gqa, gka, gqb, gkb = gains

    def body(qa_ref, ka_ref, qb_ref, kb_ref, fa_ref, gqa_ref, gka_ref, gqb_ref, gkb_ref, fb_ref,
             dqa_ref, dka_ref, dva_ref, dqb_ref, dkb_ref, dvb_ref, dc_ref, dga_ref, dgb_ref,
             dp_o, ggqa_o, ggka_o, ggqb_o, ggkb_o, gfb_o, carry_ref):
        @pl.when(pl.program_id(0) == 0)
        def _():
            carry_ref[...] = jnp.zeros_like(carry_ref)
            for o in (ggqa_o, ggka_o, ggqb_o, ggkb_o, gfb_o):
                o[...] = jnp.zeros_like(o)

        ones = _group_ones()
        dp_o[:, cols.ga:cols.ga + d] = dga_ref[...].astype(BF16)
        dp_o[:, cols.gb:cols.gb + d] = dgb_ref[...].astype(BF16)
        dp_o[:, cols.va:cols.va + 512] = dva_ref[...].astype(BF16)
        dp_o[:, cols.vb:cols.vb + LANES] = dvb_ref[...].astype(BF16)
        for src, gn, dn, off, gout in ((qa_ref, gqa_ref, dqa_ref, cols.qa, ggqa_o),
                                       (ka_ref, gka_ref, dka_ref, cols.ka, ggka_o),
                                       (qb_ref, gqb_ref, dqb_ref, cols.qb, ggqb_o),
                                       (kb_ref, gkb_ref, dkb_ref, cols.kb, ggkb_o)):
            dxs, dgs = _head_norm_bwd(src[...], gn[...], dn[...], ones)
            for b, (dx, dg) in enumerate(zip(dxs, dgs)):
                dp_o[:, off + b * LANES:off + (b + 1) * LANES] = dx.astype(BF16)
                gout[:, b * LANES:(b + 1) * LANES] += dg
        dcv = dc_ref[...]
        rr = lax.broadcasted_iota(jnp.int32, (tm, tm), 0)
        cc = lax.broadcasted_iota(jnp.int32, (tm, tm), 1)
        triu = jnp.where(cc >= rr, 1.0, 0.0).astype(BF16)
        p0, p1, p2 = _split3(dcv)
        dlogf = _dot(triu, p0) + _dot(triu, p1) + _dot(triu, p2) + carry_ref[...]
        carry_ref[...] += jnp.sum(dcv, axis=0, keepdims=True)
        z = fa_ref[...] + fb_ref[...]
        row = (nt - 1 - pl.program_id(0)) * tm + lax.broadcasted_iota(jnp.int32, (tm, LANES), 0)
        dfa = jnp.where(row >= PAD_FRONT, dlogf * jax.nn.sigmoid(-z), 0.0)
        dp_o[:, cols.fa:cols.fa + LANES] = dfa.astype(BF16)
        gfb_o[...] += jnp.sum(dfa, axis=0, keepdims=True)

    rev = lambda r: nt - 1 - r
    w512 = lambda off: pl.BlockSpec((tm, 512), lambda r, o=off // 512: (rev(r), o))
    w128 = lambda off: pl.BlockSpec((tm, LANES), lambda r, o=off // LANES: (rev(r), o))
    vec = lambda w: pl.BlockSpec((1, w), lambda r: (0, 0))
    row = lambda w: pl.BlockSpec((tm, w), lambda r: (rev(r), 0))
    return pl.pallas_call(
        body, name=name, grid=(nt,),
        in_specs=[w512(cols.qa), w512(cols.ka), w512(cols.qb), w128(cols.kb), w128(cols.fa),
                  vec(512), vec(512), vec(512), vec(LANES), vec(LANES),
                  row(512), row(512), row(512), row(512), row(LANES), row(LANES), row(LANES), row(d), row(d)],
        out_specs=[row(cols.np), vec(512), vec(512), vec(512), vec(LANES), vec(LANES)],
        out_shape=[jax.ShapeDtypeStruct((m, cols.np), BF16)] + [jax.ShapeDtypeStruct((1, 512), F32)] * 3
                  + [jax.ShapeDtypeStruct((1, LANES), F32)] * 2,
        scratch_shapes=[pltpu.VMEM((1, LANES), F32)],
        compiler_params=_params(1),
    )(proj, proj, proj, proj, proj, gqa, gka, gqb, gkb, fbias, dqa, dka, dva, dqb, dkb, dvb, dc, dga, dgb)


def dproj_bwd(dh, h, g, dproj, wp, name):
    m, d = h.shape
    npad = wp.shape[1]
    tm = _row_tile(m)

    def body(dh_ref, h_ref, g_ref, dp_ref, w_ref, dhin_ref, dgn_ref):
        @pl.when(pl.program_id(0) == 0)
        def _():
            dgn_ref[...] = jnp.zeros_like(dgn_ref)

        dn = _dot_nt(dp_ref[...], w_ref[...])
        dx, dgain = _rms_bwd(h_ref[...], g_ref[...], dn)
        dgn_ref[...] += dgain
        dhin_ref[...] = dh_ref[...] + dx

    row = lambda w: pl.BlockSpec((tm, w), lambda r: (r, 0))
    return pl.pallas_call(
        body, name=name, grid=(m // tm,),
        in_specs=[row(d), row(d), pl.BlockSpec((1, d), lambda r: (0, 0)), row(npad),
                  pl.BlockSpec((d, npad), lambda r: (0, 0))],
        out_specs=[row(d), pl.BlockSpec((1, d), lambda r: (0, 0))],
        out_shape=[jax.ShapeDtypeStruct((m, d), F32), jax.ShapeDtypeStruct((1, d), F32)],
        compiler_params=_params(1),
    )(dh, h, g, dproj, wp)


def _lane_col(v, lane_iota, idx):
    return jnp.sum(jnp.where(lane_iota == idx, v, 0.0), axis=1, keepdims=True)


def fox_fwd(q, k, v, c, c_row, name, rider=None):
    m = q.shape[0]
    t = _row_tile(m)
    nq = m // t

    def body(q_ref, k_ref, v_ref, c_ref, crow_ref, o_ref, lse_ref, acc_ref, m_ref, l_ref):
        pair = pl.program_id(0)
        qi = pl.program_id(1)
        lane1 = lax.broadcasted_iota(jnp.int32, (1, LANES), 1)
        lane_t = lax.broadcasted_iota(jnp.int32, (t, LANES), 1)
        in_head = [lane1 < HEAD_DIM, lane1 >= HEAD_DIM]
        qt = q_ref[...]
        qs = [jnp.where(in_head[e], qt, jnp.zeros_like(qt)) for e in (0, 1)]
        cfull = c_ref[...]
        ccol = [_lane_col(cfull, lane_t, 2 * pair + e) for e in (0, 1)]
        row_pos = qi * t + lax.broadcasted_iota(jnp.int32, (t, t), 0)
        col_loc = lax.broadcasted_iota(jnp.int32, (t, t), 1)
        acc_ref[...] = jnp.zeros_like(acc_ref)
        m_ref[...] = jnp.full_like(m_ref, NEG)
        l_ref[...] = jnp.zeros_like(l_ref)

        def step(ki, carry):
            off = pl.multiple_of(ki * t, t)
            kt = k_ref[pl.ds(off, t), :]
            vt = v_ref[pl.ds(off, t), :]
            col_pos = ki * t + col_loc
            valid = (col_pos <= row_pos) & (col_pos >= PAD_FRONT)
            for e in (0, 1):
                s = _dot_nt(qs[e], kt) + ccol[e] - crow_ref[e, :, pl.ds(off, t)]
                s = jnp.where(valid, s, NEG)
                m_old = m_ref[e]
                m_new = jnp.maximum(m_old, jnp.max(s, axis=1, keepdims=True))
                alpha = jnp.exp(m_old - m_new)
                p = jnp.exp(s - m_new)
                l_ref[e] = alpha * l_ref[e] + jnp.sum(p, axis=1, keepdims=True)
                m_ref[e] = m_new
                ve = jnp.where(in_head[e], vt, jnp.zeros_like(vt))
                acc_ref[...] = acc_ref[...] * jnp.where(in_head[e], alpha, 1.0) + _dot(p.astype(BF16), ve)
            return carry

        lax.fori_loop(0, qi + 1, step, 0)
        o_ref[...] = acc_ref[...] * jnp.where(in_head[0], 1.0 / l_ref[0], 1.0 / l_ref[1])
        lse_ref[0] = jnp.where(in_head[0], m_ref[0] + jnp.log(l_ref[0]), m_ref[1] + jnp.log(l_ref[1]))

    return rider_call(
        body, name, (4, nq),
        in_specs=[pl.BlockSpec((t, LANES), lambda p, i: (i, p)),
                  pl.BlockSpec((m, LANES), lambda p, i: (0, p)),
                  pl.BlockSpec((m, LANES), lambda p, i: (0, p)),
                  pl.BlockSpec((t, LANES), lambda p, i: (i, 0)),
                  pl.BlockSpec((2, 1, m), lambda p, i: (p, 0, 0))],
        out_specs=[pl.BlockSpec((t, LANES), lambda p, i: (i, p)),
                   pl.BlockSpec((1, t, LANES), lambda p, i: (p, i, 0))],
        out_shape=[jax.ShapeDtypeStruct((m, 512), F32), jax.ShapeDtypeStruct((4, m, LANES), F32)],
        scratch_shapes=[pltpu.VMEM((t, LANES), F32), pltpu.VMEM((2, t, 1), F32), pltpu.VMEM((2, t, 1), F32)],
        args=(q, k, v, c, c_row), rider=rider)


def fox_bwd(q, k, v, c, c_row, o, lse, do, name, rider=None):
    m = q.shape[0]
    t = _row_tile(m)
    nq = m // t

    def body(k_ref, v_ref, crow_ref, q_ref, do_ref, o_ref, lse_ref, c_ref,
             dk_ref, dv_ref, dq_ref, dcr_ref, dcq_ref, dka_ref, dva_ref, dca_ref):
        pair = pl.program_id(0)
        ki = pl.program_id(1)

        @pl.when(ki == 0)
        def _():
            dq_ref[...] = jnp.zeros_like(dq_ref)
            dcq_ref[...] = jnp.zeros_like(dcq_ref)

        lane1 = lax.broadcasted_iota(jnp.int32, (1, LANES), 1)
        lane_t = lax.broadcasted_iota(jnp.int32, (t, LANES), 1)
        in_head = [lane1 < HEAD_DIM, lane1 >= HEAD_DIM]
        kt = k_ref[...]
        vt = v_ref[...]
        ks = [jnp.where(in_head[e], kt, jnp.zeros_like(kt)) for e in (0, 1)]
        crow = [crow_ref[e] for e in (0, 1)]
        col_pos = ki * t + lax.broadcasted_iota(jnp.int32, (t, t), 1)
        row_loc = lax.broadcasted_iota(jnp.int32, (t, t), 0)
        dka_ref[...] = jnp.zeros_like(dka_ref)
        dva_ref[...] = jnp.zeros_like(dva_ref)
        dca_ref[...] = jnp.zeros_like(dca_ref)

        def step(qi, carry):
            off = pl.multiple_of(qi * t, t)
            qt = q_ref[pl.ds(off, t), :]
            dot = do_ref[pl.ds(off, t), :]
            prod = dot * o_ref[pl.ds(off, t), :]
            lset = lse_ref[0, pl.ds(off, t), :]
            ct = c_ref[pl.ds(off, t), :]
            valid = (col_pos <= off + row_loc) & (col_pos >= PAD_FRONT)
            row_sums = []
            for e in (0, 1):
                qe = jnp.where(in_head[e], qt, jnp.zeros_like(qt))
                doe = jnp.where(in_head[e], dot, 0.0).astype(BF16)
                delta = jnp.sum(jnp.where(in_head[e], prod, 0.0), axis=1, keepdims=True)
                lse_e = _lane_col(lset, lane_t, HEAD_DIM * e)
                ccol = _lane_col(ct, lane_t, 2 * pair + e)
                s = _dot_nt(qe, kt) + ccol - crow[e]
                s = jnp.where(valid, s, NEG)
                p = jnp.exp(s - lse_e)
                dp = _dot_nt(doe, vt)
                ds = p * (dp - delta)
                dsb = ds.astype(BF16)
                dva_ref[...] += _dot_tn(p.astype(BF16), doe)
                dka_ref[...] += _dot_tn(dsb, qe)
                dq_ref[pl.ds(off, t), :] += _dot(dsb, ks[e])
                dca_ref[e] -= jnp.sum(ds, axis=0, keepdims=True)
                row_sums.append(jnp.sum(ds, axis=1, keepdims=True))
            dcq_ref[0, pl.ds(off, t), :] += jnp.where(in_head[0], row_sums[0], row_sums[1])
            return carry

        lax.fori_loop(ki, nq, step, 0)
        dk_ref[...] = dka_ref[...]
        dv_ref[...] = dva_ref[...]
        dcr_ref[...] = dca_ref[...]

    tile = pl.BlockSpec((t, LANES), lambda p, i: (i, p))
    full = pl.BlockSpec((m, LANES), lambda p, i: (0, p))
    (dk, dv, dq, dc_row, dc_col), carried = rider_call(
        body, name, (4, nq),
        in_specs=[tile, tile, pl.BlockSpec((2, 1, t), lambda p, i: (p, 0, i)),
                  full, full, full, pl.BlockSpec((1, m, LANES), lambda p, i: (p, 0, 0)),
                  pl.BlockSpec((m, LANES), lambda p, i: (0, 0))],
        out_specs=[tile, tile, full, pl.BlockSpec((2, 1, t), lambda p, i: (p, 0, i)),
                   pl.BlockSpec((1, m, LANES), lambda p, i: (p, 0, 0))],
        out_shape=[jax.ShapeDtypeStruct((m, 512), F32), jax.ShapeDtypeStruct((m, 512), F32),
                   jax.ShapeDtypeStruct((m, 512), F32), jax.ShapeDtypeStruct((8, 1, m), F32),
                   jax.ShapeDtypeStruct((4, m, LANES), F32)],
        scratch_shapes=[pltpu.VMEM((t, LANES), F32), pltpu.VMEM((t, LANES), F32), pltpu.VMEM((2, 1, t), F32)],
        args=(k, v, c_row, q, do, o, lse, c), rider=rider)
    dc_query = jnp.stack([dc_col[:, :, 0], dc_col[:, :, HEAD_DIM]], axis=1).reshape(8, m)
    return (dq, dk, dv, dc_row.reshape(8, m) + dc_query), carried


def _bucket_ids():
    def bucket(dist):
        n = np.maximum(dist, 0)
        max_exact = N_BUCKETS // 2
        nf = np.maximum(n, 1).astype(np.float32)
        large = max_exact + (np.log(nf / max_exact) / math.log(MAX_DISTANCE / max_exact)
                             * (N_BUCKETS - max_exact)).astype(np.int32)
        return np.where(n < max_exact, n, np.minimum(large, N_BUCKETS - 1))

    tl = np.arange(LANES)[:, None]
    sl = np.arange(LANES)[None, :]
    prev = bucket(LANES + tl - sl)
    cur = bucket(tl - sl)
    meta = np.full((LANES, LANES), N_BUCKETS - 1)
    return np.concatenate([prev, cur, meta], axis=1).astype(np.int32)


def bias_build(table, name):
    ids = jnp.asarray(_bucket_ids())

    def body(t_ref, id_ref, o_ref):
        idv = id_ref[...]
        for h in range(8):
            acc = jnp.zeros((LANES, 3 * LANES), F32)
            for b in range(N_BUCKETS):
                acc = jnp.where(idv == b, t_ref[b, h], acc)
            o_ref[h] = acc

    return pl.pallas_call(
        body, name=name,
        in_specs=[pl.BlockSpec(memory_space=pltpu.SMEM), pl.BlockSpec(memory_space=pltpu.VMEM)],
        out_specs=pl.BlockSpec(memory_space=pltpu.VMEM),
        out_shape=jax.ShapeDtypeStruct((8, LANES, 3 * LANES), F32),
    )(table, ids)


def bias_reduce(dbias, name):
    ids = jnp.asarray(_bucket_ids())

    def body(d_ref, id_ref, o_ref):
        idv = id_ref[...]
        rr = lax.broadcasted_iota(jnp.int32, (N_BUCKETS, LANES), 0)
        cc = lax.broadcasted_iota(jnp.int32, (N_BUCKETS, LANES), 1)
        acc = jnp.zeros((N_BUCKETS, LANES), F32)
        for h in range(8):
            dv = d_ref[h]
            for b in range(N_BUCKETS):
                val = jnp.sum(jnp.where(idv == b, dv, 0.0), keepdims=True)
                acc = jnp.where((rr == b) & (cc == h), val, acc)
        o_ref[...] = acc

    return pl.pallas_call(
        body, name=name,
        in_specs=[pl.BlockSpec(memory_space=pltpu.VMEM), pl.BlockSpec(memory_space=pltpu.VMEM)],
        out_specs=pl.BlockSpec(memory_space=pltpu.VMEM),
        out_shape=jax.ShapeDtypeStruct((N_BUCKETS, LANES), F32),
    )(dbias, ids)


def _swa_valid(n):
    shape = (LANES, 3 * LANES)
    tl = lax.broadcasted_iota(jnp.int32, shape, 0)
    col = lax.broadcasted_iota(jnp.int32, shape, 1)
    sl = col & (LANES - 1)
    nv = jnp.full(shape, n, jnp.int32)
    is_meta = sl >= PAD_FRONT
    prev = (col < LANES) & (sl > tl) & (nv >= 1) & ((nv >= 2) | is_meta)
    cur = (col >= LANES) & (col < 2 * LANES) & (sl <= tl) & ((nv >= 1) | is_meta)
    meta = (col >= 2 * LANES) & is_meta & ((nv >= 2) | ((nv == 1) & (sl <= tl)))
    return prev | cur | meta


def _swa_keys(ref, n):
    off_prev = pl.multiple_of(jnp.maximum(n - 1, 0) * LANES, LANES)
    off_cur = pl.multiple_of(n * LANES, LANES)
    return jnp.concatenate([ref[pl.ds(off_prev, LANES), :], ref[pl.ds(off_cur, LANES), :], ref[0:LANES, :]], axis=0)


def swa_fwd(q, k, v, bias, sinks, name):
    m = q.shape[0]

    def body(q_ref, k_ref, v_ref, bias_ref, sink_ref, o_ref, lse_ref):
        n = pl.program_id(0)
        lane1 = lax.broadcasted_iota(jnp.int32, (1, LANES), 1)
        lane_t = lax.broadcasted_iota(jnp.int32, (LANES, LANES), 1)
        in_head = [lane1 < HEAD_DIM, lane1 >= HEAD_DIM]
        kall = _swa_keys(k_ref, n)
        vall = _swa_keys(v_ref, n)
        vs = [jnp.where(in_head[g], vall, jnp.zeros_like(vall)) for g in (0, 1)]
        valid = _swa_valid(n)
        lse = jnp.zeros((LANES, LANES), F32)
        for b in range(4):
            qb = q_ref[:, b * LANES:(b + 1) * LANES]
            ob = jnp.zeros((LANES, LANES), F32)
            for g in (0, 1):
                h = 4 * g + b
                qe = jnp.where(in_head[g], qb, jnp.zeros_like(qb))
                s = jnp.where(valid, _dot_nt(qe, kall) + bias_ref[h], NEG)
                sink = sink_ref[h]
                mx = jnp.maximum(jnp.max(s, axis=1, keepdims=True), sink)
                p = jnp.exp(s - mx)
                den = jnp.sum(p, axis=1, keepdims=True) + jnp.exp(sink - mx)
                ob = ob + _dot((p / den).astype(BF16), vs[g])
                lse = jnp.where(lane_t == h, mx + jnp.log(den), lse)
            o_ref[:, b * LANES:(b + 1) * LANES] = ob
        lse_ref[...] = lse

    return pl.pallas_call(
        body, name=name, grid=(m // LANES,),
        in_specs=[pl.BlockSpec((LANES, 512), lambda n: (n, 0)),
                  pl.BlockSpec((m, LANES), lambda n: (0, 0)), pl.BlockSpec((m, LANES), lambda n: (0, 0)),
                  pl.BlockSpec((8, LANES, 3 * LANES), lambda n: (0, 0, 0)),
                  pl.BlockSpec(memory_space=pltpu.SMEM)],
        out_specs=[pl.BlockSpec((LANES, 512), lambda n: (n, 0)), pl.BlockSpec((LANES, LANES), lambda n: (n, 0))],
        out_shape=[jax.ShapeDtypeStruct((m, 512), F32), jax.ShapeDtypeStruct((m, LANES), F32)],
        compiler_params=_params(1),
    )(q, k, v, bias, sinks)


def swa_bwd(q, k, v, bias, sinks, o, lse, do, name):
    m = q.shape[0]

    def body(q_ref, do_ref, o_ref, lse_ref, k_ref, v_ref, bias_ref, sink_ref,
             dq_ref, dk_ref, dv_ref, dbias_ref, dsink_ref):
        n = pl.program_id(0)

        @pl.when(n == 0)
        def _():
            for r in (dk_ref, dv_ref, dbias_ref, dsink_ref):
                r[...] = jnp.zeros_like(r)

        lane1 = lax.broadcasted_iota(jnp.int32, (1, LANES), 1)
        lane_t = lax.broadcasted_iota(jnp.int32, (LANES, LANES), 1)
        in_head = [lane1 < HEAD_DIM, lane1 >= HEAD_DIM]
        off_prev = pl.multiple_of(jnp.maximum(n - 1, 0) * LANES, LANES)
        off_cur = pl.multiple_of(n * LANES, LANES)
        kall = _swa_keys(k_ref, n)
        vall = _swa_keys(v_ref, n)
        ks = [jnp.where(in_head[g], kall, jnp.zeros_like(kall)) for g in (0, 1)]
        valid = _swa_valid(n)
        lsev = lse_ref[...]
        dsink = dsink_ref[...]
        dkall = jnp.zeros((3 * LANES, LANES), F32)
        dvall = jnp.zeros((3 * LANES, LANES), F32)
        for b in range(4):
            sl = slice(b * LANES, (b + 1) * LANES)
            qb = q_ref[:, sl]
            dob = do_ref[:, sl]
            prod = dob * o_ref[:, sl]
            dqb = jnp.zeros((LANES, LANES), F32)
            for g in (0, 1):
                h = 4 * g + b
                qe = jnp.where(in_head[g], qb, jnp.zeros_like(qb))
                doe = jnp.where(in_head[g], dob, 0.0).astype(BF16)
                delta = jnp.sum(jnp.where(in_head[g], prod, 0.0), axis=1, keepdims=True)
                lse_h = _lane_col(lsev, lane_t, h)
                s = jnp.where(valid, _dot_nt(qe, kall) + bias_ref[h], NEG)
                p = jnp.exp(s - lse_h)
                ds = p * (_dot_nt(doe, vall) - delta)
                dbias_ref[h] += ds
                sink_part = jnp.sum(-jnp.exp(sink_ref[h] - lse_h) * delta, keepdims=True)
                dsink = jnp.where(lane1 == h, dsink + sink_part, dsink)
                dsb = ds.astype(BF16)
                dqb = dqb + _dot(dsb, ks[g])
                dkall = dkall + _dot_tn(dsb, qe)
                dvall = dvall + _dot_tn(p.astype(BF16), doe)
            dq_ref[:, sl] = dqb
        dsink_ref[...] = dsink
        for ref, val in ((dk_ref, dkall), (dv_ref, dvall)):
            ref[pl.ds(off_prev, LANES), :] += val[0:LANES]
            ref[pl.ds(off_cur, LANES), :] += val[LANES:2 * LANES]
            ref[0:LANES, :] += val[2 * LANES:3 * LANES]

    blk = pl.BlockSpec((LANES, 512), lambda n: (n, 0))
    full = pl.BlockSpec((m, LANES), lambda n: (0, 0))
    return pl.pallas_call(
        body, name=name, grid=(m // LANES,),
        in_specs=[blk, blk, blk, pl.BlockSpec((LANES, LANES), lambda n: (n, 0)), full, full,
                  pl.BlockSpec((8, LANES, 3 * LANES), lambda n: (0, 0, 0)),
                  pl.BlockSpec(memory_space=pltpu.SMEM)],
        out_specs=[blk, full, full, pl.BlockSpec((8, LANES, 3 * LANES), lambda n: (0, 0, 0)),
                   pl.BlockSpec((1, LANES), lambda n: (0, 0))],
        out_shape=[jax.ShapeDtypeStruct((m, 512), F32), jax.ShapeDtypeStruct((m, LANES), F32),
                   jax.ShapeDtypeStruct((m, LANES), F32), jax.ShapeDtypeStruct((8, LANES, 3 * LANES), F32),
                   jax.ShapeDtypeStruct((1, LANES), F32)],
        compiler_params=_params(1),
    )(q, do, o, lse, k, v, bias, sinks)


def branch_out(h, o_fox, o_swa, proj, wbf, wbs, wo, cols, name):
    m, d = h.shape
    tm = _row_tile(m)

    def body(h_ref, of_ref, os_ref, ga_ref, gb_ref, wbf_ref, wbs_ref, wo_ref, hn_ref):
        tf = _dot(of_ref[...].astype(BF16), wbf_ref[...])
        ts = _dot(os_ref[...].astype(BF16), wbs_ref[...])
        y = jax.nn.sigmoid(ga_ref[...]) * tf + jax.nn.sigmoid(gb_ref[...]) * ts
        hn_ref[...] = h_ref[...] + _dot(y.astype(BF16), wo_ref[...])

    row = lambda w, o=0: pl.BlockSpec((tm, w), lambda r, o=o: (r, o))
    res = lambda a: pl.BlockSpec(a.shape, lambda r: (0, 0))
    return pl.pallas_call(
        body, name=name, grid=(m // tm,),
        in_specs=[row(d), row(512), row(512), row(d, cols.ga // d), row(d, cols.gb // d), res(wbf), res(wbs), res(wo)],
        out_specs=row(d),
        out_shape=jax.ShapeDtypeStruct((m, d), F32),
        compiler_params=_params(1),
    )(h, o_fox, o_swa, proj, proj, wbf, wbs, wo)


def branch_out_bwd(dh, o_fox, o_swa, proj, wbf, wbs, wo, cols, name):
    m, d = dh.shape
    tm = _row_tile(m)

    def body(dh_ref, of_ref, os_ref, ga_ref, gb_ref, wbf_ref, wbs_ref, wo_ref,
             y_ref, dtf_ref, dts_ref, dga_ref, dgb_ref, dof_ref, dos_ref):
        dy = _dot_nt(dh_ref[...].astype(BF16), wo_ref[...])
        tf = _dot(of_ref[...].astype(BF16), wbf_ref[...])
        ts = _dot(os_ref[...].astype(BF16), wbs_ref[...])
        sa = jax.nn.sigmoid(ga_ref[...])
        sb = jax.nn.sigmoid(gb_ref[...])
        y_ref[...] = (sa * tf + sb * ts).astype(BF16)
        dtf = (dy * sa).astype(BF16)
        dts = (dy * sb).astype(BF16)
        dtf_ref[...] = dtf
        dts_ref[...] = dts
        dga_ref[...] = (dy * tf * sa * (1.0 - sa)).astype(BF16)
        dgb_ref[...] = (dy * ts * sb * (1.0 - sb)).astype(BF16)
        dof_ref[...] = _dot_nt(dtf, wbf_ref[...])
        dos_ref[...] = _dot_nt(dts, wbs_ref[...])

    row = lambda w, o=0: pl.BlockSpec((tm, w), lambda r, o=o: (r, o))
    res = lambda a: pl.BlockSpec(a.shape, lambda r: (0, 0))
    return pl.pallas_call(
        body, name=name, grid=(m // tm,),
        in_specs=[row(d), row(512), row(512), row(d, cols.ga // d), row(d, cols.gb // d), res(wbf), res(wbs), res(wo)],
        out_specs=[row(d)] * 5 + [row(512)] * 2,
        out_shape=[jax.ShapeDtypeStruct((m, d), BF16)] * 5 + [jax.ShapeDtypeStruct((m, 512), F32)] * 2,
        compiler_params=_params(1),
    )(dh, o_fox, o_swa, proj, proj, wbf, wbs, wo)


def loss_head(h, target, name):
    m, d = h.shape

    def body(h_ref, t_ref, dh_ref, loss_ref):
        n = pl.program_id(0)

        @pl.when(n == 0)
        def _():
            loss_ref[...] = jnp.zeros_like(loss_ref)
            dh_ref[...] = jnp.zeros_like(dh_ref)

        @pl.when(n > 0)
        def _():
            err = h_ref[...] - t_ref[...]
            dh_ref[...] = err * (1.0 / d)
            loss_ref[...] += jnp.sum(err * err, keepdims=True) * (0.5 / d)

    return pl.pallas_call(
        body, name=name, grid=(m // LANES,),
        in_specs=[pl.BlockSpec((LANES, d), lambda n: (n, 0)),
                  pl.BlockSpec((LANES, d), lambda n: (jnp.maximum(n - 1, 0), 0))],
        out_specs=[pl.BlockSpec((LANES, d), lambda n: (n, 0)), pl.BlockSpec((8, LANES), lambda n: (0, 0))],
        out_shape=[jax.ShapeDtypeStruct((m, d), F32), jax.ShapeDtypeStruct((8, LANES), F32)],
        compiler_params=_params(1),
    )(h, target)


def _adamw_math(w, g, m, v):
    m = ADAM_B1 * m + (1.0 - ADAM_B1) * g
    v = ADAM_B2 * v + (1.0 - ADAM_B2) * (g * g)
    m_hat = m / (1.0 - ADAM_B1 ** ADAM_STEP)
    v_hat = v / (1.0 - ADAM_B2 ** ADAM_STEP)
    delta = -ADAM_LR * (m_hat / (jnp.sqrt(v_hat) + ADAM_EPS) + ADAM_WD * w)
    return delta, m, v


def adamw_sum(parts, w, m, v, name):
    n_layers, a, b = w.shape
    ta = next(t for t in (256, 176, 128, a) if a % t == 0)
    nr = a // ta

    def body(*refs):
        p_refs = refs[:n_layers]
        w_ref, m_ref, v_ref, g_o, d_o, m_o, v_o = refs[n_layers:]
        for l in range(n_layers):
            @pl.when(pl.program_id(0) == l)
            def _(l=l):
                g = p_refs[l][0].astype(F32)
                for j in range(1, N_DEV):
                    g = g + p_refs[l][j].astype(F32)
                g_o[0] = g
                d_o[0], m_o[0], v_o[0] = _adamw_math(w_ref[0], g, m_ref[0], v_ref[0])

    def part_spec(l):
        return pl.BlockSpec((N_DEV, ta, b), lambda i, r, l=l: (0, jnp.where(i == l, r, jnp.where(i < l, 0, nr - 1)), 0))

    row = pl.BlockSpec((1, ta, b), lambda i, r: (i, r, 0))
    return pl.pallas_call(
        body, name=name, grid=(n_layers, nr),
        in_specs=[part_spec(l) for l in range(n_layers)] + [row, row, row],
        out_specs=[row] * 4,
        out_shape=[jax.ShapeDtypeStruct(w.shape, F32)] * 4,
        compiler_params=_params(2),
    )(*parts, w, m, v)


def adamw_small(g, w, m, v, name):
    def body(g_ref, w_ref, m_ref, v_ref, d_o, m_o, v_o):
        d_o[...], m_o[...], v_o[...] = _adamw_math(w_ref[...], g_ref[...], m_ref[...], v_ref[...])

    spec = pl.BlockSpec(memory_space=pltpu.VMEM)
    return pl.pallas_call(
        body, name=name, in_specs=[spec] * 4, out_specs=[spec] * 3,
        out_shape=[jax.ShapeDtypeStruct(w.shape, F32)] * 3,
    )(g, w, m, v)


BIG = ("ffn1_w_in", "ffn1_w_out", "w_in", "w_branch_fox", "w_branch_swa", "w_out", "ffn2_w_in", "ffn2_w_out")
SMALL = ("rel_bias_table", "ffn1_norm", "mix_norm", "forget_bias", "fox_q_norm", "fox_k_norm",
         "swa_q_norm", "swa_k_norm", "swa_sinks", "ffn2_norm")
WEIGHTS = ("meta_tokens", "rel_bias_table", "ffn1_norm", "ffn1_w_in", "ffn1_w_out", "mix_norm", "w_in",
           "forget_bias", "fox_q_norm", "fox_k_norm", "swa_q_norm", "swa_k_norm", "swa_sinks", "w_branch_fox",
           "w_branch_swa", "w_out", "ffn2_norm", "ffn2_w_in", "ffn2_w_out")


def _pack(arrs, width, row_multiple, dtype):
    lead = arrs[0].shape[:-1]
    flat = jnp.concatenate([a.astype(dtype) for a in arrs], axis=-1)
    n = flat.shape[-1]
    rows = -(-n // width)
    rows = -(-rows // row_multiple) * row_multiple
    flat = jnp.pad(flat, [(0, 0)] * len(lead) + [(0, rows * width - n)])
    return flat.reshape(lead + (rows, width))


def _unpack(flat, shapes):
    flat = flat.reshape(-1)
    out, off = [], 0
    for s in shapes:
        n = int(np.prod(s))
        out.append(flat[off:off + n].reshape(s))
        off += n
    return out


def _swa_head_order():
    return [4 * (j % 2) + j // 2 for j in range(8)]


def _permute_heads(a, axis, inverse=False):
    order = _swa_head_order()
    if inverse:
        order = [order.index(hd) for hd in range(8)]
    parts = [lax.slice_in_dim(a, hd * HEAD_DIM, (hd + 1) * HEAD_DIM, axis=axis) for hd in order]
    return jnp.concatenate(parts, axis=axis)


def _w_in_segments(cols):
    d = cols.d
    segs = [(0, 512, cols.qa), (512, 512, cols.ka), (1024, 512, cols.va), (1536, 8, cols.fa)]
    order = _swa_head_order()
    segs += [(1544 + HEAD_DIM * hd, HEAD_DIM, cols.qb + HEAD_DIM * order.index(hd)) for hd in range(8)]
    segs += [(2056, 128, cols.kb), (2184, 128, cols.vb), (2312, d, cols.ga), (2312 + d, d, cols.gb)]
    return segs


def _reorder_w_in(blocks, cols):
    width = blocks[0].shape[1]
    parts = []
    for old, length, _ in sorted(_w_in_segments(cols), key=lambda s: s[2]):
        while length:
            j, off = divmod(old, width)
            take = min(length, width - off)
            parts.append(blocks[j][:, off:off + take])
            old, length = old + take, length - take
    parts.append(jnp.zeros((blocks[0].shape[0], LANES - 8), blocks[0].dtype))
    return jnp.concatenate(parts, axis=1)


def _restore_w_in(wp, cols, width):
    segs = sorted(_w_in_segments(cols))
    blocks = []
    for j in range(N_DEV):
        lo, hi = j * width, (j + 1) * width
        parts = []
        for old, length, new in segs:
            a, b = max(old, lo), min(old + length, hi)
            if a < b:
                parts.append(wp[:, new + a - old:new + b - old])
        blocks.append(jnp.concatenate(parts, axis=1))
    return jnp.stack(blocks)


def _lane_pad(v):
    return jnp.pad(v, ((0, 0), (0, LANES - v.shape[1])))


def kernel(x, meta_tokens, rel_bias_table, ffn1_norm, ffn1_w_in, ffn1_w_out, mix_norm, w_in, forget_bias, fox_q_norm, fox_k_norm, swa_q_norm, swa_k_norm, swa_sinks, w_branch_fox, w_branch_swa, w_out, ffn2_norm, ffn2_w_in, ffn2_w_out, loss_target, m_meta_tokens, m_rel_bias_table, m_ffn1_norm, m_ffn1_w_in, m_ffn1_w_out, m_mix_norm, m_w_in, m_forget_bias, m_fox_q_norm, m_fox_k_norm, m_swa_q_norm, m_swa_k_norm, m_swa_sinks, m_w_branch_fox, m_w_branch_swa, m_w_out, m_ffn2_norm, m_ffn2_w_in, m_ffn2_w_out, v_meta_tokens, v_rel_bias_table, v_ffn1_norm, v_ffn1_w_in, v_ffn1_w_out, v_mix_norm, v_w_in, v_forget_bias, v_fox_q_norm, v_fox_k_norm, v_swa_q_norm, v_swa_k_norm, v_swa_sinks, v_w_branch_fox, v_w_branch_swa, v_w_out, v_ffn2_norm, v_ffn2_w_in, v_ffn2_w_out):
    args = dict(locals())
    wts = {n: args[n] for n in WEIGHTS}
    mom1 = {n: args["m_" + n] for n in WEIGHTS}
    mom2 = {n: args["v_" + n] for n in WEIGHTS}

    seq, d = x.shape[1], x.shape[2]
    m_rows = seq + LANES
    depth = ffn1_norm.shape[0]
    fb = ffn1_w_in.shape[2]
    fo = ffn1_w_out.shape[1]
    din_shard = w_in.shape[2]
    cols = _Cols(d)
    scale = HEAD_DIM ** -0.5
    dev = 4 * lax.axis_index("x") + 2 * lax.axis_index("y") + lax.axis_index("c")

    groups = {"ffn1": ("ffn1_w_in", "ffn1_w_out"), "mix": ("w_in", "w_branch_fox", "w_branch_swa", "w_out"),
              "ffn2": ("ffn2_w_in", "ffn2_w_out")}
    shard = {n: wts[n].astype(BF16) for n in BIG}
    full, parts, gw = {}, {}, {}

    def keys_of(stages):
        return [(n, l) for g, l in stages if l < depth for n in groups[g]]

    def gather_rider(stages):
        return Rider([shard[n][l] for n, l in keys_of(stages)], True)

    def scatter_rider(stages):
        return Rider([gw[k] for k in keys_of(stages)], False)

    def ffn_weights(tag, l):
        return full[tag + "_w_in", l], full[tag + "_w_out", l].reshape(4, fb, d)

    def mixer_weights(l):
        wp = _reorder_w_in([full["w_in", l][j] for j in range(N_DEV)], cols)
        wbf = jnp.concatenate([full["w_branch_fox", l][j] for j in range(N_DEV)], axis=1)
        wbs = _permute_heads(jnp.concatenate([full["w_branch_swa", l][j] for j in range(N_DEV)], axis=1), 0)
        return wp, wbf, wbs, full["w_out", l].reshape(d, d)

    full.update(zip(keys_of([("ffn1", 0)]), exchange_hbm(gather_rider([("ffn1", 0)]).srcs, True, "gather_first")))
    meta_all = gather_small(meta_tokens.reshape(1, N_META, -1), "gather_meta")
    meta_full = meta_all.transpose(1, 0, 2).reshape(N_META, d)
    tile8 = lambda g, s=1.0: jnp.tile(g.reshape(1, HEAD_DIM) * s, (1, 8))
    tile2 = lambda g: jnp.tile(g.reshape(1, HEAD_DIM), (1, 2))
    bias = bias_build(rel_bias_table, "swa_bias")

    first = jnp.concatenate([jnp.zeros((PAD_FRONT, d), F32), meta_full], axis=0)
    h = jnp.concatenate([first, x[0]], axis=0)
    saved, lw = [], []
    for l in range(depth):
        s, w = {"h0": h}, {}
        w["ffn1_in"], w["ffn1_out"] = ffn_weights("ffn1", l)
        stages = [("mix", 0)] if l == 0 else []
        (h, s["n1"], s["gate1"], s["up1"]), got = ffn_fwd(h, ffn1_norm[l:l + 1], w["ffn1_in"], w["ffn1_out"],
                                                          f"ffn1_fwd_{l}", gather_rider(stages))
        full.update(zip(keys_of(stages), got))
        s["h1"] = h
        w["wp"], w["wbf"], w["wbs"], w["wo"] = mixer_weights(l)
        s["nm"], s["proj"] = mixer_proj(h, mix_norm[l:l + 1], w["wp"], f"mixer_proj_{l}")
        s["gains"] = (tile8(fox_q_norm[l], scale), tile8(fox_k_norm[l]), tile8(swa_q_norm[l], scale),
                      tile2(swa_k_norm[l]))
        s["fbias"] = _lane_pad(forget_bias[l:l + 1])
        qa, ka, va, qb, kb, vb, c = qk_post(s["proj"], s["gains"], s["fbias"], cols, f"qk_post_{l}")
        c_row = c[:, :8].T.reshape(8, 1, m_rows)
        s.update(qa=qa, ka=ka, va=va, qb=qb, kb=kb, vb=vb, c=c, c_row=c_row)
        stages = [("ffn2", l), ("ffn1", l + 1)]
        (s["o_fox"], s["lse_fox"]), got = fox_fwd(qa, ka, va, c, c_row, f"fox_fwd_{l}", gather_rider(stages))
        full.update(zip(keys_of(stages), got))
        s["o_swa"], s["lse_swa"] = swa_fwd(qb, kb, vb, bias, swa_sinks[l], f"swa_fwd_{l}")
        h = branch_out(h, s["o_fox"], s["o_swa"], s["proj"], w["wbf"], w["wbs"], w["wo"], cols, f"branch_out_{l}")
        s["h2"] = h
        w["ffn2_in"], w["ffn2_out"] = ffn_weights("ffn2", l)
        stages = [("mix", l + 1)]
        (h, s["n2"], s["gate2"], s["up2"]), got = ffn_fwd(h, ffn2_norm[l:l + 1], w["ffn2_in"], w["ffn2_out"],
                                                          f"ffn2_fwd_{l}", gather_rider(stages))
        full.update(zip(keys_of(stages), got))
        saved.append(s)
        lw.append(w)

    dh, loss_part = loss_head(h, loss_target[0], "loss_head")

    gs = {n: [None] * depth for n in SMALL}
    dbias_total = None
    for l in reversed(range(depth)):
        w, s = lw[l], saved[l]

        def ffn_back(dh, tag, hin, norm, n_in, gate, up, stages):
            (dh_in, a, dg, du, dgn, dhs), got = ffn_bwd(dh, hin, norm, gate, up, w[tag + "_in"], w[tag + "_out"],
                                                        f"{tag}_bwd_{l}", scatter_rider(stages))
            parts.update(zip(keys_of(stages), got))
            gw[tag + "_w_in", l] = matmul_tn(n_in[None], dg, f"{tag}_dwi_{l}", y2=du)
            gw[tag + "_w_out", l] = matmul_tn(a, dhs[None], f"{tag}_dwo_{l}").reshape(N_DEV, fo, d)
            return dh_in, dgn

        dh, gs["ffn2_norm"][l] = ffn_back(dh, "ffn2", s["h2"], ffn2_norm[l:l + 1], s["n2"], s["gate2"], s["up2"],
                                          [("ffn1", l + 1)])

        y, dtf, dts, dga, dgb, dof, dos = branch_out_bwd(dh, s["o_fox"], s["o_swa"], s["proj"], w["wbf"], w["wbs"],
                                                         w["wo"], cols, f"branch_out_bwd_{l}")
        gw["w_out", l] = matmul_tn(y[None], dh[None], f"dw_out_{l}").reshape(N_DEV, d // N_DEV, d)
        to_shards = lambda a: a.reshape(512, N_DEV, d // N_DEV).transpose(1, 0, 2)
        gw["w_branch_fox", l] = to_shards(matmul_tn(s["o_fox"][None], dtf[None], f"dw_branch_fox_{l}")[0])
        gw["w_branch_swa", l] = to_shards(_permute_heads(
            matmul_tn(s["o_swa"][None], dts[None], f"dw_branch_swa_{l}")[0], 0, inverse=True))

        stages = [("ffn2", l)]
        (dqa, dka, dva, dc_row), got = fox_bwd(s["qa"], s["ka"], s["va"], s["c"], s["c_row"], s["o_fox"], s["lse_fox"],
                                               dof, f"fox_bwd_{l}", scatter_rider(stages))
        parts.update(zip(keys_of(stages), got))
        dqb, dkb, dvb, dbias, dsink = swa_bwd(s["qb"], s["kb"], s["vb"], bias, swa_sinks[l], s["o_swa"], s["lse_swa"],
                                              dos, f"swa_bwd_{l}")
        dbias_total = dbias if dbias_total is None else dbias_total + dbias
        gs["swa_sinks"][l] = dsink[0, :8]
        dc = _lane_pad(dc_row.T)
        dproj, ggqa, ggka, ggqb, ggkb, gfb = qk_post_bwd(s["proj"], s["gains"], s["fbias"], dqa, dka, dva, dqb, dkb,
                                                         dvb, dc, dga, dgb, cols, f"qk_post_bwd_{l}")
        gs["fox_q_norm"][l] = ggqa.reshape(8, HEAD_DIM).sum(0) * scale
        gs["fox_k_norm"][l] = ggka.reshape(8, HEAD_DIM).sum(0)
        gs["swa_q_norm"][l] = ggqb.reshape(8, HEAD_DIM).sum(0) * scale
        gs["swa_k_norm"][l] = ggkb.reshape(2, HEAD_DIM).sum(0)
        gs["forget_bias"][l] = gfb[0, :8]
        dwp = matmul_tn(s["nm"][None], dproj[None], f"dw_in_{l}", tn=cols.np // 5 if cols.np % 640 == 0 else None)[0]
        gw["w_in", l] = _restore_w_in(dwp, cols, din_shard)
        dh, gs["mix_norm"][l] = dproj_bwd(dh, s["h1"], mix_norm[l:l + 1], dproj, w["wp"], f"dproj_bwd_{l}")

        dh, gs["ffn1_norm"][l] = ffn_back(dh, "ffn1", s["h0"], ffn1_norm[l:l + 1], s["n1"], s["gate1"], s["up1"],
                                          [("mix", l)])

    grad_x = dh[LANES:][None]
    dmeta = dh[PAD_FRONT:LANES]
    dtable = bias_reduce(dbias_total, "swa_dbias")[:, :8]

    parts.update(zip(keys_of([("ffn1", 0)]), exchange_hbm(scatter_rider([("ffn1", 0)]).srcs, False, "scatter_last")))
    big_out = [{}, {}, {}, {}]
    for n in BIG:
        outs = adamw_sum([parts[n, l] for l in range(depth)], wts[n], mom1[n], mom2[n], f"adamw_{n}")
        for k in range(4):
            big_out[k][n] = outs[k]

    small_g = {n: (jnp.stack(gs[n]) if n != "rel_bias_table" else None) for n in SMALL}
    small_g["rel_bias_table"] = dtable
    pieces = [loss_part[0:1, 0:1].reshape(1, 1)] + [small_g[n].reshape(1, -1) for n in SMALL] + [dmeta.reshape(1, -1)]
    small_shapes = [(1,)] + [wts[n].shape for n in SMALL] + [(N_META, d)]
    total = allsum_small(_pack(pieces, LANES, 8, F32), "allsum_small")
    summed = _unpack(total, small_shapes)
    loss = summed[0][0]
    g_small = dict(zip(SMALL, summed[1:1 + len(SMALL)]))
    g_meta = lax.dynamic_slice_in_dim(summed[-1], dev * (d // N_DEV), d // N_DEV, axis=1)
    names = SMALL + ("meta_tokens",)
    g_small["meta_tokens"] = g_meta
    pk = lambda src: _pack([src[n].reshape(1, -1) for n in names], LANES, 8, F32)[0]
    small_out = [dict(zip(names, _unpack(o, [wts[n].shape for n in names])))
                 for o in adamw_small(pk(g_small), pk(wts), pk(mom1), pk(mom2), "adamw_small")]

    grads = {**big_out[0], **g_small}
    delta = {**big_out[1], **small_out[0]}
    new_m = {**big_out[2], **small_out[1]}
    new_v = {**big_out[3], **small_out[2]}
    return (loss, grad_x, *[grads[n] for n in WEIGHTS], *[delta[n] for n in WEIGHTS],
            *[new_m[n] for n in WEIGHTS], *[new_v[n] for n in WEIGHTS])
```

```python
import math

import numpy as np
import jax
import jax.numpy as jnp
from jax import lax
from jax.experimental import pallas as pl
from jax.experimental.pallas import tpu as pltpu

F32 = jnp.float32
BF16 = jnp.bfloat16
EPS = 1e-6
NEG = -1e30
HEAD_DIM = 64
LANES = 128
N_META = 16
PAD_FRONT = LANES - N_META
N_BUCKETS = 32
MAX_DISTANCE = 128
N_DEV = 8
ADAM_LR, ADAM_B1, ADAM_B2, ADAM_EPS, ADAM_WD, ADAM_STEP = 0.001, 0.9, 0.999, 1e-08, 0.01, 10
VMEM_LIMIT = 56 * 1024 * 1024
MESH = pl.DeviceIdType.MESH


def _params(n_grid):
    return pltpu.CompilerParams(dimension_semantics=("arbitrary",) * n_grid,
                                vmem_limit_bytes=VMEM_LIMIT)


def _dot(a, b):
    return jnp.dot(a, b, preferred_element_type=F32)


def _dot_nt(a, b):
    return lax.dot_general(a, b, (((1,), (1,)), ((), ())), preferred_element_type=F32)


def _dot_tn(a, b):
    return lax.dot_general(a, b, (((0,), (0,)), ((), ())), preferred_element_type=F32)


def _rms(x):
    r = lax.rsqrt(jnp.mean(x * x, axis=-1, keepdims=True) + EPS)
    return x * r, r


def _rms_bwd(x, g, dn):
    xh, r = _rms(x)
    dxh = dn * g
    dx = r * (dxh - xh * jnp.mean(dxh * xh, axis=-1, keepdims=True))
    return dx, jnp.sum(dn * xh, axis=0, keepdims=True)


def _split2(v):
    hi = v.astype(BF16)
    return hi, (v - hi.astype(F32)).astype(BF16)


def _split3(v):
    hi = v.astype(BF16)
    r1 = v - hi.astype(F32)
    mid = r1.astype(BF16)
    return hi, mid, (r1 - mid.astype(F32)).astype(BF16)


def _group_ones():
    r = lax.broadcasted_iota(jnp.int32, (LANES, LANES), 0) // HEAD_DIM
    c = lax.broadcasted_iota(jnp.int32, (LANES, LANES), 1) // HEAD_DIM
    return jnp.where(r == c, 1.0, 0.0).astype(BF16)


def _group_mean(v, ones):
    hi, lo = _split2(v)
    return (_dot(hi, ones) + _dot(lo, ones)) * (1.0 / HEAD_DIM)


def _row_tile(m):
    return 384 if m % 384 == 0 else LANES


def _peer(k):
    x, y, c = lax.axis_index("x"), lax.axis_index("y"), lax.axis_index("c")
    px = 1 - x if k & 4 else x
    py = 1 - y if k & 2 else y
    pc = 1 - c if k & 1 else c
    return (px, py, pc), 4 * px + 2 * py + pc


def _exchange_body(src_ref, dst_ref, send_sems, recv_sems, local_sem, bcast):
    x, y, c = lax.axis_index("x"), lax.axis_index("y"), lax.axis_index("c")
    me = 4 * x + 2 * y + c
    mine = pltpu.make_async_copy(src_ref.at[0 if bcast else me], dst_ref.at[me], local_sem)
    mine.start()
    sends = []
    for k in range(1, N_DEV):
        dev, idx = _peer(k)
        cp = pltpu.make_async_remote_copy(
            src_ref=src_ref.at[0 if bcast else idx], dst_ref=dst_ref.at[me],
            send_sem=send_sems.at[k - 1], recv_sem=recv_sems.at[k - 1],
            device_id=dev, device_id_type=MESH)
        cp.start()
        sends.append(cp)
    for k in range(1, N_DEV):
        dev, idx = _peer(k)
        pltpu.make_async_remote_copy(
            src_ref=src_ref.at[0], dst_ref=dst_ref.at[idx],
            send_sem=send_sems.at[k - 1], recv_sem=recv_sems.at[k - 1],
            device_id=dev, device_id_type=MESH).wait_recv()
    for cp in sends:
        cp.wait_send()
    mine.wait()


class Rider:
    def __init__(self, srcs=(), bcast=True):
        self.srcs, self.bcast, self.n = list(srcs), bcast, len(srcs)

    def out_shapes(self):
        return [jax.ShapeDtypeStruct(((N_DEV,) + s.shape) if self.bcast else s.shape, s.dtype) for s in self.srcs]

    def specs(self):
        return [pl.BlockSpec(memory_space=pl.ANY)] * self.n

    def scratch(self):
        if not self.n:
            return []
        return [pltpu.SemaphoreType.DMA((self.n * (N_DEV - 1),)), pltpu.SemaphoreType.DMA((self.n * (N_DEV - 1),)),
                pltpu.SemaphoreType.DMA((self.n,))]

    def _copies(self, src_refs, dst_refs, send_sems, recv_sems, local_sems, arrivals):
        x, y, c = lax.axis_index("x"), lax.axis_index("y"), lax.axis_index("c")
        me = 4 * x + 2 * y + c
        own, sends, recvs = [], [], []
        for a, (s, d) in enumerate(zip(src_refs, dst_refs)):
            own.append(pltpu.make_async_copy(s if self.bcast else s.at[me], d.at[me], local_sems.at[a]))
            for k in range(1, N_DEV):
                dev, idx = _peer(k)
                sem = a * (N_DEV - 1) + k - 1
                sends.append(pltpu.make_async_remote_copy(
                    src_ref=s if self.bcast else s.at[idx], dst_ref=d.at[me],
                    send_sem=send_sems.at[sem], recv_sem=recv_sems.at[sem], device_id=dev, device_id_type=MESH))
                if arrivals:
                    recvs.append(pltpu.make_async_remote_copy(
                        src_ref=d.at[idx], dst_ref=d.at[idx],
                        send_sem=send_sems.at[sem], recv_sem=recv_sems.at[sem], device_id=dev, device_id_type=MESH))
        return own, sends, recvs

    def start(self, *refs):
        own, sends, _ = self._copies(*refs, arrivals=False)
        for cp in own + sends:
            cp.start()

    def wait(self, *refs):
        own, sends, recvs = self._copies(*refs, arrivals=True)
        for cp in recvs:
            cp.wait_recv()
        for cp in sends:
            cp.wait_send()
        for cp in own:
            cp.wait()


def rider_call(core, name, grid, in_specs, out_specs, out_shape, scratch_shapes, args, rider=None):
    rider = rider or Rider()
    n_in, n_out, n_scr, nr = len(in_specs), len(out_specs), len(scratch_shapes), rider.n

    def body(*refs):
        ins, r_src = refs[:n_in], refs[n_in:n_in + nr]
        outs = refs[n_in + nr:n_in + nr + n_out]
        r_dst = refs[n_in + nr + n_out:n_in + 2 * nr + n_out]
        scr = refs[n_in + 2 * nr + n_out:n_in + 2 * nr + n_out + n_scr]
        sems = refs[n_in + 2 * nr + n_out + n_scr:]
        if nr:
            first, last = True, True
            for ax, size in enumerate(grid):
                first = first & (pl.program_id(ax) == 0)
                last = last & (pl.program_id(ax) == size - 1)
            if not grid:
                rider.start(r_src, r_dst, *sems)
            else:
                pl.when(first)(lambda: rider.start(r_src, r_dst, *sems))
        core(*ins, *outs, *scr)
        if nr:
            if not grid:
                rider.wait(r_src, r_dst, *sems)
            else:
                pl.when(last)(lambda: rider.wait(r_src, r_dst, *sems))

    res = pl.pallas_call(
        body, name=name, grid=grid,
        in_specs=list(in_specs) + rider.specs(),
        out_specs=list(out_specs) + rider.specs(),
        out_shape=list(out_shape) + rider.out_shapes(),
        scratch_shapes=list(scratch_shapes) + rider.scratch(),
        compiler_params=_params(len(grid)),
    )(*args, *rider.srcs)
    return res[:n_out], res[n_out:]


def exchange_hbm(srcs, bcast, name):
    return rider_call(lambda: None, name, (), [], [], [], [], [], Rider(srcs, bcast))[1]


def allsum_small(vec, name):
    def body(src_ref, out_ref, dst_ref, send_sems, recv_sems, local_sem):
        _exchange_body(src_ref, dst_ref, send_sems, recv_sems, local_sem, True)
        acc = dst_ref[0]
        for j in range(1, N_DEV):
            acc = acc + dst_ref[j]
        out_ref[...] = acc

    return pl.pallas_call(
        body, name=name,
        out_shape=jax.ShapeDtypeStruct(vec.shape[1:], F32),
        in_specs=[pl.BlockSpec(memory_space=pltpu.VMEM)],
        out_specs=pl.BlockSpec(memory_space=pltpu.VMEM),
        scratch_shapes=[pltpu.VMEM((N_DEV,) + vec.shape[1:], F32),
                        pltpu.SemaphoreType.DMA((N_DEV - 1,)), pltpu.SemaphoreType.DMA((N_DEV - 1,)),
                        pltpu.SemaphoreType.DMA],
    )(vec)


def gather_small(vec, name):
    def body(src_ref, dst_ref, send_sems, recv_sems, local_sem):
        _exchange_body(src_ref, dst_ref, send_sems, recv_sems, local_sem, True)

    return pl.pallas_call(
        body, name=name,
        out_shape=jax.ShapeDtypeStruct((N_DEV,) + vec.shape[1:], F32),
        in_specs=[pl.BlockSpec(memory_space=pltpu.VMEM)],
        out_specs=pl.BlockSpec(memory_space=pltpu.VMEM),
        scratch_shapes=[pltpu.SemaphoreType.DMA((N_DEV - 1,)), pltpu.SemaphoreType.DMA((N_DEV - 1,)),
                        pltpu.SemaphoreType.DMA],
    )(vec)


def ffn_fwd(h, g, w_in8, w_out4, name, rider=None):
    m, d = h.shape
    fb = w_in8.shape[2]
    tm = _row_tile(m)

    def body(h_ref, g_ref, wg_ref, wu_ref, wo_ref, hn_ref, n_ref, gate_ref, up_ref, acc_ref):
        i = pl.program_id(1)

        @pl.when(i == 0)
        def _():
            xh, _ = _rms(h_ref[...])
            n_ref[...] = (xh * g_ref[...]).astype(BF16)
            acc_ref[...] = jnp.zeros_like(acc_ref)

        n = n_ref[...]
        gate = _dot(n, wg_ref[0])
        up = _dot(n, wu_ref[0])
        gate_ref[0] = gate
        up_ref[0] = up
        a = (gate * jax.nn.sigmoid(gate) * up).astype(BF16)
        acc_ref[...] += _dot(a, wo_ref[0])

        @pl.when(i == 3)
        def _():
            hn_ref[...] = h_ref[...] + 0.5 * acc_ref[...]

    return rider_call(
        body, name, (m // tm, 4),
        in_specs=[pl.BlockSpec((tm, d), lambda r, i: (r, 0)),
                  pl.BlockSpec((1, d), lambda r, i: (0, 0)),
                  pl.BlockSpec((1, d, fb), lambda r, i: (i, 0, 0)),
                  pl.BlockSpec((1, d, fb), lambda r, i: (i + 4, 0, 0)),
                  pl.BlockSpec((1, fb, d), lambda r, i: (i, 0, 0))],
        out_specs=[pl.BlockSpec((tm, d), lambda r, i: (r, 0)),
                   pl.BlockSpec((tm, d), lambda r, i: (r, 0)),
                   pl.BlockSpec((1, tm, fb), lambda r, i: (i, r, 0)),
                   pl.BlockSpec((1, tm, fb), lambda r, i: (i, r, 0))],
        out_shape=[jax.ShapeDtypeStruct((m, d), F32), jax.ShapeDtypeStruct((m, d), BF16),
                   jax.ShapeDtypeStruct((4, m, fb), F32), jax.ShapeDtypeStruct((4, m, fb), F32)],
        scratch_shapes=[pltpu.VMEM((tm, d), F32)],
        args=(h, g, w_in8, w_in8, w_out4), rider=rider)


def ffn_bwd(dh, h, g, gate, up, w_in8, w_out4, name, rider=None):
    m, d = h.shape
    fb = w_in8.shape[2]
    tm = _row_tile(m)

    def body(dh_ref, h_ref, g_ref, gate_ref, up_ref, wg_ref, wu_ref, wo_ref,
             dhin_ref, a_ref, dg_ref, du_ref, dgn_ref, dhs_ref, acc_ref):
        r = pl.program_id(0)
        i = pl.program_id(1)

        @pl.when(i == 0)
        def _():
            dhs_ref[...] = (0.5 * dh_ref[...]).astype(BF16)
            acc_ref[...] = jnp.zeros_like(acc_ref)

        @pl.when((r == 0) & (i == 0))
        def _():
            dgn_ref[...] = jnp.zeros_like(dgn_ref)

        da = _dot_nt(dhs_ref[...], wo_ref[0])
        gt = gate_ref[0]
        u = up_ref[0]
        sg = jax.nn.sigmoid(gt)
        sl = gt * sg
        a_ref[0] = (sl * u).astype(BF16)
        dub = (da * sl).astype(BF16)
        dgb = (da * u * (sg * (1.0 + gt * (1.0 - sg)))).astype(BF16)
        dg_ref[0] = dgb
        du_ref[0] = dub
        acc_ref[...] += _dot_nt(dgb, wg_ref[0]) + _dot_nt(dub, wu_ref[0])

        @pl.when(i == 3)
        def _():
            dx, dgain = _rms_bwd(h_ref[...], g_ref[...], acc_ref[...])
            dgn_ref[...] += dgain
            dhin_ref[...] = dh_ref[...] + dx

    row = lambda r, i: (r, 0)
    blk = lambda r, i: (i, r, 0)
    return rider_call(
        body, name, (m // tm, 4),
        in_specs=[pl.BlockSpec((tm, d), row), pl.BlockSpec((tm, d), row),
                  pl.BlockSpec((1, d), lambda r, i: (0, 0)),
                  pl.BlockSpec((1, tm, fb), blk), pl.BlockSpec((1, tm, fb), blk),
                  pl.BlockSpec((1, d, fb), lambda r, i: (i, 0, 0)),
                  pl.BlockSpec((1, d, fb), lambda r, i: (i + 4, 0, 0)),
                  pl.BlockSpec((1, fb, d), lambda r, i: (i, 0, 0))],
        out_specs=[pl.BlockSpec((tm, d), row),
                   pl.BlockSpec((1, tm, fb), blk), pl.BlockSpec((1, tm, fb), blk),
                   pl.BlockSpec((1, tm, fb), blk),
                   pl.BlockSpec((1, d), lambda r, i: (0, 0)),
                   pl.BlockSpec((tm, d), row)],
        out_shape=[jax.ShapeDtypeStruct((m, d), F32),
                   jax.ShapeDtypeStruct((4, m, fb), BF16), jax.ShapeDtypeStruct((4, m, fb), BF16),
                   jax.ShapeDtypeStruct((4, m, fb), BF16),
                   jax.ShapeDtypeStruct((1, d), F32), jax.ShapeDtypeStruct((m, d), BF16)],
        scratch_shapes=[pltpu.VMEM((tm, d), F32)],
        args=(dh, h, g, gate, up, w_in8, w_in8, w_out4), rider=rider)


def matmul_tn(x, y, name, tn=None, y2=None):
    bx, m, k = x.shape
    by, _, n = y.shape
    b = max(bx, by) * (2 if y2 is not None else 1)
    tm = _row_tile(m)
    tn = n if tn is None else tn
    nt = n // tn
    nr = m // tm

    def body(*refs):
        x_ref, y_ref = refs[0], refs[1]
        o_ref, acc_ref = refs[-2], refs[-1]
        r = pl.program_id(2)

        @pl.when(r == 0)
        def _():
            acc_ref[...] = jnp.zeros_like(acc_ref)

        if y2 is None:
            acc_ref[...] += _dot_tn(x_ref[0].astype(BF16), y_ref[0].astype(BF16))
        else:
            @pl.when(pl.program_id(0) < by)
            def _():
                acc_ref[...] += _dot_tn(x_ref[0].astype(BF16), y_ref[0].astype(BF16))

            @pl.when(pl.program_id(0) >= by)
            def _():
                acc_ref[...] += _dot_tn(x_ref[0].astype(BF16), refs[2][0].astype(BF16))

        @pl.when(r == nr - 1)
        def _():
            o_ref[0] = acc_ref[...].astype(BF16)

    x_map = (lambda i, j, r: (i, r, 0)) if bx > 1 else (lambda i, j, r: (0, r, 0))
    if y2 is None:
        y_specs = [pl.BlockSpec((1, tm, tn), (lambda i, j, r: (i, r, j)) if by > 1 else (lambda i, j, r: (0, r, j)))]
    else:
        y_specs = [pl.BlockSpec((1, tm, tn), lambda i, j, r: (jnp.minimum(i, by - 1), jnp.where(i < by, r, nr - 1), j)),
                   pl.BlockSpec((1, tm, tn), lambda i, j, r: (jnp.maximum(i - by, 0), jnp.where(i < by, 0, r), j))]
    return pl.pallas_call(
        body, name=name, grid=(b, nt, nr),
        in_specs=[pl.BlockSpec((1, tm, k), x_map)] + y_specs,
        out_specs=pl.BlockSpec((1, k, tn), lambda i, j, r: (i, 0, j)),
        out_shape=jax.ShapeDtypeStruct((b, k, n), BF16),
        scratch_shapes=[pltpu.VMEM((k, tn), F32)],
        compiler_params=_params(3),
    )(*([x, y] + ([y2] if y2 is not None else [])))


AUG = HEAD_DIM


class _Cols:
    def __init__(self, d):
        self.d = d
        self.ga, self.gb = 0, d
        self.qa, self.ka, self.va = 2 * d, 2 * d + 1024, 2 * d + 2048
        self.qb = 2 * d + 3072
        self.kb, self.vb, self.fa = self.qb + 512, self.qb + 640, self.qb + 768
        self.np = self.qb + 1024


def mixer_proj(h, g, wp, name):
    m, d = h.shape
    npad = wp.shape[1]
    tm = _row_tile(m)

    def body(h_ref, g_ref, w_ref, n_ref, p_ref):
        xh, _ = _rms(h_ref[...])
        n = (xh * g_ref[...]).astype(BF16)
        n_ref[...] = n
        p_ref[...] = _dot(n, w_ref[...])

    return pl.pallas_call(
        body, name=name, grid=(m // tm,),
        in_specs=[pl.BlockSpec((tm, d), lambda r: (r, 0)), pl.BlockSpec((1, d), lambda r: (0, 0)),
                  pl.BlockSpec((d, npad), lambda r: (0, 0))],
        out_specs=[pl.BlockSpec((tm, d), lambda r: (r, 0)), pl.BlockSpec((tm, npad), lambda r: (r, 0))],
        out_shape=[jax.ShapeDtypeStruct((m, d), BF16), jax.ShapeDtypeStruct((m, npad), F32)],
        compiler_params=_params(1),
    )(h, g, wp)


def _head_norm(x, gain, ones):
    outs = []
    for b in range(x.shape[1] // LANES):
        xb = x[:, b * LANES:(b + 1) * LANES]
        r = lax.rsqrt(_group_mean(xb * xb, ones) + EPS)
        outs.append(xb * r * gain[:, b * LANES:(b + 1) * LANES])
    return outs


def _head_norm_bwd(x, gain, dn, ones):
    dxs, dgs = [], []
    for b in range(x.shape[1] // LANES):
        sl = slice(b * LANES, (b + 1) * LANES)
        xb, dnb = x[:, sl], dn[:, sl]
        r = lax.rsqrt(_group_mean(xb * xb, ones) + EPS)
        xh = xb * r
        dxh = dnb * gain[:, sl]
        dxs.append(r * (dxh - xh * _group_mean(dxh * xh, ones)))
        dgs.append(jnp.sum(dnb * xh, axis=0, keepdims=True))
    return dxs, dgs


def _lane_col(v, lane_iota, idx):
    return jnp.sum(jnp.where(lane_iota == idx, v, 0.0), axis=1, keepdims=True)


def _aug(base, lane, vals):
    for i, v in enumerate(vals):
        base = jnp.where(lane == AUG + i, v, base)
    return base


def qk_post(proj, gains, fbias, cols, name):
    m = proj.shape[0]
    tm = _row_tile(m)
    gqa, gka, gqb, gkb = gains

    def body(qa_ref, ka_ref, va_ref, qb_ref, kb_ref, vb_ref, fa_ref, gqa_ref, gka_ref, gqb_ref, gkb_ref, fb_ref,
             qf_o, kf_o, vf_o, qb_o, kb_o, vb_o, carry_ref):
        r0 = pl.program_id(0)

        @pl.when(r0 == 0)
        def _():
            carry_ref[...] = jnp.zeros_like(carry_ref)

        z = fa_ref[...] + fb_ref[...]
        logf = jnp.minimum(z, 0.0) - jnp.log(1.0 + jnp.exp(-jnp.abs(z)))
        rr = lax.broadcasted_iota(jnp.int32, (tm, tm), 0)
        cc = lax.broadcasted_iota(jnp.int32, (tm, tm), 1)
        tril = jnp.where(cc <= rr, 1.0, 0.0).astype(BF16)
        p0, p1, p2 = _split3(logf)
        c = _dot(tril, p0) + _dot(tril, p1) + _dot(tril, p2) + carry_ref[...]
        carry_ref[...] += jnp.sum(logf, axis=0, keepdims=True)

        lane = lax.broadcasted_iota(jnp.int32, (tm, LANES), 1)
        is_pad = (r0 * tm + lax.broadcasted_iota(jnp.int32, (tm, 1), 0)) < PAD_FRONT
        ones = jnp.ones((LANES, LANES), BF16)
        for hd in range(8):
            sl = slice(hd * LANES, (hd + 1) * LANES)
            ch = _lane_col(c, lane, hd)
            ct = [p.astype(F32) for p in _split3(ch)]
            cs = [p.astype(F32) for p in _split3(-jnp.where(is_pad, -NEG, ch))]
            xq = qa_ref[:, sl]
            qn = xq * lax.rsqrt(_group_mean(xq * xq, ones) + EPS) * gqa_ref[...]
            qf_o[:, sl] = _aug(qn, lane, ct + [1.0, 1.0, 1.0]).astype(BF16)
            xk = ka_ref[:, sl]
            kn = xk * lax.rsqrt(_group_mean(xk * xk, ones) + EPS) * gka_ref[...]
            kf_o[:, sl] = _aug(kn, lane, [1.0, 1.0, 1.0] + cs).astype(BF16)
            vf_o[:, sl] = _aug(va_ref[:, sl], lane, [1.0, 1.0, 1.0]).astype(BF16)

        gones = _group_ones()
        for src, gn, dst in ((qb_ref, gqb_ref, qb_o), (kb_ref, gkb_ref, kb_o)):
            for b, blk in enumerate(_head_norm(src[...], gn[...], gones)):
                dst[:, b * LANES:(b + 1) * LANES] = blk.astype(BF16)
        vb_o[...] = vb_ref[...].astype(BF16)

    w1024 = lambda off: pl.BlockSpec((tm, 1024), lambda r, o=off // 1024: (r, o))
    w512 = lambda off: pl.BlockSpec((tm, 512), lambda r, o=off // 512: (r, o))
    w128 = lambda off: pl.BlockSpec((tm, LANES), lambda r, o=off // LANES: (r, o))
    vec = lambda w: pl.BlockSpec((1, w), lambda r: (0, 0))
    row = lambda w: pl.BlockSpec((tm, w), lambda r: (r, 0))
    return pl.pallas_call(
        body, name=name, grid=(m // tm,),
        in_specs=[w1024(cols.qa), w1024(cols.ka), w1024(cols.va), w512(cols.qb), w128(cols.kb), w128(cols.vb),
                  w128(cols.fa), vec(LANES), vec(LANES), vec(512), vec(LANES), vec(LANES)],
        out_specs=[row(1024), row(1024), row(1024), row(512), row(LANES), row(LANES)],
        out_shape=[jax.ShapeDtypeStruct((m, 1024), BF16)] * 3 + [jax.ShapeDtypeStruct((m, 512), BF16)]
                  + [jax.ShapeDtypeStruct((m, LANES), BF16)] * 2,
        scratch_shapes=[pltpu.VMEM((1, LANES), F32)],
        compiler_params=_params(1),
    )(proj, proj, proj, proj, proj, proj, proj, gqa, gka, gqb, gkb, fbias)


def qk_post_bwd(proj, gains, fbias, dqf, dkf, dvf, dqb, dkb, dvb, dc, dga, dgb, cols, name):
    m = proj.shape[0]
    d = cols.d
    tm = _row_tile(m)
    nt = m // tm
    gqa, gka, gqb, gkb = gains

    def body(qa_ref, ka_ref, qb_ref, kb_ref, fa_ref, gqa_ref, gka_ref, gqb_ref, gkb_ref, fb_ref,
             dqf_ref, dkf_ref, dvf_ref, dqb_ref, dkb_ref, dvb_ref, dc_ref, dga_ref, dgb_ref,
             dp_o, ggqa_o, ggka_o, ggqb_o, ggkb_o, gfb_o, carry_ref):
        @pl.when(pl.program_id(0) == 0)
        def _():
            carry_ref[...] = jnp.zeros_like(carry_ref)
            for o in (ggqa_o, ggka_o, ggqb_o, ggkb_o, gfb_o):
                o[...] = jnp.zeros_like(o)

        dp_o[:, cols.ga:cols.ga + d] = dga_ref[...].astype(BF16)
        dp_o[:, cols.gb:cols.gb + d] = dgb_ref[...].astype(BF16)
        dp_o[:, cols.fa + LANES:cols.np] = jnp.zeros((tm, cols.np - cols.fa - LANES), BF16)
        lane = lax.broadcasted_iota(jnp.int32, (tm, LANES), 1)
        data = lane < HEAD_DIM
        ones = jnp.ones((LANES, LANES), BF16)
        for hd in range(8):
            sl = slice(hd * LANES, (hd + 1) * LANES)
            for src, gn, dn_ref, off, gout in ((qa_ref, gqa_ref, dqf_ref, cols.qa, ggqa_o),
                                               (ka_ref, gka_ref, dkf_ref, cols.ka, ggka_o)):
                x = src[:, sl]
                dn = jnp.where(data, dn_ref[:, sl], 0.0)
                r = lax.rsqrt(_group_mean(x * x, ones) + EPS)
                xh = x * r
                dxh = dn * gn[...]
                dp_o[:, off + hd * LANES:off + (hd + 1) * LANES] = (
                    r * (dxh - xh * _group_mean(dxh * xh, ones))).astype(BF16)
                gout[...] += jnp.sum(dn * xh, axis=0, keepdims=True)
            dp_o[:, cols.va + hd * LANES:cols.va + (hd + 1) * LANES] = jnp.where(data, dvf_ref[:, sl], 0.0).astype(BF16)
        dp_o[:, cols.vb:cols.vb + LANES] = dvb_ref[...].astype(BF16)
        gones = _group_ones()
        for src, gn, dn, off, gout in ((qb_ref, gqb_ref, dqb_ref, cols.qb, ggqb_o),
                                       (kb_ref, gkb_ref, dkb_ref, cols.kb, ggkb_o)):
            dxs, dgs = _head_norm_bwd(src[...], gn[...], dn[...], gones)
            for b, (dx, dg) in enumerate(zip(dxs, dgs)):
                dp_o[:, off + b * LANES:off + (b + 1) * LANES] = dx.astype(BF16)
                gout[:, b * LANES:(b + 1) * LANES] += dg
        dcv = dc_ref[...]
        rr = lax.broadcasted_iota(jnp.int32, (tm, tm), 0)
        cc = lax.broadcasted_iota(jnp.int32, (tm, tm), 1)
        triu = jnp.where(cc >= rr, 1.0, 0.0).astype(BF16)
        p0, p1, p2 = _split3(dcv)
        dlogf = _dot(triu, p0) + _dot(triu, p1) + _dot(triu, p2) + carry_ref[...]
        carry_ref[...] += jnp.sum(dcv, axis=0, keepdims=True)
        z = fa_ref[...] + fb_ref[...]
        row = (nt - 1 - pl.program_id(0)) * tm + lax.broadcasted_iota(jnp.int32, (tm, LANES), 0)
        dfa = jnp.where(row >= PAD_FRONT, dlogf * jax.nn.sigmoid(-z), 0.0)
        dp_o[:, cols.fa:cols.fa + LANES] = dfa.astype(BF16)
        gfb_o[...] += jnp.sum(dfa, axis=0, keepdims=True)

    rev = lambda r: nt - 1 - r
    w1024 = lambda off: pl.BlockSpec((tm, 1024), lambda r, o=off // 1024: (rev(r), o))
    w512 = lambda off: pl.BlockSpec((tm, 512), lambda r, o=off // 512: (rev(r), o))
    w128 = lambda off: pl.BlockSpec((tm, LANES), lambda r, o=off // LANES: (rev(r), o))
    vec = lambda w: pl.BlockSpec((1, w), lambda r: (0, 0))
    row = lambda w: pl.BlockSpec((tm, w), lambda r: (rev(r), 0))
    return pl.pallas_call(
        body, name=name, grid=(nt,),
        in_specs=[w1024(cols.qa), w1024(cols.ka), w512(cols.qb), w128(cols.kb), w128(cols.fa),
                  vec(LANES), vec(LANES), vec(512), vec(LANES), vec(LANES),
                  row(1024), row(1024), row(1024), row(512), row(LANES), row(LANES), row(LANES), row(d), row(d)],
        out_specs=[row(cols.np), vec(LANES), vec(LANES), vec(512), vec(LANES), vec(LANES)],
        out_shape=[jax.ShapeDtypeStruct((m, cols.np), BF16), jax.ShapeDtypeStruct((1, LANES), F32),
                   jax.ShapeDtypeStruct((1, LANES), F32), jax.ShapeDtypeStruct((1, 512), F32),
                   jax.ShapeDtypeStruct((1, LANES), F32), jax.ShapeDtypeStruct((1, LANES), F32)],
        scratch_shapes=[pltpu.VMEM((1, LANES), F32)],
        compiler_params=_params(1),
    )(proj, proj, proj, proj, proj, gqa, gka, gqb, gkb, fbias, dqf, dkf, dvf, dqb, dkb, dvb, dc, dga, dgb)


def dproj_bwd(dh, h, g, dproj, wp, name):
    m, d = h.shape
    npad = wp.shape[1]
    tm = _row_tile(m)

    def body(dh_ref, h_ref, g_ref, dp_ref, w_ref, dhin_ref, dgn_ref):
        @pl.when(pl.program_id(0) == 0)
        def _():
            dgn_ref[...] = jnp.zeros_like(dgn_ref)

        dn = _dot_nt(dp_ref[...], w_ref[...])
        dx, dgain = _rms_bwd(h_ref[...], g_ref[...], dn)
        dgn_ref[...] += dgain
        dhin_ref[...] = dh_ref[...] + dx

    row = lambda w: pl.BlockSpec((tm, w), lambda r: (r, 0))
    return pl.pallas_call(
        body, name=name, grid=(m // tm,),
        in_specs=[row(d), row(d), pl.BlockSpec((1, d), lambda r: (0, 0)), row(npad),
                  pl.BlockSpec((d, npad), lambda r: (0, 0))],
        out_specs=[row(d), pl.BlockSpec((1, d), lambda r: (0, 0))],
        out_shape=[jax.ShapeDtypeStruct((m, d), F32), jax.ShapeDtypeStruct((1, d), F32)],
        compiler_params=_params(1),
    )(dh, h, g, dproj, wp)


def _causal_t(t):
    return lax.broadcasted_iota(jnp.int32, (t, t), 0) <= lax.broadcasted_iota(jnp.int32, (t, t), 1)


HEADS_PER_STEP = 2


def fox_fwd(qf, kf, vt, name, rider=None):
    m = qf.shape[0]
    t = _row_tile(m)
    nq = m // t
    hp = HEADS_PER_STEP
    w = hp * LANES

    def body(q_ref, k_ref, vt_ref, o_ref, lse_ref, acc_ref, m_ref):
        qi = pl.program_id(1)
        acc_ref[...] = jnp.zeros_like(acc_ref)
        m_ref[...] = jnp.full_like(m_ref, NEG)

        def tile(ki, diagonal):
            off = pl.multiple_of(ki * t, t)
            for e in range(hp):
                sl = slice(e * LANES, (e + 1) * LANES)
                s = _dot_nt(k_ref[pl.ds(off, t), sl], q_ref[:, sl])
                if diagonal:
                    s = jnp.where(_causal_t(t), s, NEG)
                m_old = m_ref[e]
                m_new = jnp.maximum(m_old, jnp.max(s, axis=0, keepdims=True))
                p = jnp.exp(s - m_new).astype(BF16)
                acc_ref[e] = acc_ref[e] * jnp.exp(m_old - m_new) + _dot(vt_ref[sl, pl.ds(off, t)], p)
                m_ref[e] = m_new

        def step(ki, carry):
            tile(ki, False)
            return carry

        lax.fori_loop(0, qi, step, 0)
        tile(qi, True)
        row = lax.broadcasted_iota(jnp.int32, (LANES, t), 0)
        for e in range(hp):
            l = jnp.max(acc_ref[e, AUG:AUG + 8, :], axis=0, keepdims=True)
            o_ref[e * LANES:(e + 1) * LANES, :] = jnp.where(row < HEAD_DIM, acc_ref[e] * (1.0 / l), 0.0)
            lse_ref[e] = m_ref[e] + jnp.log(l)

    return rider_call(
        body, name, (8 // hp, nq),
        in_specs=[pl.BlockSpec((t, w), lambda hd, i: (i, hd)),
                  pl.BlockSpec((m, w), lambda hd, i: (0, hd)),
                  pl.BlockSpec((w, m), lambda hd, i: (hd, 0))],
        out_specs=[pl.BlockSpec((w, t), lambda hd, i: (hd, i)),
                   pl.BlockSpec((hp, 1, t), lambda hd, i: (hd, 0, i))],
        out_shape=[jax.ShapeDtypeStruct((8 * LANES, m), F32), jax.ShapeDtypeStruct((8, 1, m), F32)],
        scratch_shapes=[pltpu.VMEM((hp, LANES, t), F32), pltpu.VMEM((hp, 1, t), F32)],
        args=(qf, kf, vt), rider=rider)


def fox_bwd(qf, kf, vf, kt, dof, lse, delta, name, rider=None):
    m = qf.shape[0]
    t = _row_tile(m)
    nq = m // t
    hp = HEADS_PER_STEP
    w = hp * LANES

    def body(k_ref, v_ref, kt_ref, q_ref, do_ref, lse_ref, delta_ref, dk_ref, dv_ref, dq_ref, dka_ref, dva_ref):
        ki = pl.program_id(1)

        @pl.when(ki == 0)
        def _():
            dq_ref[...] = jnp.zeros_like(dq_ref)

        dka_ref[...] = jnp.zeros_like(dka_ref)
        dva_ref[...] = jnp.zeros_like(dva_ref)

        def tile(qi, diagonal):
            off = pl.multiple_of(qi * t, t)
            for e in range(hp):
                sl = slice(e * LANES, (e + 1) * LANES)
                q = q_ref[pl.ds(off, t), sl]
                do = do_ref[pl.ds(off, t), sl]
                s = _dot_nt(k_ref[:, sl], q)
                if diagonal:
                    s = jnp.where(_causal_t(t), s, NEG)
                p = jnp.exp(s - lse_ref[e, :, pl.ds(off, t)])
                ds = (p * (_dot_nt(v_ref[:, sl], do) - delta_ref[e, :, pl.ds(off, t)])).astype(BF16)
                dva_ref[:, sl] += _dot(p.astype(BF16), do)
                dka_ref[:, sl] += _dot(ds, q)
                dq_ref[sl, pl.ds(off, t)] += _dot(kt_ref[sl, :], ds)

        def step(qi, carry):
            tile(qi, False)
            return carry

        tile(ki, True)
        lax.fori_loop(ki + 1, nq, step, 0)
        dk_ref[...] = dka_ref[...]
        dv_ref[...] = dva_ref[...]

    tile_spec = pl.BlockSpec((t, w), lambda hd, i: (i, hd))
    full = pl.BlockSpec((m, w), lambda hd, i: (0, hd))
    stat = pl.BlockSpec((hp, 1, m), lambda hd, i: (hd, 0, 0))
    return rider_call(
        body, name, (8 // hp, nq),
        in_specs=[tile_spec, tile_spec, pl.BlockSpec((w, t), lambda hd, i: (hd, i)), full, full, stat, stat],
        out_specs=[tile_spec, tile_spec, pl.BlockSpec((w, m), lambda hd, i: (hd, 0))],
        out_shape=[jax.ShapeDtypeStruct((m, 8 * LANES), F32), jax.ShapeDtypeStruct((m, 8 * LANES), F32),
                   jax.ShapeDtypeStruct((8 * LANES, m), F32)],
        scratch_shapes=[pltpu.VMEM((t, w), F32), pltpu.VMEM((t, w), F32)],
        args=(kf, vf, kt, qf, dof, lse, delta), rider=rider)


def _bucket_ids():
    def bucket(dist):
        n = np.maximum(dist, 0)
        max_exact = N_BUCKETS // 2
        nf = np.maximum(n, 1).astype(np.float32)
        large = max_exact + (np.log(nf / max_exact) / math.log(MAX_DISTANCE / max_exact)
                             * (N_BUCKETS - max_exact)).astype(np.int32)
        return np.where(n < max_exact, n, np.minimum(large, N_BUCKETS - 1))

    tl = np.arange(LANES)[:, None]
    sl = np.arange(LANES)[None, :]
    prev = bucket(LANES + tl - sl)
    cur = bucket(tl - sl)
    meta = np.full((LANES, LANES), N_BUCKETS - 1)
    return np.concatenate([prev, cur, meta], axis=1).astype(np.int32)


def bias_build(table, name):
    ids = jnp.asarray(_bucket_ids())

    def body(t_ref, id_ref, o_ref):
        idv = id_ref[...]
        for h in range(8):
            acc = jnp.zeros((LANES, 3 * LANES), F32)
            for b in range(N_BUCKETS):
                acc = jnp.where(idv == b, t_ref[b, h], acc)
            o_ref[h] = acc

    return pl.pallas_call(
        body, name=name,
        in_specs=[pl.BlockSpec(memory_space=pltpu.SMEM), pl.BlockSpec(memory_space=pltpu.VMEM)],
        out_specs=pl.BlockSpec(memory_space=pltpu.VMEM),
        out_shape=jax.ShapeDtypeStruct((8, LANES, 3 * LANES), F32),
    )(table, ids)


def bias_reduce(dbias, name):
    ids = jnp.asarray(_bucket_ids())

    def body(d_ref, id_ref, o_ref):
        idv = id_ref[...]
        rr = lax.broadcasted_iota(jnp.int32, (N_BUCKETS, LANES), 0)
        cc = lax.broadcasted_iota(jnp.int32, (N_BUCKETS, LANES), 1)
        acc = jnp.zeros((N_BUCKETS, LANES), F32)
        for h in range(8):
            dv = d_ref[h]
            for b in range(N_BUCKETS):
                val = jnp.sum(jnp.where(idv == b, dv, 0.0), keepdims=True)
                acc = jnp.where((rr == b) & (cc == h), val, acc)
        o_ref[...] = acc

    return pl.pallas_call(
        body, name=name,
        in_specs=[pl.BlockSpec(memory_space=pltpu.VMEM), pl.BlockSpec(memory_space=pltpu.VMEM)],
        out_specs=pl.BlockSpec(memory_space=pltpu.VMEM),
        out_shape=jax.ShapeDtypeStruct((N_BUCKETS, LANES), F32),
    )(dbias, ids)


def _swa_valid(n):
    shape = (LANES, 3 * LANES)
    tl = lax.broadcasted_iota(jnp.int32, shape, 0)
    col = lax.broadcasted_iota(jnp.int32, shape, 1)
    sl = col & (LANES - 1)
    nv = jnp.full(shape, n, jnp.int32)
    is_meta = sl >= PAD_FRONT
    prev = (col < LANES) & (sl > tl) & (nv >= 1) & ((nv >= 2) | is_meta)
    cur = (col >= LANES) & (col < 2 * LANES) & (sl <= tl) & ((nv >= 1) | is_meta)
    meta = (col >= 2 * LANES) & is_meta & ((nv >= 2) | ((nv == 1) & (sl <= tl)))
    return prev | cur | meta


def _swa_keys(ref, n):
    off_prev = pl.multiple_of(jnp.maximum(n - 1, 0) * LANES, LANES)
    off_cur = pl.multiple_of(n * LANES, LANES)
    return jnp.concatenate([ref[pl.ds(off_prev, LANES), :], ref[pl.ds(off_cur, LANES), :], ref[0:LANES, :]], axis=0)


def swa_fwd(q, k, v, bias, sinks, name):
    m = q.shape[0]

    def body(q_ref, k_ref, v_ref, bias_ref, sink_ref, o_ref, lse_ref):
        n = pl.program_id(0)
        lane1 = lax.broadcasted_iota(jnp.int32, (1, LANES), 1)
        lane_t = lax.broadcasted_iota(jnp.int32, (LANES, LANES), 1)
        in_head = [lane1 < HEAD_DIM, lane1 >= HEAD_DIM]
        kall = _swa_keys(k_ref, n)
        vall = _swa_keys(v_ref, n)
        vs = [jnp.where(in_head[g], vall, jnp.zeros_like(vall)) for g in (0, 1)]
        valid = _swa_valid(n)
        lse = jnp.zeros((LANES, LANES), F32)
        for b in range(4):
            qb = q_ref[:, b * LANES:(b + 1) * LANES]
            ob = jnp.zeros((LANES, LANES), F32)
            for g in (0, 1):
                h = 4 * g + b
                qe = jnp.where(in_head[g], qb, jnp.zeros_like(qb))
                s = jnp.where(valid, _dot_nt(qe, kall) + bias_ref[h], NEG)
                sink = sink_ref[h]
                mx = jnp.maximum(jnp.max(s, axis=1, keepdims=True), sink)
                p = jnp.exp(s - mx)
                den = jnp.sum(p, axis=1, keepdims=True) + jnp.exp(sink - mx)
                ob = ob + _dot((p / den).astype(BF16), vs[g])
                lse = jnp.where(lane_t == h, mx + jnp.log(den), lse)
            o_ref[:, b * LANES:(b + 1) * LANES] = ob
        lse_ref[...] = lse

    return pl.pallas_call(
        body, name=name, grid=(m // LANES,),
        in_specs=[pl.BlockSpec((LANES, 512), lambda n: (n, 0)),
                  pl.BlockSpec((m, LANES), lambda n: (0, 0)), pl.BlockSpec((m, LANES), lambda n: (0, 0)),
                  pl.BlockSpec((8, LANES, 3 * LANES), lambda n: (0, 0, 0)),
                  pl.BlockSpec(memory_space=pltpu.SMEM)],
        out_specs=[pl.BlockSpec((LANES, 512), lambda n: (n, 0)), pl.BlockSpec((LANES, LANES), lambda n: (n, 0))],
        out_shape=[jax.ShapeDtypeStruct((m, 512), F32), jax.ShapeDtypeStruct((m, LANES), F32)],
        compiler_params=_params(1),
    )(q, k, v, bias, sinks)


def swa_bwd(q, k, v, bias, sinks, o, lse, do, name):
    m = q.shape[0]

    def body(q_ref, do_ref, o_ref, lse_ref, k_ref, v_ref, bias_ref, sink_ref,
             dq_ref, dk_ref, dv_ref, dbias_ref, dsink_ref):
        n = pl.program_id(0)

        @pl.when(n == 0)
        def _():
            for r in (dk_ref, dv_ref, dbias_ref, dsink_ref):
                r[...] = jnp.zeros_like(r)

        lane1 = lax.broadcasted_iota(jnp.int32, (1, LANES), 1)
        lane_t = lax.broadcasted_iota(jnp.int32, (LANES, LANES), 1)
        in_head = [lane1 < HEAD_DIM, lane1 >= HEAD_DIM]
        off_prev = pl.multiple_of(jnp.maximum(n - 1, 0) * LANES, LANES)
        off_cur = pl.multiple_of(n * LANES, LANES)
        kall = _swa_keys(k_ref, n)
        vall = _swa_keys(v_ref, n)
        ks = [jnp.where(in_head[g], kall, jnp.zeros_like(kall)) for g in (0, 1)]
        valid = _swa_valid(n)
        lsev = lse_ref[...]
        dsink = dsink_ref[...]
        dkall = jnp.zeros((3 * LANES, LANES), F32)
        dvall = jnp.zeros((3 * LANES, LANES), F32)
        for b in range(4):
            sl = slice(b * LANES, (b + 1) * LANES)
            qb = q_ref[:, sl]
            dob = do_ref[:, sl]
            prod = dob * o_ref[:, sl]
            dqb = jnp.zeros((LANES, LANES), F32)
            for g in (0, 1):
                h = 4 * g + b
                qe = jnp.where(in_head[g], qb, jnp.zeros_like(qb))
                doe = jnp.where(in_head[g], dob, 0.0).astype(BF16)
                delta = jnp.sum(jnp.where(in_head[g], prod, 0.0), axis=1, keepdims=True)
                lse_h = _lane_col(lsev, lane_t, h)
                s = jnp.where(valid, _dot_nt(qe, kall) + bias_ref[h], NEG)
                p = jnp.exp(s - lse_h)
                ds = p * (_dot_nt(doe, vall) - delta)
                dbias_ref[h] += ds
                sink_part = jnp.sum(-jnp.exp(sink_ref[h] - lse_h) * delta, keepdims=True)
                dsink = jnp.where(lane1 == h, dsink + sink_part, dsink)
                dsb = ds.astype(BF16)
                dqb = dqb + _dot(dsb, ks[g])
                dkall = dkall + _dot_tn(dsb, qe)
                dvall = dvall + _dot_tn(p.astype(BF16), doe)
            dq_ref[:, sl] = dqb
        dsink_ref[...] = dsink
        for ref, val in ((dk_ref, dkall), (dv_ref, dvall)):
            ref[pl.ds(off_prev, LANES), :] += val[0:LANES]
            ref[pl.ds(off_cur, LANES), :] += val[LANES:2 * LANES]
            ref[0:LANES, :] += val[2 * LANES:3 * LANES]

    blk = pl.BlockSpec((LANES, 512), lambda n: (n, 0))
    full = pl.BlockSpec((m, LANES), lambda n: (0, 0))
    return pl.pallas_call(
        body, name=name, grid=(m // LANES,),
        in_specs=[blk, blk, blk, pl.BlockSpec((LANES, LANES), lambda n: (n, 0)), full, full,
                  pl.BlockSpec((8, LANES, 3 * LANES), lambda n: (0, 0, 0)),
                  pl.BlockSpec(memory_space=pltpu.SMEM)],
        out_specs=[blk, full, full, pl.BlockSpec((8, LANES, 3 * LANES), lambda n: (0, 0, 0)),
                   pl.BlockSpec((1, LANES), lambda n: (0, 0))],
        out_shape=[jax.ShapeDtypeStruct((m, 512), F32), jax.ShapeDtypeStruct((m, LANES), F32),
                   jax.ShapeDtypeStruct((m, LANES), F32), jax.ShapeDtypeStruct((8, LANES, 3 * LANES), F32),
                   jax.ShapeDtypeStruct((1, LANES), F32)],
        compiler_params=_params(1),
    )(q, do, o, lse, k, v, bias, sinks)


def branch_out(h, o_fox, o_swa, proj, wbf, wbs, wo, cols, name):
    m, d = h.shape
    tm = _row_tile(m)

    def body(h_ref, of_ref, os_ref, ga_ref, gb_ref, wbf_ref, wbs_ref, wo_ref, hn_ref):
        tf = _dot(of_ref[...].astype(BF16), wbf_ref[...])
        ts = _dot(os_ref[...].astype(BF16), wbs_ref[...])
        y = jax.nn.sigmoid(ga_ref[...]) * tf + jax.nn.sigmoid(gb_ref[...]) * ts
        hn_ref[...] = h_ref[...] + _dot(y.astype(BF16), wo_ref[...])

    row = lambda w, o=0: pl.BlockSpec((tm, w), lambda r, o=o: (r, o))
    res = lambda a: pl.BlockSpec(a.shape, lambda r: (0, 0))
    return pl.pallas_call(
        body, name=name, grid=(m // tm,),
        in_specs=[row(d), row(1024), row(512), row(d, cols.ga // d), row(d, cols.gb // d), res(wbf), res(wbs), res(wo)],
        out_specs=row(d),
        out_shape=jax.ShapeDtypeStruct((m, d), F32),
        compiler_params=_params(1),
    )(h, o_fox, o_swa, proj, proj, wbf, wbs, wo)


def branch_out_bwd(dh, o_fox, o_swa, proj, wbf, wbs, wo, cols, name):
    m, d = dh.shape
    tm = _row_tile(m)

    def body(dh_ref, of_ref, os_ref, ga_ref, gb_ref, wbf_ref, wbs_ref, wo_ref,
             y_ref, dtf_ref, dts_ref, dga_ref, dgb_ref, dof_ref, dos_ref, delta_ref):
        dy = _dot_nt(dh_ref[...].astype(BF16), wo_ref[...])
        tf = _dot(of_ref[...].astype(BF16), wbf_ref[...])
        ts = _dot(os_ref[...].astype(BF16), wbs_ref[...])
        sa = jax.nn.sigmoid(ga_ref[...])
        sb = jax.nn.sigmoid(gb_ref[...])
        y_ref[...] = (sa * tf + sb * ts).astype(BF16)
        dtf = (dy * sa).astype(BF16)
        dts = (dy * sb).astype(BF16)
        dtf_ref[...] = dtf
        dts_ref[...] = dts
        dga_ref[...] = (dy * tf * sa * (1.0 - sa)).astype(BF16)
        dgb_ref[...] = (dy * ts * sb * (1.0 - sb)).astype(BF16)
        dof = _dot_nt(dtf, wbf_ref[...])
        dof_ref[...] = dof.astype(BF16)
        dos_ref[...] = _dot_nt(dts, wbs_ref[...])
        lane = lax.broadcasted_iota(jnp.int32, (tm, LANES), 1)
        delta = jnp.zeros((tm, LANES), F32)
        for hd in range(8):
            sl = slice(hd * LANES, (hd + 1) * LANES)
            delta = jnp.where(lane == hd, jnp.sum(dof[:, sl] * of_ref[:, sl], axis=1, keepdims=True), delta)
        delta_ref[...] = delta

    row = lambda w, o=0: pl.BlockSpec((tm, w), lambda r, o=o: (r, o))
    res = lambda a: pl.BlockSpec(a.shape, lambda r: (0, 0))
    return pl.pallas_call(
        body, name=name, grid=(m // tm,),
        in_specs=[row(d), row(1024), row(512), row(d, cols.ga // d), row(d, cols.gb // d), res(wbf), res(wbs), res(wo)],
        out_specs=[row(d)] * 5 + [row(1024), row(512), row(LANES)],
        out_shape=[jax.ShapeDtypeStruct((m, d), BF16)] * 5 + [jax.ShapeDtypeStruct((m, 1024), BF16),
                   jax.ShapeDtypeStruct((m, 512), F32), jax.ShapeDtypeStruct((m, LANES), F32)],
        compiler_params=_params(1),
    )(dh, o_fox, o_swa, proj, proj, wbf, wbs, wo)


def loss_head(h, target, name):
    m, d = h.shape

    def body(h_ref, t_ref, dh_ref, loss_ref):
        n = pl.program_id(0)

        @pl.when(n == 0)
        def _():
            loss_ref[...] = jnp.zeros_like(loss_ref)
            dh_ref[...] = jnp.zeros_like(dh_ref)

        @pl.when(n > 0)
        def _():
            err = h_ref[...] - t_ref[...]
            dh_ref[...] = err * (1.0 / d)
            loss_ref[...] += jnp.sum(err * err, keepdims=True) * (0.5 / d)

    return pl.pallas_call(
        body, name=name, grid=(m // LANES,),
        in_specs=[pl.BlockSpec((LANES, d), lambda n: (n, 0)),
                  pl.BlockSpec((LANES, d), lambda n: (jnp.maximum(n - 1, 0), 0))],
        out_specs=[pl.BlockSpec((LANES, d), lambda n: (n, 0)), pl.BlockSpec((8, LANES), lambda n: (0, 0))],
        out_shape=[jax.ShapeDtypeStruct((m, d), F32), jax.ShapeDtypeStruct((8, LANES), F32)],
        compiler_params=_params(1),
    )(h, target)


def _adamw_math(w, g, m, v):
    m = ADAM_B1 * m + (1.0 - ADAM_B1) * g
    v = ADAM_B2 * v + (1.0 - ADAM_B2) * (g * g)
    m_hat = m / (1.0 - ADAM_B1 ** ADAM_STEP)
    v_hat = v / (1.0 - ADAM_B2 ** ADAM_STEP)
    delta = -ADAM_LR * (m_hat / (jnp.sqrt(v_hat) + ADAM_EPS) + ADAM_WD * w)
    return delta, m, v


def adamw_sum(parts, w, m, v, name):
    n_layers, a, b = w.shape
    ta = next(t for t in (256, 176, 128, a) if a % t == 0)
    nr = a // ta

    def body(*refs):
        p_refs = refs[:n_layers]
        w_ref, m_ref, v_ref, g_o, d_o, m_o, v_o = refs[n_layers:]
        for l in range(n_layers):
            @pl.when(pl.program_id(0) == l)
            def _(l=l):
                g = p_refs[l][0].astype(F32)
                for j in range(1, N_DEV):
                    g = g + p_refs[l][j].astype(F32)
                g_o[0] = g
                d_o[0], m_o[0], v_o[0] = _adamw_math(w_ref[0], g, m_ref[0], v_ref[0])

    def part_spec(l):
        return pl.BlockSpec((N_DEV, ta, b), lambda i, r, l=l: (0, jnp.where(i == l, r, jnp.where(i < l, 0, nr - 1)), 0))

    row = pl.BlockSpec((1, ta, b), lambda i, r: (i, r, 0))
    return pl.pallas_call(
        body, name=name, grid=(n_layers, nr),
        in_specs=[part_spec(l) for l in range(n_layers)] + [row, row, row],
        out_specs=[row] * 4,
        out_shape=[jax.ShapeDtypeStruct(w.shape, F32)] * 4,
        compiler_params=_params(2),
    )(*parts, w, m, v)


def adamw_small(g, w, m, v, name):
    def body(g_ref, w_ref, m_ref, v_ref, d_o, m_o, v_o):
        d_o[...], m_o[...], v_o[...] = _adamw_math(w_ref[...], g_ref[...], m_ref[...], v_ref[...])

    spec = pl.BlockSpec(memory_space=pltpu.VMEM)
    return pl.pallas_call(
        body, name=name, in_specs=[spec] * 4, out_specs=[spec] * 3,
        out_shape=[jax.ShapeDtypeStruct(w.shape, F32)] * 3,
    )(g, w, m, v)


BIG = ("ffn1_w_in", "ffn1_w_out", "w_in", "w_branch_fox", "w_branch_swa", "w_out", "ffn2_w_in", "ffn2_w_out")
SMALL = ("rel_bias_table", "ffn1_norm", "mix_norm", "forget_bias", "fox_q_norm", "fox_k_norm",
         "swa_q_norm", "swa_k_norm", "swa_sinks", "ffn2_norm")
WEIGHTS = ("meta_tokens", "rel_bias_table", "ffn1_norm", "ffn1_w_in", "ffn1_w_out", "mix_norm", "w_in",
           "forget_bias", "fox_q_norm", "fox_k_norm", "swa_q_norm", "swa_k_norm", "swa_sinks", "w_branch_fox",
           "w_branch_swa", "w_out", "ffn2_norm", "ffn2_w_in", "ffn2_w_out")


def _pack(arrs, width, row_multiple, dtype):
    lead = arrs[0].shape[:-1]
    flat = jnp.concatenate([a.astype(dtype) for a in arrs], axis=-1)
    n = flat.shape[-1]
    rows = -(-n // width)
    rows = -(-rows // row_multiple) * row_multiple
    flat = jnp.pad(flat, [(0, 0)] * len(lead) + [(0, rows * width - n)])
    return flat.reshape(lead + (rows, width))


def _unpack(flat, shapes):
    flat = flat.reshape(-1)
    out, off = [], 0
    for s in shapes:
        n = int(np.prod(s))
        out.append(flat[off:off + n].reshape(s))
        off += n
    return out


def _swa_head_order():
    return [4 * (j % 2) + j // 2 for j in range(8)]


def _permute_heads(a, axis, inverse=False):
    order = _swa_head_order()
    if inverse:
        order = [order.index(hd) for hd in range(8)]
    parts = [lax.slice_in_dim(a, hd * HEAD_DIM, (hd + 1) * HEAD_DIM, axis=axis) for hd in order]
    return jnp.concatenate(parts, axis=axis)


def _w_in_segments(cols):
    d = cols.d
    segs = [(512 * i + HEAD_DIM * hd, HEAD_DIM, new + LANES * hd)
            for i, new in enumerate((cols.qa, cols.ka, cols.va)) for hd in range(8)]
    segs.append((1536, 8, cols.fa))
    order = _swa_head_order()
    segs += [(1544 + HEAD_DIM * hd, HEAD_DIM, cols.qb + HEAD_DIM * order.index(hd)) for hd in range(8)]
    segs += [(2056, 128, cols.kb), (2184, 128, cols.vb), (2312, d, cols.ga), (2312 + d, d, cols.gb)]
    return segs


def _reorder_w_in(blocks, cols):
    width = blocks[0].shape[1]
    zeros = lambda n: jnp.zeros((blocks[0].shape[0], n), blocks[0].dtype)
    parts, at = [], 0
    for old, length, new in sorted(_w_in_segments(cols), key=lambda s: s[2]):
        if new > at:
            parts.append(zeros(new - at))
        at = new + length
        while length:
            j, off = divmod(old, width)
            take = min(length, width - off)
            parts.append(blocks[j][:, off:off + take])
            old, length = old + take, length - take
    parts.append(zeros(cols.np - at))
    return jnp.concatenate(parts, axis=1)


def _restore_w_in(wp, cols, width):
    segs = sorted(_w_in_segments(cols))
    blocks = []
    for j in range(N_DEV):
        lo, hi = j * width, (j + 1) * width
        parts = []
        for old, length, new in segs:
            a, b = max(old, lo), min(old + length, hi)
            if a < b:
                parts.append(wp[:, new + a - old:new + b - old])
        blocks.append(jnp.concatenate(parts, axis=1))
    return jnp.stack(blocks)


def _lane_pad(v):
    return jnp.pad(v, ((0, 0), (0, LANES - v.shape[1])))


def kernel(x, meta_tokens, rel_bias_table, ffn1_norm, ffn1_w_in, ffn1_w_out, mix_norm, w_in, forget_bias, fox_q_norm, fox_k_norm, swa_q_norm, swa_k_norm, swa_sinks, w_branch_fox, w_branch_swa, w_out, ffn2_norm, ffn2_w_in, ffn2_w_out, loss_target, m_meta_tokens, m_rel_bias_table, m_ffn1_norm, m_ffn1_w_in, m_ffn1_w_out, m_mix_norm, m_w_in, m_forget_bias, m_fox_q_norm, m_fox_k_norm, m_swa_q_norm, m_swa_k_norm, m_swa_sinks, m_w_branch_fox, m_w_branch_swa, m_w_out, m_ffn2_norm, m_ffn2_w_in, m_ffn2_w_out, v_meta_tokens, v_rel_bias_table, v_ffn1_norm, v_ffn1_w_in, v_ffn1_w_out, v_mix_norm, v_w_in, v_forget_bias, v_fox_q_norm, v_fox_k_norm, v_swa_q_norm, v_swa_k_norm, v_swa_sinks, v_w_branch_fox, v_w_branch_swa, v_w_out, v_ffn2_norm, v_ffn2_w_in, v_ffn2_w_out):
    args = dict(locals())
    wts = {n: args[n] for n in WEIGHTS}
    mom1 = {n: args["m_" + n] for n in WEIGHTS}
    mom2 = {n: args["v_" + n] for n in WEIGHTS}

    seq, d = x.shape[1], x.shape[2]
    m_rows = seq + LANES
    depth = ffn1_norm.shape[0]
    fb = ffn1_w_in.shape[2]
    fo = ffn1_w_out.shape[1]
    din_shard = w_in.shape[2]
    cols = _Cols(d)
    scale = HEAD_DIM ** -0.5
    dev = 4 * lax.axis_index("x") + 2 * lax.axis_index("y") + lax.axis_index("c")

    groups = {"ffn1": ("ffn1_w_in", "ffn1_w_out"), "mix": ("w_in", "w_branch_fox", "w_branch_swa", "w_out"),
              "ffn2": ("ffn2_w_in", "ffn2_w_out")}
    shard = {n: wts[n].astype(BF16) for n in BIG}
    full, parts, gw = {}, {}, {}

    def keys_of(stages):
        return [(n, l) for g, l in stages if l < depth for n in groups[g]]

    def gather_rider(stages):
        return Rider([shard[n][l] for n, l in keys_of(stages)], True)

    def scatter_rider(stages):
        return Rider([gw[k] for k in keys_of(stages)], False)

    def ffn_weights(tag, l):
        return full[tag + "_w_in", l], full[tag + "_w_out", l].reshape(4, fb, d)

    def mixer_weights(l):
        wp = _reorder_w_in([full["w_in", l][j] for j in range(N_DEV)], cols)
        wbf = jnp.concatenate([full["w_branch_fox", l][j] for j in range(N_DEV)], axis=1)
        wbf = jnp.pad(wbf.reshape(8, HEAD_DIM, d), ((0, 0), (0, LANES - HEAD_DIM), (0, 0))).reshape(8 * LANES, d)
        wbs = _permute_heads(jnp.concatenate([full["w_branch_swa", l][j] for j in range(N_DEV)], axis=1), 0)
        return wp, wbf, wbs, full["w_out", l].reshape(d, d)

    full.update(zip(keys_of([("ffn1", 0)]), exchange_hbm(gather_rider([("ffn1", 0)]).srcs, True, "gather_first")))
    meta_all = gather_small(meta_tokens.reshape(1, N_META, -1), "gather_meta")
    meta_full = meta_all.transpose(1, 0, 2).reshape(N_META, d)
    tile8 = lambda g, s=1.0: jnp.tile(g.reshape(1, HEAD_DIM) * s, (1, 8))
    tile2 = lambda g: jnp.tile(g.reshape(1, HEAD_DIM), (1, 2))
    data_lanes = lambda g, s=1.0: _lane_pad(g.reshape(1, HEAD_DIM) * s)
    bias = bias_build(rel_bias_table, "swa_bias")

    first = jnp.concatenate([jnp.zeros((PAD_FRONT, d), F32), meta_full], axis=0)
    h = jnp.concatenate([first, x[0]], axis=0)
    saved, lw = [], []
    for l in range(depth):
        s, w = {"h0": h}, {}
        w["ffn1_in"], w["ffn1_out"] = ffn_weights("ffn1", l)
        stages = [("mix", 0)] if l == 0 else []
        (h, s["n1"], s["gate1"], s["up1"]), got = ffn_fwd(h, ffn1_norm[l:l + 1], w["ffn1_in"], w["ffn1_out"],
                                                          f"ffn1_fwd_{l}", gather_rider(stages))
        full.update(zip(keys_of(stages), got))
        s["h1"] = h
        w["wp"], w["wbf"], w["wbs"], w["wo"] = mixer_weights(l)
        s["nm"], s["proj"] = mixer_proj(h, mix_norm[l:l + 1], w["wp"], f"mixer_proj_{l}")
        s["gains"] = (data_lanes(fox_q_norm[l], scale), data_lanes(fox_k_norm[l]), tile8(swa_q_norm[l], scale),
                      tile2(swa_k_norm[l]))
        s["fbias"] = _lane_pad(forget_bias[l:l + 1])
        qf, kf, vf, qb, kb, vb = qk_post(s["proj"], s["gains"], s["fbias"], cols, f"qk_post_{l}")
        s.update(qf=qf, kf=kf, vf=vf, qb=qb, kb=kb, vb=vb)
        stages = [("ffn2", l), ("ffn1", l + 1)]
        (o_t, s["lse_fox"]), got = fox_fwd(qf, kf, vf.T, f"fox_fwd_{l}", gather_rider(stages))
        s["o_fox"] = o_t.T
        full.update(zip(keys_of(stages), got))
        s["o_swa"], s["lse_swa"] = swa_fwd(qb, kb, vb, bias, swa_sinks[l], f"swa_fwd_{l}")
        h = branch_out(h, s["o_fox"], s["o_swa"], s["proj"], w["wbf"], w["wbs"], w["wo"], cols, f"branch_out_{l}")
        s["h2"] = h
        w["ffn2_in"], w["ffn2_out"] = ffn_weights("ffn2", l)
        stages = [("mix", l + 1)]
        (h, s["n2"], s["gate2"], s["up2"]), got = ffn_fwd(h, ffn2_norm[l:l + 1], w["ffn2_in"], w["ffn2_out"],
                                                          f"ffn2_fwd_{l}", gather_rider(stages))
        full.update(zip(keys_of(stages), got))
        saved.append(s)
        lw.append(w)

    dh, loss_part = loss_head(h, loss_target[0], "loss_head")

    gs = {n: [None] * depth for n in SMALL}
    dbias_total = None
    for l in reversed(range(depth)):
        w, s = lw[l], saved[l]

        def ffn_back(dh, tag, hin, norm, n_in, gate, up, stages):
            (dh_in, a, dg, du, dgn, dhs), got = ffn_bwd(dh, hin, norm, gate, up, w[tag + "_in"], w[tag + "_out"],
                                                        f"{tag}_bwd_{l}", scatter_rider(stages))
            parts.update(zip(keys_of(stages), got))
            gw[tag + "_w_in", l] = matmul_tn(n_in[None], dg, f"{tag}_dwi_{l}", y2=du)
            gw[tag + "_w_out", l] = matmul_tn(a, dhs[None], f"{tag}_dwo_{l}").reshape(N_DEV, fo, d)
            return dh_in, dgn

        dh, gs["ffn2_norm"][l] = ffn_back(dh, "ffn2", s["h2"], ffn2_norm[l:l + 1], s["n2"], s["gate2"], s["up2"],
                                          [("ffn1", l + 1)])

        y, dtf, dts, dga, dgb, dof, dos, delta = branch_out_bwd(dh, s["o_fox"], s["o_swa"], s["proj"], w["wbf"],
                                                                w["wbs"], w["wo"], cols, f"branch_out_bwd_{l}")
        gw["w_out", l] = matmul_tn(y[None], dh[None], f"dw_out_{l}").reshape(N_DEV, d // N_DEV, d)
        to_shards = lambda a: a.reshape(512, N_DEV, d // N_DEV).transpose(1, 0, 2)
        gw["w_branch_fox", l] = to_shards(matmul_tn(s["o_fox"][None], dtf[None], f"dw_branch_fox_{l}")[0]
                                          .reshape(8, LANES, d)[:, :HEAD_DIM].reshape(512, d))
        gw["w_branch_swa", l] = to_shards(_permute_heads(
            matmul_tn(s["o_swa"][None], dts[None], f"dw_branch_swa_{l}")[0], 0, inverse=True))

        stages = [("ffn2", l)]
        (dkf, dvf, dqf_t), got = fox_bwd(s["qf"], s["kf"], s["vf"], s["kf"].T, dof, s["lse_fox"],
                                         delta[:, :8].T.reshape(8, 1, m_rows), f"fox_bwd_{l}", scatter_rider(stages))
        parts.update(zip(keys_of(stages), got))
        dc = _lane_pad(dqf_t.reshape(8, LANES, m_rows)[:, AUG].T - dkf.reshape(m_rows, 8, LANES)[:, :, AUG + 3])
        dqb, dkb, dvb, dbias, dsink = swa_bwd(s["qb"], s["kb"], s["vb"], bias, swa_sinks[l], s["o_swa"], s["lse_swa"],
                                              dos, f"swa_bwd_{l}")
        dbias_total = dbias if dbias_total is None else dbias_total + dbias
        gs["swa_sinks"][l] = dsink[0, :8]
        dproj, ggqa, ggka, ggqb, ggkb, gfb = qk_post_bwd(s["proj"], s["gains"], s["fbias"], dqf_t.T, dkf, dvf, dqb, dkb,
                                                         dvb, dc, dga, dgb, cols, f"qk_post_bwd_{l}")
        gs["fox_q_norm"][l] = ggqa[0, :HEAD_DIM] * scale
        gs["fox_k_norm"][l] = ggka[0, :HEAD_DIM]
        gs["swa_q_norm"][l] = ggqb.reshape(8, HEAD_DIM).sum(0) * scale
        gs["swa_k_norm"][l] = ggkb.reshape(2, HEAD_DIM).sum(0)
        gs["forget_bias"][l] = gfb[0, :8]
        dwp = matmul_tn(s["nm"][None], dproj[None], f"dw_in_{l}", tn=1024 if cols.np % 1024 == 0 else None)[0]
        gw["w_in", l] = _restore_w_in(dwp, cols, din_shard)
        dh, gs["mix_norm"][l] = dproj_bwd(dh, s["h1"], mix_norm[l:l + 1], dproj, w["wp"], f"dproj_bwd_{l}")

        dh, gs["ffn1_norm"][l] = ffn_back(dh, "ffn1", s["h0"], ffn1_norm[l:l + 1], s["n1"], s["gate1"], s["up1"],
                                          [("mix", l)])

    grad_x = dh[LANES:][None]
    dmeta = dh[PAD_FRONT:LANES]
    dtable = bias_reduce(dbias_total, "swa_dbias")[:, :8]

    parts.update(zip(keys_of([("ffn1", 0)]), exchange_hbm(scatter_rider([("ffn1", 0)]).srcs, False, "scatter_last")))
    big_out = [{}, {}, {}, {}]
    for n in BIG:
        outs = adamw_sum([parts[n, l] for l in range(depth)], wts[n], mom1[n], mom2[n], f"adamw_{n}")
        for k in range(4):
            big_out[k][n] = outs[k]

    small_g = {n: (jnp.stack(gs[n]) if n != "rel_bias_table" else None) for n in SMALL}
    small_g["rel_bias_table"] = dtable
    pieces = [loss_part[0:1, 0:1].reshape(1, 1)] + [small_g[n].reshape(1, -1) for n in SMALL] + [dmeta.reshape(1, -1)]
    small_shapes = [(1,)] + [wts[n].shape for n in SMALL] + [(N_META, d)]
    total = allsum_small(_pack(pieces, LANES, 8, F32), "allsum_small")
    summed = _unpack(total, small_shapes)
    loss = summed[0][0]
    g_small = dict(zip(SMALL, summed[1:1 + len(SMALL)]))
    g_meta = lax.dynamic_slice_in_dim(summed[-1], dev * (d // N_DEV), d // N_DEV, axis=1)
    names = SMALL + ("meta_tokens",)
    g_small["meta_tokens"] = g_meta
    pk = lambda src: _pack([src[n].reshape(1, -1) for n in names], LANES, 8, F32)[0]
    small_out = [dict(zip(names, _unpack(o, [wts[n].shape for n in names])))
                 for o in adamw_small(pk(g_small), pk(wts), pk(mom1), pk(mom2), "adamw_small")]

    grads = {**big_out[0], **g_small}
    delta = {**big_out[1], **small_out[0]}
    new_m = {**big_out[2], **small_out[1]}
    new_v = {**big_out[3], **small_out[2]}
    return (loss, grad_x, *[grads[n] for n in WEIGHTS], *[delta[n] for n in WEIGHTS],
            *[new_m[n] for n in WEIGHTS], *[new_v[n] for n in WEIGHTS])
```

```python
import math

import numpy as np
import jax
import jax.numpy as jnp
from jax import lax
from jax.experimental import pallas as pl
from jax.experimental.pallas import tpu as pltpu

F32 = jnp.float32
BF16 = jnp.bfloat16
EPS = 1e-6
NEG = -1e30
HEAD_DIM = 64
LANES = 128
N_META = 16
PAD_FRONT = LANES - N_META
N_BUCKETS = 32
MAX_DISTANCE = 128
N_DEV = 8
ADAM_LR, ADAM_B1, ADAM_B2, ADAM_EPS, ADAM_WD, ADAM_STEP = 0.001, 0.9, 0.999, 1e-08, 0.01, 10
VMEM_LIMIT = 56 * 1024 * 1024
MESH = pl.DeviceIdType.MESH


def _params(n_grid):
    return pltpu.CompilerParams(dimension_semantics=("arbitrary",) * n_grid,
                                vmem_limit_bytes=VMEM_LIMIT)


def _dot(a, b):
    return jnp.dot(a, b, preferred_element_type=F32)


def _dot_nt(a, b):
    return lax.dot_general(a, b, (((1,), (1,)), ((), ())), preferred_element_type=F32)


def _dot_tn(a, b):
    return lax.dot_general(a, b, (((0,), (0,)), ((), ())), preferred_element_type=F32)


def _rms(x):
    r = lax.rsqrt(jnp.mean(x * x, axis=-1, keepdims=True) + EPS)
    return x * r, r


def _rms_bwd(x, g, dn):
    xh, r = _rms(x)
    dxh = dn * g
    dx = r * (dxh - xh * jnp.mean(dxh * xh, axis=-1, keepdims=True))
    return dx, jnp.sum(dn * xh, axis=0, keepdims=True)


def _split2(v):
    hi = v.astype(BF16)
    return hi, (v - hi.astype(F32)).astype(BF16)


def _split3(v):
    hi = v.astype(BF16)
    r1 = v - hi.astype(F32)
    mid = r1.astype(BF16)
    return hi, mid, (r1 - mid.astype(F32)).astype(BF16)


def _group_ones():
    r = lax.broadcasted_iota(jnp.int32, (LANES, LANES), 0) // HEAD_DIM
    c = lax.broadcasted_iota(jnp.int32, (LANES, LANES), 1) // HEAD_DIM
    return jnp.where(r == c, 1.0, 0.0).astype(BF16)


def _group_mean(v, ones):
    hi, lo = _split2(v)
    return (_dot(hi, ones) + _dot(lo, ones)) * (1.0 / HEAD_DIM)


def _row_tile(m):
    return 384 if m % 384 == 0 else LANES


def _peer(k):
    x, y, c = lax.axis_index("x"), lax.axis_index("y"), lax.axis_index("c")
    px = 1 - x if k & 4 else x
    py = 1 - y if k & 2 else y
    pc = 1 - c if k & 1 else c
    return (px, py, pc), 4 * px + 2 * py + pc


def _exchange_body(src_ref, dst_ref, send_sems, recv_sems, local_sem, bcast):
    x, y, c = lax.axis_index("x"), lax.axis_index("y"), lax.axis_index("c")
    me = 4 * x + 2 * y + c
    mine = pltpu.make_async_copy(src_ref.at[0 if bcast else me], dst_ref.at[me], local_sem)
    mine.start()
    sends = []
    for k in range(1, N_DEV):
        dev, idx = _peer(k)
        cp = pltpu.make_async_remote_copy(
            src_ref=src_ref.at[0 if bcast else idx], dst_ref=dst_ref.at[me],
            send_sem=send_sems.at[k - 1], recv_sem=recv_sems.at[k - 1],
            device_id=dev, device_id_type=MESH)
        cp.start()
        sends.append(cp)
    for k in range(1, N_DEV):
        dev, idx = _peer(k)
        pltpu.make_async_remote_copy(
            src_ref=src_ref.at[0], dst_ref=dst_ref.at[idx],
            send_sem=send_sems.at[k - 1], recv_sem=recv_sems.at[k - 1],
            device_id=dev, device_id_type=MESH).wait_recv()
    for cp in sends:
        cp.wait_send()
    mine.wait()


class Rider:
    FIRST = (1, 2, 4, 6)
    RELAYED = (2, 4, 6)

    def __init__(self, srcs=(), bcast=True):
        self.srcs, self.bcast, self.n = list(srcs), bcast, len(srcs)

    def out_shapes(self):
        return [jax.ShapeDtypeStruct(((N_DEV,) + s.shape) if self.bcast else s.shape, s.dtype) for s in self.srcs]

    def specs(self):
        return [pl.BlockSpec(memory_space=pl.ANY)] * self.n

    def scratch(self):
        if not self.n:
            return []
        return [pltpu.SemaphoreType.DMA((self.n * (N_DEV - 1),)), pltpu.SemaphoreType.DMA((self.n * (N_DEV - 1),)),
                pltpu.SemaphoreType.DMA((self.n,))]

    @staticmethod
    def _copy(src, dst, a, pair, dev, send_sems, recv_sems):
        sem = a * (N_DEV - 1) + pair - 1
        return pltpu.make_async_remote_copy(src_ref=src, dst_ref=dst, send_sem=send_sems.at[sem],
                                            recv_sem=recv_sems.at[sem], device_id=dev, device_id_type=MESH)

    def _first(self):
        return self.FIRST if self.bcast else range(1, N_DEV)

    def _own(self, s, d, a, local_sems):
        me = 4 * lax.axis_index("x") + 2 * lax.axis_index("y") + lax.axis_index("c")
        return pltpu.make_async_copy(s if self.bcast else s.at[me], d.at[me], local_sems.at[a]), me

    def start(self, src_refs, dst_refs, send_sems, recv_sems, local_sems):
        for a, (s, d) in enumerate(zip(src_refs, dst_refs)):
            own, me = self._own(s, d, a, local_sems)
            own.start()
            for k in self._first():
                dev, idx = _peer(k)
                self._copy(s if self.bcast else s.at[idx], d.at[me], a, k, dev, send_sems, recv_sems).start()

    def relay(self, src_refs, dst_refs, send_sems, recv_sems, local_sems):
        if not self.bcast:
            return
        sibling, _ = _peer(1)
        for a, d in enumerate(dst_refs):
            for k in self.RELAYED:
                dev, idx = _peer(k)
                self._copy(d.at[idx], d.at[idx], a, k, dev, send_sems, recv_sems).wait_recv()
                self._copy(d.at[idx], d.at[idx], a, k + 1, sibling, send_sems, recv_sems).start()

    def wait(self, src_refs, dst_refs, send_sems, recv_sems, local_sems):
        sibling, _ = _peer(1)
        for a, (s, d) in enumerate(zip(src_refs, dst_refs)):
            own, me = self._own(s, d, a, local_sems)
            for k in range(1, N_DEV):
                if not (self.bcast and k in self.RELAYED):
                    dev, idx = _peer(k)
                    self._copy(d.at[idx], d.at[idx], a, k, dev, send_sems, recv_sems).wait_recv()
            for k in self._first():
                dev, idx = _peer(k)
                self._copy(s if self.bcast else s.at[idx], d.at[me], a, k, dev, send_sems, recv_sems).wait_send()
            if self.bcast:
                for k in self.RELAYED:
                    dev, idx = _peer(k)
                    self._copy(d.at[idx], d.at[idx], a, k + 1, sibling, send_sems, recv_sems).wait_send()
            own.wait()


def rider_call(core, name, grid, in_specs, out_specs, out_shape, scratch_shapes, args, rider=None):
    rider = rider or Rider()
    n_in, n_out, n_scr, nr = len(in_specs), len(out_specs), len(scratch_shapes), rider.n

    def body(*refs):
        ins, r_src = refs[:n_in], refs[n_in:n_in + nr]
        outs = refs[n_in + nr:n_in + nr + n_out]
        r_dst = refs[n_in + nr + n_out:n_in + 2 * nr + n_out]
        scr = refs[n_in + 2 * nr + n_out:n_in + 2 * nr + n_out + n_scr]
        sems = refs[n_in + 2 * nr + n_out + n_scr:]
        if nr:
            first, relay, last = True, True, True
            for ax, size in enumerate(grid):
                first = first & (pl.program_id(ax) == 0)
                relay = relay & (pl.program_id(ax) == (3 * size // 4 if ax == 0 else 0))
                last = last & (pl.program_id(ax) == size - 1)
            if not grid:
                rider.start(r_src, r_dst, *sems)
                rider.relay(r_src, r_dst, *sems)
            else:
                pl.when(first)(lambda: rider.start(r_src, r_dst, *sems))
                if rider.bcast:
                    pl.when(relay)(lambda: rider.relay(r_src, r_dst, *sems))
        core(*ins, *outs, *scr)
        if nr:
            if not grid:
                rider.wait(r_src, r_dst, *sems)
            else:
                pl.when(last)(lambda: rider.wait(r_src, r_dst, *sems))

    res = pl.pallas_call(
        body, name=name, grid=grid,
        in_specs=list(in_specs) + rider.specs(),
        out_specs=list(out_specs) + rider.specs(),
        out_shape=list(out_shape) + rider.out_shapes(),
        scratch_shapes=list(scratch_shapes) + rider.scratch(),
        compiler_params=_params(len(grid)),
    )(*args, *rider.srcs)
    return res[:n_out], res[n_out:]


def exchange_hbm(srcs, bcast, name):
    return rider_call(lambda: None, name, (), [], [], [], [], [], Rider(srcs, bcast))[1]


def allsum_small(vec, name):
    def body(src_ref, out_ref, dst_ref, send_sems, recv_sems, local_sem):
        _exchange_body(src_ref, dst_ref, send_sems, recv_sems, local_sem, True)
        acc = dst_ref[0]
        for j in range(1, N_DEV):
            acc = acc + dst_ref[j]
        out_ref[...] = acc

    return pl.pallas_call(
        body, name=name,
        out_shape=jax.ShapeDtypeStruct(vec.shape[1:], F32),
        in_specs=[pl.BlockSpec(memory_space=pltpu.VMEM)],
        out_specs=pl.BlockSpec(memory_space=pltpu.VMEM),
        scratch_shapes=[pltpu.VMEM((N_DEV,) + vec.shape[1:], F32),
                        pltpu.SemaphoreType.DMA((N_DEV - 1,)), pltpu.SemaphoreType.DMA((N_DEV - 1,)),
                        pltpu.SemaphoreType.DMA],
    )(vec)


def gather_small(vec, name):
    def body(src_ref, dst_ref, send_sems, recv_sems, local_sem):
        _exchange_body(src_ref, dst_ref, send_sems, recv_sems, local_sem, True)

    return pl.pallas_call(
        body, name=name,
        out_shape=jax.ShapeDtypeStruct((N_DEV,) + vec.shape[1:], F32),
        in_specs=[pl.BlockSpec(memory_space=pltpu.VMEM)],
        out_specs=pl.BlockSpec(memory_space=pltpu.VMEM),
        scratch_shapes=[pltpu.SemaphoreType.DMA((N_DEV - 1,)), pltpu.SemaphoreType.DMA((N_DEV - 1,)),
                        pltpu.SemaphoreType.DMA],
    )(vec)


def ffn_fwd(h, g, w_in8, w_out4, name, rider=None):
    m, d = h.shape
    fb = w_in8.shape[2]
    tm = _row_tile(m)

    def body(h_ref, g_ref, wg_ref, wu_ref, wo_ref, hn_ref, n_ref, gate_ref, up_ref, acc_ref):
        i = pl.program_id(1)

        @pl.when(i == 0)
        def _():
            xh, _ = _rms(h_ref[...])
            n_ref[...] = (xh * g_ref[...]).astype(BF16)
            acc_ref[...] = jnp.zeros_like(acc_ref)

        n = n_ref[...]
        gate = _dot(n, wg_ref[0])
        up = _dot(n, wu_ref[0])
        gate_ref[0] = gate
        up_ref[0] = up
        a = (gate * jax.nn.sigmoid(gate) * up).astype(BF16)
        acc_ref[...] += _dot(a, wo_ref[0])

        @pl.when(i == 3)
        def _():
            hn_ref[...] = h_ref[...] + 0.5 * acc_ref[...]

    return rider_call(
        body, name, (m // tm, 4),
        in_specs=[pl.BlockSpec((tm, d), lambda r, i: (r, 0)),
                  pl.BlockSpec((1, d), lambda r, i: (0, 0)),
                  pl.BlockSpec((1, d, fb), lambda r, i: (i, 0, 0)),
                  pl.BlockSpec((1, d, fb), lambda r, i: (i + 4, 0, 0)),
                  pl.BlockSpec((1, fb, d), lambda r, i: (i, 0, 0))],
        out_specs=[pl.BlockSpec((tm, d), lambda r, i: (r, 0)),
                   pl.BlockSpec((tm, d), lambda r, i: (r, 0)),
                   pl.BlockSpec((1, tm, fb), lambda r, i: (i, r, 0)),
                   pl.BlockSpec((1, tm, fb), lambda r, i: (i, r, 0))],
        out_shape=[jax.ShapeDtypeStruct((m, d), F32), jax.ShapeDtypeStruct((m, d), BF16),
                   jax.ShapeDtypeStruct((4, m, fb), F32), jax.ShapeDtypeStruct((4, m, fb), F32)],
        scratch_shapes=[pltpu.VMEM((tm, d), F32)],
        args=(h, g, w_in8, w_in8, w_out4), rider=rider)


def ffn_bwd(dh, h, g, gate, up, w_in8, w_out4, name, rider=None):
    m, d = h.shape
    fb = w_in8.shape[2]
    tm = _row_tile(m)

    def body(dh_ref, h_ref, g_ref, gate_ref, up_ref, wg_ref, wu_ref, wo_ref,
             dhin_ref, a_ref, dg_ref, du_ref, dgn_ref, dhs_ref, acc_ref):
        r = pl.program_id(0)
        i = pl.program_id(1)

        @pl.when(i == 0)
        def _():
            dhs_ref[...] = (0.5 * dh_ref[...]).astype(BF16)
            acc_ref[...] = jnp.zeros_like(acc_ref)

        @pl.when((r == 0) & (i == 0))
        def _():
            dgn_ref[...] = jnp.zeros_like(dgn_ref)

        da = _dot_nt(dhs_ref[...], wo_ref[0])
        gt = gate_ref[0]
        u = up_ref[0]
        sg = jax.nn.sigmoid(gt)
        sl = gt * sg
        a_ref[0] = (sl * u).astype(BF16)
        dub = (da * sl).astype(BF16)
        dgb = (da * u * (sg * (1.0 + gt * (1.0 - sg)))).astype(BF16)
        dg_ref[0] = dgb
        du_ref[0] = dub
        acc_ref[...] += _dot_nt(dgb, wg_ref[0]) + _dot_nt(dub, wu_ref[0])

        @pl.when(i == 3)
        def _():
            dx, dgain = _rms_bwd(h_ref[...], g_ref[...], acc_ref[...])
            dgn_ref[...] += dgain
            dhin_ref[...] = dh_ref[...] + dx

    row = lambda r, i: (r, 0)
    blk = lambda r, i: (i, r, 0)
    return rider_call(
        body, name, (m // tm, 4),
        in_specs=[pl.BlockSpec((tm, d), row), pl.BlockSpec((tm, d), row),
                  pl.BlockSpec((1, d), lambda r, i: (0, 0)),
                  pl.BlockSpec((1, tm, fb), blk), pl.BlockSpec((1, tm, fb), blk),
                  pl.BlockSpec((1, d, fb), lambda r, i: (i, 0, 0)),
                  pl.BlockSpec((1, d, fb), lambda r, i: (i + 4, 0, 0)),
                  pl.BlockSpec((1, fb, d), lambda r, i: (i, 0, 0))],
        out_specs=[pl.BlockSpec((tm, d), row),
                   pl.BlockSpec((1, tm, fb), blk), pl.BlockSpec((1, tm, fb), blk),
                   pl.BlockSpec((1, tm, fb), blk),
                   pl.BlockSpec((1, d), lambda r, i: (0, 0)),
                   pl.BlockSpec((tm, d), row)],
        out_shape=[jax.ShapeDtypeStruct((m, d), F32),
                   jax.ShapeDtypeStruct((4, m, fb), BF16), jax.ShapeDtypeStruct((4, m, fb), BF16),
                   jax.ShapeDtypeStruct((4, m, fb), BF16),
                   jax.ShapeDtypeStruct((1, d), F32), jax.ShapeDtypeStruct((m, d), BF16)],
        scratch_shapes=[pltpu.VMEM((tm, d), F32)],
        args=(dh, h, g, gate, up, w_in8, w_in8, w_out4), rider=rider)


def matmul_tn(x, y, name, tn=None, y2=None, rider=None):
    bx, m, k = x.shape
    by, _, n = y.shape
    b = max(bx, by) * (2 if y2 is not None else 1)
    tm = _row_tile(m)
    tn = n if tn is None else tn
    nt = n // tn
    nr = m // tm

    def body(*refs):
        x_ref, y_ref = refs[0], refs[1]
        o_ref, acc_ref = refs[-2], refs[-1]
        r = pl.program_id(2)

        @pl.when(r == 0)
        def _():
            acc_ref[...] = jnp.zeros_like(acc_ref)

        if y2 is None:
            acc_ref[...] += _dot_tn(x_ref[0].astype(BF16), y_ref[0].astype(BF16))
        else:
            @pl.when(pl.program_id(0) < by)
            def _():
                acc_ref[...] += _dot_tn(x_ref[0].astype(BF16), y_ref[0].astype(BF16))

            @pl.when(pl.program_id(0) >= by)
            def _():
                acc_ref[...] += _dot_tn(x_ref[0].astype(BF16), refs[2][0].astype(BF16))

        @pl.when(r == nr - 1)
        def _():
            o_ref[0] = acc_ref[...].astype(BF16)

    x_map = (lambda i, j, r: (i, r, 0)) if bx > 1 else (lambda i, j, r: (0, r, 0))
    if y2 is None:
        y_specs = [pl.BlockSpec((1, tm, tn), (lambda i, j, r: (i, r, j)) if by > 1 else (lambda i, j, r: (0, r, j)))]
    else:
        y_specs = [pl.BlockSpec((1, tm, tn), lambda i, j, r: (jnp.minimum(i, by - 1), jnp.where(i < by, r, nr - 1), j)),
                   pl.BlockSpec((1, tm, tn), lambda i, j, r: (jnp.maximum(i - by, 0), jnp.where(i < by, 0, r), j))]
    (out,), carried = rider_call(
        body, name, (b, nt, nr),
        in_specs=[pl.BlockSpec((1, tm, k), x_map)] + y_specs,
        out_specs=[pl.BlockSpec((1, k, tn), lambda i, j, r: (i, 0, j))],
        out_shape=[jax.ShapeDtypeStruct((b, k, n), BF16)],
        scratch_shapes=[pltpu.VMEM((k, tn), F32)],
        args=[x, y] + ([y2] if y2 is not None else []), rider=rider)
    return (out, carried) if rider is not None else out


AUG = HEAD_DIM


class _Cols:
    def __init__(self, d):
        self.d = d
        self.ga, self.gb = 0, d
        self.qa, self.ka, self.va = 2 * d, 2 * d + 1024, 2 * d + 2048
        self.qb = 2 * d + 3072
        self.kb, self.vb, self.fa = self.qb + 512, self.qb + 640, self.qb + 768
        self.np = self.qb + 1024


def mixer_proj(h, g, wp, name):
    m, d = h.shape
    npad = wp.shape[1]
    tm = _row_tile(m)

    def body(h_ref, g_ref, w_ref, n_ref, p_ref):
        xh, _ = _rms(h_ref[...])
        n = (xh * g_ref[...]).astype(BF16)
        n_ref[...] = n
        p_ref[...] = _dot(n, w_ref[...])

    return pl.pallas_call(
        body, name=name, grid=(m // tm,),
        in_specs=[pl.BlockSpec((tm, d), lambda r: (r, 0)), pl.BlockSpec((1, d), lambda r: (0, 0)),
                  pl.BlockSpec((d, npad), lambda r: (0, 0))],
        out_specs=[pl.BlockSpec((tm, d), lambda r: (r, 0)), pl.BlockSpec((tm, npad), lambda r: (r, 0))],
        out_shape=[jax.ShapeDtypeStruct((m, d), BF16), jax.ShapeDtypeStruct((m, npad), F32)],
        compiler_params=_params(1),
    )(h, g, wp)


def _head_norm(x, gain, ones):
    outs = []
    for b in range(x.shape[1] // LANES):
        xb = x[:, b * LANES:(b + 1) * LANES]
        r = lax.rsqrt(_group_mean(xb * xb, ones) + EPS)
        outs.append(xb * r * gain[:, b * LANES:(b + 1) * LANES])
    return outs


def _head_norm_bwd(x, gain, dn, ones):
    dxs, dgs = [], []
    for b in range(x.shape[1] // LANES):
        sl = slice(b * LANES, (b + 1) * LANES)
        xb, dnb = x[:, sl], dn[:, sl]
        r = lax.rsqrt(_group_mean(xb * xb, ones) + EPS)
        xh = xb * r
        dxh = dnb * gain[:, sl]
        dxs.append(r * (dxh - xh * _group_mean(dxh * xh, ones)))
        dgs.append(jnp.sum(dnb * xh, axis=0, keepdims=True))
    return dxs, dgs


def _lane_col(v, lane_iota, idx):
    return jnp.sum(jnp.where(lane_iota == idx, v, 0.0), axis=1, keepdims=True)


def _aug(base, lane, vals):
    for i, v in enumerate(vals):
        base = jnp.where(lane == AUG + i, v, base)
    return base


def qk_post(proj, gains, fbias, cols, name):
    m = proj.shape[0]
    tm = _row_tile(m)
    gqa, gka, gqb, gkb = gains

    def body(qa_ref, ka_ref, va_ref, qb_ref, kb_ref, vb_ref, fa_ref, gqa_ref, gka_ref, gqb_ref, gkb_ref, fb_ref,
             qf_o, kf_o, vf_o, qb_o, kb_o, vb_o, carry_ref):
        r0 = pl.program_id(0)

        @pl.when(r0 == 0)
        def _():
            carry_ref[...] = jnp.zeros_like(carry_ref)

        z = fa_ref[...] + fb_ref[...]
        logf = jnp.minimum(z, 0.0) - jnp.log(1.0 + jnp.exp(-jnp.abs(z)))
        rr = lax.broadcasted_iota(jnp.int32, (tm, tm), 0)
        cc = lax.broadcasted_iota(jnp.int32, (tm, tm), 1)
        tril = jnp.where(cc <= rr, 1.0, 0.0).astype(BF16)
        p0, p1, p2 = _split3(logf)
        c = _dot(tril, p0) + _dot(tril, p1) + _dot(tril, p2) + carry_ref[...]
        carry_ref[...] += jnp.sum(logf, axis=0, keepdims=True)

        lane = lax.broadcasted_iota(jnp.int32, (tm, LANES), 1)
        is_pad = (r0 * tm + lax.broadcasted_iota(jnp.int32, (tm, 1), 0)) < PAD_FRONT
        ones = jnp.ones((LANES, LANES), BF16)
        for hd in range(8):
            sl = slice(hd * LANES, (hd + 1) * LANES)
            ch = _lane_col(c, lane, hd)
            ct = [p.astype(F32) for p in _split3(ch)]
            cs = [p.astype(F32) for p in _split3(-jnp.where(is_pad, -NEG, ch))]
            xq = qa_ref[:, sl]
            qn = xq * lax.rsqrt(_group_mean(xq * xq, ones) + EPS) * gqa_ref[...]
            qf_o[:, sl] = _aug(qn, lane, ct + [1.0, 1.0, 1.0]).astype(BF16)
            xk = ka_ref[:, sl]
            kn = xk * lax.rsqrt(_group_mean(xk * xk, ones) + EPS) * gka_ref[...]
            kf_o[:, sl] = _aug(kn, lane, [1.0, 1.0, 1.0] + cs).astype(BF16)
            vf_o[:, sl] = _aug(va_ref[:, sl], lane, [1.0, 1.0, 1.0]).astype(BF16)

        gones = _group_ones()
        for src, gn, dst in ((qb_ref, gqb_ref, qb_o), (kb_ref, gkb_ref, kb_o)):
            for b, blk in enumerate(_head_norm(src[...], gn[...], gones)):
                dst[:, b * LANES:(b + 1) * LANES] = blk.astype(BF16)
        vb_o[...] = vb_ref[...].astype(BF16)

    w1024 = lambda off: pl.BlockSpec((tm, 1024), lambda r, o=off // 1024: (r, o))
    w512 = lambda off: pl.BlockSpec((tm, 512), lambda r, o=off // 512: (r, o))
    w128 = lambda off: pl.BlockSpec((tm, LANES), lambda r, o=off // LANES: (r, o))
    vec = lambda w: pl.BlockSpec((1, w), lambda r: (0, 0))
    row = lambda w: pl.BlockSpec((tm, w), lambda r: (r, 0))
    return pl.pallas_call(
        body, name=name, grid=(m // tm,),
        in_specs=[w1024(cols.qa), w1024(cols.ka), w1024(cols.va), w512(cols.qb), w128(cols.kb), w128(cols.vb),
                  w128(cols.fa), vec(LANES), vec(LANES), vec(512), vec(LANES), vec(LANES)],
        out_specs=[row(1024), row(1024), row(1024), row(512), row(LANES), row(LANES)],
        out_shape=[jax.ShapeDtypeStruct((m, 1024), BF16)] * 3 + [jax.ShapeDtypeStruct((m, 512), BF16)]
                  + [jax.ShapeDtypeStruct((m, LANES), BF16)] * 2,
        scratch_shapes=[pltpu.VMEM((1, LANES), F32)],
        compiler_params=_params(1),
    )(proj, proj, proj, proj, proj, proj, proj, gqa, gka, gqb, gkb, fbias)


def qk_post_bwd(proj, gains, fbias, dqf, dkf, dvf, dqb, dkb, dvb, dc, dga, dgb, cols, name):
    m = proj.shape[0]
    d = cols.d
    tm = _row_tile(m)
    nt = m // tm
    gqa, gka, gqb, gkb = gains

    def body(qa_ref, ka_ref, qb_ref, kb_ref, fa_ref, gqa_ref, gka_ref, gqb_ref, gkb_ref, fb_ref,
             dqf_ref, dkf_ref, dvf_ref, dqb_ref, dkb_ref, dvb_ref, dc_ref, dga_ref, dgb_ref,
             dp_o, ggqa_o, ggka_o, ggqb_o, ggkb_o, gfb_o, carry_ref):
        @pl.when(pl.program_id(0) == 0)
        def _():
            carry_ref[...] = jnp.zeros_like(carry_ref)
            for o in (ggqa_o, ggka_o, ggqb_o, ggkb_o, gfb_o):
                o[...] = jnp.zeros_like(o)

        dp_o[:, cols.ga:cols.ga + d] = dga_ref[...].astype(BF16)
        dp_o[:, cols.gb:cols.gb + d] = dgb_ref[...].astype(BF16)
        dp_o[:, cols.fa + LANES:cols.np] = jnp.zeros((tm, cols.np - cols.fa - LANES), BF16)
        lane = lax.broadcasted_iota(jnp.int32, (tm, LANES), 1)
        data = lane < HEAD_DIM
        ones = jnp.ones((LANES, LANES), BF16)
        for hd in range(8):
            sl = slice(hd * LANES, (hd + 1) * LANES)
            for src, gn, dn_ref, off, gout in ((qa_ref, gqa_ref, dqf_ref, cols.qa, ggqa_o),
                                               (ka_ref, gka_ref, dkf_ref, cols.ka, ggka_o)):
                x = src[:, sl]
                dn = jnp.where(data, dn_ref[:, sl], 0.0)
                r = lax.rsqrt(_group_mean(x * x, ones) + EPS)
                xh = x * r
                dxh = dn * gn[...]
                dp_o[:, off + hd * LANES:off + (hd + 1) * LANES] = (
                    r * (dxh - xh * _group_mean(dxh * xh, ones))).astype(BF16)
                gout[...] += jnp.sum(dn * xh, axis=0, keepdims=True)
            dp_o[:, cols.va + hd * LANES:cols.va + (hd + 1) * LANES] = jnp.where(data, dvf_ref[:, sl], 0.0).astype(BF16)
        dp_o[:, cols.vb:cols.vb + LANES] = dvb_ref[...].astype(BF16)
        gones = _group_ones()
        for src, gn, dn, off, gout in ((qb_ref, gqb_ref, dqb_ref, cols.qb, ggqb_o),
                                       (kb_ref, gkb_ref, dkb_ref, cols.kb, ggkb_o)):
            dxs, dgs = _head_norm_bwd(src[...], gn[...], dn[...], gones)
            for b, (dx, dg) in enumerate(zip(dxs, dgs)):
                dp_o[:, off + b * LANES:off + (b + 1) * LANES] = dx.astype(BF16)
                gout[:, b * LANES:(b + 1) * LANES] += dg
        dcv = dc_ref[...]
        rr = lax.broadcasted_iota(jnp.int32, (tm, tm), 0)
        cc = lax.broadcasted_iota(jnp.int32, (tm, tm), 1)
        triu = jnp.where(cc >= rr, 1.0, 0.0).astype(BF16)
        p0, p1, p2 = _split3(dcv)
        dlogf = _dot(triu, p0) + _dot(triu, p1) + _dot(triu, p2) + carry_ref[...]
        carry_ref[...] += jnp.sum(dcv, axis=0, keepdims=True)
        z = fa_ref[...] + fb_ref[...]
        row = (nt - 1 - pl.program_id(0)) * tm + lax.broadcasted_iota(jnp.int32, (tm, LANES), 0)
        dfa = jnp.where(row >= PAD_FRONT, dlogf * jax.nn.sigmoid(-z), 0.0)
        dp_o[:, cols.fa:cols.fa + LANES] = dfa.astype(BF16)
        gfb_o[...] += jnp.sum(dfa, axis=0, keepdims=True)

    rev = lambda r: nt - 1 - r
    w1024 = lambda off: pl.BlockSpec((tm, 1024), lambda r, o=off // 1024: (rev(r), o))
    w512 = lambda off: pl.BlockSpec((tm, 512), lambda r, o=off // 512: (rev(r), o))
    w128 = lambda off: pl.BlockSpec((tm, LANES), lambda r, o=off // LANES: (rev(r), o))
    vec = lambda w: pl.BlockSpec((1, w), lambda r: (0, 0))
    row = lambda w: pl.BlockSpec((tm, w), lambda r: (rev(r), 0))
    return pl.pallas_call(
        body, name=name, grid=(nt,),
        in_specs=[w1024(cols.qa), w1024(cols.ka), w512(cols.qb), w128(cols.kb), w128(cols.fa),
                  vec(LANES), vec(LANES), vec(512), vec(LANES), vec(LANES),
                  row(1024), row(1024), row(1024), row(512), row(LANES), row(LANES), row(LANES), row(d), row(d)],
        out_specs=[row(cols.np), vec(LANES), vec(LANES), vec(512), vec(LANES), vec(LANES)],
        out_shape=[jax.ShapeDtypeStruct((m, cols.np), BF16), jax.ShapeDtypeStruct((1, LANES), F32),
                   jax.ShapeDtypeStruct((1, LANES), F32), jax.ShapeDtypeStruct((1, 512), F32),
                   jax.ShapeDtypeStruct((1, LANES), F32), jax.ShapeDtypeStruct((1, LANES), F32)],
        scratch_shapes=[pltpu.VMEM((1, LANES), F32)],
        compiler_params=_params(1),
    )(proj, proj, proj, proj, proj, gqa, gka, gqb, gkb, fbias, dqf, dkf, dvf, dqb, dkb, dvb, dc, dga, dgb)


def dproj_bwd(dh, h, g, dproj, wp, name):
    m, d = h.shape
    npad = wp.shape[1]
    tm = _row_tile(m)

    def body(dh_ref, h_ref, g_ref, dp_ref, w_ref, dhin_ref, dgn_ref):
        @pl.when(pl.program_id(0) == 0)
        def _():
            dgn_ref[...] = jnp.zeros_like(dgn_ref)

        dn = _dot_nt(dp_ref[...], w_ref[...])
        dx, dgain = _rms_bwd(h_ref[...], g_ref[...], dn)
        dgn_ref[...] += dgain
        dhin_ref[...] = dh_ref[...] + dx

    row = lambda w: pl.BlockSpec((tm, w), lambda r: (r, 0))
    return pl.pallas_call(
        body, name=name, grid=(m // tm,),
        in_specs=[row(d), row(d), pl.BlockSpec((1, d), lambda r: (0, 0)), row(npad),
                  pl.BlockSpec((d, npad), lambda r: (0, 0))],
        out_specs=[row(d), pl.BlockSpec((1, d), lambda r: (0, 0))],
        out_shape=[jax.ShapeDtypeStruct((m, d), F32), jax.ShapeDtypeStruct((1, d), F32)],
        compiler_params=_params(1),
    )(dh, h, g, dproj, wp)


def _causal_t(t):
    return lax.broadcasted_iota(jnp.int32, (t, t), 0) <= lax.broadcasted_iota(jnp.int32, (t, t), 1)


HEADS_PER_STEP = 2


def fox_fwd(qf, kf, vt, name, rider=None):
    m = qf.shape[0]
    t = _row_tile(m)
    nq = m // t
    hp = HEADS_PER_STEP
    w = hp * LANES

    def body(q_ref, k_ref, vt_ref, o_ref, lse_ref, acc_ref, m_ref):
        qi = pl.program_id(1)
        acc_ref[...] = jnp.zeros_like(acc_ref)
        m_ref[...] = jnp.full_like(m_ref, NEG)

        def tile(ki, diagonal):
            off = pl.multiple_of(ki * t, t)
            for e in range(hp):
                sl = slice(e * LANES, (e + 1) * LANES)
                s = _dot_nt(k_ref[pl.ds(off, t), sl], q_ref[:, sl])
                if diagonal:
                    s = jnp.where(_causal_t(t), s, NEG)
                m_old = m_ref[e]
                m_new = jnp.maximum(m_old, jnp.max(s, axis=0, keepdims=True))
                p = jnp.exp(s - m_new).astype(BF16)
                acc_ref[e] = acc_ref[e] * jnp.exp(m_old - m_new) + _dot(vt_ref[sl, pl.ds(off, t)], p)
                m_ref[e] = m_new

        def step(ki, carry):
            tile(ki, False)
            return carry

        lax.fori_loop(0, qi, step, 0)
        tile(qi, True)
        row = lax.broadcasted_iota(jnp.int32, (LANES, t), 0)
        for e in range(hp):
            l = jnp.max(acc_ref[e, AUG:AUG + 8, :], axis=0, keepdims=True)
            o_ref[e * LANES:(e + 1) * LANES, :] = jnp.where(row < HEAD_DIM, acc_ref[e] * (1.0 / l), 0.0)
            lse_ref[e] = m_ref[e] + jnp.log(l)

    return rider_call(
        body, name, (8 // hp, nq),
        in_specs=[pl.BlockSpec((t, w), lambda hd, i: (i, hd)),
                  pl.BlockSpec((m, w), lambda hd, i: (0, hd)),
                  pl.BlockSpec((w, m), lambda hd, i: (hd, 0))],
        out_specs=[pl.BlockSpec((w, t), lambda hd, i: (hd, i)),
                   pl.BlockSpec((hp, 1, t), lambda hd, i: (hd, 0, i))],
        out_shape=[jax.ShapeDtypeStruct((8 * LANES, m), F32), jax.ShapeDtypeStruct((8, 1, m), F32)],
        scratch_shapes=[pltpu.VMEM((hp, LANES, t), F32), pltpu.VMEM((hp, 1, t), F32)],
        args=(qf, kf, vt), rider=rider)


def fox_bwd(qf, kf, vf, kt, dof, lse, delta, name, rider=None):
    m = qf.shape[0]
    t = _row_tile(m)
    nq = m // t
    hp = HEADS_PER_STEP
    w = hp * LANES

    def body(k_ref, v_ref, kt_ref, q_ref, do_ref, lse_ref, delta_ref, dk_ref, dv_ref, dq_ref, dka_ref, dva_ref):
        ki = pl.program_id(1)

        @pl.when(ki == 0)
        def _():
            dq_ref[...] = jnp.zeros_like(dq_ref)

        dka_ref[...] = jnp.zeros_like(dka_ref)
        dva_ref[...] = jnp.zeros_like(dva_ref)

        def tile(qi, diagonal):
            off = pl.multiple_of(qi * t, t)
            for e in range(hp):
                sl = slice(e * LANES, (e + 1) * LANES)
                q = q_ref[pl.ds(off, t), sl]
                do = do_ref[pl.ds(off, t), sl]
                s = _dot_nt(k_ref[:, sl], q)
                if diagonal:
                    s = jnp.where(_causal_t(t), s, NEG)
                p = jnp.exp(s - lse_ref[e, :, pl.ds(off, t)])
                ds = (p * (_dot_nt(v_ref[:, sl], do) - delta_ref[e, :, pl.ds(off, t)])).astype(BF16)
                dva_ref[:, sl] += _dot(p.astype(BF16), do)
                dka_ref[:, sl] += _dot(ds, q)
                dq_ref[sl, pl.ds(off, t)] += _dot(kt_ref[sl, :], ds)

        def step(qi, carry):
            tile(qi, False)
            return carry

        tile(ki, True)
        lax.fori_loop(ki + 1, nq, step, 0)
        dk_ref[...] = dka_ref[...]
        dv_ref[...] = dva_ref[...]

    tile_spec = pl.BlockSpec((t, w), lambda hd, i: (i, hd))
    full = pl.BlockSpec((m, w), lambda hd, i: (0, hd))
    stat = pl.BlockSpec((hp, 1, m), lambda hd, i: (hd, 0, 0))
    return rider_call(
        body, name, (8 // hp, nq),
        in_specs=[tile_spec, tile_spec, pl.BlockSpec((w, t), lambda hd, i: (hd, i)), full, full, stat, stat],
        out_specs=[tile_spec, tile_spec, pl.BlockSpec((w, m), lambda hd, i: (hd, 0))],
        out_shape=[jax.ShapeDtypeStruct((m, 8 * LANES), F32), jax.ShapeDtypeStruct((m, 8 * LANES), F32),
                   jax.ShapeDtypeStruct((8 * LANES, m), F32)],
        scratch_shapes=[pltpu.VMEM((t, w), F32), pltpu.VMEM((t, w), F32)],
        args=(kf, vf, kt, qf, dof, lse, delta), rider=rider)


def _bucket_ids():
    def bucket(dist):
        n = np.maximum(dist, 0)
        max_exact = N_BUCKETS // 2
        nf = np.maximum(n, 1).astype(np.float32)
        large = max_exact + (np.log(nf / max_exact) / math.log(MAX_DISTANCE / max_exact)
                             * (N_BUCKETS - max_exact)).astype(np.int32)
        return np.where(n < max_exact, n, np.minimum(large, N_BUCKETS - 1))

    tl = np.arange(LANES)[:, None]
    sl = np.arange(LANES)[None, :]
    prev = bucket(LANES + tl - sl)
    cur = bucket(tl - sl)
    meta = np.full((LANES, LANES), N_BUCKETS - 1)
    return np.concatenate([prev, cur, meta], axis=1).astype(np.int32)


def bias_build(table, name):
    ids = jnp.asarray(_bucket_ids())

    def body(t_ref, id_ref, o_ref):
        idv = id_ref[...]
        for h in range(8):
            acc = jnp.zeros((LANES, 3 * LANES), F32)
            for b in range(N_BUCKETS):
                acc = jnp.where(idv == b, t_ref[b, h], acc)
            o_ref[h] = acc

    return pl.pallas_call(
        body, name=name,
        in_specs=[pl.BlockSpec(memory_space=pltpu.SMEM), pl.BlockSpec(memory_space=pltpu.VMEM)],
        out_specs=pl.BlockSpec(memory_space=pltpu.VMEM),
        out_shape=jax.ShapeDtypeStruct((8, LANES, 3 * LANES), F32),
    )(table, ids)


def bias_reduce(dbias, name):
    ids = jnp.asarray(_bucket_ids())

    def body(d_ref, id_ref, o_ref):
        idv = id_ref[...]
        rr = lax.broadcasted_iota(jnp.int32, (N_BUCKETS, LANES), 0)
        cc = lax.broadcasted_iota(jnp.int32, (N_BUCKETS, LANES), 1)
        acc = jnp.zeros((N_BUCKETS, LANES), F32)
        for h in range(8):
            dv = d_ref[h]
            for b in range(N_BUCKETS):
                val = jnp.sum(jnp.where(idv == b, dv, 0.0), keepdims=True)
                acc = jnp.where((rr == b) & (cc == h), val, acc)
        o_ref[...] = acc

    return pl.pallas_call(
        body, name=name,
        in_specs=[pl.BlockSpec(memory_space=pltpu.VMEM), pl.BlockSpec(memory_space=pltpu.VMEM)],
        out_specs=pl.BlockSpec(memory_space=pltpu.VMEM),
        out_shape=jax.ShapeDtypeStruct((N_BUCKETS, LANES), F32),
    )(dbias, ids)


def _swa_valid(n):
    shape = (LANES, 3 * LANES)
    tl = lax.broadcasted_iota(jnp.int32, shape, 0)
    col = lax.broadcasted_iota(jnp.int32, shape, 1)
    sl = col & (LANES - 1)
    nv = jnp.full(shape, n, jnp.int32)
    is_meta = sl >= PAD_FRONT
    prev = (col < LANES) & (sl > tl) & (nv >= 1) & ((nv >= 2) | is_meta)
    cur = (col >= LANES) & (col < 2 * LANES) & (sl <= tl) & ((nv >= 1) | is_meta)
    meta = (col >= 2 * LANES) & is_meta & ((nv >= 2) | ((nv == 1) & (sl <= tl)))
    return prev | cur | meta


def _swa_keys(ref, n):
    off_prev = pl.multiple_of(jnp.maximum(n - 1, 0) * LANES, LANES)
    off_cur = pl.multiple_of(n * LANES, LANES)
    return jnp.concatenate([ref[pl.ds(off_prev, LANES), :], ref[pl.ds(off_cur, LANES), :], ref[0:LANES, :]], axis=0)


def swa_fwd(q, k, v, bias, sinks, name):
    m = q.shape[0]

    def body(q_ref, k_ref, v_ref, bias_ref, sink_ref, o_ref, lse_ref):
        n = pl.program_id(0)
        lane1 = lax.broadcasted_iota(jnp.int32, (1, LANES), 1)
        lane_t = lax.broadcasted_iota(jnp.int32, (LANES, LANES), 1)
        in_head = [lane1 < HEAD_DIM, lane1 >= HEAD_DIM]
        kall = _swa_keys(k_ref, n)
        vall = _swa_keys(v_ref, n)
        vs = [jnp.where(in_head[g], vall, jnp.zeros_like(vall)) for g in (0, 1)]
        valid = _swa_valid(n)
        lse = jnp.zeros((LANES, LANES), F32)
        for b in range(4):
            qb = q_ref[:, b * LANES:(b + 1) * LANES]
            ob = jnp.zeros((LANES, LANES), F32)
            for g in (0, 1):
                h = 4 * g + b
                qe = jnp.where(in_head[g], qb, jnp.zeros_like(qb))
                s = jnp.where(valid, _dot_nt(qe, kall) + bias_ref[h], NEG)
                sink = sink_ref[h]
                mx = jnp.maximum(jnp.max(s, axis=1, keepdims=True), sink)
                p = jnp.exp(s - mx)
                den = jnp.sum(p, axis=1, keepdims=True) + jnp.exp(sink - mx)
                ob = ob + _dot((p / den).astype(BF16), vs[g])
                lse = jnp.where(lane_t == h, mx + jnp.log(den), lse)
            o_ref[:, b * LANES:(b + 1) * LANES] = ob
        lse_ref[...] = lse

    return pl.pallas_call(
        body, name=name, grid=(m // LANES,),
        in_specs=[pl.BlockSpec((LANES, 512), lambda n: (n, 0)),
                  pl.BlockSpec((m, LANES), lambda n: (0, 0)), pl.BlockSpec((m, LANES), lambda n: (0, 0)),
                  pl.BlockSpec((8, LANES, 3 * LANES), lambda n: (0, 0, 0)),
                  pl.BlockSpec(memory_space=pltpu.SMEM)],
        out_specs=[pl.BlockSpec((LANES, 512), lambda n: (n, 0)), pl.BlockSpec((LANES, LANES), lambda n: (n, 0))],
        out_shape=[jax.ShapeDtypeStruct((m, 512), F32), jax.ShapeDtypeStruct((m, LANES), F32)],
        compiler_params=_params(1),
    )(q, k, v, bias, sinks)


def swa_bwd(q, k, v, bias, sinks, o, lse, do, name):
    m = q.shape[0]

    def body(q_ref, do_ref, o_ref, lse_ref, k_ref, v_ref, bias_ref, sink_ref,
             dq_ref, dk_ref, dv_ref, dbias_ref, dsink_ref):
        n = pl.program_id(0)

        @pl.when(n == 0)
        def _():
            for r in (dk_ref, dv_ref, dbias_ref, dsink_ref):
                r[...] = jnp.zeros_like(r)

        lane1 = lax.broadcasted_iota(jnp.int32, (1, LANES), 1)
        lane_t = lax.broadcasted_iota(jnp.int32, (LANES, LANES), 1)
        in_head = [lane1 < HEAD_DIM, lane1 >= HEAD_DIM]
        off_prev = pl.multiple_of(jnp.maximum(n - 1, 0) * LANES, LANES)
        off_cur = pl.multiple_of(n * LANES, LANES)
        kall = _swa_keys(k_ref, n)
        vall = _swa_keys(v_ref, n)
        ks = [jnp.where(in_head[g], kall, jnp.zeros_like(kall)) for g in (0, 1)]
        valid = _swa_valid(n)
        lsev = lse_ref[...]
        dsink = dsink_ref[...]
        dkall = jnp.zeros((3 * LANES, LANES), F32)
        dvall = jnp.zeros((3 * LANES, LANES), F32)
        for b in range(4):
            sl = slice(b * LANES, (b + 1) * LANES)
            qb = q_ref[:, sl]
            dob = do_ref[:, sl]
            prod = dob * o_ref[:, sl]
            dqb = jnp.zeros((LANES, LANES), F32)
            for g in (0, 1):
                h = 4 * g + b
                qe = jnp.where(in_head[g], qb, jnp.zeros_like(qb))
                doe = jnp.where(in_head[g], dob, 0.0).astype(BF16)
                delta = jnp.sum(jnp.where(in_head[g], prod, 0.0), axis=1, keepdims=True)
                lse_h = _lane_col(lsev, lane_t, h)
                s = jnp.where(valid, _dot_nt(qe, kall) + bias_ref[h], NEG)
                p = jnp.exp(s - lse_h)
                ds = p * (_dot_nt(doe, vall) - delta)
                dbias_ref[h] += ds
                sink_part = jnp.sum(-jnp.exp(sink_ref[h] - lse_h) * delta, keepdims=True)
                dsink = jnp.where(lane1 == h, dsink + sink_part, dsink)
                dsb = ds.astype(BF16)
                dqb = dqb + _dot(dsb, ks[g])
                dkall = dkall + _dot_tn(dsb, qe)
                dvall = dvall + _dot_tn(p.astype(BF16), doe)
            dq_ref[:, sl] = dqb
        dsink_ref[...] = dsink
        for ref, val in ((dk_ref, dkall), (dv_ref, dvall)):
            ref[pl.ds(off_prev, LANES), :] += val[0:LANES]
            ref[pl.ds(off_cur, LANES), :] += val[LANES:2 * LANES]
            ref[0:LANES, :] += val[2 * LANES:3 * LANES]

    blk = pl.BlockSpec((LANES, 512), lambda n: (n, 0))
    full = pl.BlockSpec((m, LANES), lambda n: (0, 0))
    return pl.pallas_call(
        body, name=name, grid=(m // LANES,),
        in_specs=[blk, blk, blk, pl.BlockSpec((LANES, LANES), lambda n: (n, 0)), full, full,
                  pl.BlockSpec((8, LANES, 3 * LANES), lambda n: (0, 0, 0)),
                  pl.BlockSpec(memory_space=pltpu.SMEM)],
        out_specs=[blk, full, full, pl.BlockSpec((8, LANES, 3 * LANES), lambda n: (0, 0, 0)),
                   pl.BlockSpec((1, LANES), lambda n: (0, 0))],
        out_shape=[jax.ShapeDtypeStruct((m, 512), F32), jax.ShapeDtypeStruct((m, LANES), F32),
                   jax.ShapeDtypeStruct((m, LANES), F32), jax.ShapeDtypeStruct((8, LANES, 3 * LANES), F32),
                   jax.ShapeDtypeStruct((1, LANES), F32)],
        compiler_params=_params(1),
    )(q, do, o, lse, k, v, bias, sinks)


def branch_out(h, o_fox, o_swa, proj, wbf, wbs, wo, cols, name):
    m, d = h.shape
    tm = _row_tile(m)

    def body(h_ref, of_ref, os_ref, ga_ref, gb_ref, wbf_ref, wbs_ref, wo_ref, hn_ref):
        tf = _dot(of_ref[...].astype(BF16), wbf_ref[...])
        ts = _dot(os_ref[...].astype(BF16), wbs_ref[...])
        y = jax.nn.sigmoid(ga_ref[...]) * tf + jax.nn.sigmoid(gb_ref[...]) * ts
        hn_ref[...] = h_ref[...] + _dot(y.astype(BF16), wo_ref[...])

    row = lambda w, o=0: pl.BlockSpec((tm, w), lambda r, o=o: (r, o))
    res = lambda a: pl.BlockSpec(a.shape, lambda r: (0, 0))
    return pl.pallas_call(
        body, name=name, grid=(m // tm,),
        in_specs=[row(d), row(1024), row(512), row(d, cols.ga // d), row(d, cols.gb // d), res(wbf), res(wbs), res(wo)],
        out_specs=row(d),
        out_shape=jax.ShapeDtypeStruct((m, d), F32),
        compiler_params=_params(1),
    )(h, o_fox, o_swa, proj, proj, wbf, wbs, wo)


def branch_out_bwd(dh, o_fox, o_swa, proj, wbf, wbs, wo, cols, name):
    m, d = dh.shape
    tm = _row_tile(m)

    def body(dh_ref, of_ref, os_ref, ga_ref, gb_ref, wbf_ref, wbs_ref, wo_ref,
             y_ref, dtf_ref, dts_ref, dga_ref, dgb_ref, dof_ref, dos_ref, delta_ref):
        dy = _dot_nt(dh_ref[...].astype(BF16), wo_ref[...])
        tf = _dot(of_ref[...].astype(BF16), wbf_ref[...])
        ts = _dot(os_ref[...].astype(BF16), wbs_ref[...])
        sa = jax.nn.sigmoid(ga_ref[...])
        sb = jax.nn.sigmoid(gb_ref[...])
        y_ref[...] = (sa * tf + sb * ts).astype(BF16)
        dtf = (dy * sa).astype(BF16)
        dts = (dy * sb).astype(BF16)
        dtf_ref[...] = dtf
        dts_ref[...] = dts
        dga_ref[...] = (dy * tf * sa * (1.0 - sa)).astype(BF16)
        dgb_ref[...] = (dy * ts * sb * (1.0 - sb)).astype(BF16)
        dof = _dot_nt(dtf, wbf_ref[...])
        dof_ref[...] = dof.astype(BF16)
        dos_ref[...] = _dot_nt(dts, wbs_ref[...])
        lane = lax.broadcasted_iota(jnp.int32, (tm, LANES), 1)
        delta = jnp.zeros((tm, LANES), F32)
        for hd in range(8):
            sl = slice(hd * LANES, (hd + 1) * LANES)
            delta = jnp.where(lane == hd, jnp.sum(dof[:, sl] * of_ref[:, sl], axis=1, keepdims=True), delta)
        delta_ref[...] = delta

    row = lambda w, o=0: pl.BlockSpec((tm, w), lambda r, o=o: (r, o))
    res = lambda a: pl.BlockSpec(a.shape, lambda r: (0, 0))
    return pl.pallas_call(
        body, name=name, grid=(m // tm,),
        in_specs=[row(d), row(1024), row(512), row(d, cols.ga // d), row(d, cols.gb // d), res(wbf), res(wbs), res(wo)],
        out_specs=[row(d)] * 5 + [row(1024), row(512), row(LANES)],
        out_shape=[jax.ShapeDtypeStruct((m, d), BF16)] * 5 + [jax.ShapeDtypeStruct((m, 1024), BF16),
                   jax.ShapeDtypeStruct((m, 512), F32), jax.ShapeDtypeStruct((m, LANES), F32)],
        compiler_params=_params(1),
    )(dh, o_fox, o_swa, proj, proj, wbf, wbs, wo)


def loss_head(h, target, name):
    m, d = h.shape

    def body(h_ref, t_ref, dh_ref, loss_ref):
        n = pl.program_id(0)

        @pl.when(n == 0)
        def _():
            loss_ref[...] = jnp.zeros_like(loss_ref)
            dh_ref[...] = jnp.zeros_like(dh_ref)

        @pl.when(n > 0)
        def _():
            err = h_ref[...] - t_ref[...]
            dh_ref[...] = err * (1.0 / d)
            loss_ref[...] += jnp.sum(err * err, keepdims=True) * (0.5 / d)

    return pl.pallas_call(
        body, name=name, grid=(m // LANES,),
        in_specs=[pl.BlockSpec((LANES, d), lambda n: (n, 0)),
                  pl.BlockSpec((LANES, d), lambda n: (jnp.maximum(n - 1, 0), 0))],
        out_specs=[pl.BlockSpec((LANES, d), lambda n: (n, 0)), pl.BlockSpec((8, LANES), lambda n: (0, 0))],
        out_shape=[jax.ShapeDtypeStruct((m, d), F32), jax.ShapeDtypeStruct((8, LANES), F32)],
        compiler_params=_params(1),
    )(h, target)


def _adamw_math(w, g, m, v):
    m = ADAM_B1 * m + (1.0 - ADAM_B1) * g
    v = ADAM_B2 * v + (1.0 - ADAM_B2) * (g * g)
    m_hat = m / (1.0 - ADAM_B1 ** ADAM_STEP)
    v_hat = v / (1.0 - ADAM_B2 ** ADAM_STEP)
    delta = -ADAM_LR * (m_hat / (jnp.sqrt(v_hat) + ADAM_EPS) + ADAM_WD * w)
    return delta, m, v


def adamw_sum(parts, w, m, v, name):
    n_layers, a, b = w.shape
    ta = next(t for t in (256, 176, 128, a) if a % t == 0)
    nr = a // ta

    def body(*refs):
        p_refs = refs[:n_layers]
        w_ref, m_ref, v_ref, g_o, d_o, m_o, v_o = refs[n_layers:]
        for l in range(n_layers):
            @pl.when(pl.program_id(0) == l)
            def _(l=l):
                g = p_refs[l][0].astype(F32)
                for j in range(1, N_DEV):
                    g = g + p_refs[l][j].astype(F32)
                g_o[0] = g
                d_o[0], m_o[0], v_o[0] = _adamw_math(w_ref[0], g, m_ref[0], v_ref[0])

    def part_spec(l):
        return pl.BlockSpec((N_DEV, ta, b), lambda i, r, l=l: (0, jnp.where(i == l, r, jnp.where(i < l, 0, nr - 1)), 0))

    row = pl.BlockSpec((1, ta, b), lambda i, r: (i, r, 0))
    return pl.pallas_call(
        body, name=name, grid=(n_layers, nr),
        in_specs=[part_spec(l) for l in range(n_layers)] + [row, row, row],
        out_specs=[row] * 4,
        out_shape=[jax.ShapeDtypeStruct(w.shape, F32)] * 4,
        compiler_params=_params(2),
    )(*parts, w, m, v)


def adamw_small(g, w, m, v, name):
    def body(g_ref, w_ref, m_ref, v_ref, d_o, m_o, v_o):
        d_o[...], m_o[...], v_o[...] = _adamw_math(w_ref[...], g_ref[...], m_ref[...], v_ref[...])

    spec = pl.BlockSpec(memory_space=pltpu.VMEM)
    return pl.pallas_call(
        body, name=name, in_specs=[spec] * 4, out_specs=[spec] * 3,
        out_shape=[jax.ShapeDtypeStruct(w.shape, F32)] * 3,
    )(g, w, m, v)


BIG = ("ffn1_w_in", "ffn1_w_out", "w_in", "w_branch_fox", "w_branch_swa", "w_out", "ffn2_w_in", "ffn2_w_out")
SMALL = ("rel_bias_table", "ffn1_norm", "mix_norm", "forget_bias", "fox_q_norm", "fox_k_norm",
         "swa_q_norm", "swa_k_norm", "swa_sinks", "ffn2_norm")
WEIGHTS = ("meta_tokens", "rel_bias_table", "ffn1_norm", "ffn1_w_in", "ffn1_w_out", "mix_norm", "w_in",
           "forget_bias", "fox_q_norm", "fox_k_norm", "swa_q_norm", "swa_k_norm", "swa_sinks", "w_branch_fox",
           "w_branch_swa", "w_out", "ffn2_norm", "ffn2_w_in", "ffn2_w_out")


def _pack(arrs, width, row_multiple, dtype):
    lead = arrs[0].shape[:-1]
    flat = jnp.concatenate([a.astype(dtype) for a in arrs], axis=-1)
    n = flat.shape[-1]
    rows = -(-n // width)
    rows = -(-rows // row_multiple) * row_multiple
    flat = jnp.pad(flat, [(0, 0)] * len(lead) + [(0, rows * width - n)])
    return flat.reshape(lead + (rows, width))


def _unpack(flat, shapes):
    flat = flat.reshape(-1)
    out, off = [], 0
    for s in shapes:
        n = int(np.prod(s))
        out.append(flat[off:off + n].reshape(s))
        off += n
    return out


def _swa_head_order():
    return [4 * (j % 2) + j // 2 for j in range(8)]


def _permute_heads(a, axis, inverse=False):
    order = _swa_head_order()
    if inverse:
        order = [order.index(hd) for hd in range(8)]
    parts = [lax.slice_in_dim(a, hd * HEAD_DIM, (hd + 1) * HEAD_DIM, axis=axis) for hd in order]
    return jnp.concatenate(parts, axis=axis)


def _w_in_segments(cols):
    d = cols.d
    segs = [(512 * i + HEAD_DIM * hd, HEAD_DIM, new + LANES * hd)
            for i, new in enumerate((cols.qa, cols.ka, cols.va)) for hd in range(8)]
    segs.append((1536, 8, cols.fa))
    order = _swa_head_order()
    segs += [(1544 + HEAD_DIM * hd, HEAD_DIM, cols.qb + HEAD_DIM * order.index(hd)) for hd in range(8)]
    segs += [(2056, 128, cols.kb), (2184, 128, cols.vb), (2312, d, cols.ga), (2312 + d, d, cols.gb)]
    return segs


def _reorder_w_in(blocks, cols):
    width = blocks[0].shape[1]
    zeros = lambda n: jnp.zeros((blocks[0].shape[0], n), blocks[0].dtype)
    parts, at = [], 0
    for old, length, new in sorted(_w_in_segments(cols), key=lambda s: s[2]):
        if new > at:
            parts.append(zeros(new - at))
        at = new + length
        while length:
            j, off = divmod(old, width)
            take = min(length, width - off)
            parts.append(blocks[j][:, off:off + take])
            old, length = old + take, length - take
    parts.append(zeros(cols.np - at))
    return jnp.concatenate(parts, axis=1)


def _restore_w_in(wp, cols, width):
    segs = sorted(_w_in_segments(cols))
    blocks = []
    for j in range(N_DEV):
        lo, hi = j * width, (j + 1) * width
        parts = []
        for old, length, new in segs:
            a, b = max(old, lo), min(old + length, hi)
            if a < b:
                parts.append(wp[:, new + a - old:new + b - old])
        blocks.append(jnp.concatenate(parts, axis=1))
    return jnp.stack(blocks)


def _lane_pad(v):
    return jnp.pad(v, ((0, 0), (0, LANES - v.shape[1])))


def kernel(x, meta_tokens, rel_bias_table, ffn1_norm, ffn1_w_in, ffn1_w_out, mix_norm, w_in, forget_bias, fox_q_norm, fox_k_norm, swa_q_norm, swa_k_norm, swa_sinks, w_branch_fox, w_branch_swa, w_out, ffn2_norm, ffn2_w_in, ffn2_w_out, loss_target, m_meta_tokens, m_rel_bias_table, m_ffn1_norm, m_ffn1_w_in, m_ffn1_w_out, m_mix_norm, m_w_in, m_forget_bias, m_fox_q_norm, m_fox_k_norm, m_swa_q_norm, m_swa_k_norm, m_swa_sinks, m_w_branch_fox, m_w_branch_swa, m_w_out, m_ffn2_norm, m_ffn2_w_in, m_ffn2_w_out, v_meta_tokens, v_rel_bias_table, v_ffn1_norm, v_ffn1_w_in, v_ffn1_w_out, v_mix_norm, v_w_in, v_forget_bias, v_fox_q_norm, v_fox_k_norm, v_swa_q_norm, v_swa_k_norm, v_swa_sinks, v_w_branch_fox, v_w_branch_swa, v_w_out, v_ffn2_norm, v_ffn2_w_in, v_ffn2_w_out):
    args = dict(locals())
    wts = {n: args[n] for n in WEIGHTS}
    mom1 = {n: args["m_" + n] for n in WEIGHTS}
    mom2 = {n: args["v_" + n] for n in WEIGHTS}

    seq, d = x.shape[1], x.shape[2]
    m_rows = seq + LANES
    depth = ffn1_norm.shape[0]
    fb = ffn1_w_in.shape[2]
    fo = ffn1_w_out.shape[1]
    din_shard = w_in.shape[2]
    cols = _Cols(d)
    scale = HEAD_DIM ** -0.5
    dev = 4 * lax.axis_index("x") + 2 * lax.axis_index("y") + lax.axis_index("c")

    groups = {"ffn1": ("ffn1_w_in", "ffn1_w_out"), "mix": ("w_in", "w_branch_fox", "w_branch_swa", "w_out"),
              "ffn2": ("ffn2_w_in", "ffn2_w_out"), "ffn1_in": ("ffn1_w_in",), "ffn1_out": ("ffn1_w_out",),
              "ffn2_in": ("ffn2_w_in",), "ffn2_out": ("ffn2_w_out",)}
    shard = {n: wts[n].astype(BF16) for n in BIG}
    full, parts, gw = {}, {}, {}

    def keys_of(stages):
        return [(n, l) for g, l in stages if l < depth for n in groups[g]]

    def gather_rider(stages):
        return Rider([shard[n][l] for n, l in keys_of(stages)], True)

    def scatter_rider(stages):
        return Rider([gw[k] for k in keys_of(stages)], False)

    def ffn_weights(tag, l):
        return full[tag + "_w_in", l], full[tag + "_w_out", l].reshape(4, fb, d)

    def mixer_weights(l):
        wp = _reorder_w_in([full["w_in", l][j] for j in range(N_DEV)], cols)
        wbf = jnp.concatenate([full["w_branch_fox", l][j] for j in range(N_DEV)], axis=1)
        wbf = jnp.pad(wbf.reshape(8, HEAD_DIM, d), ((0, 0), (0, LANES - HEAD_DIM), (0, 0))).reshape(8 * LANES, d)
        wbs = _permute_heads(jnp.concatenate([full["w_branch_swa", l][j] for j in range(N_DEV)], axis=1), 0)
        return wp, wbf, wbs, full["w_out", l].reshape(d, d)

    full.update(zip(keys_of([("ffn1", 0)]), exchange_hbm(gather_rider([("ffn1", 0)]).srcs, True, "gather_first")))
    meta_all = gather_small(meta_tokens.reshape(1, N_META, -1), "gather_meta")
    meta_full = meta_all.transpose(1, 0, 2).reshape(N_META, d)
    tile8 = lambda g, s=1.0: jnp.tile(g.reshape(1, HEAD_DIM) * s, (1, 8))
    tile2 = lambda g: jnp.tile(g.reshape(1, HEAD_DIM), (1, 2))
    data_lanes = lambda g, s=1.0: _lane_pad(g.reshape(1, HEAD_DIM) * s)
    bias = bias_build(rel_bias_table, "swa_bias")

    first = jnp.concatenate([jnp.zeros((PAD_FRONT, d), F32), meta_full], axis=0)
    h = jnp.concatenate([first, x[0]], axis=0)
    saved, lw = [], []
    for l in range(depth):
        s, w = {"h0": h}, {}
        w["ffn1_in"], w["ffn1_out"] = ffn_weights("ffn1", l)
        stages = [("mix", 0)] if l == 0 else []
        (h, s["n1"], s["gate1"], s["up1"]), got = ffn_fwd(h, ffn1_norm[l:l + 1], w["ffn1_in"], w["ffn1_out"],
                                                          f"ffn1_fwd_{l}", gather_rider(stages))
        full.update(zip(keys_of(stages), got))
        s["h1"] = h
        w["wp"], w["wbf"], w["wbs"], w["wo"] = mixer_weights(l)
        s["nm"], s["proj"] = mixer_proj(h, mix_norm[l:l + 1], w["wp"], f"mixer_proj_{l}")
        s["gains"] = (data_lanes(fox_q_norm[l], scale), data_lanes(fox_k_norm[l]), tile8(swa_q_norm[l], scale),
                      tile2(swa_k_norm[l]))
        s["fbias"] = _lane_pad(forget_bias[l:l + 1])
        qf, kf, vf, qb, kb, vb = qk_post(s["proj"], s["gains"], s["fbias"], cols, f"qk_post_{l}")
        s.update(qf=qf, kf=kf, vf=vf, qb=qb, kb=kb, vb=vb)
        stages = [("ffn2", l), ("ffn1", l + 1)]
        (o_t, s["lse_fox"]), got = fox_fwd(qf, kf, vf.T, f"fox_fwd_{l}", gather_rider(stages))
        s["o_fox"] = o_t.T
        full.update(zip(keys_of(stages), got))
        s["o_swa"], s["lse_swa"] = swa_fwd(qb, kb, vb, bias, swa_sinks[l], f"swa_fwd_{l}")
        h = branch_out(h, s["o_fox"], s["o_swa"], s["proj"], w["wbf"], w["wbs"], w["wo"], cols, f"branch_out_{l}")
        s["h2"] = h
        w["ffn2_in"], w["ffn2_out"] = ffn_weights("ffn2", l)
        stages = [("mix", l + 1)]
        (h, s["n2"], s["gate2"], s["up2"]), got = ffn_fwd(h, ffn2_norm[l:l + 1], w["ffn2_in"], w["ffn2_out"],
                                                          f"ffn2_fwd_{l}", gather_rider(stages))
        full.update(zip(keys_of(stages), got))
        saved.append(s)
        lw.append(w)

    dh, loss_part = loss_head(h, loss_target[0], "loss_head")

    gs = {n: [None] * depth for n in SMALL}
    dbias_total = None
    for l in reversed(range(depth)):
        w, s = lw[l], saved[l]

        def ffn_back(dh, tag, hin, norm, n_in, gate, up, stages):
            (dh_in, a, dg, du, dgn, dhs), got = ffn_bwd(dh, hin, norm, gate, up, w[tag + "_in"], w[tag + "_out"],
                                                        f"{tag}_bwd_{l}", scatter_rider(stages))
            parts.update(zip(keys_of(stages), got))
            gw[tag + "_w_out", l] = matmul_tn(a, dhs[None], f"{tag}_dwo_{l}").reshape(N_DEV, fo, d)
            stages = [(tag + "_out", l)]
            gw[tag + "_w_in", l], got = matmul_tn(n_in[None], dg, f"{tag}_dwi_{l}", y2=du,
                                                  rider=scatter_rider(stages))
            parts.update(zip(keys_of(stages), got))
            return dh_in, dgn

        dh, gs["ffn2_norm"][l] = ffn_back(dh, "ffn2", s["h2"], ffn2_norm[l:l + 1], s["n2"], s["gate2"], s["up2"],
                                          [("ffn1_in", l + 1)])

        y, dtf, dts, dga, dgb, dof, dos, delta = branch_out_bwd(dh, s["o_fox"], s["o_swa"], s["proj"], w["wbf"],
                                                                w["wbs"], w["wo"], cols, f"branch_out_bwd_{l}")
        gw["w_out", l] = matmul_tn(y[None], dh[None], f"dw_out_{l}").reshape(N_DEV, d // N_DEV, d)
        to_shards = lambda a: a.reshape(512, N_DEV, d // N_DEV).transpose(1, 0, 2)
        gw["w_branch_fox", l] = to_shards(matmul_tn(s["o_fox"][None], dtf[None], f"dw_branch_fox_{l}")[0]
                                          .reshape(8, LANES, d)[:, :HEAD_DIM].reshape(512, d))
        gw["w_branch_swa", l] = to_shards(_permute_heads(
            matmul_tn(s["o_swa"][None], dts[None], f"dw_branch_swa_{l}")[0], 0, inverse=True))

        stages = [("ffn2_in", l)]
        (dkf, dvf, dqf_t), got = fox_bwd(s["qf"], s["kf"], s["vf"], s["kf"].T, dof, s["lse_fox"],
                                         delta[:, :8].T.reshape(8, 1, m_rows), f"fox_bwd_{l}", scatter_rider(stages))
        parts.update(zip(keys_of(stages), got))
        dc = _lane_pad(dqf_t.reshape(8, LANES, m_rows)[:, AUG].T - dkf.reshape(m_rows, 8, LANES)[:, :, AUG + 3])
        dqb, dkb, dvb, dbias, dsink = swa_bwd(s["qb"], s["kb"], s["vb"], bias, swa_sinks[l], s["o_swa"], s["lse_swa"],
                                              dos, f"swa_bwd_{l}")
        dbias_total = dbias if dbias_total is None else dbias_total + dbias
        gs["swa_sinks"][l] = dsink[0, :8]
        dproj, ggqa, ggka, ggqb, ggkb, gfb = qk_post_bwd(s["proj"], s["gains"], s["fbias"], dqf_t.T, dkf, dvf, dqb, dkb,
                                                         dvb, dc, dga, dgb, cols, f"qk_post_bwd_{l}")
        gs["fox_q_norm"][l] = ggqa[0, :HEAD_DIM] * scale
        gs["fox_k_norm"][l] = ggka[0, :HEAD_DIM]
        gs["swa_q_norm"][l] = ggqb.reshape(8, HEAD_DIM).sum(0) * scale
        gs["swa_k_norm"][l] = ggkb.reshape(2, HEAD_DIM).sum(0)
        gs["forget_bias"][l] = gfb[0, :8]
        dwp = matmul_tn(s["nm"][None], dproj[None], f"dw_in_{l}", tn=1024 if cols.np % 1024 == 0 else None)[0]
        gw["w_in", l] = _restore_w_in(dwp, cols, din_shard)
        dh, gs["mix_norm"][l] = dproj_bwd(dh, s["h1"], mix_norm[l:l + 1], dproj, w["wp"], f"dproj_bwd_{l}")

        dh, gs["ffn1_norm"][l] = ffn_back(dh, "ffn1", s["h0"], ffn1_norm[l:l + 1], s["n1"], s["gate1"], s["up1"],
                                          [("mix", l)])

    grad_x = dh[LANES:][None]
    dmeta = dh[PAD_FRONT:LANES]
    dtable = bias_reduce(dbias_total, "swa_dbias")[:, :8]

    parts.update(zip(keys_of([("ffn1_in", 0)]),
                     exchange_hbm(scatter_rider([("ffn1_in", 0)]).srcs, False, "scatter_last")))
    big_out = [{}, {}, {}, {}]
    for n in BIG:
        outs = adamw_sum([parts[n, l] for l in range(depth)], wts[n], mom1[n], mom2[n], f"adamw_{n}")
        for k in range(4):
            big_out[k][n] = outs[k]

    small_g = {n: (jnp.stack(gs[n]) if n != "rel_bias_table" else None) for n in SMALL}
    small_g["rel_bias_table"] = dtable
    pieces = [loss_part[0:1, 0:1].reshape(1, 1)] + [small_g[n].reshape(1, -1) for n in SMALL] + [dmeta.reshape(1, -1)]
    small_shapes = [(1,)] + [wts[n].shape for n in SMALL] + [(N_META, d)]
    total = allsum_small(_pack(pieces, LANES, 8, F32), "allsum_small")
    summed = _unpack(total, small_shapes)
    loss = summed[0][0]
    g_small = dict(zip(SMALL, summed[1:1 + len(SMALL)]))
    g_meta = lax.dynamic_slice_in_dim(summed[-1], dev * (d // N_DEV), d // N_DEV, axis=1)
    names = SMALL + ("meta_tokens",)
    g_small["meta_tokens"] = g_meta
    pk = lambda src: _pack([src[n].reshape(1, -1) for n in names], LANES, 8, F32)[0]
    small_out = [dict(zip(names, _unpack(o, [wts[n].shape for n in names])))
                 for o in adamw_small(pk(g_small), pk(wts), pk(mom1), pk(mom2), "adamw_small")]

    grads = {**big_out[0], **g_small}
    delta = {**big_out[1], **small_out[0]}
    new_m = {**big_out[2], **small_out[1]}
    new_v = {**big_out[3], **small_out[2]}
    return (loss, grad_x, *[grads[n] for n in WEIGHTS], *[delta[n] for n in WEIGHTS],
            *[new_m[n] for n in WEIGHTS], *[new_v[n] for n in WEIGHTS])
```

```python
import math

import numpy as np
import jax
import jax.numpy as jnp
from jax import lax
from jax.experimental import pallas as pl
from jax.experimental.pallas import tpu as pltpu

F32 = jnp.float32
BF16 = jnp.bfloat16
EPS = 1e-6
NEG = -1e30
HEAD_DIM = 64
LANES = 128
N_META = 16
PAD_FRONT = LANES - N_META
N_BUCKETS = 32
MAX_DISTANCE = 128
N_DEV = 8
ADAM_LR, ADAM_B1, ADAM_B2, ADAM_EPS, ADAM_WD, ADAM_STEP = 0.001, 0.9, 0.999, 1e-08, 0.01, 10
VMEM_LIMIT = 56 * 1024 * 1024
MESH = pl.DeviceIdType.MESH


def _params(n_grid):
    return pltpu.CompilerParams(dimension_semantics=("arbitrary",) * n_grid,
                                vmem_limit_bytes=VMEM_LIMIT)


def _dot(a, b):
    return jnp.dot(a, b, preferred_element_type=F32)


def _dot_nt(a, b):
    return lax.dot_general(a, b, (((1,), (1,)), ((), ())), preferred_element_type=F32)


def _dot_tn(a, b):
    return lax.dot_general(a, b, (((0,), (0,)), ((), ())), preferred_element_type=F32)


def _rms(x):
    r = lax.rsqrt(jnp.mean(x * x, axis=-1, keepdims=True) + EPS)
    return x * r, r


def _rms_bwd(x, g, dn):
    xh, r = _rms(x)
    dxh = dn * g
    dx = r * (dxh - xh * jnp.mean(dxh * xh, axis=-1, keepdims=True))
    return dx, jnp.sum(dn * xh, axis=0, keepdims=True)


def _split2(v):
    hi = v.astype(BF16)
    return hi, (v - hi.astype(F32)).astype(BF16)


def _split3(v):
    hi = v.astype(BF16)
    r1 = v - hi.astype(F32)
    mid = r1.astype(BF16)
    return hi, mid, (r1 - mid.astype(F32)).astype(BF16)


def _group_ones():
    r = lax.broadcasted_iota(jnp.int32, (LANES, LANES), 0) // HEAD_DIM
    c = lax.broadcasted_iota(jnp.int32, (LANES, LANES), 1) // HEAD_DIM
    return jnp.where(r == c, 1.0, 0.0).astype(BF16)


def _group_mean(v, ones):
    hi, lo = _split2(v)
    return (_dot(hi, ones) + _dot(lo, ones)) * (1.0 / HEAD_DIM)


def _row_tile(m):
    return 384 if m % 384 == 0 else LANES


def _tile(m, cap):
    return max(t for t in range(16, cap + 1, 16) if m % t == 0)


def _peer(k):
    x, y, c = lax.axis_index("x"), lax.axis_index("y"), lax.axis_index("c")
    px = 1 - x if k & 4 else x
    py = 1 - y if k & 2 else y
    pc = 1 - c if k & 1 else c
    return (px, py, pc), 4 * px + 2 * py + pc


def _exchange_body(src_ref, dst_ref, send_sems, recv_sems, local_sem, bcast):
    x, y, c = lax.axis_index("x"), lax.axis_index("y"), lax.axis_index("c")
    me = 4 * x + 2 * y + c
    mine = pltpu.make_async_copy(src_ref.at[0 if bcast else me], dst_ref.at[me], local_sem)
    mine.start()
    sends = []
    for k in range(1, N_DEV):
        dev, idx = _peer(k)
        cp = pltpu.make_async_remote_copy(
            src_ref=src_ref.at[0 if bcast else idx], dst_ref=dst_ref.at[me],
            send_sem=send_sems.at[k - 1], recv_sem=recv_sems.at[k - 1],
            device_id=dev, device_id_type=MESH)
        cp.start()
        sends.append(cp)
    for k in range(1, N_DEV):
        dev, idx = _peer(k)
        pltpu.make_async_remote_copy(
            src_ref=src_ref.at[0], dst_ref=dst_ref.at[idx],
            send_sem=send_sems.at[k - 1], recv_sem=recv_sems.at[k - 1],
            device_id=dev, device_id_type=MESH).wait_recv()
    for cp in sends:
        cp.wait_send()
    mine.wait()


class Rider:
    FIRST = (1, 2, 4, 6)
    RELAYED = (2, 4, 6)

    def __init__(self, srcs=(), bcast=True):
        self.srcs, self.bcast, self.n = list(srcs), bcast, len(srcs)

    def out_shapes(self):
        return [jax.ShapeDtypeStruct(((N_DEV,) + s.shape) if self.bcast else s.shape, s.dtype) for s in self.srcs]

    def specs(self):
        return [pl.BlockSpec(memory_space=pl.ANY)] * self.n

    def scratch(self):
        if not self.n:
            return []
        return [pltpu.SemaphoreType.DMA((self.n * (N_DEV - 1),)), pltpu.SemaphoreType.DMA((self.n * (N_DEV - 1),)),
                pltpu.SemaphoreType.DMA((self.n,))]

    @staticmethod
    def _copy(src, dst, a, pair, dev, send_sems, recv_sems):
        sem = a * (N_DEV - 1) + pair - 1
        return pltpu.make_async_remote_copy(src_ref=src, dst_ref=dst, send_sem=send_sems.at[sem],
                                            recv_sem=recv_sems.at[sem], device_id=dev, device_id_type=MESH)

    def _first(self):
        return self.FIRST if self.bcast else range(1, N_DEV)

    def _own(self, s, d, a, local_sems):
        me = 4 * lax.axis_index("x") + 2 * lax.axis_index("y") + lax.axis_index("c")
        return pltpu.make_async_copy(s if self.bcast else s.at[me], d.at[me], local_sems.at[a]), me

    def start(self, src_refs, dst_refs, send_sems, recv_sems, local_sems):
        for a, (s, d) in enumerate(zip(src_refs, dst_refs)):
            own, me = self._own(s, d, a, local_sems)
            own.start()
            for k in self._first():
                dev, idx = _peer(k)
                self._copy(s if self.bcast else s.at[idx], d.at[me], a, k, dev, send_sems, recv_sems).start()

    def relay(self, src_refs, dst_refs, send_sems, recv_sems, local_sems):
        if not self.bcast:
            return
        sibling, _ = _peer(1)
        for a, d in enumerate(dst_refs):
            for k in self.RELAYED:
                dev, idx = _peer(k)
                self._copy(d.at[idx], d.at[idx], a, k, dev, send_sems, recv_sems).wait_recv()
                self._copy(d.at[idx], d.at[idx], a, k + 1, sibling, send_sems, recv_sems).start()

    def wait(self, src_refs, dst_refs, send_sems, recv_sems, local_sems):
        sibling, _ = _peer(1)
        for a, (s, d) in enumerate(zip(src_refs, dst_refs)):
            own, me = self._own(s, d, a, local_sems)
            for k in range(1, N_DEV):
                if not (self.bcast and k in self.RELAYED):
                    dev, idx = _peer(k)
                    self._copy(d.at[idx], d.at[idx], a, k, dev, send_sems, recv_sems).wait_recv()
            for k in self._first():
                dev, idx = _peer(k)
                self._copy(s if self.bcast else s.at[idx], d.at[me], a, k, dev, send_sems, recv_sems).wait_send()
            if self.bcast:
                for k in self.RELAYED:
                    dev, idx = _peer(k)
                    self._copy(d.at[idx], d.at[idx], a, k + 1, sibling, send_sems, recv_sems).wait_send()
            own.wait()


def rider_call(core, name, grid, in_specs, out_specs, out_shape, scratch_shapes, args, rider=None):
    rider = rider or Rider()
    n_in, n_out, n_scr, nr = len(in_specs), len(out_specs), len(scratch_shapes), rider.n

    def body(*refs):
        ins, r_src = refs[:n_in], refs[n_in:n_in + nr]
        outs = refs[n_in + nr:n_in + nr + n_out]
        r_dst = refs[n_in + nr + n_out:n_in + 2 * nr + n_out]
        scr = refs[n_in + 2 * nr + n_out:n_in + 2 * nr + n_out + n_scr]
        sems = refs[n_in + 2 * nr + n_out + n_scr:]
        if nr:
            first, relay, last = True, True, True
            for ax, size in enumerate(grid):
                first = first & (pl.program_id(ax) == 0)
                relay = relay & (pl.program_id(ax) == (3 * size // 4 if ax == 0 else 0))
                last = last & (pl.program_id(ax) == size - 1)
            if not grid:
                rider.start(r_src, r_dst, *sems)
                rider.relay(r_src, r_dst, *sems)
            else:
                pl.when(first)(lambda: rider.start(r_src, r_dst, *sems))
                if rider.bcast:
                    pl.when(relay)(lambda: rider.relay(r_src, r_dst, *sems))
        core(*ins, *outs, *scr)
        if nr:
            if not grid:
                rider.wait(r_src, r_dst, *sems)
            else:
                pl.when(last)(lambda: rider.wait(r_src, r_dst, *sems))

    res = pl.pallas_call(
        body, name=name, grid=grid,
        in_specs=list(in_specs) + rider.specs(),
        out_specs=list(out_specs) + rider.specs(),
        out_shape=list(out_shape) + rider.out_shapes(),
        scratch_shapes=list(scratch_shapes) + rider.scratch(),
        compiler_params=_params(len(grid)),
    )(*args, *rider.srcs)
    return res[:n_out], res[n_out:]


def exchange_hbm(srcs, bcast, name):
    return rider_call(lambda: None, name, (), [], [], [], [], [], Rider(srcs, bcast))[1]


def allsum_small(vec, name):
    def body(src_ref, out_ref, dst_ref, send_sems, recv_sems, local_sem):
        _exchange_body(src_ref, dst_ref, send_sems, recv_sems, local_sem, True)
        acc = dst_ref[0]
        for j in range(1, N_DEV):
            acc = acc + dst_ref[j]
        out_ref[...] = acc

    return pl.pallas_call(
        body, name=name,
        out_shape=jax.ShapeDtypeStruct(vec.shape[1:], F32),
        in_specs=[pl.BlockSpec(memory_space=pltpu.VMEM)],
        out_specs=pl.BlockSpec(memory_space=pltpu.VMEM),
        scratch_shapes=[pltpu.VMEM((N_DEV,) + vec.shape[1:], F32),
                        pltpu.SemaphoreType.DMA((N_DEV - 1,)), pltpu.SemaphoreType.DMA((N_DEV - 1,)),
                        pltpu.SemaphoreType.DMA],
    )(vec)


def gather_small(vec, name):
    def body(src_ref, dst_ref, send_sems, recv_sems, local_sem):
        _exchange_body(src_ref, dst_ref, send_sems, recv_sems, local_sem, True)

    return pl.pallas_call(
        body, name=name,
        out_shape=jax.ShapeDtypeStruct((N_DEV,) + vec.shape[1:], F32),
        in_specs=[pl.BlockSpec(memory_space=pltpu.VMEM)],
        out_specs=pl.BlockSpec(memory_space=pltpu.VMEM),
        scratch_shapes=[pltpu.SemaphoreType.DMA((N_DEV - 1,)), pltpu.SemaphoreType.DMA((N_DEV - 1,)),
                        pltpu.SemaphoreType.DMA],
    )(vec)


FFN_FWD_ROWS = 1056
FFN_BWD_ROWS = 704
DW_ROWS = 1408
PROJ_ROWS = 1056
PROJ_COLS = 1024

def ffn_fwd(h, g, w_in8, w_out4, name, rider=None):
    m, d = h.shape
    fb = w_in8.shape[2]
    tm = _tile(m, FFN_FWD_ROWS)

    def body(h_ref, g_ref, wg_ref, wu_ref, wo_ref, hn_ref, n_ref, gate_ref, up_ref, acc_ref):
        i = pl.program_id(1)

        @pl.when(i == 0)
        def _():
            xh, _ = _rms(h_ref[...])
            n_ref[...] = (xh * g_ref[...]).astype(BF16)
            acc_ref[...] = jnp.zeros_like(acc_ref)

        n = n_ref[...]
        gate = _dot(n, wg_ref[0])
        up = _dot(n, wu_ref[0])
        gate_ref[0] = gate
        up_ref[0] = up
        a = (gate * jax.nn.sigmoid(gate) * up).astype(BF16)
        acc_ref[...] += _dot(a, wo_ref[0])

        @pl.when(i == 3)
        def _():
            hn_ref[...] = h_ref[...] + 0.5 * acc_ref[...]

    return rider_call(
        body, name, (m // tm, 4),
        in_specs=[pl.BlockSpec((tm, d), lambda r, i: (r, 0)),
                  pl.BlockSpec((1, d), lambda r, i: (0, 0)),
                  pl.BlockSpec((1, d, fb), lambda r, i: (i, 0, 0)),
                  pl.BlockSpec((1, d, fb), lambda r, i: (i + 4, 0, 0)),
                  pl.BlockSpec((1, fb, d), lambda r, i: (i, 0, 0))],
        out_specs=[pl.BlockSpec((tm, d), lambda r, i: (r, 0)),
                   pl.BlockSpec((tm, d), lambda r, i: (r, 0)),
                   pl.BlockSpec((1, tm, fb), lambda r, i: (i, r, 0)),
                   pl.BlockSpec((1, tm, fb), lambda r, i: (i, r, 0))],
        out_shape=[jax.ShapeDtypeStruct((m, d), F32), jax.ShapeDtypeStruct((m, d), BF16),
                   jax.ShapeDtypeStruct((4, m, fb), F32), jax.ShapeDtypeStruct((4, m, fb), F32)],
        scratch_shapes=[pltpu.VMEM((tm, d), F32)],
        args=(h, g, w_in8, w_in8, w_out4), rider=rider)


def ffn_bwd(dh, h, g, gate, up, w_in8, w_out4, name, rider=None):
    m, d = h.shape
    fb = w_in8.shape[2]
    tm = _tile(m, FFN_BWD_ROWS)

    def body(dh_ref, h_ref, g_ref, gate_ref, up_ref, wg_ref, wu_ref, wo_ref,
             dhin_ref, a_ref, dg_ref, du_ref, dgn_ref, dhs_ref, acc_ref):
        r = pl.program_id(0)
        i = pl.program_id(1)

        @pl.when(i == 0)
        def _():
            dhs_ref[...] = (0.5 * dh_ref[...]).astype(BF16)
            acc_ref[...] = jnp.zeros_like(acc_ref)

        @pl.when((r == 0) & (i == 0))
        def _():
            dgn_ref[...] = jnp.zeros_like(dgn_ref)

        da = _dot_nt(dhs_ref[...], wo_ref[0])
        gt = gate_ref[0]
        u = up_ref[0]
        sg = jax.nn.sigmoid(gt)
        sl = gt * sg
        a_ref[0] = (sl * u).astype(BF16)
        dub = (da * sl).astype(BF16)
        dgb = (da * u * (sg * (1.0 + gt * (1.0 - sg)))).astype(BF16)
        dg_ref[0] = dgb
        du_ref[0] = dub
        acc_ref[...] += _dot_nt(dgb, wg_ref[0]) + _dot_nt(dub, wu_ref[0])

        @pl.when(i == 3)
        def _():
            dx, dgain = _rms_bwd(h_ref[...], g_ref[...], acc_ref[...])
            dgn_ref[...] += dgain
            dhin_ref[...] = dh_ref[...] + dx

    row = lambda r, i: (r, 0)
    blk = lambda r, i: (i, r, 0)
    return rider_call(
        body, name, (m // tm, 4),
        in_specs=[pl.BlockSpec((tm, d), row), pl.BlockSpec((tm, d), row),
                  pl.BlockSpec((1, d), lambda r, i: (0, 0)),
                  pl.BlockSpec((1, tm, fb), blk), pl.BlockSpec((1, tm, fb), blk),
                  pl.BlockSpec((1, d, fb), lambda r, i: (i, 0, 0)),
                  pl.BlockSpec((1, d, fb), lambda r, i: (i + 4, 0, 0)),
                  pl.BlockSpec((1, fb, d), lambda r, i: (i, 0, 0))],
        out_specs=[pl.BlockSpec((tm, d), row),
                   pl.BlockSpec((1, tm, fb), blk), pl.BlockSpec((1, tm, fb), blk),
                   pl.BlockSpec((1, tm, fb), blk),
                   pl.BlockSpec((1, d), lambda r, i: (0, 0)),
                   pl.BlockSpec((tm, d), row)],
        out_shape=[jax.ShapeDtypeStruct((m, d), F32),
                   jax.ShapeDtypeStruct((4, m, fb), BF16), jax.ShapeDtypeStruct((4, m, fb), BF16),
                   jax.ShapeDtypeStruct((4, m, fb), BF16),
                   jax.ShapeDtypeStruct((1, d), F32), jax.ShapeDtypeStruct((m, d), BF16)],
        scratch_shapes=[pltpu.VMEM((tm, d), F32)],
        args=(dh, h, g, gate, up, w_in8, w_in8, w_out4), rider=rider)


def matmul_tn(x, y, name, tn=None, y2=None, rider=None):
    bx, m, k = x.shape
    by, _, n = y.shape
    b = max(bx, by) * (2 if y2 is not None else 1)
    tm = _tile(m, DW_ROWS)
    tn = n if tn is None else tn
    nt = n // tn
    nr = m // tm

    def body(*refs):
        x_ref, y_ref = refs[0], refs[1]
        o_ref, acc_ref = refs[-2], refs[-1]
        r = pl.program_id(2)

        @pl.when(r == 0)
        def _():
            acc_ref[...] = jnp.zeros_like(acc_ref)

        if y2 is None:
            acc_ref[...] += _dot_tn(x_ref[0].astype(BF16), y_ref[0].astype(BF16))
        else:
            @pl.when(pl.program_id(0) < by)
            def _():
                acc_ref[...] += _dot_tn(x_ref[0].astype(BF16), y_ref[0].astype(BF16))

            @pl.when(pl.program_id(0) >= by)
            def _():
                acc_ref[...] += _dot_tn(x_ref[0].astype(BF16), refs[2][0].astype(BF16))

        @pl.when(r == nr - 1)
        def _():
            o_ref[0] = acc_ref[...].astype(BF16)

    x_map = (lambda i, j, r: (i, r, 0)) if bx > 1 else (lambda i, j, r: (0, r, 0))
    if y2 is None:
        y_specs = [pl.BlockSpec((1, tm, tn), (lambda i, j, r: (i, r, j)) if by > 1 else (lambda i, j, r: (0, r, j)))]
    else:
        y_specs = [pl.BlockSpec((1, tm, tn), lambda i, j, r: (jnp.minimum(i, by - 1), jnp.where(i < by, r, nr - 1), j)),
                   pl.BlockSpec((1, tm, tn), lambda i, j, r: (jnp.maximum(i - by, 0), jnp.where(i < by, 0, r), j))]
    (out,), carried = rider_call(
        body, name, (b, nt, nr),
        in_specs=[pl.BlockSpec((1, tm, k), x_map)] + y_specs,
        out_specs=[pl.BlockSpec((1, k, tn), lambda i, j, r: (i, 0, j))],
        out_shape=[jax.ShapeDtypeStruct((b, k, n), BF16)],
        scratch_shapes=[pltpu.VMEM((k, tn), F32)],
        args=[x, y] + ([y2] if y2 is not None else []), rider=rider)
    return (out, carried) if rider is not None else out


AUG = HEAD_DIM


class _Cols:
    def __init__(self, d):
        self.d = d
        self.ga, self.gb = 0, d
        self.qa, self.ka, self.va = 2 * d, 2 * d + 1024, 2 * d + 2048
        self.qb = 2 * d + 3072
        self.kb, self.vb, self.fa = self.qb + 512, self.qb + 640, self.qb + 768
        self.np = self.qb + 1024


def mixer_proj(h, g, wp, name):
    m, d = h.shape
    npad = wp.shape[1]
    tm = _tile(m, PROJ_ROWS)
    tn = PROJ_COLS if npad % PROJ_COLS == 0 else npad

    def body(h_ref, g_ref, w_ref, n_ref, p_ref):
        @pl.when(pl.program_id(1) == 0)
        def _():
            xh, _ = _rms(h_ref[...])
            n_ref[...] = (xh * g_ref[...]).astype(BF16)

        p_ref[...] = _dot(n_ref[...], w_ref[...])

    return pl.pallas_call(
        body, name=name, grid=(m // tm, npad // tn),
        in_specs=[pl.BlockSpec((tm, d), lambda r, j: (r, 0)), pl.BlockSpec((1, d), lambda r, j: (0, 0)),
                  pl.BlockSpec((d, tn), lambda r, j: (0, j))],
        out_specs=[pl.BlockSpec((tm, d), lambda r, j: (r, 0)), pl.BlockSpec((tm, tn), lambda r, j: (r, j))],
        out_shape=[jax.ShapeDtypeStruct((m, d), BF16), jax.ShapeDtypeStruct((m, npad), F32)],
        compiler_params=_params(2),
    )(h, g, wp)


def _head_norm(x, gain, ones):
    outs = []
    for b in range(x.shape[1] // LANES):
        xb = x[:, b * LANES:(b + 1) * LANES]
        r = lax.rsqrt(_group_mean(xb * xb, ones) + EPS)
        outs.append(xb * r * gain[:, b * LANES:(b + 1) * LANES])
    return outs


def _head_norm_bwd(x, gain, dn, ones):
    dxs, dgs = [], []
    for b in range(x.shape[1] // LANES):
        sl = slice(b * LANES, (b + 1) * LANES)
        xb, dnb = x[:, sl], dn[:, sl]
        r = lax.rsqrt(_group_mean(xb * xb, ones) + EPS)
        xh = xb * r
        dxh = dnb * gain[:, sl]
        dxs.append(r * (dxh - xh * _group_mean(dxh * xh, ones)))
        dgs.append(jnp.sum(dnb * xh, axis=0, keepdims=True))
    return dxs, dgs


def _lane_col(v, lane_iota, idx):
    return jnp.sum(jnp.where(lane_iota == idx, v, 0.0), axis=1, keepdims=True)


def _aug(base, lane, vals):
    for i, v in enumerate(vals):
        base = jnp.where(lane == AUG + i, v, base)
    return base


def qk_post(proj, gains, fbias, cols, name):
    m = proj.shape[0]
    tm = _row_tile(m)
    gqa, gka, gqb, gkb = gains

    def body(qa_ref, ka_ref, va_ref, qb_ref, kb_ref, vb_ref, fa_ref, gqa_ref, gka_ref, gqb_ref, gkb_ref, fb_ref,
             qf_o, kf_o, vf_o, qb_o, kb_o, vb_o, carry_ref):
        r0 = pl.program_id(0)

        @pl.when(r0 == 0)
        def _():
            carry_ref[...] = jnp.zeros_like(carry_ref)

        z = fa_ref[...] + fb_ref[...]
        logf = jnp.minimum(z, 0.0) - jnp.log(1.0 + jnp.exp(-jnp.abs(z)))
        rr = lax.broadcasted_iota(jnp.int32, (tm, tm), 0)
        cc = lax.broadcasted_iota(jnp.int32, (tm, tm), 1)
        tril = jnp.where(cc <= rr, 1.0, 0.0).astype(BF16)
        p0, p1, p2 = _split3(logf)
        c = _dot(tril, p0) + _dot(tril, p1) + _dot(tril, p2) + carry_ref[...]
        carry_ref[...] += jnp.sum(logf, axis=0, keepdims=True)

        lane = lax.broadcasted_iota(jnp.int32, (tm, LANES), 1)
        is_pad = (r0 * tm + lax.broadcasted_iota(jnp.int32, (tm, 1), 0)) < PAD_FRONT
        ones = jnp.ones((LANES, LANES), BF16)
        for hd in range(8):
            sl = slice(hd * LANES, (hd + 1) * LANES)
            ch = _lane_col(c, lane, hd)
            ct = [p.astype(F32) for p in _split3(ch)]
            cs = [p.astype(F32) for p in _split3(-jnp.where(is_pad, -NEG, ch))]
            xq = qa_ref[:, sl]
            qn = xq * lax.rsqrt(_group_mean(xq * xq, ones) + EPS) * gqa_ref[...]
            qf_o[:, sl] = _aug(qn, lane, ct + [1.0, 1.0, 1.0]).astype(BF16)
            xk = ka_ref[:, sl]
            kn = xk * lax.rsqrt(_group_mean(xk * xk, ones) + EPS) * gka_ref[...]
            kf_o[:, sl] = _aug(kn, lane, [1.0, 1.0, 1.0] + cs).astype(BF16)
            vf_o[:, sl] = _aug(va_ref[:, sl], lane, [1.0, 1.0, 1.0]).astype(BF16)

        gones = _group_ones()
        for src, gn, dst in ((qb_ref, gqb_ref, qb_o), (kb_ref, gkb_ref, kb_o)):
            for b, blk in enumerate(_head_norm(src[...], gn[...], gones)):
                dst[:, b * LANES:(b + 1) * LANES] = blk.astype(BF16)
        vb_o[...] = vb_ref[...].astype(BF16)

    w1024 = lambda off: pl.BlockSpec((tm, 1024), lambda r, o=off // 1024: (r, o))
    w512 = lambda off: pl.BlockSpec((tm, 512), lambda r, o=off // 512: (r, o))
    w128 = lambda off: pl.BlockSpec((tm, LANES), lambda r, o=off // LANES: (r, o))
    vec = lambda w: pl.BlockSpec((1, w), lambda r: (0, 0))
    row = lambda w: pl.BlockSpec((tm, w), lambda r: (r, 0))
    return pl.pallas_call(
        body, name=name, grid=(m // tm,),
        in_specs=[w1024(cols.qa), w1024(cols.ka), w1024(cols.va), w512(cols.qb), w128(cols.kb), w128(cols.vb),
                  w128(cols.fa), vec(LANES), vec(LANES), vec(512), vec(LANES), vec(LANES)],
        out_specs=[row(1024), row(1024), row(1024), row(512), row(LANES), row(LANES)],
        out_shape=[jax.ShapeDtypeStruct((m, 1024), BF16)] * 3 + [jax.ShapeDtypeStruct((m, 512), BF16)]
                  + [jax.ShapeDtypeStruct((m, LANES), BF16)] * 2,
        scratch_shapes=[pltpu.VMEM((1, LANES), F32)],
        compiler_params=_params(1),
    )(proj, proj, proj, proj, proj, proj, proj, gqa, gka, gqb, gkb, fbias)


def qk_post_bwd(proj, gains, fbias, dqf, dkf, dvf, dqb, dkb, dvb, dc, dga, dgb, cols, name):
    m = proj.shape[0]
    d = cols.d
    tm = _row_tile(m)
    nt = m // tm
    gqa, gka, gqb, gkb = gains

    def body(qa_ref, ka_ref, qb_ref, kb_ref, fa_ref, gqa_ref, gka_ref, gqb_ref, gkb_ref, fb_ref,
             dqf_ref, dkf_ref, dvf_ref, dqb_ref, dkb_ref, dvb_ref, dc_ref, dga_ref, dgb_ref,
             dp_o, ggqa_o, ggka_o, ggqb_o, ggkb_o, gfb_o, carry_ref):
        @pl.when(pl.program_id(0) == 0)
        def _():
            carry_ref[...] = jnp.zeros_like(carry_ref)
            for o in (ggqa_o, ggka_o, ggqb_o, ggkb_o, gfb_o):
                o[...] = jnp.zeros_like(o)

        dp_o[:, cols.ga:cols.ga + d] = dga_ref[...].astype(BF16)
        dp_o[:, cols.gb:cols.gb + d] = dgb_ref[...].astype(BF16)
        dp_o[:, cols.fa + LANES:cols.np] = jnp.zeros((tm, cols.np - cols.fa - LANES), BF16)
        lane = lax.broadcasted_iota(jnp.int32, (tm, LANES), 1)
        data = lane < HEAD_DIM
        ones = jnp.ones((LANES, LANES), BF16)
        for hd in range(8):
            sl = slice(hd * LANES, (hd + 1) * LANES)
            for src, gn, dn_ref, off, gout in ((qa_ref, gqa_ref, dqf_ref, cols.qa, ggqa_o),
                                               (ka_ref, gka_ref, dkf_ref, cols.ka, ggka_o)):
                x = src[:, sl]
                dn = jnp.where(data, dn_ref[:, sl], 0.0)
                r = lax.rsqrt(_group_mean(x * x, ones) + EPS)
                xh = x * r
                dxh = dn * gn[...]
                dp_o[:, off + hd * LANES:off + (hd + 1) * LANES] = (
                    r * (dxh - xh * _group_mean(dxh * xh, ones))).astype(BF16)
                gout[...] += jnp.sum(dn * xh, axis=0, keepdims=True)
            dp_o[:, cols.va + hd * LANES:cols.va + (hd + 1) * LANES] = jnp.where(data, dvf_ref[:, sl], 0.0).astype(BF16)
        dp_o[:, cols.vb:cols.vb + LANES] = dvb_ref[...].astype(BF16)
        gones = _group_ones()
        for src, gn, dn, off, gout in ((qb_ref, gqb_ref, dqb_ref, cols.qb, ggqb_o),
                                       (kb_ref, gkb_ref, dkb_ref, cols.kb, ggkb_o)):
            dxs, dgs = _head_norm_bwd(src[...], gn[...], dn[...], gones)
            for b, (dx, dg) in enumerate(zip(dxs, dgs)):
                dp_o[:, off + b * LANES:off + (b + 1) * LANES] = dx.astype(BF16)
                gout[:, b * LANES:(b + 1) * LANES] += dg
        dcv = dc_ref[...]
        rr = lax.broadcasted_iota(jnp.int32, (tm, tm), 0)
        cc = lax.broadcasted_iota(jnp.int32, (tm, tm), 1)
        triu = jnp.where(cc >= rr, 1.0, 0.0).astype(BF16)
        p0, p1, p2 = _split3(dcv)
        dlogf = _dot(triu, p0) + _dot(triu, p1) + _dot(triu, p2) + carry_ref[...]
        carry_ref[...] += jnp.sum(dcv, axis=0, keepdims=True)
        z = fa_ref[...] + fb_ref[...]
        row = (nt - 1 - pl.program_id(0)) * tm + lax.broadcasted_iota(jnp.int32, (tm, LANES), 0)
        dfa = jnp.where(row >= PAD_FRONT, dlogf * jax.nn.sigmoid(-z), 0.0)
        dp_o[:, cols.fa:cols.fa + LANES] = dfa.astype(BF16)
        gfb_o[...] += jnp.sum(dfa, axis=0, keepdims=True)

    rev = lambda r: nt - 1 - r
    w1024 = lambda off: pl.BlockSpec((tm, 1024), lambda r, o=off // 1024: (rev(r), o))
    w512 = lambda off: pl.BlockSpec((tm, 512), lambda r, o=off // 512: (rev(r), o))
    w128 = lambda off: pl.BlockSpec((tm, LANES), lambda r, o=off // LANES: (rev(r), o))
    vec = lambda w: pl.BlockSpec((1, w), lambda r: (0, 0))
    row = lambda w: pl.BlockSpec((tm, w), lambda r: (rev(r), 0))
    return pl.pallas_call(
        body, name=name, grid=(nt,),
        in_specs=[w1024(cols.qa), w1024(cols.ka), w512(cols.qb), w128(cols.kb), w128(cols.fa),
                  vec(LANES), vec(LANES), vec(512), vec(LANES), vec(LANES),
                  row(1024), row(1024), row(1024), row(512), row(LANES), row(LANES), row(LANES), row(d), row(d)],
        out_specs=[row(cols.np), vec(LANES), vec(LANES), vec(512), vec(LANES), vec(LANES)],
        out_shape=[jax.ShapeDtypeStruct((m, cols.np), BF16), jax.ShapeDtypeStruct((1, LANES), F32),
                   jax.ShapeDtypeStruct((1, LANES), F32), jax.ShapeDtypeStruct((1, 512), F32),
                   jax.ShapeDtypeStruct((1, LANES), F32), jax.ShapeDtypeStruct((1, LANES), F32)],
        scratch_shapes=[pltpu.VMEM((1, LANES), F32)],
        compiler_params=_params(1),
    )(proj, proj, proj, proj, proj, gqa, gka, gqb, gkb, fbias, dqf, dkf, dvf, dqb, dkb, dvb, dc, dga, dgb)


def dproj_bwd(dh, h, g, dproj, wp, name):
    m, d = h.shape
    npad = wp.shape[1]
    tm = _tile(m, PROJ_ROWS)
    tn = PROJ_COLS if npad % PROJ_COLS == 0 else npad
    nj = npad // tn

    def body(dh_ref, h_ref, g_ref, dp_ref, w_ref, dhin_ref, dgn_ref, acc_ref):
        j = pl.program_id(1)

        @pl.when((pl.program_id(0) == 0) & (j == 0))
        def _():
            dgn_ref[...] = jnp.zeros_like(dgn_ref)

        @pl.when(j == 0)
        def _():
            acc_ref[...] = jnp.zeros_like(acc_ref)

        acc_ref[...] += _dot_nt(dp_ref[...], w_ref[...])

        @pl.when(j == nj - 1)
        def _():
            dx, dgain = _rms_bwd(h_ref[...], g_ref[...], acc_ref[...])
            dgn_ref[...] += dgain
            dhin_ref[...] = dh_ref[...] + dx

    row = lambda w: pl.BlockSpec((tm, w), lambda r, j: (r, 0))
    return pl.pallas_call(
        body, name=name, grid=(m // tm, nj),
        in_specs=[row(d), row(d), pl.BlockSpec((1, d), lambda r, j: (0, 0)),
                  pl.BlockSpec((tm, tn), lambda r, j: (r, j)), pl.BlockSpec((d, tn), lambda r, j: (0, j))],
        out_specs=[row(d), pl.BlockSpec((1, d), lambda r, j: (0, 0))],
        out_shape=[jax.ShapeDtypeStruct((m, d), F32), jax.ShapeDtypeStruct((1, d), F32)],
        scratch_shapes=[pltpu.VMEM((tm, d), F32)],
        compiler_params=_params(2),
    )(dh, h, g, dproj, wp)


def _causal_t(t):
    return lax.broadcasted_iota(jnp.int32, (t, t), 0) <= lax.broadcasted_iota(jnp.int32, (t, t), 1)


HEADS_PER_STEP = 2


def fox_fwd(qf, kf, vt, name, rider=None):
    m = qf.shape[0]
    t = _row_tile(m)
    nq = m // t
    hp = HEADS_PER_STEP
    w = hp * LANES

    def body(q_ref, k_ref, vt_ref, o_ref, lse_ref, acc_ref, m_ref):
        qi = pl.program_id(1)
        acc_ref[...] = jnp.zeros_like(acc_ref)
        m_ref[...] = jnp.full_like(m_ref, NEG)

        def tile(ki, diagonal):
            off = pl.multiple_of(ki * t, t)
            for e in range(hp):
                sl = slice(e * LANES, (e + 1) * LANES)
                s = _dot_nt(k_ref[pl.ds(off, t), sl], q_ref[:, sl])
                if diagonal:
                    s = jnp.where(_causal_t(t), s, NEG)
                m_old = m_ref[e]
                m_new = jnp.maximum(m_old, jnp.max(s, axis=0, keepdims=True))
                p = jnp.exp(s - m_new).astype(BF16)
                acc_ref[e] = acc_ref[e] * jnp.exp(m_old - m_new) + _dot(vt_ref[sl, pl.ds(off, t)], p)
                m_ref[e] = m_new

        def step(ki, carry):
            tile(ki, False)
            return carry

        lax.fori_loop(0, qi, step, 0)
        tile(qi, True)
        row = lax.broadcasted_iota(jnp.int32, (LANES, t), 0)
        for e in range(hp):
            l = jnp.max(acc_ref[e, AUG:AUG + 8, :], axis=0, keepdims=True)
            o_ref[e * LANES:(e + 1) * LANES, :] = jnp.where(row < HEAD_DIM, acc_ref[e] * (1.0 / l), 0.0)
            lse_ref[e] = m_ref[e] + jnp.log(l)

    return rider_call(
        body, name, (8 // hp, nq),
        in_specs=[pl.BlockSpec((t, w), lambda hd, i: (i, hd)),
                  pl.BlockSpec((m, w), lambda hd, i: (0, hd)),
                  pl.BlockSpec((w, m), lambda hd, i: (hd, 0))],
        out_specs=[pl.BlockSpec((w, t), lambda hd, i: (hd, i)),
                   pl.BlockSpec((hp, 1, t), lambda hd, i: (hd, 0, i))],
        out_shape=[jax.ShapeDtypeStruct((8 * LANES, m), F32), jax.ShapeDtypeStruct((8, 1, m), F32)],
        scratch_shapes=[pltpu.VMEM((hp, LANES, t), F32), pltpu.VMEM((hp, 1, t), F32)],
        args=(qf, kf, vt), rider=rider)


def fox_bwd(qf, kf, vf, kt, dof, lse, delta, name, rider=None):
    m = qf.shape[0]
    t = _row_tile(m)
    nq = m // t
    hp = HEADS_PER_STEP
    w = hp * LANES

    def body(k_ref, v_ref, kt_ref, q_ref, do_ref, lse_ref, delta_ref, dk_ref, dv_ref, dq_ref, dka_ref, dva_ref):
        ki = pl.program_id(1)

        @pl.when(ki == 0)
        def _():
            dq_ref[...] = jnp.zeros_like(dq_ref)

        dka_ref[...] = jnp.zeros_like(dka_ref)
        dva_ref[...] = jnp.zeros_like(dva_ref)

        def tile(qi, diagonal):
            off = pl.multiple_of(qi * t, t)
            for e in range(hp):
                sl = slice(e * LANES, (e + 1) * LANES)
                q = q_ref[pl.ds(off, t), sl]
                do = do_ref[pl.ds(off, t), sl]
                s = _dot_nt(k_ref[:, sl], q)
                if diagonal:
                    s = jnp.where(_causal_t(t), s, NEG)
                p = jnp.exp(s - lse_ref[e, :, pl.ds(off, t)])
                ds = (p * (_dot_nt(v_ref[:, sl], do) - delta_ref[e, :, pl.ds(off, t)])).astype(BF16)
                dva_ref[:, sl] += _dot(p.astype(BF16), do)
                dka_ref[:, sl] += _dot(ds, q)
                dq_ref[sl, pl.ds(off, t)] += _dot(kt_ref[sl, :], ds)

        def step(qi, carry):
            tile(qi, False)
            return carry

        tile(ki, True)
        lax.fori_loop(ki + 1, nq, step, 0)
        dk_ref[...] = dka_ref[...]
        dv_ref[...] = dva_ref[...]

    tile_spec = pl.BlockSpec((t, w), lambda hd, i: (i, hd))
    full = pl.BlockSpec((m, w), lambda hd, i: (0, hd))
    stat = pl.BlockSpec((hp, 1, m), lambda hd, i: (hd, 0, 0))
    return rider_call(
        body, name, (8 // hp, nq),
        in_specs=[tile_spec, tile_spec, pl.BlockSpec((w, t), lambda hd, i: (hd, i)), full, full, stat, stat],
        out_specs=[tile_spec, tile_spec, pl.BlockSpec((w, m), lambda hd, i: (hd, 0))],
        out_shape=[jax.ShapeDtypeStruct((m, 8 * LANES), F32), jax.ShapeDtypeStruct((m, 8 * LANES), F32),
                   jax.ShapeDtypeStruct((8 * LANES, m), F32)],
        scratch_shapes=[pltpu.VMEM((t, w), F32), pltpu.VMEM((t, w), F32)],
        args=(kf, vf, kt, qf, dof, lse, delta), rider=rider)


def _bucket_ids():
    def bucket(dist):
        n = np.maximum(dist, 0)
        max_exact = N_BUCKETS // 2
        nf = np.maximum(n, 1).astype(np.float32)
        large = max_exact + (np.log(nf / max_exact) / math.log(MAX_DISTANCE / max_exact)
                             * (N_BUCKETS - max_exact)).astype(np.int32)
        return np.where(n < max_exact, n, np.minimum(large, N_BUCKETS - 1))

    tl = np.arange(LANES)[:, None]
    sl = np.arange(LANES)[None, :]
    prev = bucket(LANES + tl - sl)
    cur = bucket(tl - sl)
    meta = np.full((LANES, LANES), N_BUCKETS - 1)
    return np.concatenate([prev, cur, meta], axis=1).astype(np.int32)


def bias_build(table, name):
    ids = jnp.asarray(_bucket_ids())

    def body(t_ref, id_ref, o_ref):
        idv = id_ref[...]
        for h in range(8):
            acc = jnp.zeros((LANES, 3 * LANES), F32)
            for b in range(N_BUCKETS):
                acc = jnp.where(idv == b, t_ref[b, h], acc)
            o_ref[h] = acc

    return pl.pallas_call(
        body, name=name,
        in_specs=[pl.BlockSpec(memory_space=pltpu.SMEM), pl.BlockSpec(memory_space=pltpu.VMEM)],
        out_specs=pl.BlockSpec(memory_space=pltpu.VMEM),
        out_shape=jax.ShapeDtypeStruct((8, LANES, 3 * LANES), F32),
    )(table, ids)


def bias_reduce(dbias, name):
    ids = jnp.asarray(_bucket_ids())

    def body(d_ref, id_ref, o_ref):
        idv = id_ref[...]
        rr = lax.broadcasted_iota(jnp.int32, (N_BUCKETS, LANES), 0)
        cc = lax.broadcasted_iota(jnp.int32, (N_BUCKETS, LANES), 1)
        acc = jnp.zeros((N_BUCKETS, LANES), F32)
        for h in range(8):
            dv = d_ref[h]
            for b in range(N_BUCKETS):
                val = jnp.sum(jnp.where(idv == b, dv, 0.0), keepdims=True)
                acc = jnp.where((rr == b) & (cc == h), val, acc)
        o_ref[...] = acc

    return pl.pallas_call(
        body, name=name,
        in_specs=[pl.BlockSpec(memory_space=pltpu.VMEM), pl.BlockSpec(memory_space=pltpu.VMEM)],
        out_specs=pl.BlockSpec(memory_space=pltpu.VMEM),
        out_shape=jax.ShapeDtypeStruct((N_BUCKETS, LANES), F32),
    )(dbias, ids)


def _swa_valid(n):
    shape = (LANES, 3 * LANES)
    tl = lax.broadcasted_iota(jnp.int32, shape, 0)
    col = lax.broadcasted_iota(jnp.int32, shape, 1)
    sl = col & (LANES - 1)
    nv = jnp.full(shape, n, jnp.int32)
    is_meta = sl >= PAD_FRONT
    prev = (col < LANES) & (sl > tl) & (nv >= 1) & ((nv >= 2) | is_meta)
    cur = (col >= LANES) & (col < 2 * LANES) & (sl <= tl) & ((nv >= 1) | is_meta)
    meta = (col >= 2 * LANES) & is_meta & ((nv >= 2) | ((nv == 1) & (sl <= tl)))
    return prev | cur | meta


def _swa_keys(ref, n):
    off_prev = pl.multiple_of(jnp.maximum(n - 1, 0) * LANES, LANES)
    off_cur = pl.multiple_of(n * LANES, LANES)
    return jnp.concatenate([ref[pl.ds(off_prev, LANES), :], ref[pl.ds(off_cur, LANES), :], ref[0:LANES, :]], axis=0)


def swa_fwd(q, k, v, bias, sinks, name):
    m = q.shape[0]

    def body(q_ref, k_ref, v_ref, bias_ref, sink_ref, o_ref, lse_ref):
        n = pl.program_id(0)
        lane1 = lax.broadcasted_iota(jnp.int32, (1, LANES), 1)
        lane_t = lax.broadcasted_iota(jnp.int32, (LANES, LANES), 1)
        in_head = [lane1 < HEAD_DIM, lane1 >= HEAD_DIM]
        kall = _swa_keys(k_ref, n)
        vall = _swa_keys(v_ref, n)
        vs = [jnp.where(in_head[g], vall, jnp.zeros_like(vall)) for g in (0, 1)]
        valid = _swa_valid(n)
        lse = jnp.zeros((LANES, LANES), F32)
        for b in range(4):
            qb = q_ref[:, b * LANES:(b + 1) * LANES]
            ob = jnp.zeros((LANES, LANES), F32)
            for g in (0, 1):
                h = 4 * g + b
                qe = jnp.where(in_head[g], qb, jnp.zeros_like(qb))
                s = jnp.where(valid, _dot_nt(qe, kall) + bias_ref[h], NEG)
                sink = sink_ref[h]
                mx = jnp.maximum(jnp.max(s, axis=1, keepdims=True), sink)
                p = jnp.exp(s - mx)
                den = jnp.sum(p, axis=1, keepdims=True) + jnp.exp(sink - mx)
                ob = ob + _dot((p / den).astype(BF16), vs[g])
                lse = jnp.where(lane_t == h, mx + jnp.log(den), lse)
            o_ref[:, b * LANES:(b + 1) * LANES] = ob
        lse_ref[...] = lse

    return pl.pallas_call(
        body, name=name, grid=(m // LANES,),
        in_specs=[pl.BlockSpec((LANES, 512), lambda n: (n, 0)),
                  pl.BlockSpec((m, LANES), lambda n: (0, 0)), pl.BlockSpec((m, LANES), lambda n: (0, 0)),
                  pl.BlockSpec((8, LANES, 3 * LANES), lambda n: (0, 0, 0)),
                  pl.BlockSpec(memory_space=pltpu.SMEM)],
        out_specs=[pl.BlockSpec((LANES, 512), lambda n: (n, 0)), pl.BlockSpec((LANES, LANES), lambda n: (n, 0))],
        out_shape=[jax.ShapeDtypeStruct((m, 512), F32), jax.ShapeDtypeStruct((m, LANES), F32)],
        compiler_params=_params(1),
    )(q, k, v, bias, sinks)


def swa_bwd(q, k, v, bias, sinks, o, lse, do, name):
    m = q.shape[0]

    def body(q_ref, do_ref, o_ref, lse_ref, k_ref, v_ref, bias_ref, sink_ref,
             dq_ref, dk_ref, dv_ref, dbias_ref, dsink_ref):
        n = pl.program_id(0)

        @pl.when(n == 0)
        def _():
            for r in (dk_ref, dv_ref, dbias_ref, dsink_ref):
                r[...] = jnp.zeros_like(r)

        lane1 = lax.broadcasted_iota(jnp.int32, (1, LANES), 1)
        lane_t = lax.broadcasted_iota(jnp.int32, (LANES, LANES), 1)
        in_head = [lane1 < HEAD_DIM, lane1 >= HEAD_DIM]
        off_prev = pl.multiple_of(jnp.maximum(n - 1, 0) * LANES, LANES)
        off_cur = pl.multiple_of(n * LANES, LANES)
        kall = _swa_keys(k_ref, n)
        vall = _swa_keys(v_ref, n)
        ks = [jnp.where(in_head[g], kall, jnp.zeros_like(kall)) for g in (0, 1)]
        valid = _swa_valid(n)
        lsev = lse_ref[...]
        dsink = dsink_ref[...]
        dkall = jnp.zeros((3 * LANES, LANES), F32)
        dvall = jnp.zeros((3 * LANES, LANES), F32)
        for b in range(4):
            sl = slice(b * LANES, (b + 1) * LANES)
            qb = q_ref[:, sl]
            dob = do_ref[:, sl]
            prod = dob * o_ref[:, sl]
            dqb = jnp.zeros((LANES, LANES), F32)
            for g in (0, 1):
                h = 4 * g + b
                qe = jnp.where(in_head[g], qb, jnp.zeros_like(qb))
                doe = jnp.where(in_head[g], dob, 0.0).astype(BF16)
                delta = jnp.sum(jnp.where(in_head[g], prod, 0.0), axis=1, keepdims=True)
                lse_h = _lane_col(lsev, lane_t, h)
                s = jnp.where(valid, _dot_nt(qe, kall) + bias_ref[h], NEG)
                p = jnp.exp(s - lse_h)
                ds = p * (_dot_nt(doe, vall) - delta)
                dbias_ref[h] += ds
                sink_part = jnp.sum(-jnp.exp(sink_ref[h] - lse_h) * delta, keepdims=True)
                dsink = jnp.where(lane1 == h, dsink + sink_part, dsink)
                dsb = ds.astype(BF16)
                dqb = dqb + _dot(dsb, ks[g])
                dkall = dkall + _dot_tn(dsb, qe)
                dvall = dvall + _dot_tn(p.astype(BF16), doe)
            dq_ref[:, sl] = dqb
        dsink_ref[...] = dsink
        for ref, val in ((dk_ref, dkall), (dv_ref, dvall)):
            ref[pl.ds(off_prev, LANES), :] += val[0:LANES]
            ref[pl.ds(off_cur, LANES), :] += val[LANES:2 * LANES]
            ref[0:LANES, :] += val[2 * LANES:3 * LANES]

    blk = pl.BlockSpec((LANES, 512), lambda n: (n, 0))
    full = pl.BlockSpec((m, LANES), lambda n: (0, 0))
    return pl.pallas_call(
        body, name=name, grid=(m // LANES,),
        in_specs=[blk, blk, blk, pl.BlockSpec((LANES, LANES), lambda n: (n, 0)), full, full,
                  pl.BlockSpec((8, LANES, 3 * LANES), lambda n: (0, 0, 0)),
                  pl.BlockSpec(memory_space=pltpu.SMEM)],
        out_specs=[blk, full, full, pl.BlockSpec((8, LANES, 3 * LANES), lambda n: (0, 0, 0)),
                   pl.BlockSpec((1, LANES), lambda n: (0, 0))],
        out_shape=[jax.ShapeDtypeStruct((m, 512), F32), jax.ShapeDtypeStruct((m, LANES), F32),
                   jax.ShapeDtypeStruct((m, LANES), F32), jax.ShapeDtypeStruct((8, LANES, 3 * LANES), F32),
                   jax.ShapeDtypeStruct((1, LANES), F32)],
        compiler_params=_params(1),
    )(q, do, o, lse, k, v, bias, sinks)


def branch_out(h, o_fox, o_swa, proj, wbf, wbs, wo, cols, name):
    m, d = h.shape
    tm = _row_tile(m)

    def body(h_ref, of_ref, os_ref, ga_ref, gb_ref, wbf_ref, wbs_ref, wo_ref, hn_ref):
        tf = _dot(of_ref[...].astype(BF16), wbf_ref[...])
        ts = _dot(os_ref[...].astype(BF16), wbs_ref[...])
        y = jax.nn.sigmoid(ga_ref[...]) * tf + jax.nn.sigmoid(gb_ref[...]) * ts
        hn_ref[...] = h_ref[...] + _dot(y.astype(BF16), wo_ref[...])

    row = lambda w, o=0: pl.BlockSpec((tm, w), lambda r, o=o: (r, o))
    res = lambda a: pl.BlockSpec(a.shape, lambda r: (0, 0))
    return pl.pallas_call(
        body, name=name, grid=(m // tm,),
        in_specs=[row(d), row(1024), row(512), row(d, cols.ga // d), row(d, cols.gb // d), res(wbf), res(wbs), res(wo)],
        out_specs=row(d),
        out_shape=jax.ShapeDtypeStruct((m, d), F32),
        compiler_params=_params(1),
    )(h, o_fox, o_swa, proj, proj, wbf, wbs, wo)


def branch_out_bwd(dh, o_fox, o_swa, proj, wbf, wbs, wo, cols, name):
    m, d = dh.shape
    tm = _row_tile(m)

    def body(dh_ref, of_ref, os_ref, ga_ref, gb_ref, wbf_ref, wbs_ref, wo_ref,
             y_ref, dtf_ref, dts_ref, dga_ref, dgb_ref, dof_ref, dos_ref, delta_ref):
        dy = _dot_nt(dh_ref[...].astype(BF16), wo_ref[...])
        tf = _dot(of_ref[...].astype(BF16), wbf_ref[...])
        ts = _dot(os_ref[...].astype(BF16), wbs_ref[...])
        sa = jax.nn.sigmoid(ga_ref[...])
        sb = jax.nn.sigmoid(gb_ref[...])
        y_ref[...] = (sa * tf + sb * ts).astype(BF16)
        dtf = (dy * sa).astype(BF16)
        dts = (dy * sb).astype(BF16)
        dtf_ref[...] = dtf
        dts_ref[...] = dts
        dga_ref[...] = (dy * tf * sa * (1.0 - sa)).astype(BF16)
        dgb_ref[...] = (dy * ts * sb * (1.0 - sb)).astype(BF16)
        dof = _dot_nt(dtf, wbf_ref[...])
        dof_ref[...] = dof.astype(BF16)
        dos_ref[...] = _dot_nt(dts, wbs_ref[...])
        lane = lax.broadcasted_iota(jnp.int32, (tm, LANES), 1)
        delta = jnp.zeros((tm, LANES), F32)
        for hd in range(8):
            sl = slice(hd * LANES, (hd + 1) * LANES)
            delta = jnp.where(lane == hd, jnp.sum(dof[:, sl] * of_ref[:, sl], axis=1, keepdims=True), delta)
        delta_ref[...] = delta

    row = lambda w, o=0: pl.BlockSpec((tm, w), lambda r, o=o: (r, o))
    res = lambda a: pl.BlockSpec(a.shape, lambda r: (0, 0))
    return pl.pallas_call(
        body, name=name, grid=(m // tm,),
        in_specs=[row(d), row(1024), row(512), row(d, cols.ga // d), row(d, cols.gb // d), res(wbf), res(wbs), res(wo)],
        out_specs=[row(d)] * 5 + [row(1024), row(512), row(LANES)],
        out_shape=[jax.ShapeDtypeStruct((m, d), BF16)] * 5 + [jax.ShapeDtypeStruct((m, 1024), BF16),
                   jax.ShapeDtypeStruct((m, 512), F32), jax.ShapeDtypeStruct((m, LANES), F32)],
        compiler_params=_params(1),
    )(dh, o_fox, o_swa, proj, proj, wbf, wbs, wo)


def loss_head(h, target, name):
    m, d = h.shape

    def body(h_ref, t_ref, dh_ref, loss_ref):
        n = pl.program_id(0)

        @pl.when(n == 0)
        def _():
            loss_ref[...] = jnp.zeros_like(loss_ref)
            dh_ref[...] = jnp.zeros_like(dh_ref)

        @pl.when(n > 0)
        def _():
            err = h_ref[...] - t_ref[...]
            dh_ref[...] = err * (1.0 / d)
            loss_ref[...] += jnp.sum(err * err, keepdims=True) * (0.5 / d)

    return pl.pallas_call(
        body, name=name, grid=(m // LANES,),
        in_specs=[pl.BlockSpec((LANES, d), lambda n: (n, 0)),
                  pl.BlockSpec((LANES, d), lambda n: (jnp.maximum(n - 1, 0), 0))],
        out_specs=[pl.BlockSpec((LANES, d), lambda n: (n, 0)), pl.BlockSpec((8, LANES), lambda n: (0, 0))],
        out_shape=[jax.ShapeDtypeStruct((m, d), F32), jax.ShapeDtypeStruct((8, LANES), F32)],
        compiler_params=_params(1),
    )(h, target)


def _adamw_math(w, g, m, v):
    m = ADAM_B1 * m + (1.0 - ADAM_B1) * g
    v = ADAM_B2 * v + (1.0 - ADAM_B2) * (g * g)
    m_hat = m / (1.0 - ADAM_B1 ** ADAM_STEP)
    v_hat = v / (1.0 - ADAM_B2 ** ADAM_STEP)
    delta = -ADAM_LR * (m_hat / (jnp.sqrt(v_hat) + ADAM_EPS) + ADAM_WD * w)
    return delta, m, v


def adamw_sum(parts, w, m, v, name):
    n_layers, a, b = w.shape
    ta = next(t for t in (256, 176, 128, a) if a % t == 0)
    nr = a // ta

    def body(*refs):
        p_refs = refs[:n_layers]
        w_ref, m_ref, v_ref, g_o, d_o, m_o, v_o = refs[n_layers:]
        for l in range(n_layers):
            @pl.when(pl.program_id(0) == l)
            def _(l=l):
                g = p_refs[l][0].astype(F32)
                for j in range(1, N_DEV):
                    g = g + p_refs[l][j].astype(F32)
                g_o[0] = g
                d_o[0], m_o[0], v_o[0] = _adamw_math(w_ref[0], g, m_ref[0], v_ref[0])

    def part_spec(l):
        return pl.BlockSpec((N_DEV, ta, b), lambda i, r, l=l: (0, jnp.where(i == l, r, jnp.where(i < l, 0, nr - 1)), 0))

    row = pl.BlockSpec((1, ta, b), lambda i, r: (i, r, 0))
    return pl.pallas_call(
        body, name=name, grid=(n_layers, nr),
        in_specs=[part_spec(l) for l in range(n_layers)] + [row, row, row],
        out_specs=[row] * 4,
        out_shape=[jax.ShapeDtypeStruct(w.shape, F32)] * 4,
        compiler_params=_params(2),
    )(*parts, w, m, v)


def adamw_small(g, w, m, v, name):
    def body(g_ref, w_ref, m_ref, v_ref, d_o, m_o, v_o):
        d_o[...], m_o[...], v_o[...] = _adamw_math(w_ref[...], g_ref[...], m_ref[...], v_ref[...])

    spec = pl.BlockSpec(memory_space=pltpu.VMEM)
    return pl.pallas_call(
        body, name=name, in_specs=[spec] * 4, out_specs=[spec] * 3,
        out_shape=[jax.ShapeDtypeStruct(w.shape, F32)] * 3,
    )(g, w, m, v)


BIG = ("ffn1_w_in", "ffn1_w_out", "w_in", "w_branch_fox", "w_branch_swa", "w_out", "ffn2_w_in", "ffn2_w_out")
SMALL = ("rel_bias_table", "ffn1_norm", "mix_norm", "forget_bias", "fox_q_norm", "fox_k_norm",
         "swa_q_norm", "swa_k_norm", "swa_sinks", "ffn2_norm")
WEIGHTS = ("meta_tokens", "rel_bias_table", "ffn1_norm", "ffn1_w_in", "ffn1_w_out", "mix_norm", "w_in",
           "forget_bias", "fox_q_norm", "fox_k_norm", "swa_q_norm", "swa_k_norm", "swa_sinks", "w_branch_fox",
           "w_branch_swa", "w_out", "ffn2_norm", "ffn2_w_in", "ffn2_w_out")


def _pack(arrs, width, row_multiple, dtype):
    lead = arrs[0].shape[:-1]
    flat = jnp.concatenate([a.astype(dtype) for a in arrs], axis=-1)
    n = flat.shape[-1]
    rows = -(-n // width)
    rows = -(-rows // row_multiple) * row_multiple
    flat = jnp.pad(flat, [(0, 0)] * len(lead) + [(0, rows * width - n)])
    return flat.reshape(lead + (rows, width))


def _unpack(flat, shapes):
    flat = flat.reshape(-1)
    out, off = [], 0
    for s in shapes:
        n = int(np.prod(s))
        out.append(flat[off:off + n].reshape(s))
        off += n
    return out


def _swa_head_order():
    return [4 * (j % 2) + j // 2 for j in range(8)]


def _permute_heads(a, axis, inverse=False):
    order = _swa_head_order()
    if inverse:
        order = [order.index(hd) for hd in range(8)]
    parts = [lax.slice_in_dim(a, hd * HEAD_DIM, (hd + 1) * HEAD_DIM, axis=axis) for hd in order]
    return jnp.concatenate(parts, axis=axis)


def _w_in_segments(cols):
    d = cols.d
    segs = [(512 * i + HEAD_DIM * hd, HEAD_DIM, new + LANES * hd)
            for i, new in enumerate((cols.qa, cols.ka, cols.va)) for hd in range(8)]
    segs.append((1536, 8, cols.fa))
    order = _swa_head_order()
    segs += [(1544 + HEAD_DIM * hd, HEAD_DIM, cols.qb + HEAD_DIM * order.index(hd)) for hd in range(8)]
    segs += [(2056, 128, cols.kb), (2184, 128, cols.vb), (2312, d, cols.ga), (2312 + d, d, cols.gb)]
    return segs


def _reorder_w_in(blocks, cols):
    width = blocks[0].shape[1]
    zeros = lambda n: jnp.zeros((blocks[0].shape[0], n), blocks[0].dtype)
    parts, at = [], 0
    for old, length, new in sorted(_w_in_segments(cols), key=lambda s: s[2]):
        if new > at:
            parts.append(zeros(new - at))
        at = new + length
        while length:
            j, off = divmod(old, width)
            take = min(length, width - off)
            parts.append(blocks[j][:, off:off + take])
            old, length = old + take, length - take
    parts.append(zeros(cols.np - at))
    return jnp.concatenate(parts, axis=1)


def _restore_w_in(wp, cols, width):
    segs = sorted(_w_in_segments(cols))
    blocks = []
    for j in range(N_DEV):
        lo, hi = j * width, (j + 1) * width
        parts = []
        for old, length, new in segs:
            a, b = max(old, lo), min(old + length, hi)
            if a < b:
                parts.append(wp[:, new + a - old:new + b - old])
        blocks.append(jnp.concatenate(parts, axis=1))
    return jnp.stack(blocks)


def _lane_pad(v):
    return jnp.pad(v, ((0, 0), (0, LANES - v.shape[1])))


def kernel(x, meta_tokens, rel_bias_table, ffn1_norm, ffn1_w_in, ffn1_w_out, mix_norm, w_in, forget_bias, fox_q_norm, fox_k_norm, swa_q_norm, swa_k_norm, swa_sinks, w_branch_fox, w_branch_swa, w_out, ffn2_norm, ffn2_w_in, ffn2_w_out, loss_target, m_meta_tokens, m_rel_bias_table, m_ffn1_norm, m_ffn1_w_in, m_ffn1_w_out, m_mix_norm, m_w_in, m_forget_bias, m_fox_q_norm, m_fox_k_norm, m_swa_q_norm, m_swa_k_norm, m_swa_sinks, m_w_branch_fox, m_w_branch_swa, m_w_out, m_ffn2_norm, m_ffn2_w_in, m_ffn2_w_out, v_meta_tokens, v_rel_bias_table, v_ffn1_norm, v_ffn1_w_in, v_ffn1_w_out, v_mix_norm, v_w_in, v_forget_bias, v_fox_q_norm, v_fox_k_norm, v_swa_q_norm, v_swa_k_norm, v_swa_sinks, v_w_branch_fox, v_w_branch_swa, v_w_out, v_ffn2_norm, v_ffn2_w_in, v_ffn2_w_out):
    args = dict(locals())
    wts = {n: args[n] for n in WEIGHTS}
    mom1 = {n: args["m_" + n] for n in WEIGHTS}
    mom2 = {n: args["v_" + n] for n in WEIGHTS}

    seq, d = x.shape[1], x.shape[2]
    m_rows = seq + LANES
    depth = ffn1_norm.shape[0]
    fb = ffn1_w_in.shape[2]
    fo = ffn1_w_out.shape[1]
    din_shard = w_in.shape[2]
    cols = _Cols(d)
    scale = HEAD_DIM ** -0.5
    dev = 4 * lax.axis_index("x") + 2 * lax.axis_index("y") + lax.axis_index("c")

    groups = {"ffn1": ("ffn1_w_in", "ffn1_w_out"), "mix": ("w_in", "w_branch_fox", "w_branch_swa", "w_out"),
              "ffn2": ("ffn2_w_in", "ffn2_w_out"), "ffn1_in": ("ffn1_w_in",), "ffn1_out": ("ffn1_w_out",),
              "ffn2_in": ("ffn2_w_in",), "ffn2_out": ("ffn2_w_out",)}
    shard = {n: wts[n].astype(BF16) for n in BIG}
    full, parts, gw = {}, {}, {}

    def keys_of(stages):
        return [(n, l) for g, l in stages if l < depth for n in groups[g]]

    def gather_rider(stages):
        return Rider([shard[n][l] for n, l in keys_of(stages)], True)

    def scatter_rider(stages):
        return Rider([gw[k] for k in keys_of(stages)], False)

    def ffn_weights(tag, l):
        return full[tag + "_w_in", l], full[tag + "_w_out", l].reshape(4, fb, d)

    def mixer_weights(l):
        wp = _reorder_w_in([full["w_in", l][j] for j in range(N_DEV)], cols)
        wbf = jnp.concatenate([full["w_branch_fox", l][j] for j in range(N_DEV)], axis=1)
        wbf = jnp.pad(wbf.reshape(8, HEAD_DIM, d), ((0, 0), (0, LANES - HEAD_DIM), (0, 0))).reshape(8 * LANES, d)
        wbs = _permute_heads(jnp.concatenate([full["w_branch_swa", l][j] for j in range(N_DEV)], axis=1), 0)
        return wp, wbf, wbs, full["w_out", l].reshape(d, d)

    full.update(zip(keys_of([("ffn1", 0)]), exchange_hbm(gather_rider([("ffn1", 0)]).srcs, True, "gather_first")))
    meta_all = gather_small(meta_tokens.reshape(1, N_META, -1), "gather_meta")
    meta_full = meta_all.transpose(1, 0, 2).reshape(N_META, d)
    tile8 = lambda g, s=1.0: jnp.tile(g.reshape(1, HEAD_DIM) * s, (1, 8))
    tile2 = lambda g: jnp.tile(g.reshape(1, HEAD_DIM), (1, 2))
    data_lanes = lambda g, s=1.0: _lane_pad(g.reshape(1, HEAD_DIM) * s)
    bias = bias_build(rel_bias_table, "swa_bias")

    first = jnp.concatenate([jnp.zeros((PAD_FRONT, d), F32), meta_full], axis=0)
    h = jnp.concatenate([first, x[0]], axis=0)
    saved, lw = [], []
    for l in range(depth):
        s, w = {"h0": h}, {}
        w["ffn1_in"], w["ffn1_out"] = ffn_weights("ffn1", l)
        stages = [("mix", 0)] if l == 0 else []
        (h, s["n1"], s["gate1"], s["up1"]), got = ffn_fwd(h, ffn1_norm[l:l + 1], w["ffn1_in"], w["ffn1_out"],
                                                          f"ffn1_fwd_{l}", gather_rider(stages))
        full.update(zip(keys_of(stages), got))
        s["h1"] = h
        w["wp"], w["wbf"], w["wbs"], w["wo"] = mixer_weights(l)
        s["nm"], s["proj"] = mixer_proj(h, mix_norm[l:l + 1], w["wp"], f"mixer_proj_{l}")
        s["gains"] = (data_lanes(fox_q_norm[l], scale), data_lanes(fox_k_norm[l]), tile8(swa_q_norm[l], scale),
                      tile2(swa_k_norm[l]))
        s["fbias"] = _lane_pad(forget_bias[l:l + 1])
        qf, kf, vf, qb, kb, vb = qk_post(s["proj"], s["gains"], s["fbias"], cols, f"qk_post_{l}")
        s.update(qf=qf, kf=kf, vf=vf, qb=qb, kb=kb, vb=vb)
        stages = [("ffn2", l), ("ffn1", l + 1)]
        (o_t, s["lse_fox"]), got = fox_fwd(qf, kf, vf.T, f"fox_fwd_{l}", gather_rider(stages))
        s["o_fox"] = o_t.T
        full.update(zip(keys_of(stages), got))
        s["o_swa"], s["lse_swa"] = swa_fwd(qb, kb, vb, bias, swa_sinks[l], f"swa_fwd_{l}")
        h = branch_out(h, s["o_fox"], s["o_swa"], s["proj"], w["wbf"], w["wbs"], w["wo"], cols, f"branch_out_{l}")
        s["h2"] = h
        w["ffn2_in"], w["ffn2_out"] = ffn_weights("ffn2", l)
        stages = [("mix", l + 1)]
        (h, s["n2"], s["gate2"], s["up2"]), got = ffn_fwd(h, ffn2_norm[l:l + 1], w["ffn2_in"], w["ffn2_out"],
                                                          f"ffn2_fwd_{l}", gather_rider(stages))
        full.update(zip(keys_of(stages), got))
        saved.append(s)
        lw.append(w)

    dh, loss_part = loss_head(h, loss_target[0], "loss_head")

    gs = {n: [None] * depth for n in SMALL}
    dbias_total = None
    for l in reversed(range(depth)):
        w, s = lw[l], saved[l]

        def ffn_back(dh, tag, hin, norm, n_in, gate, up, stages):
            (dh_in, a, dg, du, dgn, dhs), got = ffn_bwd(dh, hin, norm, gate, up, w[tag + "_in"], w[tag + "_out"],
                                                        f"{tag}_bwd_{l}", scatter_rider(stages))
            parts.update(zip(keys_of(stages), got))
            gw[tag + "_w_out", l] = matmul_tn(a, dhs[None], f"{tag}_dwo_{l}").reshape(N_DEV, fo, d)
            stages = [(tag + "_out", l)]
            gw[tag + "_w_in", l], got = matmul_tn(n_in[None], dg, f"{tag}_dwi_{l}", y2=du,
                                                  rider=scatter_rider(stages))
            parts.update(zip(keys_of(stages), got))
            return dh_in, dgn

        dh, gs["ffn2_norm"][l] = ffn_back(dh, "ffn2", s["h2"], ffn2_norm[l:l + 1], s["n2"], s["gate2"], s["up2"],
                                          [("ffn1_in", l + 1)])

        y, dtf, dts, dga, dgb, dof, dos, delta = branch_out_bwd(dh, s["o_fox"], s["o_swa"], s["proj"], w["wbf"],
                                                                w["wbs"], w["wo"], cols, f"branch_out_bwd_{l}")
        gw["w_out", l] = matmul_tn(y[None], dh[None], f"dw_out_{l}").reshape(N_DEV, d // N_DEV, d)
        to_shards = lambda a: a.reshape(512, N_DEV, d // N_DEV).transpose(1, 0, 2)
        gw["w_branch_fox", l] = to_shards(matmul_tn(s["o_fox"][None], dtf[None], f"dw_branch_fox_{l}")[0]
                                          .reshape(8, LANES, d)[:, :HEAD_DIM].reshape(512, d))
        gw["w_branch_swa", l] = to_shards(_permute_heads(
            matmul_tn(s["o_swa"][None], dts[None], f"dw_branch_swa_{l}")[0], 0, inverse=True))

        stages = [("ffn2_in", l)]
        (dkf, dvf, dqf_t), got = fox_bwd(s["qf"], s["kf"], s["vf"], s["kf"].T, dof, s["lse_fox"],
                                         delta[:, :8].T.reshape(8, 1, m_rows), f"fox_bwd_{l}", scatter_rider(stages))
        parts.update(zip(keys_of(stages), got))
        dc = _lane_pad(dqf_t.reshape(8, LANES, m_rows)[:, AUG].T - dkf.reshape(m_rows, 8, LANES)[:, :, AUG + 3])
        dqb, dkb, dvb, dbias, dsink = swa_bwd(s["qb"], s["kb"], s["vb"], bias, swa_sinks[l], s["o_swa"], s["lse_swa"],
                                              dos, f"swa_bwd_{l}")
        dbias_total = dbias if dbias_total is None else dbias_total + dbias
        gs["swa_sinks"][l] = dsink[0, :8]
        dproj, ggqa, ggka, ggqb, ggkb, gfb = qk_post_bwd(s["proj"], s["gains"], s["fbias"], dqf_t.T, dkf, dvf, dqb, dkb,
                                                         dvb, dc, dga, dgb, cols, f"qk_post_bwd_{l}")
        gs["fox_q_norm"][l] = ggqa[0, :HEAD_DIM] * scale
        gs["fox_k_norm"][l] = ggka[0, :HEAD_DIM]
        gs["swa_q_norm"][l] = ggqb.reshape(8, HEAD_DIM).sum(0) * scale
        gs["swa_k_norm"][l] = ggkb.reshape(2, HEAD_DIM).sum(0)
        gs["forget_bias"][l] = gfb[0, :8]
        dwp = matmul_tn(s["nm"][None], dproj[None], f"dw_in_{l}", tn=1024 if cols.np % 1024 == 0 else None)[0]
        gw["w_in", l] = _restore_w_in(dwp, cols, din_shard)
        dh, gs["mix_norm"][l] = dproj_bwd(dh, s["h1"], mix_norm[l:l + 1], dproj, w["wp"], f"dproj_bwd_{l}")

        dh, gs["ffn1_norm"][l] = ffn_back(dh, "ffn1", s["h0"], ffn1_norm[l:l + 1], s["n1"], s["gate1"], s["up1"],
                                          [("mix", l)])

    grad_x = dh[LANES:][None]
    dmeta = dh[PAD_FRONT:LANES]
    dtable = bias_reduce(dbias_total, "swa_dbias")[:, :8]

    parts.update(zip(keys_of([("ffn1_in", 0)]),
                     exchange_hbm(scatter_rider([("ffn1_in", 0)]).srcs, False, "scatter_last")))
    big_out = [{}, {}, {}, {}]
    for n in BIG:
        outs = adamw_sum([parts[n, l] for l in range(depth)], wts[n], mom1[n], mom2[n], f"adamw_{n}")
        for k in range(4):
            big_out[k][n] = outs[k]

    small_g = {n: (jnp.stack(gs[n]) if n != "rel_bias_table" else None) for n in SMALL}
    small_g["rel_bias_table"] = dtable
    pieces = [loss_part[0:1, 0:1].reshape(1, 1)] + [small_g[n].reshape(1, -1) for n in SMALL] + [dmeta.reshape(1, -1)]
    small_shapes = [(1,)] + [wts[n].shape for n in SMALL] + [(N_META, d)]
    total = allsum_small(_pack(pieces, LANES, 8, F32), "allsum_small")
    summed = _unpack(total, small_shapes)
    loss = summed[0][0]
    g_small = dict(zip(SMALL, summed[1:1 + len(SMALL)]))
    g_meta = lax.dynamic_slice_in_dim(summed[-1], dev * (d // N_DEV), d // N_DEV, axis=1)
    names = SMALL + ("meta_tokens",)
    g_small["meta_tokens"] = g_meta
    pk = lambda src: _pack([src[n].reshape(1, -1) for n in names], LANES, 8, F32)[0]
    small_out = [dict(zip(names, _unpack(o, [wts[n].shape for n in names])))
                 for o in adamw_small(pk(g_small), pk(wts), pk(mom1), pk(mom2), "adamw_small")]

    grads = {**big_out[0], **g_small}
    delta = {**big_out[1], **small_out[0]}
    new_m = {**big_out[2], **small_out[1]}
    new_v = {**big_out[3], **small_out[2]}
    return (loss, grad_x, *[grads[n] for n in WEIGHTS], *[delta[n] for n in WEIGHTS],
            *[new_m[n] for n in WEIGHTS], *[new_v[n] for n in WEIGHTS])
```

```python
import math

import numpy as np
import jax
import jax.numpy as jnp
from jax import lax
from jax.experimental import pallas as pl
from jax.experimental.pallas import tpu as pltpu

F32 = jnp.float32
BF16 = jnp.bfloat16
EPS = 1e-6
NEG = -1e30
HEAD_DIM = 64
LANES = 128
N_META = 16
PAD_FRONT = LANES - N_META
N_BUCKETS = 32
MAX_DISTANCE = 128
N_DEV = 8
ADAM_LR, ADAM_B1, ADAM_B2, ADAM_EPS, ADAM_WD, ADAM_STEP = 0.001, 0.9, 0.999, 1e-08, 0.01, 10
VMEM_LIMIT = 56 * 1024 * 1024
MESH = pl.DeviceIdType.MESH


def _params(n_grid):
    return pltpu.CompilerParams(dimension_semantics=("arbitrary",) * n_grid,
                                vmem_limit_bytes=VMEM_LIMIT)


def _dot(a, b):
    return jnp.dot(a, b, preferred_element_type=F32)


def _dot_nt(a, b):
    return lax.dot_general(a, b, (((1,), (1,)), ((), ())), preferred_element_type=F32)


def _dot_tn(a, b):
    return lax.dot_general(a, b, (((0,), (0,)), ((), ())), preferred_element_type=F32)


def _rms(x):
    r = lax.rsqrt(jnp.mean(x * x, axis=-1, keepdims=True) + EPS)
    return x * r, r


def _rms_bwd(x, g, dn):
    xh, r = _rms(x)
    dxh = dn * g
    dx = r * (dxh - xh * jnp.mean(dxh * xh, axis=-1, keepdims=True))
    return dx, jnp.sum(dn * xh, axis=0, keepdims=True)


def _split2(v):
    hi = v.astype(BF16)
    return hi, (v - hi.astype(F32)).astype(BF16)


def _split3(v):
    hi = v.astype(BF16)
    r1 = v - hi.astype(F32)
    mid = r1.astype(BF16)
    return hi, mid, (r1 - mid.astype(F32)).astype(BF16)


def _group_ones():
    r = lax.broadcasted_iota(jnp.int32, (LANES, LANES), 0) // HEAD_DIM
    c = lax.broadcasted_iota(jnp.int32, (LANES, LANES), 1) // HEAD_DIM
    return jnp.where(r == c, 1.0, 0.0).astype(BF16)


def _group_mean(v, ones):
    hi, lo = _split2(v)
    return (_dot(hi, ones) + _dot(lo, ones)) * (1.0 / HEAD_DIM)


def _row_tile(m):
    return 384 if m % 384 == 0 else LANES


def _tile(m, cap):
    return max(t for t in range(16, cap + 1, 16) if m % t == 0)


def _peer(k):
    x, y, c = lax.axis_index("x"), lax.axis_index("y"), lax.axis_index("c")
    px = 1 - x if k & 4 else x
    py = 1 - y if k & 2 else y
    pc = 1 - c if k & 1 else c
    return (px, py, pc), 4 * px + 2 * py + pc


def _exchange_body(src_ref, dst_ref, send_sems, recv_sems, local_sem, bcast):
    x, y, c = lax.axis_index("x"), lax.axis_index("y"), lax.axis_index("c")
    me = 4 * x + 2 * y + c
    mine = pltpu.make_async_copy(src_ref.at[0 if bcast else me], dst_ref.at[me], local_sem)
    mine.start()
    sends = []
    for k in range(1, N_DEV):
        dev, idx = _peer(k)
        cp = pltpu.make_async_remote_copy(
            src_ref=src_ref.at[0 if bcast else idx], dst_ref=dst_ref.at[me],
            send_sem=send_sems.at[k - 1], recv_sem=recv_sems.at[k - 1],
            device_id=dev, device_id_type=MESH)
        cp.start()
        sends.append(cp)
    for k in range(1, N_DEV):
        dev, idx = _peer(k)
        pltpu.make_async_remote_copy(
            src_ref=src_ref.at[0], dst_ref=dst_ref.at[idx],
            send_sem=send_sems.at[k - 1], recv_sem=recv_sems.at[k - 1],
            device_id=dev, device_id_type=MESH).wait_recv()
    for cp in sends:
        cp.wait_send()
    mine.wait()


class Rider:
    FIRST = (1, 2, 4, 6)
    RELAYED = (2, 4, 6)

    def __init__(self, srcs=(), bcast=True):
        self.srcs, self.bcast, self.n = list(srcs), bcast, len(srcs)

    def out_shapes(self):
        return [jax.ShapeDtypeStruct(((N_DEV,) + s.shape) if self.bcast else s.shape, s.dtype) for s in self.srcs]

    def specs(self):
        return [pl.BlockSpec(memory_space=pl.ANY)] * self.n

    def scratch(self):
        if not self.n:
            return []
        return [pltpu.SemaphoreType.DMA((self.n * (N_DEV - 1),)), pltpu.SemaphoreType.DMA((self.n * (N_DEV - 1),)),
                pltpu.SemaphoreType.DMA((self.n,))]

    @staticmethod
    def _copy(src, dst, a, pair, dev, send_sems, recv_sems):
        sem = a * (N_DEV - 1) + pair - 1
        return pltpu.make_async_remote_copy(src_ref=src, dst_ref=dst, send_sem=send_sems.at[sem],
                                            recv_sem=recv_sems.at[sem], device_id=dev, device_id_type=MESH)

    def _first(self):
        return self.FIRST if self.bcast else range(1, N_DEV)

    def _own(self, s, d, a, local_sems):
        me = 4 * lax.axis_index("x") + 2 * lax.axis_index("y") + lax.axis_index("c")
        return pltpu.make_async_copy(s if self.bcast else s.at[me], d.at[me], local_sems.at[a]), me

    def start(self, src_refs, dst_refs, send_sems, recv_sems, local_sems):
        for a, (s, d) in enumerate(zip(src_refs, dst_refs)):
            own, me = self._own(s, d, a, local_sems)
            own.start()
            for k in self._first():
                dev, idx = _peer(k)
                self._copy(s if self.bcast else s.at[idx], d.at[me], a, k, dev, send_sems, recv_sems).start()

    def relay(self, src_refs, dst_refs, send_sems, recv_sems, local_sems):
        if not self.bcast:
            return
        sibling, _ = _peer(1)
        for a, d in enumerate(dst_refs):
            for k in self.RELAYED:
                dev, idx = _peer(k)
                self._copy(d.at[idx], d.at[idx], a, k, dev, send_sems, recv_sems).wait_recv()
                self._copy(d.at[idx], d.at[idx], a, k + 1, sibling, send_sems, recv_sems).start()

    def wait(self, src_refs, dst_refs, send_sems, recv_sems, local_sems):
        sibling, _ = _peer(1)
        for a, (s, d) in enumerate(zip(src_refs, dst_refs)):
            own, me = self._own(s, d, a, local_sems)
            for k in range(1, N_DEV):
                if not (self.bcast and k in self.RELAYED):
                    dev, idx = _peer(k)
                    self._copy(d.at[idx], d.at[idx], a, k, dev, send_sems, recv_sems).wait_recv()
            for k in self._first():
                dev, idx = _peer(k)
                self._copy(s if self.bcast else s.at[idx], d.at[me], a, k, dev, send_sems, recv_sems).wait_send()
            if self.bcast:
                for k in self.RELAYED:
                    dev, idx = _peer(k)
                    self._copy(d.at[idx], d.at[idx], a, k + 1, sibling, send_sems, recv_sems).wait_send()
            own.wait()


def rider_call(core, name, grid, in_specs, out_specs, out_shape, scratch_shapes, args, rider=None):
    rider = rider or Rider()
    n_in, n_out, n_scr, nr = len(in_specs), len(out_specs), len(scratch_shapes), rider.n

    def body(*refs):
        ins, r_src = refs[:n_in], refs[n_in:n_in + nr]
        outs = refs[n_in + nr:n_in + nr + n_out]
        r_dst = refs[n_in + nr + n_out:n_in + 2 * nr + n_out]
        scr = refs[n_in + 2 * nr + n_out:n_in + 2 * nr + n_out + n_scr]
        sems = refs[n_in + 2 * nr + n_out + n_scr:]
        if nr:
            first, relay, last = True, True, True
            for ax, size in enumerate(grid):
                first = first & (pl.program_id(ax) == 0)
                relay = relay & (pl.program_id(ax) == (3 * size // 4 if ax == 0 else 0))
                last = last & (pl.program_id(ax) == size - 1)
            if not grid:
                rider.start(r_src, r_dst, *sems)
                rider.relay(r_src, r_dst, *sems)
            else:
                pl.when(first)(lambda: rider.start(r_src, r_dst, *sems))
                if rider.bcast:
                    pl.when(relay)(lambda: rider.relay(r_src, r_dst, *sems))
        core(*ins, *outs, *scr)
        if nr:
            if not grid:
                rider.wait(r_src, r_dst, *sems)
            else:
                pl.when(last)(lambda: rider.wait(r_src, r_dst, *sems))

    res = pl.pallas_call(
        body, name=name, grid=grid,
        in_specs=list(in_specs) + rider.specs(),
        out_specs=list(out_specs) + rider.specs(),
        out_shape=list(out_shape) + rider.out_shapes(),
        scratch_shapes=list(scratch_shapes) + rider.scratch(),
        compiler_params=_params(len(grid)),
    )(*args, *rider.srcs)
    return res[:n_out], res[n_out:]


def exchange_hbm(srcs, bcast, name):
    return rider_call(lambda: None, name, (), [], [], [], [], [], Rider(srcs, bcast))[1]


def allsum_small(vec, name):
    def body(src_ref, out_ref, dst_ref, send_sems, recv_sems, local_sem):
        _exchange_body(src_ref, dst_ref, send_sems, recv_sems, local_sem, True)
        acc = dst_ref[0]
        for j in range(1, N_DEV):
            acc = acc + dst_ref[j]
        out_ref[...] = acc

    return pl.pallas_call(
        body, name=name,
        out_shape=jax.ShapeDtypeStruct(vec.shape[1:], F32),
        in_specs=[pl.BlockSpec(memory_space=pltpu.VMEM)],
        out_specs=pl.BlockSpec(memory_space=pltpu.VMEM),
        scratch_shapes=[pltpu.VMEM((N_DEV,) + vec.shape[1:], F32),
                        pltpu.SemaphoreType.DMA((N_DEV - 1,)), pltpu.SemaphoreType.DMA((N_DEV - 1,)),
                        pltpu.SemaphoreType.DMA],
    )(vec)


def gather_small(vec, name):
    def body(src_ref, dst_ref, send_sems, recv_sems, local_sem):
        _exchange_body(src_ref, dst_ref, send_sems, recv_sems, local_sem, True)

    return pl.pallas_call(
        body, name=name,
        out_shape=jax.ShapeDtypeStruct((N_DEV,) + vec.shape[1:], F32),
        in_specs=[pl.BlockSpec(memory_space=pltpu.VMEM)],
        out_specs=pl.BlockSpec(memory_space=pltpu.VMEM),
        scratch_shapes=[pltpu.SemaphoreType.DMA((N_DEV - 1,)), pltpu.SemaphoreType.DMA((N_DEV - 1,)),
                        pltpu.SemaphoreType.DMA],
    )(vec)


FFN_FWD_ROWS = 1056
FFN_BWD_ROWS = 704
DW_ROWS = 1408

def ffn_fwd(h, g, w_in8, w_out4, name, rider=None):
    m, d = h.shape
    fb = w_in8.shape[2]
    tm = _tile(m, FFN_FWD_ROWS)

    def body(h_ref, g_ref, wg_ref, wu_ref, wo_ref, hn_ref, n_ref, gate_ref, up_ref, acc_ref):
        i = pl.program_id(1)

        @pl.when(i == 0)
        def _():
            xh, _ = _rms(h_ref[...])
            n_ref[...] = (xh * g_ref[...]).astype(BF16)
            acc_ref[...] = jnp.zeros_like(acc_ref)

        n = n_ref[...]
        gate = _dot(n, wg_ref[0])
        up = _dot(n, wu_ref[0])
        gate_ref[0] = gate
        up_ref[0] = up
        a = (gate * jax.nn.sigmoid(gate) * up).astype(BF16)
        acc_ref[...] += _dot(a, wo_ref[0])

        @pl.when(i == 3)
        def _():
            hn_ref[...] = h_ref[...] + 0.5 * acc_ref[...]

    return rider_call(
        body, name, (m // tm, 4),
        in_specs=[pl.BlockSpec((tm, d), lambda r, i: (r, 0)),
                  pl.BlockSpec((1, d), lambda r, i: (0, 0)),
                  pl.BlockSpec((1, d, fb), lambda r, i: (i, 0, 0)),
                  pl.BlockSpec((1, d, fb), lambda r, i: (i + 4, 0, 0)),
                  pl.BlockSpec((1, fb, d), lambda r, i: (i, 0, 0))],
        out_specs=[pl.BlockSpec((tm, d), lambda r, i: (r, 0)),
                   pl.BlockSpec((tm, d), lambda r, i: (r, 0)),
                   pl.BlockSpec((1, tm, fb), lambda r, i: (i, r, 0)),
                   pl.BlockSpec((1, tm, fb), lambda r, i: (i, r, 0))],
        out_shape=[jax.ShapeDtypeStruct((m, d), F32), jax.ShapeDtypeStruct((m, d), BF16),
                   jax.ShapeDtypeStruct((4, m, fb), F32), jax.ShapeDtypeStruct((4, m, fb), F32)],
        scratch_shapes=[pltpu.VMEM((tm, d), F32)],
        args=(h, g, w_in8, w_in8, w_out4), rider=rider)


def ffn_bwd(dh, h, g, gate, up, w_in8, w_out4, name, rider=None):
    m, d = h.shape
    fb = w_in8.shape[2]
    tm = _tile(m, FFN_BWD_ROWS)

    def body(dh_ref, h_ref, g_ref, gate_ref, up_ref, wg_ref, wu_ref, wo_ref,
             dhin_ref, a_ref, dg_ref, du_ref, dgn_ref, dhs_ref, acc_ref):
        r = pl.program_id(0)
        i = pl.program_id(1)

        @pl.when(i == 0)
        def _():
            dhs_ref[...] = (0.5 * dh_ref[...]).astype(BF16)
            acc_ref[...] = jnp.zeros_like(acc_ref)

        @pl.when((r == 0) & (i == 0))
        def _():
            dgn_ref[...] = jnp.zeros_like(dgn_ref)

        da = _dot_nt(dhs_ref[...], wo_ref[0])
        gt = gate_ref[0]
        u = up_ref[0]
        sg = jax.nn.sigmoid(gt)
        sl = gt * sg
        a_ref[0] = (sl * u).astype(BF16)
        dub = (da * sl).astype(BF16)
        dgb = (da * u * (sg * (1.0 + gt * (1.0 - sg)))).astype(BF16)
        dg_ref[0] = dgb
        du_ref[0] = dub
        acc_ref[...] += _dot_nt(dgb, wg_ref[0]) + _dot_nt(dub, wu_ref[0])

        @pl.when(i == 3)
        def _():
            dx, dgain = _rms_bwd(h_ref[...], g_ref[...], acc_ref[...])
            dgn_ref[...] += dgain
            dhin_ref[...] = dh_ref[...] + dx

    row = lambda r, i: (r, 0)
    blk = lambda r, i: (i, r, 0)
    return rider_call(
        body, name, (m // tm, 4),
        in_specs=[pl.BlockSpec((tm, d), row), pl.BlockSpec((tm, d), row),
                  pl.BlockSpec((1, d), lambda r, i: (0, 0)),
                  pl.BlockSpec((1, tm, fb), blk), pl.BlockSpec((1, tm, fb), blk),
                  pl.BlockSpec((1, d, fb), lambda r, i: (i, 0, 0)),
                  pl.BlockSpec((1, d, fb), lambda r, i: (i + 4, 0, 0)),
                  pl.BlockSpec((1, fb, d), lambda r, i: (i, 0, 0))],
        out_specs=[pl.BlockSpec((tm, d), row),
                   pl.BlockSpec((1, tm, fb), blk), pl.BlockSpec((1, tm, fb), blk),
                   pl.BlockSpec((1, tm, fb), blk),
                   pl.BlockSpec((1, d), lambda r, i: (0, 0)),
                   pl.BlockSpec((tm, d), row)],
        out_shape=[jax.ShapeDtypeStruct((m, d), F32),
                   jax.ShapeDtypeStruct((4, m, fb), BF16), jax.ShapeDtypeStruct((4, m, fb), BF16),
                   jax.ShapeDtypeStruct((4, m, fb), BF16),
                   jax.ShapeDtypeStruct((1, d), F32), jax.ShapeDtypeStruct((m, d), BF16)],
        scratch_shapes=[pltpu.VMEM((tm, d), F32)],
        args=(dh, h, g, gate, up, w_in8, w_in8, w_out4), rider=rider)


def matmul_tn(x, y, name, tn=None, y2=None, rider=None):
    bx, m, k = x.shape
    by, _, n = y.shape
    b = max(bx, by) * (2 if y2 is not None else 1)
    tm = _tile(m, DW_ROWS)
    tn = n if tn is None else tn
    nt = n // tn
    nr = m // tm

    def body(*refs):
        x_ref, y_ref = refs[0], refs[1]
        o_ref, acc_ref = refs[-2], refs[-1]
        r = pl.program_id(2)

        @pl.when(r == 0)
        def _():
            acc_ref[...] = jnp.zeros_like(acc_ref)

        if y2 is None:
            acc_ref[...] += _dot_tn(x_ref[0].astype(BF16), y_ref[0].astype(BF16))
        else:
            @pl.when(pl.program_id(0) < by)
            def _():
                acc_ref[...] += _dot_tn(x_ref[0].astype(BF16), y_ref[0].astype(BF16))

            @pl.when(pl.program_id(0) >= by)
            def _():
                acc_ref[...] += _dot_tn(x_ref[0].astype(BF16), refs[2][0].astype(BF16))

        @pl.when(r == nr - 1)
        def _():
            o_ref[0] = acc_ref[...].astype(BF16)

    x_map = (lambda i, j, r: (i, r, 0)) if bx > 1 else (lambda i, j, r: (0, r, 0))
    if y2 is None:
        y_specs = [pl.BlockSpec((1, tm, tn), (lambda i, j, r: (i, r, j)) if by > 1 else (lambda i, j, r: (0, r, j)))]
    else:
        y_specs = [pl.BlockSpec((1, tm, tn), lambda i, j, r: (jnp.minimum(i, by - 1), jnp.where(i < by, r, nr - 1), j)),
                   pl.BlockSpec((1, tm, tn), lambda i, j, r: (jnp.maximum(i - by, 0), jnp.where(i < by, 0, r), j))]
    (out,), carried = rider_call(
        body, name, (b, nt, nr),
        in_specs=[pl.BlockSpec((1, tm, k), x_map)] + y_specs,
        out_specs=[pl.BlockSpec((1, k, tn), lambda i, j, r: (i, 0, j))],
        out_shape=[jax.ShapeDtypeStruct((b, k, n), BF16)],
        scratch_shapes=[pltpu.VMEM((k, tn), F32)],
        args=[x, y] + ([y2] if y2 is not None else []), rider=rider)
    return (out, carried) if rider is not None else out


AUG = HEAD_DIM


class _Cols:
    def __init__(self, d):
        self.d = d
        self.ga, self.gb = 0, d
        self.qa, self.ka, self.va = 2 * d, 2 * d + 1024, 2 * d + 2048
        self.qb = 2 * d + 3072
        self.kb, self.vb, self.fa = self.qb + 512, self.qb + 640, self.qb + 768
        self.np = self.qb + 1024


def mixer_proj(h, g, wp, name):
    m, d = h.shape
    npad = wp.shape[1]
    tm = _row_tile(m)

    def body(h_ref, g_ref, w_ref, n_ref, p_ref):
        xh, _ = _rms(h_ref[...])
        n = (xh * g_ref[...]).astype(BF16)
        n_ref[...] = n
        p_ref[...] = _dot(n, w_ref[...])

    return pl.pallas_call(
        body, name=name, grid=(m // tm,),
        in_specs=[pl.BlockSpec((tm, d), lambda r: (r, 0)), pl.BlockSpec((1, d), lambda r: (0, 0)),
                  pl.BlockSpec((d, npad), lambda r: (0, 0))],
        out_specs=[pl.BlockSpec((tm, d), lambda r: (r, 0)), pl.BlockSpec((tm, npad), lambda r: (r, 0))],
        out_shape=[jax.ShapeDtypeStruct((m, d), BF16), jax.ShapeDtypeStruct((m, npad), F32)],
        compiler_params=_params(1),
    )(h, g, wp)


def _head_norm(x, gain, ones):
    outs = []
    for b in range(x.shape[1] // LANES):
        xb = x[:, b * LANES:(b + 1) * LANES]
        r = lax.rsqrt(_group_mean(xb * xb, ones) + EPS)
        outs.append(xb * r * gain[:, b * LANES:(b + 1) * LANES])
    return outs


def _head_norm_bwd(x, gain, dn, ones):
    dxs, dgs = [], []
    for b in range(x.shape[1] // LANES):
        sl = slice(b * LANES, (b + 1) * LANES)
        xb, dnb = x[:, sl], dn[:, sl]
        r = lax.rsqrt(_group_mean(xb * xb, ones) + EPS)
        xh = xb * r
        dxh = dnb * gain[:, sl]
        dxs.append(r * (dxh - xh * _group_mean(dxh * xh, ones)))
        dgs.append(jnp.sum(dnb * xh, axis=0, keepdims=True))
    return dxs, dgs


def _lane_col(v, lane_iota, idx):
    return jnp.sum(jnp.where(lane_iota == idx, v, 0.0), axis=1, keepdims=True)


def _aug(base, lane, vals):
    for i, v in enumerate(vals):
        base = jnp.where(lane == AUG + i, v, base)
    return base


def qk_post(proj, gains, fbias, cols, name):
    m = proj.shape[0]
    tm = _row_tile(m)
    gqa, gka, gqb, gkb = gains

    def body(qa_ref, ka_ref, va_ref, qb_ref, kb_ref, vb_ref, fa_ref, gqa_ref, gka_ref, gqb_ref, gkb_ref, fb_ref,
             qf_o, kf_o, vf_o, qb_o, kb_o, vb_o, carry_ref):
        r0 = pl.program_id(0)

        @pl.when(r0 == 0)
        def _():
            carry_ref[...] = jnp.zeros_like(carry_ref)

        z = fa_ref[...] + fb_ref[...]
        logf = jnp.minimum(z, 0.0) - jnp.log(1.0 + jnp.exp(-jnp.abs(z)))
        rr = lax.broadcasted_iota(jnp.int32, (tm, tm), 0)
        cc = lax.broadcasted_iota(jnp.int32, (tm, tm), 1)
        tril = jnp.where(cc <= rr, 1.0, 0.0).astype(BF16)
        p0, p1, p2 = _split3(logf)
        c = _dot(tril, p0) + _dot(tril, p1) + _dot(tril, p2) + carry_ref[...]
        carry_ref[...] += jnp.sum(logf, axis=0, keepdims=True)

        lane = lax.broadcasted_iota(jnp.int32, (tm, LANES), 1)
        is_pad = (r0 * tm + lax.broadcasted_iota(jnp.int32, (tm, 1), 0)) < PAD_FRONT
        ones = jnp.ones((LANES, LANES), BF16)
        for hd in range(8):
            sl = slice(hd * LANES, (hd + 1) * LANES)
            ch = _lane_col(c, lane, hd)
            ct = [p.astype(F32) for p in _split3(ch)]
            cs = [p.astype(F32) for p in _split3(-jnp.where(is_pad, -NEG, ch))]
            xq = qa_ref[:, sl]
            qn = xq * lax.rsqrt(_group_mean(xq * xq, ones) + EPS) * gqa_ref[...]
            qf_o[:, sl] = _aug(qn, lane, ct + [1.0, 1.0, 1.0]).astype(BF16)
            xk = ka_ref[:, sl]
            kn = xk * lax.rsqrt(_group_mean(xk * xk, ones) + EPS) * gka_ref[...]
            kf_o[:, sl] = _aug(kn, lane, [1.0, 1.0, 1.0] + cs).astype(BF16)
            vf_o[:, sl] = _aug(va_ref[:, sl], lane, [1.0, 1.0, 1.0]).astype(BF16)

        gones = _group_ones()
        for src, gn, dst in ((qb_ref, gqb_ref, qb_o), (kb_ref, gkb_ref, kb_o)):
            for b, blk in enumerate(_head_norm(src[...], gn[...], gones)):
                dst[:, b * LANES:(b + 1) * LANES] = blk.astype(BF16)
        vb_o[...] = vb_ref[...].astype(BF16)

    w1024 = lambda off: pl.BlockSpec((tm, 1024), lambda r, o=off // 1024: (r, o))
    w512 = lambda off: pl.BlockSpec((tm, 512), lambda r, o=off // 512: (r, o))
    w128 = lambda off: pl.BlockSpec((tm, LANES), lambda r, o=off // LANES: (r, o))
    vec = lambda w: pl.BlockSpec((1, w), lambda r: (0, 0))
    row = lambda w: pl.BlockSpec((tm, w), lambda r: (r, 0))
    return pl.pallas_call(
        body, name=name, grid=(m // tm,),
        in_specs=[w1024(cols.qa), w1024(cols.ka), w1024(cols.va), w512(cols.qb), w128(cols.kb), w128(cols.vb),
                  w128(cols.fa), vec(LANES), vec(LANES), vec(512), vec(LANES), vec(LANES)],
        out_specs=[row(1024), row(1024), row(1024), row(512), row(LANES), row(LANES)],
        out_shape=[jax.ShapeDtypeStruct((m, 1024), BF16)] * 3 + [jax.ShapeDtypeStruct((m, 512), BF16)]
                  + [jax.ShapeDtypeStruct((m, LANES), BF16)] * 2,
        scratch_shapes=[pltpu.VMEM((1, LANES), F32)],
        compiler_params=_params(1),
    )(proj, proj, proj, proj, proj, proj, proj, gqa, gka, gqb, gkb, fbias)


def qk_post_bwd(proj, gains, fbias, dqf, dkf, dvf, dqb, dkb, dvb, dc, dga, dgb, cols, name):
    m = proj.shape[0]
    d = cols.d
    tm = _row_tile(m)
    nt = m // tm
    gqa, gka, gqb, gkb = gains

    def body(qa_ref, ka_ref, qb_ref, kb_ref, fa_ref, gqa_ref, gka_ref, gqb_ref, gkb_ref, fb_ref,
             dqf_ref, dkf_ref, dvf_ref, dqb_ref, dkb_ref, dvb_ref, dc_ref, dga_ref, dgb_ref,
             dp_o, ggqa_o, ggka_o, ggqb_o, ggkb_o, gfb_o, carry_ref):
        @pl.when(pl.program_id(0) == 0)
        def _():
            carry_ref[...] = jnp.zeros_like(carry_ref)
            for o in (ggqa_o, ggka_o, ggqb_o, ggkb_o, gfb_o):
                o[...] = jnp.zeros_like(o)

        dp_o[:, cols.ga:cols.ga + d] = dga_ref[...].astype(BF16)
        dp_o[:, cols.gb:cols.gb + d] = dgb_ref[...].astype(BF16)
        dp_o[:, cols.fa + LANES:cols.np] = jnp.zeros((tm, cols.np - cols.fa - LANES), BF16)
        lane = lax.broadcasted_iota(jnp.int32, (tm, LANES), 1)
        data = lane < HEAD_DIM
        ones = jnp.ones((LANES, LANES), BF16)
        for hd in range(8):
            sl = slice(hd * LANES, (hd + 1) * LANES)
            for src, gn, dn_ref, off, gout in ((qa_ref, gqa_ref, dqf_ref, cols.qa, ggqa_o),
                                               (ka_ref, gka_ref, dkf_ref, cols.ka, ggka_o)):
                x = src[:, sl]
                dn = jnp.where(data, dn_ref[:, sl], 0.0)
                r = lax.rsqrt(_group_mean(x * x, ones) + EPS)
                xh = x * r
                dxh = dn * gn[...]
                dp_o[:, off + hd * LANES:off + (hd + 1) * LANES] = (
                    r * (dxh - xh * _group_mean(dxh * xh, ones))).astype(BF16)
                gout[...] += jnp.sum(dn * xh, axis=0, keepdims=True)
            dp_o[:, cols.va + hd * LANES:cols.va + (hd + 1) * LANES] = jnp.where(data, dvf_ref[:, sl], 0.0).astype(BF16)
        dp_o[:, cols.vb:cols.vb + LANES] = dvb_ref[...].astype(BF16)
        gones = _group_ones()
        for src, gn, dn, off, gout in ((qb_ref, gqb_ref, dqb_ref, cols.qb, ggqb_o),
                                       (kb_ref, gkb_ref, dkb_ref, cols.kb, ggkb_o)):
            dxs, dgs = _head_norm_bwd(src[...], gn[...], dn[...], gones)
            for b, (dx, dg) in enumerate(zip(dxs, dgs)):
                dp_o[:, off + b * LANES:off + (b + 1) * LANES] = dx.astype(BF16)
                gout[:, b * LANES:(b + 1) * LANES] += dg
        dcv = dc_ref[...]
        rr = lax.broadcasted_iota(jnp.int32, (tm, tm), 0)
        cc = lax.broadcasted_iota(jnp.int32, (tm, tm), 1)
        triu = jnp.where(cc >= rr, 1.0, 0.0).astype(BF16)
        p0, p1, p2 = _split3(dcv)
        dlogf = _dot(triu, p0) + _dot(triu, p1) + _dot(triu, p2) + carry_ref[...]
        carry_ref[...] += jnp.sum(dcv, axis=0, keepdims=True)
        z = fa_ref[...] + fb_ref[...]
        row = (nt - 1 - pl.program_id(0)) * tm + lax.broadcasted_iota(jnp.int32, (tm, LANES), 0)
        dfa = jnp.where(row >= PAD_FRONT, dlogf * jax.nn.sigmoid(-z), 0.0)
        dp_o[:, cols.fa:cols.fa + LANES] = dfa.astype(BF16)
        gfb_o[...] += jnp.sum(dfa, axis=0, keepdims=True)

    rev = lambda r: nt - 1 - r
    w1024 = lambda off: pl.BlockSpec((tm, 1024), lambda r, o=off // 1024: (rev(r), o))
    w512 = lambda off: pl.BlockSpec((tm, 512), lambda r, o=off // 512: (rev(r), o))
    w128 = lambda off: pl.BlockSpec((tm, LANES), lambda r, o=off // LANES: (rev(r), o))
    vec = lambda w: pl.BlockSpec((1, w), lambda r: (0, 0))
    row = lambda w: pl.BlockSpec((tm, w), lambda r: (rev(r), 0))
    return pl.pallas_call(
        body, name=name, grid=(nt,),
        in_specs=[w1024(cols.qa), w1024(cols.ka), w512(cols.qb), w128(cols.kb), w128(cols.fa),
                  vec(LANES), vec(LANES), vec(512), vec(LANES), vec(LANES),
                  row(1024), row(1024), row(1024), row(512), row(LANES), row(LANES), row(LANES), row(d), row(d)],
        out_specs=[row(cols.np), vec(LANES), vec(LANES), vec(512), vec(LANES), vec(LANES)],
        out_shape=[jax.ShapeDtypeStruct((m, cols.np), BF16), jax.ShapeDtypeStruct((1, LANES), F32),
                   jax.ShapeDtypeStruct((1, LANES), F32), jax.ShapeDtypeStruct((1, 512), F32),
                   jax.ShapeDtypeStruct((1, LANES), F32), jax.ShapeDtypeStruct((1, LANES), F32)],
        scratch_shapes=[pltpu.VMEM((1, LANES), F32)],
        compiler_params=_params(1),
    )(proj, proj, proj, proj, proj, gqa, gka, gqb, gkb, fbias, dqf, dkf, dvf, dqb, dkb, dvb, dc, dga, dgb)


def dproj_bwd(dh, h, g, dproj, wp, name):
    m, d = h.shape
    npad = wp.shape[1]
    tm = _row_tile(m)

    def body(dh_ref, h_ref, g_ref, dp_ref, w_ref, dhin_ref, dgn_ref):
        @pl.when(pl.program_id(0) == 0)
        def _():
            dgn_ref[...] = jnp.zeros_like(dgn_ref)

        dn = _dot_nt(dp_ref[...], w_ref[...])
        dx, dgain = _rms_bwd(h_ref[...], g_ref[...], dn)
        dgn_ref[...] += dgain
        dhin_ref[...] = dh_ref[...] + dx

    row = lambda w: pl.BlockSpec((tm, w), lambda r: (r, 0))
    return pl.pallas_call(
        body, name=name, grid=(m // tm,),
        in_specs=[row(d), row(d), pl.BlockSpec((1, d), lambda r: (0, 0)), row(npad),
                  pl.BlockSpec((d, npad), lambda r: (0, 0))],
        out_specs=[row(d), pl.BlockSpec((1, d), lambda r: (0, 0))],
        out_shape=[jax.ShapeDtypeStruct((m, d), F32), jax.ShapeDtypeStruct((1, d), F32)],
        compiler_params=_params(1),
    )(dh, h, g, dproj, wp)


def _causal_t(t):
    return lax.broadcasted_iota(jnp.int32, (t, t), 0) <= lax.broadcasted_iota(jnp.int32, (t, t), 1)


HEADS_PER_STEP = 2


def fox_fwd(qf, kf, vt, name, rider=None):
    m = qf.shape[0]
    t = _row_tile(m)
    nq = m // t
    hp = HEADS_PER_STEP
    w = hp * LANES

    def body(q_ref, k_ref, vt_ref, o_ref, lse_ref, acc_ref, m_ref, p_ref, a_ref):
        qi = pl.program_id(1)
        acc_ref[...] = jnp.zeros_like(acc_ref)
        m_ref[...] = jnp.full_like(m_ref, NEG)

        def scores(ki, slot, mask):
            off = pl.multiple_of(ki * t, t)
            for e in range(hp):
                sl = slice(e * LANES, (e + 1) * LANES)
                s = _dot_nt(k_ref[pl.ds(off, t), sl], q_ref[:, sl])
                if mask is not None:
                    s = jnp.where(mask, s, NEG)
                m_old = m_ref[e]
                m_new = jnp.maximum(m_old, jnp.max(s, axis=0, keepdims=True))
                p_ref[slot, e] = jnp.exp(s - m_new).astype(BF16)
                a_ref[slot, e] = jnp.exp(m_old - m_new)
                m_ref[e] = m_new

        def values(ki, slot):
            off = pl.multiple_of(ki * t, t)
            for e in range(hp):
                sl = slice(e * LANES, (e + 1) * LANES)
                acc_ref[e] = acc_ref[e] * a_ref[slot, e] + _dot(vt_ref[sl, pl.ds(off, t)], p_ref[slot, e])

        causal = _causal_t(t)
        scores(0, 0, causal | (jnp.full((t, t), qi, jnp.int32) > 0))

        def step(ki, carry):
            values(ki - 1, (ki - 1) % 2)
            scores(ki, ki % 2, None)
            return carry

        lax.fori_loop(1, qi, step, 0)

        @pl.when(qi >= 1)
        def _():
            values(qi - 1, (qi - 1) % 2)
            scores(qi, qi % 2, causal)

        values(qi, qi % 2)
        row = lax.broadcasted_iota(jnp.int32, (LANES, t), 0)
        for e in range(hp):
            l = jnp.max(acc_ref[e, AUG:AUG + 8, :], axis=0, keepdims=True)
            o_ref[e * LANES:(e + 1) * LANES, :] = jnp.where(row < HEAD_DIM, acc_ref[e] * (1.0 / l), 0.0)
            lse_ref[e] = m_ref[e] + jnp.log(l)

    return rider_call(
        body, name, (8 // hp, nq),
        in_specs=[pl.BlockSpec((t, w), lambda hd, i: (i, hd)),
                  pl.BlockSpec((m, w), lambda hd, i: (0, hd)),
                  pl.BlockSpec((w, m), lambda hd, i: (hd, 0))],
        out_specs=[pl.BlockSpec((w, t), lambda hd, i: (hd, i)),
                   pl.BlockSpec((hp, 1, t), lambda hd, i: (hd, 0, i))],
        out_shape=[jax.ShapeDtypeStruct((8 * LANES, m), F32), jax.ShapeDtypeStruct((8, 1, m), F32)],
        scratch_shapes=[pltpu.VMEM((hp, LANES, t), F32), pltpu.VMEM((hp, 1, t), F32),
                        pltpu.VMEM((2, hp, t, t), BF16), pltpu.VMEM((2, hp, 1, t), F32)],
        args=(qf, kf, vt), rider=rider)


def fox_bwd(qf, kf, vf, kt, dof, lse, delta, name, rider=None):
    m = qf.shape[0]
    t = _row_tile(m)
    nq = m // t
    hp = HEADS_PER_STEP
    w = hp * LANES

    def body(k_ref, v_ref, kt_ref, q_ref, do_ref, lse_ref, delta_ref, dk_ref, dv_ref, dq_ref, dka_ref, dva_ref):
        ki = pl.program_id(1)

        @pl.when(ki == 0)
        def _():
            dq_ref[...] = jnp.zeros_like(dq_ref)

        dka_ref[...] = jnp.zeros_like(dka_ref)
        dva_ref[...] = jnp.zeros_like(dva_ref)

        def tile(qi, diagonal):
            off = pl.multiple_of(qi * t, t)
            for e in range(hp):
                sl = slice(e * LANES, (e + 1) * LANES)
                q = q_ref[pl.ds(off, t), sl]
                do = do_ref[pl.ds(off, t), sl]
                s = _dot_nt(k_ref[:, sl], q)
                if diagonal:
                    s = jnp.where(_causal_t(t), s, NEG)
                p = jnp.exp(s - lse_ref[e, :, pl.ds(off, t)])
                ds = (p * (_dot_nt(v_ref[:, sl], do) - delta_ref[e, :, pl.ds(off, t)])).astype(BF16)
                dva_ref[:, sl] += _dot(p.astype(BF16), do)
                dka_ref[:, sl] += _dot(ds, q)
                dq_ref[sl, pl.ds(off, t)] += _dot(kt_ref[sl, :], ds)

        def step(qi, carry):
            tile(qi, False)
            return carry

        tile(ki, True)
        lax.fori_loop(ki + 1, nq, step, 0)
        dk_ref[...] = dka_ref[...]
        dv_ref[...] = dva_ref[...]

    tile_spec = pl.BlockSpec((t, w), lambda hd, i: (i, hd))
    full = pl.BlockSpec((m, w), lambda hd, i: (0, hd))
    stat = pl.BlockSpec((hp, 1, m), lambda hd, i: (hd, 0, 0))
    return rider_call(
        body, name, (8 // hp, nq),
        in_specs=[tile_spec, tile_spec, pl.BlockSpec((w, t), lambda hd, i: (hd, i)), full, full, stat, stat],
        out_specs=[tile_spec, tile_spec, pl.BlockSpec((w, m), lambda hd, i: (hd, 0))],
        out_shape=[jax.ShapeDtypeStruct((m, 8 * LANES), F32), jax.ShapeDtypeStruct((m, 8 * LANES), F32),
                   jax.ShapeDtypeStruct((8 * LANES, m), F32)],
        scratch_shapes=[pltpu.VMEM((t, w), F32), pltpu.VMEM((t, w), F32)],
        args=(kf, vf, kt, qf, dof, lse, delta), rider=rider)


def _bucket_ids():
    def bucket(dist):
        n = np.maximum(dist, 0)
        max_exact = N_BUCKETS // 2
        nf = np.maximum(n, 1).astype(np.float32)
        large = max_exact + (np.log(nf / max_exact) / math.log(MAX_DISTANCE / max_exact)
                             * (N_BUCKETS - max_exact)).astype(np.int32)
        return np.where(n < max_exact, n, np.minimum(large, N_BUCKETS - 1))

    tl = np.arange(LANES)[:, None]
    sl = np.arange(LANES)[None, :]
    prev = bucket(LANES + tl - sl)
    cur = bucket(tl - sl)
    meta = np.full((LANES, LANES), N_BUCKETS - 1)
    return np.concatenate([prev, cur, meta], axis=1).astype(np.int32)


def bias_build(table, name):
    ids = jnp.asarray(_bucket_ids())

    def body(t_ref, id_ref, o_ref):
        idv = id_ref[...]
        for h in range(8):
            acc = jnp.zeros((LANES, 3 * LANES), F32)
            for b in range(N_BUCKETS):
                acc = jnp.where(idv == b, t_ref[b, h], acc)
            o_ref[h] = acc

    return pl.pallas_call(
        body, name=name,
        in_specs=[pl.BlockSpec(memory_space=pltpu.SMEM), pl.BlockSpec(memory_space=pltpu.VMEM)],
        out_specs=pl.BlockSpec(memory_space=pltpu.VMEM),
        out_shape=jax.ShapeDtypeStruct((8, LANES, 3 * LANES), F32),
    )(table, ids)


def bias_reduce(dbias, name):
    ids = jnp.asarray(_bucket_ids())

    def body(d_ref, id_ref, o_ref):
        idv = id_ref[...]
        rr = lax.broadcasted_iota(jnp.int32, (N_BUCKETS, LANES), 0)
        cc = lax.broadcasted_iota(jnp.int32, (N_BUCKETS, LANES), 1)
        acc = jnp.zeros((N_BUCKETS, LANES), F32)
        for h in range(8):
            dv = d_ref[h]
            for b in range(N_BUCKETS):
                val = jnp.sum(jnp.where(idv == b, dv, 0.0), keepdims=True)
                acc = jnp.where((rr == b) & (cc == h), val, acc)
        o_ref[...] = acc

    return pl.pallas_call(
        body, name=name,
        in_specs=[pl.BlockSpec(memory_space=pltpu.VMEM), pl.BlockSpec(memory_space=pltpu.VMEM)],
        out_specs=pl.BlockSpec(memory_space=pltpu.VMEM),
        out_shape=jax.ShapeDtypeStruct((N_BUCKETS, LANES), F32),
    )(dbias, ids)


def _swa_valid(n):
    shape = (LANES, 3 * LANES)
    tl = lax.broadcasted_iota(jnp.int32, shape, 0)
    col = lax.broadcasted_iota(jnp.int32, shape, 1)
    sl = col & (LANES - 1)
    nv = jnp.full(shape, n, jnp.int32)
    is_meta = sl >= PAD_FRONT
    prev = (col < LANES) & (sl > tl) & (nv >= 1) & ((nv >= 2) | is_meta)
    cur = (col >= LANES) & (col < 2 * LANES) & (sl <= tl) & ((nv >= 1) | is_meta)
    meta = (col >= 2 * LANES) & is_meta & ((nv >= 2) | ((nv == 1) & (sl <= tl)))
    return prev | cur | meta


def _swa_keys(ref, n):
    off_prev = pl.multiple_of(jnp.maximum(n - 1, 0) * LANES, LANES)
    off_cur = pl.multiple_of(n * LANES, LANES)
    return jnp.concatenate([ref[pl.ds(off_prev, LANES), :], ref[pl.ds(off_cur, LANES), :], ref[0:LANES, :]], axis=0)


def swa_fwd(q, k, v, bias, sinks, name):
    m = q.shape[0]

    def body(q_ref, k_ref, v_ref, bias_ref, sink_ref, o_ref, lse_ref):
        n = pl.program_id(0)
        lane1 = lax.broadcasted_iota(jnp.int32, (1, LANES), 1)
        lane_t = lax.broadcasted_iota(jnp.int32, (LANES, LANES), 1)
        in_head = [lane1 < HEAD_DIM, lane1 >= HEAD_DIM]
        kall = _swa_keys(k_ref, n)
        vall = _swa_keys(v_ref, n)
        vs = [jnp.where(in_head[g], vall, jnp.zeros_like(vall)) for g in (0, 1)]
        valid = _swa_valid(n)
        lse = jnp.zeros((LANES, LANES), F32)
        for b in range(4):
            qb = q_ref[:, b * LANES:(b + 1) * LANES]
            ob = jnp.zeros((LANES, LANES), F32)
            for g in (0, 1):
                h = 4 * g + b
                qe = jnp.where(in_head[g], qb, jnp.zeros_like(qb))
                s = jnp.where(valid, _dot_nt(qe, kall) + bias_ref[h], NEG)
                sink = sink_ref[h]
                mx = jnp.maximum(jnp.max(s, axis=1, keepdims=True), sink)
                p = jnp.exp(s - mx)
                den = jnp.sum(p, axis=1, keepdims=True) + jnp.exp(sink - mx)
                ob = ob + _dot((p / den).astype(BF16), vs[g])
                lse = jnp.where(lane_t == h, mx + jnp.log(den), lse)
            o_ref[:, b * LANES:(b + 1) * LANES] = ob
        lse_ref[...] = lse

    return pl.pallas_call(
        body, name=name, grid=(m // LANES,),
        in_specs=[pl.BlockSpec((LANES, 512), lambda n: (n, 0)),
                  pl.BlockSpec((m, LANES), lambda n: (0, 0)), pl.BlockSpec((m, LANES), lambda n: (0, 0)),
                  pl.BlockSpec((8, LANES, 3 * LANES), lambda n: (0, 0, 0)),
                  pl.BlockSpec(memory_space=pltpu.SMEM)],
        out_specs=[pl.BlockSpec((LANES, 512), lambda n: (n, 0)), pl.BlockSpec((LANES, LANES), lambda n: (n, 0))],
        out_shape=[jax.ShapeDtypeStruct((m, 512), F32), jax.ShapeDtypeStruct((m, LANES), F32)],
        compiler_params=_params(1),
    )(q, k, v, bias, sinks)


def swa_bwd(q, k, v, bias, sinks, o, lse, do, name):
    m = q.shape[0]

    def body(q_ref, do_ref, o_ref, lse_ref, k_ref, v_ref, bias_ref, sink_ref,
             dq_ref, dk_ref, dv_ref, dbias_ref, dsink_ref):
        n = pl.program_id(0)

        @pl.when(n == 0)
        def _():
            for r in (dk_ref, dv_ref, dbias_ref, dsink_ref):
                r[...] = jnp.zeros_like(r)

        lane1 = lax.broadcasted_iota(jnp.int32, (1, LANES), 1)
        lane_t = lax.broadcasted_iota(jnp.int32, (LANES, LANES), 1)
        in_head = [lane1 < HEAD_DIM, lane1 >= HEAD_DIM]
        off_prev = pl.multiple_of(jnp.maximum(n - 1, 0) * LANES, LANES)
        off_cur = pl.multiple_of(n * LANES, LANES)
        kall = _swa_keys(k_ref, n)
        vall = _swa_keys(v_ref, n)
        ks = [jnp.where(in_head[g], kall, jnp.zeros_like(kall)) for g in (0, 1)]
        valid = _swa_valid(n)
        lsev = lse_ref[...]
        dsink = dsink_ref[...]
        dkall = jnp.zeros((3 * LANES, LANES), F32)
        dvall = jnp.zeros((3 * LANES, LANES), F32)
        for b in range(4):
            sl = slice(b * LANES, (b + 1) * LANES)
            qb = q_ref[:, sl]
            dob = do_ref[:, sl]
            prod = dob * o_ref[:, sl]
            dqb = jnp.zeros((LANES, LANES), F32)
            for g in (0, 1):
                h = 4 * g + b
                qe = jnp.where(in_head[g], qb, jnp.zeros_like(qb))
                doe = jnp.where(in_head[g], dob, 0.0).astype(BF16)
                delta = jnp.sum(jnp.where(in_head[g], prod, 0.0), axis=1, keepdims=True)
                lse_h = _lane_col(lsev, lane_t, h)
                s = jnp.where(valid, _dot_nt(qe, kall) + bias_ref[h], NEG)
                p = jnp.exp(s - lse_h)
                ds = p * (_dot_nt(doe, vall) - delta)
                dbias_ref[h] += ds
                sink_part = jnp.sum(-jnp.exp(sink_ref[h] - lse_h) * delta, keepdims=True)
                dsink = jnp.where(lane1 == h, dsink + sink_part, dsink)
                dsb = ds.astype(BF16)
                dqb = dqb + _dot(dsb, ks[g])
                dkall = dkall + _dot_tn(dsb, qe)
                dvall = dvall + _dot_tn(p.astype(BF16), doe)
            dq_ref[:, sl] = dqb
        dsink_ref[...] = dsink
        for ref, val in ((dk_ref, dkall), (dv_ref, dvall)):
            ref[pl.ds(off_prev, LANES), :] += val[0:LANES]
            ref[pl.ds(off_cur, LANES), :] += val[LANES:2 * LANES]
            ref[0:LANES, :] += val[2 * LANES:3 * LANES]

    blk = pl.BlockSpec((LANES, 512), lambda n: (n, 0))
    full = pl.BlockSpec((m, LANES), lambda n: (0, 0))
    return pl.pallas_call(
        body, name=name, grid=(m // LANES,),
        in_specs=[blk, blk, blk, pl.BlockSpec((LANES, LANES), lambda n: (n, 0)), full, full,
                  pl.BlockSpec((8, LANES, 3 * LANES), lambda n: (0, 0, 0)),
                  pl.BlockSpec(memory_space=pltpu.SMEM)],
        out_specs=[blk, full, full, pl.BlockSpec((8, LANES, 3 * LANES), lambda n: (0, 0, 0)),
                   pl.BlockSpec((1, LANES), lambda n: (0, 0))],
        out_shape=[jax.ShapeDtypeStruct((m, 512), F32), jax.ShapeDtypeStruct((m, LANES), F32),
                   jax.ShapeDtypeStruct((m, LANES), F32), jax.ShapeDtypeStruct((8, LANES, 3 * LANES), F32),
                   jax.ShapeDtypeStruct((1, LANES), F32)],
        compiler_params=_params(1),
    )(q, do, o, lse, k, v, bias, sinks)


def branch_out(h, o_fox, o_swa, proj, wbf, wbs, wo, cols, name):
    m, d = h.shape
    tm = _row_tile(m)

    def body(h_ref, of_ref, os_ref, ga_ref, gb_ref, wbf_ref, wbs_ref, wo_ref, hn_ref):
        tf = _dot(of_ref[...].astype(BF16), wbf_ref[...])
        ts = _dot(os_ref[...].astype(BF16), wbs_ref[...])
        y = jax.nn.sigmoid(ga_ref[...]) * tf + jax.nn.sigmoid(gb_ref[...]) * ts
        hn_ref[...] = h_ref[...] + _dot(y.astype(BF16), wo_ref[...])

    row = lambda w, o=0: pl.BlockSpec((tm, w), lambda r, o=o: (r, o))
    res = lambda a: pl.BlockSpec(a.shape, lambda r: (0, 0))
    return pl.pallas_call(
        body, name=name, grid=(m // tm,),
        in_specs=[row(d), row(1024), row(512), row(d, cols.ga // d), row(d, cols.gb // d), res(wbf), res(wbs), res(wo)],
        out_specs=row(d),
        out_shape=jax.ShapeDtypeStruct((m, d), F32),
        compiler_params=_params(1),
    )(h, o_fox, o_swa, proj, proj, wbf, wbs, wo)


def branch_out_bwd(dh, o_fox, o_swa, proj, wbf, wbs, wo, cols, name):
    m, d = dh.shape
    tm = _row_tile(m)

    def body(dh_ref, of_ref, os_ref, ga_ref, gb_ref, wbf_ref, wbs_ref, wo_ref,
             y_ref, dtf_ref, dts_ref, dga_ref, dgb_ref, dof_ref, dos_ref, delta_ref):
        dy = _dot_nt(dh_ref[...].astype(BF16), wo_ref[...])
        tf = _dot(of_ref[...].astype(BF16), wbf_ref[...])
        ts = _dot(os_ref[...].astype(BF16), wbs_ref[...])
        sa = jax.nn.sigmoid(ga_ref[...])
        sb = jax.nn.sigmoid(gb_ref[...])
        y_ref[...] = (sa * tf + sb * ts).astype(BF16)
        dtf = (dy * sa).astype(BF16)
        dts = (dy * sb).astype(BF16)
        dtf_ref[...] = dtf
        dts_ref[...] = dts
        dga_ref[...] = (dy * tf * sa * (1.0 - sa)).astype(BF16)
        dgb_ref[...] = (dy * ts * sb * (1.0 - sb)).astype(BF16)
        dof = _dot_nt(dtf, wbf_ref[...])
        dof_ref[...] = dof.astype(BF16)
        dos_ref[...] = _dot_nt(dts, wbs_ref[...])
        lane = lax.broadcasted_iota(jnp.int32, (tm, LANES), 1)
        delta = jnp.zeros((tm, LANES), F32)
        for hd in range(8):
            sl = slice(hd * LANES, (hd + 1) * LANES)
            delta = jnp.where(lane == hd, jnp.sum(dof[:, sl] * of_ref[:, sl], axis=1, keepdims=True), delta)
        delta_ref[...] = delta

    row = lambda w, o=0: pl.BlockSpec((tm, w), lambda r, o=o: (r, o))
    res = lambda a: pl.BlockSpec(a.shape, lambda r: (0, 0))
    return pl.pallas_call(
        body, name=name, grid=(m // tm,),
        in_specs=[row(d), row(1024), row(512), row(d, cols.ga // d), row(d, cols.gb // d), res(wbf), res(wbs), res(wo)],
        out_specs=[row(d)] * 5 + [row(1024), row(512), row(LANES)],
        out_shape=[jax.ShapeDtypeStruct((m, d), BF16)] * 5 + [jax.ShapeDtypeStruct((m, 1024), BF16),
                   jax.ShapeDtypeStruct((m, 512), F32), jax.ShapeDtypeStruct((m, LANES), F32)],
        compiler_params=_params(1),
    )(dh, o_fox, o_swa, proj, proj, wbf, wbs, wo)


def loss_head(h, target, name):
    m, d = h.shape

    def body(h_ref, t_ref, dh_ref, loss_ref):
        n = pl.program_id(0)

        @pl.when(n == 0)
        def _():
            loss_ref[...] = jnp.zeros_like(loss_ref)
            dh_ref[...] = jnp.zeros_like(dh_ref)

        @pl.when(n > 0)
        def _():
            err = h_ref[...] - t_ref[...]
            dh_ref[...] = err * (1.0 / d)
            loss_ref[...] += jnp.sum(err * err, keepdims=True) * (0.5 / d)

    return pl.pallas_call(
        body, name=name, grid=(m // LANES,),
        in_specs=[pl.BlockSpec((LANES, d), lambda n: (n, 0)),
                  pl.BlockSpec((LANES, d), lambda n: (jnp.maximum(n - 1, 0), 0))],
        out_specs=[pl.BlockSpec((LANES, d), lambda n: (n, 0)), pl.BlockSpec((8, LANES), lambda n: (0, 0))],
        out_shape=[jax.ShapeDtypeStruct((m, d), F32), jax.ShapeDtypeStruct((8, LANES), F32)],
        compiler_params=_params(1),
    )(h, target)


def _adamw_math(w, g, m, v):
    m = ADAM_B1 * m + (1.0 - ADAM_B1) * g
    v = ADAM_B2 * v + (1.0 - ADAM_B2) * (g * g)
    m_hat = m / (1.0 - ADAM_B1 ** ADAM_STEP)
    v_hat = v / (1.0 - ADAM_B2 ** ADAM_STEP)
    delta = -ADAM_LR * (m_hat / (jnp.sqrt(v_hat) + ADAM_EPS) + ADAM_WD * w)
    return delta, m, v


def adamw_sum(parts, w, m, v, name):
    n_layers, a, b = w.shape
    ta = next(t for t in (256, 176, 128, a) if a % t == 0)
    nr = a // ta

    def body(*refs):
        p_refs = refs[:n_layers]
        w_ref, m_ref, v_ref, g_o, d_o, m_o, v_o = refs[n_layers:]
        for l in range(n_layers):
            @pl.when(pl.program_id(0) == l)
            def _(l=l):
                g = p_refs[l][0].astype(F32)
                for j in range(1, N_DEV):
                    g = g + p_refs[l][j].astype(F32)
                g_o[0] = g
                d_o[0], m_o[0], v_o[0] = _adamw_math(w_ref[0], g, m_ref[0], v_ref[0])

    def part_spec(l):
        return pl.BlockSpec((N_DEV, ta, b), lambda i, r, l=l: (0, jnp.where(i == l, r, jnp.where(i < l, 0, nr - 1)), 0))

    row = pl.BlockSpec((1, ta, b), lambda i, r: (i, r, 0))
    return pl.pallas_call(
        body, name=name, grid=(n_layers, nr),
        in_specs=[part_spec(l) for l in range(n_layers)] + [row, row, row],
        out_specs=[row] * 4,
        out_shape=[jax.ShapeDtypeStruct(w.shape, F32)] * 4,
        compiler_params=_params(2),
    )(*parts, w, m, v)


def adamw_small(g, w, m, v, name):
    def body(g_ref, w_ref, m_ref, v_ref, d_o, m_o, v_o):
        d_o[...], m_o[...], v_o[...] = _adamw_math(w_ref[...], g_ref[...], m_ref[...], v_ref[...])

    spec = pl.BlockSpec(memory_space=pltpu.VMEM)
    return pl.pallas_call(
        body, name=name, in_specs=[spec] * 4, out_specs=[spec] * 3,
        out_shape=[jax.ShapeDtypeStruct(w.shape, F32)] * 3,
    )(g, w, m, v)


BIG = ("ffn1_w_in", "ffn1_w_out", "w_in", "w_branch_fox", "w_branch_swa", "w_out", "ffn2_w_in", "ffn2_w_out")
SMALL = ("rel_bias_table", "ffn1_norm", "mix_norm", "forget_bias", "fox_q_norm", "fox_k_norm",
         "swa_q_norm", "swa_k_norm", "swa_sinks", "ffn2_norm")
WEIGHTS = ("meta_tokens", "rel_bias_table", "ffn1_norm", "ffn1_w_in", "ffn1_w_out", "mix_norm", "w_in",
           "forget_bias", "fox_q_norm", "fox_k_norm", "swa_q_norm", "swa_k_norm", "swa_sinks", "w_branch_fox",
           "w_branch_swa", "w_out", "ffn2_norm", "ffn2_w_in", "ffn2_w_out")


def _pack(arrs, width, row_multiple, dtype):
    lead = arrs[0].shape[:-1]
    flat = jnp.concatenate([a.astype(dtype) for a in arrs], axis=-1)
    n = flat.shape[-1]
    rows = -(-n // width)
    rows = -(-rows // row_multiple) * row_multiple
    flat = jnp.pad(flat, [(0, 0)] * len(lead) + [(0, rows * width - n)])
    return flat.reshape(lead + (rows, width))


def _unpack(flat, shapes):
    flat = flat.reshape(-1)
    out, off = [], 0
    for s in shapes:
        n = int(np.prod(s))
        out.append(flat[off:off + n].reshape(s))
        off += n
    return out


def _swa_head_order():
    return [4 * (j % 2) + j // 2 for j in range(8)]


def _permute_heads(a, axis, inverse=False):
    order = _swa_head_order()
    if inverse:
        order = [order.index(hd) for hd in range(8)]
    parts = [lax.slice_in_dim(a, hd * HEAD_DIM, (hd + 1) * HEAD_DIM, axis=axis) for hd in order]
    return jnp.concatenate(parts, axis=axis)


def _w_in_segments(cols):
    d = cols.d
    segs = [(512 * i + HEAD_DIM * hd, HEAD_DIM, new + LANES * hd)
            for i, new in enumerate((cols.qa, cols.ka, cols.va)) for hd in range(8)]
    segs.append((1536, 8, cols.fa))
    order = _swa_head_order()
    segs += [(1544 + HEAD_DIM * hd, HEAD_DIM, cols.qb + HEAD_DIM * order.index(hd)) for hd in range(8)]
    segs += [(2056, 128, cols.kb), (2184, 128, cols.vb), (2312, d, cols.ga), (2312 + d, d, cols.gb)]
    return segs


def _reorder_w_in(blocks, cols):
    width = blocks[0].shape[1]
    zeros = lambda n: jnp.zeros((blocks[0].shape[0], n), blocks[0].dtype)
    parts, at = [], 0
    for old, length, new in sorted(_w_in_segments(cols), key=lambda s: s[2]):
        if new > at:
            parts.append(zeros(new - at))
        at = new + length
        while length:
            j, off = divmod(old, width)
            take = min(length, width - off)
            parts.append(blocks[j][:, off:off + take])
            old, length = old + take, length - take
    parts.append(zeros(cols.np - at))
    return jnp.concatenate(parts, axis=1)


def _restore_w_in(wp, cols, width):
    segs = sorted(_w_in_segments(cols))
    blocks = []
    for j in range(N_DEV):
        lo, hi = j * width, (j + 1) * width
        parts = []
        for old, length, new in segs:
            a, b = max(old, lo), min(old + length, hi)
            if a < b:
                parts.append(wp[:, new + a - old:new + b - old])
        blocks.append(jnp.concatenate(parts, axis=1))
    return jnp.stack(blocks)


def _lane_pad(v):
    return jnp.pad(v, ((0, 0), (0, LANES - v.shape[1])))


def kernel(x, meta_tokens, rel_bias_table, ffn1_norm, ffn1_w_in, ffn1_w_out, mix_norm, w_in, forget_bias, fox_q_norm, fox_k_norm, swa_q_norm, swa_k_norm, swa_sinks, w_branch_fox, w_branch_swa, w_out, ffn2_norm, ffn2_w_in, ffn2_w_out, loss_target, m_meta_tokens, m_rel_bias_table, m_ffn1_norm, m_ffn1_w_in, m_ffn1_w_out, m_mix_norm, m_w_in, m_forget_bias, m_fox_q_norm, m_fox_k_norm, m_swa_q_norm, m_swa_k_norm, m_swa_sinks, m_w_branch_fox, m_w_branch_swa, m_w_out, m_ffn2_norm, m_ffn2_w_in, m_ffn2_w_out, v_meta_tokens, v_rel_bias_table, v_ffn1_norm, v_ffn1_w_in, v_ffn1_w_out, v_mix_norm, v_w_in, v_forget_bias, v_fox_q_norm, v_fox_k_norm, v_swa_q_norm, v_swa_k_norm, v_swa_sinks, v_w_branch_fox, v_w_branch_swa, v_w_out, v_ffn2_norm, v_ffn2_w_in, v_ffn2_w_out):
    args = dict(locals())
    wts = {n: args[n] for n in WEIGHTS}
    mom1 = {n: args["m_" + n] for n in WEIGHTS}
    mom2 = {n: args["v_" + n] for n in WEIGHTS}

    seq, d = x.shape[1], x.shape[2]
    m_rows = seq + LANES
    depth = ffn1_norm.shape[0]
    fb = ffn1_w_in.shape[2]
    fo = ffn1_w_out.shape[1]
    din_shard = w_in.shape[2]
    cols = _Cols(d)
    scale = HEAD_DIM ** -0.5
    dev = 4 * lax.axis_index("x") + 2 * lax.axis_index("y") + lax.axis_index("c")

    groups = {"ffn1": ("ffn1_w_in", "ffn1_w_out"), "mix": ("w_in", "w_branch_fox", "w_branch_swa", "w_out"),
              "ffn2": ("ffn2_w_in", "ffn2_w_out"), "ffn1_in": ("ffn1_w_in",), "ffn1_out": ("ffn1_w_out",),
              "ffn2_in": ("ffn2_w_in",), "ffn2_out": ("ffn2_w_out",)}
    shard = {n: wts[n].astype(BF16) for n in BIG}
    full, parts, gw = {}, {}, {}

    def keys_of(stages):
        return [(n, l) for g, l in stages if l < depth for n in groups[g]]

    def gather_rider(stages):
        return Rider([shard[n][l] for n, l in keys_of(stages)], True)

    def scatter_rider(stages):
        return Rider([gw[k] for k in keys_of(stages)], False)

    def ffn_weights(tag, l):
        return full[tag + "_w_in", l], full[tag + "_w_out", l].reshape(4, fb, d)

    def mixer_weights(l):
        wp = _reorder_w_in([full["w_in", l][j] for j in range(N_DEV)], cols)
        wbf = jnp.concatenate([full["w_branch_fox", l][j] for j in range(N_DEV)], axis=1)
        wbf = jnp.pad(wbf.reshape(8, HEAD_DIM, d), ((0, 0), (0, LANES - HEAD_DIM), (0, 0))).reshape(8 * LANES, d)
        wbs = _permute_heads(jnp.concatenate([full["w_branch_swa", l][j] for j in range(N_DEV)], axis=1), 0)
        return wp, wbf, wbs, full["w_out", l].reshape(d, d)

    full.update(zip(keys_of([("ffn1", 0)]), exchange_hbm(gather_rider([("ffn1", 0)]).srcs, True, "gather_first")))
    meta_all = gather_small(meta_tokens.reshape(1, N_META, -1), "gather_meta")
    meta_full = meta_all.transpose(1, 0, 2).reshape(N_META, d)
    tile8 = lambda g, s=1.0: jnp.tile(g.reshape(1, HEAD_DIM) * s, (1, 8))
    tile2 = lambda g: jnp.tile(g.reshape(1, HEAD_DIM), (1, 2))
    data_lanes = lambda g, s=1.0: _lane_pad(g.reshape(1, HEAD_DIM) * s)
    bias = bias_build(rel_bias_table, "swa_bias")

    first = jnp.concatenate([jnp.zeros((PAD_FRONT, d), F32), meta_full], axis=0)
    h = jnp.concatenate([first, x[0]], axis=0)
    saved, lw = [], []
    for l in range(depth):
        s, w = {"h0": h}, {}
        w["ffn1_in"], w["ffn1_out"] = ffn_weights("ffn1", l)
        stages = [("mix", 0)] if l == 0 else []
        (h, s["n1"], s["gate1"], s["up1"]), got = ffn_fwd(h, ffn1_norm[l:l + 1], w["ffn1_in"], w["ffn1_out"],
                                                          f"ffn1_fwd_{l}", gather_rider(stages))
        full.update(zip(keys_of(stages), got))
        s["h1"] = h
        w["wp"], w["wbf"], w["wbs"], w["wo"] = mixer_weights(l)
        s["nm"], s["proj"] = mixer_proj(h, mix_norm[l:l + 1], w["wp"], f"mixer_proj_{l}")
        s["gains"] = (data_lanes(fox_q_norm[l], scale), data_lanes(fox_k_norm[l]), tile8(swa_q_norm[l], scale),
                      tile2(swa_k_norm[l]))
        s["fbias"] = _lane_pad(forget_bias[l:l + 1])
        qf, kf, vf, qb, kb, vb = qk_post(s["proj"], s["gains"], s["fbias"], cols, f"qk_post_{l}")
        s.update(qf=qf, kf=kf, vf=vf, qb=qb, kb=kb, vb=vb)
        stages = [("ffn2", l), ("ffn1", l + 1)]
        (o_t, s["lse_fox"]), got = fox_fwd(qf, kf, vf.T, f"fox_fwd_{l}", gather_rider(stages))
        s["o_fox"] = o_t.T
        full.update(zip(keys_of(stages), got))
        s["o_swa"], s["lse_swa"] = swa_fwd(qb, kb, vb, bias, swa_sinks[l], f"swa_fwd_{l}")
        h = branch_out(h, s["o_fox"], s["o_swa"], s["proj"], w["wbf"], w["wbs"], w["wo"], cols, f"branch_out_{l}")
        s["h2"] = h
        w["ffn2_in"], w["ffn2_out"] = ffn_weights("ffn2", l)
        stages = [("mix", l + 1)]
        (h, s["n2"], s["gate2"], s["up2"]), got = ffn_fwd(h, ffn2_norm[l:l + 1], w["ffn2_in"], w["ffn2_out"],
                                                          f"ffn2_fwd_{l}", gather_rider(stages))
        full.update(zip(keys_of(stages), got))
        saved.append(s)
        lw.append(w)

    dh, loss_part = loss_head(h, loss_target[0], "loss_head")

    gs = {n: [None] * depth for n in SMALL}
    dbias_total = None
    for l in reversed(range(depth)):
        w, s = lw[l], saved[l]

        def ffn_back(dh, tag, hin, norm, n_in, gate, up, stages):
            (dh_in, a, dg, du, dgn, dhs), got = ffn_bwd(dh, hin, norm, gate, up, w[tag + "_in"], w[tag + "_out"],
                                                        f"{tag}_bwd_{l}", scatter_rider(stages))
            parts.update(zip(keys_of(stages), got))
            gw[tag + "_w_out", l] = matmul_tn(a, dhs[None], f"{tag}_dwo_{l}").reshape(N_DEV, fo, d)
            stages = [(tag + "_out", l)]
            gw[tag + "_w_in", l], got = matmul_tn(n_in[None], dg, f"{tag}_dwi_{l}", y2=du,
                                                  rider=scatter_rider(stages))
            parts.update(zip(keys_of(stages), got))
            return dh_in, dgn

        dh, gs["ffn2_norm"][l] = ffn_back(dh, "ffn2", s["h2"], ffn2_norm[l:l + 1], s["n2"], s["gate2"], s["up2"],
                                          [("ffn1_in", l + 1)])

        y, dtf, dts, dga, dgb, dof, dos, delta = branch_out_bwd(dh, s["o_fox"], s["o_swa"], s["proj"], w["wbf"],
                                                                w["wbs"], w["wo"], cols, f"branch_out_bwd_{l}")
        gw["w_out", l] = matmul_tn(y[None], dh[None], f"dw_out_{l}").reshape(N_DEV, d // N_DEV, d)
        to_shards = lambda a: a.reshape(512, N_DEV, d // N_DEV).transpose(1, 0, 2)
        gw["w_branch_fox", l] = to_shards(matmul_tn(s["o_fox"][None], dtf[None], f"dw_branch_fox_{l}")[0]
                                          .reshape(8, LANES, d)[:, :HEAD_DIM].reshape(512, d))
        gw["w_branch_swa", l] = to_shards(_permute_heads(
            matmul_tn(s["o_swa"][None], dts[None], f"dw_branch_swa_{l}")[0], 0, inverse=True))

        stages = [("ffn2_in", l)]
        (dkf, dvf, dqf_t), got = fox_bwd(s["qf"], s["kf"], s["vf"], s["kf"].T, dof, s["lse_fox"],
                                         delta[:, :8].T.reshape(8, 1, m_rows), f"fox_bwd_{l}", scatter_rider(stages))
        parts.update(zip(keys_of(stages), got))
        dc = _lane_pad(dqf_t.reshape(8, LANES, m_rows)[:, AUG].T - dkf.reshape(m_rows, 8, LANES)[:, :, AUG + 3])
        dqb, dkb, dvb, dbias, dsink = swa_bwd(s["qb"], s["kb"], s["vb"], bias, swa_sinks[l], s["o_swa"], s["lse_swa"],
                                              dos, f"swa_bwd_{l}")
        dbias_total = dbias if dbias_total is None else dbias_total + dbias
        gs["swa_sinks"][l] = dsink[0, :8]
        dproj, ggqa, ggka, ggqb, ggkb, gfb = qk_post_bwd(s["proj"], s["gains"], s["fbias"], dqf_t.T, dkf, dvf, dqb, dkb,
                                                         dvb, dc, dga, dgb, cols, f"qk_post_bwd_{l}")
        gs["fox_q_norm"][l] = ggqa[0, :HEAD_DIM] * scale
        gs["fox_k_norm"][l] = ggka[0, :HEAD_DIM]
        gs["swa_q_norm"][l] = ggqb.reshape(8, HEAD_DIM).sum(0) * scale
        gs["swa_k_norm"][l] = ggkb.reshape(2, HEAD_DIM).sum(0)
        gs["forget_bias"][l] = gfb[0, :8]
        dwp = matmul_tn(s["nm"][None], dproj[None], f"dw_in_{l}", tn=1024 if cols.np % 1024 == 0 else None)[0]
        gw["w_in", l] = _restore_w_in(dwp, cols, din_shard)
        dh, gs["mix_norm"][l] = dproj_bwd(dh, s["h1"], mix_norm[l:l + 1], dproj, w["wp"], f"dproj_bwd_{l}")

        dh, gs["ffn1_norm"][l] = ffn_back(dh, "ffn1", s["h0"], ffn1_norm[l:l + 1], s["n1"], s["gate1"], s["up1"],
                                          [("mix", l)])

    grad_x = dh[LANES:][None]
    dmeta = dh[PAD_FRONT:LANES]
    dtable = bias_reduce(dbias_total, "swa_dbias")[:, :8]

    parts.update(zip(keys_of([("ffn1_in", 0)]),
                     exchange_hbm(scatter_rider([("ffn1_in", 0)]).srcs, False, "scatter_last")))
    big_out = [{}, {}, {}, {}]
    for n in BIG:
        outs = adamw_sum([parts[n, l] for l in range(depth)], wts[n], mom1[n], mom2[n], f"adamw_{n}")
        for k in range(4):
            big_out[k][n] = outs[k]

    small_g = {n: (jnp.stack(gs[n]) if n != "rel_bias_table" else None) for n in SMALL}
    small_g["rel_bias_table"] = dtable
    pieces = [loss_part[0:1, 0:1].reshape(1, 1)] + [small_g[n].reshape(1, -1) for n in SMALL] + [dmeta.reshape(1, -1)]
    small_shapes = [(1,)] + [wts[n].shape for n in SMALL] + [(N_META, d)]
    total = allsum_small(_pack(pieces, LANES, 8, F32), "allsum_small")
    summed = _unpack(total, small_shapes)
    loss = summed[0][0]
    g_small = dict(zip(SMALL, summed[1:1 + len(SMALL)]))
    g_meta = lax.dynamic_slice_in_dim(summed[-1], dev * (d // N_DEV), d // N_DEV, axis=1)
    names = SMALL + ("meta_tokens",)
    g_small["meta_tokens"] = g_meta
    pk = lambda src: _pack([src[n].reshape(1, -1) for n in names], LANES, 8, F32)[0]
    small_out = [dict(zip(names, _unpack(o, [wts[n].shape for n in names])))
                 for o in adamw_small(pk(g_small), pk(wts), pk(mom1), pk(mom2), "adamw_small")]

    grads = {**big_out[0], **g_small}
    delta = {**big_out[1], **small_out[0]}
    new_m = {**big_out[2], **small_out[1]}
    new_v = {**big_out[3], **small_out[2]}
    return (loss, grad_x, *[grads[n] for n in WEIGHTS], *[delta[n] for n in WEIGHTS],
            *[new_m[n] for n in WEIGHTS], *[new_v[n] for n in WEIGHTS])
```

```python
import math

import numpy as np
import jax
import jax.numpy as jnp
from jax import lax
from jax.experimental import pallas as pl
from jax.experimental.pallas import tpu as pltpu

F32 = jnp.float32
BF16 = jnp.bfloat16
EPS = 1e-6
NEG = -1e30
HEAD_DIM = 64
LANES = 128
N_META = 16
PAD_FRONT = LANES - N_META
N_BUCKETS = 32
MAX_DISTANCE = 128
N_DEV = 8
ADAM_LR, ADAM_B1, ADAM_B2, ADAM_EPS, ADAM_WD, ADAM_STEP = 0.001, 0.9, 0.999, 1e-08, 0.01, 10
VMEM_LIMIT = 56 * 1024 * 1024
MESH = pl.DeviceIdType.MESH


def _params(n_grid):
    return pltpu.CompilerParams(dimension_semantics=("arbitrary",) * n_grid,
                                vmem_limit_bytes=VMEM_LIMIT)


def _dot(a, b):
    return jnp.dot(a, b, preferred_element_type=F32)


def _dot_nt(a, b):
    return lax.dot_general(a, b, (((1,), (1,)), ((), ())), preferred_element_type=F32)


def _dot_tn(a, b):
    return lax.dot_general(a, b, (((0,), (0,)), ((), ())), preferred_element_type=F32)


def _rms(x):
    r = lax.rsqrt(jnp.mean(x * x, axis=-1, keepdims=True) + EPS)
    return x * r, r


def _rms_bwd(x, g, dn):
    xh, r = _rms(x)
    dxh = dn * g
    dx = r * (dxh - xh * jnp.mean(dxh * xh, axis=-1, keepdims=True))
    return dx, jnp.sum(dn * xh, axis=0, keepdims=True)


def _split2(v):
    hi = v.astype(BF16)
    return hi, (v - hi.astype(F32)).astype(BF16)


def _split3(v):
    hi = v.astype(BF16)
    r1 = v - hi.astype(F32)
    mid = r1.astype(BF16)
    return hi, mid, (r1 - mid.astype(F32)).astype(BF16)


def _group_ones():
    r = lax.broadcasted_iota(jnp.int32, (LANES, LANES), 0) // HEAD_DIM
    c = lax.broadcasted_iota(jnp.int32, (LANES, LANES), 1) // HEAD_DIM
    return jnp.where(r == c, 1.0, 0.0).astype(BF16)


def _group_mean(v, ones):
    hi, lo = _split2(v)
    return (_dot(hi, ones) + _dot(lo, ones)) * (1.0 / HEAD_DIM)


def _row_tile(m):
    return 384 if m % 384 == 0 else LANES


def _tile(m, cap):
    return max(t for t in range(16, cap + 1, 16) if m % t == 0)


def _peer(k):
    x, y, c = lax.axis_index("x"), lax.axis_index("y"), lax.axis_index("c")
    px = 1 - x if k & 4 else x
    py = 1 - y if k & 2 else y
    pc = 1 - c if k & 1 else c
    return (px, py, pc), 4 * px + 2 * py + pc


def _exchange_body(src_ref, dst_ref, send_sems, recv_sems, local_sem, bcast):
    x, y, c = lax.axis_index("x"), lax.axis_index("y"), lax.axis_index("c")
    me = 4 * x + 2 * y + c
    mine = pltpu.make_async_copy(src_ref.at[0 if bcast else me], dst_ref.at[me], local_sem)
    mine.start()
    sends = []
    for k in range(1, N_DEV):
        dev, idx = _peer(k)
        cp = pltpu.make_async_remote_copy(
            src_ref=src_ref.at[0 if bcast else idx], dst_ref=dst_ref.at[me],
            send_sem=send_sems.at[k - 1], recv_sem=recv_sems.at[k - 1],
            device_id=dev, device_id_type=MESH)
        cp.start()
        sends.append(cp)
    for k in range(1, N_DEV):
        dev, idx = _peer(k)
        pltpu.make_async_remote_copy(
            src_ref=src_ref.at[0], dst_ref=dst_ref.at[idx],
            send_sem=send_sems.at[k - 1], recv_sem=recv_sems.at[k - 1],
            device_id=dev, device_id_type=MESH).wait_recv()
    for cp in sends:
        cp.wait_send()
    mine.wait()


class Rider:
    FIRST = (1, 2, 4, 6)
    RELAYED = (2, 4, 6)

    def __init__(self, srcs=(), bcast=True):
        self.srcs, self.bcast, self.n = list(srcs), bcast, len(srcs)

    def out_shapes(self):
        return [jax.ShapeDtypeStruct(((N_DEV,) + s.shape) if self.bcast else s.shape, s.dtype) for s in self.srcs]

    def specs(self):
        return [pl.BlockSpec(memory_space=pl.ANY)] * self.n

    def scratch(self):
        if not self.n:
            return []
        return [pltpu.SemaphoreType.DMA((self.n * (N_DEV - 1),)), pltpu.SemaphoreType.DMA((self.n * (N_DEV - 1),)),
                pltpu.SemaphoreType.DMA((self.n,))]

    @staticmethod
    def _copy(src, dst, a, pair, dev, send_sems, recv_sems):
        sem = a * (N_DEV - 1) + pair - 1
        return pltpu.make_async_remote_copy(src_ref=src, dst_ref=dst, send_sem=send_sems.at[sem],
                                            recv_sem=recv_sems.at[sem], device_id=dev, device_id_type=MESH)

    def _first(self):
        return self.FIRST if self.bcast else range(1, N_DEV)

    def _own(self, s, d, a, local_sems):
        me = 4 * lax.axis_index("x") + 2 * lax.axis_index("y") + lax.axis_index("c")
        return pltpu.make_async_copy(s if self.bcast else s.at[me], d.at[me], local_sems.at[a]), me

    def start(self, src_refs, dst_refs, send_sems, recv_sems, local_sems):
        for a, (s, d) in enumerate(zip(src_refs, dst_refs)):
            own, me = self._own(s, d, a, local_sems)
            own.start()
            for k in self._first():
                dev, idx = _peer(k)
                self._copy(s if self.bcast else s.at[idx], d.at[me], a, k, dev, send_sems, recv_sems).start()

    def relay(self, src_refs, dst_refs, send_sems, recv_sems, local_sems):
        if not self.bcast:
            return
        sibling, _ = _peer(1)
        for a, d in enumerate(dst_refs):
            for k in self.RELAYED:
                dev, idx = _peer(k)
                self._copy(d.at[idx], d.at[idx], a, k, dev, send_sems, recv_sems).wait_recv()
                self._copy(d.at[idx], d.at[idx], a, k + 1, sibling, send_sems, recv_sems).start()

    def wait(self, src_refs, dst_refs, send_sems, recv_sems, local_sems):
        sibling, _ = _peer(1)
        for a, (s, d) in enumerate(zip(src_refs, dst_refs)):
            own, me = self._own(s, d, a, local_sems)
            for k in range(1, N_DEV):
                if not (self.bcast and k in self.RELAYED):
                    dev, idx = _peer(k)
                    self._copy(d.at[idx], d.at[idx], a, k, dev, send_sems, recv_sems).wait_recv()
            for k in self._first():
                dev, idx = _peer(k)
                self._copy(s if self.bcast else s.at[idx], d.at[me], a, k, dev, send_sems, recv_sems).wait_send()
            if self.bcast:
                for k in self.RELAYED:
                    dev, idx = _peer(k)
                    self._copy(d.at[idx], d.at[idx], a, k + 1, sibling, send_sems, recv_sems).wait_send()
            own.wait()


def rider_call(core, name, grid, in_specs, out_specs, out_shape, scratch_shapes, args, rider=None):
    rider = rider or Rider()
    n_in, n_out, n_scr, nr = len(in_specs), len(out_specs), len(scratch_shapes), rider.n

    def body(*refs):
        ins, r_src = refs[:n_in], refs[n_in:n_in + nr]
        outs = refs[n_in + nr:n_in + nr + n_out]
        r_dst = refs[n_in + nr + n_out:n_in + 2 * nr + n_out]
        scr = refs[n_in + 2 * nr + n_out:n_in + 2 * nr + n_out + n_scr]
        sems = refs[n_in + 2 * nr + n_out + n_scr:]
        if nr:
            first, relay, last = True, True, True
            for ax, size in enumerate(grid):
                first = first & (pl.program_id(ax) == 0)
                relay = relay & (pl.program_id(ax) == (3 * size // 4 if ax == 0 else 0))
                last = last & (pl.program_id(ax) == size - 1)
            if not grid:
                rider.start(r_src, r_dst, *sems)
                rider.relay(r_src, r_dst, *sems)
            else:
                pl.when(first)(lambda: rider.start(r_src, r_dst, *sems))
                if rider.bcast:
                    pl.when(relay)(lambda: rider.relay(r_src, r_dst, *sems))
        core(*ins, *outs, *scr)
        if nr:
            if not grid:
                rider.wait(r_src, r_dst, *sems)
            else:
                pl.when(last)(lambda: rider.wait(r_src, r_dst, *sems))

    res = pl.pallas_call(
        body, name=name, grid=grid,
        in_specs=list(in_specs) + rider.specs(),
        out_specs=list(out_specs) + rider.specs(),
        out_shape=list(out_shape) + rider.out_shapes(),
        scratch_shapes=list(scratch_shapes) + rider.scratch(),
        compiler_params=_params(len(grid)),
    )(*args, *rider.srcs)
    return res[:n_out], res[n_out:]


def exchange_hbm(srcs, bcast, name):
    return rider_call(lambda: None, name, (), [], [], [], [], [], Rider(srcs, bcast))[1]


def allsum_small(vec, name):
    def body(src_ref, out_ref, dst_ref, send_sems, recv_sems, local_sem):
        _exchange_body(src_ref, dst_ref, send_sems, recv_sems, local_sem, True)
        acc = dst_ref[0]
        for j in range(1, N_DEV):
            acc = acc + dst_ref[j]
        out_ref[...] = acc

    return pl.pallas_call(
        body, name=name,
        out_shape=jax.ShapeDtypeStruct(vec.shape[1:], F32),
        in_specs=[pl.BlockSpec(memory_space=pltpu.VMEM)],
        out_specs=pl.BlockSpec(memory_space=pltpu.VMEM),
        scratch_shapes=[pltpu.VMEM((N_DEV,) + vec.shape[1:], F32),
                        pltpu.SemaphoreType.DMA((N_DEV - 1,)), pltpu.SemaphoreType.DMA((N_DEV - 1,)),
                        pltpu.SemaphoreType.DMA],
    )(vec)


def gather_small(vec, name):
    def body(src_ref, dst_ref, send_sems, recv_sems, local_sem):
        _exchange_body(src_ref, dst_ref, send_sems, recv_sems, local_sem, True)

    return pl.pallas_call(
        body, name=name,
        out_shape=jax.ShapeDtypeStruct((N_DEV,) + vec.shape[1:], F32),
        in_specs=[pl.BlockSpec(memory_space=pltpu.VMEM)],
        out_specs=pl.BlockSpec(memory_space=pltpu.VMEM),
        scratch_shapes=[pltpu.SemaphoreType.DMA((N_DEV - 1,)), pltpu.SemaphoreType.DMA((N_DEV - 1,)),
                        pltpu.SemaphoreType.DMA],
    )(vec)


FFN_FWD_ROWS = 1056
FFN_BWD_ROWS = 704
DW_ROWS = 1408

def ffn_fwd(h, g, w_in8, w_out4, name, rider=None):
    m, d = h.shape
    fb = w_in8.shape[2]
    tm = _tile(m, FFN_FWD_ROWS)

    def body(h_ref, g_ref, wg_ref, wu_ref, wo_ref, hn_ref, n_ref, gate_ref, up_ref, acc_ref):
        i = pl.program_id(1)

        @pl.when(i == 0)
        def _():
            xh, _ = _rms(h_ref[...])
            n_ref[...] = (xh * g_ref[...]).astype(BF16)
            acc_ref[...] = jnp.zeros_like(acc_ref)

        n = n_ref[...]
        gate = _dot(n, wg_ref[0])
        up = _dot(n, wu_ref[0])
        gate_ref[0] = gate
        up_ref[0] = up
        a = (gate * jax.nn.sigmoid(gate) * up).astype(BF16)
        acc_ref[...] += _dot(a, wo_ref[0])

        @pl.when(i == 3)
        def _():
            hn_ref[...] = h_ref[...] + 0.5 * acc_ref[...]

    return rider_call(
        body, name, (m // tm, 4),
        in_specs=[pl.BlockSpec((tm, d), lambda r, i: (r, 0)),
                  pl.BlockSpec((1, d), lambda r, i: (0, 0)),
                  pl.BlockSpec((1, d, fb), lambda r, i: (i, 0, 0)),
                  pl.BlockSpec((1, d, fb), lambda r, i: (i + 4, 0, 0)),
                  pl.BlockSpec((1, fb, d), lambda r, i: (i, 0, 0))],
        out_specs=[pl.BlockSpec((tm, d), lambda r, i: (r, 0)),
                   pl.BlockSpec((tm, d), lambda r, i: (r, 0)),
                   pl.BlockSpec((1, tm, fb), lambda r, i: (i, r, 0)),
                   pl.BlockSpec((1, tm, fb), lambda r, i: (i, r, 0))],
        out_shape=[jax.ShapeDtypeStruct((m, d), F32), jax.ShapeDtypeStruct((m, d), BF16),
                   jax.ShapeDtypeStruct((4, m, fb), F32), jax.ShapeDtypeStruct((4, m, fb), F32)],
        scratch_shapes=[pltpu.VMEM((tm, d), F32)],
        args=(h, g, w_in8, w_in8, w_out4), rider=rider)


def ffn_bwd(dh, h, g, gate, up, w_in8, w_out4, name, rider=None):
    m, d = h.shape
    fb = w_in8.shape[2]
    tm = _tile(m, FFN_BWD_ROWS)

    def body(dh_ref, h_ref, g_ref, gate_ref, up_ref, wg_ref, wu_ref, wo_ref,
             dhin_ref, a_ref, dg_ref, du_ref, dgn_ref, dhs_ref, acc_ref):
        r = pl.program_id(0)
        i = pl.program_id(1)

        @pl.when(i == 0)
        def _():
            dhs_ref[...] = (0.5 * dh_ref[...]).astype(BF16)
            acc_ref[...] = jnp.zeros_like(acc_ref)

        @pl.when((r == 0) & (i == 0))
        def _():
            dgn_ref[...] = jnp.zeros_like(dgn_ref)

        da = _dot_nt(dhs_ref[...], wo_ref[0])
        gt = gate_ref[0]
        u = up_ref[0]
        sg = jax.nn.sigmoid(gt)
        sl = gt * sg
        a_ref[0] = (sl * u).astype(BF16)
        dub = (da * sl).astype(BF16)
        dgb = (da * u * (sg * (1.0 + gt * (1.0 - sg)))).astype(BF16)
        dg_ref[0] = dgb
        du_ref[0] = dub
        acc_ref[...] += _dot_nt(dgb, wg_ref[0]) + _dot_nt(dub, wu_ref[0])

        @pl.when(i == 3)
        def _():
            dx, dgain = _rms_bwd(h_ref[...], g_ref[...], acc_ref[...])
            dgn_ref[...] += dgain
            dhin_ref[...] = dh_ref[...] + dx

    row = lambda r, i: (r, 0)
    blk = lambda r, i: (i, r, 0)
    return rider_call(
        body, name, (m // tm, 4),
        in_specs=[pl.BlockSpec((tm, d), row), pl.BlockSpec((tm, d), row),
                  pl.BlockSpec((1, d), lambda r, i: (0, 0)),
                  pl.BlockSpec((1, tm, fb), blk), pl.BlockSpec((1, tm, fb), blk),
                  pl.BlockSpec((1, d, fb), lambda r, i: (i, 0, 0)),
                  pl.BlockSpec((1, d, fb), lambda r, i: (i + 4, 0, 0)),
                  pl.BlockSpec((1, fb, d), lambda r, i: (i, 0, 0))],
        out_specs=[pl.BlockSpec((tm, d), row),
                   pl.BlockSpec((1, tm, fb), blk), pl.BlockSpec((1, tm, fb), blk),
                   pl.BlockSpec((1, tm, fb), blk),
                   pl.BlockSpec((1, d), lambda r, i: (0, 0)),
                   pl.BlockSpec((tm, d), row)],
        out_shape=[jax.ShapeDtypeStruct((m, d), F32),
                   jax.ShapeDtypeStruct((4, m, fb), BF16), jax.ShapeDtypeStruct((4, m, fb), BF16),
                   jax.ShapeDtypeStruct((4, m, fb), BF16),
                   jax.ShapeDtypeStruct((1, d), F32), jax.ShapeDtypeStruct((m, d), BF16)],
        scratch_shapes=[pltpu.VMEM((tm, d), F32)],
        args=(dh, h, g, gate, up, w_in8, w_in8, w_out4), rider=rider)


def matmul_tn(x, y, name, tn=None, y2=None, rider=None):
    bx, m, k = x.shape
    by, _, n = y.shape
    b = max(bx, by) * (2 if y2 is not None else 1)
    tm = _tile(m, DW_ROWS)
    tn = n if tn is None else tn
    nt = n // tn
    nr = m // tm

    def body(*refs):
        x_ref, y_ref = refs[0], refs[1]
        o_ref, acc_ref = refs[-2], refs[-1]
        r = pl.program_id(2)

        @pl.when(r == 0)
        def _():
            acc_ref[...] = jnp.zeros_like(acc_ref)

        if y2 is None:
            acc_ref[...] += _dot_tn(x_ref[0].astype(BF16), y_ref[0].astype(BF16))
        else:
            @pl.when(pl.program_id(0) < by)
            def _():
                acc_ref[...] += _dot_tn(x_ref[0].astype(BF16), y_ref[0].astype(BF16))

            @pl.when(pl.program_id(0) >= by)
            def _():
                acc_ref[...] += _dot_tn(x_ref[0].astype(BF16), refs[2][0].astype(BF16))

        @pl.when(r == nr - 1)
        def _():
            o_ref[0] = acc_ref[...].astype(BF16)

    x_map = (lambda i, j, r: (i, r, 0)) if bx > 1 else (lambda i, j, r: (0, r, 0))
    if y2 is None:
        y_specs = [pl.BlockSpec((1, tm, tn), (lambda i, j, r: (i, r, j)) if by > 1 else (lambda i, j, r: (0, r, j)))]
    else:
        y_specs = [pl.BlockSpec((1, tm, tn), lambda i, j, r: (jnp.minimum(i, by - 1), jnp.where(i < by, r, nr - 1), j)),
                   pl.BlockSpec((1, tm, tn), lambda i, j, r: (jnp.maximum(i - by, 0), jnp.where(i < by, 0, r), j))]
    (out,), carried = rider_call(
        body, name, (b, nt, nr),
        in_specs=[pl.BlockSpec((1, tm, k), x_map)] + y_specs,
        out_specs=[pl.BlockSpec((1, k, tn), lambda i, j, r: (i, 0, j))],
        out_shape=[jax.ShapeDtypeStruct((b, k, n), BF16)],
        scratch_shapes=[pltpu.VMEM((k, tn), F32)],
        args=[x, y] + ([y2] if y2 is not None else []), rider=rider)
    return (out, carried) if rider is not None else out


AUG = HEAD_DIM


class _Cols:
    def __init__(self, d):
        self.d = d
        self.ga, self.gb = 0, d
        self.qa, self.ka, self.va = 2 * d, 2 * d + 1024, 2 * d + 2048
        self.qb = 2 * d + 3072
        self.kb, self.vb, self.fa = self.qb + 512, self.qb + 640, self.qb + 768
        self.np = self.qb + 1024


def mixer_proj(h, g, wp, name):
    m, d = h.shape
    npad = wp.shape[1]
    tm = _row_tile(m)

    def body(h_ref, g_ref, w_ref, n_ref, p_ref):
        xh, _ = _rms(h_ref[...])
        n = (xh * g_ref[...]).astype(BF16)
        n_ref[...] = n
        p_ref[...] = _dot(n, w_ref[...])

    return pl.pallas_call(
        body, name=name, grid=(m // tm,),
        in_specs=[pl.BlockSpec((tm, d), lambda r: (r, 0)), pl.BlockSpec((1, d), lambda r: (0, 0)),
                  pl.BlockSpec((d, npad), lambda r: (0, 0))],
        out_specs=[pl.BlockSpec((tm, d), lambda r: (r, 0)), pl.BlockSpec((tm, npad), lambda r: (r, 0))],
        out_shape=[jax.ShapeDtypeStruct((m, d), BF16), jax.ShapeDtypeStruct((m, npad), F32)],
        compiler_params=_params(1),
    )(h, g, wp)


def _head_norm(x, gain, ones):
    outs = []
    for b in range(x.shape[1] // LANES):
        xb = x[:, b * LANES:(b + 1) * LANES]
        r = lax.rsqrt(_group_mean(xb * xb, ones) + EPS)
        outs.append(xb * r * gain[:, b * LANES:(b + 1) * LANES])
    return outs


def _head_norm_bwd(x, gain, dn, ones):
    dxs, dgs = [], []
    for b in range(x.shape[1] // LANES):
        sl = slice(b * LANES, (b + 1) * LANES)
        xb, dnb = x[:, sl], dn[:, sl]
        r = lax.rsqrt(_group_mean(xb * xb, ones) + EPS)
        xh = xb * r
        dxh = dnb * gain[:, sl]
        dxs.append(r * (dxh - xh * _group_mean(dxh * xh, ones)))
        dgs.append(jnp.sum(dnb * xh, axis=0, keepdims=True))
    return dxs, dgs


def _lane_col(v, lane_iota, idx):
    return jnp.sum(jnp.where(lane_iota == idx, v, 0.0), axis=1, keepdims=True)


def _aug(base, lane, vals):
    for i, v in enumerate(vals):
        base = jnp.where(lane == AUG + i, v, base)
    return base


def qk_post(proj, gains, fbias, cols, name):
    m = proj.shape[0]
    tm = _row_tile(m)
    gqa, gka, gqb, gkb = gains

    def body(qa_ref, ka_ref, va_ref, qb_ref, kb_ref, vb_ref, fa_ref, gqa_ref, gka_ref, gqb_ref, gkb_ref, fb_ref,
             qf_o, kf_o, vf_o, qb_o, kb_o, vb_o, carry_ref):
        r0 = pl.program_id(0)

        @pl.when(r0 == 0)
        def _():
            carry_ref[...] = jnp.zeros_like(carry_ref)

        z = fa_ref[...] + fb_ref[...]
        logf = jnp.minimum(z, 0.0) - jnp.log(1.0 + jnp.exp(-jnp.abs(z)))
        rr = lax.broadcasted_iota(jnp.int32, (tm, tm), 0)
        cc = lax.broadcasted_iota(jnp.int32, (tm, tm), 1)
        tril = jnp.where(cc <= rr, 1.0, 0.0).astype(BF16)
        p0, p1, p2 = _split3(logf)
        c = _dot(tril, p0) + _dot(tril, p1) + _dot(tril, p2) + carry_ref[...]
        carry_ref[...] += jnp.sum(logf, axis=0, keepdims=True)

        lane = lax.broadcasted_iota(jnp.int32, (tm, LANES), 1)
        is_pad = (r0 * tm + lax.broadcasted_iota(jnp.int32, (tm, 1), 0)) < PAD_FRONT
        ones = jnp.ones((LANES, LANES), BF16)
        for hd in range(8):
            sl = slice(hd * LANES, (hd + 1) * LANES)
            ch = _lane_col(c, lane, hd)
            ct = [p.astype(F32) for p in _split3(ch)]
            cs = [p.astype(F32) for p in _split3(-jnp.where(is_pad, -NEG, ch))]
            xq = qa_ref[:, sl]
            qn = xq * lax.rsqrt(_group_mean(xq * xq, ones) + EPS) * gqa_ref[...]
            qf_o[:, sl] = _aug(qn, lane, ct + [1.0, 1.0, 1.0]).astype(BF16)
            xk = ka_ref[:, sl]
            kn = xk * lax.rsqrt(_group_mean(xk * xk, ones) + EPS) * gka_ref[...]
            kf_o[:, sl] = _aug(kn, lane, [1.0, 1.0, 1.0] + cs).astype(BF16)
            vf_o[:, sl] = _aug(va_ref[:, sl], lane, [1.0, 1.0, 1.0]).astype(BF16)

        gones = _group_ones()
        for src, gn, dst in ((qb_ref, gqb_ref, qb_o), (kb_ref, gkb_ref, kb_o)):
            for b, blk in enumerate(_head_norm(src[...], gn[...], gones)):
                dst[:, b * LANES:(b + 1) * LANES] = blk.astype(BF16)
        vb_o[...] = vb_ref[...].astype(BF16)

    w1024 = lambda off: pl.BlockSpec((tm, 1024), lambda r, o=off // 1024: (r, o))
    w512 = lambda off: pl.BlockSpec((tm, 512), lambda r, o=off // 512: (r, o))
    w128 = lambda off: pl.BlockSpec((tm, LANES), lambda r, o=off // LANES: (r, o))
    vec = lambda w: pl.BlockSpec((1, w), lambda r: (0, 0))
    row = lambda w: pl.BlockSpec((tm, w), lambda r: (r, 0))
    return pl.pallas_call(
        body, name=name, grid=(m // tm,),
        in_specs=[w1024(cols.qa), w1024(cols.ka), w1024(cols.va), w512(cols.qb), w128(cols.kb), w128(cols.vb),
                  w128(cols.fa), vec(LANES), vec(LANES), vec(512), vec(LANES), vec(LANES)],
        out_specs=[row(1024), row(1024), row(1024), row(512), row(LANES), row(LANES)],
        out_shape=[jax.ShapeDtypeStruct((m, 1024), BF16)] * 3 + [jax.ShapeDtypeStruct((m, 512), BF16)]
                  + [jax.ShapeDtypeStruct((m, LANES), BF16)] * 2,
        scratch_shapes=[pltpu.VMEM((1, LANES), F32)],
        compiler_params=_params(1),
    )(proj, proj, proj, proj, proj, proj, proj, gqa, gka, gqb, gkb, fbias)


def qk_post_bwd(proj, gains, fbias, dqf, dkf, dvf, dqb, dkb, dvb, dc, dga, dgb, cols, name):
    m = proj.shape[0]
    d = cols.d
    tm = _row_tile(m)
    nt = m // tm
    gqa, gka, gqb, gkb = gains

    def body(qa_ref, ka_ref, qb_ref, kb_ref, fa_ref, gqa_ref, gka_ref, gqb_ref, gkb_ref, fb_ref,
             dqf_ref, dkf_ref, dvf_ref, dqb_ref, dkb_ref, dvb_ref, dc_ref, dga_ref, dgb_ref,
             dp_o, ggqa_o, ggka_o, ggqb_o, ggkb_o, gfb_o, carry_ref):
        @pl.when(pl.program_id(0) == 0)
        def _():
            carry_ref[...] = jnp.zeros_like(carry_ref)
            for o in (ggqa_o, ggka_o, ggqb_o, ggkb_o, gfb_o):
                o[...] = jnp.zeros_like(o)

        dp_o[:, cols.ga:cols.ga + d] = dga_ref[...].astype(BF16)
        dp_o[:, cols.gb:cols.gb + d] = dgb_ref[...].astype(BF16)
        dp_o[:, cols.fa + LANES:cols.np] = jnp.zeros((tm, cols.np - cols.fa - LANES), BF16)
        lane = lax.broadcasted_iota(jnp.int32, (tm, LANES), 1)
        data = lane < HEAD_DIM
        ones = jnp.ones((LANES, LANES), BF16)
        for hd in range(8):
            sl = slice(hd * LANES, (hd + 1) * LANES)
            for src, gn, dn_ref, off, gout in ((qa_ref, gqa_ref, dqf_ref, cols.qa, ggqa_o),
                                               (ka_ref, gka_ref, dkf_ref, cols.ka, ggka_o)):
                x = src[:, sl]
                dn = jnp.where(data, dn_ref[:, sl], 0.0)
                r = lax.rsqrt(_group_mean(x * x, ones) + EPS)
                xh = x * r
                dxh = dn * gn[...]
                dp_o[:, off + hd * LANES:off + (hd + 1) * LANES] = (
                    r * (dxh - xh * _group_mean(dxh * xh, ones))).astype(BF16)
                gout[...] += jnp.sum(dn * xh, axis=0, keepdims=True)
            dp_o[:, cols.va + hd * LANES:cols.va + (hd + 1) * LANES] = jnp.where(data, dvf_ref[:, sl], 0.0).astype(BF16)
        dp_o[:, cols.vb:cols.vb + LANES] = dvb_ref[...].astype(BF16)
        gones = _group_ones()
        for src, gn, dn, off, gout in ((qb_ref, gqb_ref, dqb_ref, cols.qb, ggqb_o),
                                       (kb_ref, gkb_ref, dkb_ref, cols.kb, ggkb_o)):
            dxs, dgs = _head_norm_bwd(src[...], gn[...], dn[...], gones)
            for b, (dx, dg) in enumerate(zip(dxs, dgs)):
                dp_o[:, off + b * LANES:off + (b + 1) * LANES] = dx.astype(BF16)
                gout[:, b * LANES:(b + 1) * LANES] += dg
        dcv = dc_ref[...]
        rr = lax.broadcasted_iota(jnp.int32, (tm, tm), 0)
        cc = lax.broadcasted_iota(jnp.int32, (tm, tm), 1)
        triu = jnp.where(cc >= rr, 1.0, 0.0).astype(BF16)
        p0, p1, p2 = _split3(dcv)
        dlogf = _dot(triu, p0) + _dot(triu, p1) + _dot(triu, p2) + carry_ref[...]
        carry_ref[...] += jnp.sum(dcv, axis=0, keepdims=True)
        z = fa_ref[...] + fb_ref[...]
        row = (nt - 1 - pl.program_id(0)) * tm + lax.broadcasted_iota(jnp.int32, (tm, LANES), 0)
        dfa = jnp.where(row >= PAD_FRONT, dlogf * jax.nn.sigmoid(-z), 0.0)
        dp_o[:, cols.fa:cols.fa + LANES] = dfa.astype(BF16)
        gfb_o[...] += jnp.sum(dfa, axis=0, keepdims=True)

    rev = lambda r: nt - 1 - r
    w1024 = lambda off: pl.BlockSpec((tm, 1024), lambda r, o=off // 1024: (rev(r), o))
    w512 = lambda off: pl.BlockSpec((tm, 512), lambda r, o=off // 512: (rev(r), o))
    w128 = lambda off: pl.BlockSpec((tm, LANES), lambda r, o=off // LANES: (rev(r), o))
    vec = lambda w: pl.BlockSpec((1, w), lambda r: (0, 0))
    row = lambda w: pl.BlockSpec((tm, w), lambda r: (rev(r), 0))
    return pl.pallas_call(
        body, name=name, grid=(nt,),
        in_specs=[w1024(cols.qa), w1024(cols.ka), w512(cols.qb), w128(cols.kb), w128(cols.fa),
                  vec(LANES), vec(LANES), vec(512), vec(LANES), vec(LANES),
                  row(1024), row(1024), row(1024), row(512), row(LANES), row(LANES), row(LANES), row(d), row(d)],
        out_specs=[row(cols.np), vec(LANES), vec(LANES), vec(512), vec(LANES), vec(LANES)],
        out_shape=[jax.ShapeDtypeStruct((m, cols.np), BF16), jax.ShapeDtypeStruct((1, LANES), F32),
                   jax.ShapeDtypeStruct((1, LANES), F32), jax.ShapeDtypeStruct((1, 512), F32),
                   jax.ShapeDtypeStruct((1, LANES), F32), jax.ShapeDtypeStruct((1, LANES), F32)],
        scratch_shapes=[pltpu.VMEM((1, LANES), F32)],
        compiler_params=_params(1),
    )(proj, proj, proj, proj, proj, gqa, gka, gqb, gkb, fbias, dqf, dkf, dvf, dqb, dkb, dvb, dc, dga, dgb)


def dproj_bwd(dh, h, g, dproj, wp, name):
    m, d = h.shape
    npad = wp.shape[1]
    tm = _row_tile(m)

    def body(dh_ref, h_ref, g_ref, dp_ref, w_ref, dhin_ref, dgn_ref):
        @pl.when(pl.program_id(0) == 0)
        def _():
            dgn_ref[...] = jnp.zeros_like(dgn_ref)

        dn = _dot_nt(dp_ref[...], w_ref[...])
        dx, dgain = _rms_bwd(h_ref[...], g_ref[...], dn)
        dgn_ref[...] += dgain
        dhin_ref[...] = dh_ref[...] + dx

    row = lambda w: pl.BlockSpec((tm, w), lambda r: (r, 0))
    return pl.pallas_call(
        body, name=name, grid=(m // tm,),
        in_specs=[row(d), row(d), pl.BlockSpec((1, d), lambda r: (0, 0)), row(npad),
                  pl.BlockSpec((d, npad), lambda r: (0, 0))],
        out_specs=[row(d), pl.BlockSpec((1, d), lambda r: (0, 0))],
        out_shape=[jax.ShapeDtypeStruct((m, d), F32), jax.ShapeDtypeStruct((1, d), F32)],
        compiler_params=_params(1),
    )(dh, h, g, dproj, wp)


def _causal_t(t):
    return lax.broadcasted_iota(jnp.int32, (t, t), 0) <= lax.broadcasted_iota(jnp.int32, (t, t), 1)


HEADS_PER_STEP = 2


def fox_fwd(qf, kf, vt, name, rider=None):
    m = qf.shape[0]
    t = _row_tile(m)
    nq = m // t
    hp = HEADS_PER_STEP
    w = hp * LANES

    def body(q_ref, k_ref, vt_ref, o_ref, lse_ref, acc_ref, m_ref, p_ref, a_ref):
        qi = pl.program_id(1)
        acc_ref[...] = jnp.zeros_like(acc_ref)
        m_ref[...] = jnp.full_like(m_ref, NEG)

        def scores(ki, slot, mask):
            off = pl.multiple_of(ki * t, t)
            for e in range(hp):
                sl = slice(e * LANES, (e + 1) * LANES)
                s = _dot_nt(k_ref[pl.ds(off, t), sl], q_ref[:, sl])
                if mask is not None:
                    s = jnp.where(mask, s, NEG)
                m_old = m_ref[e]
                m_new = jnp.maximum(m_old, jnp.max(s, axis=0, keepdims=True))
                p_ref[slot, e] = jnp.exp(s - m_new).astype(BF16)
                a_ref[slot, e] = jnp.exp(m_old - m_new)
                m_ref[e] = m_new

        def values(ki, slot):
            off = pl.multiple_of(ki * t, t)
            for e in range(hp):
                sl = slice(e * LANES, (e + 1) * LANES)
                acc_ref[e] = acc_ref[e] * a_ref[slot, e] + _dot(vt_ref[sl, pl.ds(off, t)], p_ref[slot, e])

        causal = _causal_t(t)
        scores(0, 0, causal | (jnp.full((t, t), qi, jnp.int32) > 0))

        def step(ki, carry):
            values(ki - 1, (ki - 1) % 2)
            scores(ki, ki % 2, None)
            return carry

        lax.fori_loop(1, qi, step, 0)

        @pl.when(qi >= 1)
        def _():
            values(qi - 1, (qi - 1) % 2)
            scores(qi, qi % 2, causal)

        values(qi, qi % 2)
        row = lax.broadcasted_iota(jnp.int32, (LANES, t), 0)
        for e in range(hp):
            l = jnp.max(acc_ref[e, AUG:AUG + 8, :], axis=0, keepdims=True)
            o_ref[:, e * LANES:(e + 1) * LANES] = jnp.where(row < HEAD_DIM, acc_ref[e] * (1.0 / l), 0.0).T
            lse_ref[e] = m_ref[e] + jnp.log(l)

    return rider_call(
        body, name, (8 // hp, nq),
        in_specs=[pl.BlockSpec((t, w), lambda hd, i: (i, hd)),
                  pl.BlockSpec((m, w), lambda hd, i: (0, hd)),
                  pl.BlockSpec((w, m), lambda hd, i: (hd, 0))],
        out_specs=[pl.BlockSpec((t, w), lambda hd, i: (i, hd)),
                   pl.BlockSpec((hp, 1, t), lambda hd, i: (hd, 0, i))],
        out_shape=[jax.ShapeDtypeStruct((m, 8 * LANES), F32), jax.ShapeDtypeStruct((8, 1, m), F32)],
        scratch_shapes=[pltpu.VMEM((hp, LANES, t), F32), pltpu.VMEM((hp, 1, t), F32),
                        pltpu.VMEM((2, hp, t, t), BF16), pltpu.VMEM((2, hp, 1, t), F32)],
        args=(qf, kf, vt), rider=rider)


def fox_bwd(qf, kf, vf, kt, dof, lse, delta, name, rider=None):
    m = qf.shape[0]
    t = _row_tile(m)
    nq = m // t
    hp = HEADS_PER_STEP
    w = hp * LANES

    def body(k_ref, v_ref, kt_ref, q_ref, do_ref, lse_ref, delta_ref, dk_ref, dv_ref, dq_ref, dck_ref, dcq_ref,
             dka_ref, dva_ref, dqt_ref):
        ki = pl.program_id(1)

        @pl.when(ki == 0)
        def _():
            dqt_ref[...] = jnp.zeros_like(dqt_ref)

        dka_ref[...] = jnp.zeros_like(dka_ref)
        dva_ref[...] = jnp.zeros_like(dva_ref)

        def tile(qi, diagonal):
            off = pl.multiple_of(qi * t, t)
            for e in range(hp):
                sl = slice(e * LANES, (e + 1) * LANES)
                q = q_ref[pl.ds(off, t), sl]
                do = do_ref[pl.ds(off, t), sl]
                s = _dot_nt(k_ref[:, sl], q)
                if diagonal:
                    s = jnp.where(_causal_t(t), s, NEG)
                p = jnp.exp(s - lse_ref[e, :, pl.ds(off, t)])
                ds = (p * (_dot_nt(v_ref[:, sl], do) - delta_ref[e, :, pl.ds(off, t)])).astype(BF16)
                dva_ref[:, sl] += _dot(p.astype(BF16), do)
                dka_ref[:, sl] += _dot(ds, q)
                dqt_ref[sl, pl.ds(off, t)] += _dot(kt_ref[sl, :], ds)

        def step(qi, carry):
            tile(qi, False)
            return carry

        tile(ki, True)
        lax.fori_loop(ki + 1, nq, step, 0)
        dk_ref[...] = dka_ref[...]
        dv_ref[...] = dva_ref[...]
        row8 = lax.broadcasted_iota(jnp.int32, (8, 1), 0)
        for e in range(hp):
            slab = dka_ref[:, e * LANES:(e + 1) * LANES].T[AUG:AUG + 8, :]
            dck_ref[e] = -jnp.sum(jnp.where(row8 == 3, slab, 0.0), axis=0, keepdims=True)

        @pl.when(ki == nq - 1)
        def _():
            for e in range(hp):
                sl = slice(e * LANES, (e + 1) * LANES)
                slab = dqt_ref[e * LANES + AUG:e * LANES + AUG + 8, :]
                dcq_ref[e] = jnp.sum(jnp.where(row8 == 0, slab, 0.0), axis=0, keepdims=True)
                for j in range(nq):
                    dq_ref[j * t:(j + 1) * t, sl] = dqt_ref[sl, j * t:(j + 1) * t].T

    tile_spec = pl.BlockSpec((t, w), lambda hd, i: (i, hd))
    full = pl.BlockSpec((m, w), lambda hd, i: (0, hd))
    stat = pl.BlockSpec((hp, 1, m), lambda hd, i: (hd, 0, 0))
    (dkf, dvf, dqf, dck, dcq), carried = rider_call(
        body, name, (8 // hp, nq),
        in_specs=[tile_spec, tile_spec, pl.BlockSpec((w, t), lambda hd, i: (hd, i)), full, full, stat, stat],
        out_specs=[tile_spec, tile_spec, full, pl.BlockSpec((hp, 1, t), lambda hd, i: (hd, 0, i)), stat],
        out_shape=[jax.ShapeDtypeStruct((m, 8 * LANES), F32), jax.ShapeDtypeStruct((m, 8 * LANES), F32),
                   jax.ShapeDtypeStruct((m, 8 * LANES), F32), jax.ShapeDtypeStruct((8, 1, m), F32),
                   jax.ShapeDtypeStruct((8, 1, m), F32)],
        scratch_shapes=[pltpu.VMEM((t, w), F32), pltpu.VMEM((t, w), F32), pltpu.VMEM((w, m), F32)],
        args=(kf, vf, kt, qf, dof, lse, delta), rider=rider)
    return (dkf, dvf, dqf, dcq + dck), carried


def _bucket_ids():
    def bucket(dist):
        n = np.maximum(dist, 0)
        max_exact = N_BUCKETS // 2
        nf = np.maximum(n, 1).astype(np.float32)
        large = max_exact + (np.log(nf / max_exact) / math.log(MAX_DISTANCE / max_exact)
                             * (N_BUCKETS - max_exact)).astype(np.int32)
        return np.where(n < max_exact, n, np.minimum(large, N_BUCKETS - 1))

    tl = np.arange(LANES)[:, None]
    sl = np.arange(LANES)[None, :]
    prev = bucket(LANES + tl - sl)
    cur = bucket(tl - sl)
    meta = np.full((LANES, LANES), N_BUCKETS - 1)
    return np.concatenate([prev, cur, meta], axis=1).astype(np.int32)


def bias_build(table, name):
    ids = jnp.asarray(_bucket_ids())

    def body(t_ref, id_ref, o_ref):
        idv = id_ref[...]
        for h in range(8):
            acc = jnp.zeros((LANES, 3 * LANES), F32)
            for b in range(N_BUCKETS):
                acc = jnp.where(idv == b, t_ref[b, h], acc)
            o_ref[h] = acc

    return pl.pallas_call(
        body, name=name,
        in_specs=[pl.BlockSpec(memory_space=pltpu.SMEM), pl.BlockSpec(memory_space=pltpu.VMEM)],
        out_specs=pl.BlockSpec(memory_space=pltpu.VMEM),
        out_shape=jax.ShapeDtypeStruct((8, LANES, 3 * LANES), F32),
    )(table, ids)


def bias_reduce(dbias, name):
    ids = jnp.asarray(_bucket_ids())

    def body(d_ref, id_ref, o_ref):
        idv = id_ref[...]
        rr = lax.broadcasted_iota(jnp.int32, (N_BUCKETS, LANES), 0)
        cc = lax.broadcasted_iota(jnp.int32, (N_BUCKETS, LANES), 1)
        acc = jnp.zeros((N_BUCKETS, LANES), F32)
        for h in range(8):
            dv = d_ref[h]
            for b in range(N_BUCKETS):
                val = jnp.sum(jnp.where(idv == b, dv, 0.0), keepdims=True)
                acc = jnp.where((rr == b) & (cc == h), val, acc)
        o_ref[...] = acc

    return pl.pallas_call(
        body, name=name,
        in_specs=[pl.BlockSpec(memory_space=pltpu.VMEM), pl.BlockSpec(memory_space=pltpu.VMEM)],
        out_specs=pl.BlockSpec(memory_space=pltpu.VMEM),
        out_shape=jax.ShapeDtypeStruct((N_BUCKETS, LANES), F32),
    )(dbias, ids)


def _swa_valid(n):
    shape = (LANES, 3 * LANES)
    tl = lax.broadcasted_iota(jnp.int32, shape, 0)
    col = lax.broadcasted_iota(jnp.int32, shape, 1)
    sl = col & (LANES - 1)
    nv = jnp.full(shape, n, jnp.int32)
    is_meta = sl >= PAD_FRONT
    prev = (col < LANES) & (sl > tl) & (nv >= 1) & ((nv >= 2) | is_meta)
    cur = (col >= LANES) & (col < 2 * LANES) & (sl <= tl) & ((nv >= 1) | is_meta)
    meta = (col >= 2 * LANES) & is_meta & ((nv >= 2) | ((nv == 1) & (sl <= tl)))
    return prev | cur | meta


def _swa_keys(ref, n):
    off_prev = pl.multiple_of(jnp.maximum(n - 1, 0) * LANES, LANES)
    off_cur = pl.multiple_of(n * LANES, LANES)
    return jnp.concatenate([ref[pl.ds(off_prev, LANES), :], ref[pl.ds(off_cur, LANES), :], ref[0:LANES, :]], axis=0)


def swa_fwd(q, k, v, bias, sinks, name):
    m = q.shape[0]

    def body(q_ref, k_ref, v_ref, bias_ref, sink_ref, o_ref, lse_ref):
        n = pl.program_id(0)
        lane1 = lax.broadcasted_iota(jnp.int32, (1, LANES), 1)
        lane_t = lax.broadcasted_iota(jnp.int32, (LANES, LANES), 1)
        in_head = [lane1 < HEAD_DIM, lane1 >= HEAD_DIM]
        kall = _swa_keys(k_ref, n)
        vall = _swa_keys(v_ref, n)
        vs = [jnp.where(in_head[g], vall, jnp.zeros_like(vall)) for g in (0, 1)]
        valid = _swa_valid(n)
        lse = jnp.zeros((LANES, LANES), F32)
        for b in range(4):
            qb = q_ref[:, b * LANES:(b + 1) * LANES]
            ob = jnp.zeros((LANES, LANES), F32)
            for g in (0, 1):
                h = 4 * g + b
                qe = jnp.where(in_head[g], qb, jnp.zeros_like(qb))
                s = jnp.where(valid, _dot_nt(qe, kall) + bias_ref[h], NEG)
                sink = sink_ref[h]
                mx = jnp.maximum(jnp.max(s, axis=1, keepdims=True), sink)
                p = jnp.exp(s - mx)
                den = jnp.sum(p, axis=1, keepdims=True) + jnp.exp(sink - mx)
                ob = ob + _dot((p / den).astype(BF16), vs[g])
                lse = jnp.where(lane_t == h, mx + jnp.log(den), lse)
            o_ref[:, b * LANES:(b + 1) * LANES] = ob
        lse_ref[...] = lse

    return pl.pallas_call(
        body, name=name, grid=(m // LANES,),
        in_specs=[pl.BlockSpec((LANES, 512), lambda n: (n, 0)),
                  pl.BlockSpec((m, LANES), lambda n: (0, 0)), pl.BlockSpec((m, LANES), lambda n: (0, 0)),
                  pl.BlockSpec((8, LANES, 3 * LANES), lambda n: (0, 0, 0)),
                  pl.BlockSpec(memory_space=pltpu.SMEM)],
        out_specs=[pl.BlockSpec((LANES, 512), lambda n: (n, 0)), pl.BlockSpec((LANES, LANES), lambda n: (n, 0))],
        out_shape=[jax.ShapeDtypeStruct((m, 512), F32), jax.ShapeDtypeStruct((m, LANES), F32)],
        compiler_params=_params(1),
    )(q, k, v, bias, sinks)


def swa_bwd(q, k, v, bias, sinks, o, lse, do, name):
    m = q.shape[0]

    def body(q_ref, do_ref, o_ref, lse_ref, k_ref, v_ref, bias_ref, sink_ref,
             dq_ref, dk_ref, dv_ref, dbias_ref, dsink_ref):
        n = pl.program_id(0)

        @pl.when(n == 0)
        def _():
            for r in (dk_ref, dv_ref, dbias_ref, dsink_ref):
                r[...] = jnp.zeros_like(r)

        lane1 = lax.broadcasted_iota(jnp.int32, (1, LANES), 1)
        lane_t = lax.broadcasted_iota(jnp.int32, (LANES, LANES), 1)
        in_head = [lane1 < HEAD_DIM, lane1 >= HEAD_DIM]
        off_prev = pl.multiple_of(jnp.maximum(n - 1, 0) * LANES, LANES)
        off_cur = pl.multiple_of(n * LANES, LANES)
        kall = _swa_keys(k_ref, n)
        vall = _swa_keys(v_ref, n)
        ks = [jnp.where(in_head[g], kall, jnp.zeros_like(kall)) for g in (0, 1)]
        valid = _swa_valid(n)
        lsev = lse_ref[...]
        dsink = dsink_ref[...]
        dkall = jnp.zeros((3 * LANES, LANES), F32)
        dvall = jnp.zeros((3 * LANES, LANES), F32)
        for b in range(4):
            sl = slice(b * LANES, (b + 1) * LANES)
            qb = q_ref[:, sl]
            dob = do_ref[:, sl]
            prod = dob * o_ref[:, sl]
            dqb = jnp.zeros((LANES, LANES), F32)
            for g in (0, 1):
                h = 4 * g + b
                qe = jnp.where(in_head[g], qb, jnp.zeros_like(qb))
                doe = jnp.where(in_head[g], dob, 0.0).astype(BF16)
                delta = jnp.sum(jnp.where(in_head[g], prod, 0.0), axis=1, keepdims=True)
                lse_h = _lane_col(lsev, lane_t, h)
                s = jnp.where(valid, _dot_nt(qe, kall) + bias_ref[h], NEG)
                p = jnp.exp(s - lse_h)
                ds = p * (_dot_nt(doe, vall) - delta)
                dbias_ref[h] += ds
                sink_part = jnp.sum(-jnp.exp(sink_ref[h] - lse_h) * delta, keepdims=True)
                dsink = jnp.where(lane1 == h, dsink + sink_part, dsink)
                dsb = ds.astype(BF16)
                dqb = dqb + _dot(dsb, ks[g])
                dkall = dkall + _dot_tn(dsb, qe)
                dvall = dvall + _dot_tn(p.astype(BF16), doe)
            dq_ref[:, sl] = dqb
        dsink_ref[...] = dsink
        for ref, val in ((dk_ref, dkall), (dv_ref, dvall)):
            ref[pl.ds(off_prev, LANES), :] += val[0:LANES]
            ref[pl.ds(off_cur, LANES), :] += val[LANES:2 * LANES]
            ref[0:LANES, :] += val[2 * LANES:3 * LANES]

    blk = pl.BlockSpec((LANES, 512), lambda n: (n, 0))
    full = pl.BlockSpec((m, LANES), lambda n: (0, 0))
    return pl.pallas_call(
        body, name=name, grid=(m // LANES,),
        in_specs=[blk, blk, blk, pl.BlockSpec((LANES, LANES), lambda n: (n, 0)), full, full,
                  pl.BlockSpec((8, LANES, 3 * LANES), lambda n: (0, 0, 0)),
                  pl.BlockSpec(memory_space=pltpu.SMEM)],
        out_specs=[blk, full, full, pl.BlockSpec((8, LANES, 3 * LANES), lambda n: (0, 0, 0)),
                   pl.BlockSpec((1, LANES), lambda n: (0, 0))],
        out_shape=[jax.ShapeDtypeStruct((m, 512), F32), jax.ShapeDtypeStruct((m, LANES), F32),
                   jax.ShapeDtypeStruct((m, LANES), F32), jax.ShapeDtypeStruct((8, LANES, 3 * LANES), F32),
                   jax.ShapeDtypeStruct((1, LANES), F32)],
        compiler_params=_params(1),
    )(q, do, o, lse, k, v, bias, sinks)


def branch_out(h, o_fox, o_swa, proj, wbf, wbs, wo, cols, name):
    m, d = h.shape
    tm = _row_tile(m)

    def body(h_ref, of_ref, os_ref, ga_ref, gb_ref, wbf_ref, wbs_ref, wo_ref, hn_ref):
        tf = _dot(of_ref[...].astype(BF16), wbf_ref[...])
        ts = _dot(os_ref[...].astype(BF16), wbs_ref[...])
        y = jax.nn.sigmoid(ga_ref[...]) * tf + jax.nn.sigmoid(gb_ref[...]) * ts
        hn_ref[...] = h_ref[...] + _dot(y.astype(BF16), wo_ref[...])

    row = lambda w, o=0: pl.BlockSpec((tm, w), lambda r, o=o: (r, o))
    res = lambda a: pl.BlockSpec(a.shape, lambda r: (0, 0))
    return pl.pallas_call(
        body, name=name, grid=(m // tm,),
        in_specs=[row(d), row(1024), row(512), row(d, cols.ga // d), row(d, cols.gb // d), res(wbf), res(wbs), res(wo)],
        out_specs=row(d),
        out_shape=jax.ShapeDtypeStruct((m, d), F32),
        compiler_params=_params(1),
    )(h, o_fox, o_swa, proj, proj, wbf, wbs, wo)


def branch_out_bwd(dh, o_fox, o_swa, proj, wbf, wbs, wo, cols, name):
    m, d = dh.shape
    tm = _row_tile(m)

    def body(dh_ref, of_ref, os_ref, ga_ref, gb_ref, wbf_ref, wbs_ref, wo_ref,
             y_ref, dtf_ref, dts_ref, dga_ref, dgb_ref, dof_ref, dos_ref, delta_ref):
        dy = _dot_nt(dh_ref[...].astype(BF16), wo_ref[...])
        tf = _dot(of_ref[...].astype(BF16), wbf_ref[...])
        ts = _dot(os_ref[...].astype(BF16), wbs_ref[...])
        sa = jax.nn.sigmoid(ga_ref[...])
        sb = jax.nn.sigmoid(gb_ref[...])
        y_ref[...] = (sa * tf + sb * ts).astype(BF16)
        dtf = (dy * sa).astype(BF16)
        dts = (dy * sb).astype(BF16)
        dtf_ref[...] = dtf
        dts_ref[...] = dts
        dga_ref[...] = (dy * tf * sa * (1.0 - sa)).astype(BF16)
        dgb_ref[...] = (dy * ts * sb * (1.0 - sb)).astype(BF16)
        dof = _dot_nt(dtf, wbf_ref[...])
        dof_ref[...] = dof.astype(BF16)
        dos_ref[...] = _dot_nt(dts, wbs_ref[...])
        lane = lax.broadcasted_iota(jnp.int32, (tm, LANES), 1)
        delta = jnp.zeros((tm, LANES), F32)
        for hd in range(8):
            sl = slice(hd * LANES, (hd + 1) * LANES)
            delta = jnp.where(lane == hd, jnp.sum(dof[:, sl] * of_ref[:, sl], axis=1, keepdims=True), delta)
        delta_ref[...] = delta

    row = lambda w, o=0: pl.BlockSpec((tm, w), lambda r, o=o: (r, o))
    res = lambda a: pl.BlockSpec(a.shape, lambda r: (0, 0))
    return pl.pallas_call(
        body, name=name, grid=(m // tm,),
        in_specs=[row(d), row(1024), row(512), row(d, cols.ga // d), row(d, cols.gb // d), res(wbf), res(wbs), res(wo)],
        out_specs=[row(d)] * 5 + [row(1024), row(512), row(LANES)],
        out_shape=[jax.ShapeDtypeStruct((m, d), BF16)] * 5 + [jax.ShapeDtypeStruct((m, 1024), BF16),
                   jax.ShapeDtypeStruct((m, 512), F32), jax.ShapeDtypeStruct((m, LANES), F32)],
        compiler_params=_params(1),
    )(dh, o_fox, o_swa, proj, proj, wbf, wbs, wo)


def loss_head(h, target, name):
    m, d = h.shape

    def body(h_ref, t_ref, dh_ref, loss_ref):
        n = pl.program_id(0)

        @pl.when(n == 0)
        def _():
            loss_ref[...] = jnp.zeros_like(loss_ref)
            dh_ref[...] = jnp.zeros_like(dh_ref)

        @pl.when(n > 0)
        def _():
            err = h_ref[...] - t_ref[...]
            dh_ref[...] = err * (1.0 / d)
            loss_ref[...] += jnp.sum(err * err, keepdims=True) * (0.5 / d)

    return pl.pallas_call(
        body, name=name, grid=(m // LANES,),
        in_specs=[pl.BlockSpec((LANES, d), lambda n: (n, 0)),
                  pl.BlockSpec((LANES, d), lambda n: (jnp.maximum(n - 1, 0), 0))],
        out_specs=[pl.BlockSpec((LANES, d), lambda n: (n, 0)), pl.BlockSpec((8, LANES), lambda n: (0, 0))],
        out_shape=[jax.ShapeDtypeStruct((m, d), F32), jax.ShapeDtypeStruct((8, LANES), F32)],
        compiler_params=_params(1),
    )(h, target)


def _adamw_math(w, g, m, v):
    m = ADAM_B1 * m + (1.0 - ADAM_B1) * g
    v = ADAM_B2 * v + (1.0 - ADAM_B2) * (g * g)
    m_hat = m / (1.0 - ADAM_B1 ** ADAM_STEP)
    v_hat = v / (1.0 - ADAM_B2 ** ADAM_STEP)
    delta = -ADAM_LR * (m_hat / (jnp.sqrt(v_hat) + ADAM_EPS) + ADAM_WD * w)
    return delta, m, v


def adamw_sum(parts, w, m, v, name):
    n_layers, a, b = w.shape
    ta = next(t for t in (256, 176, 128, a) if a % t == 0)
    nr = a // ta

    def body(*refs):
        p_refs = refs[:n_layers]
        w_ref, m_ref, v_ref, g_o, d_o, m_o, v_o = refs[n_layers:]
        for l in range(n_layers):
            @pl.when(pl.program_id(0) == l)
            def _(l=l):
                g = p_refs[l][0].astype(F32)
                for j in range(1, N_DEV):
                    g = g + p_refs[l][j].astype(F32)
                g_o[0] = g
                d_o[0], m_o[0], v_o[0] = _adamw_math(w_ref[0], g, m_ref[0], v_ref[0])

    def part_spec(l):
        return pl.BlockSpec((N_DEV, ta, b), lambda i, r, l=l: (0, jnp.where(i == l, r, jnp.where(i < l, 0, nr - 1)), 0))

    row = pl.BlockSpec((1, ta, b), lambda i, r: (i, r, 0))
    return pl.pallas_call(
        body, name=name, grid=(n_layers, nr),
        in_specs=[part_spec(l) for l in range(n_layers)] + [row, row, row],
        out_specs=[row] * 4,
        out_shape=[jax.ShapeDtypeStruct(w.shape, F32)] * 4,
        compiler_params=_params(2),
    )(*parts, w, m, v)


def adamw_small(g, w, m, v, name):
    def body(g_ref, w_ref, m_ref, v_ref, d_o, m_o, v_o):
        d_o[...], m_o[...], v_o[...] = _adamw_math(w_ref[...], g_ref[...], m_ref[...], v_ref[...])

    spec = pl.BlockSpec(memory_space=pltpu.VMEM)
    return pl.pallas_call(
        body, name=name, in_specs=[spec] * 4, out_specs=[spec] * 3,
        out_shape=[jax.ShapeDtypeStruct(w.shape, F32)] * 3,
    )(g, w, m, v)


BIG = ("ffn1_w_in", "ffn1_w_out", "w_in", "w_branch_fox", "w_branch_swa", "w_out", "ffn2_w_in", "ffn2_w_out")
SMALL = ("rel_bias_table", "ffn1_norm", "mix_norm", "forget_bias", "fox_q_norm", "fox_k_norm",
         "swa_q_norm", "swa_k_norm", "swa_sinks", "ffn2_norm")
WEIGHTS = ("meta_tokens", "rel_bias_table", "ffn1_norm", "ffn1_w_in", "ffn1_w_out", "mix_norm", "w_in",
           "forget_bias", "fox_q_norm", "fox_k_norm", "swa_q_norm", "swa_k_norm", "swa_sinks", "w_branch_fox",
           "w_branch_swa", "w_out", "ffn2_norm", "ffn2_w_in", "ffn2_w_out")


def _pack(arrs, width, row_multiple, dtype):
    lead = arrs[0].shape[:-1]
    flat = jnp.concatenate([a.astype(dtype) for a in arrs], axis=-1)
    n = flat.shape[-1]
    rows = -(-n // width)
    rows = -(-rows // row_multiple) * row_multiple
    flat = jnp.pad(flat, [(0, 0)] * len(lead) + [(0, rows * width - n)])
    return flat.reshape(lead + (rows, width))


def _unpack(flat, shapes):
    flat = flat.reshape(-1)
    out, off = [], 0
    for s in shapes:
        n = int(np.prod(s))
        out.append(flat[off:off + n].reshape(s))
        off += n
    return out


def _swa_head_order():
    return [4 * (j % 2) + j // 2 for j in range(8)]


def _permute_heads(a, axis, inverse=False):
    order = _swa_head_order()
    if inverse:
        order = [order.index(hd) for hd in range(8)]
    parts = [lax.slice_in_dim(a, hd * HEAD_DIM, (hd + 1) * HEAD_DIM, axis=axis) for hd in order]
    return jnp.concatenate(parts, axis=axis)


def _w_in_segments(cols):
    d = cols.d
    segs = [(512 * i + HEAD_DIM * hd, HEAD_DIM, new + LANES * hd)
            for i, new in enumerate((cols.qa, cols.ka, cols.va)) for hd in range(8)]
    segs.append((1536, 8, cols.fa))
    order = _swa_head_order()
    segs += [(1544 + HEAD_DIM * hd, HEAD_DIM, cols.qb + HEAD_DIM * order.index(hd)) for hd in range(8)]
    segs += [(2056, 128, cols.kb), (2184, 128, cols.vb), (2312, d, cols.ga), (2312 + d, d, cols.gb)]
    return segs


def _reorder_w_in(blocks, cols):
    width = blocks[0].shape[1]
    zeros = lambda n: jnp.zeros((blocks[0].shape[0], n), blocks[0].dtype)
    parts, at = [], 0
    for old, length, new in sorted(_w_in_segments(cols), key=lambda s: s[2]):
        if new > at:
            parts.append(zeros(new - at))
        at = new + length
        while length:
            j, off = divmod(old, width)
            take = min(length, width - off)
            parts.append(blocks[j][:, off:off + take])
            old, length = old + take, length - take
    parts.append(zeros(cols.np - at))
    return jnp.concatenate(parts, axis=1)


def _restore_w_in(wp, cols, width):
    segs = sorted(_w_in_segments(cols))
    blocks = []
    for j in range(N_DEV):
        lo, hi = j * width, (j + 1) * width
        parts = []
        for old, length, new in segs:
            a, b = max(old, lo), min(old + length, hi)
            if a < b:
                parts.append(wp[:, new + a - old:new + b - old])
        blocks.append(jnp.concatenate(parts, axis=1))
    return jnp.stack(blocks)


def _lane_pad(v):
    return jnp.pad(v, ((0, 0), (0, LANES - v.shape[1])))


def kernel(x, meta_tokens, rel_bias_table, ffn1_norm, ffn1_w_in, ffn1_w_out, mix_norm, w_in, forget_bias, fox_q_norm, fox_k_norm, swa_q_norm, swa_k_norm, swa_sinks, w_branch_fox, w_branch_swa, w_out, ffn2_norm, ffn2_w_in, ffn2_w_out, loss_target, m_meta_tokens, m_rel_bias_table, m_ffn1_norm, m_ffn1_w_in, m_ffn1_w_out, m_mix_norm, m_w_in, m_forget_bias, m_fox_q_norm, m_fox_k_norm, m_swa_q_norm, m_swa_k_norm, m_swa_sinks, m_w_branch_fox, m_w_branch_swa, m_w_out, m_ffn2_norm, m_ffn2_w_in, m_ffn2_w_out, v_meta_tokens, v_rel_bias_table, v_ffn1_norm, v_ffn1_w_in, v_ffn1_w_out, v_mix_norm, v_w_in, v_forget_bias, v_fox_q_norm, v_fox_k_norm, v_swa_q_norm, v_swa_k_norm, v_swa_sinks, v_w_branch_fox, v_w_branch_swa, v_w_out, v_ffn2_norm, v_ffn2_w_in, v_ffn2_w_out):
    args = dict(locals())
    wts = {n: args[n] for n in WEIGHTS}
    mom1 = {n: args["m_" + n] for n in WEIGHTS}
    mom2 = {n: args["v_" + n] for n in WEIGHTS}

    seq, d = x.shape[1], x.shape[2]
    m_rows = seq + LANES
    depth = ffn1_norm.shape[0]
    fb = ffn1_w_in.shape[2]
    fo = ffn1_w_out.shape[1]
    din_shard = w_in.shape[2]
    cols = _Cols(d)
    scale = HEAD_DIM ** -0.5
    dev = 4 * lax.axis_index("x") + 2 * lax.axis_index("y") + lax.axis_index("c")

    groups = {"ffn1": ("ffn1_w_in", "ffn1_w_out"), "mix": ("w_in", "w_branch_fox", "w_branch_swa", "w_out"),
              "ffn2": ("ffn2_w_in", "ffn2_w_out"), "ffn1_in": ("ffn1_w_in",), "ffn1_out": ("ffn1_w_out",),
              "ffn2_in": ("ffn2_w_in",), "ffn2_out": ("ffn2_w_out",)}
    shard = {n: wts[n].astype(BF16) for n in BIG}
    full, parts, gw = {}, {}, {}

    def keys_of(stages):
        return [(n, l) for g, l in stages if l < depth for n in groups[g]]

    def gather_rider(stages):
        return Rider([shard[n][l] for n, l in keys_of(stages)], True)

    def scatter_rider(stages):
        return Rider([gw[k] for k in keys_of(stages)], False)

    def ffn_weights(tag, l):
        return full[tag + "_w_in", l], full[tag + "_w_out", l].reshape(4, fb, d)

    def mixer_weights(l):
        wp = _reorder_w_in([full["w_in", l][j] for j in range(N_DEV)], cols)
        wbf = jnp.concatenate([full["w_branch_fox", l][j] for j in range(N_DEV)], axis=1)
        wbf = jnp.pad(wbf.reshape(8, HEAD_DIM, d), ((0, 0), (0, LANES - HEAD_DIM), (0, 0))).reshape(8 * LANES, d)
        wbs = _permute_heads(jnp.concatenate([full["w_branch_swa", l][j] for j in range(N_DEV)], axis=1), 0)
        return wp, wbf, wbs, full["w_out", l].reshape(d, d)

    full.update(zip(keys_of([("ffn1", 0)]), exchange_hbm(gather_rider([("ffn1", 0)]).srcs, True, "gather_first")))
    meta_all = gather_small(meta_tokens.reshape(1, N_META, -1), "gather_meta")
    meta_full = meta_all.transpose(1, 0, 2).reshape(N_META, d)
    tile8 = lambda g, s=1.0: jnp.tile(g.reshape(1, HEAD_DIM) * s, (1, 8))
    tile2 = lambda g: jnp.tile(g.reshape(1, HEAD_DIM), (1, 2))
    data_lanes = lambda g, s=1.0: _lane_pad(g.reshape(1, HEAD_DIM) * s)
    bias = bias_build(rel_bias_table, "swa_bias")

    first = jnp.concatenate([jnp.zeros((PAD_FRONT, d), F32), meta_full], axis=0)
    h = jnp.concatenate([first, x[0]], axis=0)
    saved, lw = [], []
    for l in range(depth):
        s, w = {"h0": h}, {}
        w["ffn1_in"], w["ffn1_out"] = ffn_weights("ffn1", l)
        stages = [("mix", l)]
        (h, s["n1"], s["gate1"], s["up1"]), got = ffn_fwd(h, ffn1_norm[l:l + 1], w["ffn1_in"], w["ffn1_out"],
                                                          f"ffn1_fwd_{l}", gather_rider(stages))
        full.update(zip(keys_of(stages), got))
        s["h1"] = h
        w["wp"], w["wbf"], w["wbs"], w["wo"] = mixer_weights(l)
        s["nm"], s["proj"] = mixer_proj(h, mix_norm[l:l + 1], w["wp"], f"mixer_proj_{l}")
        s["gains"] = (data_lanes(fox_q_norm[l], scale), data_lanes(fox_k_norm[l]), tile8(swa_q_norm[l], scale),
                      tile2(swa_k_norm[l]))
        s["fbias"] = _lane_pad(forget_bias[l:l + 1])
        qf, kf, vf, qb, kb, vb = qk_post(s["proj"], s["gains"], s["fbias"], cols, f"qk_post_{l}")
        s.update(qf=qf, kf=kf, vf=vf, qb=qb, kb=kb, vb=vb)
        stages = [("ffn2", l)]
        (s["o_fox"], s["lse_fox"]), got = fox_fwd(qf, kf, vf.T, f"fox_fwd_{l}", gather_rider(stages))
        full.update(zip(keys_of(stages), got))
        s["o_swa"], s["lse_swa"] = swa_fwd(qb, kb, vb, bias, swa_sinks[l], f"swa_fwd_{l}")
        h = branch_out(h, s["o_fox"], s["o_swa"], s["proj"], w["wbf"], w["wbs"], w["wo"], cols, f"branch_out_{l}")
        s["h2"] = h
        w["ffn2_in"], w["ffn2_out"] = ffn_weights("ffn2", l)
        stages = [("ffn1", l + 1)]
        (h, s["n2"], s["gate2"], s["up2"]), got = ffn_fwd(h, ffn2_norm[l:l + 1], w["ffn2_in"], w["ffn2_out"],
                                                          f"ffn2_fwd_{l}", gather_rider(stages))
        full.update(zip(keys_of(stages), got))
        saved.append(s)
        lw.append(w)

    dh, loss_part = loss_head(h, loss_target[0], "loss_head")

    gs = {n: [None] * depth for n in SMALL}
    dbias_total = None
    for l in reversed(range(depth)):
        w, s = lw[l], saved[l]

        def ffn_back(dh, tag, hin, norm, n_in, gate, up, stages):
            (dh_in, a, dg, du, dgn, dhs), got = ffn_bwd(dh, hin, norm, gate, up, w[tag + "_in"], w[tag + "_out"],
                                                        f"{tag}_bwd_{l}", scatter_rider(stages))
            parts.update(zip(keys_of(stages), got))
            gw[tag + "_w_out", l] = matmul_tn(a, dhs[None], f"{tag}_dwo_{l}").reshape(N_DEV, fo, d)
            stages = [(tag + "_out", l)]
            gw[tag + "_w_in", l], got = matmul_tn(n_in[None], dg, f"{tag}_dwi_{l}", y2=du,
                                                  rider=scatter_rider(stages))
            parts.update(zip(keys_of(stages), got))
            return dh_in, dgn

        dh, gs["ffn2_norm"][l] = ffn_back(dh, "ffn2", s["h2"], ffn2_norm[l:l + 1], s["n2"], s["gate2"], s["up2"],
                                          [("ffn1_in", l + 1)])

        y, dtf, dts, dga, dgb, dof, dos, delta = branch_out_bwd(dh, s["o_fox"], s["o_swa"], s["proj"], w["wbf"],
                                                                w["wbs"], w["wo"], cols, f"branch_out_bwd_{l}")
        gw["w_out", l] = matmul_tn(y[None], dh[None], f"dw_out_{l}").reshape(N_DEV, d // N_DEV, d)
        to_shards = lambda a: a.reshape(512, N_DEV, d // N_DEV).transpose(1, 0, 2)
        gw["w_branch_fox", l] = to_shards(matmul_tn(s["o_fox"][None], dtf[None], f"dw_branch_fox_{l}")[0]
                                          .reshape(8, LANES, d)[:, :HEAD_DIM].reshape(512, d))
        gw["w_branch_swa", l] = to_shards(_permute_heads(
            matmul_tn(s["o_swa"][None], dts[None], f"dw_branch_swa_{l}")[0], 0, inverse=True))

        stages = [("ffn2_in", l)]
        (dkf, dvf, dqf, dc_rows), got = fox_bwd(s["qf"], s["kf"], s["vf"], s["kf"].T, dof, s["lse_fox"],
                                         delta[:, :8].T.reshape(8, 1, m_rows), f"fox_bwd_{l}", scatter_rider(stages))
        parts.update(zip(keys_of(stages), got))
        dc = _lane_pad(dc_rows.reshape(8, m_rows).T)
        dqb, dkb, dvb, dbias, dsink = swa_bwd(s["qb"], s["kb"], s["vb"], bias, swa_sinks[l], s["o_swa"], s["lse_swa"],
                                              dos, f"swa_bwd_{l}")
        dbias_total = dbias if dbias_total is None else dbias_total + dbias
        gs["swa_sinks"][l] = dsink[0, :8]
        dproj, ggqa, ggka, ggqb, ggkb, gfb = qk_post_bwd(s["proj"], s["gains"], s["fbias"], dqf, dkf, dvf, dqb, dkb,
                                                         dvb, dc, dga, dgb, cols, f"qk_post_bwd_{l}")
        gs["fox_q_norm"][l] = ggqa[0, :HEAD_DIM] * scale
        gs["fox_k_norm"][l] = ggka[0, :HEAD_DIM]
        gs["swa_q_norm"][l] = ggqb.reshape(8, HEAD_DIM).sum(0) * scale
        gs["swa_k_norm"][l] = ggkb.reshape(2, HEAD_DIM).sum(0)
        gs["forget_bias"][l] = gfb[0, :8]
        dwp = matmul_tn(s["nm"][None], dproj[None], f"dw_in_{l}", tn=1024 if cols.np % 1024 == 0 else None)[0]
        gw["w_in", l] = _restore_w_in(dwp, cols, din_shard)
        dh, gs["mix_norm"][l] = dproj_bwd(dh, s["h1"], mix_norm[l:l + 1], dproj, w["wp"], f"dproj_bwd_{l}")

        dh, gs["ffn1_norm"][l] = ffn_back(dh, "ffn1", s["h0"], ffn1_norm[l:l + 1], s["n1"], s["gate1"], s["up1"],
                                          [("mix", l)])

    grad_x = dh[LANES:][None]
    dmeta = dh[PAD_FRONT:LANES]
    dtable = bias_reduce(dbias_total, "swa_dbias")[:, :8]

    parts.update(zip(keys_of([("ffn1_in", 0)]),
                     exchange_hbm(scatter_rider([("ffn1_in", 0)]).srcs, False, "scatter_last")))
    big_out = [{}, {}, {}, {}]
    for n in BIG:
        outs = adamw_sum([parts[n, l] for l in range(depth)], wts[n], mom1[n], mom2[n], f"adamw_{n}")
        for k in range(4):
            big_out[k][n] = outs[k]

    small_g = {n: (jnp.stack(gs[n]) if n != "rel_bias_table" else None) for n in SMALL}
    small_g["rel_bias_table"] = dtable
    pieces = [loss_part[0:1, 0:1].reshape(1, 1)] + [small_g[n].reshape(1, -1) for n in SMALL] + [dmeta.reshape(1, -1)]
    small_shapes = [(1,)] + [wts[n].shape for n in SMALL] + [(N_META, d)]
    total = allsum_small(_pack(pieces, LANES, 8, F32), "allsum_small")
    summed = _unpack(total, small_shapes)
    loss = summed[0][0]
    g_small = dict(zip(SMALL, summed[1:1 + len(SMALL)]))
    g_meta = lax.dynamic_slice_in_dim(summed[-1], dev * (d // N_DEV), d // N_DEV, axis=1)
    names = SMALL + ("meta_tokens",)
    g_small["meta_tokens"] = g_meta
    pk = lambda src: _pack([src[n].reshape(1, -1) for n in names], LANES, 8, F32)[0]
    small_out = [dict(zip(names, _unpack(o, [wts[n].shape for n in names])))
                 for o in adamw_small(pk(g_small), pk(wts), pk(mom1), pk(mom2), "adamw_small")]

    grads = {**big_out[0], **g_small}
    delta = {**big_out[1], **small_out[0]}
    new_m = {**big_out[2], **small_out[1]}
    new_v = {**big_out[3], **small_out[2]}
    return (loss, grad_x, *[grads[n] for n in WEIGHTS], *[delta[n] for n in WEIGHTS],
            *[new_m[n] for n in WEIGHTS], *[new_v[n] for n in WEIGHTS])
```

```python
import math

import numpy as np
import jax
import jax.numpy as jnp
from jax import lax
from jax.experimental import pallas as pl
from jax.experimental.pallas import tpu as pltpu

F32 = jnp.float32
BF16 = jnp.bfloat16
EPS = 1e-6
NEG = -1e30
HEAD_DIM = 64
LANES = 128
N_META = 16
PAD_FRONT = LANES - N_META
N_BUCKETS = 32
MAX_DISTANCE = 128
N_DEV = 8
ADAM_LR, ADAM_B1, ADAM_B2, ADAM_EPS, ADAM_WD, ADAM_STEP = 0.001, 0.9, 0.999, 1e-08, 0.01, 10
VMEM_LIMIT = 56 * 1024 * 1024
MESH = pl.DeviceIdType.MESH


def _params(n_grid):
    return pltpu.CompilerParams(dimension_semantics=("arbitrary",) * n_grid,
                                vmem_limit_bytes=VMEM_LIMIT)


def _dot(a, b):
    return jnp.dot(a, b, preferred_element_type=F32)


def _dot_nt(a, b):
    return lax.dot_general(a, b, (((1,), (1,)), ((), ())), preferred_element_type=F32)


def _dot_tn(a, b):
    return lax.dot_general(a, b, (((0,), (0,)), ((), ())), preferred_element_type=F32)


def _rms(x):
    r = lax.rsqrt(jnp.mean(x * x, axis=-1, keepdims=True) + EPS)
    return x * r, r


def _rms_bwd(x, g, dn):
    xh, r = _rms(x)
    dxh = dn * g
    dx = r * (dxh - xh * jnp.mean(dxh * xh, axis=-1, keepdims=True))
    return dx, jnp.sum(dn * xh, axis=0, keepdims=True)


def _split2(v):
    hi = v.astype(BF16)
    return hi, (v - hi.astype(F32)).astype(BF16)


def _split3(v):
    hi = v.astype(BF16)
    r1 = v - hi.astype(F32)
    mid = r1.astype(BF16)
    return hi, mid, (r1 - mid.astype(F32)).astype(BF16)


def _group_ones():
    r = lax.broadcasted_iota(jnp.int32, (LANES, LANES), 0) // HEAD_DIM
    c = lax.broadcasted_iota(jnp.int32, (LANES, LANES), 1) // HEAD_DIM
    return jnp.where(r == c, 1.0, 0.0).astype(BF16)


def _group_mean(v, ones):
    hi, lo = _split2(v)
    return (_dot(hi, ones) + _dot(lo, ones)) * (1.0 / HEAD_DIM)


def _row_tile(m):
    return 384 if m % 384 == 0 else LANES


def _tile(m, cap):
    return max(t for t in range(16, cap + 1, 16) if m % t == 0)


def _peer(k):
    x, y, c = lax.axis_index("x"), lax.axis_index("y"), lax.axis_index("c")
    px = 1 - x if k & 4 else x
    py = 1 - y if k & 2 else y
    pc = 1 - c if k & 1 else c
    return (px, py, pc), 4 * px + 2 * py + pc


def _exchange_body(src_ref, dst_ref, send_sems, recv_sems, local_sem, bcast):
    x, y, c = lax.axis_index("x"), lax.axis_index("y"), lax.axis_index("c")
    me = 4 * x + 2 * y + c
    mine = pltpu.make_async_copy(src_ref.at[0 if bcast else me], dst_ref.at[me], local_sem)
    mine.start()
    sends = []
    for k in range(1, N_DEV):
        dev, idx = _peer(k)
        cp = pltpu.make_async_remote_copy(
            src_ref=src_ref.at[0 if bcast else idx], dst_ref=dst_ref.at[me],
            send_sem=send_sems.at[k - 1], recv_sem=recv_sems.at[k - 1],
            device_id=dev, device_id_type=MESH)
        cp.start()
        sends.append(cp)
    for k in range(1, N_DEV):
        dev, idx = _peer(k)
        pltpu.make_async_remote_copy(
            src_ref=src_ref.at[0], dst_ref=dst_ref.at[idx],
            send_sem=send_sems.at[k - 1], recv_sem=recv_sems.at[k - 1],
            device_id=dev, device_id_type=MESH).wait_recv()
    for cp in sends:
        cp.wait_send()
    mine.wait()


class Rider:
    FIRST = (1, 2, 4, 6)
    RELAYED = (2, 4, 6)

    def __init__(self, srcs=(), bcast=True):
        self.srcs, self.bcast, self.n = list(srcs), bcast, len(srcs)

    def out_shapes(self):
        return [jax.ShapeDtypeStruct(((N_DEV,) + s.shape) if self.bcast else s.shape, s.dtype) for s in self.srcs]

    def specs(self):
        return [pl.BlockSpec(memory_space=pl.ANY)] * self.n

    def scratch(self):
        if not self.n:
            return []
        return [pltpu.SemaphoreType.DMA((self.n * (N_DEV - 1),)), pltpu.SemaphoreType.DMA((self.n * (N_DEV - 1),)),
                pltpu.SemaphoreType.DMA((self.n,))]

    @staticmethod
    def _copy(src, dst, a, pair, dev, send_sems, recv_sems):
        sem = a * (N_DEV - 1) + pair - 1
        return pltpu.make_async_remote_copy(src_ref=src, dst_ref=dst, send_sem=send_sems.at[sem],
                                            recv_sem=recv_sems.at[sem], device_id=dev, device_id_type=MESH)

    def _first(self):
        return self.FIRST if self.bcast else range(1, N_DEV)

    def _own(self, s, d, a, local_sems):
        me = 4 * lax.axis_index("x") + 2 * lax.axis_index("y") + lax.axis_index("c")
        return pltpu.make_async_copy(s if self.bcast else s.at[me], d.at[me], local_sems.at[a]), me

    def start(self, src_refs, dst_refs, send_sems, recv_sems, local_sems):
        for a, (s, d) in enumerate(zip(src_refs, dst_refs)):
            own, me = self._own(s, d, a, local_sems)
            own.start()
            for k in self._first():
                dev, idx = _peer(k)
                self._copy(s if self.bcast else s.at[idx], d.at[me], a, k, dev, send_sems, recv_sems).start()

    def relay(self, src_refs, dst_refs, send_sems, recv_sems, local_sems):
        if not self.bcast:
            return
        sibling, _ = _peer(1)
        for a, d in enumerate(dst_refs):
            for k in self.RELAYED:
                dev, idx = _peer(k)
                self._copy(d.at[idx], d.at[idx], a, k, dev, send_sems, recv_sems).wait_recv()
                self._copy(d.at[idx], d.at[idx], a, k + 1, sibling, send_sems, recv_sems).start()

    def wait(self, src_refs, dst_refs, send_sems, recv_sems, local_sems):
        sibling, _ = _peer(1)
        for a, (s, d) in enumerate(zip(src_refs, dst_refs)):
            own, me = self._own(s, d, a, local_sems)
            for k in range(1, N_DEV):
                if not (self.bcast and k in self.RELAYED):
                    dev, idx = _peer(k)
                    self._copy(d.at[idx], d.at[idx], a, k, dev, send_sems, recv_sems).wait_recv()
            for k in self._first():
                dev, idx = _peer(k)
                self._copy(s if self.bcast else s.at[idx], d.at[me], a, k, dev, send_sems, recv_sems).wait_send()
            if self.bcast:
                for k in self.RELAYED:
                    dev, idx = _peer(k)
                    self._copy(d.at[idx], d.at[idx], a, k + 1, sibling, send_sems, recv_sems).wait_send()
            own.wait()


def rider_call(core, name, grid, in_specs, out_specs, out_shape, scratch_shapes, args, rider=None):
    rider = rider or Rider()
    n_in, n_out, n_scr, nr = len(in_specs), len(out_specs), len(scratch_shapes), rider.n

    def body(*refs):
        ins, r_src = refs[:n_in], refs[n_in:n_in + nr]
        outs = refs[n_in + nr:n_in + nr + n_out]
        r_dst = refs[n_in + nr + n_out:n_in + 2 * nr + n_out]
        scr = refs[n_in + 2 * nr + n_out:n_in + 2 * nr + n_out + n_scr]
        sems = refs[n_in + 2 * nr + n_out + n_scr:]
        if nr:
            first, relay, last = True, True, True
            for ax, size in enumerate(grid):
                first = first & (pl.program_id(ax) == 0)
                relay = relay & (pl.program_id(ax) == (3 * size // 4 if ax == 0 else 0))
                last = last & (pl.program_id(ax) == size - 1)
            if not grid:
                rider.start(r_src, r_dst, *sems)
                rider.relay(r_src, r_dst, *sems)
            else:
                pl.when(first)(lambda: rider.start(r_src, r_dst, *sems))
                if rider.bcast:
                    pl.when(relay)(lambda: rider.relay(r_src, r_dst, *sems))
        core(*ins, *outs, *scr)
        if nr:
            if not grid:
                rider.wait(r_src, r_dst, *sems)
            else:
                pl.when(last)(lambda: rider.wait(r_src, r_dst, *sems))

    res = pl.pallas_call(
        body, name=name, grid=grid,
        in_specs=list(in_specs) + rider.specs(),
        out_specs=list(out_specs) + rider.specs(),
        out_shape=list(out_shape) + rider.out_shapes(),
        scratch_shapes=list(scratch_shapes) + rider.scratch(),
        compiler_params=_params(len(grid)),
    )(*args, *rider.srcs)
    return res[:n_out], res[n_out:]


def exchange_hbm(srcs, bcast, name):
    return rider_call(lambda: None, name, (), [], [], [], [], [], Rider(srcs, bcast))[1]


def allsum_small(vec, name):
    def body(src_ref, out_ref, dst_ref, send_sems, recv_sems, local_sem):
        _exchange_body(src_ref, dst_ref, send_sems, recv_sems, local_sem, True)
        acc = dst_ref[0]
        for j in range(1, N_DEV):
            acc = acc + dst_ref[j]
        out_ref[...] = acc

    return pl.pallas_call(
        body, name=name,
        out_shape=jax.ShapeDtypeStruct(vec.shape[1:], F32),
        in_specs=[pl.BlockSpec(memory_space=pltpu.VMEM)],
        out_specs=pl.BlockSpec(memory_space=pltpu.VMEM),
        scratch_shapes=[pltpu.VMEM((N_DEV,) + vec.shape[1:], F32),
                        pltpu.SemaphoreType.DMA((N_DEV - 1,)), pltpu.SemaphoreType.DMA((N_DEV - 1,)),
                        pltpu.SemaphoreType.DMA],
    )(vec)


def gather_small(vec, name):
    def body(src_ref, dst_ref, send_sems, recv_sems, local_sem):
        _exchange_body(src_ref, dst_ref, send_sems, recv_sems, local_sem, True)

    return pl.pallas_call(
        body, name=name,
        out_shape=jax.ShapeDtypeStruct((N_DEV,) + vec.shape[1:], F32),
        in_specs=[pl.BlockSpec(memory_space=pltpu.VMEM)],
        out_specs=pl.BlockSpec(memory_space=pltpu.VMEM),
        scratch_shapes=[pltpu.SemaphoreType.DMA((N_DEV - 1,)), pltpu.SemaphoreType.DMA((N_DEV - 1,)),
                        pltpu.SemaphoreType.DMA],
    )(vec)


FFN_FWD_ROWS = 1056
FFN_BWD_ROWS = 704
DW_ROWS = 1408

def ffn_fwd(h, g, w_in8, w_out4, name, rider=None):
    m, d = h.shape
    fb = w_in8.shape[2]
    tm = _tile(m, FFN_FWD_ROWS)

    def body(h_ref, g_ref, wg_ref, wu_ref, wo_ref, hn_ref, n_ref, a_ref, fg_ref, fu_ref, acc_ref):
        i = pl.program_id(1)

        @pl.when(i == 0)
        def _():
            xh, _ = _rms(h_ref[...])
            n_ref[...] = (xh * g_ref[...]).astype(BF16)
            acc_ref[...] = jnp.zeros_like(acc_ref)

        n = n_ref[...]
        gate = _dot(n, wg_ref[0])
        up = _dot(n, wu_ref[0])
        sg = jax.nn.sigmoid(gate)
        silu = gate * sg
        a = (silu * up).astype(BF16)
        a_ref[0] = a
        fg_ref[0] = (up * (sg * (1.0 + gate * (1.0 - sg)))).astype(BF16)
        fu_ref[0] = silu.astype(BF16)
        acc_ref[...] += _dot(a, wo_ref[0])

        @pl.when(i == 3)
        def _():
            hn_ref[...] = h_ref[...] + 0.5 * acc_ref[...]

    return rider_call(
        body, name, (m // tm, 4),
        in_specs=[pl.BlockSpec((tm, d), lambda r, i: (r, 0)),
                  pl.BlockSpec((1, d), lambda r, i: (0, 0)),
                  pl.BlockSpec((1, d, fb), lambda r, i: (i, 0, 0)),
                  pl.BlockSpec((1, d, fb), lambda r, i: (i + 4, 0, 0)),
                  pl.BlockSpec((1, fb, d), lambda r, i: (i, 0, 0))],
        out_specs=[pl.BlockSpec((tm, d), lambda r, i: (r, 0)),
                   pl.BlockSpec((tm, d), lambda r, i: (r, 0))] + [pl.BlockSpec((1, tm, fb), lambda r, i: (i, r, 0))] * 3,
        out_shape=[jax.ShapeDtypeStruct((m, d), F32), jax.ShapeDtypeStruct((m, d), BF16)]
                  + [jax.ShapeDtypeStruct((4, m, fb), BF16)] * 3,
        scratch_shapes=[pltpu.VMEM((tm, d), F32)],
        args=(h, g, w_in8, w_in8, w_out4), rider=rider)


def ffn_bwd(dh, h, g, f_gate, f_up, w_in8, w_out4, name, rider=None):
    m, d = h.shape
    fb = w_in8.shape[2]
    tm = _tile(m, FFN_BWD_ROWS)

    def body(dh_ref, h_ref, g_ref, fg_ref, fu_ref, wg_ref, wu_ref, wo_ref,
             dhin_ref, dg_ref, du_ref, dgn_ref, dhs_ref, acc_ref):
        r = pl.program_id(0)
        i = pl.program_id(1)

        @pl.when(i == 0)
        def _():
            dhs_ref[...] = (0.5 * dh_ref[...]).astype(BF16)
            acc_ref[...] = jnp.zeros_like(acc_ref)

        @pl.when((r == 0) & (i == 0))
        def _():
            dgn_ref[...] = jnp.zeros_like(dgn_ref)

        da = _dot_nt(dhs_ref[...], wo_ref[0])
        dub = (da * fu_ref[0]).astype(BF16)
        dgb = (da * fg_ref[0]).astype(BF16)
        dg_ref[0] = dgb
        du_ref[0] = dub
        acc_ref[...] += _dot_nt(dgb, wg_ref[0]) + _dot_nt(dub, wu_ref[0])

        @pl.when(i == 3)
        def _():
            dx, dgain = _rms_bwd(h_ref[...], g_ref[...], acc_ref[...])
            dgn_ref[...] += dgain
            dhin_ref[...] = dh_ref[...] + dx

    row = lambda r, i: (r, 0)
    blk = lambda r, i: (i, r, 0)
    return rider_call(
        body, name, (m // tm, 4),
        in_specs=[pl.BlockSpec((tm, d), row), pl.BlockSpec((tm, d), row),
                  pl.BlockSpec((1, d), lambda r, i: (0, 0)),
                  pl.BlockSpec((1, tm, fb), blk), pl.BlockSpec((1, tm, fb), blk),
                  pl.BlockSpec((1, d, fb), lambda r, i: (i, 0, 0)),
                  pl.BlockSpec((1, d, fb), lambda r, i: (i + 4, 0, 0)),
                  pl.BlockSpec((1, fb, d), lambda r, i: (i, 0, 0))],
        out_specs=[pl.BlockSpec((tm, d), row),
                   pl.BlockSpec((1, tm, fb), blk), pl.BlockSpec((1, tm, fb), blk),
                   pl.BlockSpec((1, d), lambda r, i: (0, 0)),
                   pl.BlockSpec((tm, d), row)],
        out_shape=[jax.ShapeDtypeStruct((m, d), F32),
                   jax.ShapeDtypeStruct((4, m, fb), BF16), jax.ShapeDtypeStruct((4, m, fb), BF16),
                   jax.ShapeDtypeStruct((1, d), F32), jax.ShapeDtypeStruct((m, d), BF16)],
        scratch_shapes=[pltpu.VMEM((tm, d), F32)],
        args=(dh, h, g, f_gate, f_up, w_in8, w_in8, w_out4), rider=rider)


def matmul_tn(x, y, name, tn=None, y2=None, rider=None):
    bx, m, k = x.shape
    by, _, n = y.shape
    b = max(bx, by) * (2 if y2 is not None else 1)
    tm = _tile(m, DW_ROWS)
    tn = n if tn is None else tn
    nt = n // tn
    nr = m // tm

    def body(*refs):
        x_ref, y_ref = refs[0], refs[1]
        o_ref, acc_ref = refs[-2], refs[-1]
        r = pl.program_id(2)

        @pl.when(r == 0)
        def _():
            acc_ref[...] = jnp.zeros_like(acc_ref)

        if y2 is None:
            acc_ref[...] += _dot_tn(x_ref[0].astype(BF16), y_ref[0].astype(BF16))
        else:
            @pl.when(pl.program_id(0) < by)
            def _():
                acc_ref[...] += _dot_tn(x_ref[0].astype(BF16), y_ref[0].astype(BF16))

            @pl.when(pl.program_id(0) >= by)
            def _():
                acc_ref[...] += _dot_tn(x_ref[0].astype(BF16), refs[2][0].astype(BF16))

        @pl.when(r == nr - 1)
        def _():
            o_ref[0] = acc_ref[...].astype(BF16)

    x_map = (lambda i, j, r: (i, r, 0)) if bx > 1 else (lambda i, j, r: (0, r, 0))
    if y2 is None:
        y_specs = [pl.BlockSpec((1, tm, tn), (lambda i, j, r: (i, r, j)) if by > 1 else (lambda i, j, r: (0, r, j)))]
    else:
        y_specs = [pl.BlockSpec((1, tm, tn), lambda i, j, r: (jnp.minimum(i, by - 1), jnp.where(i < by, r, nr - 1), j)),
                   pl.BlockSpec((1, tm, tn), lambda i, j, r: (jnp.maximum(i - by, 0), jnp.where(i < by, 0, r), j))]
    (out,), carried = rider_call(
        body, name, (b, nt, nr),
        in_specs=[pl.BlockSpec((1, tm, k), x_map)] + y_specs,
        out_specs=[pl.BlockSpec((1, k, tn), lambda i, j, r: (i, 0, j))],
        out_shape=[jax.ShapeDtypeStruct((b, k, n), BF16)],
        scratch_shapes=[pltpu.VMEM((k, tn), F32)],
        args=[x, y] + ([y2] if y2 is not None else []), rider=rider)
    return (out, carried) if rider is not None else out


AUG = HEAD_DIM


class _Cols:
    def __init__(self, d):
        self.d = d
        self.ga, self.gb = 0, d
        self.qa, self.ka, self.va = 2 * d, 2 * d + 1024, 2 * d + 2048
        self.qb = 2 * d + 3072
        self.kb, self.vb, self.fa = self.qb + 512, self.qb + 640, self.qb + 768
        self.np = self.qb + 1024


def mixer_proj(h, g, wp, name):
    m, d = h.shape
    npad = wp.shape[1]
    tm = _row_tile(m)

    def body(h_ref, g_ref, w_ref, n_ref, p_ref):
        xh, _ = _rms(h_ref[...])
        n = (xh * g_ref[...]).astype(BF16)
        n_ref[...] = n
        p_ref[...] = _dot(n, w_ref[...])

    return pl.pallas_call(
        body, name=name, grid=(m // tm,),
        in_specs=[pl.BlockSpec((tm, d), lambda r: (r, 0)), pl.BlockSpec((1, d), lambda r: (0, 0)),
                  pl.BlockSpec((d, npad), lambda r: (0, 0))],
        out_specs=[pl.BlockSpec((tm, d), lambda r: (r, 0)), pl.BlockSpec((tm, npad), lambda r: (r, 0))],
        out_shape=[jax.ShapeDtypeStruct((m, d), BF16), jax.ShapeDtypeStruct((m, npad), F32)],
        compiler_params=_params(1),
    )(h, g, wp)


def _head_norm(x, gain, ones):
    outs = []
    for b in range(x.shape[1] // LANES):
        xb = x[:, b * LANES:(b + 1) * LANES]
        r = lax.rsqrt(_group_mean(xb * xb, ones) + EPS)
        outs.append(xb * r * gain[:, b * LANES:(b + 1) * LANES])
    return outs


def _head_norm_bwd(x, gain, dn, ones):
    dxs, dgs = [], []
    for b in range(x.shape[1] // LANES):
        sl = slice(b * LANES, (b + 1) * LANES)
        xb, dnb = x[:, sl], dn[:, sl]
        r = lax.rsqrt(_group_mean(xb * xb, ones) + EPS)
        xh = xb * r
        dxh = dnb * gain[:, sl]
        dxs.append(r * (dxh - xh * _group_mean(dxh * xh, ones)))
        dgs.append(jnp.sum(dnb * xh, axis=0, keepdims=True))
    return dxs, dgs


def _lane_col(v, lane_iota, idx):
    return jnp.sum(jnp.where(lane_iota == idx, v, 0.0), axis=1, keepdims=True)


def _aug(base, lane, vals):
    for i, v in enumerate(vals):
        base = jnp.where(lane == AUG + i, v, base)
    return base


def qk_post(proj, gains, fbias, cols, name):
    m = proj.shape[0]
    tm = _row_tile(m)
    gqa, gka, gqb, gkb = gains

    def body(qa_ref, ka_ref, va_ref, qb_ref, kb_ref, vb_ref, fa_ref, gqa_ref, gka_ref, gqb_ref, gkb_ref, fb_ref,
             qf_o, kf_o, vf_o, qb_o, kb_o, vb_o, carry_ref):
        r0 = pl.program_id(0)

        @pl.when(r0 == 0)
        def _():
            carry_ref[...] = jnp.zeros_like(carry_ref)

        z = fa_ref[...] + fb_ref[...]
        logf = jnp.minimum(z, 0.0) - jnp.log(1.0 + jnp.exp(-jnp.abs(z)))
        rr = lax.broadcasted_iota(jnp.int32, (tm, tm), 0)
        cc = lax.broadcasted_iota(jnp.int32, (tm, tm), 1)
        tril = jnp.where(cc <= rr, 1.0, 0.0).astype(BF16)
        p0, p1, p2 = _split3(logf)
        c = _dot(tril, p0) + _dot(tril, p1) + _dot(tril, p2) + carry_ref[...]
        carry_ref[...] += jnp.sum(logf, axis=0, keepdims=True)

        lane = lax.broadcasted_iota(jnp.int32, (tm, LANES), 1)
        is_pad = (r0 * tm + lax.broadcasted_iota(jnp.int32, (tm, 1), 0)) < PAD_FRONT
        ones = jnp.ones((LANES, LANES), BF16)
        for hd in range(8):
            sl = slice(hd * LANES, (hd + 1) * LANES)
            ch = _lane_col(c, lane, hd)
            ct = [p.astype(F32) for p in _split3(ch)]
            cs = [p.astype(F32) for p in _split3(-jnp.where(is_pad, -NEG, ch))]
            xq = qa_ref[:, sl]
            qn = xq * lax.rsqrt(_group_mean(xq * xq, ones) + EPS) * gqa_ref[...]
            qf_o[:, sl] = _aug(qn, lane, ct + [1.0, 1.0, 1.0]).astype(BF16)
            xk = ka_ref[:, sl]
            kn = xk * lax.rsqrt(_group_mean(xk * xk, ones) + EPS) * gka_ref[...]
            kf_o[:, sl] = _aug(kn, lane, [1.0, 1.0, 1.0] + cs).astype(BF16)
            vf_o[:, sl] = _aug(va_ref[:, sl], lane, [1.0, 1.0, 1.0]).astype(BF16)

        gones = _group_ones()
        for src, gn, dst in ((qb_ref, gqb_ref, qb_o), (kb_ref, gkb_ref, kb_o)):
            for b, blk in enumerate(_head_norm(src[...], gn[...], gones)):
                dst[:, b * LANES:(b + 1) * LANES] = blk.astype(BF16)
        vb_o[...] = vb_ref[...].astype(BF16)

    w1024 = lambda off: pl.BlockSpec((tm, 1024), lambda r, o=off // 1024: (r, o))
    w512 = lambda off: pl.BlockSpec((tm, 512), lambda r, o=off // 512: (r, o))
    w128 = lambda off: pl.BlockSpec((tm, LANES), lambda r, o=off // LANES: (r, o))
    vec = lambda w: pl.BlockSpec((1, w), lambda r: (0, 0))
    row = lambda w: pl.BlockSpec((tm, w), lambda r: (r, 0))
    return pl.pallas_call(
        body, name=name, grid=(m // tm,),
        in_specs=[w1024(cols.qa), w1024(cols.ka), w1024(cols.va), w512(cols.qb), w128(cols.kb), w128(cols.vb),
                  w128(cols.fa), vec(LANES), vec(LANES), vec(512), vec(LANES), vec(LANES)],
        out_specs=[row(1024), row(1024), row(1024), row(512), row(LANES), row(LANES)],
        out_shape=[jax.ShapeDtypeStruct((m, 1024), BF16)] * 3 + [jax.ShapeDtypeStruct((m, 512), BF16)]
                  + [jax.ShapeDtypeStruct((m, LANES), BF16)] * 2,
        scratch_shapes=[pltpu.VMEM((1, LANES), F32)],
        compiler_params=_params(1),
    )(proj, proj, proj, proj, proj, proj, proj, gqa, gka, gqb, gkb, fbias)


def qk_post_bwd(proj, gains, fbias, dqf, dkf, dvf, dqb, dkb, dvb, dc, dga, dgb, cols, name):
    m = proj.shape[0]
    d = cols.d
    tm = _row_tile(m)
    nt = m // tm
    gqa, gka, gqb, gkb = gains

    def body(qa_ref, ka_ref, qb_ref, kb_ref, fa_ref, gqa_ref, gka_ref, gqb_ref, gkb_ref, fb_ref,
             dqf_ref, dkf_ref, dvf_ref, dqb_ref, dkb_ref, dvb_ref, dc_ref, dga_ref, dgb_ref,
             dp_o, ggqa_o, ggka_o, ggqb_o, ggkb_o, gfb_o, carry_ref):
        @pl.when(pl.program_id(0) == 0)
        def _():
            carry_ref[...] = jnp.zeros_like(carry_ref)
            for o in (ggqa_o, ggka_o, ggqb_o, ggkb_o, gfb_o):
                o[...] = jnp.zeros_like(o)

        dp_o[:, cols.ga:cols.ga + d] = dga_ref[...].astype(BF16)
        dp_o[:, cols.gb:cols.gb + d] = dgb_ref[...].astype(BF16)
        dp_o[:, cols.fa + LANES:cols.np] = jnp.zeros((tm, cols.np - cols.fa - LANES), BF16)
        lane = lax.broadcasted_iota(jnp.int32, (tm, LANES), 1)
        data = lane < HEAD_DIM
        ones = jnp.ones((LANES, LANES), BF16)
        for hd in range(8):
            sl = slice(hd * LANES, (hd + 1) * LANES)
            for src, gn, dn_ref, off, gout in ((qa_ref, gqa_ref, dqf_ref, cols.qa, ggqa_o),
                                               (ka_ref, gka_ref, dkf_ref, cols.ka, ggka_o)):
                x = src[:, sl]
                dn = jnp.where(data, dn_ref[:, sl], 0.0)
                r = lax.rsqrt(_group_mean(x * x, ones) + EPS)
                xh = x * r
                dxh = dn * gn[...]
                dp_o[:, off + hd * LANES:off + (hd + 1) * LANES] = (
                    r * (dxh - xh * _group_mean(dxh * xh, ones))).astype(BF16)
                gout[...] += jnp.sum(dn * xh, axis=0, keepdims=True)
            dp_o[:, cols.va + hd * LANES:cols.va + (hd + 1) * LANES] = jnp.where(data, dvf_ref[:, sl], 0.0).astype(BF16)
        dp_o[:, cols.vb:cols.vb + LANES] = dvb_ref[...].astype(BF16)
        gones = _group_ones()
        for src, gn, dn, off, gout in ((qb_ref, gqb_ref, dqb_ref, cols.qb, ggqb_o),
                                       (kb_ref, gkb_ref, dkb_ref, cols.kb, ggkb_o)):
            dxs, dgs = _head_norm_bwd(src[...], gn[...], dn[...], gones)
            for b, (dx, dg) in enumerate(zip(dxs, dgs)):
                dp_o[:, off + b * LANES:off + (b + 1) * LANES] = dx.astype(BF16)
                gout[:, b * LANES:(b + 1) * LANES] += dg
        dcv = dc_ref[...]
        rr = lax.broadcasted_iota(jnp.int32, (tm, tm), 0)
        cc = lax.broadcasted_iota(jnp.int32, (tm, tm), 1)
        triu = jnp.where(cc >= rr, 1.0, 0.0).astype(BF16)
        p0, p1, p2 = _split3(dcv)
        dlogf = _dot(triu, p0) + _dot(triu, p1) + _dot(triu, p2) + carry_ref[...]
        carry_ref[...] += jnp.sum(dcv, axis=0, keepdims=True)
        z = fa_ref[...] + fb_ref[...]
        row = (nt - 1 - pl.program_id(0)) * tm + lax.broadcasted_iota(jnp.int32, (tm, LANES), 0)
        dfa = jnp.where(row >= PAD_FRONT, dlogf * jax.nn.sigmoid(-z), 0.0)
        dp_o[:, cols.fa:cols.fa + LANES] = dfa.astype(BF16)
        gfb_o[...] += jnp.sum(dfa, axis=0, keepdims=True)

    rev = lambda r: nt - 1 - r
    w1024 = lambda off: pl.BlockSpec((tm, 1024), lambda r, o=off // 1024: (rev(r), o))
    w512 = lambda off: pl.BlockSpec((tm, 512), lambda r, o=off // 512: (rev(r), o))
    w128 = lambda off: pl.BlockSpec((tm, LANES), lambda r, o=off // LANES: (rev(r), o))
    vec = lambda w: pl.BlockSpec((1, w), lambda r: (0, 0))
    row = lambda w: pl.BlockSpec((tm, w), lambda r: (rev(r), 0))
    return pl.pallas_call(
        body, name=name, grid=(nt,),
        in_specs=[w1024(cols.qa), w1024(cols.ka), w512(cols.qb), w128(cols.kb), w128(cols.fa),
                  vec(LANES), vec(LANES), vec(512), vec(LANES), vec(LANES),
                  row(1024), row(1024), row(1024), row(512), row(LANES), row(LANES), row(LANES), row(d), row(d)],
        out_specs=[row(cols.np), vec(LANES), vec(LANES), vec(512), vec(LANES), vec(LANES)],
        out_shape=[jax.ShapeDtypeStruct((m, cols.np), BF16), jax.ShapeDtypeStruct((1, LANES), F32),
                   jax.ShapeDtypeStruct((1, LANES), F32), jax.ShapeDtypeStruct((1, 512), F32),
                   jax.ShapeDtypeStruct((1, LANES), F32), jax.ShapeDtypeStruct((1, LANES), F32)],
        scratch_shapes=[pltpu.VMEM((1, LANES), F32)],
        compiler_params=_params(1),
    )(proj, proj, proj, proj, proj, gqa, gka, gqb, gkb, fbias, dqf, dkf, dvf, dqb, dkb, dvb, dc, dga, dgb)


def dproj_bwd(dh, h, g, dproj, wp, name):
    m, d = h.shape
    npad = wp.shape[1]
    tm = _row_tile(m)

    def body(dh_ref, h_ref, g_ref, dp_ref, w_ref, dhin_ref, dgn_ref):
        @pl.when(pl.program_id(0) == 0)
        def _():
            dgn_ref[...] = jnp.zeros_like(dgn_ref)

        dn = _dot_nt(dp_ref[...], w_ref[...])
        dx, dgain = _rms_bwd(h_ref[...], g_ref[...], dn)
        dgn_ref[...] += dgain
        dhin_ref[...] = dh_ref[...] + dx

    row = lambda w: pl.BlockSpec((tm, w), lambda r: (r, 0))
    return pl.pallas_call(
        body, name=name, grid=(m // tm,),
        in_specs=[row(d), row(d), pl.BlockSpec((1, d), lambda r: (0, 0)), row(npad),
                  pl.BlockSpec((d, npad), lambda r: (0, 0))],
        out_specs=[row(d), pl.BlockSpec((1, d), lambda r: (0, 0))],
        out_shape=[jax.ShapeDtypeStruct((m, d), F32), jax.ShapeDtypeStruct((1, d), F32)],
        compiler_params=_params(1),
    )(dh, h, g, dproj, wp)


def _causal_t(t):
    return lax.broadcasted_iota(jnp.int32, (t, t), 0) <= lax.broadcasted_iota(jnp.int32, (t, t), 1)


HEADS_PER_STEP = 2


def fox_fwd(qf, kf, vt, name, rider=None):
    m = qf.shape[0]
    t = _row_tile(m)
    nq = m // t
    hp = HEADS_PER_STEP
    w = hp * LANES

    def body(q_ref, k_ref, vt_ref, o_ref, lse_ref, acc_ref, m_ref, p_ref, a_ref):
        qi = pl.program_id(1)
        acc_ref[...] = jnp.zeros_like(acc_ref)
        m_ref[...] = jnp.full_like(m_ref, NEG)

        def scores(ki, slot, mask):
            off = pl.multiple_of(ki * t, t)
            for e in range(hp):
                sl = slice(e * LANES, (e + 1) * LANES)
                s = _dot_nt(k_ref[pl.ds(off, t), sl], q_ref[:, sl])
                if mask is not None:
                    s = jnp.where(mask, s, NEG)
                m_old = m_ref[e]
                m_new = jnp.maximum(m_old, jnp.max(s, axis=0, keepdims=True))
                p_ref[slot, e] = jnp.exp(s - m_new).astype(BF16)
                a_ref[slot, e] = jnp.exp(m_old - m_new)
                m_ref[e] = m_new

        def values(ki, slot):
            off = pl.multiple_of(ki * t, t)
            for e in range(hp):
                sl = slice(e * LANES, (e + 1) * LANES)
                acc_ref[e] = acc_ref[e] * a_ref[slot, e] + _dot(vt_ref[sl, pl.ds(off, t)], p_ref[slot, e])

        causal = _causal_t(t)
        scores(0, 0, causal | (jnp.full((t, t), qi, jnp.int32) > 0))

        def step(ki, carry):
            values(ki - 1, (ki - 1) % 2)
            scores(ki, ki % 2, None)
            return carry

        lax.fori_loop(1, qi, step, 0)

        @pl.when(qi >= 1)
        def _():
            values(qi - 1, (qi - 1) % 2)
            scores(qi, qi % 2, causal)

        values(qi, qi % 2)
        row = lax.broadcasted_iota(jnp.int32, (LANES, t), 0)
        for e in range(hp):
            l = jnp.max(acc_ref[e, AUG:AUG + 8, :], axis=0, keepdims=True)
            o_ref[:, e * LANES:(e + 1) * LANES] = jnp.where(row < HEAD_DIM, acc_ref[e] * (1.0 / l), 0.0).T
            lse_ref[e] = m_ref[e] + jnp.log(l)

    return rider_call(
        body, name, (8 // hp, nq),
        in_specs=[pl.BlockSpec((t, w), lambda hd, i: (i, hd)),
                  pl.BlockSpec((m, w), lambda hd, i: (0, hd)),
                  pl.BlockSpec((w, m), lambda hd, i: (hd, 0))],
        out_specs=[pl.BlockSpec((t, w), lambda hd, i: (i, hd)),
                   pl.BlockSpec((hp, 1, t), lambda hd, i: (hd, 0, i))],
        out_shape=[jax.ShapeDtypeStruct((m, 8 * LANES), F32), jax.ShapeDtypeStruct((8, 1, m), F32)],
        scratch_shapes=[pltpu.VMEM((hp, LANES, t), F32), pltpu.VMEM((hp, 1, t), F32),
                        pltpu.VMEM((2, hp, t, t), BF16), pltpu.VMEM((2, hp, 1, t), F32)],
        args=(qf, kf, vt), rider=rider)


def fox_bwd(qf, kf, vf, kt, dof, lse, delta, name, rider=None):
    m = qf.shape[0]
    t = _row_tile(m)
    nq = m // t
    hp = HEADS_PER_STEP
    w = hp * LANES

    def body(k_ref, v_ref, kt_ref, q_ref, do_ref, lse_ref, delta_ref, dk_ref, dv_ref, dq_ref, dck_ref, dcq_ref,
             dka_ref, dva_ref, dqt_ref):
        ki = pl.program_id(1)

        @pl.when(ki == 0)
        def _():
            dqt_ref[...] = jnp.zeros_like(dqt_ref)

        dka_ref[...] = jnp.zeros_like(dka_ref)
        dva_ref[...] = jnp.zeros_like(dva_ref)

        def tile(qi, diagonal):
            off = pl.multiple_of(qi * t, t)
            for e in range(hp):
                sl = slice(e * LANES, (e + 1) * LANES)
                q = q_ref[pl.ds(off, t), sl]
                do = do_ref[pl.ds(off, t), sl]
                s = _dot_nt(k_ref[:, sl], q)
                if diagonal:
                    s = jnp.where(_causal_t(t), s, NEG)
                p = jnp.exp(s - lse_ref[e, :, pl.ds(off, t)])
                ds = (p * (_dot_nt(v_ref[:, sl], do) - delta_ref[e, :, pl.ds(off, t)])).astype(BF16)
                dva_ref[:, sl] += _dot(p.astype(BF16), do)
                dka_ref[:, sl] += _dot(ds, q)
                dqt_ref[sl, pl.ds(off, t)] += _dot(kt_ref[sl, :], ds)

        def step(qi, carry):
            tile(qi, False)
            return carry

        tile(ki, True)
        lax.fori_loop(ki + 1, nq, step, 0)
        dk_ref[...] = dka_ref[...]
        dv_ref[...] = dva_ref[...]
        row8 = lax.broadcasted_iota(jnp.int32, (8, 1), 0)
        for e in range(hp):
            slab = dka_ref[:, e * LANES:(e + 1) * LANES].T[AUG:AUG + 8, :]
            dck_ref[e] = -jnp.sum(jnp.where(row8 == 3, slab, 0.0), axis=0, keepdims=True)

        @pl.when(ki == nq - 1)
        def _():
            for e in range(hp):
                sl = slice(e * LANES, (e + 1) * LANES)
                slab = dqt_ref[e * LANES + AUG:e * LANES + AUG + 8, :]
                dcq_ref[e] = jnp.sum(jnp.where(row8 == 0, slab, 0.0), axis=0, keepdims=True)
                for j in range(nq):
                    dq_ref[j * t:(j + 1) * t, sl] = dqt_ref[sl, j * t:(j + 1) * t].T

    tile_spec = pl.BlockSpec((t, w), lambda hd, i: (i, hd))
    full = pl.BlockSpec((m, w), lambda hd, i: (0, hd))
    stat = pl.BlockSpec((hp, 1, m), lambda hd, i: (hd, 0, 0))
    (dkf, dvf, dqf, dck, dcq), carried = rider_call(
        body, name, (8 // hp, nq),
        in_specs=[tile_spec, tile_spec, pl.BlockSpec((w, t), lambda hd, i: (hd, i)), full, full, stat, stat],
        out_specs=[tile_spec, tile_spec, full, pl.BlockSpec((hp, 1, t), lambda hd, i: (hd, 0, i)), stat],
        out_shape=[jax.ShapeDtypeStruct((m, 8 * LANES), F32), jax.ShapeDtypeStruct((m, 8 * LANES), F32),
                   jax.ShapeDtypeStruct((m, 8 * LANES), F32), jax.ShapeDtypeStruct((8, 1, m), F32),
                   jax.ShapeDtypeStruct((8, 1, m), F32)],
        scratch_shapes=[pltpu.VMEM((t, w), F32), pltpu.VMEM((t, w), F32), pltpu.VMEM((w, m), F32)],
        args=(kf, vf, kt, qf, dof, lse, delta), rider=rider)
    return (dkf, dvf, dqf, dcq + dck), carried


def _bucket_ids():
    def bucket(dist):
        n = np.maximum(dist, 0)
        max_exact = N_BUCKETS // 2
        nf = np.maximum(n, 1).astype(np.float32)
        large = max_exact + (np.log(nf / max_exact) / math.log(MAX_DISTANCE / max_exact)
                             * (N_BUCKETS - max_exact)).astype(np.int32)
        return np.where(n < max_exact, n, np.minimum(large, N_BUCKETS - 1))

    tl = np.arange(LANES)[:, None]
    sl = np.arange(LANES)[None, :]
    prev = bucket(LANES + tl - sl)
    cur = bucket(tl - sl)
    meta = np.full((LANES, LANES), N_BUCKETS - 1)
    return np.concatenate([prev, cur, meta], axis=1).astype(np.int32)


def bias_build(table, name):
    ids = jnp.asarray(_bucket_ids())

    def body(t_ref, id_ref, o_ref):
        idv = id_ref[...]
        for h in range(8):
            acc = jnp.zeros((LANES, 3 * LANES), F32)
            for b in range(N_BUCKETS):
                acc = jnp.where(idv == b, t_ref[b, h], acc)
            o_ref[h] = acc

    return pl.pallas_call(
        body, name=name,
        in_specs=[pl.BlockSpec(memory_space=pltpu.SMEM), pl.BlockSpec(memory_space=pltpu.VMEM)],
        out_specs=pl.BlockSpec(memory_space=pltpu.VMEM),
        out_shape=jax.ShapeDtypeStruct((8, LANES, 3 * LANES), F32),
    )(table, ids)


def bias_reduce(dbias, name):
    ids = jnp.asarray(_bucket_ids())

    def body(d_ref, id_ref, o_ref):
        idv = id_ref[...]
        rr = lax.broadcasted_iota(jnp.int32, (N_BUCKETS, LANES), 0)
        cc = lax.broadcasted_iota(jnp.int32, (N_BUCKETS, LANES), 1)
        acc = jnp.zeros((N_BUCKETS, LANES), F32)
        for h in range(8):
            dv = d_ref[h]
            for b in range(N_BUCKETS):
                val = jnp.sum(jnp.where(idv == b, dv, 0.0), keepdims=True)
                acc = jnp.where((rr == b) & (cc == h), val, acc)
        o_ref[...] = acc

    return pl.pallas_call(
        body, name=name,
        in_specs=[pl.BlockSpec(memory_space=pltpu.VMEM), pl.BlockSpec(memory_space=pltpu.VMEM)],
        out_specs=pl.BlockSpec(memory_space=pltpu.VMEM),
        out_shape=jax.ShapeDtypeStruct((N_BUCKETS, LANES), F32),
    )(dbias, ids)


def _swa_valid(n):
    shape = (LANES, 3 * LANES)
    tl = lax.broadcasted_iota(jnp.int32, shape, 0)
    col = lax.broadcasted_iota(jnp.int32, shape, 1)
    sl = col & (LANES - 1)
    nv = jnp.full(shape, n, jnp.int32)
    is_meta = sl >= PAD_FRONT
    prev = (col < LANES) & (sl > tl) & (nv >= 1) & ((nv >= 2) | is_meta)
    cur = (col >= LANES) & (col < 2 * LANES) & (sl <= tl) & ((nv >= 1) | is_meta)
    meta = (col >= 2 * LANES) & is_meta & ((nv >= 2) | ((nv == 1) & (sl <= tl)))
    return prev | cur | meta


def _swa_keys(ref, n):
    off_prev = pl.multiple_of(jnp.maximum(n - 1, 0) * LANES, LANES)
    off_cur = pl.multiple_of(n * LANES, LANES)
    return jnp.concatenate([ref[pl.ds(off_prev, LANES), :], ref[pl.ds(off_cur, LANES), :], ref[0:LANES, :]], axis=0)


def swa_fwd(q, k, v, bias, sinks, name):
    m = q.shape[0]

    def body(q_ref, k_ref, v_ref, bias_ref, sink_ref, o_ref, lse_ref):
        n = pl.program_id(0)
        lane1 = lax.broadcasted_iota(jnp.int32, (1, LANES), 1)
        lane_t = lax.broadcasted_iota(jnp.int32, (LANES, LANES), 1)
        in_head = [lane1 < HEAD_DIM, lane1 >= HEAD_DIM]
        kall = _swa_keys(k_ref, n)
        vall = _swa_keys(v_ref, n)
        vs = [jnp.where(in_head[g], vall, jnp.zeros_like(vall)) for g in (0, 1)]
        valid = _swa_valid(n)
        lse = jnp.zeros((LANES, LANES), F32)
        for b in range(4):
            qb = q_ref[:, b * LANES:(b + 1) * LANES]
            ob = jnp.zeros((LANES, LANES), F32)
            for g in (0, 1):
                h = 4 * g + b
                qe = jnp.where(in_head[g], qb, jnp.zeros_like(qb))
                s = jnp.where(valid, _dot_nt(qe, kall) + bias_ref[h], NEG)
                sink = sink_ref[h]
                mx = jnp.maximum(jnp.max(s, axis=1, keepdims=True), sink)
                p = jnp.exp(s - mx)
                den = jnp.sum(p, axis=1, keepdims=True) + jnp.exp(sink - mx)
                ob = ob + _dot((p / den).astype(BF16), vs[g])
                lse = jnp.where(lane_t == h, mx + jnp.log(den), lse)
            o_ref[:, b * LANES:(b + 1) * LANES] = ob
        lse_ref[...] = lse

    return pl.pallas_call(
        body, name=name, grid=(m // LANES,),
        in_specs=[pl.BlockSpec((LANES, 512), lambda n: (n, 0)),
                  pl.BlockSpec((m, LANES), lambda n: (0, 0)), pl.BlockSpec((m, LANES), lambda n: (0, 0)),
                  pl.BlockSpec((8, LANES, 3 * LANES), lambda n: (0, 0, 0)),
                  pl.BlockSpec(memory_space=pltpu.SMEM)],
        out_specs=[pl.BlockSpec((LANES, 512), lambda n: (n, 0)), pl.BlockSpec((LANES, LANES), lambda n: (n, 0))],
        out_shape=[jax.ShapeDtypeStruct((m, 512), F32), jax.ShapeDtypeStruct((m, LANES), F32)],
        compiler_params=_params(1),
    )(q, k, v, bias, sinks)


def swa_bwd(q, k, v, bias, sinks, o, lse, do, name):
    m = q.shape[0]

    def body(q_ref, do_ref, o_ref, lse_ref, k_ref, v_ref, bias_ref, sink_ref,
             dq_ref, dk_ref, dv_ref, dbias_ref, dsink_ref):
        n = pl.program_id(0)

        @pl.when(n == 0)
        def _():
            for r in (dk_ref, dv_ref, dbias_ref, dsink_ref):
                r[...] = jnp.zeros_like(r)

        lane1 = lax.broadcasted_iota(jnp.int32, (1, LANES), 1)
        lane_t = lax.broadcasted_iota(jnp.int32, (LANES, LANES), 1)
        in_head = [lane1 < HEAD_DIM, lane1 >= HEAD_DIM]
        off_prev = pl.multiple_of(jnp.maximum(n - 1, 0) * LANES, LANES)
        off_cur = pl.multiple_of(n * LANES, LANES)
        kall = _swa_keys(k_ref, n)
        vall = _swa_keys(v_ref, n)
        ks = [jnp.where(in_head[g], kall, jnp.zeros_like(kall)) for g in (0, 1)]
        valid = _swa_valid(n)
        lsev = lse_ref[...]
        dsink = dsink_ref[...]
        dkall = jnp.zeros((3 * LANES, LANES), F32)
        dvall = jnp.zeros((3 * LANES, LANES), F32)
        for b in range(4):
            sl = slice(b * LANES, (b + 1) * LANES)
            qb = q_ref[:, sl]
            dob = do_ref[:, sl]
            prod = dob * o_ref[:, sl]
            dqb = jnp.zeros((LANES, LANES), F32)
            for g in (0, 1):
                h = 4 * g + b
                qe = jnp.where(in_head[g], qb, jnp.zeros_like(qb))
                doe = jnp.where(in_head[g], dob, 0.0).astype(BF16)
                delta = jnp.sum(jnp.where(in_head[g], prod, 0.0), axis=1, keepdims=True)
                lse_h = _lane_col(lsev, lane_t, h)
                s = jnp.where(valid, _dot_nt(qe, kall) + bias_ref[h], NEG)
                p = jnp.exp(s - lse_h)
                ds = p * (_dot_nt(doe, vall) - delta)
                dbias_ref[h] += ds
                sink_part = jnp.sum(-jnp.exp(sink_ref[h] - lse_h) * delta, keepdims=True)
                dsink = jnp.where(lane1 == h, dsink + sink_part, dsink)
                dsb = ds.astype(BF16)
                dqb = dqb + _dot(dsb, ks[g])
                dkall = dkall + _dot_tn(dsb, qe)
                dvall = dvall + _dot_tn(p.astype(BF16), doe)
            dq_ref[:, sl] = dqb
        dsink_ref[...] = dsink
        for ref, val in ((dk_ref, dkall), (dv_ref, dvall)):
            ref[pl.ds(off_prev, LANES), :] += val[0:LANES]
            ref[pl.ds(off_cur, LANES), :] += val[LANES:2 * LANES]
            ref[0:LANES, :] += val[2 * LANES:3 * LANES]

    blk = pl.BlockSpec((LANES, 512), lambda n: (n, 0))
    full = pl.BlockSpec((m, LANES), lambda n: (0, 0))
    return pl.pallas_call(
        body, name=name, grid=(m // LANES,),
        in_specs=[blk, blk, blk, pl.BlockSpec((LANES, LANES), lambda n: (n, 0)), full, full,
                  pl.BlockSpec((8, LANES, 3 * LANES), lambda n: (0, 0, 0)),
                  pl.BlockSpec(memory_space=pltpu.SMEM)],
        out_specs=[blk, full, full, pl.BlockSpec((8, LANES, 3 * LANES), lambda n: (0, 0, 0)),
                   pl.BlockSpec((1, LANES), lambda n: (0, 0))],
        out_shape=[jax.ShapeDtypeStruct((m, 512), F32), jax.ShapeDtypeStruct((m, LANES), F32),
                   jax.ShapeDtypeStruct((m, LANES), F32), jax.ShapeDtypeStruct((8, LANES, 3 * LANES), F32),
                   jax.ShapeDtypeStruct((1, LANES), F32)],
        compiler_params=_params(1),
    )(q, do, o, lse, k, v, bias, sinks)


def branch_out(h, o_fox, o_swa, proj, wbf, wbs, wo, cols, name):
    m, d = h.shape
    tm = _row_tile(m)

    def body(h_ref, of_ref, os_ref, ga_ref, gb_ref, wbf_ref, wbs_ref, wo_ref, hn_ref):
        tf = _dot(of_ref[...].astype(BF16), wbf_ref[...])
        ts = _dot(os_ref[...].astype(BF16), wbs_ref[...])
        y = jax.nn.sigmoid(ga_ref[...]) * tf + jax.nn.sigmoid(gb_ref[...]) * ts
        hn_ref[...] = h_ref[...] + _dot(y.astype(BF16), wo_ref[...])

    row = lambda w, o=0: pl.BlockSpec((tm, w), lambda r, o=o: (r, o))
    res = lambda a: pl.BlockSpec(a.shape, lambda r: (0, 0))
    return pl.pallas_call(
        body, name=name, grid=(m // tm,),
        in_specs=[row(d), row(1024), row(512), row(d, cols.ga // d), row(d, cols.gb // d), res(wbf), res(wbs), res(wo)],
        out_specs=row(d),
        out_shape=jax.ShapeDtypeStruct((m, d), F32),
        compiler_params=_params(1),
    )(h, o_fox, o_swa, proj, proj, wbf, wbs, wo)


def branch_out_bwd(dh, o_fox, o_swa, proj, wbf, wbs, wo, cols, name):
    m, d = dh.shape
    tm = _row_tile(m)

    def body(dh_ref, of_ref, os_ref, ga_ref, gb_ref, wbf_ref, wbs_ref, wo_ref,
             y_ref, dtf_ref, dts_ref, dga_ref, dgb_ref, dof_ref, dos_ref, delta_ref):
        dy = _dot_nt(dh_ref[...].astype(BF16), wo_ref[...])
        tf = _dot(of_ref[...].astype(BF16), wbf_ref[...])
        ts = _dot(os_ref[...].astype(BF16), wbs_ref[...])
        sa = jax.nn.sigmoid(ga_ref[...])
        sb = jax.nn.sigmoid(gb_ref[...])
        y_ref[...] = (sa * tf + sb * ts).astype(BF16)
        dtf = (dy * sa).astype(BF16)
        dts = (dy * sb).astype(BF16)
        dtf_ref[...] = dtf
        dts_ref[...] = dts
        dga_ref[...] = (dy * tf * sa * (1.0 - sa)).astype(BF16)
        dgb_ref[...] = (dy * ts * sb * (1.0 - sb)).astype(BF16)
        dof = _dot_nt(dtf, wbf_ref[...])
        dof_ref[...] = dof.astype(BF16)
        dos_ref[...] = _dot_nt(dts, wbs_ref[...])
        lane = lax.broadcasted_iota(jnp.int32, (tm, LANES), 1)
        delta = jnp.zeros((tm, LANES), F32)
        for hd in range(8):
            sl = slice(hd * LANES, (hd + 1) * LANES)
            delta = jnp.where(lane == hd, jnp.sum(dof[:, sl] * of_ref[:, sl], axis=1, keepdims=True), delta)
        delta_ref[...] = delta

    row = lambda w, o=0: pl.BlockSpec((tm, w), lambda r, o=o: (r, o))
    res = lambda a: pl.BlockSpec(a.shape, lambda r: (0, 0))
    return pl.pallas_call(
        body, name=name, grid=(m // tm,),
        in_specs=[row(d), row(1024), row(512), row(d, cols.ga // d), row(d, cols.gb // d), res(wbf), res(wbs), res(wo)],
        out_specs=[row(d)] * 5 + [row(1024), row(512), row(LANES)],
        out_shape=[jax.ShapeDtypeStruct((m, d), BF16)] * 5 + [jax.ShapeDtypeStruct((m, 1024), BF16),
                   jax.ShapeDtypeStruct((m, 512), F32), jax.ShapeDtypeStruct((m, LANES), F32)],
        compiler_params=_params(1),
    )(dh, o_fox, o_swa, proj, proj, wbf, wbs, wo)


def loss_head(h, target, name):
    m, d = h.shape

    def body(h_ref, t_ref, dh_ref, loss_ref):
        n = pl.program_id(0)

        @pl.when(n == 0)
        def _():
            loss_ref[...] = jnp.zeros_like(loss_ref)
            dh_ref[...] = jnp.zeros_like(dh_ref)

        @pl.when(n > 0)
        def _():
            err = h_ref[...] - t_ref[...]
            dh_ref[...] = err * (1.0 / d)
            loss_ref[...] += jnp.sum(err * err, keepdims=True) * (0.5 / d)

    return pl.pallas_call(
        body, name=name, grid=(m // LANES,),
        in_specs=[pl.BlockSpec((LANES, d), lambda n: (n, 0)),
                  pl.BlockSpec((LANES, d), lambda n: (jnp.maximum(n - 1, 0), 0))],
        out_specs=[pl.BlockSpec((LANES, d), lambda n: (n, 0)), pl.BlockSpec((8, LANES), lambda n: (0, 0))],
        out_shape=[jax.ShapeDtypeStruct((m, d), F32), jax.ShapeDtypeStruct((8, LANES), F32)],
        compiler_params=_params(1),
    )(h, target)


def _adamw_math(w, g, m, v):
    m = ADAM_B1 * m + (1.0 - ADAM_B1) * g
    v = ADAM_B2 * v + (1.0 - ADAM_B2) * (g * g)
    m_hat = m / (1.0 - ADAM_B1 ** ADAM_STEP)
    v_hat = v / (1.0 - ADAM_B2 ** ADAM_STEP)
    delta = -ADAM_LR * (m_hat / (jnp.sqrt(v_hat) + ADAM_EPS) + ADAM_WD * w)
    return delta, m, v


def adamw_sum(parts, w, m, v, name):
    n_layers, a, b = w.shape
    ta = next(t for t in (256, 176, 128, a) if a % t == 0)
    nr = a // ta

    def body(*refs):
        p_refs = refs[:n_layers]
        w_ref, m_ref, v_ref, g_o, d_o, m_o, v_o = refs[n_layers:]
        for l in range(n_layers):
            @pl.when(pl.program_id(0) == l)
            def _(l=l):
                g = p_refs[l][0].astype(F32)
                for j in range(1, N_DEV):
                    g = g + p_refs[l][j].astype(F32)
                g_o[0] = g
                d_o[0], m_o[0], v_o[0] = _adamw_math(w_ref[0], g, m_ref[0], v_ref[0])

    def part_spec(l):
        return pl.BlockSpec((N_DEV, ta, b), lambda i, r, l=l: (0, jnp.where(i == l, r, jnp.where(i < l, 0, nr - 1)), 0))

    row = pl.BlockSpec((1, ta, b), lambda i, r: (i, r, 0))
    return pl.pallas_call(
        body, name=name, grid=(n_layers, nr),
        in_specs=[part_spec(l) for l in range(n_layers)] + [row, row, row],
        out_specs=[row] * 4,
        out_shape=[jax.ShapeDtypeStruct(w.shape, F32)] * 4,
        compiler_params=_params(2),
    )(*parts, w, m, v)


def adamw_small(g, w, m, v, name):
    def body(g_ref, w_ref, m_ref, v_ref, d_o, m_o, v_o):
        d_o[...], m_o[...], v_o[...] = _adamw_math(w_ref[...], g_ref[...], m_ref[...], v_ref[...])

    spec = pl.BlockSpec(memory_space=pltpu.VMEM)
    return pl.pallas_call(
        body, name=name, in_specs=[spec] * 4, out_specs=[spec] * 3,
        out_shape=[jax.ShapeDtypeStruct(w.shape, F32)] * 3,
    )(g, w, m, v)


BIG = ("ffn1_w_in", "ffn1_w_out", "w_in", "w_branch_fox", "w_branch_swa", "w_out", "ffn2_w_in", "ffn2_w_out")
SMALL = ("rel_bias_table", "ffn1_norm", "mix_norm", "forget_bias", "fox_q_norm", "fox_k_norm",
         "swa_q_norm", "swa_k_norm", "swa_sinks", "ffn2_norm")
WEIGHTS = ("meta_tokens", "rel_bias_table", "ffn1_norm", "ffn1_w_in", "ffn1_w_out", "mix_norm", "w_in",
           "forget_bias", "fox_q_norm", "fox_k_norm", "swa_q_norm", "swa_k_norm", "swa_sinks", "w_branch_fox",
           "w_branch_swa", "w_out", "ffn2_norm", "ffn2_w_in", "ffn2_w_out")


def _pack(arrs, width, row_multiple, dtype):
    lead = arrs[0].shape[:-1]
    flat = jnp.concatenate([a.astype(dtype) for a in arrs], axis=-1)
    n = flat.shape[-1]
    rows = -(-n // width)
    rows = -(-rows // row_multiple) * row_multiple
    flat = jnp.pad(flat, [(0, 0)] * len(lead) + [(0, rows * width - n)])
    return flat.reshape(lead + (rows, width))


def _unpack(flat, shapes):
    flat = flat.reshape(-1)
    out, off = [], 0
    for s in shapes:
        n = int(np.prod(s))
        out.append(flat[off:off + n].reshape(s))
        off += n
    return out


def _swa_head_order():
    return [4 * (j % 2) + j // 2 for j in range(8)]


def _permute_heads(a, axis, inverse=False):
    order = _swa_head_order()
    if inverse:
        order = [order.index(hd) for hd in range(8)]
    parts = [lax.slice_in_dim(a, hd * HEAD_DIM, (hd + 1) * HEAD_DIM, axis=axis) for hd in order]
    return jnp.concatenate(parts, axis=axis)


def _w_in_segments(cols):
    d = cols.d
    segs = [(512 * i + HEAD_DIM * hd, HEAD_DIM, new + LANES * hd)
            for i, new in enumerate((cols.qa, cols.ka, cols.va)) for hd in range(8)]
    segs.append((1536, 8, cols.fa))
    order = _swa_head_order()
    segs += [(1544 + HEAD_DIM * hd, HEAD_DIM, cols.qb + HEAD_DIM * order.index(hd)) for hd in range(8)]
    segs += [(2056, 128, cols.kb), (2184, 128, cols.vb), (2312, d, cols.ga), (2312 + d, d, cols.gb)]
    return segs


def _reorder_w_in(blocks, cols):
    width = blocks[0].shape[1]
    zeros = lambda n: jnp.zeros((blocks[0].shape[0], n), blocks[0].dtype)
    parts, at = [], 0
    for old, length, new in sorted(_w_in_segments(cols), key=lambda s: s[2]):
        if new > at:
            parts.append(zeros(new - at))
        at = new + length
        while length:
            j, off = divmod(old, width)
            take = min(length, width - off)
            parts.append(blocks[j][:, off:off + take])
            old, length = old + take, length - take
    parts.append(zeros(cols.np - at))
    return jnp.concatenate(parts, axis=1)


def _restore_w_in(wp, cols, width):
    segs = sorted(_w_in_segments(cols))
    blocks = []
    for j in range(N_DEV):
        lo, hi = j * width, (j + 1) * width
        parts = []
        for old, length, new in segs:
            a, b = max(old, lo), min(old + length, hi)
            if a < b:
                parts.append(wp[:, new + a - old:new + b - old])
        blocks.append(jnp.concatenate(parts, axis=1))
    return jnp.stack(blocks)


def _lane_pad(v):
    return jnp.pad(v, ((0, 0), (0, LANES - v.shape[1])))


def kernel(x, meta_tokens, rel_bias_table, ffn1_norm, ffn1_w_in, ffn1_w_out, mix_norm, w_in, forget_bias, fox_q_norm, fox_k_norm, swa_q_norm, swa_k_norm, swa_sinks, w_branch_fox, w_branch_swa, w_out, ffn2_norm, ffn2_w_in, ffn2_w_out, loss_target, m_meta_tokens, m_rel_bias_table, m_ffn1_norm, m_ffn1_w_in, m_ffn1_w_out, m_mix_norm, m_w_in, m_forget_bias, m_fox_q_norm, m_fox_k_norm, m_swa_q_norm, m_swa_k_norm, m_swa_sinks, m_w_branch_fox, m_w_branch_swa, m_w_out, m_ffn2_norm, m_ffn2_w_in, m_ffn2_w_out, v_meta_tokens, v_rel_bias_table, v_ffn1_norm, v_ffn1_w_in, v_ffn1_w_out, v_mix_norm, v_w_in, v_forget_bias, v_fox_q_norm, v_fox_k_norm, v_swa_q_norm, v_swa_k_norm, v_swa_sinks, v_w_branch_fox, v_w_branch_swa, v_w_out, v_ffn2_norm, v_ffn2_w_in, v_ffn2_w_out):
    args = dict(locals())
    wts = {n: args[n] for n in WEIGHTS}
    mom1 = {n: args["m_" + n] for n in WEIGHTS}
    mom2 = {n: args["v_" + n] for n in WEIGHTS}

    seq, d = x.shape[1], x.shape[2]
    m_rows = seq + LANES
    depth = ffn1_norm.shape[0]
    fb = ffn1_w_in.shape[2]
    fo = ffn1_w_out.shape[1]
    din_shard = w_in.shape[2]
    cols = _Cols(d)
    scale = HEAD_DIM ** -0.5
    dev = 4 * lax.axis_index("x") + 2 * lax.axis_index("y") + lax.axis_index("c")

    groups = {"ffn1": ("ffn1_w_in", "ffn1_w_out"), "mix": ("w_in", "w_branch_fox", "w_branch_swa", "w_out"),
              "ffn2": ("ffn2_w_in", "ffn2_w_out"), "ffn1_in": ("ffn1_w_in",), "ffn1_out": ("ffn1_w_out",),
              "ffn2_in": ("ffn2_w_in",), "ffn2_out": ("ffn2_w_out",)}
    shard = {n: wts[n].astype(BF16) for n in BIG}
    full, parts, gw = {}, {}, {}

    def keys_of(stages):
        return [(n, l) for g, l in stages if l < depth for n in groups[g]]

    def gather_rider(stages):
        return Rider([shard[n][l] for n, l in keys_of(stages)], True)

    def scatter_rider(stages):
        return Rider([gw[k] for k in keys_of(stages)], False)

    def ffn_weights(tag, l):
        return full[tag + "_w_in", l], full[tag + "_w_out", l].reshape(4, fb, d)

    def mixer_weights(l):
        wp = _reorder_w_in([full["w_in", l][j] for j in range(N_DEV)], cols)
        wbf = jnp.concatenate([full["w_branch_fox", l][j] for j in range(N_DEV)], axis=1)
        wbf = jnp.pad(wbf.reshape(8, HEAD_DIM, d), ((0, 0), (0, LANES - HEAD_DIM), (0, 0))).reshape(8 * LANES, d)
        wbs = _permute_heads(jnp.concatenate([full["w_branch_swa", l][j] for j in range(N_DEV)], axis=1), 0)
        return wp, wbf, wbs, full["w_out", l].reshape(d, d)

    full.update(zip(keys_of([("ffn1", 0)]), exchange_hbm(gather_rider([("ffn1", 0)]).srcs, True, "gather_first")))
    meta_all = gather_small(meta_tokens.reshape(1, N_META, -1), "gather_meta")
    meta_full = meta_all.transpose(1, 0, 2).reshape(N_META, d)
    tile8 = lambda g, s=1.0: jnp.tile(g.reshape(1, HEAD_DIM) * s, (1, 8))
    tile2 = lambda g: jnp.tile(g.reshape(1, HEAD_DIM), (1, 2))
    data_lanes = lambda g, s=1.0: _lane_pad(g.reshape(1, HEAD_DIM) * s)
    bias = bias_build(rel_bias_table, "swa_bias")

    first = jnp.concatenate([jnp.zeros((PAD_FRONT, d), F32), meta_full], axis=0)
    h = jnp.concatenate([first, x[0]], axis=0)
    saved, lw = [], []
    for l in range(depth):
        s, w = {"h0": h}, {}
        w["ffn1_in"], w["ffn1_out"] = ffn_weights("ffn1", l)
        stages = [("mix", l)]
        (h, s["n1"], s["a1"], s["fg1"], s["fu1"]), got = ffn_fwd(h, ffn1_norm[l:l + 1], w["ffn1_in"], w["ffn1_out"],
                                                          f"ffn1_fwd_{l}", gather_rider(stages))
        full.update(zip(keys_of(stages), got))
        s["h1"] = h
        w["wp"], w["wbf"], w["wbs"], w["wo"] = mixer_weights(l)
        s["nm"], s["proj"] = mixer_proj(h, mix_norm[l:l + 1], w["wp"], f"mixer_proj_{l}")
        s["gains"] = (data_lanes(fox_q_norm[l], scale), data_lanes(fox_k_norm[l]), tile8(swa_q_norm[l], scale),
                      tile2(swa_k_norm[l]))
        s["fbias"] = _lane_pad(forget_bias[l:l + 1])
        qf, kf, vf, qb, kb, vb = qk_post(s["proj"], s["gains"], s["fbias"], cols, f"qk_post_{l}")
        s.update(qf=qf, kf=kf, vf=vf, qb=qb, kb=kb, vb=vb)
        stages = [("ffn2", l), ("ffn1_in", l + 1)]
        (s["o_fox"], s["lse_fox"]), got = fox_fwd(qf, kf, vf.T, f"fox_fwd_{l}", gather_rider(stages))
        full.update(zip(keys_of(stages), got))
        s["o_swa"], s["lse_swa"] = swa_fwd(qb, kb, vb, bias, swa_sinks[l], f"swa_fwd_{l}")
        h = branch_out(h, s["o_fox"], s["o_swa"], s["proj"], w["wbf"], w["wbs"], w["wo"], cols, f"branch_out_{l}")
        s["h2"] = h
        w["ffn2_in"], w["ffn2_out"] = ffn_weights("ffn2", l)
        stages = [("ffn1_out", l + 1)]
        (h, s["n2"], s["a2"], s["fg2"], s["fu2"]), got = ffn_fwd(h, ffn2_norm[l:l + 1], w["ffn2_in"], w["ffn2_out"],
                                                          f"ffn2_fwd_{l}", gather_rider(stages))
        full.update(zip(keys_of(stages), got))
        saved.append(s)
        lw.append(w)

    dh, loss_part = loss_head(h, loss_target[0], "loss_head")

    gs = {n: [None] * depth for n in SMALL}
    dbias_total = None
    for l in reversed(range(depth)):
        w, s = lw[l], saved[l]

        def ffn_back(dh, tag, hin, norm, n_in, a, f_gate, f_up, stages):
            (dh_in, dg, du, dgn, dhs), got = ffn_bwd(dh, hin, norm, f_gate, f_up, w[tag + "_in"], w[tag + "_out"],
                                                     f"{tag}_bwd_{l}", scatter_rider(stages))
            parts.update(zip(keys_of(stages), got))
            gw[tag + "_w_out", l] = matmul_tn(a, dhs[None], f"{tag}_dwo_{l}").reshape(N_DEV, fo, d)
            stages = [(tag + "_out", l)]
            gw[tag + "_w_in", l], got = matmul_tn(n_in[None], dg, f"{tag}_dwi_{l}", y2=du,
                                                  rider=scatter_rider(stages))
            parts.update(zip(keys_of(stages), got))
            return dh_in, dgn

        dh, gs["ffn2_norm"][l] = ffn_back(dh, "ffn2", s["h2"], ffn2_norm[l:l + 1], s["n2"], s["a2"], s["fg2"],
                                          s["fu2"], [("ffn1_in", l + 1)])

        y, dtf, dts, dga, dgb, dof, dos, delta = branch_out_bwd(dh, s["o_fox"], s["o_swa"], s["proj"], w["wbf"],
                                                                w["wbs"], w["wo"], cols, f"branch_out_bwd_{l}")
        gw["w_out", l] = matmul_tn(y[None], dh[None], f"dw_out_{l}").reshape(N_DEV, d // N_DEV, d)
        to_shards = lambda a: a.reshape(512, N_DEV, d // N_DEV).transpose(1, 0, 2)
        gw["w_branch_fox", l] = to_shards(matmul_tn(s["o_fox"][None], dtf[None], f"dw_branch_fox_{l}")[0]
                                          .reshape(8, LANES, d)[:, :HEAD_DIM].reshape(512, d))
        gw["w_branch_swa", l] = to_shards(_permute_heads(
            matmul_tn(s["o_swa"][None], dts[None], f"dw_branch_swa_{l}")[0], 0, inverse=True))

        stages = [("ffn2_in", l)]
        (dkf, dvf, dqf, dc_rows), got = fox_bwd(s["qf"], s["kf"], s["vf"], s["kf"].T, dof, s["lse_fox"],
                                         delta[:, :8].T.reshape(8, 1, m_rows), f"fox_bwd_{l}", scatter_rider(stages))
        parts.update(zip(keys_of(stages), got))
        dc = _lane_pad(dc_rows.reshape(8, m_rows).T)
        dqb, dkb, dvb, dbias, dsink = swa_bwd(s["qb"], s["kb"], s["vb"], bias, swa_sinks[l], s["o_swa"], s["lse_swa"],
                                              dos, f"swa_bwd_{l}")
        dbias_total = dbias if dbias_total is None else dbias_total + dbias
        gs["swa_sinks"][l] = dsink[0, :8]
        dproj, ggqa, ggka, ggqb, ggkb, gfb = qk_post_bwd(s["proj"], s["gains"], s["fbias"], dqf, dkf, dvf, dqb, dkb,
                                                         dvb, dc, dga, dgb, cols, f"qk_post_bwd_{l}")
        gs["fox_q_norm"][l] = ggqa[0, :HEAD_DIM] * scale
        gs["fox_k_norm"][l] = ggka[0, :HEAD_DIM]
        gs["swa_q_norm"][l] = ggqb.reshape(8, HEAD_DIM).sum(0) * scale
        gs["swa_k_norm"][l] = ggkb.reshape(2, HEAD_DIM).sum(0)
        gs["forget_bias"][l] = gfb[0, :8]
        dwp = matmul_tn(s["nm"][None], dproj[None], f"dw_in_{l}", tn=1024 if cols.np % 1024 == 0 else None)[0]
        gw["w_in", l] = _restore_w_in(dwp, cols, din_shard)
        dh, gs["mix_norm"][l] = dproj_bwd(dh, s["h1"], mix_norm[l:l + 1], dproj, w["wp"], f"dproj_bwd_{l}")

        dh, gs["ffn1_norm"][l] = ffn_back(dh, "ffn1", s["h0"], ffn1_norm[l:l + 1], s["n1"], s["a1"], s["fg1"],
                                          s["fu1"], [("mix", l)])

    grad_x = dh[LANES:][None]
    dmeta = dh[PAD_FRONT:LANES]
    dtable = bias_reduce(dbias_total, "swa_dbias")[:, :8]

    parts.update(zip(keys_of([("ffn1_in", 0)]),
                     exchange_hbm(scatter_rider([("ffn1_in", 0)]).srcs, False, "scatter_last")))
    big_out = [{}, {}, {}, {}]
    for n in BIG:
        outs = adamw_sum([parts[n, l] for l in range(depth)], wts[n], mom1[n], mom2[n], f"adamw_{n}")
        for k in range(4):
            big_out[k][n] = outs[k]

    small_g = {n: (jnp.stack(gs[n]) if n != "rel_bias_table" else None) for n in SMALL}
    small_g["rel_bias_table"] = dtable
    pieces = [loss_part[0:1, 0:1].reshape(1, 1)] + [small_g[n].reshape(1, -1) for n in SMALL] + [dmeta.reshape(1, -1)]
    small_shapes = [(1,)] + [wts[n].shape for n in SMALL] + [(N_META, d)]
    total = allsum_small(_pack(pieces, LANES, 8, F32), "allsum_small")
    summed = _unpack(total, small_shapes)
    loss = summed[0][0]
    g_small = dict(zip(SMALL, summed[1:1 + len(SMALL)]))
    g_meta = lax.dynamic_slice_in_dim(summed[-1], dev * (d // N_DEV), d // N_DEV, axis=1)
    names = SMALL + ("meta_tokens",)
    g_small["meta_tokens"] = g_meta
    pk = lambda src: _pack([src[n].reshape(1, -1) for n in names], LANES, 8, F32)[0]
    small_out = [dict(zip(names, _unpack(o, [wts[n].shape for n in names])))
                 for o in adamw_small(pk(g_small), pk(wts), pk(mom1), pk(mom2), "adamw_small")]

    grads = {**big_out[0], **g_small}
    delta = {**big_out[1], **small_out[0]}
    new_m = {**big_out[2], **small_out[1]}
    new_v = {**big_out[3], **small_out[2]}
    return (loss, grad_x, *[grads[n] for n in WEIGHTS], *[delta[n] for n in WEIGHTS],
            *[new_m[n] for n in WEIGHTS], *[new_v[n] for n in WEIGHTS])
```

```python
import math

import numpy as np
import jax
import jax.numpy as jnp
from jax import lax
from jax.experimental import pallas as pl
from jax.experimental.pallas import tpu as pltpu

F32 = jnp.float32
BF16 = jnp.bfloat16
EPS = 1e-6
NEG = -1e30
HEAD_DIM = 64
LANES = 128
N_META = 16
PAD_FRONT = LANES - N_META
N_BUCKETS = 32
MAX_DISTANCE = 128
N_DEV = 8
ADAM_LR, ADAM_B1, ADAM_B2, ADAM_EPS, ADAM_WD, ADAM_STEP = 0.001, 0.9, 0.999, 1e-08, 0.01, 10
VMEM_LIMIT = 56 * 1024 * 1024
MESH = pl.DeviceIdType.MESH


def _params(n_grid):
    return pltpu.CompilerParams(dimension_semantics=("arbitrary",) * n_grid,
                                vmem_limit_bytes=VMEM_LIMIT)


def _dot(a, b):
    return jnp.dot(a, b, preferred_element_type=F32)


def _dot_nt(a, b):
    return lax.dot_general(a, b, (((1,), (1,)), ((), ())), preferred_element_type=F32)


def _dot_tn(a, b):
    return lax.dot_general(a, b, (((0,), (0,)), ((), ())), preferred_element_type=F32)


def _rms(x):
    r = lax.rsqrt(jnp.mean(x * x, axis=-1, keepdims=True) + EPS)
    return x * r, r


def _rms_bwd(x, g, dn):
    xh, r = _rms(x)
    dxh = dn * g
    dx = r * (dxh - xh * jnp.mean(dxh * xh, axis=-1, keepdims=True))
    return dx, jnp.sum(dn * xh, axis=0, keepdims=True)


def _split2(v):
    hi = v.astype(BF16)
    return hi, (v - hi.astype(F32)).astype(BF16)


def _split3(v):
    hi = v.astype(BF16)
    r1 = v - hi.astype(F32)
    mid = r1.astype(BF16)
    return hi, mid, (r1 - mid.astype(F32)).astype(BF16)


def _group_ones():
    r = lax.broadcasted_iota(jnp.int32, (LANES, LANES), 0) // HEAD_DIM
    c = lax.broadcasted_iota(jnp.int32, (LANES, LANES), 1) // HEAD_DIM
    return jnp.where(r == c, 1.0, 0.0).astype(BF16)


def _group_mean(v, ones):
    hi, lo = _split2(v)
    return (_dot(hi, ones) + _dot(lo, ones)) * (1.0 / HEAD_DIM)


def _row_tile(m):
    return 384 if m % 384 == 0 else LANES


def _tile(m, cap):
    return max(t for t in range(16, cap + 1, 16) if m % t == 0)


def _peer(k):
    x, y, c = lax.axis_index("x"), lax.axis_index("y"), lax.axis_index("c")
    px = 1 - x if k & 4 else x
    py = 1 - y if k & 2 else y
    pc = 1 - c if k & 1 else c
    return (px, py, pc), 4 * px + 2 * py + pc


def _exchange_body(src_ref, dst_ref, send_sems, recv_sems, local_sem, bcast):
    x, y, c = lax.axis_index("x"), lax.axis_index("y"), lax.axis_index("c")
    me = 4 * x + 2 * y + c
    mine = pltpu.make_async_copy(src_ref.at[0 if bcast else me], dst_ref.at[me], local_sem)
    mine.start()
    sends = []
    for k in range(1, N_DEV):
        dev, idx = _peer(k)
        cp = pltpu.make_async_remote_copy(
            src_ref=src_ref.at[0 if bcast else idx], dst_ref=dst_ref.at[me],
            send_sem=send_sems.at[k - 1], recv_sem=recv_sems.at[k - 1],
            device_id=dev, device_id_type=MESH)
        cp.start()
        sends.append(cp)
    for k in range(1, N_DEV):
        dev, idx = _peer(k)
        pltpu.make_async_remote_copy(
            src_ref=src_ref.at[0], dst_ref=dst_ref.at[idx],
            send_sem=send_sems.at[k - 1], recv_sem=recv_sems.at[k - 1],
            device_id=dev, device_id_type=MESH).wait_recv()
    for cp in sends:
        cp.wait_send()
    mine.wait()


class Rider:
    FIRST = (1, 2, 4, 6)
    RELAYED = (2, 4, 6)

    def __init__(self, srcs=(), bcast=True):
        self.srcs, self.bcast, self.n = list(srcs), bcast, len(srcs)

    def out_shapes(self):
        return [jax.ShapeDtypeStruct(((N_DEV,) + s.shape) if self.bcast else s.shape, s.dtype) for s in self.srcs]

    def specs(self):
        return [pl.BlockSpec(memory_space=pl.ANY)] * self.n

    def scratch(self):
        if not self.n:
            return []
        return [pltpu.SemaphoreType.DMA((self.n * (N_DEV - 1),)), pltpu.SemaphoreType.DMA((self.n * (N_DEV - 1),)),
                pltpu.SemaphoreType.DMA((self.n,))]

    @staticmethod
    def _copy(src, dst, a, pair, dev, send_sems, recv_sems):
        sem = a * (N_DEV - 1) + pair - 1
        return pltpu.make_async_remote_copy(src_ref=src, dst_ref=dst, send_sem=send_sems.at[sem],
                                            recv_sem=recv_sems.at[sem], device_id=dev, device_id_type=MESH)

    def _first(self):
        return self.FIRST if self.bcast else range(1, N_DEV)

    def _own(self, s, d, a, local_sems):
        me = 4 * lax.axis_index("x") + 2 * lax.axis_index("y") + lax.axis_index("c")
        return pltpu.make_async_copy(s if self.bcast else s.at[me], d.at[me], local_sems.at[a]), me

    def start(self, src_refs, dst_refs, send_sems, recv_sems, local_sems):
        for a, (s, d) in enumerate(zip(src_refs, dst_refs)):
            own, me = self._own(s, d, a, local_sems)
            own.start()
            for k in self._first():
                dev, idx = _peer(k)
                self._copy(s if self.bcast else s.at[idx], d.at[me], a, k, dev, send_sems, recv_sems).start()

    def relay(self, src_refs, dst_refs, send_sems, recv_sems, local_sems):
        if not self.bcast:
            return
        sibling, _ = _peer(1)
        for a, d in enumerate(dst_refs):
            for k in self.RELAYED:
                dev, idx = _peer(k)
                self._copy(d.at[idx], d.at[idx], a, k, dev, send_sems, recv_sems).wait_recv()
                self._copy(d.at[idx], d.at[idx], a, k + 1, sibling, send_sems, recv_sems).start()

    def wait(self, src_refs, dst_refs, send_sems, recv_sems, local_sems):
        sibling, _ = _peer(1)
        for a, (s, d) in enumerate(zip(src_refs, dst_refs)):
            own, me = self._own(s, d, a, local_sems)
            for k in range(1, N_DEV):
                if not (self.bcast and k in self.RELAYED):
                    dev, idx = _peer(k)
                    self._copy(d.at[idx], d.at[idx], a, k, dev, send_sems, recv_sems).wait_recv()
            for k in self._first():
                dev, idx = _peer(k)
                self._copy(s if self.bcast else s.at[idx], d.at[me], a, k, dev, send_sems, recv_sems).wait_send()
            if self.bcast:
                for k in self.RELAYED:
                    dev, idx = _peer(k)
                    self._copy(d.at[idx], d.at[idx], a, k + 1, sibling, send_sems, recv_sems).wait_send()
            own.wait()


def rider_call(core, name, grid, in_specs, out_specs, out_shape, scratch_shapes, args, rider=None):
    rider = rider or Rider()
    n_in, n_out, n_scr, nr = len(in_specs), len(out_specs), len(scratch_shapes), rider.n

    def body(*refs):
        ins, r_src = refs[:n_in], refs[n_in:n_in + nr]
        outs = refs[n_in + nr:n_in + nr + n_out]
        r_dst = refs[n_in + nr + n_out:n_in + 2 * nr + n_out]
        scr = refs[n_in + 2 * nr + n_out:n_in + 2 * nr + n_out + n_scr]
        sems = refs[n_in + 2 * nr + n_out + n_scr:]
        if nr:
            first, relay, last = True, True, True
            for ax, size in enumerate(grid):
                first = first & (pl.program_id(ax) == 0)
                relay = relay & (pl.program_id(ax) == (3 * size // 4 if ax == 0 else 0))
                last = last & (pl.program_id(ax) == size - 1)
            if not grid:
                rider.start(r_src, r_dst, *sems)
                rider.relay(r_src, r_dst, *sems)
            else:
                pl.when(first)(lambda: rider.start(r_src, r_dst, *sems))
                if rider.bcast:
                    pl.when(relay)(lambda: rider.relay(r_src, r_dst, *sems))
        core(*ins, *outs, *scr)
        if nr:
            if not grid:
                rider.wait(r_src, r_dst, *sems)
            else:
                pl.when(last)(lambda: rider.wait(r_src, r_dst, *sems))

    res = pl.pallas_call(
        body, name=name, grid=grid,
        in_specs=list(in_specs) + rider.specs(),
        out_specs=list(out_specs) + rider.specs(),
        out_shape=list(out_shape) + rider.out_shapes(),
        scratch_shapes=list(scratch_shapes) + rider.scratch(),
        compiler_params=_params(len(grid)),
    )(*args, *rider.srcs)
    return res[:n_out], res[n_out:]


def exchange_hbm(srcs, bcast, name):
    return rider_call(lambda: None, name, (), [], [], [], [], [], Rider(srcs, bcast))[1]


_HBM = pl.BlockSpec(memory_space=pltpu.HBM)
_SEM = pl.BlockSpec(memory_space=pltpu.SEMAPHORE)
_EFFECT = pltpu.CompilerParams(has_side_effects=pltpu.SideEffectType.DATAFLOW_SIDE_EFFECTING)


def scatter_start(src, name):
    def body(src_ref, land_ref, send_sems, recv_sems, src_thru, land_thru, token):
        me = 4 * lax.axis_index("x") + 2 * lax.axis_index("y") + lax.axis_index("c")
        for k in range(1, N_DEV):
            dev, idx = _peer(k)
            pltpu.make_async_remote_copy(src_ref=src_ref.at[idx], dst_ref=land_ref.at[me], send_sem=send_sems.at[k - 1],
                                         recv_sem=recv_sems.at[k - 1], device_id=dev, device_id_type=MESH).start()
        token[...] = jnp.zeros_like(token)

    return pl.pallas_call(
        body, name=name,
        out_shape=(pltpu.SemaphoreType.DMA((N_DEV - 1,)), pltpu.SemaphoreType.DMA((N_DEV - 1,)),
                   pltpu.HBM(src.shape, src.dtype), pltpu.HBM(src.shape, src.dtype), jax.ShapeDtypeStruct((8, LANES), F32)),
        in_specs=(_HBM, _HBM), out_specs=(_SEM, _SEM, _HBM, _HBM, pl.BlockSpec(memory_space=pltpu.VMEM)),
        input_output_aliases={0: 2, 1: 3}, compiler_params=_EFFECT,
    )(pltpu.with_memory_space_constraint(src, pltpu.HBM),
      pltpu.with_memory_space_constraint(lax.empty(src.shape, src.dtype), pltpu.HBM))


def scatter_wait(send_sems, recv_sems, src_thru, land_thru, after, name):
    n_after = len(after)

    def body(*refs):
        src_ref, land_ref, send_sems, recv_sems = refs[:4]
        for k in range(1, N_DEV):
            dev, idx = _peer(k)
            copy = pltpu.make_async_remote_copy(src_ref=src_ref.at[idx], dst_ref=land_ref.at[idx],
                                                send_sem=send_sems.at[k - 1], recv_sem=recv_sems.at[k - 1],
                                                device_id=dev, device_id_type=MESH)
            copy.wait_send()
            copy.wait_recv()

    return pl.pallas_call(
        body, name=name,
        out_shape=(pltpu.HBM(src_thru.shape, src_thru.dtype), pltpu.HBM(land_thru.shape, land_thru.dtype)),
        in_specs=(_HBM, _HBM, _SEM, _SEM) + (pl.BlockSpec(memory_space=pl.ANY),) * n_after, out_specs=(_HBM, _HBM),
        input_output_aliases={0: 0, 1: 1}, compiler_params=_EFFECT,
    )(src_thru, land_thru, send_sems, recv_sems, *after)


def allsum_small(vec, name):
    def body(src_ref, out_ref, dst_ref, send_sems, recv_sems, local_sem):
        _exchange_body(src_ref, dst_ref, send_sems, recv_sems, local_sem, True)
        acc = dst_ref[0]
        for j in range(1, N_DEV):
            acc = acc + dst_ref[j]
        out_ref[...] = acc

    return pl.pallas_call(
        body, name=name,
        out_shape=jax.ShapeDtypeStruct(vec.shape[1:], F32),
        in_specs=[pl.BlockSpec(memory_space=pltpu.VMEM)],
        out_specs=pl.BlockSpec(memory_space=pltpu.VMEM),
        scratch_shapes=[pltpu.VMEM((N_DEV,) + vec.shape[1:], F32),
                        pltpu.SemaphoreType.DMA((N_DEV - 1,)), pltpu.SemaphoreType.DMA((N_DEV - 1,)),
                        pltpu.SemaphoreType.DMA],
    )(vec)


def gather_small(vec, name):
    def body(src_ref, dst_ref, send_sems, recv_sems, local_sem):
        _exchange_body(src_ref, dst_ref, send_sems, recv_sems, local_sem, True)

    return pl.pallas_call(
        body, name=name,
        out_shape=jax.ShapeDtypeStruct((N_DEV,) + vec.shape[1:], F32),
        in_specs=[pl.BlockSpec(memory_space=pltpu.VMEM)],
        out_specs=pl.BlockSpec(memory_space=pltpu.VMEM),
        scratch_shapes=[pltpu.SemaphoreType.DMA((N_DEV - 1,)), pltpu.SemaphoreType.DMA((N_DEV - 1,)),
                        pltpu.SemaphoreType.DMA],
    )(vec)


FFN_FWD_ROWS = 1056
FFN_BWD_ROWS = 704
DW_ROWS = 1408

def ffn_fwd(h, g, w_in8, w_out4, name, rider=None):
    m, d = h.shape
    fb = w_in8.shape[2]
    tm = _tile(m, FFN_FWD_ROWS)

    def body(h_ref, g_ref, wg_ref, wu_ref, wo_ref, hn_ref, n_ref, a_ref, fg_ref, fu_ref, acc_ref):
        i = pl.program_id(1)

        @pl.when(i == 0)
        def _():
            xh, _ = _rms(h_ref[...])
            n_ref[...] = (xh * g_ref[...]).astype(BF16)
            acc_ref[...] = jnp.zeros_like(acc_ref)

        n = n_ref[...]
        gate = _dot(n, wg_ref[0])
        up = _dot(n, wu_ref[0])
        sg = jax.nn.sigmoid(gate)
        silu = gate * sg
        a = (silu * up).astype(BF16)
        a_ref[0] = a
        fg_ref[0] = (up * (sg * (1.0 + gate * (1.0 - sg)))).astype(BF16)
        fu_ref[0] = silu.astype(BF16)
        acc_ref[...] += _dot(a, wo_ref[0])

        @pl.when(i == 3)
        def _():
            hn_ref[...] = h_ref[...] + 0.5 * acc_ref[...]

    return rider_call(
        body, name, (m // tm, 4),
        in_specs=[pl.BlockSpec((tm, d), lambda r, i: (r, 0)),
                  pl.BlockSpec((1, d), lambda r, i: (0, 0)),
                  pl.BlockSpec((1, d, fb), lambda r, i: (i, 0, 0)),
                  pl.BlockSpec((1, d, fb), lambda r, i: (i + 4, 0, 0)),
                  pl.BlockSpec((1, fb, d), lambda r, i: (i, 0, 0))],
        out_specs=[pl.BlockSpec((tm, d), lambda r, i: (r, 0)),
                   pl.BlockSpec((tm, d), lambda r, i: (r, 0))] + [pl.BlockSpec((1, tm, fb), lambda r, i: (i, r, 0))] * 3,
        out_shape=[jax.ShapeDtypeStruct((m, d), F32), jax.ShapeDtypeStruct((m, d), BF16)]
                  + [jax.ShapeDtypeStruct((4, m, fb), BF16)] * 3,
        scratch_shapes=[pltpu.VMEM((tm, d), F32)],
        args=(h, g, w_in8, w_in8, w_out4), rider=rider)


def ffn_bwd(dh, h, g, f_gate, f_up, w_in8, w_out4, name, rider=None):
    m, d = h.shape
    fb = w_in8.shape[2]
    tm = _tile(m, FFN_BWD_ROWS)

    def body(dh_ref, h_ref, g_ref, fg_ref, fu_ref, wg_ref, wu_ref, wo_ref,
             dhin_ref, dg_ref, du_ref, dgn_ref, dhs_ref, acc_ref):
        r = pl.program_id(0)
        i = pl.program_id(1)

        @pl.when(i == 0)
        def _():
            dhs_ref[...] = (0.5 * dh_ref[...]).astype(BF16)
            acc_ref[...] = jnp.zeros_like(acc_ref)

        @pl.when((r == 0) & (i == 0))
        def _():
            dgn_ref[...] = jnp.zeros_like(dgn_ref)

        da = _dot_nt(dhs_ref[...], wo_ref[0])
        dub = (da * fu_ref[0]).astype(BF16)
        dgb = (da * fg_ref[0]).astype(BF16)
        dg_ref[0] = dgb
        du_ref[0] = dub
        acc_ref[...] += _dot_nt(dgb, wg_ref[0]) + _dot_nt(dub, wu_ref[0])

        @pl.when(i == 3)
        def _():
            dx, dgain = _rms_bwd(h_ref[...], g_ref[...], acc_ref[...])
            dgn_ref[...] += dgain
            dhin_ref[...] = dh_ref[...] + dx

    row = lambda r, i: (r, 0)
    blk = lambda r, i: (i, r, 0)
    return rider_call(
        body, name, (m // tm, 4),
        in_specs=[pl.BlockSpec((tm, d), row), pl.BlockSpec((tm, d), row),
                  pl.BlockSpec((1, d), lambda r, i: (0, 0)),
                  pl.BlockSpec((1, tm, fb), blk), pl.BlockSpec((1, tm, fb), blk),
                  pl.BlockSpec((1, d, fb), lambda r, i: (i, 0, 0)),
                  pl.BlockSpec((1, d, fb), lambda r, i: (i + 4, 0, 0)),
                  pl.BlockSpec((1, fb, d), lambda r, i: (i, 0, 0))],
        out_specs=[pl.BlockSpec((tm, d), row),
                   pl.BlockSpec((1, tm, fb), blk), pl.BlockSpec((1, tm, fb), blk),
                   pl.BlockSpec((1, d), lambda r, i: (0, 0)),
                   pl.BlockSpec((tm, d), row)],
        out_shape=[jax.ShapeDtypeStruct((m, d), F32),
                   jax.ShapeDtypeStruct((4, m, fb), BF16), jax.ShapeDtypeStruct((4, m, fb), BF16),
                   jax.ShapeDtypeStruct((1, d), F32), jax.ShapeDtypeStruct((m, d), BF16)],
        scratch_shapes=[pltpu.VMEM((tm, d), F32)],
        args=(dh, h, g, f_gate, f_up, w_in8, w_in8, w_out4), rider=rider)


def matmul_tn(x, y, name, tn=None, y2=None, rider=None):
    bx, m, k = x.shape
    by, _, n = y.shape
    b = max(bx, by) * (2 if y2 is not None else 1)
    tm = _tile(m, DW_ROWS)
    tn = n if tn is None else tn
    nt = n // tn
    nr = m // tm

    def body(*refs):
        x_ref, y_ref = refs[0], refs[1]
        o_ref, acc_ref = refs[-2], refs[-1]
        r = pl.program_id(2)

        @pl.when(r == 0)
        def _():
            acc_ref[...] = jnp.zeros_like(acc_ref)

        if y2 is None:
            acc_ref[...] += _dot_tn(x_ref[0].astype(BF16), y_ref[0].astype(BF16))
        else:
            @pl.when(pl.program_id(0) < by)
            def _():
                acc_ref[...] += _dot_tn(x_ref[0].astype(BF16), y_ref[0].astype(BF16))

            @pl.when(pl.program_id(0) >= by)
            def _():
                acc_ref[...] += _dot_tn(x_ref[0].astype(BF16), refs[2][0].astype(BF16))

        @pl.when(r == nr - 1)
        def _():
            o_ref[0] = acc_ref[...].astype(BF16)

    x_map = (lambda i, j, r: (i, r, 0)) if bx > 1 else (lambda i, j, r: (0, r, 0))
    if y2 is None:
        y_specs = [pl.BlockSpec((1, tm, tn), (lambda i, j, r: (i, r, j)) if by > 1 else (lambda i, j, r: (0, r, j)))]
    else:
        y_specs = [pl.BlockSpec((1, tm, tn), lambda i, j, r: (jnp.minimum(i, by - 1), jnp.where(i < by, r, nr - 1), j)),
                   pl.BlockSpec((1, tm, tn), lambda i, j, r: (jnp.maximum(i - by, 0), jnp.where(i < by, 0, r), j))]
    (out,), carried = rider_call(
        body, name, (b, nt, nr),
        in_specs=[pl.BlockSpec((1, tm, k), x_map)] + y_specs,
        out_specs=[pl.BlockSpec((1, k, tn), lambda i, j, r: (i, 0, j))],
        out_shape=[jax.ShapeDtypeStruct((b, k, n), BF16)],
        scratch_shapes=[pltpu.VMEM((k, tn), F32)],
        args=[x, y] + ([y2] if y2 is not None else []), rider=rider)
    return (out, carried) if rider is not None else out


AUG = HEAD_DIM


class _Cols:
    def __init__(self, d):
        self.d = d
        self.ga, self.gb = 0, d
        self.qa, self.ka, self.va = 2 * d, 2 * d + 1024, 2 * d + 2048
        self.qb = 2 * d + 3072
        self.kb, self.vb, self.fa = self.qb + 512, self.qb + 640, self.qb + 768
        self.np = self.qb + 1024


def mixer_proj(h, g, wp, name):
    m, d = h.shape
    npad = wp.shape[1]
    tm = _row_tile(m)

    def body(h_ref, g_ref, w_ref, n_ref, p_ref):
        xh, _ = _rms(h_ref[...])
        n = (xh * g_ref[...]).astype(BF16)
        n_ref[...] = n
        p_ref[...] = _dot(n, w_ref[...])

    return pl.pallas_call(
        body, name=name, grid=(m // tm,),
        in_specs=[pl.BlockSpec((tm, d), lambda r: (r, 0)), pl.BlockSpec((1, d), lambda r: (0, 0)),
                  pl.BlockSpec((d, npad), lambda r: (0, 0))],
        out_specs=[pl.BlockSpec((tm, d), lambda r: (r, 0)), pl.BlockSpec((tm, npad), lambda r: (r, 0))],
        out_shape=[jax.ShapeDtypeStruct((m, d), BF16), jax.ShapeDtypeStruct((m, npad), F32)],
        compiler_params=_params(1),
    )(h, g, wp)


def _head_norm(x, gain, ones):
    outs = []
    for b in range(x.shape[1] // LANES):
        xb = x[:, b * LANES:(b + 1) * LANES]
        r = lax.rsqrt(_group_mean(xb * xb, ones) + EPS)
        outs.append(xb * r * gain[:, b * LANES:(b + 1) * LANES])
    return outs


def _head_norm_bwd(x, gain, dn, ones):
    dxs, dgs = [], []
    for b in range(x.shape[1] // LANES):
        sl = slice(b * LANES, (b + 1) * LANES)
        xb, dnb = x[:, sl], dn[:, sl]
        r = lax.rsqrt(_group_mean(xb * xb, ones) + EPS)
        xh = xb * r
        dxh = dnb * gain[:, sl]
        dxs.append(r * (dxh - xh * _group_mean(dxh * xh, ones)))
        dgs.append(jnp.sum(dnb * xh, axis=0, keepdims=True))
    return dxs, dgs


def _lane_col(v, lane_iota, idx):
    return jnp.sum(jnp.where(lane_iota == idx, v, 0.0), axis=1, keepdims=True)


def _aug(base, lane, vals):
    for i, v in enumerate(vals):
        base = jnp.where(lane == AUG + i, v, base)
    return base


def qk_post(proj, gains, fbias, cols, name):
    m = proj.shape[0]
    tm = _row_tile(m)
    gqa, gka, gqb, gkb = gains

    def body(qa_ref, ka_ref, va_ref, qb_ref, kb_ref, vb_ref, fa_ref, gqa_ref, gka_ref, gqb_ref, gkb_ref, fb_ref,
             qf_o, kf_o, vf_o, qb_o, kb_o, vb_o, carry_ref):
        r0 = pl.program_id(0)

        @pl.when(r0 == 0)
        def _():
            carry_ref[...] = jnp.zeros_like(carry_ref)

        z = fa_ref[...] + fb_ref[...]
        logf = jnp.minimum(z, 0.0) - jnp.log(1.0 + jnp.exp(-jnp.abs(z)))
        rr = lax.broadcasted_iota(jnp.int32, (tm, tm), 0)
        cc = lax.broadcasted_iota(jnp.int32, (tm, tm), 1)
        tril = jnp.where(cc <= rr, 1.0, 0.0).astype(BF16)
        p0, p1, p2 = _split3(logf)
        c = _dot(tril, p0) + _dot(tril, p1) + _dot(tril, p2) + carry_ref[...]
        carry_ref[...] += jnp.sum(logf, axis=0, keepdims=True)

        lane = lax.broadcasted_iota(jnp.int32, (tm, LANES), 1)
        is_pad = (r0 * tm + lax.broadcasted_iota(jnp.int32, (tm, 1), 0)) < PAD_FRONT
        ones = jnp.ones((LANES, LANES), BF16)
        for hd in range(8):
            sl = slice(hd * LANES, (hd + 1) * LANES)
            ch = _lane_col(c, lane, hd)
            ct = [p.astype(F32) for p in _split3(ch)]
            cs = [p.astype(F32) for p in _split3(-jnp.where(is_pad, -NEG, ch))]
            xq = qa_ref[:, sl]
            qn = xq * lax.rsqrt(_group_mean(xq * xq, ones) + EPS) * gqa_ref[...]
            qf_o[:, sl] = _aug(qn, lane, ct + [1.0, 1.0, 1.0]).astype(BF16)
            xk = ka_ref[:, sl]
            kn = xk * lax.rsqrt(_group_mean(xk * xk, ones) + EPS) * gka_ref[...]
            kf_o[:, sl] = _aug(kn, lane, [1.0, 1.0, 1.0] + cs).astype(BF16)
            vf_o[:, sl] = _aug(va_ref[:, sl], lane, [1.0, 1.0, 1.0]).astype(BF16)

        gones = _group_ones()
        for src, gn, dst in ((qb_ref, gqb_ref, qb_o), (kb_ref, gkb_ref, kb_o)):
            for b, blk in enumerate(_head_norm(src[...], gn[...], gones)):
                dst[:, b * LANES:(b + 1) * LANES] = blk.astype(BF16)
        vb_o[...] = vb_ref[...].astype(BF16)

    w1024 = lambda off: pl.BlockSpec((tm, 1024), lambda r, o=off // 1024: (r, o))
    w512 = lambda off: pl.BlockSpec((tm, 512), lambda r, o=off // 512: (r, o))
    w128 = lambda off: pl.BlockSpec((tm, LANES), lambda r, o=off // LANES: (r, o))
    vec = lambda w: pl.BlockSpec((1, w), lambda r: (0, 0))
    row = lambda w: pl.BlockSpec((tm, w), lambda r: (r, 0))
    return pl.pallas_call(
        body, name=name, grid=(m // tm,),
        in_specs=[w1024(cols.qa), w1024(cols.ka), w1024(cols.va), w512(cols.qb), w128(cols.kb), w128(cols.vb),
                  w128(cols.fa), vec(LANES), vec(LANES), vec(512), vec(LANES), vec(LANES)],
        out_specs=[row(1024), row(1024), row(1024), row(512), row(LANES), row(LANES)],
        out_shape=[jax.ShapeDtypeStruct((m, 1024), BF16)] * 3 + [jax.ShapeDtypeStruct((m, 512), BF16)]
                  + [jax.ShapeDtypeStruct((m, LANES), BF16)] * 2,
        scratch_shapes=[pltpu.VMEM((1, LANES), F32)],
        compiler_params=_params(1),
    )(proj, proj, proj, proj, proj, proj, proj, gqa, gka, gqb, gkb, fbias)


def qk_post_bwd(proj, gains, fbias, dqf, dkf, dvf, dqb, dkb, dvb, dc, dga, dgb, cols, name):
    m = proj.shape[0]
    d = cols.d
    tm = _row_tile(m)
    nt = m // tm
    gqa, gka, gqb, gkb = gains

    def body(qa_ref, ka_ref, qb_ref, kb_ref, fa_ref, gqa_ref, gka_ref, gqb_ref, gkb_ref, fb_ref,
             dqf_ref, dkf_ref, dvf_ref, dqb_ref, dkb_ref, dvb_ref, dc_ref, dga_ref, dgb_ref,
             dp_o, ggqa_o, ggka_o, ggqb_o, ggkb_o, gfb_o, carry_ref):
        @pl.when(pl.program_id(0) == 0)
        def _():
            carry_ref[...] = jnp.zeros_like(carry_ref)
            for o in (ggqa_o, ggka_o, ggqb_o, ggkb_o, gfb_o):
                o[...] = jnp.zeros_like(o)

        dp_o[:, cols.ga:cols.ga + d] = dga_ref[...].astype(BF16)
        dp_o[:, cols.gb:cols.gb + d] = dgb_ref[...].astype(BF16)
        dp_o[:, cols.fa + LANES:cols.np] = jnp.zeros((tm, cols.np - cols.fa - LANES), BF16)
        lane = lax.broadcasted_iota(jnp.int32, (tm, LANES), 1)
        data = lane < HEAD_DIM
        ones = jnp.ones((LANES, LANES), BF16)
        for hd in range(8):
            sl = slice(hd * LANES, (hd + 1) * LANES)
            for src, gn, dn_ref, off, gout in ((qa_ref, gqa_ref, dqf_ref, cols.qa, ggqa_o),
                                               (ka_ref, gka_ref, dkf_ref, cols.ka, ggka_o)):
                x = src[:, sl]
                dn = jnp.where(data, dn_ref[:, sl], 0.0)
                r = lax.rsqrt(_group_mean(x * x, ones) + EPS)
                xh = x * r
                dxh = dn * gn[...]
                dp_o[:, off + hd * LANES:off + (hd + 1) * LANES] = (
                    r * (dxh - xh * _group_mean(dxh * xh, ones))).astype(BF16)
                gout[...] += jnp.sum(dn * xh, axis=0, keepdims=True)
            dp_o[:, cols.va + hd * LANES:cols.va + (hd + 1) * LANES] = jnp.where(data, dvf_ref[:, sl], 0.0).astype(BF16)
        dp_o[:, cols.vb:cols.vb + LANES] = dvb_ref[...].astype(BF16)
        gones = _group_ones()
        for src, gn, dn, off, gout in ((qb_ref, gqb_ref, dqb_ref, cols.qb, ggqb_o),
                                       (kb_ref, gkb_ref, dkb_ref, cols.kb, ggkb_o)):
            dxs, dgs = _head_norm_bwd(src[...], gn[...], dn[...], gones)
            for b, (dx, dg) in enumerate(zip(dxs, dgs)):
                dp_o[:, off + b * LANES:off + (b + 1) * LANES] = dx.astype(BF16)
                gout[:, b * LANES:(b + 1) * LANES] += dg
        dcv = dc_ref[...]
        rr = lax.broadcasted_iota(jnp.int32, (tm, tm), 0)
        cc = lax.broadcasted_iota(jnp.int32, (tm, tm), 1)
        triu = jnp.where(cc >= rr, 1.0, 0.0).astype(BF16)
        p0, p1, p2 = _split3(dcv)
        dlogf = _dot(triu, p0) + _dot(triu, p1) + _dot(triu, p2) + carry_ref[...]
        carry_ref[...] += jnp.sum(dcv, axis=0, keepdims=True)
        z = fa_ref[...] + fb_ref[...]
        row = (nt - 1 - pl.program_id(0)) * tm + lax.broadcasted_iota(jnp.int32, (tm, LANES), 0)
        dfa = jnp.where(row >= PAD_FRONT, dlogf * jax.nn.sigmoid(-z), 0.0)
        dp_o[:, cols.fa:cols.fa + LANES] = dfa.astype(BF16)
        gfb_o[...] += jnp.sum(dfa, axis=0, keepdims=True)

    rev = lambda r: nt - 1 - r
    w1024 = lambda off: pl.BlockSpec((tm, 1024), lambda r, o=off // 1024: (rev(r), o))
    w512 = lambda off: pl.BlockSpec((tm, 512), lambda r, o=off // 512: (rev(r), o))
    w128 = lambda off: pl.BlockSpec((tm, LANES), lambda r, o=off // LANES: (rev(r), o))
    vec = lambda w: pl.BlockSpec((1, w), lambda r: (0, 0))
    row = lambda w: pl.BlockSpec((tm, w), lambda r: (rev(r), 0))
    return pl.pallas_call(
        body, name=name, grid=(nt,),
        in_specs=[w1024(cols.qa), w1024(cols.ka), w512(cols.qb), w128(cols.kb), w128(cols.fa),
                  vec(LANES), vec(LANES), vec(512), vec(LANES), vec(LANES),
                  row(1024), row(1024), row(1024), row(512), row(LANES), row(LANES), row(LANES), row(d), row(d)],
        out_specs=[row(cols.np), vec(LANES), vec(LANES), vec(512), vec(LANES), vec(LANES)],
        out_shape=[jax.ShapeDtypeStruct((m, cols.np), BF16), jax.ShapeDtypeStruct((1, LANES), F32),
                   jax.ShapeDtypeStruct((1, LANES), F32), jax.ShapeDtypeStruct((1, 512), F32),
                   jax.ShapeDtypeStruct((1, LANES), F32), jax.ShapeDtypeStruct((1, LANES), F32)],
        scratch_shapes=[pltpu.VMEM((1, LANES), F32)],
        compiler_params=_params(1),
    )(proj, proj, proj, proj, proj, gqa, gka, gqb, gkb, fbias, dqf, dkf, dvf, dqb, dkb, dvb, dc, dga, dgb)


def dproj_bwd(dh, h, g, dproj, wp, name):
    m, d = h.shape
    npad = wp.shape[1]
    tm = _row_tile(m)

    def body(dh_ref, h_ref, g_ref, dp_ref, w_ref, dhin_ref, dgn_ref):
        @pl.when(pl.program_id(0) == 0)
        def _():
            dgn_ref[...] = jnp.zeros_like(dgn_ref)

        dn = _dot_nt(dp_ref[...], w_ref[...])
        dx, dgain = _rms_bwd(h_ref[...], g_ref[...], dn)
        dgn_ref[...] += dgain
        dhin_ref[...] = dh_ref[...] + dx

    row = lambda w: pl.BlockSpec((tm, w), lambda r: (r, 0))
    return pl.pallas_call(
        body, name=name, grid=(m // tm,),
        in_specs=[row(d), row(d), pl.BlockSpec((1, d), lambda r: (0, 0)), row(npad),
                  pl.BlockSpec((d, npad), lambda r: (0, 0))],
        out_specs=[row(d), pl.BlockSpec((1, d), lambda r: (0, 0))],
        out_shape=[jax.ShapeDtypeStruct((m, d), F32), jax.ShapeDtypeStruct((1, d), F32)],
        compiler_params=_params(1),
    )(dh, h, g, dproj, wp)


def _causal_t(t):
    return lax.broadcasted_iota(jnp.int32, (t, t), 0) <= lax.broadcasted_iota(jnp.int32, (t, t), 1)


HEADS_PER_STEP = 2


def fox_fwd(qf, kf, vt, name, rider=None):
    m = qf.shape[0]
    t = _row_tile(m)
    nq = m // t
    hp = HEADS_PER_STEP
    w = hp * LANES

    def body(q_ref, k_ref, vt_ref, o_ref, lse_ref, acc_ref, m_ref, p_ref, a_ref):
        qi = pl.program_id(1)
        acc_ref[...] = jnp.zeros_like(acc_ref)
        m_ref[...] = jnp.full_like(m_ref, NEG)

        def scores(ki, slot, mask):
            off = pl.multiple_of(ki * t, t)
            for e in range(hp):
                sl = slice(e * LANES, (e + 1) * LANES)
                s = _dot_nt(k_ref[pl.ds(off, t), sl], q_ref[:, sl])
                if mask is not None:
                    s = jnp.where(mask, s, NEG)
                m_old = m_ref[e]
                m_new = jnp.maximum(m_old, jnp.max(s, axis=0, keepdims=True))
                p_ref[slot, e] = jnp.exp(s - m_new).astype(BF16)
                a_ref[slot, e] = jnp.exp(m_old - m_new)
                m_ref[e] = m_new

        def values(ki, slot):
            off = pl.multiple_of(ki * t, t)
            for e in range(hp):
                sl = slice(e * LANES, (e + 1) * LANES)
                acc_ref[e] = acc_ref[e] * a_ref[slot, e] + _dot(vt_ref[sl, pl.ds(off, t)], p_ref[slot, e])

        causal = _causal_t(t)
        scores(0, 0, causal | (jnp.full((t, t), qi, jnp.int32) > 0))

        def step(ki, carry):
            values(ki - 1, (ki - 1) % 2)
            scores(ki, ki % 2, None)
            return carry

        lax.fori_loop(1, qi, step, 0)

        @pl.when(qi >= 1)
        def _():
            values(qi - 1, (qi - 1) % 2)
            scores(qi, qi % 2, causal)

        values(qi, qi % 2)
        row = lax.broadcasted_iota(jnp.int32, (LANES, t), 0)
        for e in range(hp):
            l = jnp.max(acc_ref[e, AUG:AUG + 8, :], axis=0, keepdims=True)
            o_ref[:, e * LANES:(e + 1) * LANES] = jnp.where(row < HEAD_DIM, acc_ref[e] * (1.0 / l), 0.0).T
            lse_ref[e] = m_ref[e] + jnp.log(l)

    return rider_call(
        body, name, (8 // hp, nq),
        in_specs=[pl.BlockSpec((t, w), lambda hd, i: (i, hd)),
                  pl.BlockSpec((m, w), lambda hd, i: (0, hd)),
                  pl.BlockSpec((w, m), lambda hd, i: (hd, 0))],
        out_specs=[pl.BlockSpec((t, w), lambda hd, i: (i, hd)),
                   pl.BlockSpec((hp, 1, t), lambda hd, i: (hd, 0, i))],
        out_shape=[jax.ShapeDtypeStruct((m, 8 * LANES), F32), jax.ShapeDtypeStruct((8, 1, m), F32)],
        scratch_shapes=[pltpu.VMEM((hp, LANES, t), F32), pltpu.VMEM((hp, 1, t), F32),
                        pltpu.VMEM((2, hp, t, t), BF16), pltpu.VMEM((2, hp, 1, t), F32)],
        args=(qf, kf, vt), rider=rider)


def fox_bwd(qf, kf, vf, kt, dof, lse, delta, name, rider=None):
    m = qf.shape[0]
    t = _row_tile(m)
    nq = m // t
    hp = HEADS_PER_STEP
    w = hp * LANES

    def body(k_ref, v_ref, kt_ref, q_ref, do_ref, lse_ref, delta_ref, dk_ref, dv_ref, dq_ref, dck_ref, dcq_ref,
             dka_ref, dva_ref, dqt_ref):
        ki = pl.program_id(1)

        @pl.when(ki == 0)
        def _():
            dqt_ref[...] = jnp.zeros_like(dqt_ref)

        dka_ref[...] = jnp.zeros_like(dka_ref)
        dva_ref[...] = jnp.zeros_like(dva_ref)

        def tile(qi, diagonal):
            off = pl.multiple_of(qi * t, t)
            for e in range(hp):
                sl = slice(e * LANES, (e + 1) * LANES)
                q = q_ref[pl.ds(off, t), sl]
                do = do_ref[pl.ds(off, t), sl]
                s = _dot_nt(k_ref[:, sl], q)
                if diagonal:
                    s = jnp.where(_causal_t(t), s, NEG)
                p = jnp.exp(s - lse_ref[e, :, pl.ds(off, t)])
                ds = (p * (_dot_nt(v_ref[:, sl], do) - delta_ref[e, :, pl.ds(off, t)])).astype(BF16)
                dva_ref[:, sl] += _dot(p.astype(BF16), do)
                dka_ref[:, sl] += _dot(ds, q)
                dqt_ref[sl, pl.ds(off, t)] += _dot(kt_ref[sl, :], ds)

        def step(qi, carry):
            tile(qi, False)
            return carry

        tile(ki, True)
        lax.fori_loop(ki + 1, nq, step, 0)
        dk_ref[...] = dka_ref[...]
        dv_ref[...] = dva_ref[...]
        row8 = lax.broadcasted_iota(jnp.int32, (8, 1), 0)
        for e in range(hp):
            slab = dka_ref[:, e * LANES:(e + 1) * LANES].T[AUG:AUG + 8, :]
            dck_ref[e] = -jnp.sum(jnp.where(row8 == 3, slab, 0.0), axis=0, keepdims=True)

        @pl.when(ki == nq - 1)
        def _():
            for e in range(hp):
                sl = slice(e * LANES, (e + 1) * LANES)
                slab = dqt_ref[e * LANES + AUG:e * LANES + AUG + 8, :]
                dcq_ref[e] = jnp.sum(jnp.where(row8 == 0, slab, 0.0), axis=0, keepdims=True)
                for j in range(nq):
                    dq_ref[j * t:(j + 1) * t, sl] = dqt_ref[sl, j * t:(j + 1) * t].T

    tile_spec = pl.BlockSpec((t, w), lambda hd, i: (i, hd))
    full = pl.BlockSpec((m, w), lambda hd, i: (0, hd))
    stat = pl.BlockSpec((hp, 1, m), lambda hd, i: (hd, 0, 0))
    (dkf, dvf, dqf, dck, dcq), carried = rider_call(
        body, name, (8 // hp, nq),
        in_specs=[tile_spec, tile_spec, pl.BlockSpec((w, t), lambda hd, i: (hd, i)), full, full, stat, stat],
        out_specs=[tile_spec, tile_spec, full, pl.BlockSpec((hp, 1, t), lambda hd, i: (hd, 0, i)), stat],
        out_shape=[jax.ShapeDtypeStruct((m, 8 * LANES), F32), jax.ShapeDtypeStruct((m, 8 * LANES), F32),
                   jax.ShapeDtypeStruct((m, 8 * LANES), F32), jax.ShapeDtypeStruct((8, 1, m), F32),
                   jax.ShapeDtypeStruct((8, 1, m), F32)],
        scratch_shapes=[pltpu.VMEM((t, w), F32), pltpu.VMEM((t, w), F32), pltpu.VMEM((w, m), F32)],
        args=(kf, vf, kt, qf, dof, lse, delta), rider=rider)
    return (dkf, dvf, dqf, dcq + dck), carried


def _bucket_ids():
    def bucket(dist):
        n = np.maximum(dist, 0)
        max_exact = N_BUCKETS // 2
        nf = np.maximum(n, 1).astype(np.float32)
        large = max_exact + (np.log(nf / max_exact) / math.log(MAX_DISTANCE / max_exact)
                             * (N_BUCKETS - max_exact)).astype(np.int32)
        return np.where(n < max_exact, n, np.minimum(large, N_BUCKETS - 1))

    tl = np.arange(LANES)[:, None]
    sl = np.arange(LANES)[None, :]
    prev = bucket(LANES + tl - sl)
    cur = bucket(tl - sl)
    meta = np.full((LANES, LANES), N_BUCKETS - 1)
    return np.concatenate([prev, cur, meta], axis=1).astype(np.int32)


def bias_build(table, name):
    ids = jnp.asarray(_bucket_ids())

    def body(t_ref, id_ref, o_ref):
        idv = id_ref[...]
        for h in range(8):
            acc = jnp.zeros((LANES, 3 * LANES), F32)
            for b in range(N_BUCKETS):
                acc = jnp.where(idv == b, t_ref[b, h], acc)
            o_ref[h] = acc

    return pl.pallas_call(
        body, name=name,
        in_specs=[pl.BlockSpec(memory_space=pltpu.SMEM), pl.BlockSpec(memory_space=pltpu.VMEM)],
        out_specs=pl.BlockSpec(memory_space=pltpu.VMEM),
        out_shape=jax.ShapeDtypeStruct((8, LANES, 3 * LANES), F32),
    )(table, ids)


def bias_reduce(dbias, name):
    ids = jnp.asarray(_bucket_ids())

    def body(d_ref, id_ref, o_ref):
        idv = id_ref[...]
        rr = lax.broadcasted_iota(jnp.int32, (N_BUCKETS, LANES), 0)
        cc = lax.broadcasted_iota(jnp.int32, (N_BUCKETS, LANES), 1)
        acc = jnp.zeros((N_BUCKETS, LANES), F32)
        for h in range(8):
            dv = d_ref[h]
            for b in range(N_BUCKETS):
                val = jnp.sum(jnp.where(idv == b, dv, 0.0), keepdims=True)
                acc = jnp.where((rr == b) & (cc == h), val, acc)
        o_ref[...] = acc

    return pl.pallas_call(
        body, name=name,
        in_specs=[pl.BlockSpec(memory_space=pltpu.VMEM), pl.BlockSpec(memory_space=pltpu.VMEM)],
        out_specs=pl.BlockSpec(memory_space=pltpu.VMEM),
        out_shape=jax.ShapeDtypeStruct((N_BUCKETS, LANES), F32),
    )(dbias, ids)


def _swa_penalty(n):
    shape = (LANES, 3 * LANES)
    tl = lax.broadcasted_iota(jnp.int32, shape, 0)
    col = lax.broadcasted_iota(jnp.int32, shape, 1)
    sl = col & (LANES - 1)
    nv = jnp.full(shape, n, jnp.int32)
    is_meta = sl >= PAD_FRONT
    prev = (col < LANES) & (sl > tl) & (nv >= 1) & ((nv >= 2) | is_meta)
    cur = (col >= LANES) & (col < 2 * LANES) & (sl <= tl) & ((nv >= 1) | is_meta)
    meta = (col >= 2 * LANES) & is_meta & ((nv >= 2) | ((nv == 1) & (sl <= tl)))
    return jnp.where(prev | cur | meta, 0.0, NEG)


def _swa_keys(ref, n):
    off_prev = pl.multiple_of(jnp.maximum(n - 1, 0) * LANES, LANES)
    off_cur = pl.multiple_of(n * LANES, LANES)
    return jnp.concatenate([ref[pl.ds(off_prev, LANES), :], ref[pl.ds(off_cur, LANES), :], ref[0:LANES, :]], axis=0)


def swa_fwd(q, k, v, bias, sinks, name):
    m = q.shape[0]

    def body(q_ref, k_ref, v_ref, bias_ref, sink_ref, o_ref, lse_ref):
        n = pl.program_id(0)
        lane1 = lax.broadcasted_iota(jnp.int32, (1, LANES), 1)
        lane_t = lax.broadcasted_iota(jnp.int32, (LANES, LANES), 1)
        in_head = [lane1 < HEAD_DIM, lane1 >= HEAD_DIM]
        kall = _swa_keys(k_ref, n)
        vall = _swa_keys(v_ref, n)
        vs = [jnp.where(in_head[g], vall, jnp.zeros_like(vall)) for g in (0, 1)]
        penalty = _swa_penalty(n)
        lse = jnp.zeros((LANES, LANES), F32)
        for b in range(4):
            qb = q_ref[:, b * LANES:(b + 1) * LANES]
            ob = jnp.zeros((LANES, LANES), F32)
            for g in (0, 1):
                h = 4 * g + b
                qe = jnp.where(in_head[g], qb, jnp.zeros_like(qb))
                s = _dot_nt(qe, kall) + bias_ref[h] + penalty
                sink = sink_ref[h]
                mx = jnp.maximum(jnp.max(s, axis=1, keepdims=True), sink)
                p = jnp.exp(s - mx)
                den = jnp.sum(p, axis=1, keepdims=True) + jnp.exp(sink - mx)
                ob = ob + _dot((p / den).astype(BF16), vs[g])
                lse = jnp.where(lane_t == h, mx + jnp.log(den), lse)
            o_ref[:, b * LANES:(b + 1) * LANES] = ob
        lse_ref[...] = lse

    return pl.pallas_call(
        body, name=name, grid=(m // LANES,),
        in_specs=[pl.BlockSpec((LANES, 512), lambda n: (n, 0)),
                  pl.BlockSpec((m, LANES), lambda n: (0, 0)), pl.BlockSpec((m, LANES), lambda n: (0, 0)),
                  pl.BlockSpec((8, LANES, 3 * LANES), lambda n: (0, 0, 0)),
                  pl.BlockSpec(memory_space=pltpu.SMEM)],
        out_specs=[pl.BlockSpec((LANES, 512), lambda n: (n, 0)), pl.BlockSpec((LANES, LANES), lambda n: (n, 0))],
        out_shape=[jax.ShapeDtypeStruct((m, 512), F32), jax.ShapeDtypeStruct((m, LANES), F32)],
        compiler_params=_params(1),
    )(q, k, v, bias, sinks)


def swa_bwd(q, k, v, bias, sinks, o, lse, do, name):
    m = q.shape[0]

    def body(q_ref, do_ref, o_ref, lse_ref, k_ref, v_ref, bias_ref, sink_ref,
             dq_ref, dk_ref, dv_ref, dbias_ref, dsink_ref):
        n = pl.program_id(0)

        @pl.when(n == 0)
        def _():
            for r in (dk_ref, dv_ref, dbias_ref, dsink_ref):
                r[...] = jnp.zeros_like(r)

        lane1 = lax.broadcasted_iota(jnp.int32, (1, LANES), 1)
        lane_t = lax.broadcasted_iota(jnp.int32, (LANES, LANES), 1)
        in_head = [lane1 < HEAD_DIM, lane1 >= HEAD_DIM]
        off_prev = pl.multiple_of(jnp.maximum(n - 1, 0) * LANES, LANES)
        off_cur = pl.multiple_of(n * LANES, LANES)
        kall = _swa_keys(k_ref, n)
        vall = _swa_keys(v_ref, n)
        ks = [jnp.where(in_head[g], kall, jnp.zeros_like(kall)) for g in (0, 1)]
        penalty = _swa_penalty(n)
        lsev = lse_ref[...]
        dsink = dsink_ref[...]
        dkall = jnp.zeros((3 * LANES, LANES), F32)
        dvall = jnp.zeros((3 * LANES, LANES), F32)
        for b in range(4):
            sl = slice(b * LANES, (b + 1) * LANES)
            qb = q_ref[:, sl]
            dob = do_ref[:, sl]
            prod = dob * o_ref[:, sl]
            dqb = jnp.zeros((LANES, LANES), F32)
            for g in (0, 1):
                h = 4 * g + b
                qe = jnp.where(in_head[g], qb, jnp.zeros_like(qb))
                doe = jnp.where(in_head[g], dob, 0.0).astype(BF16)
                delta = jnp.sum(jnp.where(in_head[g], prod, 0.0), axis=1, keepdims=True)
                lse_h = _lane_col(lsev, lane_t, h)
                s = _dot_nt(qe, kall) + bias_ref[h] + penalty
                p = jnp.exp(s - lse_h)
                ds = p * (_dot_nt(doe, vall) - delta)
                dbias_ref[h] += ds
                sink_part = jnp.sum(-jnp.exp(sink_ref[h] - lse_h) * delta, keepdims=True)
                dsink = jnp.where(lane1 == h, dsink + sink_part, dsink)
                dsb = ds.astype(BF16)
                dqb = dqb + _dot(dsb, ks[g])
                dkall = dkall + _dot_tn(dsb, qe)
                dvall = dvall + _dot_tn(p.astype(BF16), doe)
            dq_ref[:, sl] = dqb
        dsink_ref[...] = dsink
        for ref, val in ((dk_ref, dkall), (dv_ref, dvall)):
            ref[pl.ds(off_prev, LANES), :] += val[0:LANES]
            ref[pl.ds(off_cur, LANES), :] += val[LANES:2 * LANES]
            ref[0:LANES, :] += val[2 * LANES:3 * LANES]

    blk = pl.BlockSpec((LANES, 512), lambda n: (n, 0))
    full = pl.BlockSpec((m, LANES), lambda n: (0, 0))
    return pl.pallas_call(
        body, name=name, grid=(m // LANES,),
        in_specs=[blk, blk, blk, pl.BlockSpec((LANES, LANES), lambda n: (n, 0)), full, full,
                  pl.BlockSpec((8, LANES, 3 * LANES), lambda n: (0, 0, 0)),
                  pl.BlockSpec(memory_space=pltpu.SMEM)],
        out_specs=[blk, full, full, pl.BlockSpec((8, LANES, 3 * LANES), lambda n: (0, 0, 0)),
                   pl.BlockSpec((1, LANES), lambda n: (0, 0))],
        out_shape=[jax.ShapeDtypeStruct((m, 512), F32), jax.ShapeDtypeStruct((m, LANES), F32),
                   jax.ShapeDtypeStruct((m, LANES), F32), jax.ShapeDtypeStruct((8, LANES, 3 * LANES), F32),
                   jax.ShapeDtypeStruct((1, LANES), F32)],
        compiler_params=_params(1),
    )(q, do, o, lse, k, v, bias, sinks)


def branch_out(h, o_fox, o_swa, proj, wbf, wbs, wo, cols, name):
    m, d = h.shape
    tm = _row_tile(m)

    def body(h_ref, of_ref, os_ref, ga_ref, gb_ref, wbf_ref, wbs_ref, wo_ref, hn_ref):
        tf = _dot(of_ref[...].astype(BF16), wbf_ref[...])
        ts = _dot(os_ref[...].astype(BF16), wbs_ref[...])
        y = jax.nn.sigmoid(ga_ref[...]) * tf + jax.nn.sigmoid(gb_ref[...]) * ts
        hn_ref[...] = h_ref[...] + _dot(y.astype(BF16), wo_ref[...])

    row = lambda w, o=0: pl.BlockSpec((tm, w), lambda r, o=o: (r, o))
    res = lambda a: pl.BlockSpec(a.shape, lambda r: (0, 0))
    return pl.pallas_call(
        body, name=name, grid=(m // tm,),
        in_specs=[row(d), row(1024), row(512), row(d, cols.ga // d), row(d, cols.gb // d), res(wbf), res(wbs), res(wo)],
        out_specs=row(d),
        out_shape=jax.ShapeDtypeStruct((m, d), F32),
        compiler_params=_params(1),
    )(h, o_fox, o_swa, proj, proj, wbf, wbs, wo)


def branch_out_bwd(dh, o_fox, o_swa, proj, wbf, wbs, wo, cols, name):
    m, d = dh.shape
    tm = _row_tile(m)

    def body(dh_ref, of_ref, os_ref, ga_ref, gb_ref, wbf_ref, wbs_ref, wo_ref,
             y_ref, dtf_ref, dts_ref, dga_ref, dgb_ref, dof_ref, dos_ref, delta_ref):
        dy = _dot_nt(dh_ref[...].astype(BF16), wo_ref[...])
        tf = _dot(of_ref[...].astype(BF16), wbf_ref[...])
        ts = _dot(os_ref[...].astype(BF16), wbs_ref[...])
        sa = jax.nn.sigmoid(ga_ref[...])
        sb = jax.nn.sigmoid(gb_ref[...])
        y_ref[...] = (sa * tf + sb * ts).astype(BF16)
        dtf = (dy * sa).astype(BF16)
        dts = (dy * sb).astype(BF16)
        dtf_ref[...] = dtf
        dts_ref[...] = dts
        dga_ref[...] = (dy * tf * sa * (1.0 - sa)).astype(BF16)
        dgb_ref[...] = (dy * ts * sb * (1.0 - sb)).astype(BF16)
        dof = _dot_nt(dtf, wbf_ref[...])
        dof_ref[...] = dof.astype(BF16)
        dos_ref[...] = _dot_nt(dts, wbs_ref[...])
        lane = lax.broadcasted_iota(jnp.int32, (tm, LANES), 1)
        delta = jnp.zeros((tm, LANES), F32)
        for hd in range(8):
            sl = slice(hd * LANES, (hd + 1) * LANES)
            delta = jnp.where(lane == hd, jnp.sum(dof[:, sl] * of_ref[:, sl], axis=1, keepdims=True), delta)
        delta_ref[...] = delta

    row = lambda w, o=0: pl.BlockSpec((tm, w), lambda r, o=o: (r, o))
    res = lambda a: pl.BlockSpec(a.shape, lambda r: (0, 0))
    return pl.pallas_call(
        body, name=name, grid=(m // tm,),
        in_specs=[row(d), row(1024), row(512), row(d, cols.ga // d), row(d, cols.gb // d), res(wbf), res(wbs), res(wo)],
        out_specs=[row(d)] * 5 + [row(1024), row(512), row(LANES)],
        out_shape=[jax.ShapeDtypeStruct((m, d), BF16)] * 5 + [jax.ShapeDtypeStruct((m, 1024), BF16),
                   jax.ShapeDtypeStruct((m, 512), F32), jax.ShapeDtypeStruct((m, LANES), F32)],
        compiler_params=_params(1),
    )(dh, o_fox, o_swa, proj, proj, wbf, wbs, wo)


def loss_head(h, target, name):
    m, d = h.shape

    def body(h_ref, t_ref, dh_ref, loss_ref):
        n = pl.program_id(0)

        @pl.when(n == 0)
        def _():
            loss_ref[...] = jnp.zeros_like(loss_ref)
            dh_ref[...] = jnp.zeros_like(dh_ref)

        @pl.when(n > 0)
        def _():
            err = h_ref[...] - t_ref[...]
            dh_ref[...] = err * (1.0 / d)
            loss_ref[...] += jnp.sum(err * err, keepdims=True) * (0.5 / d)

    return pl.pallas_call(
        body, name=name, grid=(m // LANES,),
        in_specs=[pl.BlockSpec((LANES, d), lambda n: (n, 0)),
                  pl.BlockSpec((LANES, d), lambda n: (jnp.maximum(n - 1, 0), 0))],
        out_specs=[pl.BlockSpec((LANES, d), lambda n: (n, 0)), pl.BlockSpec((8, LANES), lambda n: (0, 0))],
        out_shape=[jax.ShapeDtypeStruct((m, d), F32), jax.ShapeDtypeStruct((8, LANES), F32)],
        compiler_params=_params(1),
    )(h, target)


def _adamw_math(w, g, m, v):
    m = ADAM_B1 * m + (1.0 - ADAM_B1) * g
    v = ADAM_B2 * v + (1.0 - ADAM_B2) * (g * g)
    m_hat = m / (1.0 - ADAM_B1 ** ADAM_STEP)
    v_hat = v / (1.0 - ADAM_B2 ** ADAM_STEP)
    delta = -ADAM_LR * (m_hat / (jnp.sqrt(v_hat) + ADAM_EPS) + ADAM_WD * w)
    return delta, m, v


def adamw_sum(parts, w, m, v, name, after=None):
    n_layers, a, b = w.shape
    ta = next(t for t in (256, 176, 128, a) if a % t == 0)
    nr = a // ta

    def body(*refs):
        p_refs = refs[:n_layers]
        w_ref, m_ref, v_ref = refs[n_layers:n_layers + 3]
        g_o, d_o, m_o, v_o = refs[-4:]
        for l in range(n_layers):
            @pl.when(pl.program_id(0) == l)
            def _(l=l):
                g = p_refs[l][0].astype(F32)
                for j in range(1, N_DEV):
                    g = g + p_refs[l][j].astype(F32)
                g_o[0] = g
                d_o[0], m_o[0], v_o[0] = _adamw_math(w_ref[0], g, m_ref[0], v_ref[0])

    def part_spec(l):
        return pl.BlockSpec((N_DEV, ta, b), lambda i, r, l=l: (0, jnp.where(i == l, r, jnp.where(i < l, 0, nr - 1)), 0))

    row = pl.BlockSpec((1, ta, b), lambda i, r: (i, r, 0))
    return pl.pallas_call(
        body, name=name, grid=(n_layers, nr),
        in_specs=[part_spec(l) for l in range(n_layers)] + [row, row, row]
                 + ([pl.BlockSpec(memory_space=pl.ANY)] if after is not None else []),
        out_specs=[row] * 4,
        out_shape=[jax.ShapeDtypeStruct(w.shape, F32)] * 4,
        compiler_params=_params(2),
    )(*parts, w, m, v, *([after] if after is not None else []))


def adamw_small(g, w, m, v, name):
    def body(g_ref, w_ref, m_ref, v_ref, d_o, m_o, v_o):
        d_o[...], m_o[...], v_o[...] = _adamw_math(w_ref[...], g_ref[...], m_ref[...], v_ref[...])

    spec = pl.BlockSpec(memory_space=pltpu.VMEM)
    return pl.pallas_call(
        body, name=name, in_specs=[spec] * 4, out_specs=[spec] * 3,
        out_shape=[jax.ShapeDtypeStruct(w.shape, F32)] * 3,
    )(g, w, m, v)


BIG = ("ffn1_w_in", "ffn1_w_out", "w_in", "w_branch_fox", "w_branch_swa", "w_out", "ffn2_w_in", "ffn2_w_out")
SMALL = ("rel_bias_table", "ffn1_norm", "mix_norm", "forget_bias", "fox_q_norm", "fox_k_norm",
         "swa_q_norm", "swa_k_norm", "swa_sinks", "ffn2_norm")
WEIGHTS = ("meta_tokens", "rel_bias_table", "ffn1_norm", "ffn1_w_in", "ffn1_w_out", "mix_norm", "w_in",
           "forget_bias", "fox_q_norm", "fox_k_norm", "swa_q_norm", "swa_k_norm", "swa_sinks", "w_branch_fox",
           "w_branch_swa", "w_out", "ffn2_norm", "ffn2_w_in", "ffn2_w_out")


def _pack(arrs, width, row_multiple, dtype):
    lead = arrs[0].shape[:-1]
    flat = jnp.concatenate([a.astype(dtype) for a in arrs], axis=-1)
    n = flat.shape[-1]
    rows = -(-n // width)
    rows = -(-rows // row_multiple) * row_multiple
    flat = jnp.pad(flat, [(0, 0)] * len(lead) + [(0, rows * width - n)])
    return flat.reshape(lead + (rows, width))


def _unpack(flat, shapes):
    flat = flat.reshape(-1)
    out, off = [], 0
    for s in shapes:
        n = int(np.prod(s))
        out.append(flat[off:off + n].reshape(s))
        off += n
    return out


def _swa_head_order():
    return [4 * (j % 2) + j // 2 for j in range(8)]


def _permute_heads(a, axis, inverse=False):
    order = _swa_head_order()
    if inverse:
        order = [order.index(hd) for hd in range(8)]
    parts = [lax.slice_in_dim(a, hd * HEAD_DIM, (hd + 1) * HEAD_DIM, axis=axis) for hd in order]
    return jnp.concatenate(parts, axis=axis)


def _w_in_segments(cols):
    d = cols.d
    segs = [(512 * i + HEAD_DIM * hd, HEAD_DIM, new + LANES * hd)
            for i, new in enumerate((cols.qa, cols.ka, cols.va)) for hd in range(8)]
    segs.append((1536, 8, cols.fa))
    order = _swa_head_order()
    segs += [(1544 + HEAD_DIM * hd, HEAD_DIM, cols.qb + HEAD_DIM * order.index(hd)) for hd in range(8)]
    segs += [(2056, 128, cols.kb), (2184, 128, cols.vb), (2312, d, cols.ga), (2312 + d, d, cols.gb)]
    return segs


def _reorder_w_in(blocks, cols):
    width = blocks[0].shape[1]
    zeros = lambda n: jnp.zeros((blocks[0].shape[0], n), blocks[0].dtype)
    parts, at = [], 0
    for old, length, new in sorted(_w_in_segments(cols), key=lambda s: s[2]):
        if new > at:
            parts.append(zeros(new - at))
        at = new + length
        while length:
            j, off = divmod(old, width)
            take = min(length, width - off)
            parts.append(blocks[j][:, off:off + take])
            old, length = old + take, length - take
    parts.append(zeros(cols.np - at))
    return jnp.concatenate(parts, axis=1)


def _restore_w_in(wp, cols, width):
    segs = sorted(_w_in_segments(cols))
    blocks = []
    for j in range(N_DEV):
        lo, hi = j * width, (j + 1) * width
        parts = []
        for old, length, new in segs:
            a, b = max(old, lo), min(old + length, hi)
            if a < b:
                parts.append(wp[:, new + a - old:new + b - old])
        blocks.append(jnp.concatenate(parts, axis=1))
    return jnp.stack(blocks)


def _lane_pad(v):
    return jnp.pad(v, ((0, 0), (0, LANES - v.shape[1])))


def kernel(x, meta_tokens, rel_bias_table, ffn1_norm, ffn1_w_in, ffn1_w_out, mix_norm, w_in, forget_bias, fox_q_norm, fox_k_norm, swa_q_norm, swa_k_norm, swa_sinks, w_branch_fox, w_branch_swa, w_out, ffn2_norm, ffn2_w_in, ffn2_w_out, loss_target, m_meta_tokens, m_rel_bias_table, m_ffn1_norm, m_ffn1_w_in, m_ffn1_w_out, m_mix_norm, m_w_in, m_forget_bias, m_fox_q_norm, m_fox_k_norm, m_swa_q_norm, m_swa_k_norm, m_swa_sinks, m_w_branch_fox, m_w_branch_swa, m_w_out, m_ffn2_norm, m_ffn2_w_in, m_ffn2_w_out, v_meta_tokens, v_rel_bias_table, v_ffn1_norm, v_ffn1_w_in, v_ffn1_w_out, v_mix_norm, v_w_in, v_forget_bias, v_fox_q_norm, v_fox_k_norm, v_swa_q_norm, v_swa_k_norm, v_swa_sinks, v_w_branch_fox, v_w_branch_swa, v_w_out, v_ffn2_norm, v_ffn2_w_in, v_ffn2_w_out):
    args = dict(locals())
    wts = {n: args[n] for n in WEIGHTS}
    mom1 = {n: args["m_" + n] for n in WEIGHTS}
    mom2 = {n: args["v_" + n] for n in WEIGHTS}

    seq, d = x.shape[1], x.shape[2]
    m_rows = seq + LANES
    depth = ffn1_norm.shape[0]
    fb = ffn1_w_in.shape[2]
    fo = ffn1_w_out.shape[1]
    din_shard = w_in.shape[2]
    cols = _Cols(d)
    scale = HEAD_DIM ** -0.5
    dev = 4 * lax.axis_index("x") + 2 * lax.axis_index("y") + lax.axis_index("c")

    groups = {"ffn1": ("ffn1_w_in", "ffn1_w_out"), "mix": ("w_in", "w_branch_fox", "w_branch_swa", "w_out"),
              "ffn2": ("ffn2_w_in", "ffn2_w_out"), "ffn1_in": ("ffn1_w_in",), "ffn1_out": ("ffn1_w_out",),
              "ffn2_in": ("ffn2_w_in",), "ffn2_out": ("ffn2_w_out",)}
    shard = {n: wts[n].astype(BF16) for n in BIG}
    full, parts, gw = {}, {}, {}

    def keys_of(stages):
        return [(n, l) for g, l in stages if l < depth for n in groups[g]]

    def gather_rider(stages):
        return Rider([shard[n][l] for n, l in keys_of(stages)], True)

    def scatter_rider(stages):
        return Rider([gw[k] for k in keys_of(stages)], False)

    def ffn_weights(tag, l):
        return full[tag + "_w_in", l], full[tag + "_w_out", l].reshape(4, fb, d)

    def mixer_weights(l):
        wp = _reorder_w_in([full["w_in", l][j] for j in range(N_DEV)], cols)
        wbf = jnp.concatenate([full["w_branch_fox", l][j] for j in range(N_DEV)], axis=1)
        wbf = jnp.pad(wbf.reshape(8, HEAD_DIM, d), ((0, 0), (0, LANES - HEAD_DIM), (0, 0))).reshape(8 * LANES, d)
        wbs = _permute_heads(jnp.concatenate([full["w_branch_swa", l][j] for j in range(N_DEV)], axis=1), 0)
        return wp, wbf, wbs, full["w_out", l].reshape(d, d)

    full.update(zip(keys_of([("ffn1", 0)]), exchange_hbm(gather_rider([("ffn1", 0)]).srcs, True, "gather_first")))
    meta_all = gather_small(meta_tokens.reshape(1, N_META, -1), "gather_meta")
    meta_full = meta_all.transpose(1, 0, 2).reshape(N_META, d)
    tile8 = lambda g, s=1.0: jnp.tile(g.reshape(1, HEAD_DIM) * s, (1, 8))
    tile2 = lambda g: jnp.tile(g.reshape(1, HEAD_DIM), (1, 2))
    data_lanes = lambda g, s=1.0: _lane_pad(g.reshape(1, HEAD_DIM) * s)
    bias = bias_build(rel_bias_table, "swa_bias")

    first = jnp.concatenate([jnp.zeros((PAD_FRONT, d), F32), meta_full], axis=0)
    h = jnp.concatenate([first, x[0]], axis=0)
    saved, lw = [], []
    for l in range(depth):
        s, w = {"h0": h}, {}
        w["ffn1_in"], w["ffn1_out"] = ffn_weights("ffn1", l)
        stages = [("mix", l)]
        (h, s["n1"], s["a1"], s["fg1"], s["fu1"]), got = ffn_fwd(h, ffn1_norm[l:l + 1], w["ffn1_in"], w["ffn1_out"],
                                                          f"ffn1_fwd_{l}", gather_rider(stages))
        full.update(zip(keys_of(stages), got))
        s["h1"] = h
        w["wp"], w["wbf"], w["wbs"], w["wo"] = mixer_weights(l)
        s["nm"], s["proj"] = mixer_proj(h, mix_norm[l:l + 1], w["wp"], f"mixer_proj_{l}")
        s["gains"] = (data_lanes(fox_q_norm[l], scale), data_lanes(fox_k_norm[l]), tile8(swa_q_norm[l], scale),
                      tile2(swa_k_norm[l]))
        s["fbias"] = _lane_pad(forget_bias[l:l + 1])
        qf, kf, vf, qb, kb, vb = qk_post(s["proj"], s["gains"], s["fbias"], cols, f"qk_post_{l}")
        s.update(qf=qf, kf=kf, vf=vf, qb=qb, kb=kb, vb=vb)
        stages = [("ffn2", l)]
        (s["o_fox"], s["lse_fox"]), got = fox_fwd(qf, kf, vf.T, f"fox_fwd_{l}", gather_rider(stages))
        full.update(zip(keys_of(stages), got))
        s["o_swa"], s["lse_swa"] = swa_fwd(qb, kb, vb, bias, swa_sinks[l], f"swa_fwd_{l}")
        h = branch_out(h, s["o_fox"], s["o_swa"], s["proj"], w["wbf"], w["wbs"], w["wo"], cols, f"branch_out_{l}")
        s["h2"] = h
        w["ffn2_in"], w["ffn2_out"] = ffn_weights("ffn2", l)
        stages = [("ffn1", l + 1)]
        (h, s["n2"], s["a2"], s["fg2"], s["fu2"]), got = ffn_fwd(h, ffn2_norm[l:l + 1], w["ffn2_in"], w["ffn2_out"],
                                                          f"ffn2_fwd_{l}", gather_rider(stages))
        full.update(zip(keys_of(stages), got))
        saved.append(s)
        lw.append(w)

    dh, loss_part = loss_head(h, loss_target[0], "loss_head")

    gs = {n: [None] * depth for n in SMALL}
    dbias_total = None
    for l in reversed(range(depth)):
        w, s = lw[l], saved[l]

        def ffn_back(dh, tag, hin, norm, n_in, a, f_gate, f_up, stages):
            (dh_in, dg, du, dgn, dhs), got = ffn_bwd(dh, hin, norm, f_gate, f_up, w[tag + "_in"], w[tag + "_out"],
                                                     f"{tag}_bwd_{l}", scatter_rider(stages))
            parts.update(zip(keys_of(stages), got))
            gw[tag + "_w_out", l] = matmul_tn(a, dhs[None], f"{tag}_dwo_{l}").reshape(N_DEV, fo, d)
            stages = [(tag + "_out", l)]
            gw[tag + "_w_in", l], got = matmul_tn(n_in[None], dg, f"{tag}_dwi_{l}", y2=du,
                                                  rider=scatter_rider(stages))
            parts.update(zip(keys_of(stages), got))
            return dh_in, dgn

        dh, gs["ffn2_norm"][l] = ffn_back(dh, "ffn2", s["h2"], ffn2_norm[l:l + 1], s["n2"], s["a2"], s["fg2"],
                                          s["fu2"], [("ffn1_in", l + 1)])

        y, dtf, dts, dga, dgb, dof, dos, delta = branch_out_bwd(dh, s["o_fox"], s["o_swa"], s["proj"], w["wbf"],
                                                                w["wbs"], w["wo"], cols, f"branch_out_bwd_{l}")
        gw["w_out", l] = matmul_tn(y[None], dh[None], f"dw_out_{l}").reshape(N_DEV, d // N_DEV, d)
        to_shards = lambda a: a.reshape(512, N_DEV, d // N_DEV).transpose(1, 0, 2)
        gw["w_branch_fox", l] = to_shards(matmul_tn(s["o_fox"][None], dtf[None], f"dw_branch_fox_{l}")[0]
                                          .reshape(8, LANES, d)[:, :HEAD_DIM].reshape(512, d))
        gw["w_branch_swa", l] = to_shards(_permute_heads(
            matmul_tn(s["o_swa"][None], dts[None], f"dw_branch_swa_{l}")[0], 0, inverse=True))

        stages = [("ffn2_in", l)]
        (dkf, dvf, dqf, dc_rows), got = fox_bwd(s["qf"], s["kf"], s["vf"], s["kf"].T, dof, s["lse_fox"],
                                         delta[:, :8].T.reshape(8, 1, m_rows), f"fox_bwd_{l}", scatter_rider(stages))
        parts.update(zip(keys_of(stages), got))
        dc = _lane_pad(dc_rows.reshape(8, m_rows).T)
        dqb, dkb, dvb, dbias, dsink = swa_bwd(s["qb"], s["kb"], s["vb"], bias, swa_sinks[l], s["o_swa"], s["lse_swa"],
                                              dos, f"swa_bwd_{l}")
        dbias_total = dbias if dbias_total is None else dbias_total + dbias
        gs["swa_sinks"][l] = dsink[0, :8]
        dproj, ggqa, ggka, ggqb, ggkb, gfb = qk_post_bwd(s["proj"], s["gains"], s["fbias"], dqf, dkf, dvf, dqb, dkb,
                                                         dvb, dc, dga, dgb, cols, f"qk_post_bwd_{l}")
        gs["fox_q_norm"][l] = ggqa[0, :HEAD_DIM] * scale
        gs["fox_k_norm"][l] = ggka[0, :HEAD_DIM]
        gs["swa_q_norm"][l] = ggqb.reshape(8, HEAD_DIM).sum(0) * scale
        gs["swa_k_norm"][l] = ggkb.reshape(2, HEAD_DIM).sum(0)
        gs["forget_bias"][l] = gfb[0, :8]
        dwp = matmul_tn(s["nm"][None], dproj[None], f"dw_in_{l}", tn=1024 if cols.np % 1024 == 0 else None)[0]
        gw["w_in", l] = _restore_w_in(dwp, cols, din_shard)
        dh, gs["mix_norm"][l] = dproj_bwd(dh, s["h1"], mix_norm[l:l + 1], dproj, w["wp"], f"dproj_bwd_{l}")

        dh, gs["ffn1_norm"][l] = ffn_back(dh, "ffn1", s["h0"], ffn1_norm[l:l + 1], s["n1"], s["a1"], s["fg1"],
                                          s["fu1"], [("mix", l)])

    grad_x = dh[LANES:][None]
    dmeta = dh[PAD_FRONT:LANES]
    dtable = bias_reduce(dbias_total, "swa_dbias")[:, :8]

    last = ("ffn1_w_in", 0)
    send_sems, recv_sems, src_thru, land_thru, token = scatter_start(gw[last], "scatter_last_start")
    big_out = [{}, {}, {}, {}]
    for n in BIG:
        if n != last[0]:
            outs = adamw_sum([parts[n, l] for l in range(depth)], wts[n], mom1[n], mom2[n], f"adamw_{n}", after=token)
            for k in range(4):
                big_out[k][n] = outs[k]
    sent, landed = scatter_wait(send_sems, recv_sems, src_thru, land_thru,
                                [big_out[1][n] for n in BIG if n != last[0]], "scatter_last_wait")
    parts[last] = lax.dynamic_update_slice_in_dim(landed, lax.dynamic_slice_in_dim(sent, dev, 1, axis=0), dev, axis=0)
    outs = adamw_sum([parts[last[0], l] for l in range(depth)], wts[last[0]], mom1[last[0]], mom2[last[0]],
                     f"adamw_{last[0]}")
    for k in range(4):
        big_out[k][last[0]] = outs[k]

    small_g = {n: (jnp.stack(gs[n]) if n != "rel_bias_table" else None) for n in SMALL}
    small_g["rel_bias_table"] = dtable
    pieces = [loss_part[0:1, 0:1].reshape(1, 1)] + [small_g[n].reshape(1, -1) for n in SMALL] + [dmeta.reshape(1, -1)]
    small_shapes = [(1,)] + [wts[n].shape for n in SMALL] + [(N_META, d)]
    total = allsum_small(_pack(pieces, LANES, 8, F32), "allsum_small")
    summed = _unpack(total, small_shapes)
    loss = summed[0][0]
    g_small = dict(zip(SMALL, summed[1:1 + len(SMALL)]))
    g_meta = lax.dynamic_slice_in_dim(summed[-1], dev * (d // N_DEV), d // N_DEV, axis=1)
    names = SMALL + ("meta_tokens",)
    g_small["meta_tokens"] = g_meta
    pk = lambda src: _pack([src[n].reshape(1, -1) for n in names], LANES, 8, F32)[0]
    small_out = [dict(zip(names, _unpack(o, [wts[n].shape for n in names])))
                 for o in adamw_small(pk(g_small), pk(wts), pk(mom1), pk(mom2), "adamw_small")]

    grads = {**big_out[0], **g_small}
    delta = {**big_out[1], **small_out[0]}
    new_m = {**big_out[2], **small_out[1]}
    new_v = {**big_out[3], **small_out[2]}
    return (loss, grad_x, *[grads[n] for n in WEIGHTS], *[delta[n] for n in WEIGHTS],
            *[new_m[n] for n in WEIGHTS], *[new_v[n] for n in WEIGHTS])
```

```python
import math

import numpy as np
import jax
import jax.numpy as jnp
from jax import lax
from jax.experimental import pallas as pl
from jax.experimental.pallas import tpu as pltpu

F32 = jnp.float32
BF16 = jnp.bfloat16
EPS = 1e-6
NEG = -1e30
HEAD_DIM = 64
LANES = 128
N_META = 16
PAD_FRONT = LANES - N_META
N_BUCKETS = 32
MAX_DISTANCE = 128
N_DEV = 8
ADAM_LR, ADAM_B1, ADAM_B2, ADAM_EPS, ADAM_WD, ADAM_STEP = 0.001, 0.9, 0.999, 1e-08, 0.01, 10
VMEM_LIMIT = 56 * 1024 * 1024
MESH = pl.DeviceIdType.MESH


def _params(n_grid):
    return pltpu.CompilerParams(dimension_semantics=("arbitrary",) * n_grid,
                                vmem_limit_bytes=VMEM_LIMIT)


def _dot(a, b):
    return jnp.dot(a, b, preferred_element_type=F32)


def _dot_nt(a, b):
    return lax.dot_general(a, b, (((1,), (1,)), ((), ())), preferred_element_type=F32)


def _dot_tn(a, b):
    return lax.dot_general(a, b, (((0,), (0,)), ((), ())), preferred_element_type=F32)


def _rms(x):
    r = lax.rsqrt(jnp.mean(x * x, axis=-1, keepdims=True) + EPS)
    return x * r, r


def _rms_bwd(x, g, dn):
    xh, r = _rms(x)
    dxh = dn * g
    dx = r * (dxh - xh * jnp.mean(dxh * xh, axis=-1, keepdims=True))
    return dx, jnp.sum(dn * xh, axis=0, keepdims=True)


def _split2(v):
    hi = v.astype(BF16)
    return hi, (v - hi.astype(F32)).astype(BF16)


def _split3(v):
    hi = v.astype(BF16)
    r1 = v - hi.astype(F32)
    mid = r1.astype(BF16)
    return hi, mid, (r1 - mid.astype(F32)).astype(BF16)


def _group_ones():
    r = lax.broadcasted_iota(jnp.int32, (LANES, LANES), 0) // HEAD_DIM
    c = lax.broadcasted_iota(jnp.int32, (LANES, LANES), 1) // HEAD_DIM
    return jnp.where(r == c, 1.0, 0.0).astype(BF16)


def _group_mean(v, ones):
    hi, lo = _split2(v)
    return (_dot(hi, ones) + _dot(lo, ones)) * (1.0 / HEAD_DIM)


def _row_tile(m):
    return 384 if m % 384 == 0 else LANES


def _tile(m, cap):
    return max(t for t in range(16, cap + 1, 16) if m % t == 0)


def _peer(k):
    x, y, c = lax.axis_index("x"), lax.axis_index("y"), lax.axis_index("c")
    px = 1 - x if k & 4 else x
    py = 1 - y if k & 2 else y
    pc = 1 - c if k & 1 else c
    return (px, py, pc), 4 * px + 2 * py + pc


def _exchange_body(src_ref, dst_ref, send_sems, recv_sems, local_sem, bcast):
    x, y, c = lax.axis_index("x"), lax.axis_index("y"), lax.axis_index("c")
    me = 4 * x + 2 * y + c
    mine = pltpu.make_async_copy(src_ref.at[0 if bcast else me], dst_ref.at[me], local_sem)
    mine.start()
    sends = []
    for k in range(1, N_DEV):
        dev, idx = _peer(k)
        cp = pltpu.make_async_remote_copy(
            src_ref=src_ref.at[0 if bcast else idx], dst_ref=dst_ref.at[me],
            send_sem=send_sems.at[k - 1], recv_sem=recv_sems.at[k - 1],
            device_id=dev, device_id_type=MESH)
        cp.start()
        sends.append(cp)
    for k in range(1, N_DEV):
        dev, idx = _peer(k)
        pltpu.make_async_remote_copy(
            src_ref=src_ref.at[0], dst_ref=dst_ref.at[idx],
            send_sem=send_sems.at[k - 1], recv_sem=recv_sems.at[k - 1],
            device_id=dev, device_id_type=MESH).wait_recv()
    for cp in sends:
        cp.wait_send()
    mine.wait()


class Rider:
    FIRST = (1, 2, 4, 6)
    RELAYED = (2, 4, 6)

    def __init__(self, srcs=(), bcast=True):
        self.srcs, self.bcast, self.n = list(srcs), bcast, len(srcs)

    def out_shapes(self):
        return [jax.ShapeDtypeStruct(((N_DEV,) + s.shape) if self.bcast else s.shape, s.dtype) for s in self.srcs]

    def specs(self):
        return [pl.BlockSpec(memory_space=pl.ANY)] * self.n

    def scratch(self):
        if not self.n:
            return []
        return [pltpu.SemaphoreType.DMA((self.n * (N_DEV - 1),)), pltpu.SemaphoreType.DMA((self.n * (N_DEV - 1),)),
                pltpu.SemaphoreType.DMA((self.n,))]

    @staticmethod
    def _copy(src, dst, a, pair, dev, send_sems, recv_sems):
        sem = a * (N_DEV - 1) + pair - 1
        return pltpu.make_async_remote_copy(src_ref=src, dst_ref=dst, send_sem=send_sems.at[sem],
                                            recv_sem=recv_sems.at[sem], device_id=dev, device_id_type=MESH)

    def _first(self):
        return self.FIRST if self.bcast else range(1, N_DEV)

    def _own(self, s, d, a, local_sems):
        me = 4 * lax.axis_index("x") + 2 * lax.axis_index("y") + lax.axis_index("c")
        return pltpu.make_async_copy(s if self.bcast else s.at[me], d.at[me], local_sems.at[a]), me

    def start(self, src_refs, dst_refs, send_sems, recv_sems, local_sems):
        for a, (s, d) in enumerate(zip(src_refs, dst_refs)):
            own, me = self._own(s, d, a, local_sems)
            own.start()
            for k in self._first():
                dev, idx = _peer(k)
                self._copy(s if self.bcast else s.at[idx], d.at[me], a, k, dev, send_sems, recv_sems).start()

    def relay(self, src_refs, dst_refs, send_sems, recv_sems, local_sems):
        if not self.bcast:
            return
        sibling, _ = _peer(1)
        for a, d in enumerate(dst_refs):
            for k in self.RELAYED:
                dev, idx = _peer(k)
                self._copy(d.at[idx], d.at[idx], a, k, dev, send_sems, recv_sems).wait_recv()
                self._copy(d.at[idx], d.at[idx], a, k + 1, sibling, send_sems, recv_sems).start()

    def wait(self, src_refs, dst_refs, send_sems, recv_sems, local_sems):
        sibling, _ = _peer(1)
        for a, (s, d) in enumerate(zip(src_refs, dst_refs)):
            own, me = self._own(s, d, a, local_sems)
            for k in range(1, N_DEV):
                if not (self.bcast and k in self.RELAYED):
                    dev, idx = _peer(k)
                    self._copy(d.at[idx], d.at[idx], a, k, dev, send_sems, recv_sems).wait_recv()
            for k in self._first():
                dev, idx = _peer(k)
                self._copy(s if self.bcast else s.at[idx], d.at[me], a, k, dev, send_sems, recv_sems).wait_send()
            if self.bcast:
                for k in self.RELAYED:
                    dev, idx = _peer(k)
                    self._copy(d.at[idx], d.at[idx], a, k + 1, sibling, send_sems, recv_sems).wait_send()
            own.wait()


def rider_call(core, name, grid, in_specs, out_specs, out_shape, scratch_shapes, args, rider=None):
    rider = rider or Rider()
    n_in, n_out, n_scr, nr = len(in_specs), len(out_specs), len(scratch_shapes), rider.n

    def body(*refs):
        ins, r_src = refs[:n_in], refs[n_in:n_in + nr]
        outs = refs[n_in + nr:n_in + nr + n_out]
        r_dst = refs[n_in + nr + n_out:n_in + 2 * nr + n_out]
        scr = refs[n_in + 2 * nr + n_out:n_in + 2 * nr + n_out + n_scr]
        sems = refs[n_in + 2 * nr + n_out + n_scr:]
        if nr:
            first, relay, last = True, True, True
            for ax, size in enumerate(grid):
                first = first & (pl.program_id(ax) == 0)
                relay = relay & (pl.program_id(ax) == (3 * size // 4 if ax == 0 else 0))
                last = last & (pl.program_id(ax) == size - 1)
            if not grid:
                rider.start(r_src, r_dst, *sems)
                rider.relay(r_src, r_dst, *sems)
            else:
                pl.when(first)(lambda: rider.start(r_src, r_dst, *sems))
                if rider.bcast:
                    pl.when(relay)(lambda: rider.relay(r_src, r_dst, *sems))
        core(*ins, *outs, *scr)
        if nr:
            if not grid:
                rider.wait(r_src, r_dst, *sems)
            else:
                pl.when(last)(lambda: rider.wait(r_src, r_dst, *sems))

    res = pl.pallas_call(
        body, name=name, grid=grid,
        in_specs=list(in_specs) + rider.specs(),
        out_specs=list(out_specs) + rider.specs(),
        out_shape=list(out_shape) + rider.out_shapes(),
        scratch_shapes=list(scratch_shapes) + rider.scratch(),
        compiler_params=_params(len(grid)),
    )(*args, *rider.srcs)
    return res[:n_out], res[n_out:]


def exchange_hbm(srcs, bcast, name):
    return rider_call(lambda: None, name, (), [], [], [], [], [], Rider(srcs, bcast))[1]


_HBM = pl.BlockSpec(memory_space=pltpu.HBM)
_SEM = pl.BlockSpec(memory_space=pltpu.SEMAPHORE)
_EFFECT = pltpu.CompilerParams(has_side_effects=pltpu.SideEffectType.DATAFLOW_SIDE_EFFECTING)


def scatter_start(src, name):
    def body(src_ref, land_ref, send_sems, recv_sems, src_thru, land_thru, token):
        me = 4 * lax.axis_index("x") + 2 * lax.axis_index("y") + lax.axis_index("c")
        for k in range(1, N_DEV):
            dev, idx = _peer(k)
            pltpu.make_async_remote_copy(src_ref=src_ref.at[idx], dst_ref=land_ref.at[me], send_sem=send_sems.at[k - 1],
                                         recv_sem=recv_sems.at[k - 1], device_id=dev, device_id_type=MESH).start()
        token[...] = jnp.zeros_like(token)

    return pl.pallas_call(
        body, name=name,
        out_shape=(pltpu.SemaphoreType.DMA((N_DEV - 1,)), pltpu.SemaphoreType.DMA((N_DEV - 1,)),
                   pltpu.HBM(src.shape, src.dtype), pltpu.HBM(src.shape, src.dtype), jax.ShapeDtypeStruct((8, LANES), F32)),
        in_specs=(_HBM, _HBM), out_specs=(_SEM, _SEM, _HBM, _HBM, pl.BlockSpec(memory_space=pltpu.VMEM)),
        input_output_aliases={0: 2, 1: 3}, compiler_params=_EFFECT,
    )(pltpu.with_memory_space_constraint(src, pltpu.HBM),
      pltpu.with_memory_space_constraint(lax.empty(src.shape, src.dtype), pltpu.HBM))


def scatter_wait(send_sems, recv_sems, src_thru, land_thru, after, name):
    n_after = len(after)

    def body(*refs):
        src_ref, land_ref, send_sems, recv_sems = refs[:4]
        for k in range(1, N_DEV):
            dev, idx = _peer(k)
            copy = pltpu.make_async_remote_copy(src_ref=src_ref.at[idx], dst_ref=land_ref.at[idx],
                                                send_sem=send_sems.at[k - 1], recv_sem=recv_sems.at[k - 1],
                                                device_id=dev, device_id_type=MESH)
            copy.wait_send()
            copy.wait_recv()

    return pl.pallas_call(
        body, name=name,
        out_shape=(pltpu.HBM(src_thru.shape, src_thru.dtype), pltpu.HBM(land_thru.shape, land_thru.dtype)),
        in_specs=(_HBM, _HBM, _SEM, _SEM) + (pl.BlockSpec(memory_space=pl.ANY),) * n_after, out_specs=(_HBM, _HBM),
        input_output_aliases={0: 0, 1: 1}, compiler_params=_EFFECT,
    )(src_thru, land_thru, send_sems, recv_sems, *after)


def allsum_small(vec, name):
    def body(src_ref, out_ref, dst_ref, send_sems, recv_sems, local_sem):
        _exchange_body(src_ref, dst_ref, send_sems, recv_sems, local_sem, True)
        acc = dst_ref[0]
        for j in range(1, N_DEV):
            acc = acc + dst_ref[j]
        out_ref[...] = acc

    return pl.pallas_call(
        body, name=name,
        out_shape=jax.ShapeDtypeStruct(vec.shape[1:], F32),
        in_specs=[pl.BlockSpec(memory_space=pltpu.VMEM)],
        out_specs=pl.BlockSpec(memory_space=pltpu.VMEM),
        scratch_shapes=[pltpu.VMEM((N_DEV,) + vec.shape[1:], F32),
                        pltpu.SemaphoreType.DMA((N_DEV - 1,)), pltpu.SemaphoreType.DMA((N_DEV - 1,)),
                        pltpu.SemaphoreType.DMA],
    )(vec)


def gather_small(vec, name):
    def body(src_ref, dst_ref, send_sems, recv_sems, local_sem):
        _exchange_body(src_ref, dst_ref, send_sems, recv_sems, local_sem, True)

    return pl.pallas_call(
        body, name=name,
        out_shape=jax.ShapeDtypeStruct((N_DEV,) + vec.shape[1:], F32),
        in_specs=[pl.BlockSpec(memory_space=pltpu.VMEM)],
        out_specs=pl.BlockSpec(memory_space=pltpu.VMEM),
        scratch_shapes=[pltpu.SemaphoreType.DMA((N_DEV - 1,)), pltpu.SemaphoreType.DMA((N_DEV - 1,)),
                        pltpu.SemaphoreType.DMA],
    )(vec)


FFN_FWD_ROWS = 1056
FFN_BWD_ROWS = 704
DW_ROWS = 1408

def ffn_fwd(h, g, w_in8, w_out4, name, rider=None):
    m, d = h.shape
    fb = w_in8.shape[1]
    tm = _tile(m, FFN_FWD_ROWS)

    def body(h_ref, g_ref, wg_ref, wu_ref, wo_ref, hn_ref, n_ref, a_ref, fg_ref, fu_ref, acc_ref):
        i = pl.program_id(1)

        @pl.when(i == 0)
        def _():
            xh, _ = _rms(h_ref[...])
            n_ref[...] = (xh * g_ref[...]).astype(BF16)
            acc_ref[...] = jnp.zeros_like(acc_ref)

        n = n_ref[...]
        gate = _dot_nt(n, wg_ref[0])
        up = _dot_nt(n, wu_ref[0])
        sg = jax.nn.sigmoid(gate)
        silu = gate * sg
        a = (silu * up).astype(BF16)
        a_ref[0] = a
        fg_ref[0] = (up * (sg * (1.0 + gate * (1.0 - sg)))).astype(BF16)
        fu_ref[0] = silu.astype(BF16)
        acc_ref[...] += _dot(a, wo_ref[0])

        @pl.when(i == 3)
        def _():
            hn_ref[...] = h_ref[...] + 0.5 * acc_ref[...]

    return rider_call(
        body, name, (m // tm, 4),
        in_specs=[pl.BlockSpec((tm, d), lambda r, i: (r, 0)),
                  pl.BlockSpec((1, d), lambda r, i: (0, 0)),
                  pl.BlockSpec((1, fb, d), lambda r, i: (i, 0, 0)),
                  pl.BlockSpec((1, fb, d), lambda r, i: (i + 4, 0, 0)),
                  pl.BlockSpec((1, fb, d), lambda r, i: (i, 0, 0))],
        out_specs=[pl.BlockSpec((tm, d), lambda r, i: (r, 0)),
                   pl.BlockSpec((tm, d), lambda r, i: (r, 0))] + [pl.BlockSpec((1, tm, fb), lambda r, i: (i, r, 0))] * 3,
        out_shape=[jax.ShapeDtypeStruct((m, d), F32), jax.ShapeDtypeStruct((m, d), BF16)]
                  + [jax.ShapeDtypeStruct((4, m, fb), BF16)] * 3,
        scratch_shapes=[pltpu.VMEM((tm, d), F32)],
        args=(h, g, w_in8, w_in8, w_out4), rider=rider)


def ffn_bwd(dh, h, g, f_gate, f_up, w_in8, w_out4, name, rider=None):
    m, d = h.shape
    fb = w_in8.shape[1]
    tm = _tile(m, FFN_BWD_ROWS)

    def body(dh_ref, h_ref, g_ref, fg_ref, fu_ref, wg_ref, wu_ref, wo_ref,
             dhin_ref, dg_ref, du_ref, dgn_ref, dhs_ref, acc_ref):
        r = pl.program_id(0)
        i = pl.program_id(1)

        @pl.when(i == 0)
        def _():
            dhs_ref[...] = (0.5 * dh_ref[...]).astype(BF16)
            acc_ref[...] = jnp.zeros_like(acc_ref)

        @pl.when((r == 0) & (i == 0))
        def _():
            dgn_ref[...] = jnp.zeros_like(dgn_ref)

        da = _dot_nt(dhs_ref[...], wo_ref[0])
        dub = (da * fu_ref[0]).astype(BF16)
        dgb = (da * fg_ref[0]).astype(BF16)
        dg_ref[0] = dgb
        du_ref[0] = dub
        acc_ref[...] += _dot(dgb, wg_ref[0]) + _dot(dub, wu_ref[0])

        @pl.when(i == 3)
        def _():
            dx, dgain = _rms_bwd(h_ref[...], g_ref[...], acc_ref[...])
            dgn_ref[...] += dgain
            dhin_ref[...] = dh_ref[...] + dx

    row = lambda r, i: (r, 0)
    blk = lambda r, i: (i, r, 0)
    return rider_call(
        body, name, (m // tm, 4),
        in_specs=[pl.BlockSpec((tm, d), row), pl.BlockSpec((tm, d), row),
                  pl.BlockSpec((1, d), lambda r, i: (0, 0)),
                  pl.BlockSpec((1, tm, fb), blk), pl.BlockSpec((1, tm, fb), blk),
                  pl.BlockSpec((1, fb, d), lambda r, i: (i, 0, 0)),
                  pl.BlockSpec((1, fb, d), lambda r, i: (i + 4, 0, 0)),
                  pl.BlockSpec((1, fb, d), lambda r, i: (i, 0, 0))],
        out_specs=[pl.BlockSpec((tm, d), row),
                   pl.BlockSpec((1, tm, fb), blk), pl.BlockSpec((1, tm, fb), blk),
                   pl.BlockSpec((1, d), lambda r, i: (0, 0)),
                   pl.BlockSpec((tm, d), row)],
        out_shape=[jax.ShapeDtypeStruct((m, d), F32),
                   jax.ShapeDtypeStruct((4, m, fb), BF16), jax.ShapeDtypeStruct((4, m, fb), BF16),
                   jax.ShapeDtypeStruct((1, d), F32), jax.ShapeDtypeStruct((m, d), BF16)],
        scratch_shapes=[pltpu.VMEM((tm, d), F32)],
        args=(dh, h, g, f_gate, f_up, w_in8, w_in8, w_out4), rider=rider)


def matmul_tn(x, y, name, tn=None, x2=None, rider=None):
    bx, m, k = x.shape
    by, _, n = y.shape
    b = max(bx, by) * (2 if x2 is not None else 1)
    tm = _tile(m, DW_ROWS)
    tn = n if tn is None else tn
    nt = n // tn
    nr = m // tm

    def body(*refs):
        x_ref, y_ref = refs[0], refs[-3]
        o_ref, acc_ref = refs[-2], refs[-1]
        r = pl.program_id(2)

        @pl.when(r == 0)
        def _():
            acc_ref[...] = jnp.zeros_like(acc_ref)

        if x2 is None:
            acc_ref[...] += _dot_tn(x_ref[0].astype(BF16), y_ref[0].astype(BF16))
        else:
            @pl.when(pl.program_id(0) < bx)
            def _():
                acc_ref[...] += _dot_tn(x_ref[0].astype(BF16), y_ref[0].astype(BF16))

            @pl.when(pl.program_id(0) >= bx)
            def _():
                acc_ref[...] += _dot_tn(refs[1][0].astype(BF16), y_ref[0].astype(BF16))

        @pl.when(r == nr - 1)
        def _():
            o_ref[0] = acc_ref[...].astype(BF16)

    if x2 is None:
        x_specs = [pl.BlockSpec((1, tm, k), (lambda i, j, r: (i, r, 0)) if bx > 1 else (lambda i, j, r: (0, r, 0)))]
    else:
        x_specs = [pl.BlockSpec((1, tm, k), lambda i, j, r: (jnp.minimum(i, bx - 1), jnp.where(i < bx, r, nr - 1), 0)),
                   pl.BlockSpec((1, tm, k), lambda i, j, r: (jnp.maximum(i - bx, 0), jnp.where(i < bx, 0, r), 0))]
    y_map = (lambda i, j, r: (i, r, j)) if by > 1 else (lambda i, j, r: (0, r, j))
    (out,), carried = rider_call(
        body, name, (b, nt, nr),
        in_specs=x_specs + [pl.BlockSpec((1, tm, tn), y_map)],
        out_specs=[pl.BlockSpec((1, k, tn), lambda i, j, r: (i, 0, j))],
        out_shape=[jax.ShapeDtypeStruct((b, k, n), BF16)],
        scratch_shapes=[pltpu.VMEM((k, tn), F32)],
        args=[x] + ([x2] if x2 is not None else []) + [y], rider=rider)
    return (out, carried) if rider is not None else out


AUG = HEAD_DIM


class _Cols:
    def __init__(self, d):
        self.d = d
        self.ga, self.gb = 0, d
        self.qa, self.ka, self.va = 2 * d, 2 * d + 1024, 2 * d + 2048
        self.qb = 2 * d + 3072
        self.kb, self.vb, self.fa = self.qb + 512, self.qb + 640, self.qb + 768
        self.np = self.qb + 1024


def mixer_proj(h, g, wp, name):
    m, d = h.shape
    npad = wp.shape[1]
    tm = _row_tile(m)

    def body(h_ref, g_ref, w_ref, n_ref, p_ref):
        xh, _ = _rms(h_ref[...])
        n = (xh * g_ref[...]).astype(BF16)
        n_ref[...] = n
        p_ref[...] = _dot(n, w_ref[...])

    return pl.pallas_call(
        body, name=name, grid=(m // tm,),
        in_specs=[pl.BlockSpec((tm, d), lambda r: (r, 0)), pl.BlockSpec((1, d), lambda r: (0, 0)),
                  pl.BlockSpec((d, npad), lambda r: (0, 0))],
        out_specs=[pl.BlockSpec((tm, d), lambda r: (r, 0)), pl.BlockSpec((tm, npad), lambda r: (r, 0))],
        out_shape=[jax.ShapeDtypeStruct((m, d), BF16), jax.ShapeDtypeStruct((m, npad), F32)],
        compiler_params=_params(1),
    )(h, g, wp)


def _head_norm(x, gain, ones):
    outs = []
    for b in range(x.shape[1] // LANES):
        xb = x[:, b * LANES:(b + 1) * LANES]
        r = lax.rsqrt(_group_mean(xb * xb, ones) + EPS)
        outs.append(xb * r * gain[:, b * LANES:(b + 1) * LANES])
    return outs


def _head_norm_bwd(x, gain, dn, ones):
    dxs, dgs = [], []
    for b in range(x.shape[1] // LANES):
        sl = slice(b * LANES, (b + 1) * LANES)
        xb, dnb = x[:, sl], dn[:, sl]
        r = lax.rsqrt(_group_mean(xb * xb, ones) + EPS)
        xh = xb * r
        dxh = dnb * gain[:, sl]
        dxs.append(r * (dxh - xh * _group_mean(dxh * xh, ones)))
        dgs.append(jnp.sum(dnb * xh, axis=0, keepdims=True))
    return dxs, dgs


def _lane_col(v, lane_iota, idx):
    return jnp.sum(jnp.where(lane_iota == idx, v, 0.0), axis=1, keepdims=True)


def _aug(base, lane, vals):
    for i, v in enumerate(vals):
        base = jnp.where(lane == AUG + i, v, base)
    return base


def qk_post(proj, gains, fbias, cols, name):
    m = proj.shape[0]
    tm = _row_tile(m)
    gqa, gka, gqb, gkb = gains

    def body(qa_ref, ka_ref, va_ref, qb_ref, kb_ref, vb_ref, fa_ref, gqa_ref, gka_ref, gqb_ref, gkb_ref, fb_ref,
             qf_o, kf_o, vf_o, qb_o, kb_o, vb_o, carry_ref):
        r0 = pl.program_id(0)

        @pl.when(r0 == 0)
        def _():
            carry_ref[...] = jnp.zeros_like(carry_ref)

        z = fa_ref[...] + fb_ref[...]
        logf = jnp.minimum(z, 0.0) - jnp.log(1.0 + jnp.exp(-jnp.abs(z)))
        rr = lax.broadcasted_iota(jnp.int32, (tm, tm), 0)
        cc = lax.broadcasted_iota(jnp.int32, (tm, tm), 1)
        tril = jnp.where(cc <= rr, 1.0, 0.0).astype(BF16)
        p0, p1, p2 = _split3(logf)
        c = _dot(tril, p0) + _dot(tril, p1) + _dot(tril, p2) + carry_ref[...]
        carry_ref[...] += jnp.sum(logf, axis=0, keepdims=True)

        lane = lax.broadcasted_iota(jnp.int32, (tm, LANES), 1)
        is_pad = (r0 * tm + lax.broadcasted_iota(jnp.int32, (tm, 1), 0)) < PAD_FRONT
        ones = jnp.ones((LANES, LANES), BF16)
        for hd in range(8):
            sl = slice(hd * LANES, (hd + 1) * LANES)
            ch = _lane_col(c, lane, hd)
            ct = [p.astype(F32) for p in _split3(ch)]
            cs = [p.astype(F32) for p in _split3(-jnp.where(is_pad, -NEG, ch))]
            xq = qa_ref[:, sl]
            qn = xq * lax.rsqrt(_group_mean(xq * xq, ones) + EPS) * gqa_ref[...]
            qf_o[:, sl] = _aug(qn, lane, ct + [1.0, 1.0, 1.0]).astype(BF16)
            xk = ka_ref[:, sl]
            kn = xk * lax.rsqrt(_group_mean(xk * xk, ones) + EPS) * gka_ref[...]
            kf_o[:, sl] = _aug(kn, lane, [1.0, 1.0, 1.0] + cs).astype(BF16)
            vf_o[:, sl] = _aug(va_ref[:, sl], lane, [1.0, 1.0, 1.0]).astype(BF16)

        gones = _group_ones()
        for src, gn, dst in ((qb_ref, gqb_ref, qb_o), (kb_ref, gkb_ref, kb_o)):
            for b, blk in enumerate(_head_norm(src[...], gn[...], gones)):
                dst[:, b * LANES:(b + 1) * LANES] = blk.astype(BF16)
        vb_o[...] = vb_ref[...].astype(BF16)

    w1024 = lambda off: pl.BlockSpec((tm, 1024), lambda r, o=off // 1024: (r, o))
    w512 = lambda off: pl.BlockSpec((tm, 512), lambda r, o=off // 512: (r, o))
    w128 = lambda off: pl.BlockSpec((tm, LANES), lambda r, o=off // LANES: (r, o))
    vec = lambda w: pl.BlockSpec((1, w), lambda r: (0, 0))
    row = lambda w: pl.BlockSpec((tm, w), lambda r: (r, 0))
    return pl.pallas_call(
        body, name=name, grid=(m // tm,),
        in_specs=[w1024(cols.qa), w1024(cols.ka), w1024(cols.va), w512(cols.qb), w128(cols.kb), w128(cols.vb),
                  w128(cols.fa), vec(LANES), vec(LANES), vec(512), vec(LANES), vec(LANES)],
        out_specs=[row(1024), row(1024), row(1024), row(512), row(LANES), row(LANES)],
        out_shape=[jax.ShapeDtypeStruct((m, 1024), BF16)] * 3 + [jax.ShapeDtypeStruct((m, 512), BF16)]
                  + [jax.ShapeDtypeStruct((m, LANES), BF16)] * 2,
        scratch_shapes=[pltpu.VMEM((1, LANES), F32)],
        compiler_params=_params(1),
    )(proj, proj, proj, proj, proj, proj, proj, gqa, gka, gqb, gkb, fbias)


def qk_post_bwd(proj, gains, fbias, dqf, dkf, dvf, dqb, dkb, dvb, dc, dga, dgb, cols, name):
    m = proj.shape[0]
    d = cols.d
    tm = _row_tile(m)
    nt = m // tm
    gqa, gka, gqb, gkb = gains

    def body(qa_ref, ka_ref, qb_ref, kb_ref, fa_ref, gqa_ref, gka_ref, gqb_ref, gkb_ref, fb_ref,
             dqf_ref, dkf_ref, dvf_ref, dqb_ref, dkb_ref, dvb_ref, dc_ref, dga_ref, dgb_ref,
             dp_o, ggqa_o, ggka_o, ggqb_o, ggkb_o, gfb_o, carry_ref):
        @pl.when(pl.program_id(0) == 0)
        def _():
            carry_ref[...] = jnp.zeros_like(carry_ref)
            for o in (ggqa_o, ggka_o, ggqb_o, ggkb_o, gfb_o):
                o[...] = jnp.zeros_like(o)

        dp_o[:, cols.ga:cols.ga + d] = dga_ref[...].astype(BF16)
        dp_o[:, cols.gb:cols.gb + d] = dgb_ref[...].astype(BF16)
        dp_o[:, cols.fa + LANES:cols.np] = jnp.zeros((tm, cols.np - cols.fa - LANES), BF16)
        lane = lax.broadcasted_iota(jnp.int32, (tm, LANES), 1)
        data = lane < HEAD_DIM
        ones = jnp.ones((LANES, LANES), BF16)
        for hd in range(8):
            sl = slice(hd * LANES, (hd + 1) * LANES)
            for src, gn, dn_ref, off, gout in ((qa_ref, gqa_ref, dqf_ref, cols.qa, ggqa_o),
                                               (ka_ref, gka_ref, dkf_ref, cols.ka, ggka_o)):
                x = src[:, sl]
                dn = jnp.where(data, dn_ref[:, sl], 0.0)
                r = lax.rsqrt(_group_mean(x * x, ones) + EPS)
                xh = x * r
                dxh = dn * gn[...]
                dp_o[:, off + hd * LANES:off + (hd + 1) * LANES] = (
                    r * (dxh - xh * _group_mean(dxh * xh, ones))).astype(BF16)
                gout[...] += jnp.sum(dn * xh, axis=0, keepdims=True)
            dp_o[:, cols.va + hd * LANES:cols.va + (hd + 1) * LANES] = jnp.where(data, dvf_ref[:, sl], 0.0).astype(BF16)
        dp_o[:, cols.vb:cols.vb + LANES] = dvb_ref[...].astype(BF16)
        gones = _group_ones()
        for src, gn, dn, off, gout in ((qb_ref, gqb_ref, dqb_ref, cols.qb, ggqb_o),
                                       (kb_ref, gkb_ref, dkb_ref, cols.kb, ggkb_o)):
            dxs, dgs = _head_norm_bwd(src[...], gn[...], dn[...], gones)
            for b, (dx, dg) in enumerate(zip(dxs, dgs)):
                dp_o[:, off + b * LANES:off + (b + 1) * LANES] = dx.astype(BF16)
                gout[:, b * LANES:(b + 1) * LANES] += dg
        dcv = dc_ref[...]
        rr = lax.broadcasted_iota(jnp.int32, (tm, tm), 0)
        cc = lax.broadcasted_iota(jnp.int32, (tm, tm), 1)
        triu = jnp.where(cc >= rr, 1.0, 0.0).astype(BF16)
        p0, p1, p2 = _split3(dcv)
        dlogf = _dot(triu, p0) + _dot(triu, p1) + _dot(triu, p2) + carry_ref[...]
        carry_ref[...] += jnp.sum(dcv, axis=0, keepdims=True)
        z = fa_ref[...] + fb_ref[...]
        row = (nt - 1 - pl.program_id(0)) * tm + lax.broadcasted_iota(jnp.int32, (tm, LANES), 0)
        dfa = jnp.where(row >= PAD_FRONT, dlogf * jax.nn.sigmoid(-z), 0.0)
        dp_o[:, cols.fa:cols.fa + LANES] = dfa.astype(BF16)
        gfb_o[...] += jnp.sum(dfa, axis=0, keepdims=True)

    rev = lambda r: nt - 1 - r
    w1024 = lambda off: pl.BlockSpec((tm, 1024), lambda r, o=off // 1024: (rev(r), o))
    w512 = lambda off: pl.BlockSpec((tm, 512), lambda r, o=off // 512: (rev(r), o))
    w128 = lambda off: pl.BlockSpec((tm, LANES), lambda r, o=off // LANES: (rev(r), o))
    vec = lambda w: pl.BlockSpec((1, w), lambda r: (0, 0))
    row = lambda w: pl.BlockSpec((tm, w), lambda r: (rev(r), 0))
    return pl.pallas_call(
        body, name=name, grid=(nt,),
        in_specs=[w1024(cols.qa), w1024(cols.ka), w512(cols.qb), w128(cols.kb), w128(cols.fa),
                  vec(LANES), vec(LANES), vec(512), vec(LANES), vec(LANES),
                  row(1024), row(1024), row(1024), row(512), row(LANES), row(LANES), row(LANES), row(d), row(d)],
        out_specs=[row(cols.np), vec(LANES), vec(LANES), vec(512), vec(LANES), vec(LANES)],
        out_shape=[jax.ShapeDtypeStruct((m, cols.np), BF16), jax.ShapeDtypeStruct((1, LANES), F32),
                   jax.ShapeDtypeStruct((1, LANES), F32), jax.ShapeDtypeStruct((1, 512), F32),
                   jax.ShapeDtypeStruct((1, LANES), F32), jax.ShapeDtypeStruct((1, LANES), F32)],
        scratch_shapes=[pltpu.VMEM((1, LANES), F32)],
        compiler_params=_params(1),
    )(proj, proj, proj, proj, proj, gqa, gka, gqb, gkb, fbias, dqf, dkf, dvf, dqb, dkb, dvb, dc, dga, dgb)


def dproj_bwd(dh, h, g, dproj, wp, name):
    m, d = h.shape
    npad = wp.shape[1]
    tm = _row_tile(m)

    def body(dh_ref, h_ref, g_ref, dp_ref, w_ref, dhin_ref, dgn_ref):
        @pl.when(pl.program_id(0) == 0)
        def _():
            dgn_ref[...] = jnp.zeros_like(dgn_ref)

        dn = _dot_nt(dp_ref[...], w_ref[...])
        dx, dgain = _rms_bwd(h_ref[...], g_ref[...], dn)
        dgn_ref[...] += dgain
        dhin_ref[...] = dh_ref[...] + dx

    row = lambda w: pl.BlockSpec((tm, w), lambda r: (r, 0))
    return pl.pallas_call(
        body, name=name, grid=(m // tm,),
        in_specs=[row(d), row(d), pl.BlockSpec((1, d), lambda r: (0, 0)), row(npad),
                  pl.BlockSpec((d, npad), lambda r: (0, 0))],
        out_specs=[row(d), pl.BlockSpec((1, d), lambda r: (0, 0))],
        out_shape=[jax.ShapeDtypeStruct((m, d), F32), jax.ShapeDtypeStruct((1, d), F32)],
        compiler_params=_params(1),
    )(dh, h, g, dproj, wp)


def _causal_t(t):
    return lax.broadcasted_iota(jnp.int32, (t, t), 0) <= lax.broadcasted_iota(jnp.int32, (t, t), 1)


HEADS_PER_STEP = 4


def fox_fwd(qf, kf, vt, name, rider=None):
    m = qf.shape[0]
    t = _row_tile(m)
    nq = m // t
    hp = HEADS_PER_STEP
    w = hp * LANES

    def body(q_ref, k_ref, vt_ref, o_ref, lse_ref, acc_ref, m_ref, p_ref, a_ref):
        qi = pl.program_id(1)
        acc_ref[...] = jnp.zeros_like(acc_ref)
        m_ref[...] = jnp.full_like(m_ref, NEG)

        def scores(ki, slot, mask):
            off = pl.multiple_of(ki * t, t)
            for e in range(hp):
                sl = slice(e * LANES, (e + 1) * LANES)
                s = _dot_nt(k_ref[pl.ds(off, t), sl], q_ref[:, sl])
                if mask is not None:
                    s = jnp.where(mask, s, NEG)
                m_old = m_ref[e]
                m_new = jnp.maximum(m_old, jnp.max(s, axis=0, keepdims=True))
                p_ref[slot, e] = jnp.exp(s - m_new).astype(BF16)
                a_ref[slot, e] = jnp.exp(m_old - m_new)
                m_ref[e] = m_new

        def values(ki, slot):
            off = pl.multiple_of(ki * t, t)
            for e in range(hp):
                sl = slice(e * LANES, (e + 1) * LANES)
                acc_ref[e] = acc_ref[e] * a_ref[slot, e] + _dot(vt_ref[sl, pl.ds(off, t)], p_ref[slot, e])

        causal = _causal_t(t)
        scores(0, 0, causal | (jnp.full((t, t), qi, jnp.int32) > 0))

        def step(ki, carry):
            values(ki - 1, (ki - 1) % 2)
            scores(ki, ki % 2, None)
            return carry

        lax.fori_loop(1, qi, step, 0)

        @pl.when(qi >= 1)
        def _():
            values(qi - 1, (qi - 1) % 2)
            scores(qi, qi % 2, causal)

        values(qi, qi % 2)
        row = lax.broadcasted_iota(jnp.int32, (LANES, t), 0)
        for e in range(hp):
            l = jnp.max(acc_ref[e, AUG:AUG + 8, :], axis=0, keepdims=True)
            o_ref[:, e * LANES:(e + 1) * LANES] = jnp.where(row < HEAD_DIM, acc_ref[e] * (1.0 / l), 0.0).T
            lse_ref[e] = m_ref[e] + jnp.log(l)

    return rider_call(
        body, name, (8 // hp, nq),
        in_specs=[pl.BlockSpec((t, w), lambda hd, i: (i, hd)),
                  pl.BlockSpec((m, w), lambda hd, i: (0, hd)),
                  pl.BlockSpec((w, m), lambda hd, i: (hd, 0))],
        out_specs=[pl.BlockSpec((t, w), lambda hd, i: (i, hd)),
                   pl.BlockSpec((hp, 1, t), lambda hd, i: (hd, 0, i))],
        out_shape=[jax.ShapeDtypeStruct((m, 8 * LANES), F32), jax.ShapeDtypeStruct((8, 1, m), F32)],
        scratch_shapes=[pltpu.VMEM((hp, LANES, t), F32), pltpu.VMEM((hp, 1, t), F32),
                        pltpu.VMEM((2, hp, t, t), BF16), pltpu.VMEM((2, hp, 1, t), F32)],
        args=(qf, kf, vt), rider=rider)


def fox_bwd(qf, kf, vf, kt, dof, lse, delta, name, rider=None):
    m = qf.shape[0]
    t = _row_tile(m)
    nq = m // t
    hp = HEADS_PER_STEP
    w = hp * LANES

    def body(k_ref, v_ref, kt_ref, q_ref, do_ref, lse_ref, delta_ref, dk_ref, dv_ref, dq_ref, dck_ref, dcq_ref,
             dka_ref, dva_ref, dqt_ref):
        ki = pl.program_id(1)

        @pl.when(ki == 0)
        def _():
            dqt_ref[...] = jnp.zeros_like(dqt_ref)

        dka_ref[...] = jnp.zeros_like(dka_ref)
        dva_ref[...] = jnp.zeros_like(dva_ref)

        def tile(qi, diagonal):
            off = pl.multiple_of(qi * t, t)
            for e in range(hp):
                sl = slice(e * LANES, (e + 1) * LANES)
                q = q_ref[pl.ds(off, t), sl]
                do = do_ref[pl.ds(off, t), sl]
                s = _dot_nt(k_ref[:, sl], q)
                if diagonal:
                    s = jnp.where(_causal_t(t), s, NEG)
                p = jnp.exp(s - lse_ref[e, :, pl.ds(off, t)])
                ds = (p * (_dot_nt(v_ref[:, sl], do) - delta_ref[e, :, pl.ds(off, t)])).astype(BF16)
                dva_ref[:, sl] += _dot(p.astype(BF16), do)
                dka_ref[:, sl] += _dot(ds, q)
                dqt_ref[sl, pl.ds(off, t)] += _dot(kt_ref[sl, :], ds)

        def step(qi, carry):
            tile(qi, False)
            return carry

        tile(ki, True)
        lax.fori_loop(ki + 1, nq, step, 0)
        dk_ref[...] = dka_ref[...]
        dv_ref[...] = dva_ref[...]
        row8 = lax.broadcasted_iota(jnp.int32, (8, 1), 0)
        for e in range(hp):
            slab = dka_ref[:, e * LANES:(e + 1) * LANES].T[AUG:AUG + 8, :]
            dck_ref[e] = -jnp.sum(jnp.where(row8 == 3, slab, 0.0), axis=0, keepdims=True)

        @pl.when(ki == nq - 1)
        def _():
            for e in range(hp):
                sl = slice(e * LANES, (e + 1) * LANES)
                slab = dqt_ref[e * LANES + AUG:e * LANES + AUG + 8, :]
                dcq_ref[e] = jnp.sum(jnp.where(row8 == 0, slab, 0.0), axis=0, keepdims=True)
                for j in range(nq):
                    dq_ref[j * t:(j + 1) * t, sl] = dqt_ref[sl, j * t:(j + 1) * t].T

    tile_spec = pl.BlockSpec((t, w), lambda hd, i: (i, hd))
    full = pl.BlockSpec((m, w), lambda hd, i: (0, hd))
    stat = pl.BlockSpec((hp, 1, m), lambda hd, i: (hd, 0, 0))
    (dkf, dvf, dqf, dck, dcq), carried = rider_call(
        body, name, (8 // hp, nq),
        in_specs=[tile_spec, tile_spec, pl.BlockSpec((w, t), lambda hd, i: (hd, i)), full, full, stat, stat],
        out_specs=[tile_spec, tile_spec, full, pl.BlockSpec((hp, 1, t), lambda hd, i: (hd, 0, i)), stat],
        out_shape=[jax.ShapeDtypeStruct((m, 8 * LANES), F32), jax.ShapeDtypeStruct((m, 8 * LANES), F32),
                   jax.ShapeDtypeStruct((m, 8 * LANES), F32), jax.ShapeDtypeStruct((8, 1, m), F32),
                   jax.ShapeDtypeStruct((8, 1, m), F32)],
        scratch_shapes=[pltpu.VMEM((t, w), F32), pltpu.VMEM((t, w), F32), pltpu.VMEM((w, m), F32)],
        args=(kf, vf, kt, qf, dof, lse, delta), rider=rider)
    return (dkf, dvf, dqf, dcq + dck), carried


def _bucket_ids():
    def bucket(dist):
        n = np.maximum(dist, 0)
        max_exact = N_BUCKETS // 2
        nf = np.maximum(n, 1).astype(np.float32)
        large = max_exact + (np.log(nf / max_exact) / math.log(MAX_DISTANCE / max_exact)
                             * (N_BUCKETS - max_exact)).astype(np.int32)
        return np.where(n < max_exact, n, np.minimum(large, N_BUCKETS - 1))

    tl = np.arange(LANES)[:, None]
    sl = np.arange(LANES)[None, :]
    prev = bucket(LANES + tl - sl)
    cur = bucket(tl - sl)
    meta = np.full((LANES, LANES), N_BUCKETS - 1)
    return np.concatenate([prev, cur, meta], axis=1).astype(np.int32)


def bias_build(table, name):
    ids = jnp.asarray(_bucket_ids())

    def body(t_ref, id_ref, o_ref):
        idv = id_ref[...]
        for h in range(8):
            acc = jnp.zeros((LANES, 3 * LANES), F32)
            for b in range(N_BUCKETS):
                acc = jnp.where(idv == b, t_ref[b, h], acc)
            o_ref[h] = acc

    return pl.pallas_call(
        body, name=name,
        in_specs=[pl.BlockSpec(memory_space=pltpu.SMEM), pl.BlockSpec(memory_space=pltpu.VMEM)],
        out_specs=pl.BlockSpec(memory_space=pltpu.VMEM),
        out_shape=jax.ShapeDtypeStruct((8, LANES, 3 * LANES), F32),
    )(table, ids)


def bias_reduce(dbias, name):
    ids = jnp.asarray(_bucket_ids())

    def body(d_ref, id_ref, o_ref):
        idv = id_ref[...]
        rr = lax.broadcasted_iota(jnp.int32, (N_BUCKETS, LANES), 0)
        cc = lax.broadcasted_iota(jnp.int32, (N_BUCKETS, LANES), 1)
        acc = jnp.zeros((N_BUCKETS, LANES), F32)
        for h in range(8):
            dv = d_ref[h]
            for b in range(N_BUCKETS):
                val = jnp.sum(jnp.where(idv == b, dv, 0.0), keepdims=True)
                acc = jnp.where((rr == b) & (cc == h), val, acc)
        o_ref[...] = acc

    return pl.pallas_call(
        body, name=name,
        in_specs=[pl.BlockSpec(memory_space=pltpu.VMEM), pl.BlockSpec(memory_space=pltpu.VMEM)],
        out_specs=pl.BlockSpec(memory_space=pltpu.VMEM),
        out_shape=jax.ShapeDtypeStruct((N_BUCKETS, LANES), F32),
    )(dbias, ids)


def _swa_penalty(n):
    shape = (LANES, 3 * LANES)
    tl = lax.broadcasted_iota(jnp.int32, shape, 0)
    col = lax.broadcasted_iota(jnp.int32, shape, 1)
    sl = col & (LANES - 1)
    nv = jnp.full(shape, n, jnp.int32)
    is_meta = sl >= PAD_FRONT
    prev = (col < LANES) & (sl > tl) & (nv >= 1) & ((nv >= 2) | is_meta)
    cur = (col >= LANES) & (col < 2 * LANES) & (sl <= tl) & ((nv >= 1) | is_meta)
    meta = (col >= 2 * LANES) & is_meta & ((nv >= 2) | ((nv == 1) & (sl <= tl)))
    return jnp.where(prev | cur | meta, 0.0, NEG)


def _swa_keys(ref, n):
    off_prev = pl.multiple_of(jnp.maximum(n - 1, 0) * LANES, LANES)
    off_cur = pl.multiple_of(n * LANES, LANES)
    return jnp.concatenate([ref[pl.ds(off_prev, LANES), :], ref[pl.ds(off_cur, LANES), :], ref[0:LANES, :]], axis=0)


def swa_fwd(q, k, v, bias, sinks, name):
    m = q.shape[0]

    def body(q_ref, k_ref, v_ref, bias_ref, sink_ref, o_ref, lse_ref):
        n = pl.program_id(0)
        lane1 = lax.broadcasted_iota(jnp.int32, (1, LANES), 1)
        lane_t = lax.broadcasted_iota(jnp.int32, (LANES, LANES), 1)
        in_head = [lane1 < HEAD_DIM, lane1 >= HEAD_DIM]
        kall = _swa_keys(k_ref, n)
        vall = _swa_keys(v_ref, n)
        vs = [jnp.where(in_head[g], vall, jnp.zeros_like(vall)) for g in (0, 1)]
        penalty = _swa_penalty(n)
        lse = jnp.zeros((LANES, LANES), F32)
        for b in range(4):
            qb = q_ref[:, b * LANES:(b + 1) * LANES]
            ob = jnp.zeros((LANES, LANES), F32)
            for g in (0, 1):
                h = 4 * g + b
                qe = jnp.where(in_head[g], qb, jnp.zeros_like(qb))
                s = _dot_nt(qe, kall) + bias_ref[h] + penalty
                sink = sink_ref[h]
                mx = jnp.maximum(jnp.max(s, axis=1, keepdims=True), sink)
                p = jnp.exp(s - mx)
                den = jnp.sum(p, axis=1, keepdims=True) + jnp.exp(sink - mx)
                ob = ob + _dot((p / den).astype(BF16), vs[g])
                lse = jnp.where(lane_t == h, mx + jnp.log(den), lse)
            o_ref[:, b * LANES:(b + 1) * LANES] = ob
        lse_ref[...] = lse

    return pl.pallas_call(
        body, name=name, grid=(m // LANES,),
        in_specs=[pl.BlockSpec((LANES, 512), lambda n: (n, 0)),
                  pl.BlockSpec((m, LANES), lambda n: (0, 0)), pl.BlockSpec((m, LANES), lambda n: (0, 0)),
                  pl.BlockSpec((8, LANES, 3 * LANES), lambda n: (0, 0, 0)),
                  pl.BlockSpec(memory_space=pltpu.SMEM)],
        out_specs=[pl.BlockSpec((LANES, 512), lambda n: (n, 0)), pl.BlockSpec((LANES, LANES), lambda n: (n, 0))],
        out_shape=[jax.ShapeDtypeStruct((m, 512), F32), jax.ShapeDtypeStruct((m, LANES), F32)],
        compiler_params=_params(1),
    )(q, k, v, bias, sinks)


def swa_bwd(q, k, v, bias, sinks, o, lse, do, name):
    m = q.shape[0]

    def body(q_ref, do_ref, o_ref, lse_ref, k_ref, v_ref, bias_ref, sink_ref,
             dq_ref, dk_ref, dv_ref, dbias_ref, dsink_ref):
        n = pl.program_id(0)

        @pl.when(n == 0)
        def _():
            for r in (dk_ref, dv_ref, dbias_ref, dsink_ref):
                r[...] = jnp.zeros_like(r)

        lane1 = lax.broadcasted_iota(jnp.int32, (1, LANES), 1)
        lane_t = lax.broadcasted_iota(jnp.int32, (LANES, LANES), 1)
        in_head = [lane1 < HEAD_DIM, lane1 >= HEAD_DIM]
        off_prev = pl.multiple_of(jnp.maximum(n - 1, 0) * LANES, LANES)
        off_cur = pl.multiple_of(n * LANES, LANES)
        kall = _swa_keys(k_ref, n)
        vall = _swa_keys(v_ref, n)
        ks = [jnp.where(in_head[g], kall, jnp.zeros_like(kall)) for g in (0, 1)]
        penalty = _swa_penalty(n)
        lsev = lse_ref[...]
        dsink = dsink_ref[...]
        dkall = jnp.zeros((3 * LANES, LANES), F32)
        dvall = jnp.zeros((3 * LANES, LANES), F32)
        for b in range(4):
            sl = slice(b * LANES, (b + 1) * LANES)
            qb = q_ref[:, sl]
            dob = do_ref[:, sl]
            prod = dob * o_ref[:, sl]
            dqb = jnp.zeros((LANES, LANES), F32)
            for g in (0, 1):
                h = 4 * g + b
                qe = jnp.where(in_head[g], qb, jnp.zeros_like(qb))
                doe = jnp.where(in_head[g], dob, 0.0).astype(BF16)
                delta = jnp.sum(jnp.where(in_head[g], prod, 0.0), axis=1, keepdims=True)
                lse_h = _lane_col(lsev, lane_t, h)
                s = _dot_nt(qe, kall) + bias_ref[h] + penalty
                p = jnp.exp(s - lse_h)
                ds = p * (_dot_nt(doe, vall) - delta)
                dbias_ref[h] += ds
                sink_part = jnp.sum(-jnp.exp(sink_ref[h] - lse_h) * delta, keepdims=True)
                dsink = jnp.where(lane1 == h, dsink + sink_part, dsink)
                dsb = ds.astype(BF16)
                dqb = dqb + _dot(dsb, ks[g])
                dkall = dkall + _dot_tn(dsb, qe)
                dvall = dvall + _dot_tn(p.astype(BF16), doe)
            dq_ref[:, sl] = dqb
        dsink_ref[...] = dsink
        for ref, val in ((dk_ref, dkall), (dv_ref, dvall)):
            ref[pl.ds(off_prev, LANES), :] += val[0:LANES]
            ref[pl.ds(off_cur, LANES), :] += val[LANES:2 * LANES]
            ref[0:LANES, :] += val[2 * LANES:3 * LANES]

    blk = pl.BlockSpec((LANES, 512), lambda n: (n, 0))
    full = pl.BlockSpec((m, LANES), lambda n: (0, 0))
    return pl.pallas_call(
        body, name=name, grid=(m // LANES,),
        in_specs=[blk, blk, blk, pl.BlockSpec((LANES, LANES), lambda n: (n, 0)), full, full,
                  pl.BlockSpec((8, LANES, 3 * LANES), lambda n: (0, 0, 0)),
                  pl.BlockSpec(memory_space=pltpu.SMEM)],
        out_specs=[blk, full, full, pl.BlockSpec((8, LANES, 3 * LANES), lambda n: (0, 0, 0)),
                   pl.BlockSpec((1, LANES), lambda n: (0, 0))],
        out_shape=[jax.ShapeDtypeStruct((m, 512), F32), jax.ShapeDtypeStruct((m, LANES), F32),
                   jax.ShapeDtypeStruct((m, LANES), F32), jax.ShapeDtypeStruct((8, LANES, 3 * LANES), F32),
                   jax.ShapeDtypeStruct((1, LANES), F32)],
        compiler_params=_params(1),
    )(q, do, o, lse, k, v, bias, sinks)


def branch_out(h, o_fox, o_swa, proj, wbf, wbs, wo, cols, name):
    m, d = h.shape
    tm = _row_tile(m)

    def body(h_ref, of_ref, os_ref, ga_ref, gb_ref, wbf_ref, wbs_ref, wo_ref, hn_ref):
        tf = _dot(of_ref[...].astype(BF16), wbf_ref[...])
        ts = _dot(os_ref[...].astype(BF16), wbs_ref[...])
        y = jax.nn.sigmoid(ga_ref[...]) * tf + jax.nn.sigmoid(gb_ref[...]) * ts
        hn_ref[...] = h_ref[...] + _dot(y.astype(BF16), wo_ref[...])

    row = lambda w, o=0: pl.BlockSpec((tm, w), lambda r, o=o: (r, o))
    res = lambda a: pl.BlockSpec(a.shape, lambda r: (0, 0))
    return pl.pallas_call(
        body, name=name, grid=(m // tm,),
        in_specs=[row(d), row(1024), row(512), row(d, cols.ga // d), row(d, cols.gb // d), res(wbf), res(wbs), res(wo)],
        out_specs=row(d),
        out_shape=jax.ShapeDtypeStruct((m, d), F32),
        compiler_params=_params(1),
    )(h, o_fox, o_swa, proj, proj, wbf, wbs, wo)


def branch_out_bwd(dh, o_fox, o_swa, proj, wbf, wbs, wo, cols, name):
    m, d = dh.shape
    tm = _row_tile(m)

    def body(dh_ref, of_ref, os_ref, ga_ref, gb_ref, wbf_ref, wbs_ref, wo_ref,
             y_ref, dtf_ref, dts_ref, dga_ref, dgb_ref, dof_ref, dos_ref, delta_ref):
        dy = _dot_nt(dh_ref[...].astype(BF16), wo_ref[...])
        tf = _dot(of_ref[...].astype(BF16), wbf_ref[...])
        ts = _dot(os_ref[...].astype(BF16), wbs_ref[...])
        sa = jax.nn.sigmoid(ga_ref[...])
        sb = jax.nn.sigmoid(gb_ref[...])
        y_ref[...] = (sa * tf + sb * ts).astype(BF16)
        dtf = (dy * sa).astype(BF16)
        dts = (dy * sb).astype(BF16)
        dtf_ref[...] = dtf
        dts_ref[...] = dts
        dga_ref[...] = (dy * tf * sa * (1.0 - sa)).astype(BF16)
        dgb_ref[...] = (dy * ts * sb * (1.0 - sb)).astype(BF16)
        dof = _dot_nt(dtf, wbf_ref[...])
        dof_ref[...] = dof.astype(BF16)
        dos_ref[...] = _dot_nt(dts, wbs_ref[...])
        lane = lax.broadcasted_iota(jnp.int32, (tm, LANES), 1)
        delta = jnp.zeros((tm, LANES), F32)
        for hd in range(8):
            sl = slice(hd * LANES, (hd + 1) * LANES)
            delta = jnp.where(lane == hd, jnp.sum(dof[:, sl] * of_ref[:, sl], axis=1, keepdims=True), delta)
        delta_ref[...] = delta

    row = lambda w, o=0: pl.BlockSpec((tm, w), lambda r, o=o: (r, o))
    res = lambda a: pl.BlockSpec(a.shape, lambda r: (0, 0))
    return pl.pallas_call(
        body, name=name, grid=(m // tm,),
        in_specs=[row(d), row(1024), row(512), row(d, cols.ga // d), row(d, cols.gb // d), res(wbf), res(wbs), res(wo)],
        out_specs=[row(d)] * 5 + [row(1024), row(512), row(LANES)],
        out_shape=[jax.ShapeDtypeStruct((m, d), BF16)] * 5 + [jax.ShapeDtypeStruct((m, 1024), BF16),
                   jax.ShapeDtypeStruct((m, 512), F32), jax.ShapeDtypeStruct((m, LANES), F32)],
        compiler_params=_params(1),
    )(dh, o_fox, o_swa, proj, proj, wbf, wbs, wo)


def loss_head(h, target, name):
    m, d = h.shape

    def body(h_ref, t_ref, dh_ref, loss_ref):
        n = pl.program_id(0)

        @pl.when(n == 0)
        def _():
            loss_ref[...] = jnp.zeros_like(loss_ref)
            dh_ref[...] = jnp.zeros_like(dh_ref)

        @pl.when(n > 0)
        def _():
            err = h_ref[...] - t_ref[...]
            dh_ref[...] = err * (1.0 / d)
            loss_ref[...] += jnp.sum(err * err, keepdims=True) * (0.5 / d)

    return pl.pallas_call(
        body, name=name, grid=(m // LANES,),
        in_specs=[pl.BlockSpec((LANES, d), lambda n: (n, 0)),
                  pl.BlockSpec((LANES, d), lambda n: (jnp.maximum(n - 1, 0), 0))],
        out_specs=[pl.BlockSpec((LANES, d), lambda n: (n, 0)), pl.BlockSpec((8, LANES), lambda n: (0, 0))],
        out_shape=[jax.ShapeDtypeStruct((m, d), F32), jax.ShapeDtypeStruct((8, LANES), F32)],
        compiler_params=_params(1),
    )(h, target)


def _adamw_math(w, g, m, v):
    m = ADAM_B1 * m + (1.0 - ADAM_B1) * g
    v = ADAM_B2 * v + (1.0 - ADAM_B2) * (g * g)
    m_hat = m / (1.0 - ADAM_B1 ** ADAM_STEP)
    v_hat = v / (1.0 - ADAM_B2 ** ADAM_STEP)
    delta = -ADAM_LR * (m_hat / (jnp.sqrt(v_hat) + ADAM_EPS) + ADAM_WD * w)
    return delta, m, v


def adamw_sum(parts, w, m, v, name, after=None):
    n_layers, a, b = w.shape
    ta = next(t for t in (256, 176, 128, a) if a % t == 0)
    nr = a // ta

    def body(*refs):
        p_refs = refs[:n_layers]
        w_ref, m_ref, v_ref = refs[n_layers:n_layers + 3]
        g_o, d_o, m_o, v_o = refs[-4:]
        for l in range(n_layers):
            @pl.when(pl.program_id(0) == l)
            def _(l=l):
                g = p_refs[l][0].astype(F32)
                for j in range(1, N_DEV):
                    g = g + p_refs[l][j].astype(F32)
                g_o[0] = g
                d_o[0], m_o[0], v_o[0] = _adamw_math(w_ref[0], g, m_ref[0], v_ref[0])

    def part_spec(l):
        return pl.BlockSpec((N_DEV, ta, b), lambda i, r, l=l: (0, jnp.where(i == l, r, jnp.where(i < l, 0, nr - 1)), 0))

    row = pl.BlockSpec((1, ta, b), lambda i, r: (i, r, 0))
    return pl.pallas_call(
        body, name=name, grid=(n_layers, nr),
        in_specs=[part_spec(l) for l in range(n_layers)] + [row, row, row]
                 + ([pl.BlockSpec(memory_space=pl.ANY)] if after is not None else []),
        out_specs=[row] * 4,
        out_shape=[jax.ShapeDtypeStruct(w.shape, F32)] * 4,
        compiler_params=_params(2),
    )(*parts, w, m, v, *([after] if after is not None else []))


def adamw_small(g, w, m, v, name):
    def body(g_ref, w_ref, m_ref, v_ref, d_o, m_o, v_o):
        d_o[...], m_o[...], v_o[...] = _adamw_math(w_ref[...], g_ref[...], m_ref[...], v_ref[...])

    spec = pl.BlockSpec(memory_space=pltpu.VMEM)
    return pl.pallas_call(
        body, name=name, in_specs=[spec] * 4, out_specs=[spec] * 3,
        out_shape=[jax.ShapeDtypeStruct(w.shape, F32)] * 3,
    )(g, w, m, v)


BIG = ("ffn1_w_in", "ffn1_w_out", "w_in", "w_branch_fox", "w_branch_swa", "w_out", "ffn2_w_in", "ffn2_w_out")
SMALL = ("rel_bias_table", "ffn1_norm", "mix_norm", "forget_bias", "fox_q_norm", "fox_k_norm",
         "swa_q_norm", "swa_k_norm", "swa_sinks", "ffn2_norm")
WEIGHTS = ("meta_tokens", "rel_bias_table", "ffn1_norm", "ffn1_w_in", "ffn1_w_out", "mix_norm", "w_in",
           "forget_bias", "fox_q_norm", "fox_k_norm", "swa_q_norm", "swa_k_norm", "swa_sinks", "w_branch_fox",
           "w_branch_swa", "w_out", "ffn2_norm", "ffn2_w_in", "ffn2_w_out")


def _pack(arrs, width, row_multiple, dtype):
    lead = arrs[0].shape[:-1]
    flat = jnp.concatenate([a.astype(dtype) for a in arrs], axis=-1)
    n = flat.shape[-1]
    rows = -(-n // width)
    rows = -(-rows // row_multiple) * row_multiple
    flat = jnp.pad(flat, [(0, 0)] * len(lead) + [(0, rows * width - n)])
    return flat.reshape(lead + (rows, width))


def _unpack(flat, shapes):
    flat = flat.reshape(-1)
    out, off = [], 0
    for s in shapes:
        n = int(np.prod(s))
        out.append(flat[off:off + n].reshape(s))
        off += n
    return out


def _swa_head_order():
    return [4 * (j % 2) + j // 2 for j in range(8)]


def _permute_heads(a, axis, inverse=False):
    order = _swa_head_order()
    if inverse:
        order = [order.index(hd) for hd in range(8)]
    parts = [lax.slice_in_dim(a, hd * HEAD_DIM, (hd + 1) * HEAD_DIM, axis=axis) for hd in order]
    return jnp.concatenate(parts, axis=axis)


def _w_in_segments(cols):
    d = cols.d
    segs = [(512 * i + HEAD_DIM * hd, HEAD_DIM, new + LANES * hd)
            for i, new in enumerate((cols.qa, cols.ka, cols.va)) for hd in range(8)]
    segs.append((1536, 8, cols.fa))
    order = _swa_head_order()
    segs += [(1544 + HEAD_DIM * hd, HEAD_DIM, cols.qb + HEAD_DIM * order.index(hd)) for hd in range(8)]
    segs += [(2056, 128, cols.kb), (2184, 128, cols.vb), (2312, d, cols.ga), (2312 + d, d, cols.gb)]
    return segs


def _reorder_w_in(blocks, cols):
    width = blocks[0].shape[1]
    zeros = lambda n: jnp.zeros((blocks[0].shape[0], n), blocks[0].dtype)
    parts, at = [], 0
    for old, length, new in sorted(_w_in_segments(cols), key=lambda s: s[2]):
        if new > at:
            parts.append(zeros(new - at))
        at = new + length
        while length:
            j, off = divmod(old, width)
            take = min(length, width - off)
            parts.append(blocks[j][:, off:off + take])
            old, length = old + take, length - take
    parts.append(zeros(cols.np - at))
    return jnp.concatenate(parts, axis=1)


def _restore_w_in(wp, cols, width):
    segs = sorted(_w_in_segments(cols))
    blocks = []
    for j in range(N_DEV):
        lo, hi = j * width, (j + 1) * width
        parts = []
        for old, length, new in segs:
            a, b = max(old, lo), min(old + length, hi)
            if a < b:
                parts.append(wp[:, new + a - old:new + b - old])
        blocks.append(jnp.concatenate(parts, axis=1))
    return jnp.stack(blocks)


def _lane_pad(v):
    return jnp.pad(v, ((0, 0), (0, LANES - v.shape[1])))


def kernel(x, meta_tokens, rel_bias_table, ffn1_norm, ffn1_w_in, ffn1_w_out, mix_norm, w_in, forget_bias, fox_q_norm, fox_k_norm, swa_q_norm, swa_k_norm, swa_sinks, w_branch_fox, w_branch_swa, w_out, ffn2_norm, ffn2_w_in, ffn2_w_out, loss_target, m_meta_tokens, m_rel_bias_table, m_ffn1_norm, m_ffn1_w_in, m_ffn1_w_out, m_mix_norm, m_w_in, m_forget_bias, m_fox_q_norm, m_fox_k_norm, m_swa_q_norm, m_swa_k_norm, m_swa_sinks, m_w_branch_fox, m_w_branch_swa, m_w_out, m_ffn2_norm, m_ffn2_w_in, m_ffn2_w_out, v_meta_tokens, v_rel_bias_table, v_ffn1_norm, v_ffn1_w_in, v_ffn1_w_out, v_mix_norm, v_w_in, v_forget_bias, v_fox_q_norm, v_fox_k_norm, v_swa_q_norm, v_swa_k_norm, v_swa_sinks, v_w_branch_fox, v_w_branch_swa, v_w_out, v_ffn2_norm, v_ffn2_w_in, v_ffn2_w_out):
    args = dict(locals())
    wts = {n: args[n] for n in WEIGHTS}
    mom1 = {n: args["m_" + n] for n in WEIGHTS}
    mom2 = {n: args["v_" + n] for n in WEIGHTS}

    seq, d = x.shape[1], x.shape[2]
    m_rows = seq + LANES
    depth = ffn1_norm.shape[0]
    fb = ffn1_w_in.shape[2]
    fo = ffn1_w_out.shape[1]
    din_shard = w_in.shape[2]
    cols = _Cols(d)
    scale = HEAD_DIM ** -0.5
    dev = 4 * lax.axis_index("x") + 2 * lax.axis_index("y") + lax.axis_index("c")

    groups = {"ffn1": ("ffn1_w_in", "ffn1_w_out"), "mix": ("w_in", "w_branch_fox", "w_branch_swa", "w_out"),
              "ffn2": ("ffn2_w_in", "ffn2_w_out"), "ffn1_in": ("ffn1_w_in",), "ffn1_out": ("ffn1_w_out",),
              "ffn2_in": ("ffn2_w_in",), "ffn2_out": ("ffn2_w_out",)}
    flipped = ("ffn1_w_in", "ffn2_w_in")
    for n in flipped:
        wts[n], mom1[n], mom2[n] = (jnp.swapaxes(a, 1, 2) for a in (wts[n], mom1[n], mom2[n]))
    shard = {n: wts[n].astype(BF16) for n in BIG}
    full, parts, gw = {}, {}, {}

    def keys_of(stages):
        return [(n, l) for g, l in stages if l < depth for n in groups[g]]

    def gather_rider(stages):
        return Rider([shard[n][l] for n, l in keys_of(stages)], True)

    def scatter_rider(stages):
        return Rider([gw[k] for k in keys_of(stages)], False)

    def ffn_weights(tag, l):
        return full[tag + "_w_in", l], full[tag + "_w_out", l].reshape(4, fb, d)

    def mixer_weights(l):
        wp = _reorder_w_in([full["w_in", l][j] for j in range(N_DEV)], cols)
        wbf = jnp.concatenate([full["w_branch_fox", l][j] for j in range(N_DEV)], axis=1)
        wbf = jnp.pad(wbf.reshape(8, HEAD_DIM, d), ((0, 0), (0, LANES - HEAD_DIM), (0, 0))).reshape(8 * LANES, d)
        wbs = _permute_heads(jnp.concatenate([full["w_branch_swa", l][j] for j in range(N_DEV)], axis=1), 0)
        return wp, wbf, wbs, full["w_out", l].reshape(d, d)

    full.update(zip(keys_of([("ffn1", 0)]), exchange_hbm(gather_rider([("ffn1", 0)]).srcs, True, "gather_first")))
    meta_all = gather_small(meta_tokens.reshape(1, N_META, -1), "gather_meta")
    meta_full = meta_all.transpose(1, 0, 2).reshape(N_META, d)
    tile8 = lambda g, s=1.0: jnp.tile(g.reshape(1, HEAD_DIM) * s, (1, 8))
    tile2 = lambda g: jnp.tile(g.reshape(1, HEAD_DIM), (1, 2))
    data_lanes = lambda g, s=1.0: _lane_pad(g.reshape(1, HEAD_DIM) * s)
    bias = bias_build(rel_bias_table, "swa_bias")

    first = jnp.concatenate([jnp.zeros((PAD_FRONT, d), F32), meta_full], axis=0)
    h = jnp.concatenate([first, x[0]], axis=0)
    saved, lw = [], []
    for l in range(depth):
        s, w = {"h0": h}, {}
        w["ffn1_in"], w["ffn1_out"] = ffn_weights("ffn1", l)
        stages = [("mix", l)]
        (h, s["n1"], s["a1"], s["fg1"], s["fu1"]), got = ffn_fwd(h, ffn1_norm[l:l + 1], w["ffn1_in"], w["ffn1_out"],
                                                          f"ffn1_fwd_{l}", gather_rider(stages))
        full.update(zip(keys_of(stages), got))
        s["h1"] = h
        w["wp"], w["wbf"], w["wbs"], w["wo"] = mixer_weights(l)
        s["nm"], s["proj"] = mixer_proj(h, mix_norm[l:l + 1], w["wp"], f"mixer_proj_{l}")
        s["gains"] = (data_lanes(fox_q_norm[l], scale), data_lanes(fox_k_norm[l]), tile8(swa_q_norm[l], scale),
                      tile2(swa_k_norm[l]))
        s["fbias"] = _lane_pad(forget_bias[l:l + 1])
        qf, kf, vf, qb, kb, vb = qk_post(s["proj"], s["gains"], s["fbias"], cols, f"qk_post_{l}")
        s.update(qf=qf, kf=kf, vf=vf, qb=qb, kb=kb, vb=vb)
        stages = [("ffn2", l)]
        (s["o_fox"], s["lse_fox"]), got = fox_fwd(qf, kf, vf.T, f"fox_fwd_{l}", gather_rider(stages))
        full.update(zip(keys_of(stages), got))
        s["o_swa"], s["lse_swa"] = swa_fwd(qb, kb, vb, bias, swa_sinks[l], f"swa_fwd_{l}")
        h = branch_out(h, s["o_fox"], s["o_swa"], s["proj"], w["wbf"], w["wbs"], w["wo"], cols, f"branch_out_{l}")
        s["h2"] = h
        w["ffn2_in"], w["ffn2_out"] = ffn_weights("ffn2", l)
        stages = [("ffn1", l + 1)]
        (h, s["n2"], s["a2"], s["fg2"], s["fu2"]), got = ffn_fwd(h, ffn2_norm[l:l + 1], w["ffn2_in"], w["ffn2_out"],
                                                          f"ffn2_fwd_{l}", gather_rider(stages))
        full.update(zip(keys_of(stages), got))
        saved.append(s)
        lw.append(w)

    dh, loss_part = loss_head(h, loss_target[0], "loss_head")

    gs = {n: [None] * depth for n in SMALL}
    dbias_total = None
    for l in reversed(range(depth)):
        w, s = lw[l], saved[l]

        def ffn_back(dh, tag, hin, norm, n_in, a, f_gate, f_up, stages):
            (dh_in, dg, du, dgn, dhs), got = ffn_bwd(dh, hin, norm, f_gate, f_up, w[tag + "_in"], w[tag + "_out"],
                                                     f"{tag}_bwd_{l}", scatter_rider(stages))
            parts.update(zip(keys_of(stages), got))
            gw[tag + "_w_out", l] = matmul_tn(a, dhs[None], f"{tag}_dwo_{l}").reshape(N_DEV, fo, d)
            stages = [(tag + "_out", l)]
            gw[tag + "_w_in", l], got = matmul_tn(dg, n_in[None], f"{tag}_dwi_{l}", x2=du,
                                                  rider=scatter_rider(stages))
            parts.update(zip(keys_of(stages), got))
            return dh_in, dgn

        dh, gs["ffn2_norm"][l] = ffn_back(dh, "ffn2", s["h2"], ffn2_norm[l:l + 1], s["n2"], s["a2"], s["fg2"],
                                          s["fu2"], [("ffn1_in", l + 1)])

        y, dtf, dts, dga, dgb, dof, dos, delta = branch_out_bwd(dh, s["o_fox"], s["o_swa"], s["proj"], w["wbf"],
                                                                w["wbs"], w["wo"], cols, f"branch_out_bwd_{l}")
        gw["w_out", l] = matmul_tn(y[None], dh[None], f"dw_out_{l}").reshape(N_DEV, d // N_DEV, d)
        to_shards = lambda a: a.reshape(512, N_DEV, d // N_DEV).transpose(1, 0, 2)
        gw["w_branch_fox", l] = to_shards(matmul_tn(s["o_fox"][None], dtf[None], f"dw_branch_fox_{l}")[0]
                                          .reshape(8, LANES, d)[:, :HEAD_DIM].reshape(512, d))
        gw["w_branch_swa", l] = to_shards(_permute_heads(
            matmul_tn(s["o_swa"][None], dts[None], f"dw_branch_swa_{l}")[0], 0, inverse=True))

        stages = [("ffn2_in", l)]
        (dkf, dvf, dqf, dc_rows), got = fox_bwd(s["qf"], s["kf"], s["vf"], s["kf"].T, dof, s["lse_fox"],
                                         delta[:, :8].T.reshape(8, 1, m_rows), f"fox_bwd_{l}", scatter_rider(stages))
        parts.update(zip(keys_of(stages), got))
        dc = _lane_pad(dc_rows.reshape(8, m_rows).T)
        dqb, dkb, dvb, dbias, dsink = swa_bwd(s["qb"], s["kb"], s["vb"], bias, swa_sinks[l], s["o_swa"], s["lse_swa"],
                                              dos, f"swa_bwd_{l}")
        dbias_total = dbias if dbias_total is None else dbias_total + dbias
        gs["swa_sinks"][l] = dsink[0, :8]
        dproj, ggqa, ggka, ggqb, ggkb, gfb = qk_post_bwd(s["proj"], s["gains"], s["fbias"], dqf, dkf, dvf, dqb, dkb,
                                                         dvb, dc, dga, dgb, cols, f"qk_post_bwd_{l}")
        gs["fox_q_norm"][l] = ggqa[0, :HEAD_DIM] * scale
        gs["fox_k_norm"][l] = ggka[0, :HEAD_DIM]
        gs["swa_q_norm"][l] = ggqb.reshape(8, HEAD_DIM).sum(0) * scale
        gs["swa_k_norm"][l] = ggkb.reshape(2, HEAD_DIM).sum(0)
        gs["forget_bias"][l] = gfb[0, :8]
        dwp = matmul_tn(s["nm"][None], dproj[None], f"dw_in_{l}", tn=1024 if cols.np % 1024 == 0 else None)[0]
        gw["w_in", l] = _restore_w_in(dwp, cols, din_shard)
        dh, gs["mix_norm"][l] = dproj_bwd(dh, s["h1"], mix_norm[l:l + 1], dproj, w["wp"], f"dproj_bwd_{l}")

        dh, gs["ffn1_norm"][l] = ffn_back(dh, "ffn1", s["h0"], ffn1_norm[l:l + 1], s["n1"], s["a1"], s["fg1"],
                                          s["fu1"], [("mix", l)])

    grad_x = dh[LANES:][None]
    dmeta = dh[PAD_FRONT:LANES]
    dtable = bias_reduce(dbias_total, "swa_dbias")[:, :8]

    last = ("ffn1_w_in", 0)
    send_sems, recv_sems, src_thru, land_thru, token = scatter_start(gw[last], "scatter_last_start")
    big_out = [{}, {}, {}, {}]
    for n in BIG:
        if n != last[0]:
            outs = adamw_sum([parts[n, l] for l in range(depth)], wts[n], mom1[n], mom2[n], f"adamw_{n}", after=token)
            for k in range(4):
                big_out[k][n] = outs[k]
    sent, landed = scatter_wait(send_sems, recv_sems, src_thru, land_thru,
                                [big_out[1][n] for n in BIG if n != last[0]], "scatter_last_wait")
    parts[last] = lax.dynamic_update_slice_in_dim(landed, lax.dynamic_slice_in_dim(sent, dev, 1, axis=0), dev, axis=0)
    outs = adamw_sum([parts[last[0], l] for l in range(depth)], wts[last[0]], mom1[last[0]], mom2[last[0]],
                     f"adamw_{last[0]}")
    for k in range(4):
        big_out[k][last[0]] = outs[k]

    small_g = {n: (jnp.stack(gs[n]) if n != "rel_bias_table" else None) for n in SMALL}
    small_g["rel_bias_table"] = dtable
    pieces = [loss_part[0:1, 0:1].reshape(1, 1)] + [small_g[n].reshape(1, -1) for n in SMALL] + [dmeta.reshape(1, -1)]
    small_shapes = [(1,)] + [wts[n].shape for n in SMALL] + [(N_META, d)]
    total = allsum_small(_pack(pieces, LANES, 8, F32), "allsum_small")
    summed = _unpack(total, small_shapes)
    loss = summed[0][0]
    g_small = dict(zip(SMALL, summed[1:1 + len(SMALL)]))
    g_meta = lax.dynamic_slice_in_dim(summed[-1], dev * (d // N_DEV), d // N_DEV, axis=1)
    names = SMALL + ("meta_tokens",)
    g_small["meta_tokens"] = g_meta
    pk = lambda src: _pack([src[n].reshape(1, -1) for n in names], LANES, 8, F32)[0]
    small_out = [dict(zip(names, _unpack(o, [wts[n].shape for n in names])))
                 for o in adamw_small(pk(g_small), pk(wts), pk(mom1), pk(mom2), "adamw_small")]

    for out in big_out:
        for n in flipped:
            out[n] = jnp.swapaxes(out[n], 1, 2)
    grads = {**big_out[0], **g_small}
    delta = {**big_out[1], **small_out[0]}
    new_m = {**big_out[2], **small_out[1]}
    new_v = {**big_out[3], **small_out[2]}
    return (loss, grad_x, *[grads[n] for n in WEIGHTS], *[delta[n] for n in WEIGHTS],
            *[new_m[n] for n in WEIGHTS], *[new_v[n] for n in WEIGHTS])
```

```python
import math

import numpy as np
import jax
import jax.numpy as jnp
from jax import lax
from jax.experimental import pallas as pl
from jax.experimental.pallas import tpu as pltpu

F32 = jnp.float32
BF16 = jnp.bfloat16
EPS = 1e-6
NEG = -1e30
HEAD_DIM = 64
LANES = 128
N_META = 16
PAD_FRONT = LANES - N_META
N_BUCKETS = 32
MAX_DISTANCE = 128
N_DEV = 8
ADAM_LR, ADAM_B1, ADAM_B2, ADAM_EPS, ADAM_WD, ADAM_STEP = 0.001, 0.9, 0.999, 1e-08, 0.01, 10
VMEM_LIMIT = 56 * 1024 * 1024
MESH = pl.DeviceIdType.MESH


def _params(n_grid):
    return pltpu.CompilerParams(dimension_semantics=("arbitrary",) * n_grid,
                                vmem_limit_bytes=VMEM_LIMIT)


def _dot(a, b):
    return jnp.dot(a, b, preferred_element_type=F32)


def _dot_nt(a, b):
    return lax.dot_general(a, b, (((1,), (1,)), ((), ())), preferred_element_type=F32)


def _dot_tn(a, b):
    return lax.dot_general(a, b, (((0,), (0,)), ((), ())), preferred_element_type=F32)


def _rms(x):
    r = lax.rsqrt(jnp.mean(x * x, axis=-1, keepdims=True) + EPS)
    return x * r, r


def _rms_bwd(x, g, dn):
    xh, r = _rms(x)
    dxh = dn * g
    dx = r * (dxh - xh * jnp.mean(dxh * xh, axis=-1, keepdims=True))
    return dx, jnp.sum(dn * xh, axis=0, keepdims=True)


def _split2(v):
    hi = v.astype(BF16)
    return hi, (v - hi.astype(F32)).astype(BF16)


def _split3(v):
    hi = v.astype(BF16)
    r1 = v - hi.astype(F32)
    mid = r1.astype(BF16)
    return hi, mid, (r1 - mid.astype(F32)).astype(BF16)


def _group_ones():
    r = lax.broadcasted_iota(jnp.int32, (LANES, LANES), 0) // HEAD_DIM
    c = lax.broadcasted_iota(jnp.int32, (LANES, LANES), 1) // HEAD_DIM
    return jnp.where(r == c, 1.0, 0.0).astype(BF16)


def _group_mean(v, ones):
    hi, lo = _split2(v)
    return (_dot(hi, ones) + _dot(lo, ones)) * (1.0 / HEAD_DIM)


def _row_tile(m):
    return 384 if m % 384 == 0 else LANES


def _tile(m, cap):
    return max(t for t in range(16, cap + 1, 16) if m % t == 0)


def _peer(k):
    x, y, c = lax.axis_index("x"), lax.axis_index("y"), lax.axis_index("c")
    px = 1 - x if k & 4 else x
    py = 1 - y if k & 2 else y
    pc = 1 - c if k & 1 else c
    return (px, py, pc), 4 * px + 2 * py + pc


def _exchange_body(src_ref, dst_ref, send_sems, recv_sems, local_sem, bcast):
    x, y, c = lax.axis_index("x"), lax.axis_index("y"), lax.axis_index("c")
    me = 4 * x + 2 * y + c
    mine = pltpu.make_async_copy(src_ref.at[0 if bcast else me], dst_ref.at[me], local_sem)
    mine.start()
    sends = []
    for k in range(1, N_DEV):
        dev, idx = _peer(k)
        cp = pltpu.make_async_remote_copy(
            src_ref=src_ref.at[0 if bcast else idx], dst_ref=dst_ref.at[me],
            send_sem=send_sems.at[k - 1], recv_sem=recv_sems.at[k - 1],
            device_id=dev, device_id_type=MESH)
        cp.start()
        sends.append(cp)
    for k in range(1, N_DEV):
        dev, idx = _peer(k)
        pltpu.make_async_remote_copy(
            src_ref=src_ref.at[0], dst_ref=dst_ref.at[idx],
            send_sem=send_sems.at[k - 1], recv_sem=recv_sems.at[k - 1],
            device_id=dev, device_id_type=MESH).wait_recv()
    for cp in sends:
        cp.wait_send()
    mine.wait()


class Rider:
    FIRST = (1, 2, 4, 6)
    RELAYED = (2, 4, 6)

    def __init__(self, srcs=(), bcast=True):
        self.srcs, self.bcast, self.n = list(srcs), bcast, len(srcs)

    def out_shapes(self):
        return [jax.ShapeDtypeStruct(((N_DEV,) + s.shape) if self.bcast else s.shape, s.dtype) for s in self.srcs]

    def specs(self):
        return [pl.BlockSpec(memory_space=pl.ANY)] * self.n

    def scratch(self):
        if not self.n:
            return []
        return [pltpu.SemaphoreType.DMA((self.n * (N_DEV - 1),)), pltpu.SemaphoreType.DMA((self.n * (N_DEV - 1),)),
                pltpu.SemaphoreType.DMA((self.n,))]

    @staticmethod
    def _copy(src, dst, a, pair, dev, send_sems, recv_sems):
        sem = a * (N_DEV - 1) + pair - 1
        return pltpu.make_async_remote_copy(src_ref=src, dst_ref=dst, send_sem=send_sems.at[sem],
                                            recv_sem=recv_sems.at[sem], device_id=dev, device_id_type=MESH)

    def _first(self):
        return self.FIRST if self.bcast else range(1, N_DEV)

    def _own(self, s, d, a, local_sems):
        me = 4 * lax.axis_index("x") + 2 * lax.axis_index("y") + lax.axis_index("c")
        return pltpu.make_async_copy(s if self.bcast else s.at[me], d.at[me], local_sems.at[a]), me

    def start(self, src_refs, dst_refs, send_sems, recv_sems, local_sems):
        for a, (s, d) in enumerate(zip(src_refs, dst_refs)):
            own, me = self._own(s, d, a, local_sems)
            own.start()
            for k in self._first():
                dev, idx = _peer(k)
                self._copy(s if self.bcast else s.at[idx], d.at[me], a, k, dev, send_sems, recv_sems).start()

    def relay(self, src_refs, dst_refs, send_sems, recv_sems, local_sems):
        if not self.bcast:
            return
        sibling, _ = _peer(1)
        for a, d in enumerate(dst_refs):
            for k in self.RELAYED:
                dev, idx = _peer(k)
                self._copy(d.at[idx], d.at[idx], a, k, dev, send_sems, recv_sems).wait_recv()
                self._copy(d.at[idx], d.at[idx], a, k + 1, sibling, send_sems, recv_sems).start()

    def wait(self, src_refs, dst_refs, send_sems, recv_sems, local_sems):
        sibling, _ = _peer(1)
        for a, (s, d) in enumerate(zip(src_refs, dst_refs)):
            own, me = self._own(s, d, a, local_sems)
            for k in range(1, N_DEV):
                if not (self.bcast and k in self.RELAYED):
                    dev, idx = _peer(k)
                    self._copy(d.at[idx], d.at[idx], a, k, dev, send_sems, recv_sems).wait_recv()
            for k in self._first():
                dev, idx = _peer(k)
                self._copy(s if self.bcast else s.at[idx], d.at[me], a, k, dev, send_sems, recv_sems).wait_send()
            if self.bcast:
                for k in self.RELAYED:
                    dev, idx = _peer(k)
                    self._copy(d.at[idx], d.at[idx], a, k + 1, sibling, send_sems, recv_sems).wait_send()
            own.wait()


def rider_call(core, name, grid, in_specs, out_specs, out_shape, scratch_shapes, args, rider=None):
    rider = rider or Rider()
    n_in, n_out, n_scr, nr = len(in_specs), len(out_specs), len(scratch_shapes), rider.n

    def body(*refs):
        ins, r_src = refs[:n_in], refs[n_in:n_in + nr]
        outs = refs[n_in + nr:n_in + nr + n_out]
        r_dst = refs[n_in + nr + n_out:n_in + 2 * nr + n_out]
        scr = refs[n_in + 2 * nr + n_out:n_in + 2 * nr + n_out + n_scr]
        sems = refs[n_in + 2 * nr + n_out + n_scr:]
        if nr:
            first, relay, last = True, True, True
            for ax, size in enumerate(grid):
                first = first & (pl.program_id(ax) == 0)
                relay = relay & (pl.program_id(ax) == (3 * size // 4 if ax == 0 else 0))
                last = last & (pl.program_id(ax) == size - 1)
            if not grid:
                rider.start(r_src, r_dst, *sems)
                rider.relay(r_src, r_dst, *sems)
            else:
                pl.when(first)(lambda: rider.start(r_src, r_dst, *sems))
                if rider.bcast:
                    pl.when(relay)(lambda: rider.relay(r_src, r_dst, *sems))
        core(*ins, *outs, *scr)
        if nr:
            if not grid:
                rider.wait(r_src, r_dst, *sems)
            else:
                pl.when(last)(lambda: rider.wait(r_src, r_dst, *sems))

    res = pl.pallas_call(
        body, name=name, grid=grid,
        in_specs=list(in_specs) + rider.specs(),
        out_specs=list(out_specs) + rider.specs(),
        out_shape=list(out_shape) + rider.out_shapes(),
        scratch_shapes=list(scratch_shapes) + rider.scratch(),
        compiler_params=_params(len(grid)),
    )(*args, *rider.srcs)
    return res[:n_out], res[n_out:]


def exchange_hbm(srcs, bcast, name):
    return rider_call(lambda: None, name, (), [], [], [], [], [], Rider(srcs, bcast))[1]


_HBM = pl.BlockSpec(memory_space=pltpu.HBM)
_SEM = pl.BlockSpec(memory_space=pltpu.SEMAPHORE)
_EFFECT = pltpu.CompilerParams(has_side_effects=pltpu.SideEffectType.DATAFLOW_SIDE_EFFECTING)


def scatter_start(src, name):
    def body(src_ref, land_ref, send_sems, recv_sems, src_thru, land_thru, token):
        me = 4 * lax.axis_index("x") + 2 * lax.axis_index("y") + lax.axis_index("c")
        for k in range(1, N_DEV):
            dev, idx = _peer(k)
            pltpu.make_async_remote_copy(src_ref=src_ref.at[idx], dst_ref=land_ref.at[me], send_sem=send_sems.at[k - 1],
                                         recv_sem=recv_sems.at[k - 1], device_id=dev, device_id_type=MESH).start()
        token[...] = jnp.zeros_like(token)

    return pl.pallas_call(
        body, name=name,
        out_shape=(pltpu.SemaphoreType.DMA((N_DEV - 1,)), pltpu.SemaphoreType.DMA((N_DEV - 1,)),
                   pltpu.HBM(src.shape, src.dtype), pltpu.HBM(src.shape, src.dtype), jax.ShapeDtypeStruct((8, LANES), F32)),
        in_specs=(_HBM, _HBM), out_specs=(_SEM, _SEM, _HBM, _HBM, pl.BlockSpec(memory_space=pltpu.VMEM)),
        input_output_aliases={0: 2, 1: 3}, compiler_params=_EFFECT,
    )(pltpu.with_memory_space_constraint(src, pltpu.HBM),
      pltpu.with_memory_space_constraint(lax.empty(src.shape, src.dtype), pltpu.HBM))


def scatter_wait(send_sems, recv_sems, src_thru, land_thru, after, name):
    n_after = len(after)

    def body(*refs):
        src_ref, land_ref, send_sems, recv_sems = refs[:4]
        for k in range(1, N_DEV):
            dev, idx = _peer(k)
            copy = pltpu.make_async_remote_copy(src_ref=src_ref.at[idx], dst_ref=land_ref.at[idx],
                                                send_sem=send_sems.at[k - 1], recv_sem=recv_sems.at[k - 1],
                                                device_id=dev, device_id_type=MESH)
            copy.wait_send()
            copy.wait_recv()

    return pl.pallas_call(
        body, name=name,
        out_shape=(pltpu.HBM(src_thru.shape, src_thru.dtype), pltpu.HBM(land_thru.shape, land_thru.dtype)),
        in_specs=(_HBM, _HBM, _SEM, _SEM) + (pl.BlockSpec(memory_space=pl.ANY),) * n_after, out_specs=(_HBM, _HBM),
        input_output_aliases={0: 0, 1: 1}, compiler_params=_EFFECT,
    )(src_thru, land_thru, send_sems, recv_sems, *after)


def allsum_small(vec, name):
    def body(src_ref, out_ref, dst_ref, send_sems, recv_sems, local_sem):
        _exchange_body(src_ref, dst_ref, send_sems, recv_sems, local_sem, True)
        acc = dst_ref[0]
        for j in range(1, N_DEV):
            acc = acc + dst_ref[j]
        out_ref[...] = acc

    return pl.pallas_call(
        body, name=name,
        out_shape=jax.ShapeDtypeStruct(vec.shape[1:], F32),
        in_specs=[pl.BlockSpec(memory_space=pltpu.VMEM)],
        out_specs=pl.BlockSpec(memory_space=pltpu.VMEM),
        scratch_shapes=[pltpu.VMEM((N_DEV,) + vec.shape[1:], F32),
                        pltpu.SemaphoreType.DMA((N_DEV - 1,)), pltpu.SemaphoreType.DMA((N_DEV - 1,)),
                        pltpu.SemaphoreType.DMA],
    )(vec)


def gather_small(vec, name):
    def body(src_ref, dst_ref, send_sems, recv_sems, local_sem):
        _exchange_body(src_ref, dst_ref, send_sems, recv_sems, local_sem, True)

    return pl.pallas_call(
        body, name=name,
        out_shape=jax.ShapeDtypeStruct((N_DEV,) + vec.shape[1:], F32),
        in_specs=[pl.BlockSpec(memory_space=pltpu.VMEM)],
        out_specs=pl.BlockSpec(memory_space=pltpu.VMEM),
        scratch_shapes=[pltpu.SemaphoreType.DMA((N_DEV - 1,)), pltpu.SemaphoreType.DMA((N_DEV - 1,)),
                        pltpu.SemaphoreType.DMA],
    )(vec)


FFN_FWD_ROWS = 1056
FFN_BWD_ROWS = 704
DW_ROWS = 1408

def ffn_fwd(h, g, w_in8, w_out4, name, rider=None):
    m, d = h.shape
    fb = w_in8.shape[1]
    tm = _tile(m, FFN_FWD_ROWS)

    def body(h_ref, g_ref, wg_ref, wu_ref, wo_ref, hn_ref, n_ref, a_ref, fg_ref, fu_ref, acc_ref):
        i = pl.program_id(1)

        @pl.when(i == 0)
        def _():
            xh, _ = _rms(h_ref[...])
            n_ref[...] = (xh * g_ref[...]).astype(BF16)
            acc_ref[...] = jnp.zeros_like(acc_ref)

        n = n_ref[...]
        gate = _dot_nt(n, wg_ref[0])
        up = _dot_nt(n, wu_ref[0])
        sg = jax.nn.sigmoid(gate)
        silu = gate * sg
        a = (silu * up).astype(BF16)
        a_ref[0] = a
        fg_ref[0] = (up * (sg * (1.0 + gate * (1.0 - sg)))).astype(BF16)
        fu_ref[0] = silu.astype(BF16)
        acc_ref[...] += _dot(a, wo_ref[0])

        @pl.when(i == 3)
        def _():
            hn_ref[...] = h_ref[...] + 0.5 * acc_ref[...]

    return rider_call(
        body, name, (m // tm, 4),
        in_specs=[pl.BlockSpec((tm, d), lambda r, i: (r, 0)),
                  pl.BlockSpec((1, d), lambda r, i: (0, 0)),
                  pl.BlockSpec((1, fb, d), lambda r, i: (i, 0, 0)),
                  pl.BlockSpec((1, fb, d), lambda r, i: (i + 4, 0, 0)),
                  pl.BlockSpec((1, fb, d), lambda r, i: (i, 0, 0))],
        out_specs=[pl.BlockSpec((tm, d), lambda r, i: (r, 0)),
                   pl.BlockSpec((tm, d), lambda r, i: (r, 0))] + [pl.BlockSpec((1, tm, fb), lambda r, i: (i, r, 0))] * 3,
        out_shape=[jax.ShapeDtypeStruct((m, d), F32), jax.ShapeDtypeStruct((m, d), BF16)]
                  + [jax.ShapeDtypeStruct((4, m, fb), BF16)] * 3,
        scratch_shapes=[pltpu.VMEM((tm, d), F32)],
        args=(h, g, w_in8, w_in8, w_out4), rider=rider)


def ffn_bwd(dh, h, g, f_gate, f_up, w_in8, w_out4, name, rider=None):
    m, d = h.shape
    fb = w_in8.shape[1]
    tm = _tile(m, FFN_BWD_ROWS)

    def body(dh_ref, h_ref, g_ref, fg_ref, fu_ref, wg_ref, wu_ref, wo_ref,
             dhin_ref, dg_ref, du_ref, dgn_ref, dhs_ref, acc_ref):
        r = pl.program_id(0)
        i = pl.program_id(1)

        @pl.when(i == 0)
        def _():
            dhs_ref[...] = (0.5 * dh_ref[...]).astype(BF16)
            acc_ref[...] = jnp.zeros_like(acc_ref)

        @pl.when((r == 0) & (i == 0))
        def _():
            dgn_ref[...] = jnp.zeros_like(dgn_ref)

        da = _dot_nt(dhs_ref[...], wo_ref[0])
        dub = (da * fu_ref[0]).astype(BF16)
        dgb = (da * fg_ref[0]).astype(BF16)
        dg_ref[0] = dgb
        du_ref[0] = dub
        acc_ref[...] += _dot(dgb, wg_ref[0]) + _dot(dub, wu_ref[0])

        @pl.when(i == 3)
        def _():
            dx, dgain = _rms_bwd(h_ref[...], g_ref[...], acc_ref[...])
            dgn_ref[...] += dgain
            dhin_ref[...] = dh_ref[...] + dx

    row = lambda r, i: (r, 0)
    blk = lambda r, i: (i, r, 0)
    return rider_call(
        body, name, (m // tm, 4),
        in_specs=[pl.BlockSpec((tm, d), row), pl.BlockSpec((tm, d), row),
                  pl.BlockSpec((1, d), lambda r, i: (0, 0)),
                  pl.BlockSpec((1, tm, fb), blk), pl.BlockSpec((1, tm, fb), blk),
                  pl.BlockSpec((1, fb, d), lambda r, i: (i, 0, 0)),
                  pl.BlockSpec((1, fb, d), lambda r, i: (i + 4, 0, 0)),
                  pl.BlockSpec((1, fb, d), lambda r, i: (i, 0, 0))],
        out_specs=[pl.BlockSpec((tm, d), row),
                   pl.BlockSpec((1, tm, fb), blk), pl.BlockSpec((1, tm, fb), blk),
                   pl.BlockSpec((1, d), lambda r, i: (0, 0)),
                   pl.BlockSpec((tm, d), row)],
        out_shape=[jax.ShapeDtypeStruct((m, d), F32),
                   jax.ShapeDtypeStruct((4, m, fb), BF16), jax.ShapeDtypeStruct((4, m, fb), BF16),
                   jax.ShapeDtypeStruct((1, d), F32), jax.ShapeDtypeStruct((m, d), BF16)],
        scratch_shapes=[pltpu.VMEM((tm, d), F32)],
        args=(dh, h, g, f_gate, f_up, w_in8, w_in8, w_out4), rider=rider)


def matmul_tn(x, y, name, tn=None, x2=None, rider=None):
    bx, m, k = x.shape
    by, _, n = y.shape
    b = max(bx, by) * (2 if x2 is not None else 1)
    tm = _tile(m, DW_ROWS)
    tn = n if tn is None else tn
    nt = n // tn
    nr = m // tm

    def body(*refs):
        x_ref, y_ref = refs[0], refs[-3]
        o_ref, acc_ref = refs[-2], refs[-1]
        r = pl.program_id(2)

        @pl.when(r == 0)
        def _():
            acc_ref[...] = jnp.zeros_like(acc_ref)

        if x2 is None:
            acc_ref[...] += _dot_tn(x_ref[0].astype(BF16), y_ref[0].astype(BF16))
        else:
            @pl.when(pl.program_id(0) < bx)
            def _():
                acc_ref[...] += _dot_tn(x_ref[0].astype(BF16), y_ref[0].astype(BF16))

            @pl.when(pl.program_id(0) >= bx)
            def _():
                acc_ref[...] += _dot_tn(refs[1][0].astype(BF16), y_ref[0].astype(BF16))

        @pl.when(r == nr - 1)
        def _():
            o_ref[0] = acc_ref[...].astype(BF16)

    if x2 is None:
        x_specs = [pl.BlockSpec((1, tm, k), (lambda i, j, r: (i, r, 0)) if bx > 1 else (lambda i, j, r: (0, r, 0)))]
    else:
        x_specs = [pl.BlockSpec((1, tm, k), lambda i, j, r: (jnp.minimum(i, bx - 1), jnp.where(i < bx, r, nr - 1), 0)),
                   pl.BlockSpec((1, tm, k), lambda i, j, r: (jnp.maximum(i - bx, 0), jnp.where(i < bx, 0, r), 0))]
    y_map = (lambda i, j, r: (i, r, j)) if by > 1 else (lambda i, j, r: (0, r, j))
    (out,), carried = rider_call(
        body, name, (b, nt, nr),
        in_specs=x_specs + [pl.BlockSpec((1, tm, tn), y_map)],
        out_specs=[pl.BlockSpec((1, k, tn), lambda i, j, r: (i, 0, j))],
        out_shape=[jax.ShapeDtypeStruct((b, k, n), BF16)],
        scratch_shapes=[pltpu.VMEM((k, tn), F32)],
        args=[x] + ([x2] if x2 is not None else []) + [y], rider=rider)
    return (out, carried) if rider is not None else out


AUG = HEAD_DIM


class _Cols:
    def __init__(self, d):
        self.d = d
        self.ga, self.gb = 0, d
        self.qa, self.ka, self.va = 2 * d, 2 * d + 1024, 2 * d + 2048
        self.qb = 2 * d + 3072
        self.kb, self.vb, self.fa = self.qb + 512, self.qb + 640, self.qb + 768
        self.np = self.qb + 1024


def mixer_proj(h, g, wp, name):
    m, d = h.shape
    npad = wp.shape[1]
    tm = _row_tile(m)

    def body(h_ref, g_ref, w_ref, n_ref, p_ref):
        xh, _ = _rms(h_ref[...])
        n = (xh * g_ref[...]).astype(BF16)
        n_ref[...] = n
        p_ref[...] = _dot(n, w_ref[...])

    return pl.pallas_call(
        body, name=name, grid=(m // tm,),
        in_specs=[pl.BlockSpec((tm, d), lambda r: (r, 0)), pl.BlockSpec((1, d), lambda r: (0, 0)),
                  pl.BlockSpec((d, npad), lambda r: (0, 0))],
        out_specs=[pl.BlockSpec((tm, d), lambda r: (r, 0)), pl.BlockSpec((tm, npad), lambda r: (r, 0))],
        out_shape=[jax.ShapeDtypeStruct((m, d), BF16), jax.ShapeDtypeStruct((m, npad), F32)],
        compiler_params=_params(1),
    )(h, g, wp)


def _head_norm(x, gain, ones):
    outs = []
    for b in range(x.shape[1] // LANES):
        xb = x[:, b * LANES:(b + 1) * LANES]
        r = lax.rsqrt(_group_mean(xb * xb, ones) + EPS)
        outs.append(xb * r * gain[:, b * LANES:(b + 1) * LANES])
    return outs


def _head_norm_bwd(x, gain, dn, ones):
    dxs, dgs = [], []
    for b in range(x.shape[1] // LANES):
        sl = slice(b * LANES, (b + 1) * LANES)
        xb, dnb = x[:, sl], dn[:, sl]
        r = lax.rsqrt(_group_mean(xb * xb, ones) + EPS)
        xh = xb * r
        dxh = dnb * gain[:, sl]
        dxs.append(r * (dxh - xh * _group_mean(dxh * xh, ones)))
        dgs.append(jnp.sum(dnb * xh, axis=0, keepdims=True))
    return dxs, dgs


def _lane_col(v, lane_iota, idx):
    return jnp.sum(jnp.where(lane_iota == idx, v, 0.0), axis=1, keepdims=True)


def _aug(base, lane, vals):
    for i, v in enumerate(vals):
        base = jnp.where(lane == AUG + i, v, base)
    return base


def qk_post(proj, gains, fbias, cols, name):
    m = proj.shape[0]
    tm = _row_tile(m)
    gqa, gka, gqb, gkb = gains

    def body(qa_ref, ka_ref, va_ref, qb_ref, kb_ref, vb_ref, fa_ref, gqa_ref, gka_ref, gqb_ref, gkb_ref, fb_ref,
             qf_o, kf_o, vf_o, kt_o, vt_o, qb_o, kb_o, vb_o, carry_ref):
        r0 = pl.program_id(0)

        @pl.when(r0 == 0)
        def _():
            carry_ref[...] = jnp.zeros_like(carry_ref)

        z = fa_ref[...] + fb_ref[...]
        logf = jnp.minimum(z, 0.0) - jnp.log(1.0 + jnp.exp(-jnp.abs(z)))
        rr = lax.broadcasted_iota(jnp.int32, (tm, tm), 0)
        cc = lax.broadcasted_iota(jnp.int32, (tm, tm), 1)
        tril = jnp.where(cc <= rr, 1.0, 0.0).astype(BF16)
        p0, p1, p2 = _split3(logf)
        c = _dot(tril, p0) + _dot(tril, p1) + _dot(tril, p2) + carry_ref[...]
        carry_ref[...] += jnp.sum(logf, axis=0, keepdims=True)

        lane = lax.broadcasted_iota(jnp.int32, (tm, LANES), 1)
        is_pad = (r0 * tm + lax.broadcasted_iota(jnp.int32, (tm, 1), 0)) < PAD_FRONT
        ones = jnp.ones((LANES, LANES), BF16)
        for hd in range(8):
            sl = slice(hd * LANES, (hd + 1) * LANES)
            ch = _lane_col(c, lane, hd)
            ct = [p.astype(F32) for p in _split3(ch)]
            cs = [p.astype(F32) for p in _split3(-jnp.where(is_pad, -NEG, ch))]
            xq = qa_ref[:, sl]
            qn = xq * lax.rsqrt(_group_mean(xq * xq, ones) + EPS) * gqa_ref[...]
            qf_o[:, sl] = _aug(qn, lane, ct + [1.0, 1.0, 1.0]).astype(BF16)
            xk = ka_ref[:, sl]
            kn = xk * lax.rsqrt(_group_mean(xk * xk, ones) + EPS) * gka_ref[...]
            kf = _aug(kn, lane, [1.0, 1.0, 1.0] + cs)
            vf = _aug(va_ref[:, sl], lane, [1.0, 1.0, 1.0])
            kf_o[:, sl] = kf.astype(BF16)
            vf_o[:, sl] = vf.astype(BF16)
            kt_o[sl, :] = kf.T.astype(BF16)
            vt_o[sl, :] = vf.T.astype(BF16)

        gones = _group_ones()
        for src, gn, dst in ((qb_ref, gqb_ref, qb_o), (kb_ref, gkb_ref, kb_o)):
            for b, blk in enumerate(_head_norm(src[...], gn[...], gones)):
                dst[:, b * LANES:(b + 1) * LANES] = blk.astype(BF16)
        vb_o[...] = vb_ref[...].astype(BF16)

    w1024 = lambda off: pl.BlockSpec((tm, 1024), lambda r, o=off // 1024: (r, o))
    w512 = lambda off: pl.BlockSpec((tm, 512), lambda r, o=off // 512: (r, o))
    w128 = lambda off: pl.BlockSpec((tm, LANES), lambda r, o=off // LANES: (r, o))
    vec = lambda w: pl.BlockSpec((1, w), lambda r: (0, 0))
    row = lambda w: pl.BlockSpec((tm, w), lambda r: (r, 0))
    return pl.pallas_call(
        body, name=name, grid=(m // tm,),
        in_specs=[w1024(cols.qa), w1024(cols.ka), w1024(cols.va), w512(cols.qb), w128(cols.kb), w128(cols.vb),
                  w128(cols.fa), vec(LANES), vec(LANES), vec(512), vec(LANES), vec(LANES)],
        out_specs=[row(1024), row(1024), row(1024)] + [pl.BlockSpec((1024, tm), lambda r: (0, r))] * 2
                  + [row(512), row(LANES), row(LANES)],
        out_shape=[jax.ShapeDtypeStruct((m, 1024), BF16)] * 3 + [jax.ShapeDtypeStruct((1024, m), BF16)] * 2
                  + [jax.ShapeDtypeStruct((m, 512), BF16)] + [jax.ShapeDtypeStruct((m, LANES), BF16)] * 2,
        scratch_shapes=[pltpu.VMEM((1, LANES), F32)],
        compiler_params=_params(1),
    )(proj, proj, proj, proj, proj, proj, proj, gqa, gka, gqb, gkb, fbias)


def qk_post_bwd(proj, gains, fbias, dqf, dkf, dvf, dqb, dkb, dvb, dc, dga, dgb, cols, name):
    m = proj.shape[0]
    d = cols.d
    tm = _row_tile(m)
    nt = m // tm
    gqa, gka, gqb, gkb = gains

    def body(qa_ref, ka_ref, qb_ref, kb_ref, fa_ref, gqa_ref, gka_ref, gqb_ref, gkb_ref, fb_ref,
             dqf_ref, dkf_ref, dvf_ref, dqb_ref, dkb_ref, dvb_ref, dc_ref, dga_ref, dgb_ref,
             dp_o, ggqa_o, ggka_o, ggqb_o, ggkb_o, gfb_o, carry_ref):
        @pl.when(pl.program_id(0) == 0)
        def _():
            carry_ref[...] = jnp.zeros_like(carry_ref)
            for o in (ggqa_o, ggka_o, ggqb_o, ggkb_o, gfb_o):
                o[...] = jnp.zeros_like(o)

        dp_o[:, cols.ga:cols.ga + d] = dga_ref[...].astype(BF16)
        dp_o[:, cols.gb:cols.gb + d] = dgb_ref[...].astype(BF16)
        dp_o[:, cols.fa + LANES:cols.np] = jnp.zeros((tm, cols.np - cols.fa - LANES), BF16)
        lane = lax.broadcasted_iota(jnp.int32, (tm, LANES), 1)
        data = lane < HEAD_DIM
        ones = jnp.ones((LANES, LANES), BF16)
        for hd in range(8):
            sl = slice(hd * LANES, (hd + 1) * LANES)
            for src, gn, dn_ref, off, gout in ((qa_ref, gqa_ref, dqf_ref, cols.qa, ggqa_o),
                                               (ka_ref, gka_ref, dkf_ref, cols.ka, ggka_o)):
                x = src[:, sl]
                dn = jnp.where(data, dn_ref[:, sl], 0.0)
                r = lax.rsqrt(_group_mean(x * x, ones) + EPS)
                xh = x * r
                dxh = dn * gn[...]
                dp_o[:, off + hd * LANES:off + (hd + 1) * LANES] = (
                    r * (dxh - xh * _group_mean(dxh * xh, ones))).astype(BF16)
                gout[...] += jnp.sum(dn * xh, axis=0, keepdims=True)
            dp_o[:, cols.va + hd * LANES:cols.va + (hd + 1) * LANES] = jnp.where(data, dvf_ref[:, sl], 0.0).astype(BF16)
        dp_o[:, cols.vb:cols.vb + LANES] = dvb_ref[...].astype(BF16)
        gones = _group_ones()
        for src, gn, dn, off, gout in ((qb_ref, gqb_ref, dqb_ref, cols.qb, ggqb_o),
                                       (kb_ref, gkb_ref, dkb_ref, cols.kb, ggkb_o)):
            dxs, dgs = _head_norm_bwd(src[...], gn[...], dn[...], gones)
            for b, (dx, dg) in enumerate(zip(dxs, dgs)):
                dp_o[:, off + b * LANES:off + (b + 1) * LANES] = dx.astype(BF16)
                gout[:, b * LANES:(b + 1) * LANES] += dg
        dcv = dc_ref[...]
        rr = lax.broadcasted_iota(jnp.int32, (tm, tm), 0)
        cc = lax.broadcasted_iota(jnp.int32, (tm, tm), 1)
        triu = jnp.where(cc >= rr, 1.0, 0.0).astype(BF16)
        p0, p1, p2 = _split3(dcv)
        dlogf = _dot(triu, p0) + _dot(triu, p1) + _dot(triu, p2) + carry_ref[...]
        carry_ref[...] += jnp.sum(dcv, axis=0, keepdims=True)
        z = fa_ref[...] + fb_ref[...]
        row = (nt - 1 - pl.program_id(0)) * tm + lax.broadcasted_iota(jnp.int32, (tm, LANES), 0)
        dfa = jnp.where(row >= PAD_FRONT, dlogf * jax.nn.sigmoid(-z), 0.0)
        dp_o[:, cols.fa:cols.fa + LANES] = dfa.astype(BF16)
        gfb_o[...] += jnp.sum(dfa, axis=0, keepdims=True)

    rev = lambda r: nt - 1 - r
    w1024 = lambda off: pl.BlockSpec((tm, 1024), lambda r, o=off // 1024: (rev(r), o))
    w512 = lambda off: pl.BlockSpec((tm, 512), lambda r, o=off // 512: (rev(r), o))
    w128 = lambda off: pl.BlockSpec((tm, LANES), lambda r, o=off // LANES: (rev(r), o))
    vec = lambda w: pl.BlockSpec((1, w), lambda r: (0, 0))
    row = lambda w: pl.BlockSpec((tm, w), lambda r: (rev(r), 0))
    return pl.pallas_call(
        body, name=name, grid=(nt,),
        in_specs=[w1024(cols.qa), w1024(cols.ka), w512(cols.qb), w128(cols.kb), w128(cols.fa),
                  vec(LANES), vec(LANES), vec(512), vec(LANES), vec(LANES),
                  row(1024), row(1024), row(1024), row(512), row(LANES), row(LANES), row(LANES), row(d), row(d)],
        out_specs=[row(cols.np), vec(LANES), vec(LANES), vec(512), vec(LANES), vec(LANES)],
        out_shape=[jax.ShapeDtypeStruct((m, cols.np), BF16), jax.ShapeDtypeStruct((1, LANES), F32),
                   jax.ShapeDtypeStruct((1, LANES), F32), jax.ShapeDtypeStruct((1, 512), F32),
                   jax.ShapeDtypeStruct((1, LANES), F32), jax.ShapeDtypeStruct((1, LANES), F32)],
        scratch_shapes=[pltpu.VMEM((1, LANES), F32)],
        compiler_params=_params(1),
    )(proj, proj, proj, proj, proj, gqa, gka, gqb, gkb, fbias, dqf, dkf, dvf, dqb, dkb, dvb, dc, dga, dgb)


def dproj_bwd(dh, h, g, dproj, wp, name):
    m, d = h.shape
    npad = wp.shape[1]
    tm = _row_tile(m)

    def body(dh_ref, h_ref, g_ref, dp_ref, w_ref, dhin_ref, dgn_ref):
        @pl.when(pl.program_id(0) == 0)
        def _():
            dgn_ref[...] = jnp.zeros_like(dgn_ref)

        dn = _dot_nt(dp_ref[...], w_ref[...])
        dx, dgain = _rms_bwd(h_ref[...], g_ref[...], dn)
        dgn_ref[...] += dgain
        dhin_ref[...] = dh_ref[...] + dx

    row = lambda w: pl.BlockSpec((tm, w), lambda r: (r, 0))
    return pl.pallas_call(
        body, name=name, grid=(m // tm,),
        in_specs=[row(d), row(d), pl.BlockSpec((1, d), lambda r: (0, 0)), row(npad),
                  pl.BlockSpec((d, npad), lambda r: (0, 0))],
        out_specs=[row(d), pl.BlockSpec((1, d), lambda r: (0, 0))],
        out_shape=[jax.ShapeDtypeStruct((m, d), F32), jax.ShapeDtypeStruct((1, d), F32)],
        compiler_params=_params(1),
    )(dh, h, g, dproj, wp)


def _causal_t(t):
    return lax.broadcasted_iota(jnp.int32, (t, t), 0) <= lax.broadcasted_iota(jnp.int32, (t, t), 1)


HEADS_PER_STEP = 4


def fox_fwd(qf, kf, vt, name, rider=None):
    m = qf.shape[0]
    t = _row_tile(m)
    nq = m // t
    hp = HEADS_PER_STEP
    w = hp * LANES

    def body(q_ref, k_ref, vt_ref, o_ref, lse_ref, acc_ref, m_ref, p_ref, a_ref):
        qi = pl.program_id(1)
        acc_ref[...] = jnp.zeros_like(acc_ref)
        m_ref[...] = jnp.full_like(m_ref, NEG)

        def scores(ki, slot, mask):
            off = pl.multiple_of(ki * t, t)
            for e in range(hp):
                sl = slice(e * LANES, (e + 1) * LANES)
                s = _dot_nt(k_ref[pl.ds(off, t), sl], q_ref[:, sl])
                if mask is not None:
                    s = jnp.where(mask, s, NEG)
                m_old = m_ref[e]
                m_new = jnp.maximum(m_old, jnp.max(s, axis=0, keepdims=True))
                p_ref[slot, e] = jnp.exp(s - m_new).astype(BF16)
                a_ref[slot, e] = jnp.exp(m_old - m_new)
                m_ref[e] = m_new

        def values(ki, slot):
            off = pl.multiple_of(ki * t, t)
            for e in range(hp):
                sl = slice(e * LANES, (e + 1) * LANES)
                acc_ref[e] = acc_ref[e] * a_ref[slot, e] + _dot(vt_ref[sl, pl.ds(off, t)], p_ref[slot, e])

        causal = _causal_t(t)
        scores(0, 0, causal | (jnp.full((t, t), qi, jnp.int32) > 0))

        def step(ki, carry):
            values(ki - 1, (ki - 1) % 2)
            scores(ki, ki % 2, None)
            return carry

        lax.fori_loop(1, qi, step, 0)

        @pl.when(qi >= 1)
        def _():
            values(qi - 1, (qi - 1) % 2)
            scores(qi, qi % 2, causal)

        values(qi, qi % 2)
        row = lax.broadcasted_iota(jnp.int32, (LANES, t), 0)
        for e in range(hp):
            l = jnp.max(acc_ref[e, AUG:AUG + 8, :], axis=0, keepdims=True)
            o_ref[:, e * LANES:(e + 1) * LANES] = jnp.where(row < HEAD_DIM, acc_ref[e] * (1.0 / l), 0.0).T
            lse_ref[e] = m_ref[e] + jnp.log(l)

    return rider_call(
        body, name, (8 // hp, nq),
        in_specs=[pl.BlockSpec((t, w), lambda hd, i: (i, hd)),
                  pl.BlockSpec((m, w), lambda hd, i: (0, hd)),
                  pl.BlockSpec((w, m), lambda hd, i: (hd, 0))],
        out_specs=[pl.BlockSpec((t, w), lambda hd, i: (i, hd)),
                   pl.BlockSpec((hp, 1, t), lambda hd, i: (hd, 0, i))],
        out_shape=[jax.ShapeDtypeStruct((m, 8 * LANES), F32), jax.ShapeDtypeStruct((8, 1, m), F32)],
        scratch_shapes=[pltpu.VMEM((hp, LANES, t), F32), pltpu.VMEM((hp, 1, t), F32),
                        pltpu.VMEM((2, hp, t, t), BF16), pltpu.VMEM((2, hp, 1, t), F32)],
        args=(qf, kf, vt), rider=rider)


def fox_bwd(qf, kf, vf, kt, dof, lse, delta, name, rider=None):
    m = qf.shape[0]
    t = _row_tile(m)
    nq = m // t
    hp = HEADS_PER_STEP
    w = hp * LANES

    def body(k_ref, v_ref, kt_ref, q_ref, do_ref, lse_ref, delta_ref, dk_ref, dv_ref, dq_ref, dck_ref, dcq_ref,
             dka_ref, dva_ref, dqt_ref):
        ki = pl.program_id(1)

        @pl.when(ki == 0)
        def _():
            dqt_ref[...] = jnp.zeros_like(dqt_ref)

        dka_ref[...] = jnp.zeros_like(dka_ref)
        dva_ref[...] = jnp.zeros_like(dva_ref)

        def tile(qi, diagonal):
            off = pl.multiple_of(qi * t, t)
            for e in range(hp):
                sl = slice(e * LANES, (e + 1) * LANES)
                q = q_ref[pl.ds(off, t), sl]
                do = do_ref[pl.ds(off, t), sl]
                s = _dot_nt(k_ref[:, sl], q)
                if diagonal:
                    s = jnp.where(_causal_t(t), s, NEG)
                p = jnp.exp(s - lse_ref[e, :, pl.ds(off, t)])
                ds = (p * (_dot_nt(v_ref[:, sl], do) - delta_ref[e, :, pl.ds(off, t)])).astype(BF16)
                dva_ref[:, sl] += _dot(p.astype(BF16), do)
                dka_ref[:, sl] += _dot(ds, q)
                dqt_ref[sl, pl.ds(off, t)] += _dot(kt_ref[sl, :], ds)

        def step(qi, carry):
            tile(qi, False)
            return carry

        tile(ki, True)
        lax.fori_loop(ki + 1, nq, step, 0)
        dk_ref[...] = dka_ref[...]
        dv_ref[...] = dva_ref[...]
        row8 = lax.broadcasted_iota(jnp.int32, (8, 1), 0)
        for e in range(hp):
            slab = dka_ref[:, e * LANES:(e + 1) * LANES].T[AUG:AUG + 8, :]
            dck_ref[e] = -jnp.sum(jnp.where(row8 == 3, slab, 0.0), axis=0, keepdims=True)

        @pl.when(ki == nq - 1)
        def _():
            for e in range(hp):
                sl = slice(e * LANES, (e + 1) * LANES)
                slab = dqt_ref[e * LANES + AUG:e * LANES + AUG + 8, :]
                dcq_ref[e] = jnp.sum(jnp.where(row8 == 0, slab, 0.0), axis=0, keepdims=True)
                for j in range(nq):
                    dq_ref[j * t:(j + 1) * t, sl] = dqt_ref[sl, j * t:(j + 1) * t].T

    tile_spec = pl.BlockSpec((t, w), lambda hd, i: (i, hd))
    full = pl.BlockSpec((m, w), lambda hd, i: (0, hd))
    stat = pl.BlockSpec((hp, 1, m), lambda hd, i: (hd, 0, 0))
    (dkf, dvf, dqf, dck, dcq), carried = rider_call(
        body, name, (8 // hp, nq),
        in_specs=[tile_spec, tile_spec, pl.BlockSpec((w, t), lambda hd, i: (hd, i)), full, full, stat, stat],
        out_specs=[tile_spec, tile_spec, full, pl.BlockSpec((hp, 1, t), lambda hd, i: (hd, 0, i)), stat],
        out_shape=[jax.ShapeDtypeStruct((m, 8 * LANES), F32), jax.ShapeDtypeStruct((m, 8 * LANES), F32),
                   jax.ShapeDtypeStruct((m, 8 * LANES), F32), jax.ShapeDtypeStruct((8, 1, m), F32),
                   jax.ShapeDtypeStruct((8, 1, m), F32)],
        scratch_shapes=[pltpu.VMEM((t, w), F32), pltpu.VMEM((t, w), F32), pltpu.VMEM((w, m), F32)],
        args=(kf, vf, kt, qf, dof, lse, delta), rider=rider)
    return (dkf, dvf, dqf, dcq + dck), carried


def _bucket_ids():
    def bucket(dist):
        n = np.maximum(dist, 0)
        max_exact = N_BUCKETS // 2
        nf = np.maximum(n, 1).astype(np.float32)
        large = max_exact + (np.log(nf / max_exact) / math.log(MAX_DISTANCE / max_exact)
                             * (N_BUCKETS - max_exact)).astype(np.int32)
        return np.where(n < max_exact, n, np.minimum(large, N_BUCKETS - 1))

    tl = np.arange(LANES)[:, None]
    sl = np.arange(LANES)[None, :]
    prev = bucket(LANES + tl - sl)
    cur = bucket(tl - sl)
    meta = np.full((LANES, LANES), N_BUCKETS - 1)
    return np.concatenate([prev, cur, meta], axis=1).astype(np.int32)


def bias_build(table, name):
    ids = jnp.asarray(_bucket_ids())

    def body(t_ref, id_ref, o_ref):
        idv = id_ref[...]
        for h in range(8):
            acc = jnp.zeros((LANES, 3 * LANES), F32)
            for b in range(N_BUCKETS):
                acc = jnp.where(idv == b, t_ref[b, h], acc)
            o_ref[h] = acc

    return pl.pallas_call(
        body, name=name,
        in_specs=[pl.BlockSpec(memory_space=pltpu.SMEM), pl.BlockSpec(memory_space=pltpu.VMEM)],
        out_specs=pl.BlockSpec(memory_space=pltpu.VMEM),
        out_shape=jax.ShapeDtypeStruct((8, LANES, 3 * LANES), F32),
    )(table, ids)


def bias_reduce(dbias, name):
    ids = jnp.asarray(_bucket_ids())

    def body(d_ref, id_ref, o_ref):
        idv = id_ref[...]
        rr = lax.broadcasted_iota(jnp.int32, (N_BUCKETS, LANES), 0)
        cc = lax.broadcasted_iota(jnp.int32, (N_BUCKETS, LANES), 1)
        acc = jnp.zeros((N_BUCKETS, LANES), F32)
        for h in range(8):
            dv = d_ref[h]
            for b in range(N_BUCKETS):
                val = jnp.sum(jnp.where(idv == b, dv, 0.0), keepdims=True)
                acc = jnp.where((rr == b) & (cc == h), val, acc)
        o_ref[...] = acc

    return pl.pallas_call(
        body, name=name,
        in_specs=[pl.BlockSpec(memory_space=pltpu.VMEM), pl.BlockSpec(memory_space=pltpu.VMEM)],
        out_specs=pl.BlockSpec(memory_space=pltpu.VMEM),
        out_shape=jax.ShapeDtypeStruct((N_BUCKETS, LANES), F32),
    )(dbias, ids)


def _swa_penalty(n):
    shape = (LANES, 3 * LANES)
    tl = lax.broadcasted_iota(jnp.int32, shape, 0)
    col = lax.broadcasted_iota(jnp.int32, shape, 1)
    sl = col & (LANES - 1)
    nv = jnp.full(shape, n, jnp.int32)
    is_meta = sl >= PAD_FRONT
    prev = (col < LANES) & (sl > tl) & (nv >= 1) & ((nv >= 2) | is_meta)
    cur = (col >= LANES) & (col < 2 * LANES) & (sl <= tl) & ((nv >= 1) | is_meta)
    meta = (col >= 2 * LANES) & is_meta & ((nv >= 2) | ((nv == 1) & (sl <= tl)))
    return jnp.where(prev | cur | meta, 0.0, NEG)


def _swa_keys(ref, n):
    off_prev = pl.multiple_of(jnp.maximum(n - 1, 0) * LANES, LANES)
    off_cur = pl.multiple_of(n * LANES, LANES)
    return jnp.concatenate([ref[pl.ds(off_prev, LANES), :], ref[pl.ds(off_cur, LANES), :], ref[0:LANES, :]], axis=0)


def swa_fwd(q, k, v, bias, sinks, name):
    m = q.shape[0]

    def body(q_ref, k_ref, v_ref, bias_ref, sink_ref, o_ref, lse_ref):
        n = pl.program_id(0)
        lane1 = lax.broadcasted_iota(jnp.int32, (1, LANES), 1)
        lane_t = lax.broadcasted_iota(jnp.int32, (LANES, LANES), 1)
        in_head = [lane1 < HEAD_DIM, lane1 >= HEAD_DIM]
        kall = _swa_keys(k_ref, n)
        vall = _swa_keys(v_ref, n)
        vs = [jnp.where(in_head[g], vall, jnp.zeros_like(vall)) for g in (0, 1)]
        penalty = _swa_penalty(n)
        lse = jnp.zeros((LANES, LANES), F32)
        for b in range(4):
            qb = q_ref[:, b * LANES:(b + 1) * LANES]
            ob = jnp.zeros((LANES, LANES), F32)
            for g in (0, 1):
                h = 4 * g + b
                qe = jnp.where(in_head[g], qb, jnp.zeros_like(qb))
                s = _dot_nt(qe, kall) + bias_ref[h] + penalty
                sink = sink_ref[h]
                mx = jnp.maximum(jnp.max(s, axis=1, keepdims=True), sink)
                p = jnp.exp(s - mx)
                den = jnp.sum(p, axis=1, keepdims=True) + jnp.exp(sink - mx)
                ob = ob + _dot((p / den).astype(BF16), vs[g])
                lse = jnp.where(lane_t == h, mx + jnp.log(den), lse)
            o_ref[:, b * LANES:(b + 1) * LANES] = ob
        lse_ref[...] = lse

    return pl.pallas_call(
        body, name=name, grid=(m // LANES,),
        in_specs=[pl.BlockSpec((LANES, 512), lambda n: (n, 0)),
                  pl.BlockSpec((m, LANES), lambda n: (0, 0)), pl.BlockSpec((m, LANES), lambda n: (0, 0)),
                  pl.BlockSpec((8, LANES, 3 * LANES), lambda n: (0, 0, 0)),
                  pl.BlockSpec(memory_space=pltpu.SMEM)],
        out_specs=[pl.BlockSpec((LANES, 512), lambda n: (n, 0)), pl.BlockSpec((LANES, LANES), lambda n: (n, 0))],
        out_shape=[jax.ShapeDtypeStruct((m, 512), F32), jax.ShapeDtypeStruct((m, LANES), F32)],
        compiler_params=_params(1),
    )(q, k, v, bias, sinks)


def swa_bwd(q, k, v, bias, sinks, o, lse, do, name):
    m = q.shape[0]

    def body(q_ref, do_ref, o_ref, lse_ref, k_ref, v_ref, bias_ref, sink_ref,
             dq_ref, dk_ref, dv_ref, dbias_ref, dsink_ref):
        n = pl.program_id(0)

        @pl.when(n == 0)
        def _():
            for r in (dk_ref, dv_ref, dbias_ref, dsink_ref):
                r[...] = jnp.zeros_like(r)

        lane1 = lax.broadcasted_iota(jnp.int32, (1, LANES), 1)
        lane_t = lax.broadcasted_iota(jnp.int32, (LANES, LANES), 1)
        in_head = [lane1 < HEAD_DIM, lane1 >= HEAD_DIM]
        off_prev = pl.multiple_of(jnp.maximum(n - 1, 0) * LANES, LANES)
        off_cur = pl.multiple_of(n * LANES, LANES)
        kall = _swa_keys(k_ref, n)
        vall = _swa_keys(v_ref, n)
        ks = [jnp.where(in_head[g], kall, jnp.zeros_like(kall)) for g in (0, 1)]
        penalty = _swa_penalty(n)
        lsev = lse_ref[...]
        dsink = dsink_ref[...]
        dkall = jnp.zeros((3 * LANES, LANES), F32)
        dvall = jnp.zeros((3 * LANES, LANES), F32)
        for b in range(4):
            sl = slice(b * LANES, (b + 1) * LANES)
            qb = q_ref[:, sl]
            dob = do_ref[:, sl]
            prod = dob * o_ref[:, sl]
            dqb = jnp.zeros((LANES, LANES), F32)
            for g in (0, 1):
                h = 4 * g + b
                qe = jnp.where(in_head[g], qb, jnp.zeros_like(qb))
                doe = jnp.where(in_head[g], dob, 0.0).astype(BF16)
                delta = jnp.sum(jnp.where(in_head[g], prod, 0.0), axis=1, keepdims=True)
                lse_h = _lane_col(lsev, lane_t, h)
                s = _dot_nt(qe, kall) + bias_ref[h] + penalty
                p = jnp.exp(s - lse_h)
                ds = p * (_dot_nt(doe, vall) - delta)
                dbias_ref[h] += ds
                sink_part = jnp.sum(-jnp.exp(sink_ref[h] - lse_h) * delta, keepdims=True)
                dsink = jnp.where(lane1 == h, dsink + sink_part, dsink)
                dsb = ds.astype(BF16)
                dqb = dqb + _dot(dsb, ks[g])
                dkall = dkall + _dot_tn(dsb, qe)
                dvall = dvall + _dot_tn(p.astype(BF16), doe)
            dq_ref[:, sl] = dqb
        dsink_ref[...] = dsink
        for ref, val in ((dk_ref, dkall), (dv_ref, dvall)):
            ref[pl.ds(off_prev, LANES), :] += val[0:LANES]
            ref[pl.ds(off_cur, LANES), :] += val[LANES:2 * LANES]
            ref[0:LANES, :] += val[2 * LANES:3 * LANES]

    blk = pl.BlockSpec((LANES, 512), lambda n: (n, 0))
    full = pl.BlockSpec((m, LANES), lambda n: (0, 0))
    return pl.pallas_call(
        body, name=name, grid=(m // LANES,),
        in_specs=[blk, blk, blk, pl.BlockSpec((LANES, LANES), lambda n: (n, 0)), full, full,
                  pl.BlockSpec((8, LANES, 3 * LANES), lambda n: (0, 0, 0)),
                  pl.BlockSpec(memory_space=pltpu.SMEM)],
        out_specs=[blk, full, full, pl.BlockSpec((8, LANES, 3 * LANES), lambda n: (0, 0, 0)),
                   pl.BlockSpec((1, LANES), lambda n: (0, 0))],
        out_shape=[jax.ShapeDtypeStruct((m, 512), F32), jax.ShapeDtypeStruct((m, LANES), F32),
                   jax.ShapeDtypeStruct((m, LANES), F32), jax.ShapeDtypeStruct((8, LANES, 3 * LANES), F32),
                   jax.ShapeDtypeStruct((1, LANES), F32)],
        compiler_params=_params(1),
    )(q, do, o, lse, k, v, bias, sinks)


def branch_out(h, o_fox, o_swa, proj, wbf, wbs, wo, cols, name):
    m, d = h.shape
    tm = _row_tile(m)

    def body(h_ref, of_ref, os_ref, ga_ref, gb_ref, wbf_ref, wbs_ref, wo_ref, hn_ref):
        tf = _dot(of_ref[...].astype(BF16), wbf_ref[...])
        ts = _dot(os_ref[...].astype(BF16), wbs_ref[...])
        y = jax.nn.sigmoid(ga_ref[...]) * tf + jax.nn.sigmoid(gb_ref[...]) * ts
        hn_ref[...] = h_ref[...] + _dot(y.astype(BF16), wo_ref[...])

    row = lambda w, o=0: pl.BlockSpec((tm, w), lambda r, o=o: (r, o))
    res = lambda a: pl.BlockSpec(a.shape, lambda r: (0, 0))
    return pl.pallas_call(
        body, name=name, grid=(m // tm,),
        in_specs=[row(d), row(1024), row(512), row(d, cols.ga // d), row(d, cols.gb // d), res(wbf), res(wbs), res(wo)],
        out_specs=row(d),
        out_shape=jax.ShapeDtypeStruct((m, d), F32),
        compiler_params=_params(1),
    )(h, o_fox, o_swa, proj, proj, wbf, wbs, wo)


def branch_out_bwd(dh, o_fox, o_swa, proj, wbf, wbs, wo, cols, name):
    m, d = dh.shape
    tm = _row_tile(m)

    def body(dh_ref, of_ref, os_ref, ga_ref, gb_ref, wbf_ref, wbs_ref, wo_ref,
             y_ref, dtf_ref, dts_ref, dga_ref, dgb_ref, dof_ref, dos_ref, delta_ref):
        dy = _dot_nt(dh_ref[...].astype(BF16), wo_ref[...])
        tf = _dot(of_ref[...].astype(BF16), wbf_ref[...])
        ts = _dot(os_ref[...].astype(BF16), wbs_ref[...])
        sa = jax.nn.sigmoid(ga_ref[...])
        sb = jax.nn.sigmoid(gb_ref[...])
        y_ref[...] = (sa * tf + sb * ts).astype(BF16)
        dtf = (dy * sa).astype(BF16)
        dts = (dy * sb).astype(BF16)
        dtf_ref[...] = dtf
        dts_ref[...] = dts
        dga_ref[...] = (dy * tf * sa * (1.0 - sa)).astype(BF16)
        dgb_ref[...] = (dy * ts * sb * (1.0 - sb)).astype(BF16)
        dof = _dot_nt(dtf, wbf_ref[...])
        dof_ref[...] = dof.astype(BF16)
        dos_ref[...] = _dot_nt(dts, wbs_ref[...])
        lane = lax.broadcasted_iota(jnp.int32, (tm, LANES), 1)
        delta = jnp.zeros((tm, LANES), F32)
        for hd in range(8):
            sl = slice(hd * LANES, (hd + 1) * LANES)
            delta = jnp.where(lane == hd, jnp.sum(dof[:, sl] * of_ref[:, sl], axis=1, keepdims=True), delta)
        delta_ref[...] = delta

    row = lambda w, o=0: pl.BlockSpec((tm, w), lambda r, o=o: (r, o))
    res = lambda a: pl.BlockSpec(a.shape, lambda r: (0, 0))
    return pl.pallas_call(
        body, name=name, grid=(m // tm,),
        in_specs=[row(d), row(1024), row(512), row(d, cols.ga // d), row(d, cols.gb // d), res(wbf), res(wbs), res(wo)],
        out_specs=[row(d)] * 5 + [row(1024), row(512), row(LANES)],
        out_shape=[jax.ShapeDtypeStruct((m, d), BF16)] * 5 + [jax.ShapeDtypeStruct((m, 1024), BF16),
                   jax.ShapeDtypeStruct((m, 512), F32), jax.ShapeDtypeStruct((m, LANES), F32)],
        compiler_params=_params(1),
    )(dh, o_fox, o_swa, proj, proj, wbf, wbs, wo)


def loss_head(h, target, name):
    m, d = h.shape

    def body(h_ref, t_ref, dh_ref, loss_ref):
        n = pl.program_id(0)

        @pl.when(n == 0)
        def _():
            loss_ref[...] = jnp.zeros_like(loss_ref)
            dh_ref[...] = jnp.zeros_like(dh_ref)

        @pl.when(n > 0)
        def _():
            err = h_ref[...] - t_ref[...]
            dh_ref[...] = err * (1.0 / d)
            loss_ref[...] += jnp.sum(err * err, keepdims=True) * (0.5 / d)

    return pl.pallas_call(
        body, name=name, grid=(m // LANES,),
        in_specs=[pl.BlockSpec((LANES, d), lambda n: (n, 0)),
                  pl.BlockSpec((LANES, d), lambda n: (jnp.maximum(n - 1, 0), 0))],
        out_specs=[pl.BlockSpec((LANES, d), lambda n: (n, 0)), pl.BlockSpec((8, LANES), lambda n: (0, 0))],
        out_shape=[jax.ShapeDtypeStruct((m, d), F32), jax.ShapeDtypeStruct((8, LANES), F32)],
        compiler_params=_params(1),
    )(h, target)


def _adamw_math(w, g, m, v):
    m = ADAM_B1 * m + (1.0 - ADAM_B1) * g
    v = ADAM_B2 * v + (1.0 - ADAM_B2) * (g * g)
    m_hat = m / (1.0 - ADAM_B1 ** ADAM_STEP)
    v_hat = v / (1.0 - ADAM_B2 ** ADAM_STEP)
    delta = -ADAM_LR * (m_hat / (jnp.sqrt(v_hat) + ADAM_EPS) + ADAM_WD * w)
    return delta, m, v


def adamw_sum(parts, w, m, v, name, after=None):
    n_layers, a, b = w.shape
    ta = next(t for t in (256, 176, 128, a) if a % t == 0)
    nr = a // ta

    def body(*refs):
        p_refs = refs[:n_layers]
        w_ref, m_ref, v_ref = refs[n_layers:n_layers + 3]
        g_o, d_o, m_o, v_o = refs[-4:]
        for l in range(n_layers):
            @pl.when(pl.program_id(0) == l)
            def _(l=l):
                g = p_refs[l][0].astype(F32)
                for j in range(1, N_DEV):
                    g = g + p_refs[l][j].astype(F32)
                g_o[0] = g
                d_o[0], m_o[0], v_o[0] = _adamw_math(w_ref[0], g, m_ref[0], v_ref[0])

    def part_spec(l):
        return pl.BlockSpec((N_DEV, ta, b), lambda i, r, l=l: (0, jnp.where(i == l, r, jnp.where(i < l, 0, nr - 1)), 0))

    row = pl.BlockSpec((1, ta, b), lambda i, r: (i, r, 0))
    return pl.pallas_call(
        body, name=name, grid=(n_layers, nr),
        in_specs=[part_spec(l) for l in range(n_layers)] + [row, row, row]
                 + ([pl.BlockSpec(memory_space=pl.ANY)] if after is not None else []),
        out_specs=[row] * 4,
        out_shape=[jax.ShapeDtypeStruct(w.shape, F32)] * 4,
        compiler_params=_params(2),
    )(*parts, w, m, v, *([after] if after is not None else []))


def adamw_small(g, w, m, v, name):
    def body(g_ref, w_ref, m_ref, v_ref, d_o, m_o, v_o):
        d_o[...], m_o[...], v_o[...] = _adamw_math(w_ref[...], g_ref[...], m_ref[...], v_ref[...])

    spec = pl.BlockSpec(memory_space=pltpu.VMEM)
    return pl.pallas_call(
        body, name=name, in_specs=[spec] * 4, out_specs=[spec] * 3,
        out_shape=[jax.ShapeDtypeStruct(w.shape, F32)] * 3,
    )(g, w, m, v)


BIG = ("ffn1_w_in", "ffn1_w_out", "w_in", "w_branch_fox", "w_branch_swa", "w_out", "ffn2_w_in", "ffn2_w_out")
SMALL = ("rel_bias_table", "ffn1_norm", "mix_norm", "forget_bias", "fox_q_norm", "fox_k_norm",
         "swa_q_norm", "swa_k_norm", "swa_sinks", "ffn2_norm")
WEIGHTS = ("meta_tokens", "rel_bias_table", "ffn1_norm", "ffn1_w_in", "ffn1_w_out", "mix_norm", "w_in",
           "forget_bias", "fox_q_norm", "fox_k_norm", "swa_q_norm", "swa_k_norm", "swa_sinks", "w_branch_fox",
           "w_branch_swa", "w_out", "ffn2_norm", "ffn2_w_in", "ffn2_w_out")


def _pack(arrs, width, row_multiple, dtype):
    lead = arrs[0].shape[:-1]
    flat = jnp.concatenate([a.astype(dtype) for a in arrs], axis=-1)
    n = flat.shape[-1]
    rows = -(-n // width)
    rows = -(-rows // row_multiple) * row_multiple
    flat = jnp.pad(flat, [(0, 0)] * len(lead) + [(0, rows * width - n)])
    return flat.reshape(lead + (rows, width))


def _unpack(flat, shapes):
    flat = flat.reshape(-1)
    out, off = [], 0
    for s in shapes:
        n = int(np.prod(s))
        out.append(flat[off:off + n].reshape(s))
        off += n
    return out


def _swa_head_order():
    return [4 * (j % 2) + j // 2 for j in range(8)]


def _permute_heads(a, axis, inverse=False):
    order = _swa_head_order()
    if inverse:
        order = [order.index(hd) for hd in range(8)]
    parts = [lax.slice_in_dim(a, hd * HEAD_DIM, (hd + 1) * HEAD_DIM, axis=axis) for hd in order]
    return jnp.concatenate(parts, axis=axis)


def _w_in_segments(cols):
    d = cols.d
    segs = [(512 * i + HEAD_DIM * hd, HEAD_DIM, new + LANES * hd)
            for i, new in enumerate((cols.qa, cols.ka, cols.va)) for hd in range(8)]
    segs.append((1536, 8, cols.fa))
    order = _swa_head_order()
    segs += [(1544 + HEAD_DIM * hd, HEAD_DIM, cols.qb + HEAD_DIM * order.index(hd)) for hd in range(8)]
    segs += [(2056, 128, cols.kb), (2184, 128, cols.vb), (2312, d, cols.ga), (2312 + d, d, cols.gb)]
    return segs


def _reorder_w_in(blocks, cols):
    width = blocks[0].shape[1]
    zeros = lambda n: jnp.zeros((blocks[0].shape[0], n), blocks[0].dtype)
    parts, at = [], 0
    for old, length, new in sorted(_w_in_segments(cols), key=lambda s: s[2]):
        if new > at:
            parts.append(zeros(new - at))
        at = new + length
        while length:
            j, off = divmod(old, width)
            take = min(length, width - off)
            parts.append(blocks[j][:, off:off + take])
            old, length = old + take, length - take
    parts.append(zeros(cols.np - at))
    return jnp.concatenate(parts, axis=1)


def _restore_w_in(wp, cols, width):
    segs = sorted(_w_in_segments(cols))
    blocks = []
    for j in range(N_DEV):
        lo, hi = j * width, (j + 1) * width
        parts = []
        for old, length, new in segs:
            a, b = max(old, lo), min(old + length, hi)
            if a < b:
                parts.append(wp[:, new + a - old:new + b - old])
        blocks.append(jnp.concatenate(parts, axis=1))
    return jnp.stack(blocks)


def _lane_pad(v):
    return jnp.pad(v, ((0, 0), (0, LANES - v.shape[1])))


def kernel(x, meta_tokens, rel_bias_table, ffn1_norm, ffn1_w_in, ffn1_w_out, mix_norm, w_in, forget_bias, fox_q_norm, fox_k_norm, swa_q_norm, swa_k_norm, swa_sinks, w_branch_fox, w_branch_swa, w_out, ffn2_norm, ffn2_w_in, ffn2_w_out, loss_target, m_meta_tokens, m_rel_bias_table, m_ffn1_norm, m_ffn1_w_in, m_ffn1_w_out, m_mix_norm, m_w_in, m_forget_bias, m_fox_q_norm, m_fox_k_norm, m_swa_q_norm, m_swa_k_norm, m_swa_sinks, m_w_branch_fox, m_w_branch_swa, m_w_out, m_ffn2_norm, m_ffn2_w_in, m_ffn2_w_out, v_meta_tokens, v_rel_bias_table, v_ffn1_norm, v_ffn1_w_in, v_ffn1_w_out, v_mix_norm, v_w_in, v_forget_bias, v_fox_q_norm, v_fox_k_norm, v_swa_q_norm, v_swa_k_norm, v_swa_sinks, v_w_branch_fox, v_w_branch_swa, v_w_out, v_ffn2_norm, v_ffn2_w_in, v_ffn2_w_out):
    args = dict(locals())
    wts = {n: args[n] for n in WEIGHTS}
    mom1 = {n: args["m_" + n] for n in WEIGHTS}
    mom2 = {n: args["v_" + n] for n in WEIGHTS}

    seq, d = x.shape[1], x.shape[2]
    m_rows = seq + LANES
    depth = ffn1_norm.shape[0]
    fb = ffn1_w_in.shape[2]
    fo = ffn1_w_out.shape[1]
    din_shard = w_in.shape[2]
    cols = _Cols(d)
    scale = HEAD_DIM ** -0.5
    dev = 4 * lax.axis_index("x") + 2 * lax.axis_index("y") + lax.axis_index("c")

    groups = {"ffn1": ("ffn1_w_in", "ffn1_w_out"), "mix": ("w_in", "w_branch_fox", "w_branch_swa", "w_out"),
              "ffn2": ("ffn2_w_in", "ffn2_w_out"), "ffn1_in": ("ffn1_w_in",), "ffn1_out": ("ffn1_w_out",),
              "ffn2_in": ("ffn2_w_in",), "ffn2_out": ("ffn2_w_out",)}
    flipped = ("ffn1_w_in", "ffn2_w_in")
    for n in flipped:
        wts[n], mom1[n], mom2[n] = (jnp.swapaxes(a, 1, 2) for a in (wts[n], mom1[n], mom2[n]))
    shard = {n: wts[n].astype(BF16) for n in BIG}
    full, parts, gw = {}, {}, {}

    def keys_of(stages):
        return [(n, l) for g, l in stages if l < depth for n in groups[g]]

    def gather_rider(stages):
        return Rider([shard[n][l] for n, l in keys_of(stages)], True)

    def scatter_rider(stages):
        return Rider([gw[k] for k in keys_of(stages)], False)

    def ffn_weights(tag, l):
        return full[tag + "_w_in", l], full[tag + "_w_out", l].reshape(4, fb, d)

    def mixer_weights(l):
        wp = _reorder_w_in([full["w_in", l][j] for j in range(N_DEV)], cols)
        wbf = jnp.concatenate([full["w_branch_fox", l][j] for j in range(N_DEV)], axis=1)
        wbf = jnp.pad(wbf.reshape(8, HEAD_DIM, d), ((0, 0), (0, LANES - HEAD_DIM), (0, 0))).reshape(8 * LANES, d)
        wbs = _permute_heads(jnp.concatenate([full["w_branch_swa", l][j] for j in range(N_DEV)], axis=1), 0)
        return wp, wbf, wbs, full["w_out", l].reshape(d, d)

    full.update(zip(keys_of([("ffn1", 0)]), exchange_hbm(gather_rider([("ffn1", 0)]).srcs, True, "gather_first")))
    meta_all = gather_small(meta_tokens.reshape(1, N_META, -1), "gather_meta")
    meta_full = meta_all.transpose(1, 0, 2).reshape(N_META, d)
    tile8 = lambda g, s=1.0: jnp.tile(g.reshape(1, HEAD_DIM) * s, (1, 8))
    tile2 = lambda g: jnp.tile(g.reshape(1, HEAD_DIM), (1, 2))
    data_lanes = lambda g, s=1.0: _lane_pad(g.reshape(1, HEAD_DIM) * s)
    bias = bias_build(rel_bias_table, "swa_bias")

    first = jnp.concatenate([jnp.zeros((PAD_FRONT, d), F32), meta_full], axis=0)
    h = jnp.concatenate([first, x[0]], axis=0)
    saved, lw = [], []
    for l in range(depth):
        s, w = {"h0": h}, {}
        w["ffn1_in"], w["ffn1_out"] = ffn_weights("ffn1", l)
        stages = [("mix", l)]
        (h, s["n1"], s["a1"], s["fg1"], s["fu1"]), got = ffn_fwd(h, ffn1_norm[l:l + 1], w["ffn1_in"], w["ffn1_out"],
                                                          f"ffn1_fwd_{l}", gather_rider(stages))
        full.update(zip(keys_of(stages), got))
        s["h1"] = h
        w["wp"], w["wbf"], w["wbs"], w["wo"] = mixer_weights(l)
        s["nm"], s["proj"] = mixer_proj(h, mix_norm[l:l + 1], w["wp"], f"mixer_proj_{l}")
        s["gains"] = (data_lanes(fox_q_norm[l], scale), data_lanes(fox_k_norm[l]), tile8(swa_q_norm[l], scale),
                      tile2(swa_k_norm[l]))
        s["fbias"] = _lane_pad(forget_bias[l:l + 1])
        qf, kf, vf, kt, vt, qb, kb, vb = qk_post(s["proj"], s["gains"], s["fbias"], cols, f"qk_post_{l}")
        s.update(qf=qf, kf=kf, vf=vf, kt=kt, qb=qb, kb=kb, vb=vb)
        stages = [("ffn2", l)]
        (s["o_fox"], s["lse_fox"]), got = fox_fwd(qf, kf, vt, f"fox_fwd_{l}", gather_rider(stages))
        full.update(zip(keys_of(stages), got))
        s["o_swa"], s["lse_swa"] = swa_fwd(qb, kb, vb, bias, swa_sinks[l], f"swa_fwd_{l}")
        h = branch_out(h, s["o_fox"], s["o_swa"], s["proj"], w["wbf"], w["wbs"], w["wo"], cols, f"branch_out_{l}")
        s["h2"] = h
        w["ffn2_in"], w["ffn2_out"] = ffn_weights("ffn2", l)
        stages = [("ffn1", l + 1)]
        (h, s["n2"], s["a2"], s["fg2"], s["fu2"]), got = ffn_fwd(h, ffn2_norm[l:l + 1], w["ffn2_in"], w["ffn2_out"],
                                                          f"ffn2_fwd_{l}", gather_rider(stages))
        full.update(zip(keys_of(stages), got))
        saved.append(s)
        lw.append(w)

    dh, loss_part = loss_head(h, loss_target[0], "loss_head")

    gs = {n: [None] * depth for n in SMALL}
    dbias_total = None
    for l in reversed(range(depth)):
        w, s = lw[l], saved[l]

        def ffn_back(dh, tag, hin, norm, n_in, a, f_gate, f_up, stages):
            (dh_in, dg, du, dgn, dhs), got = ffn_bwd(dh, hin, norm, f_gate, f_up, w[tag + "_in"], w[tag + "_out"],
                                                     f"{tag}_bwd_{l}", scatter_rider(stages))
            parts.update(zip(keys_of(stages), got))
            gw[tag + "_w_out", l] = matmul_tn(a, dhs[None], f"{tag}_dwo_{l}").reshape(N_DEV, fo, d)
            stages = [(tag + "_out", l)]
            gw[tag + "_w_in", l], got = matmul_tn(dg, n_in[None], f"{tag}_dwi_{l}", x2=du,
                                                  rider=scatter_rider(stages))
            parts.update(zip(keys_of(stages), got))
            return dh_in, dgn

        dh, gs["ffn2_norm"][l] = ffn_back(dh, "ffn2", s["h2"], ffn2_norm[l:l + 1], s["n2"], s["a2"], s["fg2"],
                                          s["fu2"], [("ffn1_in", l + 1)])

        y, dtf, dts, dga, dgb, dof, dos, delta = branch_out_bwd(dh, s["o_fox"], s["o_swa"], s["proj"], w["wbf"],
                                                                w["wbs"], w["wo"], cols, f"branch_out_bwd_{l}")
        gw["w_out", l] = matmul_tn(y[None], dh[None], f"dw_out_{l}").reshape(N_DEV, d // N_DEV, d)
        to_shards = lambda a: a.reshape(512, N_DEV, d // N_DEV).transpose(1, 0, 2)
        gw["w_branch_fox", l] = to_shards(matmul_tn(s["o_fox"][None], dtf[None], f"dw_branch_fox_{l}")[0]
                                          .reshape(8, LANES, d)[:, :HEAD_DIM].reshape(512, d))
        gw["w_branch_swa", l] = to_shards(_permute_heads(
            matmul_tn(s["o_swa"][None], dts[None], f"dw_branch_swa_{l}")[0], 0, inverse=True))

        stages = [("ffn2_in", l)]
        (dkf, dvf, dqf, dc_rows), got = fox_bwd(s["qf"], s["kf"], s["vf"], s["kt"], dof, s["lse_fox"],
                                         delta[:, :8].T.reshape(8, 1, m_rows), f"fox_bwd_{l}", scatter_rider(stages))
        parts.update(zip(keys_of(stages), got))
        dc = _lane_pad(dc_rows.reshape(8, m_rows).T)
        dqb, dkb, dvb, dbias, dsink = swa_bwd(s["qb"], s["kb"], s["vb"], bias, swa_sinks[l], s["o_swa"], s["lse_swa"],
                                              dos, f"swa_bwd_{l}")
        dbias_total = dbias if dbias_total is None else dbias_total + dbias
        gs["swa_sinks"][l] = dsink[0, :8]
        dproj, ggqa, ggka, ggqb, ggkb, gfb = qk_post_bwd(s["proj"], s["gains"], s["fbias"], dqf, dkf, dvf, dqb, dkb,
                                                         dvb, dc, dga, dgb, cols, f"qk_post_bwd_{l}")
        gs["fox_q_norm"][l] = ggqa[0, :HEAD_DIM] * scale
        gs["fox_k_norm"][l] = ggka[0, :HEAD_DIM]
        gs["swa_q_norm"][l] = ggqb.reshape(8, HEAD_DIM).sum(0) * scale
        gs["swa_k_norm"][l] = ggkb.reshape(2, HEAD_DIM).sum(0)
        gs["forget_bias"][l] = gfb[0, :8]
        dwp = matmul_tn(s["nm"][None], dproj[None], f"dw_in_{l}", tn=1024 if cols.np % 1024 == 0 else None)[0]
        gw["w_in", l] = _restore_w_in(dwp, cols, din_shard)
        dh, gs["mix_norm"][l] = dproj_bwd(dh, s["h1"], mix_norm[l:l + 1], dproj, w["wp"], f"dproj_bwd_{l}")

        dh, gs["ffn1_norm"][l] = ffn_back(dh, "ffn1", s["h0"], ffn1_norm[l:l + 1], s["n1"], s["a1"], s["fg1"],
                                          s["fu1"], [("mix", l)])

    grad_x = dh[LANES:][None]
    dmeta = dh[PAD_FRONT:LANES]
    dtable = bias_reduce(dbias_total, "swa_dbias")[:, :8]

    last = ("ffn1_w_in", 0)
    send_sems, recv_sems, src_thru, land_thru, token = scatter_start(gw[last], "scatter_last_start")
    big_out = [{}, {}, {}, {}]
    for n in BIG:
        if n != last[0]:
            outs = adamw_sum([parts[n, l] for l in range(depth)], wts[n], mom1[n], mom2[n], f"adamw_{n}", after=token)
            for k in range(4):
                big_out[k][n] = outs[k]
    sent, landed = scatter_wait(send_sems, recv_sems, src_thru, land_thru,
                                [big_out[1][n] for n in BIG if n != last[0]], "scatter_last_wait")
    parts[last] = lax.dynamic_update_slice_in_dim(landed, lax.dynamic_slice_in_dim(sent, dev, 1, axis=0), dev, axis=0)
    outs = adamw_sum([parts[last[0], l] for l in range(depth)], wts[last[0]], mom1[last[0]], mom2[last[0]],
                     f"adamw_{last[0]}")
    for k in range(4):
        big_out[k][last[0]] = outs[k]

    small_g = {n: (jnp.stack(gs[n]) if n != "rel_bias_table" else None) for n in SMALL}
    small_g["rel_bias_table"] = dtable
    pieces = [loss_part[0:1, 0:1].reshape(1, 1)] + [small_g[n].reshape(1, -1) for n in SMALL] + [dmeta.reshape(1, -1)]
    small_shapes = [(1,)] + [wts[n].shape for n in SMALL] + [(N_META, d)]
    total = allsum_small(_pack(pieces, LANES, 8, F32), "allsum_small")
    summed = _unpack(total, small_shapes)
    loss = summed[0][0]
    g_small = dict(zip(SMALL, summed[1:1 + len(SMALL)]))
    g_meta = lax.dynamic_slice_in_dim(summed[-1], dev * (d // N_DEV), d // N_DEV, axis=1)
    names = SMALL + ("meta_tokens",)
    g_small["meta_tokens"] = g_meta
    pk = lambda src: _pack([src[n].reshape(1, -1) for n in names], LANES, 8, F32)[0]
    small_out = [dict(zip(names, _unpack(o, [wts[n].shape for n in names])))
                 for o in adamw_small(pk(g_small), pk(wts), pk(mom1), pk(mom2), "adamw_small")]

    for out in big_out:
        for n in flipped:
            out[n] = jnp.swapaxes(out[n], 1, 2)
    grads = {**big_out[0], **g_small}
    delta = {**big_out[1], **small_out[0]}
    new_m = {**big_out[2], **small_out[1]}
    new_v = {**big_out[3], **small_out[2]}
    return (loss, grad_x, *[grads[n] for n in WEIGHTS], *[delta[n] for n in WEIGHTS],
            *[new_m[n] for n in WEIGHTS], *[new_v[n] for n in WEIGHTS])
```

```python
import math

import numpy as np
import jax
import jax.numpy as jnp
from jax import lax
from jax.experimental import pallas as pl
from jax.experimental.pallas import tpu as pltpu

F32 = jnp.float32
BF16 = jnp.bfloat16
EPS = 1e-6
NEG = -1e30
HEAD_DIM = 64
LANES = 128
N_META = 16
PAD_FRONT = LANES - N_META
N_BUCKETS = 32
MAX_DISTANCE = 128
N_DEV = 8
ADAM_LR, ADAM_B1, ADAM_B2, ADAM_EPS, ADAM_WD, ADAM_STEP = 0.001, 0.9, 0.999, 1e-08, 0.01, 10
VMEM_LIMIT = 56 * 1024 * 1024
MESH = pl.DeviceIdType.MESH


def _params(n_grid):
    return pltpu.CompilerParams(dimension_semantics=("arbitrary",) * n_grid,
                                vmem_limit_bytes=VMEM_LIMIT)


def _dot(a, b):
    return jnp.dot(a, b, preferred_element_type=F32)


def _dot_nt(a, b):
    return lax.dot_general(a, b, (((1,), (1,)), ((), ())), preferred_element_type=F32)


def _dot_tn(a, b):
    return lax.dot_general(a, b, (((0,), (0,)), ((), ())), preferred_element_type=F32)


def _rms(x):
    r = lax.rsqrt(jnp.mean(x * x, axis=-1, keepdims=True) + EPS)
    return x * r, r


def _rms_bwd(x, g, dn):
    xh, r = _rms(x)
    dxh = dn * g
    dx = r * (dxh - xh * jnp.mean(dxh * xh, axis=-1, keepdims=True))
    return dx, jnp.sum(dn * xh, axis=0, keepdims=True)


def _split2(v):
    hi = v.astype(BF16)
    return hi, (v - hi.astype(F32)).astype(BF16)


def _split3(v):
    hi = v.astype(BF16)
    r1 = v - hi.astype(F32)
    mid = r1.astype(BF16)
    return hi, mid, (r1 - mid.astype(F32)).astype(BF16)


def _group_ones():
    r = lax.broadcasted_iota(jnp.int32, (LANES, LANES), 0) // HEAD_DIM
    c = lax.broadcasted_iota(jnp.int32, (LANES, LANES), 1) // HEAD_DIM
    return jnp.where(r == c, 1.0, 0.0).astype(BF16)


def _group_mean(v, ones):
    hi, lo = _split2(v)
    return (_dot(hi, ones) + _dot(lo, ones)) * (1.0 / HEAD_DIM)


def _row_tile(m):
    return 384 if m % 384 == 0 else LANES


def _tile(m, cap):
    return max(t for t in range(16, cap + 1, 16) if m % t == 0)


def _peer(k):
    x, y, c = lax.axis_index("x"), lax.axis_index("y"), lax.axis_index("c")
    px = 1 - x if k & 4 else x
    py = 1 - y if k & 2 else y
    pc = 1 - c if k & 1 else c
    return (px, py, pc), 4 * px + 2 * py + pc


def _exchange_body(src_ref, dst_ref, send_sems, recv_sems, local_sem, bcast):
    x, y, c = lax.axis_index("x"), lax.axis_index("y"), lax.axis_index("c")
    me = 4 * x + 2 * y + c
    mine = pltpu.make_async_copy(src_ref.at[0 if bcast else me], dst_ref.at[me], local_sem)
    mine.start()
    sends = []
    for k in range(1, N_DEV):
        dev, idx = _peer(k)
        cp = pltpu.make_async_remote_copy(
            src_ref=src_ref.at[0 if bcast else idx], dst_ref=dst_ref.at[me],
            send_sem=send_sems.at[k - 1], recv_sem=recv_sems.at[k - 1],
            device_id=dev, device_id_type=MESH)
        cp.start()
        sends.append(cp)
    for k in range(1, N_DEV):
        dev, idx = _peer(k)
        pltpu.make_async_remote_copy(
            src_ref=src_ref.at[0], dst_ref=dst_ref.at[idx],
            send_sem=send_sems.at[k - 1], recv_sem=recv_sems.at[k - 1],
            device_id=dev, device_id_type=MESH).wait_recv()
    for cp in sends:
        cp.wait_send()
    mine.wait()


class Rider:
    FIRST = (1, 2, 4, 6)
    RELAYED = (2, 4, 6)

    def __init__(self, srcs=(), bcast=True):
        self.srcs, self.bcast, self.n = list(srcs), bcast, len(srcs)

    def out_shapes(self):
        return [jax.ShapeDtypeStruct(((N_DEV,) + s.shape) if self.bcast else s.shape, s.dtype) for s in self.srcs]

    def specs(self):
        return [pl.BlockSpec(memory_space=pl.ANY)] * self.n

    def scratch(self):
        if not self.n:
            return []
        return [pltpu.SemaphoreType.DMA((self.n * (N_DEV - 1),)), pltpu.SemaphoreType.DMA((self.n * (N_DEV - 1),)),
                pltpu.SemaphoreType.DMA((self.n,))]

    @staticmethod
    def _copy(src, dst, a, pair, dev, send_sems, recv_sems):
        sem = a * (N_DEV - 1) + pair - 1
        return pltpu.make_async_remote_copy(src_ref=src, dst_ref=dst, send_sem=send_sems.at[sem],
                                            recv_sem=recv_sems.at[sem], device_id=dev, device_id_type=MESH)

    def _first(self):
        return self.FIRST if self.bcast else range(1, N_DEV)

    def _own(self, s, d, a, local_sems):
        me = 4 * lax.axis_index("x") + 2 * lax.axis_index("y") + lax.axis_index("c")
        return pltpu.make_async_copy(s if self.bcast else s.at[me], d.at[me], local_sems.at[a]), me

    def start(self, src_refs, dst_refs, send_sems, recv_sems, local_sems):
        for a, (s, d) in enumerate(zip(src_refs, dst_refs)):
            own, me = self._own(s, d, a, local_sems)
            own.start()
            for k in self._first():
                dev, idx = _peer(k)
                self._copy(s if self.bcast else s.at[idx], d.at[me], a, k, dev, send_sems, recv_sems).start()

    def relay(self, src_refs, dst_refs, send_sems, recv_sems, local_sems):
        if not self.bcast:
            return
        sibling, _ = _peer(1)
        for a, d in enumerate(dst_refs):
            for k in self.RELAYED:
                dev, idx = _peer(k)
                self._copy(d.at[idx], d.at[idx], a, k, dev, send_sems, recv_sems).wait_recv()
                self._copy(d.at[idx], d.at[idx], a, k + 1, sibling, send_sems, recv_sems).start()

    def wait(self, src_refs, dst_refs, send_sems, recv_sems, local_sems):
        sibling, _ = _peer(1)
        for a, (s, d) in enumerate(zip(src_refs, dst_refs)):
            own, me = self._own(s, d, a, local_sems)
            for k in range(1, N_DEV):
                if not (self.bcast and k in self.RELAYED):
                    dev, idx = _peer(k)
                    self._copy(d.at[idx], d.at[idx], a, k, dev, send_sems, recv_sems).wait_recv()
            for k in self._first():
                dev, idx = _peer(k)
                self._copy(s if self.bcast else s.at[idx], d.at[me], a, k, dev, send_sems, recv_sems).wait_send()
            if self.bcast:
                for k in self.RELAYED:
                    dev, idx = _peer(k)
                    self._copy(d.at[idx], d.at[idx], a, k + 1, sibling, send_sems, recv_sems).wait_send()
            own.wait()


def rider_call(core, name, grid, in_specs, out_specs, out_shape, scratch_shapes, args, rider=None):
    rider = rider or Rider()
    n_in, n_out, n_scr, nr = len(in_specs), len(out_specs), len(scratch_shapes), rider.n

    def body(*refs):
        ins, r_src = refs[:n_in], refs[n_in:n_in + nr]
        outs = refs[n_in + nr:n_in + nr + n_out]
        r_dst = refs[n_in + nr + n_out:n_in + 2 * nr + n_out]
        scr = refs[n_in + 2 * nr + n_out:n_in + 2 * nr + n_out + n_scr]
        sems = refs[n_in + 2 * nr + n_out + n_scr:]
        if nr:
            first, relay, last = True, True, True
            for ax, size in enumerate(grid):
                first = first & (pl.program_id(ax) == 0)
                relay = relay & (pl.program_id(ax) == (3 * size // 4 if ax == 0 else 0))
                last = last & (pl.program_id(ax) == size - 1)
            if not grid:
                rider.start(r_src, r_dst, *sems)
                rider.relay(r_src, r_dst, *sems)
            else:
                pl.when(first)(lambda: rider.start(r_src, r_dst, *sems))
                if rider.bcast:
                    pl.when(relay)(lambda: rider.relay(r_src, r_dst, *sems))
        core(*ins, *outs, *scr)
        if nr:
            if not grid:
                rider.wait(r_src, r_dst, *sems)
            else:
                pl.when(last)(lambda: rider.wait(r_src, r_dst, *sems))

    res = pl.pallas_call(
        body, name=name, grid=grid,
        in_specs=list(in_specs) + rider.specs(),
        out_specs=list(out_specs) + rider.specs(),
        out_shape=list(out_shape) + rider.out_shapes(),
        scratch_shapes=list(scratch_shapes) + rider.scratch(),
        compiler_params=_params(len(grid)),
    )(*args, *rider.srcs)
    return res[:n_out], res[n_out:]


def exchange_hbm(srcs, bcast, name):
    return rider_call(lambda: None, name, (), [], [], [], [], [], Rider(srcs, bcast))[1]


_HBM = pl.BlockSpec(memory_space=pltpu.HBM)
_SEM = pl.BlockSpec(memory_space=pltpu.SEMAPHORE)
_EFFECT = pltpu.CompilerParams(has_side_effects=pltpu.SideEffectType.DATAFLOW_SIDE_EFFECTING)


def scatter_start(src, name):
    def body(src_ref, land_ref, send_sems, recv_sems, src_thru, land_thru, token):
        me = 4 * lax.axis_index("x") + 2 * lax.axis_index("y") + lax.axis_index("c")
        for k in range(1, N_DEV):
            dev, idx = _peer(k)
            pltpu.make_async_remote_copy(src_ref=src_ref.at[idx], dst_ref=land_ref.at[me], send_sem=send_sems.at[k - 1],
                                         recv_sem=recv_sems.at[k - 1], device_id=dev, device_id_type=MESH).start()
        token[...] = jnp.zeros_like(token)

    return pl.pallas_call(
        body, name=name,
        out_shape=(pltpu.SemaphoreType.DMA((N_DEV - 1,)), pltpu.SemaphoreType.DMA((N_DEV - 1,)),
                   pltpu.HBM(src.shape, src.dtype), pltpu.HBM(src.shape, src.dtype), jax.ShapeDtypeStruct((8, LANES), F32)),
        in_specs=(_HBM, _HBM), out_specs=(_SEM, _SEM, _HBM, _HBM, pl.BlockSpec(memory_space=pltpu.VMEM)),
        input_output_aliases={0: 2, 1: 3}, compiler_params=_EFFECT,
    )(pltpu.with_memory_space_constraint(src, pltpu.HBM),
      pltpu.with_memory_space_constraint(lax.empty(src.shape, src.dtype), pltpu.HBM))


def scatter_wait(send_sems, recv_sems, src_thru, land_thru, after, name):
    n_after = len(after)

    def body(*refs):
        src_ref, land_ref, send_sems, recv_sems = refs[:4]
        for k in range(1, N_DEV):
            dev, idx = _peer(k)
            copy = pltpu.make_async_remote_copy(src_ref=src_ref.at[idx], dst_ref=land_ref.at[idx],
                                                send_sem=send_sems.at[k - 1], recv_sem=recv_sems.at[k - 1],
                                                device_id=dev, device_id_type=MESH)
            copy.wait_send()
            copy.wait_recv()

    return pl.pallas_call(
        body, name=name,
        out_shape=(pltpu.HBM(src_thru.shape, src_thru.dtype), pltpu.HBM(land_thru.shape, land_thru.dtype)),
        in_specs=(_HBM, _HBM, _SEM, _SEM) + (pl.BlockSpec(memory_space=pl.ANY),) * n_after, out_specs=(_HBM, _HBM),
        input_output_aliases={0: 0, 1: 1}, compiler_params=_EFFECT,
    )(src_thru, land_thru, send_sems, recv_sems, *after)


def allsum_small(vec, name):
    def body(src_ref, out_ref, dst_ref, send_sems, recv_sems, local_sem):
        _exchange_body(src_ref, dst_ref, send_sems, recv_sems, local_sem, True)
        acc = dst_ref[0]
        for j in range(1, N_DEV):
            acc = acc + dst_ref[j]
        out_ref[...] = acc

    return pl.pallas_call(
        body, name=name,
        out_shape=jax.ShapeDtypeStruct(vec.shape[1:], F32),
        in_specs=[pl.BlockSpec(memory_space=pltpu.VMEM)],
        out_specs=pl.BlockSpec(memory_space=pltpu.VMEM),
        scratch_shapes=[pltpu.VMEM((N_DEV,) + vec.shape[1:], F32),
                        pltpu.SemaphoreType.DMA((N_DEV - 1,)), pltpu.SemaphoreType.DMA((N_DEV - 1,)),
                        pltpu.SemaphoreType.DMA],
    )(vec)


def gather_small(vec, name):
    def body(src_ref, dst_ref, send_sems, recv_sems, local_sem):
        _exchange_body(src_ref, dst_ref, send_sems, recv_sems, local_sem, True)

    return pl.pallas_call(
        body, name=name,
        out_shape=jax.ShapeDtypeStruct((N_DEV,) + vec.shape[1:], F32),
        in_specs=[pl.BlockSpec(memory_space=pltpu.VMEM)],
        out_specs=pl.BlockSpec(memory_space=pltpu.VMEM),
        scratch_shapes=[pltpu.SemaphoreType.DMA((N_DEV - 1,)), pltpu.SemaphoreType.DMA((N_DEV - 1,)),
                        pltpu.SemaphoreType.DMA],
    )(vec)


FFN_FWD_ROWS = 1056
FFN_BWD_ROWS = 704
DW_ROWS = 1408

def ffn_fwd(h, g, w_in8, w_out4, name, rider=None):
    m, d = h.shape
    fb = w_in8.shape[1]
    tm = _tile(m, FFN_FWD_ROWS)

    def body(h_ref, g_ref, wg_ref, wu_ref, wo_ref, hn_ref, n_ref, a_ref, fg_ref, fu_ref, acc_ref):
        i = pl.program_id(1)

        @pl.when(i == 0)
        def _():
            xh, _ = _rms(h_ref[...])
            n_ref[...] = (xh * g_ref[...]).astype(BF16)
            acc_ref[...] = jnp.zeros_like(acc_ref)

        n = n_ref[...]
        gate = _dot_nt(n, wg_ref[0])
        up = _dot_nt(n, wu_ref[0])
        sg = jax.nn.sigmoid(gate)
        silu = gate * sg
        a = (silu * up).astype(BF16)
        a_ref[0] = a
        fg_ref[0] = (up * (sg * (1.0 + gate * (1.0 - sg)))).astype(BF16)
        fu_ref[0] = silu.astype(BF16)
        acc_ref[...] += _dot(a, wo_ref[0])

        @pl.when(i == 3)
        def _():
            hn_ref[...] = h_ref[...] + 0.5 * acc_ref[...]

    return rider_call(
        body, name, (m // tm, 4),
        in_specs=[pl.BlockSpec((tm, d), lambda r, i: (r, 0)),
                  pl.BlockSpec((1, d), lambda r, i: (0, 0)),
                  pl.BlockSpec((1, fb, d), lambda r, i: (i, 0, 0)),
                  pl.BlockSpec((1, fb, d), lambda r, i: (i + 4, 0, 0)),
                  pl.BlockSpec((1, fb, d), lambda r, i: (i, 0, 0))],
        out_specs=[pl.BlockSpec((tm, d), lambda r, i: (r, 0)),
                   pl.BlockSpec((tm, d), lambda r, i: (r, 0))] + [pl.BlockSpec((1, tm, fb), lambda r, i: (i, r, 0))] * 3,
        out_shape=[jax.ShapeDtypeStruct((m, d), F32), jax.ShapeDtypeStruct((m, d), BF16)]
                  + [jax.ShapeDtypeStruct((4, m, fb), BF16)] * 3,
        scratch_shapes=[pltpu.VMEM((tm, d), F32)],
        args=(h, g, w_in8, w_in8, w_out4), rider=rider)


def ffn_bwd(dh, h, g, f_gate, f_up, w_in8, w_out4, name, rider=None):
    m, d = h.shape
    fb = w_in8.shape[1]
    tm = _tile(m, FFN_BWD_ROWS)

    def body(dh_ref, h_ref, g_ref, fg_ref, fu_ref, wg_ref, wu_ref, wo_ref,
             dhin_ref, dg_ref, du_ref, dgn_ref, dhs_ref, acc_ref):
        r = pl.program_id(0)
        i = pl.program_id(1)

        @pl.when(i == 0)
        def _():
            dhs_ref[...] = (0.5 * dh_ref[...]).astype(BF16)
            acc_ref[...] = jnp.zeros_like(acc_ref)

        @pl.when((r == 0) & (i == 0))
        def _():
            dgn_ref[...] = jnp.zeros_like(dgn_ref)

        da = _dot_nt(dhs_ref[...], wo_ref[0])
        dub = (da * fu_ref[0]).astype(BF16)
        dgb = (da * fg_ref[0]).astype(BF16)
        dg_ref[0] = dgb
        du_ref[0] = dub
        acc_ref[...] += _dot(dgb, wg_ref[0]) + _dot(dub, wu_ref[0])

        @pl.when(i == 3)
        def _():
            dx, dgain = _rms_bwd(h_ref[...], g_ref[...], acc_ref[...])
            dgn_ref[...] += dgain
            dhin_ref[...] = dh_ref[...] + dx

    row = lambda r, i: (r, 0)
    blk = lambda r, i: (i, r, 0)
    return rider_call(
        body, name, (m // tm, 4),
        in_specs=[pl.BlockSpec((tm, d), row), pl.BlockSpec((tm, d), row),
                  pl.BlockSpec((1, d), lambda r, i: (0, 0)),
                  pl.BlockSpec((1, tm, fb), blk), pl.BlockSpec((1, tm, fb), blk),
                  pl.BlockSpec((1, fb, d), lambda r, i: (i, 0, 0)),
                  pl.BlockSpec((1, fb, d), lambda r, i: (i + 4, 0, 0)),
                  pl.BlockSpec((1, fb, d), lambda r, i: (i, 0, 0))],
        out_specs=[pl.BlockSpec((tm, d), row),
                   pl.BlockSpec((1, tm, fb), blk), pl.BlockSpec((1, tm, fb), blk),
                   pl.BlockSpec((1, d), lambda r, i: (0, 0)),
                   pl.BlockSpec((tm, d), row)],
        out_shape=[jax.ShapeDtypeStruct((m, d), F32),
                   jax.ShapeDtypeStruct((4, m, fb), BF16), jax.ShapeDtypeStruct((4, m, fb), BF16),
                   jax.ShapeDtypeStruct((1, d), F32), jax.ShapeDtypeStruct((m, d), BF16)],
        scratch_shapes=[pltpu.VMEM((tm, d), F32)],
        args=(dh, h, g, f_gate, f_up, w_in8, w_in8, w_out4), rider=rider)


def matmul_tn(x, y, name, tn=None, tk=None, x2=None, rider=None):
    if tk is not None:
        assert x.shape[0] == y.shape[0] == 1 and x2 is None and rider is None
        bk, m, kf = x.shape[2] // tk, x.shape[1], x.shape[2]
        tm = _tile(m, DW_ROWS)
        tn_ = y.shape[2] if tn is None else tn

        def tiled(x_ref, y_ref, o_ref, acc_ref):
            r = pl.program_id(2)

            @pl.when(r == 0)
            def _():
                acc_ref[...] = jnp.zeros_like(acc_ref)

            acc_ref[...] += _dot_tn(x_ref[0].astype(BF16), y_ref[0].astype(BF16))

            @pl.when(r == m // tm - 1)
            def _():
                o_ref[0] = acc_ref[...].astype(BF16)

        return pl.pallas_call(
            tiled, name=name, grid=(bk, y.shape[2] // tn_, m // tm),
            in_specs=[pl.BlockSpec((1, tm, tk), lambda i, j, r: (0, r, i)),
                      pl.BlockSpec((1, tm, tn_), lambda i, j, r: (0, r, j))],
            out_specs=pl.BlockSpec((1, tk, tn_), lambda i, j, r: (0, i, j)),
            out_shape=jax.ShapeDtypeStruct((1, kf, y.shape[2]), BF16),
            scratch_shapes=[pltpu.VMEM((tk, tn_), F32)],
            compiler_params=_params(3),
        )(x, y)
    bx, m, k = x.shape
    by, _, n = y.shape
    b = max(bx, by) * (2 if x2 is not None else 1)
    tm = _tile(m, DW_ROWS)
    tn = n if tn is None else tn
    nt = n // tn
    nr = m // tm

    def body(*refs):
        x_ref, y_ref = refs[0], refs[-3]
        o_ref, acc_ref = refs[-2], refs[-1]
        r = pl.program_id(2)

        @pl.when(r == 0)
        def _():
            acc_ref[...] = jnp.zeros_like(acc_ref)

        if x2 is None:
            acc_ref[...] += _dot_tn(x_ref[0].astype(BF16), y_ref[0].astype(BF16))
        else:
            @pl.when(pl.program_id(0) < bx)
            def _():
                acc_ref[...] += _dot_tn(x_ref[0].astype(BF16), y_ref[0].astype(BF16))

            @pl.when(pl.program_id(0) >= bx)
            def _():
                acc_ref[...] += _dot_tn(refs[1][0].astype(BF16), y_ref[0].astype(BF16))

        @pl.when(r == nr - 1)
        def _():
            o_ref[0] = acc_ref[...].astype(BF16)

    if x2 is None:
        x_specs = [pl.BlockSpec((1, tm, k), (lambda i, j, r: (i, r, 0)) if bx > 1 else (lambda i, j, r: (0, r, 0)))]
    else:
        x_specs = [pl.BlockSpec((1, tm, k), lambda i, j, r: (jnp.minimum(i, bx - 1), jnp.where(i < bx, r, nr - 1), 0)),
                   pl.BlockSpec((1, tm, k), lambda i, j, r: (jnp.maximum(i - bx, 0), jnp.where(i < bx, 0, r), 0))]
    y_map = (lambda i, j, r: (i, r, j)) if by > 1 else (lambda i, j, r: (0, r, j))
    (out,), carried = rider_call(
        body, name, (b, nt, nr),
        in_specs=x_specs + [pl.BlockSpec((1, tm, tn), y_map)],
        out_specs=[pl.BlockSpec((1, k, tn), lambda i, j, r: (i, 0, j))],
        out_shape=[jax.ShapeDtypeStruct((b, k, n), BF16)],
        scratch_shapes=[pltpu.VMEM((k, tn), F32)],
        args=[x] + ([x2] if x2 is not None else []) + [y], rider=rider)
    return (out, carried) if rider is not None else out


AUG = HEAD_DIM


class _Cols:
    def __init__(self, d):
        self.d = d
        self.ga, self.gb = 0, d
        self.qa, self.ka, self.va = 2 * d, 2 * d + 1024, 2 * d + 2048
        self.qb = 2 * d + 3072
        self.kb, self.vb, self.fa = self.qb + 512, self.qb + 640, self.qb + 768
        self.np = self.qb + 1024


def mixer_proj(h, g, wp, name):
    m, d = h.shape
    npad = wp.shape[0]
    tm = _row_tile(m)

    def body(h_ref, g_ref, w_ref, n_ref, p_ref):
        xh, _ = _rms(h_ref[...])
        n = (xh * g_ref[...]).astype(BF16)
        n_ref[...] = n
        p_ref[...] = _dot_nt(n, w_ref[...])

    return pl.pallas_call(
        body, name=name, grid=(m // tm,),
        in_specs=[pl.BlockSpec((tm, d), lambda r: (r, 0)), pl.BlockSpec((1, d), lambda r: (0, 0)),
                  pl.BlockSpec((npad, d), lambda r: (0, 0))],
        out_specs=[pl.BlockSpec((tm, d), lambda r: (r, 0)), pl.BlockSpec((tm, npad), lambda r: (r, 0))],
        out_shape=[jax.ShapeDtypeStruct((m, d), BF16), jax.ShapeDtypeStruct((m, npad), F32)],
        compiler_params=_params(1),
    )(h, g, wp)


def _head_norm(x, gain, ones):
    outs = []
    for b in range(x.shape[1] // LANES):
        xb = x[:, b * LANES:(b + 1) * LANES]
        r = lax.rsqrt(_group_mean(xb * xb, ones) + EPS)
        outs.append(xb * r * gain[:, b * LANES:(b + 1) * LANES])
    return outs


def _head_norm_bwd(x, gain, dn, ones):
    dxs, dgs = [], []
    for b in range(x.shape[1] // LANES):
        sl = slice(b * LANES, (b + 1) * LANES)
        xb, dnb = x[:, sl], dn[:, sl]
        r = lax.rsqrt(_group_mean(xb * xb, ones) + EPS)
        xh = xb * r
        dxh = dnb * gain[:, sl]
        dxs.append(r * (dxh - xh * _group_mean(dxh * xh, ones)))
        dgs.append(jnp.sum(dnb * xh, axis=0, keepdims=True))
    return dxs, dgs


def _lane_col(v, lane_iota, idx):
    return jnp.sum(jnp.where(lane_iota == idx, v, 0.0), axis=1, keepdims=True)


def _aug(base, lane, vals):
    for i, v in enumerate(vals):
        base = jnp.where(lane == AUG + i, v, base)
    return base


def qk_post(proj, gains, fbias, cols, name):
    m = proj.shape[0]
    tm = _row_tile(m)
    gqa, gka, gqb, gkb = gains

    def body(qa_ref, ka_ref, va_ref, qb_ref, kb_ref, vb_ref, fa_ref, gqa_ref, gka_ref, gqb_ref, gkb_ref, fb_ref,
             qf_o, kf_o, vf_o, kt_o, vt_o, qb_o, kb_o, vb_o, carry_ref):
        r0 = pl.program_id(0)

        @pl.when(r0 == 0)
        def _():
            carry_ref[...] = jnp.zeros_like(carry_ref)

        z = fa_ref[...] + fb_ref[...]
        logf = jnp.minimum(z, 0.0) - jnp.log(1.0 + jnp.exp(-jnp.abs(z)))
        rr = lax.broadcasted_iota(jnp.int32, (tm, tm), 0)
        cc = lax.broadcasted_iota(jnp.int32, (tm, tm), 1)
        tril = jnp.where(cc <= rr, 1.0, 0.0).astype(BF16)
        p0, p1, p2 = _split3(logf)
        c = _dot(tril, p0) + _dot(tril, p1) + _dot(tril, p2) + carry_ref[...]
        carry_ref[...] += jnp.sum(logf, axis=0, keepdims=True)

        lane = lax.broadcasted_iota(jnp.int32, (tm, LANES), 1)
        is_pad = (r0 * tm + lax.broadcasted_iota(jnp.int32, (tm, 1), 0)) < PAD_FRONT
        ones = jnp.ones((LANES, LANES), BF16)
        for hd in range(8):
            sl = slice(hd * LANES, (hd + 1) * LANES)
            ch = _lane_col(c, lane, hd)
            ct = [p.astype(F32) for p in _split3(ch)]
            cs = [p.astype(F32) for p in _split3(-jnp.where(is_pad, -NEG, ch))]
            xq = qa_ref[:, sl]
            qn = xq * lax.rsqrt(_group_mean(xq * xq, ones) + EPS) * gqa_ref[...]
            qf_o[:, sl] = _aug(qn, lane, ct + [1.0, 1.0, 1.0]).astype(BF16)
            xk = ka_ref[:, sl]
            kn = xk * lax.rsqrt(_group_mean(xk * xk, ones) + EPS) * gka_ref[...]
            kf = _aug(kn, lane, [1.0, 1.0, 1.0] + cs)
            vf = _aug(va_ref[:, sl], lane, [1.0, 1.0, 1.0])
            kf_o[:, sl] = kf.astype(BF16)
            vf_o[:, sl] = vf.astype(BF16)
            kt_o[sl, :] = kf.T.astype(BF16)
            vt_o[sl, :] = vf.T.astype(BF16)

        gones = _group_ones()
        for src, gn, dst in ((qb_ref, gqb_ref, qb_o), (kb_ref, gkb_ref, kb_o)):
            for b, blk in enumerate(_head_norm(src[...], gn[...], gones)):
                dst[:, b * LANES:(b + 1) * LANES] = blk.astype(BF16)
        vb_o[...] = vb_ref[...].astype(BF16)

    w1024 = lambda off: pl.BlockSpec((tm, 1024), lambda r, o=off // 1024: (r, o))
    w512 = lambda off: pl.BlockSpec((tm, 512), lambda r, o=off // 512: (r, o))
    w128 = lambda off: pl.BlockSpec((tm, LANES), lambda r, o=off // LANES: (r, o))
    vec = lambda w: pl.BlockSpec((1, w), lambda r: (0, 0))
    row = lambda w: pl.BlockSpec((tm, w), lambda r: (r, 0))
    return pl.pallas_call(
        body, name=name, grid=(m // tm,),
        in_specs=[w1024(cols.qa), w1024(cols.ka), w1024(cols.va), w512(cols.qb), w128(cols.kb), w128(cols.vb),
                  w128(cols.fa), vec(LANES), vec(LANES), vec(512), vec(LANES), vec(LANES)],
        out_specs=[row(1024), row(1024), row(1024)] + [pl.BlockSpec((1024, tm), lambda r: (0, r))] * 2
                  + [row(512), row(LANES), row(LANES)],
        out_shape=[jax.ShapeDtypeStruct((m, 1024), BF16)] * 3 + [jax.ShapeDtypeStruct((1024, m), BF16)] * 2
                  + [jax.ShapeDtypeStruct((m, 512), BF16)] + [jax.ShapeDtypeStruct((m, LANES), BF16)] * 2,
        scratch_shapes=[pltpu.VMEM((1, LANES), F32)],
        compiler_params=_params(1),
    )(proj, proj, proj, proj, proj, proj, proj, gqa, gka, gqb, gkb, fbias)


def qk_post_bwd(proj, gains, fbias, dqf, dkf, dvf, dqb, dkb, dvb, dc, dga, dgb, cols, name):
    m = proj.shape[0]
    d = cols.d
    tm = _row_tile(m)
    nt = m // tm
    gqa, gka, gqb, gkb = gains

    def body(qa_ref, ka_ref, qb_ref, kb_ref, fa_ref, gqa_ref, gka_ref, gqb_ref, gkb_ref, fb_ref,
             dqf_ref, dkf_ref, dvf_ref, dqb_ref, dkb_ref, dvb_ref, dc_ref, dga_ref, dgb_ref,
             dp_o, ggqa_o, ggka_o, ggqb_o, ggkb_o, gfb_o, carry_ref):
        @pl.when(pl.program_id(0) == 0)
        def _():
            carry_ref[...] = jnp.zeros_like(carry_ref)
            for o in (ggqa_o, ggka_o, ggqb_o, ggkb_o, gfb_o):
                o[...] = jnp.zeros_like(o)

        dp_o[:, cols.ga:cols.ga + d] = dga_ref[...].astype(BF16)
        dp_o[:, cols.gb:cols.gb + d] = dgb_ref[...].astype(BF16)
        dp_o[:, cols.fa + LANES:cols.np] = jnp.zeros((tm, cols.np - cols.fa - LANES), BF16)
        lane = lax.broadcasted_iota(jnp.int32, (tm, LANES), 1)
        data = lane < HEAD_DIM
        ones = jnp.ones((LANES, LANES), BF16)
        for hd in range(8):
            sl = slice(hd * LANES, (hd + 1) * LANES)
            for src, gn, dn_ref, off, gout in ((qa_ref, gqa_ref, dqf_ref, cols.qa, ggqa_o),
                                               (ka_ref, gka_ref, dkf_ref, cols.ka, ggka_o)):
                x = src[:, sl]
                dn = jnp.where(data, dn_ref[:, sl], 0.0)
                r = lax.rsqrt(_group_mean(x * x, ones) + EPS)
                xh = x * r
                dxh = dn * gn[...]
                dp_o[:, off + hd * LANES:off + (hd + 1) * LANES] = (
                    r * (dxh - xh * _group_mean(dxh * xh, ones))).astype(BF16)
                gout[...] += jnp.sum(dn * xh, axis=0, keepdims=True)
            dp_o[:, cols.va + hd * LANES:cols.va + (hd + 1) * LANES] = jnp.where(data, dvf_ref[:, sl], 0.0).astype(BF16)
        dp_o[:, cols.vb:cols.vb + LANES] = dvb_ref[...].astype(BF16)
        gones = _group_ones()
        for src, gn, dn, off, gout in ((qb_ref, gqb_ref, dqb_ref, cols.qb, ggqb_o),
                                       (kb_ref, gkb_ref, dkb_ref, cols.kb, ggkb_o)):
            dxs, dgs = _head_norm_bwd(src[...], gn[...], dn[...], gones)
            for b, (dx, dg) in enumerate(zip(dxs, dgs)):
                dp_o[:, off + b * LANES:off + (b + 1) * LANES] = dx.astype(BF16)
                gout[:, b * LANES:(b + 1) * LANES] += dg
        dcv = dc_ref[...]
        rr = lax.broadcasted_iota(jnp.int32, (tm, tm), 0)
        cc = lax.broadcasted_iota(jnp.int32, (tm, tm), 1)
        triu = jnp.where(cc >= rr, 1.0, 0.0).astype(BF16)
        p0, p1, p2 = _split3(dcv)
        dlogf = _dot(triu, p0) + _dot(triu, p1) + _dot(triu, p2) + carry_ref[...]
        carry_ref[...] += jnp.sum(dcv, axis=0, keepdims=True)
        z = fa_ref[...] + fb_ref[...]
        row = (nt - 1 - pl.program_id(0)) * tm + lax.broadcasted_iota(jnp.int32, (tm, LANES), 0)
        dfa = jnp.where(row >= PAD_FRONT, dlogf * jax.nn.sigmoid(-z), 0.0)
        dp_o[:, cols.fa:cols.fa + LANES] = dfa.astype(BF16)
        gfb_o[...] += jnp.sum(dfa, axis=0, keepdims=True)

    rev = lambda r: nt - 1 - r
    w1024 = lambda off: pl.BlockSpec((tm, 1024), lambda r, o=off // 1024: (rev(r), o))
    w512 = lambda off: pl.BlockSpec((tm, 512), lambda r, o=off // 512: (rev(r), o))
    w128 = lambda off: pl.BlockSpec((tm, LANES), lambda r, o=off // LANES: (rev(r), o))
    vec = lambda w: pl.BlockSpec((1, w), lambda r: (0, 0))
    row = lambda w: pl.BlockSpec((tm, w), lambda r: (rev(r), 0))
    return pl.pallas_call(
        body, name=name, grid=(nt,),
        in_specs=[w1024(cols.qa), w1024(cols.ka), w512(cols.qb), w128(cols.kb), w128(cols.fa),
                  vec(LANES), vec(LANES), vec(512), vec(LANES), vec(LANES),
                  row(1024), row(1024), row(1024), row(512), row(LANES), row(LANES), row(LANES), row(d), row(d)],
        out_specs=[row(cols.np), vec(LANES), vec(LANES), vec(512), vec(LANES), vec(LANES)],
        out_shape=[jax.ShapeDtypeStruct((m, cols.np), BF16), jax.ShapeDtypeStruct((1, LANES), F32),
                   jax.ShapeDtypeStruct((1, LANES), F32), jax.ShapeDtypeStruct((1, 512), F32),
                   jax.ShapeDtypeStruct((1, LANES), F32), jax.ShapeDtypeStruct((1, LANES), F32)],
        scratch_shapes=[pltpu.VMEM((1, LANES), F32)],
        compiler_params=_params(1),
    )(proj, proj, proj, proj, proj, gqa, gka, gqb, gkb, fbias, dqf, dkf, dvf, dqb, dkb, dvb, dc, dga, dgb)


def dproj_bwd(dh, h, g, dproj, wp, name):
    m, d = h.shape
    npad = wp.shape[0]
    tm = _row_tile(m)

    def body(dh_ref, h_ref, g_ref, dp_ref, w_ref, dhin_ref, dgn_ref):
        @pl.when(pl.program_id(0) == 0)
        def _():
            dgn_ref[...] = jnp.zeros_like(dgn_ref)

        dn = _dot(dp_ref[...], w_ref[...])
        dx, dgain = _rms_bwd(h_ref[...], g_ref[...], dn)
        dgn_ref[...] += dgain
        dhin_ref[...] = dh_ref[...] + dx

    row = lambda w: pl.BlockSpec((tm, w), lambda r: (r, 0))
    return pl.pallas_call(
        body, name=name, grid=(m // tm,),
        in_specs=[row(d), row(d), pl.BlockSpec((1, d), lambda r: (0, 0)), row(npad),
                  pl.BlockSpec((npad, d), lambda r: (0, 0))],
        out_specs=[row(d), pl.BlockSpec((1, d), lambda r: (0, 0))],
        out_shape=[jax.ShapeDtypeStruct((m, d), F32), jax.ShapeDtypeStruct((1, d), F32)],
        compiler_params=_params(1),
    )(dh, h, g, dproj, wp)


def _causal_t(t):
    return lax.broadcasted_iota(jnp.int32, (t, t), 0) <= lax.broadcasted_iota(jnp.int32, (t, t), 1)


HEADS_PER_STEP = 4


def fox_fwd(qf, kf, vt, name, rider=None):
    m = qf.shape[0]
    t = _row_tile(m)
    nq = m // t
    hp = HEADS_PER_STEP
    w = hp * LANES

    def body(q_ref, k_ref, vt_ref, o_ref, lse_ref, acc_ref, m_ref, p_ref, a_ref):
        qi = pl.program_id(1)
        acc_ref[...] = jnp.zeros_like(acc_ref)
        m_ref[...] = jnp.full_like(m_ref, NEG)

        def scores(ki, slot, mask):
            off = pl.multiple_of(ki * t, t)
            for e in range(hp):
                sl = slice(e * LANES, (e + 1) * LANES)
                s = _dot_nt(k_ref[pl.ds(off, t), sl], q_ref[:, sl])
                if mask is not None:
                    s = jnp.where(mask, s, NEG)
                m_old = m_ref[e]
                m_new = jnp.maximum(m_old, jnp.max(s, axis=0, keepdims=True))
                p_ref[slot, e] = jnp.exp(s - m_new).astype(BF16)
                a_ref[slot, e] = jnp.exp(m_old - m_new)
                m_ref[e] = m_new

        def values(ki, slot):
            off = pl.multiple_of(ki * t, t)
            for e in range(hp):
                sl = slice(e * LANES, (e + 1) * LANES)
                acc_ref[e] = acc_ref[e] * a_ref[slot, e] + _dot(vt_ref[sl, pl.ds(off, t)], p_ref[slot, e])

        causal = _causal_t(t)
        scores(0, 0, causal | (jnp.full((t, t), qi, jnp.int32) > 0))

        def step(ki, carry):
            values(ki - 1, (ki - 1) % 2)
            scores(ki, ki % 2, None)
            return carry

        lax.fori_loop(1, qi, step, 0)

        @pl.when(qi >= 1)
        def _():
            values(qi - 1, (qi - 1) % 2)
            scores(qi, qi % 2, causal)

        values(qi, qi % 2)
        row = lax.broadcasted_iota(jnp.int32, (LANES, t), 0)
        for e in range(hp):
            l = jnp.max(acc_ref[e, AUG:AUG + 8, :], axis=0, keepdims=True)
            o_ref[:, e * LANES:(e + 1) * LANES] = jnp.where(row < HEAD_DIM, acc_ref[e] * (1.0 / l), 0.0).T
            lse_ref[e] = m_ref[e] + jnp.log(l)

    return rider_call(
        body, name, (8 // hp, nq),
        in_specs=[pl.BlockSpec((t, w), lambda hd, i: (i, hd)),
                  pl.BlockSpec((m, w), lambda hd, i: (0, hd)),
                  pl.BlockSpec((w, m), lambda hd, i: (hd, 0))],
        out_specs=[pl.BlockSpec((t, w), lambda hd, i: (i, hd)),
                   pl.BlockSpec((hp, 1, t), lambda hd, i: (hd, 0, i))],
        out_shape=[jax.ShapeDtypeStruct((m, 8 * LANES), F32), jax.ShapeDtypeStruct((8, 1, m), F32)],
        scratch_shapes=[pltpu.VMEM((hp, LANES, t), F32), pltpu.VMEM((hp, 1, t), F32),
                        pltpu.VMEM((2, hp, t, t), BF16), pltpu.VMEM((2, hp, 1, t), F32)],
        args=(qf, kf, vt), rider=rider)


def fox_bwd(qf, kf, vf, kt, dof, lse, delta, name, rider=None):
    m = qf.shape[0]
    t = _row_tile(m)
    nq = m // t
    hp = HEADS_PER_STEP
    w = hp * LANES

    def body(k_ref, v_ref, kt_ref, q_ref, do_ref, lse_ref, delta_ref, dk_ref, dv_ref, dq_ref, dck_ref, dcq_ref,
             dka_ref, dva_ref, dqt_ref):
        ki = pl.program_id(1)

        @pl.when(ki == 0)
        def _():
            dqt_ref[...] = jnp.zeros_like(dqt_ref)

        dka_ref[...] = jnp.zeros_like(dka_ref)
        dva_ref[...] = jnp.zeros_like(dva_ref)

        def tile(qi, diagonal):
            off = pl.multiple_of(qi * t, t)
            for e in range(hp):
                sl = slice(e * LANES, (e + 1) * LANES)
                q = q_ref[pl.ds(off, t), sl]
                do = do_ref[pl.ds(off, t), sl]
                s = _dot_nt(k_ref[:, sl], q)
                if diagonal:
                    s = jnp.where(_causal_t(t), s, NEG)
                p = jnp.exp(s - lse_ref[e, :, pl.ds(off, t)])
                ds = (p * (_dot_nt(v_ref[:, sl], do) - delta_ref[e, :, pl.ds(off, t)])).astype(BF16)
                dva_ref[:, sl] += _dot(p.astype(BF16), do)
                dka_ref[:, sl] += _dot(ds, q)
                dqt_ref[sl, pl.ds(off, t)] += _dot(kt_ref[sl, :], ds)

        def step(qi, carry):
            tile(qi, False)
            return carry

        tile(ki, True)
        lax.fori_loop(ki + 1, nq, step, 0)
        dk_ref[...] = dka_ref[...]
        dv_ref[...] = dva_ref[...]
        row8 = lax.broadcasted_iota(jnp.int32, (8, 1), 0)
        for e in range(hp):
            slab = dka_ref[:, e * LANES:(e + 1) * LANES].T[AUG:AUG + 8, :]
            dck_ref[e] = -jnp.sum(jnp.where(row8 == 3, slab, 0.0), axis=0, keepdims=True)

        @pl.when(ki == nq - 1)
        def _():
            for e in range(hp):
                sl = slice(e * LANES, (e + 1) * LANES)
                slab = dqt_ref[e * LANES + AUG:e * LANES + AUG + 8, :]
                dcq_ref[e] = jnp.sum(jnp.where(row8 == 0, slab, 0.0), axis=0, keepdims=True)
                for j in range(nq):
                    dq_ref[j * t:(j + 1) * t, sl] = dqt_ref[sl, j * t:(j + 1) * t].T

    tile_spec = pl.BlockSpec((t, w), lambda hd, i: (i, hd))
    full = pl.BlockSpec((m, w), lambda hd, i: (0, hd))
    stat = pl.BlockSpec((hp, 1, m), lambda hd, i: (hd, 0, 0))
    (dkf, dvf, dqf, dck, dcq), carried = rider_call(
        body, name, (8 // hp, nq),
        in_specs=[tile_spec, tile_spec, pl.BlockSpec((w, t), lambda hd, i: (hd, i)), full, full, stat, stat],
        out_specs=[tile_spec, tile_spec, full, pl.BlockSpec((hp, 1, t), lambda hd, i: (hd, 0, i)), stat],
        out_shape=[jax.ShapeDtypeStruct((m, 8 * LANES), F32), jax.ShapeDtypeStruct((m, 8 * LANES), F32),
                   jax.ShapeDtypeStruct((m, 8 * LANES), F32), jax.ShapeDtypeStruct((8, 1, m), F32),
                   jax.ShapeDtypeStruct((8, 1, m), F32)],
        scratch_shapes=[pltpu.VMEM((t, w), F32), pltpu.VMEM((t, w), F32), pltpu.VMEM((w, m), F32)],
        args=(kf, vf, kt, qf, dof, lse, delta), rider=rider)
    return (dkf, dvf, dqf, dcq + dck), carried


def _bucket_ids():
    def bucket(dist):
        n = np.maximum(dist, 0)
        max_exact = N_BUCKETS // 2
        nf = np.maximum(n, 1).astype(np.float32)
        large = max_exact + (np.log(nf / max_exact) / math.log(MAX_DISTANCE / max_exact)
                             * (N_BUCKETS - max_exact)).astype(np.int32)
        return np.where(n < max_exact, n, np.minimum(large, N_BUCKETS - 1))

    tl = np.arange(LANES)[:, None]
    sl = np.arange(LANES)[None, :]
    prev = bucket(LANES + tl - sl)
    cur = bucket(tl - sl)
    meta = np.full((LANES, LANES), N_BUCKETS - 1)
    return np.concatenate([prev, cur, meta], axis=1).astype(np.int32)


def bias_build(table, name):
    ids = jnp.asarray(_bucket_ids())

    def body(t_ref, id_ref, o_ref):
        idv = id_ref[...]
        for h in range(8):
            acc = jnp.zeros((LANES, 3 * LANES), F32)
            for b in range(N_BUCKETS):
                acc = jnp.where(idv == b, t_ref[b, h], acc)
            o_ref[h] = acc

    return pl.pallas_call(
        body, name=name,
        in_specs=[pl.BlockSpec(memory_space=pltpu.SMEM), pl.BlockSpec(memory_space=pltpu.VMEM)],
        out_specs=pl.BlockSpec(memory_space=pltpu.VMEM),
        out_shape=jax.ShapeDtypeStruct((8, LANES, 3 * LANES), F32),
    )(table, ids)


def bias_reduce(dbias, name):
    ids = jnp.asarray(_bucket_ids())

    def body(d_ref, id_ref, o_ref):
        idv = id_ref[...]
        rr = lax.broadcasted_iota(jnp.int32, (N_BUCKETS, LANES), 0)
        cc = lax.broadcasted_iota(jnp.int32, (N_BUCKETS, LANES), 1)
        acc = jnp.zeros((N_BUCKETS, LANES), F32)
        for h in range(8):
            dv = d_ref[h]
            for b in range(N_BUCKETS):
                val = jnp.sum(jnp.where(idv == b, dv, 0.0), keepdims=True)
                acc = jnp.where((rr == b) & (cc == h), val, acc)
        o_ref[...] = acc

    return pl.pallas_call(
        body, name=name,
        in_specs=[pl.BlockSpec(memory_space=pltpu.VMEM), pl.BlockSpec(memory_space=pltpu.VMEM)],
        out_specs=pl.BlockSpec(memory_space=pltpu.VMEM),
        out_shape=jax.ShapeDtypeStruct((N_BUCKETS, LANES), F32),
    )(dbias, ids)


def _swa_penalty(n):
    shape = (LANES, 3 * LANES)
    tl = lax.broadcasted_iota(jnp.int32, shape, 0)
    col = lax.broadcasted_iota(jnp.int32, shape, 1)
    sl = col & (LANES - 1)
    nv = jnp.full(shape, n, jnp.int32)
    is_meta = sl >= PAD_FRONT
    prev = (col < LANES) & (sl > tl) & (nv >= 1) & ((nv >= 2) | is_meta)
    cur = (col >= LANES) & (col < 2 * LANES) & (sl <= tl) & ((nv >= 1) | is_meta)
    meta = (col >= 2 * LANES) & is_meta & ((nv >= 2) | ((nv == 1) & (sl <= tl)))
    return jnp.where(prev | cur | meta, 0.0, NEG)


def _swa_keys(ref, n):
    off_prev = pl.multiple_of(jnp.maximum(n - 1, 0) * LANES, LANES)
    off_cur = pl.multiple_of(n * LANES, LANES)
    return jnp.concatenate([ref[pl.ds(off_prev, LANES), :], ref[pl.ds(off_cur, LANES), :], ref[0:LANES, :]], axis=0)


def swa_fwd(q, k, v, bias, sinks, name):
    m = q.shape[0]

    def body(q_ref, k_ref, v_ref, bias_ref, sink_ref, o_ref, lse_ref):
        n = pl.program_id(0)
        lane1 = lax.broadcasted_iota(jnp.int32, (1, LANES), 1)
        lane_t = lax.broadcasted_iota(jnp.int32, (LANES, LANES), 1)
        in_head = [lane1 < HEAD_DIM, lane1 >= HEAD_DIM]
        kall = _swa_keys(k_ref, n)
        vall = _swa_keys(v_ref, n)
        vs = [jnp.where(in_head[g], vall, jnp.zeros_like(vall)) for g in (0, 1)]
        penalty = _swa_penalty(n)
        lse = jnp.zeros((LANES, LANES), F32)
        for b in range(4):
            qb = q_ref[:, b * LANES:(b + 1) * LANES]
            ob = jnp.zeros((LANES, LANES), F32)
            for g in (0, 1):
                h = 4 * g + b
                qe = jnp.where(in_head[g], qb, jnp.zeros_like(qb))
                s = _dot_nt(qe, kall) + bias_ref[h] + penalty
                sink = sink_ref[h]
                mx = jnp.maximum(jnp.max(s, axis=1, keepdims=True), sink)
                p = jnp.exp(s - mx)
                den = jnp.sum(p, axis=1, keepdims=True) + jnp.exp(sink - mx)
                ob = ob + _dot((p / den).astype(BF16), vs[g])
                lse = jnp.where(lane_t == h, mx + jnp.log(den), lse)
            o_ref[:, b * LANES:(b + 1) * LANES] = ob
        lse_ref[...] = lse

    return pl.pallas_call(
        body, name=name, grid=(m // LANES,),
        in_specs=[pl.BlockSpec((LANES, 512), lambda n: (n, 0)),
                  pl.BlockSpec((m, LANES), lambda n: (0, 0)), pl.BlockSpec((m, LANES), lambda n: (0, 0)),
                  pl.BlockSpec((8, LANES, 3 * LANES), lambda n: (0, 0, 0)),
                  pl.BlockSpec(memory_space=pltpu.SMEM)],
        out_specs=[pl.BlockSpec((LANES, 512), lambda n: (n, 0)), pl.BlockSpec((LANES, LANES), lambda n: (n, 0))],
        out_shape=[jax.ShapeDtypeStruct((m, 512), F32), jax.ShapeDtypeStruct((m, LANES), F32)],
        compiler_params=_params(1),
    )(q, k, v, bias, sinks)


def swa_bwd(q, k, v, bias, sinks, o, lse, do, name):
    m = q.shape[0]

    def body(q_ref, do_ref, o_ref, lse_ref, k_ref, v_ref, bias_ref, sink_ref,
             dq_ref, dk_ref, dv_ref, dbias_ref, dsink_ref):
        n = pl.program_id(0)

        @pl.when(n == 0)
        def _():
            for r in (dk_ref, dv_ref, dbias_ref, dsink_ref):
                r[...] = jnp.zeros_like(r)

        lane1 = lax.broadcasted_iota(jnp.int32, (1, LANES), 1)
        lane_t = lax.broadcasted_iota(jnp.int32, (LANES, LANES), 1)
        in_head = [lane1 < HEAD_DIM, lane1 >= HEAD_DIM]
        off_prev = pl.multiple_of(jnp.maximum(n - 1, 0) * LANES, LANES)
        off_cur = pl.multiple_of(n * LANES, LANES)
        kall = _swa_keys(k_ref, n)
        vall = _swa_keys(v_ref, n)
        ks = [jnp.where(in_head[g], kall, jnp.zeros_like(kall)) for g in (0, 1)]
        penalty = _swa_penalty(n)
        lsev = lse_ref[...]
        dsink = dsink_ref[...]
        dkall = jnp.zeros((3 * LANES, LANES), F32)
        dvall = jnp.zeros((3 * LANES, LANES), F32)
        for b in range(4):
            sl = slice(b * LANES, (b + 1) * LANES)
            qb = q_ref[:, sl]
            dob = do_ref[:, sl]
            prod = dob * o_ref[:, sl]
            dqb = jnp.zeros((LANES, LANES), F32)
            for g in (0, 1):
                h = 4 * g + b
                qe = jnp.where(in_head[g], qb, jnp.zeros_like(qb))
                doe = jnp.where(in_head[g], dob, 0.0).astype(BF16)
                delta = jnp.sum(jnp.where(in_head[g], prod, 0.0), axis=1, keepdims=True)
                lse_h = _lane_col(lsev, lane_t, h)
                s = _dot_nt(qe, kall) + bias_ref[h] + penalty
                p = jnp.exp(s - lse_h)
                ds = p * (_dot_nt(doe, vall) - delta)
                dbias_ref[h] += ds
                sink_part = jnp.sum(-jnp.exp(sink_ref[h] - lse_h) * delta, keepdims=True)
                dsink = jnp.where(lane1 == h, dsink + sink_part, dsink)
                dsb = ds.astype(BF16)
                dqb = dqb + _dot(dsb, ks[g])
                dkall = dkall + _dot_tn(dsb, qe)
                dvall = dvall + _dot_tn(p.astype(BF16), doe)
            dq_ref[:, sl] = dqb
        dsink_ref[...] = dsink
        for ref, val in ((dk_ref, dkall), (dv_ref, dvall)):
            ref[pl.ds(off_prev, LANES), :] += val[0:LANES]
            ref[pl.ds(off_cur, LANES), :] += val[LANES:2 * LANES]
            ref[0:LANES, :] += val[2 * LANES:3 * LANES]

    blk = pl.BlockSpec((LANES, 512), lambda n: (n, 0))
    full = pl.BlockSpec((m, LANES), lambda n: (0, 0))
    return pl.pallas_call(
        body, name=name, grid=(m // LANES,),
        in_specs=[blk, blk, blk, pl.BlockSpec((LANES, LANES), lambda n: (n, 0)), full, full,
                  pl.BlockSpec((8, LANES, 3 * LANES), lambda n: (0, 0, 0)),
                  pl.BlockSpec(memory_space=pltpu.SMEM)],
        out_specs=[blk, full, full, pl.BlockSpec((8, LANES, 3 * LANES), lambda n: (0, 0, 0)),
                   pl.BlockSpec((1, LANES), lambda n: (0, 0))],
        out_shape=[jax.ShapeDtypeStruct((m, 512), F32), jax.ShapeDtypeStruct((m, LANES), F32),
                   jax.ShapeDtypeStruct((m, LANES), F32), jax.ShapeDtypeStruct((8, LANES, 3 * LANES), F32),
                   jax.ShapeDtypeStruct((1, LANES), F32)],
        compiler_params=_params(1),
    )(q, do, o, lse, k, v, bias, sinks)


def branch_out(h, o_fox, o_swa, proj, wbf, wbs, wo, cols, name):
    m, d = h.shape
    tm = _row_tile(m)

    def body(h_ref, of_ref, os_ref, ga_ref, gb_ref, wbf_ref, wbs_ref, wo_ref, hn_ref):
        tf = _dot(of_ref[...].astype(BF16), wbf_ref[...])
        ts = _dot(os_ref[...].astype(BF16), wbs_ref[...])
        y = jax.nn.sigmoid(ga_ref[...]) * tf + jax.nn.sigmoid(gb_ref[...]) * ts
        hn_ref[...] = h_ref[...] + _dot(y.astype(BF16), wo_ref[...])

    row = lambda w, o=0: pl.BlockSpec((tm, w), lambda r, o=o: (r, o))
    res = lambda a: pl.BlockSpec(a.shape, lambda r: (0, 0))
    return pl.pallas_call(
        body, name=name, grid=(m // tm,),
        in_specs=[row(d), row(1024), row(512), row(d, cols.ga // d), row(d, cols.gb // d), res(wbf), res(wbs), res(wo)],
        out_specs=row(d),
        out_shape=jax.ShapeDtypeStruct((m, d), F32),
        compiler_params=_params(1),
    )(h, o_fox, o_swa, proj, proj, wbf, wbs, wo)


def branch_out_bwd(dh, o_fox, o_swa, proj, wbf, wbs, wo, cols, name):
    m, d = dh.shape
    tm = _row_tile(m)

    def body(dh_ref, of_ref, os_ref, ga_ref, gb_ref, wbf_ref, wbs_ref, wo_ref,
             y_ref, dtf_ref, dts_ref, dga_ref, dgb_ref, dof_ref, dos_ref, delta_ref):
        dy = _dot_nt(dh_ref[...].astype(BF16), wo_ref[...])
        tf = _dot(of_ref[...].astype(BF16), wbf_ref[...])
        ts = _dot(os_ref[...].astype(BF16), wbs_ref[...])
        sa = jax.nn.sigmoid(ga_ref[...])
        sb = jax.nn.sigmoid(gb_ref[...])
        y_ref[...] = (sa * tf + sb * ts).astype(BF16)
        dtf = (dy * sa).astype(BF16)
        dts = (dy * sb).astype(BF16)
        dtf_ref[...] = dtf
        dts_ref[...] = dts
        dga_ref[...] = (dy * tf * sa * (1.0 - sa)).astype(BF16)
        dgb_ref[...] = (dy * ts * sb * (1.0 - sb)).astype(BF16)
        dof = _dot_nt(dtf, wbf_ref[...])
        dof_ref[...] = dof.astype(BF16)
        dos_ref[...] = _dot_nt(dts, wbs_ref[...])
        lane = lax.broadcasted_iota(jnp.int32, (tm, LANES), 1)
        delta = jnp.zeros((tm, LANES), F32)
        for hd in range(8):
            sl = slice(hd * LANES, (hd + 1) * LANES)
            delta = jnp.where(lane == hd, jnp.sum(dof[:, sl] * of_ref[:, sl], axis=1, keepdims=True), delta)
        delta_ref[...] = delta

    row = lambda w, o=0: pl.BlockSpec((tm, w), lambda r, o=o: (r, o))
    res = lambda a: pl.BlockSpec(a.shape, lambda r: (0, 0))
    return pl.pallas_call(
        body, name=name, grid=(m // tm,),
        in_specs=[row(d), row(1024), row(512), row(d, cols.ga // d), row(d, cols.gb // d), res(wbf), res(wbs), res(wo)],
        out_specs=[row(d)] * 5 + [row(1024), row(512), row(LANES)],
        out_shape=[jax.ShapeDtypeStruct((m, d), BF16)] * 5 + [jax.ShapeDtypeStruct((m, 1024), BF16),
                   jax.ShapeDtypeStruct((m, 512), F32), jax.ShapeDtypeStruct((m, LANES), F32)],
        compiler_params=_params(1),
    )(dh, o_fox, o_swa, proj, proj, wbf, wbs, wo)


def loss_head(h, target, name):
    m, d = h.shape

    def body(h_ref, t_ref, dh_ref, loss_ref):
        n = pl.program_id(0)

        @pl.when(n == 0)
        def _():
            loss_ref[...] = jnp.zeros_like(loss_ref)
            dh_ref[...] = jnp.zeros_like(dh_ref)

        @pl.when(n > 0)
        def _():
            err = h_ref[...] - t_ref[...]
            dh_ref[...] = err * (1.0 / d)
            loss_ref[...] += jnp.sum(err * err, keepdims=True) * (0.5 / d)

    return pl.pallas_call(
        body, name=name, grid=(m // LANES,),
        in_specs=[pl.BlockSpec((LANES, d), lambda n: (n, 0)),
                  pl.BlockSpec((LANES, d), lambda n: (jnp.maximum(n - 1, 0), 0))],
        out_specs=[pl.BlockSpec((LANES, d), lambda n: (n, 0)), pl.BlockSpec((8, LANES), lambda n: (0, 0))],
        out_shape=[jax.ShapeDtypeStruct((m, d), F32), jax.ShapeDtypeStruct((8, LANES), F32)],
        compiler_params=_params(1),
    )(h, target)


def _adamw_math(w, g, m, v):
    m = ADAM_B1 * m + (1.0 - ADAM_B1) * g
    v = ADAM_B2 * v + (1.0 - ADAM_B2) * (g * g)
    m_hat = m / (1.0 - ADAM_B1 ** ADAM_STEP)
    v_hat = v / (1.0 - ADAM_B2 ** ADAM_STEP)
    delta = -ADAM_LR * (m_hat / (jnp.sqrt(v_hat) + ADAM_EPS) + ADAM_WD * w)
    return delta, m, v


def adamw_sum(parts, w, m, v, name, after=None):
    n_layers, a, b = w.shape
    ta = next(t for t in (256, 176, 128, a) if a % t == 0)
    nr = a // ta

    def body(*refs):
        p_refs = refs[:n_layers]
        w_ref, m_ref, v_ref = refs[n_layers:n_layers + 3]
        g_o, d_o, m_o, v_o = refs[-4:]
        for l in range(n_layers):
            @pl.when(pl.program_id(0) == l)
            def _(l=l):
                g = p_refs[l][0].astype(F32)
                for j in range(1, N_DEV):
                    g = g + p_refs[l][j].astype(F32)
                g_o[0] = g
                d_o[0], m_o[0], v_o[0] = _adamw_math(w_ref[0], g, m_ref[0], v_ref[0])

    def part_spec(l):
        return pl.BlockSpec((N_DEV, ta, b), lambda i, r, l=l: (0, jnp.where(i == l, r, jnp.where(i < l, 0, nr - 1)), 0))

    row = pl.BlockSpec((1, ta, b), lambda i, r: (i, r, 0))
    return pl.pallas_call(
        body, name=name, grid=(n_layers, nr),
        in_specs=[part_spec(l) for l in range(n_layers)] + [row, row, row]
                 + ([pl.BlockSpec(memory_space=pl.ANY)] if after is not None else []),
        out_specs=[row] * 4,
        out_shape=[jax.ShapeDtypeStruct(w.shape, F32)] * 4,
        compiler_params=_params(2),
    )(*parts, w, m, v, *([after] if after is not None else []))


def adamw_sum_cols(parts, w, m, v, name, after=None):
    n_layers = len(parts)
    a, b = parts[0].shape[1:]
    tc = 512 if b % 512 == 0 else b
    nc = b // tc

    def body(*refs):
        p_refs = refs[:n_layers]
        w_ref, m_ref, v_ref = refs[n_layers:n_layers + 3]
        g_o, d_o, m_o, v_o = refs[-4:]
        for l in range(n_layers):
            @pl.when(pl.program_id(0) == l)
            def _(l=l):
                g = p_refs[l][0].astype(F32)
                for j in range(1, N_DEV):
                    g = g + p_refs[l][j].astype(F32)
                g_o[...] = g
                d_o[...], m_o[...], v_o[...] = _adamw_math(w_ref[...], g, m_ref[...], v_ref[...])

    def part_spec(l):
        return pl.BlockSpec((N_DEV, a, tc), lambda i, c, l=l: (0, 0, jnp.where(i == l, c, jnp.where(i < l, 0, nc - 1))))

    col = pl.BlockSpec((a, tc), lambda i, c: (0, i * nc + c))
    return pl.pallas_call(
        body, name=name, grid=(n_layers, nc),
        in_specs=[part_spec(l) for l in range(n_layers)] + [col, col, col]
                 + ([pl.BlockSpec(memory_space=pl.ANY)] if after is not None else []),
        out_specs=[col] * 4,
        out_shape=[jax.ShapeDtypeStruct(w.shape, F32)] * 4,
        compiler_params=_params(2),
    )(*parts, w, m, v, *([after] if after is not None else []))


def adamw_small(g, w, m, v, name):
    def body(g_ref, w_ref, m_ref, v_ref, d_o, m_o, v_o):
        d_o[...], m_o[...], v_o[...] = _adamw_math(w_ref[...], g_ref[...], m_ref[...], v_ref[...])

    spec = pl.BlockSpec(memory_space=pltpu.VMEM)
    return pl.pallas_call(
        body, name=name, in_specs=[spec] * 4, out_specs=[spec] * 3,
        out_shape=[jax.ShapeDtypeStruct(w.shape, F32)] * 3,
    )(g, w, m, v)


BIG = ("ffn1_w_in", "ffn1_w_out", "w_in", "w_branch_fox", "w_branch_swa", "w_out", "ffn2_w_in", "ffn2_w_out")
SMALL = ("rel_bias_table", "ffn1_norm", "mix_norm", "forget_bias", "fox_q_norm", "fox_k_norm",
         "swa_q_norm", "swa_k_norm", "swa_sinks", "ffn2_norm")
WEIGHTS = ("meta_tokens", "rel_bias_table", "ffn1_norm", "ffn1_w_in", "ffn1_w_out", "mix_norm", "w_in",
           "forget_bias", "fox_q_norm", "fox_k_norm", "swa_q_norm", "swa_k_norm", "swa_sinks", "w_branch_fox",
           "w_branch_swa", "w_out", "ffn2_norm", "ffn2_w_in", "ffn2_w_out")


def _pack(arrs, width, row_multiple, dtype):
    lead = arrs[0].shape[:-1]
    flat = jnp.concatenate([a.astype(dtype) for a in arrs], axis=-1)
    n = flat.shape[-1]
    rows = -(-n // width)
    rows = -(-rows // row_multiple) * row_multiple
    flat = jnp.pad(flat, [(0, 0)] * len(lead) + [(0, rows * width - n)])
    return flat.reshape(lead + (rows, width))


def _unpack(flat, shapes):
    flat = flat.reshape(-1)
    out, off = [], 0
    for s in shapes:
        n = int(np.prod(s))
        out.append(flat[off:off + n].reshape(s))
        off += n
    return out


def _swa_head_order():
    return [4 * (j % 2) + j // 2 for j in range(8)]


def _permute_heads(a, axis, inverse=False):
    order = _swa_head_order()
    if inverse:
        order = [order.index(hd) for hd in range(8)]
    parts = [lax.slice_in_dim(a, hd * HEAD_DIM, (hd + 1) * HEAD_DIM, axis=axis) for hd in order]
    return jnp.concatenate(parts, axis=axis)


def _w_in_segments(cols):
    d = cols.d
    segs = [(512 * i + HEAD_DIM * hd, HEAD_DIM, new + LANES * hd)
            for i, new in enumerate((cols.qa, cols.ka, cols.va)) for hd in range(8)]
    segs.append((1536, 8, cols.fa))
    order = _swa_head_order()
    segs += [(1544 + HEAD_DIM * hd, HEAD_DIM, cols.qb + HEAD_DIM * order.index(hd)) for hd in range(8)]
    segs += [(2056, 128, cols.kb), (2184, 128, cols.vb), (2312, d, cols.ga), (2312 + d, d, cols.gb)]
    return segs


def _reorder_w_in(wt, cols):
    zeros = lambda n: jnp.zeros((n, wt.shape[1]), wt.dtype)
    parts, at = [], 0
    for old, length, new in sorted(_w_in_segments(cols), key=lambda s: s[2]):
        if new > at:
            parts.append(zeros(new - at))
        at = new + length
        parts.append(wt[old:old + length])
    parts.append(zeros(cols.np - at))
    return jnp.concatenate(parts, axis=0)


def _restore_w_in(wpt, cols, width):
    segs = sorted(_w_in_segments(cols))
    blocks = []
    for j in range(N_DEV):
        lo, hi = j * width, (j + 1) * width
        parts = []
        for old, length, new in segs:
            a, b = max(old, lo), min(old + length, hi)
            if a < b:
                parts.append(wpt[new + a - old:new + b - old])
        blocks.append(jnp.concatenate(parts, axis=0))
    return jnp.stack(blocks)


def _lane_pad(v):
    return jnp.pad(v, ((0, 0), (0, LANES - v.shape[1])))


def kernel(x, meta_tokens, rel_bias_table, ffn1_norm, ffn1_w_in, ffn1_w_out, mix_norm, w_in, forget_bias, fox_q_norm, fox_k_norm, swa_q_norm, swa_k_norm, swa_sinks, w_branch_fox, w_branch_swa, w_out, ffn2_norm, ffn2_w_in, ffn2_w_out, loss_target, m_meta_tokens, m_rel_bias_table, m_ffn1_norm, m_ffn1_w_in, m_ffn1_w_out, m_mix_norm, m_w_in, m_forget_bias, m_fox_q_norm, m_fox_k_norm, m_swa_q_norm, m_swa_k_norm, m_swa_sinks, m_w_branch_fox, m_w_branch_swa, m_w_out, m_ffn2_norm, m_ffn2_w_in, m_ffn2_w_out, v_meta_tokens, v_rel_bias_table, v_ffn1_norm, v_ffn1_w_in, v_ffn1_w_out, v_mix_norm, v_w_in, v_forget_bias, v_fox_q_norm, v_fox_k_norm, v_swa_q_norm, v_swa_k_norm, v_swa_sinks, v_w_branch_fox, v_w_branch_swa, v_w_out, v_ffn2_norm, v_ffn2_w_in, v_ffn2_w_out):
    args = dict(locals())
    wts = {n: args[n] for n in WEIGHTS}
    mom1 = {n: args["m_" + n] for n in WEIGHTS}
    mom2 = {n: args["v_" + n] for n in WEIGHTS}

    seq, d = x.shape[1], x.shape[2]
    m_rows = seq + LANES
    depth = ffn1_norm.shape[0]
    fb = ffn1_w_in.shape[2]
    fo = ffn1_w_out.shape[1]
    din_shard = w_in.shape[2]
    cols = _Cols(d)
    scale = HEAD_DIM ** -0.5
    dev = 4 * lax.axis_index("x") + 2 * lax.axis_index("y") + lax.axis_index("c")

    groups = {"ffn1": ("ffn1_w_in", "ffn1_w_out"), "mix": ("w_in", "w_branch_fox", "w_branch_swa", "w_out"),
              "ffn2": ("ffn2_w_in", "ffn2_w_out"), "ffn1_in": ("ffn1_w_in",), "ffn1_out": ("ffn1_w_out",),
              "ffn2_in": ("ffn2_w_in",), "ffn2_out": ("ffn2_w_out",)}
    flipped = ("ffn1_w_in", "ffn2_w_in")
    for n in flipped:
        wts[n], mom1[n], mom2[n] = (jnp.swapaxes(a, 1, 2) for a in (wts[n], mom1[n], mom2[n]))
    to_rows = lambda a: jnp.transpose(a, (2, 0, 1)).reshape(din_shard, depth * d)
    from_rows = lambda a: jnp.transpose(a.reshape(din_shard, depth, d), (1, 2, 0))
    wts["w_in"], mom1["w_in"], mom2["w_in"] = (to_rows(a) for a in (wts["w_in"], mom1["w_in"], mom2["w_in"]))
    shard = {n: wts[n].astype(BF16) for n in BIG}
    w_in_rows = shard.pop("w_in")
    shard["w_in"] = [w_in_rows[:, l * d:(l + 1) * d] for l in range(depth)]
    full, parts, gw = {}, {}, {}

    def keys_of(stages):
        return [(n, l) for g, l in stages if l < depth for n in groups[g]]

    def gather_rider(stages):
        return Rider([shard[n][l] for n, l in keys_of(stages)], True)

    def scatter_rider(stages):
        return Rider([gw[k] for k in keys_of(stages)], False)

    def ffn_weights(tag, l):
        return full[tag + "_w_in", l], full[tag + "_w_out", l].reshape(4, fb, d)

    def mixer_weights(l):
        wp = _reorder_w_in(full["w_in", l].reshape(N_DEV * din_shard, d), cols)
        wbf = jnp.concatenate([full["w_branch_fox", l][j] for j in range(N_DEV)], axis=1)
        wbf = jnp.pad(wbf.reshape(8, HEAD_DIM, d), ((0, 0), (0, LANES - HEAD_DIM), (0, 0))).reshape(8 * LANES, d)
        wbs = _permute_heads(jnp.concatenate([full["w_branch_swa", l][j] for j in range(N_DEV)], axis=1), 0)
        return wp, wbf, wbs, full["w_out", l].reshape(d, d)

    full.update(zip(keys_of([("ffn1", 0)]), exchange_hbm(gather_rider([("ffn1", 0)]).srcs, True, "gather_first")))
    meta_all = gather_small(meta_tokens.reshape(1, N_META, -1), "gather_meta")
    meta_full = meta_all.transpose(1, 0, 2).reshape(N_META, d)
    tile8 = lambda g, s=1.0: jnp.tile(g.reshape(1, HEAD_DIM) * s, (1, 8))
    tile2 = lambda g: jnp.tile(g.reshape(1, HEAD_DIM), (1, 2))
    data_lanes = lambda g, s=1.0: _lane_pad(g.reshape(1, HEAD_DIM) * s)
    bias = bias_build(rel_bias_table, "swa_bias")

    first = jnp.concatenate([jnp.zeros((PAD_FRONT, d), F32), meta_full], axis=0)
    h = jnp.concatenate([first, x[0]], axis=0)
    saved, lw = [], []
    for l in range(depth):
        s, w = {"h0": h}, {}
        w["ffn1_in"], w["ffn1_out"] = ffn_weights("ffn1", l)
        stages = [("mix", l)]
        (h, s["n1"], s["a1"], s["fg1"], s["fu1"]), got = ffn_fwd(h, ffn1_norm[l:l + 1], w["ffn1_in"], w["ffn1_out"],
                                                          f"ffn1_fwd_{l}", gather_rider(stages))
        full.update(zip(keys_of(stages), got))
        s["h1"] = h
        w["wp"], w["wbf"], w["wbs"], w["wo"] = mixer_weights(l)
        s["nm"], s["proj"] = mixer_proj(h, mix_norm[l:l + 1], w["wp"], f"mixer_proj_{l}")
        s["gains"] = (data_lanes(fox_q_norm[l], scale), data_lanes(fox_k_norm[l]), tile8(swa_q_norm[l], scale),
                      tile2(swa_k_norm[l]))
        s["fbias"] = _lane_pad(forget_bias[l:l + 1])
        qf, kf, vf, kt, vt, qb, kb, vb = qk_post(s["proj"], s["gains"], s["fbias"], cols, f"qk_post_{l}")
        s.update(qf=qf, kf=kf, vf=vf, kt=kt, qb=qb, kb=kb, vb=vb)
        stages = [("ffn2", l)]
        (s["o_fox"], s["lse_fox"]), got = fox_fwd(qf, kf, vt, f"fox_fwd_{l}", gather_rider(stages))
        full.update(zip(keys_of(stages), got))
        s["o_swa"], s["lse_swa"] = swa_fwd(qb, kb, vb, bias, swa_sinks[l], f"swa_fwd_{l}")
        h = branch_out(h, s["o_fox"], s["o_swa"], s["proj"], w["wbf"], w["wbs"], w["wo"], cols, f"branch_out_{l}")
        s["h2"] = h
        w["ffn2_in"], w["ffn2_out"] = ffn_weights("ffn2", l)
        stages = [("ffn1", l + 1)]
        (h, s["n2"], s["a2"], s["fg2"], s["fu2"]), got = ffn_fwd(h, ffn2_norm[l:l + 1], w["ffn2_in"], w["ffn2_out"],
                                                          f"ffn2_fwd_{l}", gather_rider(stages))
        full.update(zip(keys_of(stages), got))
        saved.append(s)
        lw.append(w)

    dh, loss_part = loss_head(h, loss_target[0], "loss_head")

    gs = {n: [None] * depth for n in SMALL}
    dbias_total = None
    for l in reversed(range(depth)):
        w, s = lw[l], saved[l]

        def ffn_back(dh, tag, hin, norm, n_in, a, f_gate, f_up, stages):
            (dh_in, dg, du, dgn, dhs), got = ffn_bwd(dh, hin, norm, f_gate, f_up, w[tag + "_in"], w[tag + "_out"],
                                                     f"{tag}_bwd_{l}", scatter_rider(stages))
            parts.update(zip(keys_of(stages), got))
            gw[tag + "_w_out", l] = matmul_tn(a, dhs[None], f"{tag}_dwo_{l}").reshape(N_DEV, fo, d)
            stages = [(tag + "_out", l)]
            gw[tag + "_w_in", l], got = matmul_tn(dg, n_in[None], f"{tag}_dwi_{l}", x2=du,
                                                  rider=scatter_rider(stages))
            parts.update(zip(keys_of(stages), got))
            return dh_in, dgn

        dh, gs["ffn2_norm"][l] = ffn_back(dh, "ffn2", s["h2"], ffn2_norm[l:l + 1], s["n2"], s["a2"], s["fg2"],
                                          s["fu2"], [("ffn1_in", l + 1)])

        y, dtf, dts, dga, dgb, dof, dos, delta = branch_out_bwd(dh, s["o_fox"], s["o_swa"], s["proj"], w["wbf"],
                                                                w["wbs"], w["wo"], cols, f"branch_out_bwd_{l}")
        gw["w_out", l] = matmul_tn(y[None], dh[None], f"dw_out_{l}").reshape(N_DEV, d // N_DEV, d)
        to_shards = lambda a: a.reshape(512, N_DEV, d // N_DEV).transpose(1, 0, 2)
        gw["w_branch_fox", l] = to_shards(matmul_tn(s["o_fox"][None], dtf[None], f"dw_branch_fox_{l}")[0]
                                          .reshape(8, LANES, d)[:, :HEAD_DIM].reshape(512, d))
        gw["w_branch_swa", l] = to_shards(_permute_heads(
            matmul_tn(s["o_swa"][None], dts[None], f"dw_branch_swa_{l}")[0], 0, inverse=True))

        stages = [("ffn2_in", l)]
        (dkf, dvf, dqf, dc_rows), got = fox_bwd(s["qf"], s["kf"], s["vf"], s["kt"], dof, s["lse_fox"],
                                         delta[:, :8].T.reshape(8, 1, m_rows), f"fox_bwd_{l}", scatter_rider(stages))
        parts.update(zip(keys_of(stages), got))
        dc = _lane_pad(dc_rows.reshape(8, m_rows).T)
        dqb, dkb, dvb, dbias, dsink = swa_bwd(s["qb"], s["kb"], s["vb"], bias, swa_sinks[l], s["o_swa"], s["lse_swa"],
                                              dos, f"swa_bwd_{l}")
        dbias_total = dbias if dbias_total is None else dbias_total + dbias
        gs["swa_sinks"][l] = dsink[0, :8]
        dproj, ggqa, ggka, ggqb, ggkb, gfb = qk_post_bwd(s["proj"], s["gains"], s["fbias"], dqf, dkf, dvf, dqb, dkb,
                                                         dvb, dc, dga, dgb, cols, f"qk_post_bwd_{l}")
        gs["fox_q_norm"][l] = ggqa[0, :HEAD_DIM] * scale
        gs["fox_k_norm"][l] = ggka[0, :HEAD_DIM]
        gs["swa_q_norm"][l] = ggqb.reshape(8, HEAD_DIM).sum(0) * scale
        gs["swa_k_norm"][l] = ggkb.reshape(2, HEAD_DIM).sum(0)
        gs["forget_bias"][l] = gfb[0, :8]
        dwp = matmul_tn(dproj[None], s["nm"][None], f"dw_in_{l}", tk=1024 if cols.np % 1024 == 0 else cols.np)[0]
        gw["w_in", l] = _restore_w_in(dwp, cols, din_shard)
        dh, gs["mix_norm"][l] = dproj_bwd(dh, s["h1"], mix_norm[l:l + 1], dproj, w["wp"], f"dproj_bwd_{l}")

        dh, gs["ffn1_norm"][l] = ffn_back(dh, "ffn1", s["h0"], ffn1_norm[l:l + 1], s["n1"], s["a1"], s["fg1"],
                                          s["fu1"], [("mix", l)])

    grad_x = dh[LANES:][None]
    dmeta = dh[PAD_FRONT:LANES]
    dtable = bias_reduce(dbias_total, "swa_dbias")[:, :8]

    last = ("ffn1_w_in", 0)
    send_sems, recv_sems, src_thru, land_thru, token = scatter_start(gw[last], "scatter_last_start")
    big_out = [{}, {}, {}, {}]
    for n in BIG:
        if n != last[0]:
            update = adamw_sum_cols if n == "w_in" else adamw_sum
            outs = update([parts[n, l] for l in range(depth)], wts[n], mom1[n], mom2[n], f"adamw_{n}", after=token)
            for k in range(4):
                big_out[k][n] = outs[k]
    sent, landed = scatter_wait(send_sems, recv_sems, src_thru, land_thru,
                                [big_out[1][n] for n in BIG if n != last[0]], "scatter_last_wait")
    parts[last] = lax.dynamic_update_slice_in_dim(landed, lax.dynamic_slice_in_dim(sent, dev, 1, axis=0), dev, axis=0)
    outs = adamw_sum([parts[last[0], l] for l in range(depth)], wts[last[0]], mom1[last[0]], mom2[last[0]],
                     f"adamw_{last[0]}")
    for k in range(4):
        big_out[k][last[0]] = outs[k]

    small_g = {n: (jnp.stack(gs[n]) if n != "rel_bias_table" else None) for n in SMALL}
    small_g["rel_bias_table"] = dtable
    pieces = [loss_part[0:1, 0:1].reshape(1, 1)] + [small_g[n].reshape(1, -1) for n in SMALL] + [dmeta.reshape(1, -1)]
    small_shapes = [(1,)] + [wts[n].shape for n in SMALL] + [(N_META, d)]
    total = allsum_small(_pack(pieces, LANES, 8, F32), "allsum_small")
    summed = _unpack(total, small_shapes)
    loss = summed[0][0]
    g_small = dict(zip(SMALL, summed[1:1 + len(SMALL)]))
    g_meta = lax.dynamic_slice_in_dim(summed[-1], dev * (d // N_DEV), d // N_DEV, axis=1)
    names = SMALL + ("meta_tokens",)
    g_small["meta_tokens"] = g_meta
    pk = lambda src: _pack([src[n].reshape(1, -1) for n in names], LANES, 8, F32)[0]
    small_out = [dict(zip(names, _unpack(o, [wts[n].shape for n in names])))
                 for o in adamw_small(pk(g_small), pk(wts), pk(mom1), pk(mom2), "adamw_small")]

    for out in big_out:
        for n in flipped:
            out[n] = jnp.swapaxes(out[n], 1, 2)
        out["w_in"] = from_rows(out["w_in"])
    grads = {**big_out[0], **g_small}
    delta = {**big_out[1], **small_out[0]}
    new_m = {**big_out[2], **small_out[1]}
    new_v = {**big_out[3], **small_out[2]}
    return (loss, grad_x, *[grads[n] for n in WEIGHTS], *[delta[n] for n in WEIGHTS],
            *[new_m[n] for n in WEIGHTS], *[new_v[n] for n in WEIGHTS])
```

```python
import math

import numpy as np
import jax
import jax.numpy as jnp
from jax import lax
from jax.experimental import pallas as pl
from jax.experimental.pallas import tpu as pltpu

F32 = jnp.float32
BF16 = jnp.bfloat16
EPS = 1e-6
NEG = -1e30
HEAD_DIM = 64
LANES = 128
N_META = 16
PAD_FRONT = LANES - N_META
N_BUCKETS = 32
MAX_DISTANCE = 128
N_DEV = 8
ADAM_LR, ADAM_B1, ADAM_B2, ADAM_EPS, ADAM_WD, ADAM_STEP = 0.001, 0.9, 0.999, 1e-08, 0.01, 10
VMEM_LIMIT = 56 * 1024 * 1024
MESH = pl.DeviceIdType.MESH


def _params(n_grid):
    return pltpu.CompilerParams(dimension_semantics=("arbitrary",) * n_grid,
                                vmem_limit_bytes=VMEM_LIMIT)


def _dot(a, b):
    return jnp.dot(a, b, preferred_element_type=F32)


def _dot_nt(a, b):
    return lax.dot_general(a, b, (((1,), (1,)), ((), ())), preferred_element_type=F32)


def _dot_tn(a, b):
    return lax.dot_general(a, b, (((0,), (0,)), ((), ())), preferred_element_type=F32)


def _rms(x):
    r = lax.rsqrt(jnp.mean(x * x, axis=-1, keepdims=True) + EPS)
    return x * r, r


def _rms_bwd(x, g, dn):
    xh, r = _rms(x)
    dxh = dn * g
    dx = r * (dxh - xh * jnp.mean(dxh * xh, axis=-1, keepdims=True))
    return dx, jnp.sum(dn * xh, axis=0, keepdims=True)


def _split2(v):
    hi = v.astype(BF16)
    return hi, (v - hi.astype(F32)).astype(BF16)


def _split3(v):
    hi = v.astype(BF16)
    r1 = v - hi.astype(F32)
    mid = r1.astype(BF16)
    return hi, mid, (r1 - mid.astype(F32)).astype(BF16)


def _group_ones():
    r = lax.broadcasted_iota(jnp.int32, (LANES, LANES), 0) // HEAD_DIM
    c = lax.broadcasted_iota(jnp.int32, (LANES, LANES), 1) // HEAD_DIM
    return jnp.where(r == c, 1.0, 0.0).astype(BF16)


def _group_mean(v, ones):
    hi, lo = _split2(v)
    return (_dot(hi, ones) + _dot(lo, ones)) * (1.0 / HEAD_DIM)


def _row_tile(m):
    return 384 if m % 384 == 0 else LANES


def _tile(m, cap):
    return max(t for t in range(16, cap + 1, 16) if m % t == 0)


def _peer(k):
    x, y, c = lax.axis_index("x"), lax.axis_index("y"), lax.axis_index("c")
    px = 1 - x if k & 4 else x
    py = 1 - y if k & 2 else y
    pc = 1 - c if k & 1 else c
    return (px, py, pc), 4 * px + 2 * py + pc


def _exchange_body(src_ref, dst_ref, send_sems, recv_sems, local_sem, bcast):
    x, y, c = lax.axis_index("x"), lax.axis_index("y"), lax.axis_index("c")
    me = 4 * x + 2 * y + c
    mine = pltpu.make_async_copy(src_ref.at[0 if bcast else me], dst_ref.at[me], local_sem)
    mine.start()
    sends = []
    for k in range(1, N_DEV):
        dev, idx = _peer(k)
        cp = pltpu.make_async_remote_copy(
            src_ref=src_ref.at[0 if bcast else idx], dst_ref=dst_ref.at[me],
            send_sem=send_sems.at[k - 1], recv_sem=recv_sems.at[k - 1],
            device_id=dev, device_id_type=MESH)
        cp.start()
        sends.append(cp)
    for k in range(1, N_DEV):
        dev, idx = _peer(k)
        pltpu.make_async_remote_copy(
            src_ref=src_ref.at[0], dst_ref=dst_ref.at[idx],
            send_sem=send_sems.at[k - 1], recv_sem=recv_sems.at[k - 1],
            device_id=dev, device_id_type=MESH).wait_recv()
    for cp in sends:
        cp.wait_send()
    mine.wait()


class Rider:
    FIRST = (1, 2, 4, 6)
    RELAYED = (2, 4, 6)

    def __init__(self, srcs=(), bcast=True):
        self.srcs, self.bcast, self.n = list(srcs), bcast, len(srcs)

    def out_shapes(self):
        return [jax.ShapeDtypeStruct(((N_DEV,) + s.shape) if self.bcast else s.shape, s.dtype) for s in self.srcs]

    def specs(self):
        return [pl.BlockSpec(memory_space=pl.ANY)] * self.n

    def scratch(self):
        if not self.n:
            return []
        return [pltpu.SemaphoreType.DMA((self.n * (N_DEV - 1),)), pltpu.SemaphoreType.DMA((self.n * (N_DEV - 1),)),
                pltpu.SemaphoreType.DMA((self.n,))]

    @staticmethod
    def _copy(src, dst, a, pair, dev, send_sems, recv_sems):
        sem = a * (N_DEV - 1) + pair - 1
        return pltpu.make_async_remote_copy(src_ref=src, dst_ref=dst, send_sem=send_sems.at[sem],
                                            recv_sem=recv_sems.at[sem], device_id=dev, device_id_type=MESH)

    def _first(self):
        return self.FIRST if self.bcast else range(1, N_DEV)

    def _own(self, s, d, a, local_sems):
        me = 4 * lax.axis_index("x") + 2 * lax.axis_index("y") + lax.axis_index("c")
        return pltpu.make_async_copy(s if self.bcast else s.at[me], d.at[me], local_sems.at[a]), me

    def start(self, src_refs, dst_refs, send_sems, recv_sems, local_sems):
        for a, (s, d) in enumerate(zip(src_refs, dst_refs)):
            own, me = self._own(s, d, a, local_sems)
            own.start()
            for k in self._first():
                dev, idx = _peer(k)
                self._copy(s if self.bcast else s.at[idx], d.at[me], a, k, dev, send_sems, recv_sems).start()

    def relay(self, src_refs, dst_refs, send_sems, recv_sems, local_sems):
        if not self.bcast:
            return
        sibling, _ = _peer(1)
        for a, d in enumerate(dst_refs):
            for k in self.RELAYED:
                dev, idx = _peer(k)
                self._copy(d.at[idx], d.at[idx], a, k, dev, send_sems, recv_sems).wait_recv()
                self._copy(d.at[idx], d.at[idx], a, k + 1, sibling, send_sems, recv_sems).start()

    def wait(self, src_refs, dst_refs, send_sems, recv_sems, local_sems):
        sibling, _ = _peer(1)
        for a, (s, d) in enumerate(zip(src_refs, dst_refs)):
            own, me = self._own(s, d, a, local_sems)
            for k in range(1, N_DEV):
                if not (self.bcast and k in self.RELAYED):
                    dev, idx = _peer(k)
                    self._copy(d.at[idx], d.at[idx], a, k, dev, send_sems, recv_sems).wait_recv()
            for k in self._first():
                dev, idx = _peer(k)
                self._copy(s if self.bcast else s.at[idx], d.at[me], a, k, dev, send_sems, recv_sems).wait_send()
            if self.bcast:
                for k in self.RELAYED:
                    dev, idx = _peer(k)
                    self._copy(d.at[idx], d.at[idx], a, k + 1, sibling, send_sems, recv_sems).wait_send()
            own.wait()


def rider_call(core, name, grid, in_specs, out_specs, out_shape, scratch_shapes, args, rider=None):
    rider = rider or Rider()
    n_in, n_out, n_scr, nr = len(in_specs), len(out_specs), len(scratch_shapes), rider.n

    def body(*refs):
        ins, r_src = refs[:n_in], refs[n_in:n_in + nr]
        outs = refs[n_in + nr:n_in + nr + n_out]
        r_dst = refs[n_in + nr + n_out:n_in + 2 * nr + n_out]
        scr = refs[n_in + 2 * nr + n_out:n_in + 2 * nr + n_out + n_scr]
        sems = refs[n_in + 2 * nr + n_out + n_scr:]
        if nr:
            first, relay, last = True, True, True
            for ax, size in enumerate(grid):
                first = first & (pl.program_id(ax) == 0)
                relay = relay & (pl.program_id(ax) == (3 * size // 4 if ax == 0 else 0))
                last = last & (pl.program_id(ax) == size - 1)
            if not grid:
                rider.start(r_src, r_dst, *sems)
                rider.relay(r_src, r_dst, *sems)
            else:
                pl.when(first)(lambda: rider.start(r_src, r_dst, *sems))
                if rider.bcast:
                    pl.when(relay)(lambda: rider.relay(r_src, r_dst, *sems))
        core(*ins, *outs, *scr)
        if nr:
            if not grid:
                rider.wait(r_src, r_dst, *sems)
            else:
                pl.when(last)(lambda: rider.wait(r_src, r_dst, *sems))

    res = pl.pallas_call(
        body, name=name, grid=grid,
        in_specs=list(in_specs) + rider.specs(),
        out_specs=list(out_specs) + rider.specs(),
        out_shape=list(out_shape) + rider.out_shapes(),
        scratch_shapes=list(scratch_shapes) + rider.scratch(),
        compiler_params=_params(len(grid)),
    )(*args, *rider.srcs)
    return res[:n_out], res[n_out:]


def exchange_hbm(srcs, bcast, name):
    return rider_call(lambda: None, name, (), [], [], [], [], [], Rider(srcs, bcast))[1]


_HBM = pl.BlockSpec(memory_space=pltpu.HBM)
_SEM = pl.BlockSpec(memory_space=pltpu.SEMAPHORE)
_EFFECT = pltpu.CompilerParams(has_side_effects=pltpu.SideEffectType.DATAFLOW_SIDE_EFFECTING)


def scatter_start(src, name):
    def body(src_ref, land_ref, send_sems, recv_sems, src_thru, land_thru, token):
        me = 4 * lax.axis_index("x") + 2 * lax.axis_index("y") + lax.axis_index("c")
        for k in range(1, N_DEV):
            dev, idx = _peer(k)
            pltpu.make_async_remote_copy(src_ref=src_ref.at[idx], dst_ref=land_ref.at[me], send_sem=send_sems.at[k - 1],
                                         recv_sem=recv_sems.at[k - 1], device_id=dev, device_id_type=MESH).start()
        token[...] = jnp.zeros_like(token)

    return pl.pallas_call(
        body, name=name,
        out_shape=(pltpu.SemaphoreType.DMA((N_DEV - 1,)), pltpu.SemaphoreType.DMA((N_DEV - 1,)),
                   pltpu.HBM(src.shape, src.dtype), pltpu.HBM(src.shape, src.dtype), jax.ShapeDtypeStruct((8, LANES), F32)),
        in_specs=(_HBM, _HBM), out_specs=(_SEM, _SEM, _HBM, _HBM, pl.BlockSpec(memory_space=pltpu.VMEM)),
        input_output_aliases={0: 2, 1: 3}, compiler_params=_EFFECT,
    )(pltpu.with_memory_space_constraint(src, pltpu.HBM),
      pltpu.with_memory_space_constraint(lax.empty(src.shape, src.dtype), pltpu.HBM))


def scatter_wait(send_sems, recv_sems, src_thru, land_thru, after, name):
    n_after = len(after)

    def body(*refs):
        src_ref, land_ref, send_sems, recv_sems = refs[:4]
        for k in range(1, N_DEV):
            dev, idx = _peer(k)
            copy = pltpu.make_async_remote_copy(src_ref=src_ref.at[idx], dst_ref=land_ref.at[idx],
                                                send_sem=send_sems.at[k - 1], recv_sem=recv_sems.at[k - 1],
                                                device_id=dev, device_id_type=MESH)
            copy.wait_send()
            copy.wait_recv()

    return pl.pallas_call(
        body, name=name,
        out_shape=(pltpu.HBM(src_thru.shape, src_thru.dtype), pltpu.HBM(land_thru.shape, land_thru.dtype)),
        in_specs=(_HBM, _HBM, _SEM, _SEM) + (pl.BlockSpec(memory_space=pl.ANY),) * n_after, out_specs=(_HBM, _HBM),
        input_output_aliases={0: 0, 1: 1}, compiler_params=_EFFECT,
    )(src_thru, land_thru, send_sems, recv_sems, *after)


def allsum_small(vec, name):
    def body(src_ref, out_ref, dst_ref, send_sems, recv_sems, local_sem):
        _exchange_body(src_ref, dst_ref, send_sems, recv_sems, local_sem, True)
        acc = dst_ref[0]
        for j in range(1, N_DEV):
            acc = acc + dst_ref[j]
        out_ref[...] = acc

    return pl.pallas_call(
        body, name=name,
        out_shape=jax.ShapeDtypeStruct(vec.shape[1:], F32),
        in_specs=[pl.BlockSpec(memory_space=pltpu.VMEM)],
        out_specs=pl.BlockSpec(memory_space=pltpu.VMEM),
        scratch_shapes=[pltpu.VMEM((N_DEV,) + vec.shape[1:], F32),
                        pltpu.SemaphoreType.DMA((N_DEV - 1,)), pltpu.SemaphoreType.DMA((N_DEV - 1,)),
                        pltpu.SemaphoreType.DMA],
    )(vec)


def gather_small(vec, name):
    def body(src_ref, dst_ref, send_sems, recv_sems, local_sem):
        _exchange_body(src_ref, dst_ref, send_sems, recv_sems, local_sem, True)

    return pl.pallas_call(
        body, name=name,
        out_shape=jax.ShapeDtypeStruct((N_DEV,) + vec.shape[1:], F32),
        in_specs=[pl.BlockSpec(memory_space=pltpu.VMEM)],
        out_specs=pl.BlockSpec(memory_space=pltpu.VMEM),
        scratch_shapes=[pltpu.SemaphoreType.DMA((N_DEV - 1,)), pltpu.SemaphoreType.DMA((N_DEV - 1,)),
                        pltpu.SemaphoreType.DMA],
    )(vec)


FFN_FWD_ROWS = 1056
FFN_BWD_ROWS = 704
DW_ROWS = 1408

def ffn_fwd(h, g, w_in8, w_out4, name, rider=None):
    m, d = h.shape
    fb = w_in8.shape[1]
    tm = _tile(m, FFN_FWD_ROWS)

    def body(h_ref, g_ref, wg_ref, wu_ref, wo_ref, hn_ref, n_ref, a_ref, fg_ref, fu_ref, acc_ref):
        i = pl.program_id(1)

        @pl.when(i == 0)
        def _():
            xh, _ = _rms(h_ref[...])
            n_ref[...] = (xh * g_ref[...]).astype(BF16)
            acc_ref[...] = jnp.zeros_like(acc_ref)

        n = n_ref[...]
        gate = _dot_nt(n, wg_ref[0])
        up = _dot_nt(n, wu_ref[0])
        sg = jax.nn.sigmoid(gate)
        silu = gate * sg
        a = (silu * up).astype(BF16)
        a_ref[0] = a
        fg_ref[0] = (up * (sg * (1.0 + gate * (1.0 - sg)))).astype(BF16)
        fu_ref[0] = silu.astype(BF16)
        acc_ref[...] += _dot(a, wo_ref[0])

        @pl.when(i == 3)
        def _():
            hn_ref[...] = h_ref[...] + 0.5 * acc_ref[...]

    return rider_call(
        body, name, (m // tm, 4),
        in_specs=[pl.BlockSpec((tm, d), lambda r, i: (r, 0)),
                  pl.BlockSpec((1, d), lambda r, i: (0, 0)),
                  pl.BlockSpec((1, fb, d), lambda r, i: (i, 0, 0)),
                  pl.BlockSpec((1, fb, d), lambda r, i: (i + 4, 0, 0)),
                  pl.BlockSpec((1, fb, d), lambda r, i: (i, 0, 0))],
        out_specs=[pl.BlockSpec((tm, d), lambda r, i: (r, 0)),
                   pl.BlockSpec((tm, d), lambda r, i: (r, 0))] + [pl.BlockSpec((1, tm, fb), lambda r, i: (i, r, 0))] * 3,
        out_shape=[jax.ShapeDtypeStruct((m, d), F32), jax.ShapeDtypeStruct((m, d), BF16)]
                  + [jax.ShapeDtypeStruct((4, m, fb), BF16)] * 3,
        scratch_shapes=[pltpu.VMEM((tm, d), F32)],
        args=(h, g, w_in8, w_in8, w_out4), rider=rider)


def ffn_bwd(dh, h, g, f_gate, f_up, w_in8, w_out4, name, rider=None):
    m, d = h.shape
    fb = w_in8.shape[1]
    tm = _tile(m, FFN_BWD_ROWS)

    def body(dh_ref, h_ref, g_ref, fg_ref, fu_ref, wg_ref, wu_ref, wo_ref,
             dhin_ref, dg_ref, du_ref, dgn_ref, dhs_ref, acc_ref):
        r = pl.program_id(0)
        i = pl.program_id(1)

        @pl.when(i == 0)
        def _():
            dhs_ref[...] = (0.5 * dh_ref[...]).astype(BF16)
            acc_ref[...] = jnp.zeros_like(acc_ref)

        @pl.when((r == 0) & (i == 0))
        def _():
            dgn_ref[...] = jnp.zeros_like(dgn_ref)

        da = _dot_nt(dhs_ref[...], wo_ref[0])
        dub = (da * fu_ref[0]).astype(BF16)
        dgb = (da * fg_ref[0]).astype(BF16)
        dg_ref[0] = dgb
        du_ref[0] = dub
        acc_ref[...] += _dot(dgb, wg_ref[0]) + _dot(dub, wu_ref[0])

        @pl.when(i == 3)
        def _():
            dx, dgain = _rms_bwd(h_ref[...], g_ref[...], acc_ref[...])
            dgn_ref[...] += dgain
            dhin_ref[...] = dh_ref[...] + dx

    row = lambda r, i: (r, 0)
    blk = lambda r, i: (i, r, 0)
    return rider_call(
        body, name, (m // tm, 4),
        in_specs=[pl.BlockSpec((tm, d), row), pl.BlockSpec((tm, d), row),
                  pl.BlockSpec((1, d), lambda r, i: (0, 0)),
                  pl.BlockSpec((1, tm, fb), blk), pl.BlockSpec((1, tm, fb), blk),
                  pl.BlockSpec((1, fb, d), lambda r, i: (i, 0, 0)),
                  pl.BlockSpec((1, fb, d), lambda r, i: (i + 4, 0, 0)),
                  pl.BlockSpec((1, fb, d), lambda r, i: (i, 0, 0))],
        out_specs=[pl.BlockSpec((tm, d), row),
                   pl.BlockSpec((1, tm, fb), blk), pl.BlockSpec((1, tm, fb), blk),
                   pl.BlockSpec((1, d), lambda r, i: (0, 0)),
                   pl.BlockSpec((tm, d), row)],
        out_shape=[jax.ShapeDtypeStruct((m, d), F32),
                   jax.ShapeDtypeStruct((4, m, fb), BF16), jax.ShapeDtypeStruct((4, m, fb), BF16),
                   jax.ShapeDtypeStruct((1, d), F32), jax.ShapeDtypeStruct((m, d), BF16)],
        scratch_shapes=[pltpu.VMEM((tm, d), F32)],
        args=(dh, h, g, f_gate, f_up, w_in8, w_in8, w_out4), rider=rider)


def matmul_tn(x, y, name, tn=None, tk=None, x2=None, rider=None):
    if tk is not None:
        assert x.shape[0] == y.shape[0] == 1 and x2 is None and rider is None
        bk, m, kf = x.shape[2] // tk, x.shape[1], x.shape[2]
        tm = _tile(m, DW_ROWS)
        tn_ = y.shape[2] if tn is None else tn

        def tiled(x_ref, y_ref, o_ref, acc_ref):
            r = pl.program_id(2)

            @pl.when(r == 0)
            def _():
                acc_ref[...] = jnp.zeros_like(acc_ref)

            acc_ref[...] += _dot_tn(x_ref[0].astype(BF16), y_ref[0].astype(BF16))

            @pl.when(r == m // tm - 1)
            def _():
                o_ref[0] = acc_ref[...].astype(BF16)

        return pl.pallas_call(
            tiled, name=name, grid=(bk, y.shape[2] // tn_, m // tm),
            in_specs=[pl.BlockSpec((1, tm, tk), lambda i, j, r: (0, r, i)),
                      pl.BlockSpec((1, tm, tn_), lambda i, j, r: (0, r, j))],
            out_specs=pl.BlockSpec((1, tk, tn_), lambda i, j, r: (0, i, j)),
            out_shape=jax.ShapeDtypeStruct((1, kf, y.shape[2]), BF16),
            scratch_shapes=[pltpu.VMEM((tk, tn_), F32)],
            compiler_params=_params(3),
        )(x, y)
    bx, m, k = x.shape
    by, _, n = y.shape
    b = max(bx, by) * (2 if x2 is not None else 1)
    tm = _tile(m, DW_ROWS)
    tn = n if tn is None else tn
    nt = n // tn
    nr = m // tm

    def body(*refs):
        x_ref, y_ref = refs[0], refs[-3]
        o_ref, acc_ref = refs[-2], refs[-1]
        r = pl.program_id(2)

        @pl.when(r == 0)
        def _():
            acc_ref[...] = jnp.zeros_like(acc_ref)

        if x2 is None:
            acc_ref[...] += _dot_tn(x_ref[0].astype(BF16), y_ref[0].astype(BF16))
        else:
            @pl.when(pl.program_id(0) < bx)
            def _():
                acc_ref[...] += _dot_tn(x_ref[0].astype(BF16), y_ref[0].astype(BF16))

            @pl.when(pl.program_id(0) >= bx)
            def _():
                acc_ref[...] += _dot_tn(refs[1][0].astype(BF16), y_ref[0].astype(BF16))

        @pl.when(r == nr - 1)
        def _():
            o_ref[0] = acc_ref[...].astype(BF16)

    if x2 is None:
        x_specs = [pl.BlockSpec((1, tm, k), (lambda i, j, r: (i, r, 0)) if bx > 1 else (lambda i, j, r: (0, r, 0)))]
    else:
        x_specs = [pl.BlockSpec((1, tm, k), lambda i, j, r: (jnp.minimum(i, bx - 1), jnp.where(i < bx, r, nr - 1), 0)),
                   pl.BlockSpec((1, tm, k), lambda i, j, r: (jnp.maximum(i - bx, 0), jnp.where(i < bx, 0, r), 0))]
    y_map = (lambda i, j, r: (i, r, j)) if by > 1 else (lambda i, j, r: (0, r, j))
    (out,), carried = rider_call(
        body, name, (b, nt, nr),
        in_specs=x_specs + [pl.BlockSpec((1, tm, tn), y_map)],
        out_specs=[pl.BlockSpec((1, k, tn), lambda i, j, r: (i, 0, j))],
        out_shape=[jax.ShapeDtypeStruct((b, k, n), BF16)],
        scratch_shapes=[pltpu.VMEM((k, tn), F32)],
        args=[x] + ([x2] if x2 is not None else []) + [y], rider=rider)
    return (out, carried) if rider is not None else out


AUG = HEAD_DIM


class _Cols:
    def __init__(self, d):
        self.d = d
        self.ga, self.gb = 0, d
        self.qa, self.ka, self.va = 2 * d, 2 * d + 1024, 2 * d + 2048
        self.qb = 2 * d + 3072
        self.kb, self.vb, self.fa = self.qb + 512, self.qb + 640, self.qb + 768
        self.np = self.qb + 1024


def mixer_proj(h, g, wp, name):
    m, d = h.shape
    npad = wp.shape[0]
    tm = _row_tile(m)

    def body(h_ref, g_ref, w_ref, n_ref, p_ref):
        xh, _ = _rms(h_ref[...])
        n = (xh * g_ref[...]).astype(BF16)
        n_ref[...] = n
        p_ref[...] = _dot_nt(n, w_ref[...])

    return pl.pallas_call(
        body, name=name, grid=(m // tm,),
        in_specs=[pl.BlockSpec((tm, d), lambda r: (r, 0)), pl.BlockSpec((1, d), lambda r: (0, 0)),
                  pl.BlockSpec((npad, d), lambda r: (0, 0))],
        out_specs=[pl.BlockSpec((tm, d), lambda r: (r, 0)), pl.BlockSpec((tm, npad), lambda r: (r, 0))],
        out_shape=[jax.ShapeDtypeStruct((m, d), BF16), jax.ShapeDtypeStruct((m, npad), F32)],
        compiler_params=_params(1),
    )(h, g, wp)


def _head_norm(x, gain, ones):
    outs = []
    for b in range(x.shape[1] // LANES):
        xb = x[:, b * LANES:(b + 1) * LANES]
        r = lax.rsqrt(_group_mean(xb * xb, ones) + EPS)
        outs.append(xb * r * gain[:, b * LANES:(b + 1) * LANES])
    return outs


def _head_norm_bwd(x, gain, dn, ones):
    dxs, dgs = [], []
    for b in range(x.shape[1] // LANES):
        sl = slice(b * LANES, (b + 1) * LANES)
        xb, dnb = x[:, sl], dn[:, sl]
        r = lax.rsqrt(_group_mean(xb * xb, ones) + EPS)
        xh = xb * r
        dxh = dnb * gain[:, sl]
        dxs.append(r * (dxh - xh * _group_mean(dxh * xh, ones)))
        dgs.append(jnp.sum(dnb * xh, axis=0, keepdims=True))
    return dxs, dgs


def _lane_col(v, lane_iota, idx):
    return jnp.sum(jnp.where(lane_iota == idx, v, 0.0), axis=1, keepdims=True)


def _aug(base, lane, vals):
    for i, v in enumerate(vals):
        base = jnp.where(lane == AUG + i, v, base)
    return base


def qk_post(proj, gains, fbias, cols, name):
    m = proj.shape[0]
    tm = _row_tile(m)
    gqa, gka, gqb, gkb = gains

    def body(qa_ref, ka_ref, va_ref, qb_ref, kb_ref, vb_ref, fa_ref, gqa_ref, gka_ref, gqb_ref, gkb_ref, fb_ref,
             qf_o, kf_o, vf_o, kt_o, vt_o, qb_o, kb_o, vb_o, carry_ref):
        r0 = pl.program_id(0)

        @pl.when(r0 == 0)
        def _():
            carry_ref[...] = jnp.zeros_like(carry_ref)

        z = fa_ref[...] + fb_ref[...]
        logf = jnp.minimum(z, 0.0) - jnp.log(1.0 + jnp.exp(-jnp.abs(z)))
        rr = lax.broadcasted_iota(jnp.int32, (tm, tm), 0)
        cc = lax.broadcasted_iota(jnp.int32, (tm, tm), 1)
        tril = jnp.where(cc <= rr, 1.0, 0.0).astype(BF16)
        p0, p1, p2 = _split3(logf)
        c = _dot(tril, p0) + _dot(tril, p1) + _dot(tril, p2) + carry_ref[...]
        carry_ref[...] += jnp.sum(logf, axis=0, keepdims=True)

        lane = lax.broadcasted_iota(jnp.int32, (tm, LANES), 1)
        is_pad = (r0 * tm + lax.broadcasted_iota(jnp.int32, (tm, 1), 0)) < PAD_FRONT
        ones = jnp.ones((LANES, LANES), BF16)
        for hd in range(8):
            sl = slice(hd * LANES, (hd + 1) * LANES)
            ch = _lane_col(c, lane, hd)
            ct = [p.astype(F32) for p in _split3(ch)]
            cs = [p.astype(F32) for p in _split3(-jnp.where(is_pad, -NEG, ch))]
            xq = qa_ref[:, sl]
            qn = xq * lax.rsqrt(_group_mean(xq * xq, ones) + EPS) * gqa_ref[...]
            qf_o[:, sl] = _aug(qn, lane, ct + [1.0, 1.0, 1.0]).astype(BF16)
            xk = ka_ref[:, sl]
            kn = xk * lax.rsqrt(_group_mean(xk * xk, ones) + EPS) * gka_ref[...]
            kf = _aug(kn, lane, [1.0, 1.0, 1.0] + cs)
            vf = _aug(va_ref[:, sl], lane, [1.0, 1.0, 1.0])
            kf_o[:, sl] = kf.astype(BF16)
            vf_o[:, sl] = vf.astype(BF16)
            kt_o[sl, :] = kf.T.astype(BF16)
            vt_o[sl, :] = vf.T.astype(BF16)

        gones = _group_ones()
        for src, gn, dst in ((qb_ref, gqb_ref, qb_o), (kb_ref, gkb_ref, kb_o)):
            for b, blk in enumerate(_head_norm(src[...], gn[...], gones)):
                dst[:, b * LANES:(b + 1) * LANES] = blk.astype(BF16)
        vb_o[...] = vb_ref[...].astype(BF16)

    w1024 = lambda off: pl.BlockSpec((tm, 1024), lambda r, o=off // 1024: (r, o))
    w512 = lambda off: pl.BlockSpec((tm, 512), lambda r, o=off // 512: (r, o))
    w128 = lambda off: pl.BlockSpec((tm, LANES), lambda r, o=off // LANES: (r, o))
    vec = lambda w: pl.BlockSpec((1, w), lambda r: (0, 0))
    row = lambda w: pl.BlockSpec((tm, w), lambda r: (r, 0))
    return pl.pallas_call(
        body, name=name, grid=(m // tm,),
        in_specs=[w1024(cols.qa), w1024(cols.ka), w1024(cols.va), w512(cols.qb), w128(cols.kb), w128(cols.vb),
                  w128(cols.fa), vec(LANES), vec(LANES), vec(512), vec(LANES), vec(LANES)],
        out_specs=[row(1024), row(1024), row(1024)] + [pl.BlockSpec((1024, tm), lambda r: (0, r))] * 2
                  + [row(512), row(LANES), row(LANES)],
        out_shape=[jax.ShapeDtypeStruct((m, 1024), BF16)] * 3 + [jax.ShapeDtypeStruct((1024, m), BF16)] * 2
                  + [jax.ShapeDtypeStruct((m, 512), BF16)] + [jax.ShapeDtypeStruct((m, LANES), BF16)] * 2,
        scratch_shapes=[pltpu.VMEM((1, LANES), F32)],
        compiler_params=_params(1),
    )(proj, proj, proj, proj, proj, proj, proj, gqa, gka, gqb, gkb, fbias)


def qk_post_bwd(proj, gains, fbias, dqf, dkf, dvf, dqb, dkb, dvb, dc, dga, dgb, cols, name):
    m = proj.shape[0]
    d = cols.d
    tm = _row_tile(m)
    nt = m // tm
    gqa, gka, gqb, gkb = gains

    def body(qa_ref, ka_ref, qb_ref, kb_ref, fa_ref, gqa_ref, gka_ref, gqb_ref, gkb_ref, fb_ref,
             dqf_ref, dkf_ref, dvf_ref, dqb_ref, dkb_ref, dvb_ref, dc_ref, dga_ref, dgb_ref,
             dp_o, ggqa_o, ggka_o, ggqb_o, ggkb_o, gfb_o, carry_ref):
        @pl.when(pl.program_id(0) == 0)
        def _():
            carry_ref[...] = jnp.zeros_like(carry_ref)
            for o in (ggqa_o, ggka_o, ggqb_o, ggkb_o, gfb_o):
                o[...] = jnp.zeros_like(o)

        dp_o[:, cols.ga:cols.ga + d] = dga_ref[...].astype(BF16)
        dp_o[:, cols.gb:cols.gb + d] = dgb_ref[...].astype(BF16)
        dp_o[:, cols.fa + LANES:cols.np] = jnp.zeros((tm, cols.np - cols.fa - LANES), BF16)
        lane = lax.broadcasted_iota(jnp.int32, (tm, LANES), 1)
        data = lane < HEAD_DIM
        ones = jnp.ones((LANES, LANES), BF16)
        for hd in range(8):
            sl = slice(hd * LANES, (hd + 1) * LANES)
            for src, gn, dn_ref, off, gout in ((qa_ref, gqa_ref, dqf_ref, cols.qa, ggqa_o),
                                               (ka_ref, gka_ref, dkf_ref, cols.ka, ggka_o)):
                x = src[:, sl]
                dn = jnp.where(data, dn_ref[:, sl], 0.0)
                r = lax.rsqrt(_group_mean(x * x, ones) + EPS)
                xh = x * r
                dxh = dn * gn[...]
                dp_o[:, off + hd * LANES:off + (hd + 1) * LANES] = (
                    r * (dxh - xh * _group_mean(dxh * xh, ones))).astype(BF16)
                gout[...] += jnp.sum(dn * xh, axis=0, keepdims=True)
            dp_o[:, cols.va + hd * LANES:cols.va + (hd + 1) * LANES] = jnp.where(data, dvf_ref[:, sl], 0.0).astype(BF16)
        dp_o[:, cols.vb:cols.vb + LANES] = dvb_ref[...].astype(BF16)
        gones = _group_ones()
        for src, gn, dn, off, gout in ((qb_ref, gqb_ref, dqb_ref, cols.qb, ggqb_o),
                                       (kb_ref, gkb_ref, dkb_ref, cols.kb, ggkb_o)):
            dxs, dgs = _head_norm_bwd(src[...], gn[...], dn[...], gones)
            for b, (dx, dg) in enumerate(zip(dxs, dgs)):
                dp_o[:, off + b * LANES:off + (b + 1) * LANES] = dx.astype(BF16)
                gout[:, b * LANES:(b + 1) * LANES] += dg
        dcv = dc_ref[...]
        rr = lax.broadcasted_iota(jnp.int32, (tm, tm), 0)
        cc = lax.broadcasted_iota(jnp.int32, (tm, tm), 1)
        triu = jnp.where(cc >= rr, 1.0, 0.0).astype(BF16)
        p0, p1, p2 = _split3(dcv)
        dlogf = _dot(triu, p0) + _dot(triu, p1) + _dot(triu, p2) + carry_ref[...]
        carry_ref[...] += jnp.sum(dcv, axis=0, keepdims=True)
        z = fa_ref[...] + fb_ref[...]
        row = (nt - 1 - pl.program_id(0)) * tm + lax.broadcasted_iota(jnp.int32, (tm, LANES), 0)
        dfa = jnp.where(row >= PAD_FRONT, dlogf * jax.nn.sigmoid(-z), 0.0)
        dp_o[:, cols.fa:cols.fa + LANES] = dfa.astype(BF16)
        gfb_o[...] += jnp.sum(dfa, axis=0, keepdims=True)

    rev = lambda r: nt - 1 - r
    w1024 = lambda off: pl.BlockSpec((tm, 1024), lambda r, o=off // 1024: (rev(r), o))
    w512 = lambda off: pl.BlockSpec((tm, 512), lambda r, o=off // 512: (rev(r), o))
    w128 = lambda off: pl.BlockSpec((tm, LANES), lambda r, o=off // LANES: (rev(r), o))
    vec = lambda w: pl.BlockSpec((1, w), lambda r: (0, 0))
    row = lambda w: pl.BlockSpec((tm, w), lambda r: (rev(r), 0))
    return pl.pallas_call(
        body, name=name, grid=(nt,),
        in_specs=[w1024(cols.qa), w1024(cols.ka), w512(cols.qb), w128(cols.kb), w128(cols.fa),
                  vec(LANES), vec(LANES), vec(512), vec(LANES), vec(LANES),
                  row(1024), row(1024), row(1024), row(512), row(LANES), row(LANES), row(LANES), row(d), row(d)],
        out_specs=[row(cols.np), vec(LANES), vec(LANES), vec(512), vec(LANES), vec(LANES)],
        out_shape=[jax.ShapeDtypeStruct((m, cols.np), BF16), jax.ShapeDtypeStruct((1, LANES), F32),
                   jax.ShapeDtypeStruct((1, LANES), F32), jax.ShapeDtypeStruct((1, 512), F32),
                   jax.ShapeDtypeStruct((1, LANES), F32), jax.ShapeDtypeStruct((1, LANES), F32)],
        scratch_shapes=[pltpu.VMEM((1, LANES), F32)],
        compiler_params=_params(1),
    )(proj, proj, proj, proj, proj, gqa, gka, gqb, gkb, fbias, dqf, dkf, dvf, dqb, dkb, dvb, dc, dga, dgb)


def dproj_bwd(dh, h, g, dproj, wp, name):
    m, d = h.shape
    npad = wp.shape[0]
    tm = _row_tile(m)

    def body(dh_ref, h_ref, g_ref, dp_ref, w_ref, dhin_ref, dgn_ref):
        @pl.when(pl.program_id(0) == 0)
        def _():
            dgn_ref[...] = jnp.zeros_like(dgn_ref)

        dn = _dot(dp_ref[...], w_ref[...])
        dx, dgain = _rms_bwd(h_ref[...], g_ref[...], dn)
        dgn_ref[...] += dgain
        dhin_ref[...] = dh_ref[...] + dx

    row = lambda w: pl.BlockSpec((tm, w), lambda r: (r, 0))
    return pl.pallas_call(
        body, name=name, grid=(m // tm,),
        in_specs=[row(d), row(d), pl.BlockSpec((1, d), lambda r: (0, 0)), row(npad),
                  pl.BlockSpec((npad, d), lambda r: (0, 0))],
        out_specs=[row(d), pl.BlockSpec((1, d), lambda r: (0, 0))],
        out_shape=[jax.ShapeDtypeStruct((m, d), F32), jax.ShapeDtypeStruct((1, d), F32)],
        compiler_params=_params(1),
    )(dh, h, g, dproj, wp)


def _causal_t(t):
    return lax.broadcasted_iota(jnp.int32, (t, t), 0) <= lax.broadcasted_iota(jnp.int32, (t, t), 1)


HEADS_PER_STEP = 4


def fox_fwd(qf, kf, vt, name, rider=None):
    m = qf.shape[0]
    t = _row_tile(m)
    nq = m // t
    hp = HEADS_PER_STEP
    w = hp * LANES

    def body(q_ref, k_ref, vt_ref, o_ref, lse_ref, acc_ref, m_ref, p_ref, a_ref):
        qi = pl.program_id(1)
        acc_ref[...] = jnp.zeros_like(acc_ref)
        m_ref[...] = jnp.full_like(m_ref, NEG)

        def scores(ki, slot, mask):
            off = pl.multiple_of(ki * t, t)
            for e in range(hp):
                sl = slice(e * LANES, (e + 1) * LANES)
                s = _dot_nt(k_ref[pl.ds(off, t), sl], q_ref[:, sl])
                if mask is not None:
                    s = jnp.where(mask, s, NEG)
                m_old = m_ref[e]
                m_new = jnp.maximum(m_old, jnp.max(s, axis=0, keepdims=True))
                p_ref[slot, e] = jnp.exp(s - m_new).astype(BF16)
                a_ref[slot, e] = jnp.exp(m_old - m_new)
                m_ref[e] = m_new

        def values(ki, slot):
            off = pl.multiple_of(ki * t, t)
            for e in range(hp):
                sl = slice(e * LANES, (e + 1) * LANES)
                acc_ref[e] = acc_ref[e] * a_ref[slot, e] + _dot(vt_ref[sl, pl.ds(off, t)], p_ref[slot, e])

        causal = _causal_t(t)
        scores(0, 0, causal | (jnp.full((t, t), qi, jnp.int32) > 0))

        def step(ki, carry):
            values(ki - 1, (ki - 1) % 2)
            scores(ki, ki % 2, None)
            return carry

        lax.fori_loop(1, qi, step, 0)

        @pl.when(qi >= 1)
        def _():
            values(qi - 1, (qi - 1) % 2)
            scores(qi, qi % 2, causal)

        values(qi, qi % 2)
        row = lax.broadcasted_iota(jnp.int32, (LANES, t), 0)
        for e in range(hp):
            l = jnp.max(acc_ref[e, AUG:AUG + 8, :], axis=0, keepdims=True)
            o_ref[:, e * LANES:(e + 1) * LANES] = jnp.where(row < HEAD_DIM, acc_ref[e] * (1.0 / l), 0.0).T
            lse_ref[e] = m_ref[e] + jnp.log(l)

    return rider_call(
        body, name, (8 // hp, nq),
        in_specs=[pl.BlockSpec((t, w), lambda hd, i: (i, hd)),
                  pl.BlockSpec((m, w), lambda hd, i: (0, hd)),
                  pl.BlockSpec((w, m), lambda hd, i: (hd, 0))],
        out_specs=[pl.BlockSpec((t, w), lambda hd, i: (i, hd)),
                   pl.BlockSpec((hp, 1, t), lambda hd, i: (hd, 0, i))],
        out_shape=[jax.ShapeDtypeStruct((m, 8 * LANES), F32), jax.ShapeDtypeStruct((8, 1, m), F32)],
        scratch_shapes=[pltpu.VMEM((hp, LANES, t), F32), pltpu.VMEM((hp, 1, t), F32),
                        pltpu.VMEM((2, hp, t, t), BF16), pltpu.VMEM((2, hp, 1, t), F32)],
        args=(qf, kf, vt), rider=rider)


def fox_bwd(qf, kf, vf, kt, dof, lse, delta, name, rider=None):
    m = qf.shape[0]
    t = _row_tile(m)
    nq = m // t
    hp = HEADS_PER_STEP
    w = hp * LANES

    def body(k_ref, v_ref, kt_ref, q_ref, do_ref, lse_ref, delta_ref, dk_ref, dv_ref, dq_ref, dck_ref, dcq_ref,
             dka_ref, dva_ref, dqt_ref):
        ki = pl.program_id(1)

        @pl.when(ki == 0)
        def _():
            dqt_ref[...] = jnp.zeros_like(dqt_ref)

        dka_ref[...] = jnp.zeros_like(dka_ref)
        dva_ref[...] = jnp.zeros_like(dva_ref)

        def tile(qi, diagonal):
            off = pl.multiple_of(qi * t, t)
            for e in range(hp):
                sl = slice(e * LANES, (e + 1) * LANES)
                q = q_ref[pl.ds(off, t), sl]
                do = do_ref[pl.ds(off, t), sl]
                s = _dot_nt(k_ref[:, sl], q)
                if diagonal:
                    s = jnp.where(_causal_t(t), s, NEG)
                p = jnp.exp(s - lse_ref[e, :, pl.ds(off, t)])
                ds = (p * (_dot_nt(v_ref[:, sl], do) - delta_ref[e, :, pl.ds(off, t)])).astype(BF16)
                dva_ref[:, sl] += _dot(p.astype(BF16), do)
                dka_ref[:, sl] += _dot(ds, q)
                dqt_ref[sl, pl.ds(off, t)] += _dot(kt_ref[sl, :], ds)

        def step(qi, carry):
            tile(qi, False)
            return carry

        tile(ki, True)
        lax.fori_loop(ki + 1, nq, step, 0)
        dk_ref[...] = dka_ref[...]
        dv_ref[...] = dva_ref[...]
        row8 = lax.broadcasted_iota(jnp.int32, (8, 1), 0)
        for e in range(hp):
            slab = dka_ref[:, e * LANES:(e + 1) * LANES].T[AUG:AUG + 8, :]
            dck_ref[e] = -jnp.sum(jnp.where(row8 == 3, slab, 0.0), axis=0, keepdims=True)

        @pl.when(ki == nq - 1)
        def _():
            for e in range(hp):
                sl = slice(e * LANES, (e + 1) * LANES)
                slab = dqt_ref[e * LANES + AUG:e * LANES + AUG + 8, :]
                dcq_ref[e] = jnp.sum(jnp.where(row8 == 0, slab, 0.0), axis=0, keepdims=True)
                for j in range(nq):
                    dq_ref[j * t:(j + 1) * t, sl] = dqt_ref[sl, j * t:(j + 1) * t].T

    tile_spec = pl.BlockSpec((t, w), lambda hd, i: (i, hd))
    full = pl.BlockSpec((m, w), lambda hd, i: (0, hd))
    stat = pl.BlockSpec((hp, 1, m), lambda hd, i: (hd, 0, 0))
    (dkf, dvf, dqf, dck, dcq), carried = rider_call(
        body, name, (8 // hp, nq),
        in_specs=[tile_spec, tile_spec, pl.BlockSpec((w, t), lambda hd, i: (hd, i)), full, full, stat, stat],
        out_specs=[tile_spec, tile_spec, full, pl.BlockSpec((hp, 1, t), lambda hd, i: (hd, 0, i)), stat],
        out_shape=[jax.ShapeDtypeStruct((m, 8 * LANES), F32), jax.ShapeDtypeStruct((m, 8 * LANES), F32),
                   jax.ShapeDtypeStruct((m, 8 * LANES), F32), jax.ShapeDtypeStruct((8, 1, m), F32),
                   jax.ShapeDtypeStruct((8, 1, m), F32)],
        scratch_shapes=[pltpu.VMEM((t, w), F32), pltpu.VMEM((t, w), F32), pltpu.VMEM((w, m), F32)],
        args=(kf, vf, kt, qf, dof, lse, delta), rider=rider)
    return (dkf, dvf, dqf, dcq + dck), carried


def _bucket_ids():
    def bucket(dist):
        n = np.maximum(dist, 0)
        max_exact = N_BUCKETS // 2
        nf = np.maximum(n, 1).astype(np.float32)
        large = max_exact + (np.log(nf / max_exact) / math.log(MAX_DISTANCE / max_exact)
                             * (N_BUCKETS - max_exact)).astype(np.int32)
        return np.where(n < max_exact, n, np.minimum(large, N_BUCKETS - 1))

    tl = np.arange(LANES)[:, None]
    sl = np.arange(LANES)[None, :]
    prev = bucket(LANES + tl - sl)
    cur = bucket(tl - sl)
    meta = np.full((LANES, LANES), N_BUCKETS - 1)
    return np.concatenate([prev, cur, meta], axis=1).astype(np.int32)


def bias_build(table, name):
    ids = jnp.asarray(_bucket_ids())

    def body(t_ref, id_ref, o_ref):
        idv = id_ref[...]
        for h in range(8):
            acc = jnp.zeros((LANES, 3 * LANES), F32)
            for b in range(N_BUCKETS):
                acc = jnp.where(idv == b, t_ref[b, h], acc)
            o_ref[h] = acc

    return pl.pallas_call(
        body, name=name,
        in_specs=[pl.BlockSpec(memory_space=pltpu.SMEM), pl.BlockSpec(memory_space=pltpu.VMEM)],
        out_specs=pl.BlockSpec(memory_space=pltpu.VMEM),
        out_shape=jax.ShapeDtypeStruct((8, LANES, 3 * LANES), F32),
    )(table, ids)


def bias_reduce(dbias, name):
    ids = jnp.asarray(_bucket_ids())

    def body(d_ref, id_ref, o_ref):
        idv = id_ref[...]
        rr = lax.broadcasted_iota(jnp.int32, (N_BUCKETS, LANES), 0)
        cc = lax.broadcasted_iota(jnp.int32, (N_BUCKETS, LANES), 1)
        acc = jnp.zeros((N_BUCKETS, LANES), F32)
        for h in range(8):
            dv = d_ref[h]
            for b in range(N_BUCKETS):
                val = jnp.sum(jnp.where(idv == b, dv, 0.0), keepdims=True)
                acc = jnp.where((rr == b) & (cc == h), val, acc)
        o_ref[...] = acc

    return pl.pallas_call(
        body, name=name,
        in_specs=[pl.BlockSpec(memory_space=pltpu.VMEM), pl.BlockSpec(memory_space=pltpu.VMEM)],
        out_specs=pl.BlockSpec(memory_space=pltpu.VMEM),
        out_shape=jax.ShapeDtypeStruct((N_BUCKETS, LANES), F32),
    )(dbias, ids)


def _swa_penalty(n):
    shape = (LANES, 3 * LANES)
    tl = lax.broadcasted_iota(jnp.int32, shape, 0)
    col = lax.broadcasted_iota(jnp.int32, shape, 1)
    sl = col & (LANES - 1)
    nv = jnp.full(shape, n, jnp.int32)
    is_meta = sl >= PAD_FRONT
    prev = (col < LANES) & (sl > tl) & (nv >= 1) & ((nv >= 2) | is_meta)
    cur = (col >= LANES) & (col < 2 * LANES) & (sl <= tl) & ((nv >= 1) | is_meta)
    meta = (col >= 2 * LANES) & is_meta & ((nv >= 2) | ((nv == 1) & (sl <= tl)))
    return jnp.where(prev | cur | meta, 0.0, NEG)


def _swa_keys(ref, n):
    off_prev = pl.multiple_of(jnp.maximum(n - 1, 0) * LANES, LANES)
    off_cur = pl.multiple_of(n * LANES, LANES)
    return jnp.concatenate([ref[pl.ds(off_prev, LANES), :], ref[pl.ds(off_cur, LANES), :], ref[0:LANES, :]], axis=0)


def swa_fwd(q, k, v, bias, sinks, name):
    m = q.shape[0]

    def body(q_ref, k_ref, v_ref, bias_ref, sink_ref, o_ref, lse_ref):
        n = pl.program_id(0)
        lane1 = lax.broadcasted_iota(jnp.int32, (1, LANES), 1)
        lane_t = lax.broadcasted_iota(jnp.int32, (LANES, LANES), 1)
        in_head = [lane1 < HEAD_DIM, lane1 >= HEAD_DIM]
        kall = _swa_keys(k_ref, n)
        vall = _swa_keys(v_ref, n)
        vs = [jnp.where(in_head[g], vall, jnp.zeros_like(vall)) for g in (0, 1)]
        penalty = _swa_penalty(n)
        lse = jnp.zeros((LANES, LANES), F32)
        for b in range(4):
            qb = q_ref[:, b * LANES:(b + 1) * LANES]
            ob = jnp.zeros((LANES, LANES), F32)
            for g in (0, 1):
                h = 4 * g + b
                qe = jnp.where(in_head[g], qb, jnp.zeros_like(qb))
                s = _dot_nt(qe, kall) + bias_ref[h] + penalty
                sink = sink_ref[h]
                mx = jnp.maximum(jnp.max(s, axis=1, keepdims=True), sink)
                p = jnp.exp(s - mx)
                den = jnp.sum(p, axis=1, keepdims=True) + jnp.exp(sink - mx)
                ob = ob + _dot((p / den).astype(BF16), vs[g])
                lse = jnp.where(lane_t == h, mx + jnp.log(den), lse)
            o_ref[:, b * LANES:(b + 1) * LANES] = ob
        lse_ref[...] = lse

    return pl.pallas_call(
        body, name=name, grid=(m // LANES,),
        in_specs=[pl.BlockSpec((LANES, 512), lambda n: (n, 0)),
                  pl.BlockSpec((m, LANES), lambda n: (0, 0)), pl.BlockSpec((m, LANES), lambda n: (0, 0)),
                  pl.BlockSpec((8, LANES, 3 * LANES), lambda n: (0, 0, 0)),
                  pl.BlockSpec(memory_space=pltpu.SMEM)],
        out_specs=[pl.BlockSpec((LANES, 512), lambda n: (n, 0)), pl.BlockSpec((LANES, LANES), lambda n: (n, 0))],
        out_shape=[jax.ShapeDtypeStruct((m, 512), F32), jax.ShapeDtypeStruct((m, LANES), F32)],
        compiler_params=_params(1),
    )(q, k, v, bias, sinks)


def swa_bwd(q, k, v, bias, sinks, o, lse, do, name):
    m = q.shape[0]

    def body(q_ref, do_ref, o_ref, lse_ref, k_ref, v_ref, bias_ref, sink_ref,
             dq_ref, dk_ref, dv_ref, dbias_ref, dsink_ref):
        n = pl.program_id(0)

        @pl.when(n == 0)
        def _():
            for r in (dk_ref, dv_ref, dbias_ref, dsink_ref):
                r[...] = jnp.zeros_like(r)

        lane1 = lax.broadcasted_iota(jnp.int32, (1, LANES), 1)
        lane_t = lax.broadcasted_iota(jnp.int32, (LANES, LANES), 1)
        in_head = [lane1 < HEAD_DIM, lane1 >= HEAD_DIM]
        off_prev = pl.multiple_of(jnp.maximum(n - 1, 0) * LANES, LANES)
        off_cur = pl.multiple_of(n * LANES, LANES)
        kall = _swa_keys(k_ref, n)
        vall = _swa_keys(v_ref, n)
        ks = [jnp.where(in_head[g], kall, jnp.zeros_like(kall)) for g in (0, 1)]
        penalty = _swa_penalty(n)
        lsev = lse_ref[...]
        dsink = dsink_ref[...]
        dkall = jnp.zeros((3 * LANES, LANES), F32)
        dvall = jnp.zeros((3 * LANES, LANES), F32)
        for b in range(4):
            sl = slice(b * LANES, (b + 1) * LANES)
            qb = q_ref[:, sl]
            dob = do_ref[:, sl]
            prod = dob * o_ref[:, sl]
            dqb = jnp.zeros((LANES, LANES), F32)
            for g in (0, 1):
                h = 4 * g + b
                qe = jnp.where(in_head[g], qb, jnp.zeros_like(qb))
                doe = jnp.where(in_head[g], dob, 0.0).astype(BF16)
                delta = jnp.sum(jnp.where(in_head[g], prod, 0.0), axis=1, keepdims=True)
                lse_h = _lane_col(lsev, lane_t, h)
                s = _dot_nt(qe, kall) + bias_ref[h] + penalty
                p = jnp.exp(s - lse_h)
                ds = p * (_dot_nt(doe, vall) - delta)
                dbias_ref[h] += ds
                sink_part = jnp.sum(-jnp.exp(sink_ref[h] - lse_h) * delta, keepdims=True)
                dsink = jnp.where(lane1 == h, dsink + sink_part, dsink)
                dsb = ds.astype(BF16)
                dqb = dqb + _dot(dsb, ks[g])
                dkall = dkall + _dot_tn(dsb, qe)
                dvall = dvall + _dot_tn(p.astype(BF16), doe)
            dq_ref[:, sl] = dqb
        dsink_ref[...] = dsink
        for ref, val in ((dk_ref, dkall), (dv_ref, dvall)):
            ref[pl.ds(off_prev, LANES), :] += val[0:LANES]
            ref[pl.ds(off_cur, LANES), :] += val[LANES:2 * LANES]
            ref[0:LANES, :] += val[2 * LANES:3 * LANES]

    blk = pl.BlockSpec((LANES, 512), lambda n: (n, 0))
    full = pl.BlockSpec((m, LANES), lambda n: (0, 0))
    return pl.pallas_call(
        body, name=name, grid=(m // LANES,),
        in_specs=[blk, blk, blk, pl.BlockSpec((LANES, LANES), lambda n: (n, 0)), full, full,
                  pl.BlockSpec((8, LANES, 3 * LANES), lambda n: (0, 0, 0)),
                  pl.BlockSpec(memory_space=pltpu.SMEM)],
        out_specs=[blk, full, full, pl.BlockSpec((8, LANES, 3 * LANES), lambda n: (0, 0, 0)),
                   pl.BlockSpec((1, LANES), lambda n: (0, 0))],
        out_shape=[jax.ShapeDtypeStruct((m, 512), F32), jax.ShapeDtypeStruct((m, LANES), F32),
                   jax.ShapeDtypeStruct((m, LANES), F32), jax.ShapeDtypeStruct((8, LANES, 3 * LANES), F32),
                   jax.ShapeDtypeStruct((1, LANES), F32)],
        compiler_params=_params(1),
    )(q, do, o, lse, k, v, bias, sinks)


def branch_out(h, o_fox, o_swa, proj, wbf, wbs, wo, cols, name):
    m, d = h.shape
    tm = _row_tile(m)

    def body(h_ref, of_ref, os_ref, ga_ref, gb_ref, wbf_ref, wbs_ref, wo_ref, hn_ref):
        tf = _dot(of_ref[...].astype(BF16), wbf_ref[...])
        ts = _dot(os_ref[...].astype(BF16), wbs_ref[...])
        y = jax.nn.sigmoid(ga_ref[...]) * tf + jax.nn.sigmoid(gb_ref[...]) * ts
        hn_ref[...] = h_ref[...] + _dot(y.astype(BF16), wo_ref[...])

    row = lambda w, o=0: pl.BlockSpec((tm, w), lambda r, o=o: (r, o))
    res = lambda a: pl.BlockSpec(a.shape, lambda r: (0, 0))
    return pl.pallas_call(
        body, name=name, grid=(m // tm,),
        in_specs=[row(d), row(1024), row(512), row(d, cols.ga // d), row(d, cols.gb // d), res(wbf), res(wbs), res(wo)],
        out_specs=row(d),
        out_shape=jax.ShapeDtypeStruct((m, d), F32),
        compiler_params=_params(1),
    )(h, o_fox, o_swa, proj, proj, wbf, wbs, wo)


def branch_out_bwd(dh, o_fox, o_swa, proj, wbf, wbs, wo, cols, name):
    m, d = dh.shape
    tm = _row_tile(m)

    def body(dh_ref, of_ref, os_ref, ga_ref, gb_ref, wbf_ref, wbs_ref, wo_ref,
             y_ref, dtf_ref, dts_ref, dga_ref, dgb_ref, dof_ref, dos_ref, delta_ref):
        dy = _dot_nt(dh_ref[...].astype(BF16), wo_ref[...])
        tf = _dot(of_ref[...].astype(BF16), wbf_ref[...])
        ts = _dot(os_ref[...].astype(BF16), wbs_ref[...])
        sa = jax.nn.sigmoid(ga_ref[...])
        sb = jax.nn.sigmoid(gb_ref[...])
        y_ref[...] = (sa * tf + sb * ts).astype(BF16)
        dtf = (dy * sa).astype(BF16)
        dts = (dy * sb).astype(BF16)
        dtf_ref[...] = dtf
        dts_ref[...] = dts
        dga_ref[...] = (dy * tf * sa * (1.0 - sa)).astype(BF16)
        dgb_ref[...] = (dy * ts * sb * (1.0 - sb)).astype(BF16)
        dof = _dot_nt(dtf, wbf_ref[...])
        dof_ref[...] = dof.astype(BF16)
        dos_ref[...] = _dot_nt(dts, wbs_ref[...])
        lane = lax.broadcasted_iota(jnp.int32, (tm, LANES), 1)
        delta = jnp.zeros((tm, LANES), F32)
        for hd in range(8):
            sl = slice(hd * LANES, (hd + 1) * LANES)
            delta = jnp.where(lane == hd, jnp.sum(dof[:, sl] * of_ref[:, sl], axis=1, keepdims=True), delta)
        delta_ref[...] = delta

    row = lambda w, o=0: pl.BlockSpec((tm, w), lambda r, o=o: (r, o))
    res = lambda a: pl.BlockSpec(a.shape, lambda r: (0, 0))
    return pl.pallas_call(
        body, name=name, grid=(m // tm,),
        in_specs=[row(d), row(1024), row(512), row(d, cols.ga // d), row(d, cols.gb // d), res(wbf), res(wbs), res(wo)],
        out_specs=[row(d)] * 5 + [row(1024), row(512), row(LANES)],
        out_shape=[jax.ShapeDtypeStruct((m, d), BF16)] * 5 + [jax.ShapeDtypeStruct((m, 1024), BF16),
                   jax.ShapeDtypeStruct((m, 512), F32), jax.ShapeDtypeStruct((m, LANES), F32)],
        compiler_params=_params(1),
    )(dh, o_fox, o_swa, proj, proj, wbf, wbs, wo)


def loss_head(h, target, name):
    m, d = h.shape

    def body(h_ref, t_ref, dh_ref, loss_ref):
        n = pl.program_id(0)

        @pl.when(n == 0)
        def _():
            loss_ref[...] = jnp.zeros_like(loss_ref)
            dh_ref[...] = jnp.zeros_like(dh_ref)

        @pl.when(n > 0)
        def _():
            err = h_ref[...] - t_ref[...]
            dh_ref[...] = err * (1.0 / d)
            loss_ref[...] += jnp.sum(err * err, keepdims=True) * (0.5 / d)

    return pl.pallas_call(
        body, name=name, grid=(m // LANES,),
        in_specs=[pl.BlockSpec((LANES, d), lambda n: (n, 0)),
                  pl.BlockSpec((LANES, d), lambda n: (jnp.maximum(n - 1, 0), 0))],
        out_specs=[pl.BlockSpec((LANES, d), lambda n: (n, 0)), pl.BlockSpec((8, LANES), lambda n: (0, 0))],
        out_shape=[jax.ShapeDtypeStruct((m, d), F32), jax.ShapeDtypeStruct((8, LANES), F32)],
        compiler_params=_params(1),
    )(h, target)


def _adamw_math(w, g, m, v):
    m = ADAM_B1 * m + (1.0 - ADAM_B1) * g
    v = ADAM_B2 * v + (1.0 - ADAM_B2) * (g * g)
    m_hat = m / (1.0 - ADAM_B1 ** ADAM_STEP)
    v_hat = v / (1.0 - ADAM_B2 ** ADAM_STEP)
    delta = -ADAM_LR * (m_hat / (jnp.sqrt(v_hat) + ADAM_EPS) + ADAM_WD * w)
    return delta, m, v


def adamw_sum(parts, w, m, v, name, after=None):
    n_layers, a, b = w.shape
    ta = next(t for t in (256, 176, 128, a) if a % t == 0)
    nr = a // ta

    def body(*refs):
        p_refs = refs[:n_layers]
        w_ref, m_ref, v_ref = refs[n_layers:n_layers + 3]
        g_o, d_o, m_o, v_o = refs[-4:]
        for l in range(n_layers):
            @pl.when(pl.program_id(0) == l)
            def _(l=l):
                g = p_refs[l][0].astype(F32)
                for j in range(1, N_DEV):
                    g = g + p_refs[l][j].astype(F32)
                g_o[0] = g
                d_o[0], m_o[0], v_o[0] = _adamw_math(w_ref[0], g, m_ref[0], v_ref[0])

    def part_spec(l):
        return pl.BlockSpec((N_DEV, ta, b), lambda i, r, l=l: (0, jnp.where(i == l, r, jnp.where(i < l, 0, nr - 1)), 0))

    row = pl.BlockSpec((1, ta, b), lambda i, r: (i, r, 0))
    return pl.pallas_call(
        body, name=name, grid=(n_layers, nr),
        in_specs=[part_spec(l) for l in range(n_layers)] + [row, row, row]
                 + ([pl.BlockSpec(memory_space=pl.ANY)] if after is not None else []),
        out_specs=[row] * 4,
        out_shape=[jax.ShapeDtypeStruct(w.shape, F32)] * 4,
        compiler_params=_params(2),
    )(*parts, w, m, v, *([after] if after is not None else []))


def adamw_sum_cols(parts, w, m, v, name, after=None):
    n_layers = len(parts)
    a, b = parts[0].shape[1:]
    tc = 512 if b % 512 == 0 else b
    nc = b // tc

    def body(*refs):
        p_refs = refs[:n_layers]
        w_ref, m_ref, v_ref = refs[n_layers:n_layers + 3]
        g_o, d_o, m_o, v_o = refs[-4:]
        for l in range(n_layers):
            @pl.when(pl.program_id(0) == l)
            def _(l=l):
                g = p_refs[l][0].astype(F32)
                for j in range(1, N_DEV):
                    g = g + p_refs[l][j].astype(F32)
                g_o[...] = g
                d_o[...], m_o[...], v_o[...] = _adamw_math(w_ref[...], g, m_ref[...], v_ref[...])

    def part_spec(l):
        return pl.BlockSpec((N_DEV, a, tc), lambda i, c, l=l: (0, 0, jnp.where(i == l, c, jnp.where(i < l, 0, nc - 1))))

    col = pl.BlockSpec((a, tc), lambda i, c: (0, i * nc + c))
    return pl.pallas_call(
        body, name=name, grid=(n_layers, nc),
        in_specs=[part_spec(l) for l in range(n_layers)] + [col, col, col]
                 + ([pl.BlockSpec(memory_space=pl.ANY)] if after is not None else []),
        out_specs=[col] * 4,
        out_shape=[jax.ShapeDtypeStruct(w.shape, F32)] * 4,
        compiler_params=_params(2),
    )(*parts, w, m, v, *([after] if after is not None else []))


def adamw_small(g, w, m, v, name):
    def body(g_ref, w_ref, m_ref, v_ref, d_o, m_o, v_o):
        d_o[...], m_o[...], v_o[...] = _adamw_math(w_ref[...], g_ref[...], m_ref[...], v_ref[...])

    spec = pl.BlockSpec(memory_space=pltpu.VMEM)
    return pl.pallas_call(
        body, name=name, in_specs=[spec] * 4, out_specs=[spec] * 3,
        out_shape=[jax.ShapeDtypeStruct(w.shape, F32)] * 3,
    )(g, w, m, v)


BIG = ("ffn1_w_in", "ffn1_w_out", "w_in", "w_branch_fox", "w_branch_swa", "w_out", "ffn2_w_in", "ffn2_w_out")
SMALL = ("rel_bias_table", "ffn1_norm", "mix_norm", "forget_bias", "fox_q_norm", "fox_k_norm",
         "swa_q_norm", "swa_k_norm", "swa_sinks", "ffn2_norm")
WEIGHTS = ("meta_tokens", "rel_bias_table", "ffn1_norm", "ffn1_w_in", "ffn1_w_out", "mix_norm", "w_in",
           "forget_bias", "fox_q_norm", "fox_k_norm", "swa_q_norm", "swa_k_norm", "swa_sinks", "w_branch_fox",
           "w_branch_swa", "w_out", "ffn2_norm", "ffn2_w_in", "ffn2_w_out")


def _pack(arrs, width, row_multiple, dtype):
    lead = arrs[0].shape[:-1]
    flat = jnp.concatenate([a.astype(dtype) for a in arrs], axis=-1)
    n = flat.shape[-1]
    rows = -(-n // width)
    rows = -(-rows // row_multiple) * row_multiple
    flat = jnp.pad(flat, [(0, 0)] * len(lead) + [(0, rows * width - n)])
    return flat.reshape(lead + (rows, width))


def _unpack(flat, shapes):
    flat = flat.reshape(-1)
    out, off = [], 0
    for s in shapes:
        n = int(np.prod(s))
        out.append(flat[off:off + n].reshape(s))
        off += n
    return out


def _swa_head_order():
    return [4 * (j % 2) + j // 2 for j in range(8)]


def _permute_heads(a, axis, inverse=False):
    order = _swa_head_order()
    if inverse:
        order = [order.index(hd) for hd in range(8)]
    parts = [lax.slice_in_dim(a, hd * HEAD_DIM, (hd + 1) * HEAD_DIM, axis=axis) for hd in order]
    return jnp.concatenate(parts, axis=axis)


def _pad_heads(a):
    return jnp.pad(a.reshape(8, HEAD_DIM, -1), ((0, 0), (0, LANES - HEAD_DIM), (0, 0))).reshape(8 * LANES, -1)


def _unpad_heads(a):
    return a.reshape(8, LANES, -1)[:, :HEAD_DIM].reshape(8 * HEAD_DIM, -1)


def _swa_rows(a, inverse=False):
    shape = (4, 2) if inverse else (2, 4)
    return a.reshape(shape + (HEAD_DIM, -1)).transpose(1, 0, 2, 3).reshape(a.shape)


def _w_in_rows(d):
    return np.cumsum([0, 512, 512, 512, 8, 512, 128, 128, d, d])


def _reorder_w_in(wt, cols):
    o = _w_in_rows(cols.d)
    qa, ka, va, fa, qb, kb, vb, ga, gb = [wt[o[i]:o[i + 1]] for i in range(9)]
    zeros = jnp.zeros((cols.np - cols.fa - 8, wt.shape[1]), wt.dtype)
    return jnp.concatenate([ga, gb, _pad_heads(qa), _pad_heads(ka), _pad_heads(va), _swa_rows(qb), kb, vb, fa, zeros],
                           axis=0)


def _restore_w_in(wpt, cols, width):
    seg = lambda off, n: wpt[off:off + n]
    rows = jnp.concatenate([_unpad_heads(seg(cols.qa, 1024)), _unpad_heads(seg(cols.ka, 1024)),
                            _unpad_heads(seg(cols.va, 1024)), seg(cols.fa, 8), _swa_rows(seg(cols.qb, 512), True),
                            seg(cols.kb, 128), seg(cols.vb, 128), seg(cols.ga, cols.d), seg(cols.gb, cols.d)], axis=0)
    return rows.reshape(N_DEV, width, -1)


def _lane_pad(v):
    return jnp.pad(v, ((0, 0), (0, LANES - v.shape[1])))


def kernel(x, meta_tokens, rel_bias_table, ffn1_norm, ffn1_w_in, ffn1_w_out, mix_norm, w_in, forget_bias, fox_q_norm, fox_k_norm, swa_q_norm, swa_k_norm, swa_sinks, w_branch_fox, w_branch_swa, w_out, ffn2_norm, ffn2_w_in, ffn2_w_out, loss_target, m_meta_tokens, m_rel_bias_table, m_ffn1_norm, m_ffn1_w_in, m_ffn1_w_out, m_mix_norm, m_w_in, m_forget_bias, m_fox_q_norm, m_fox_k_norm, m_swa_q_norm, m_swa_k_norm, m_swa_sinks, m_w_branch_fox, m_w_branch_swa, m_w_out, m_ffn2_norm, m_ffn2_w_in, m_ffn2_w_out, v_meta_tokens, v_rel_bias_table, v_ffn1_norm, v_ffn1_w_in, v_ffn1_w_out, v_mix_norm, v_w_in, v_forget_bias, v_fox_q_norm, v_fox_k_norm, v_swa_q_norm, v_swa_k_norm, v_swa_sinks, v_w_branch_fox, v_w_branch_swa, v_w_out, v_ffn2_norm, v_ffn2_w_in, v_ffn2_w_out):
    args = dict(locals())
    wts = {n: args[n] for n in WEIGHTS}
    mom1 = {n: args["m_" + n] for n in WEIGHTS}
    mom2 = {n: args["v_" + n] for n in WEIGHTS}

    seq, d = x.shape[1], x.shape[2]
    m_rows = seq + LANES
    depth = ffn1_norm.shape[0]
    fb = ffn1_w_in.shape[2]
    fo = ffn1_w_out.shape[1]
    din_shard = w_in.shape[2]
    cols = _Cols(d)
    scale = HEAD_DIM ** -0.5
    dev = 4 * lax.axis_index("x") + 2 * lax.axis_index("y") + lax.axis_index("c")

    groups = {"ffn1": ("ffn1_w_in", "ffn1_w_out"), "mix": ("w_in", "w_branch_fox", "w_branch_swa", "w_out"),
              "ffn2": ("ffn2_w_in", "ffn2_w_out"), "ffn1_in": ("ffn1_w_in",), "ffn1_out": ("ffn1_w_out",),
              "ffn2_in": ("ffn2_w_in",), "ffn2_out": ("ffn2_w_out",)}
    flipped = ("ffn1_w_in", "ffn2_w_in")
    for n in flipped:
        wts[n], mom1[n], mom2[n] = (jnp.swapaxes(a, 1, 2) for a in (wts[n], mom1[n], mom2[n]))
    to_rows = lambda a: jnp.transpose(a, (2, 0, 1)).reshape(din_shard, depth * d)
    from_rows = lambda a: jnp.transpose(a.reshape(din_shard, depth, d), (1, 2, 0))
    wts["w_in"], mom1["w_in"], mom2["w_in"] = (to_rows(a) for a in (wts["w_in"], mom1["w_in"], mom2["w_in"]))
    shard = {n: wts[n].astype(BF16) for n in BIG}
    w_in_rows = shard.pop("w_in")
    shard["w_in"] = [w_in_rows[:, l * d:(l + 1) * d] for l in range(depth)]
    full, parts, gw = {}, {}, {}

    def keys_of(stages):
        return [(n, l) for g, l in stages if l < depth for n in groups[g]]

    def gather_rider(stages):
        return Rider([shard[n][l] for n, l in keys_of(stages)], True)

    def scatter_rider(stages):
        return Rider([gw[k] for k in keys_of(stages)], False)

    def ffn_weights(tag, l):
        return full[tag + "_w_in", l], full[tag + "_w_out", l].reshape(4, fb, d)

    def mixer_weights(l):
        wp = _reorder_w_in(full["w_in", l].reshape(N_DEV * din_shard, d), cols)
        wbf = jnp.concatenate([full["w_branch_fox", l][j] for j in range(N_DEV)], axis=1)
        wbf = jnp.pad(wbf.reshape(8, HEAD_DIM, d), ((0, 0), (0, LANES - HEAD_DIM), (0, 0))).reshape(8 * LANES, d)
        wbs = _permute_heads(jnp.concatenate([full["w_branch_swa", l][j] for j in range(N_DEV)], axis=1), 0)
        return wp, wbf, wbs, full["w_out", l].reshape(d, d)

    full.update(zip(keys_of([("ffn1", 0)]), exchange_hbm(gather_rider([("ffn1", 0)]).srcs, True, "gather_first")))
    meta_all = gather_small(meta_tokens.reshape(1, N_META, -1), "gather_meta")
    meta_full = meta_all.transpose(1, 0, 2).reshape(N_META, d)
    tile8 = lambda g, s=1.0: jnp.tile(g.reshape(1, HEAD_DIM) * s, (1, 8))
    tile2 = lambda g: jnp.tile(g.reshape(1, HEAD_DIM), (1, 2))
    data_lanes = lambda g, s=1.0: _lane_pad(g.reshape(1, HEAD_DIM) * s)
    bias = bias_build(rel_bias_table, "swa_bias")

    first = jnp.concatenate([jnp.zeros((PAD_FRONT, d), F32), meta_full], axis=0)
    h = jnp.concatenate([first, x[0]], axis=0)
    saved, lw = [], []
    for l in range(depth):
        s, w = {"h0": h}, {}
        w["ffn1_in"], w["ffn1_out"] = ffn_weights("ffn1", l)
        stages = [("mix", l)]
        (h, s["n1"], s["a1"], s["fg1"], s["fu1"]), got = ffn_fwd(h, ffn1_norm[l:l + 1], w["ffn1_in"], w["ffn1_out"],
                                                          f"ffn1_fwd_{l}", gather_rider(stages))
        full.update(zip(keys_of(stages), got))
        s["h1"] = h
        w["wp"], w["wbf"], w["wbs"], w["wo"] = mixer_weights(l)
        s["nm"], s["proj"] = mixer_proj(h, mix_norm[l:l + 1], w["wp"], f"mixer_proj_{l}")
        s["gains"] = (data_lanes(fox_q_norm[l], scale), data_lanes(fox_k_norm[l]), tile8(swa_q_norm[l], scale),
                      tile2(swa_k_norm[l]))
        s["fbias"] = _lane_pad(forget_bias[l:l + 1])
        qf, kf, vf, kt, vt, qb, kb, vb = qk_post(s["proj"], s["gains"], s["fbias"], cols, f"qk_post_{l}")
        s.update(qf=qf, kf=kf, vf=vf, kt=kt, qb=qb, kb=kb, vb=vb)
        stages = [("ffn2", l)]
        (s["o_fox"], s["lse_fox"]), got = fox_fwd(qf, kf, vt, f"fox_fwd_{l}", gather_rider(stages))
        full.update(zip(keys_of(stages), got))
        s["o_swa"], s["lse_swa"] = swa_fwd(qb, kb, vb, bias, swa_sinks[l], f"swa_fwd_{l}")
        h = branch_out(h, s["o_fox"], s["o_swa"], s["proj"], w["wbf"], w["wbs"], w["wo"], cols, f"branch_out_{l}")
        s["h2"] = h
        w["ffn2_in"], w["ffn2_out"] = ffn_weights("ffn2", l)
        stages = [("ffn1", l + 1)]
        (h, s["n2"], s["a2"], s["fg2"], s["fu2"]), got = ffn_fwd(h, ffn2_norm[l:l + 1], w["ffn2_in"], w["ffn2_out"],
                                                          f"ffn2_fwd_{l}", gather_rider(stages))
        full.update(zip(keys_of(stages), got))
        saved.append(s)
        lw.append(w)

    dh, loss_part = loss_head(h, loss_target[0], "loss_head")

    gs = {n: [None] * depth for n in SMALL}
    dbias_total = None
    for l in reversed(range(depth)):
        w, s = lw[l], saved[l]

        def ffn_back(dh, tag, hin, norm, n_in, a, f_gate, f_up, stages):
            (dh_in, dg, du, dgn, dhs), got = ffn_bwd(dh, hin, norm, f_gate, f_up, w[tag + "_in"], w[tag + "_out"],
                                                     f"{tag}_bwd_{l}", scatter_rider(stages))
            parts.update(zip(keys_of(stages), got))
            gw[tag + "_w_out", l] = matmul_tn(a, dhs[None], f"{tag}_dwo_{l}").reshape(N_DEV, fo, d)
            stages = [(tag + "_out", l)]
            gw[tag + "_w_in", l], got = matmul_tn(dg, n_in[None], f"{tag}_dwi_{l}", x2=du,
                                                  rider=scatter_rider(stages))
            parts.update(zip(keys_of(stages), got))
            return dh_in, dgn

        dh, gs["ffn2_norm"][l] = ffn_back(dh, "ffn2", s["h2"], ffn2_norm[l:l + 1], s["n2"], s["a2"], s["fg2"],
                                          s["fu2"], [("ffn1_in", l + 1)])

        y, dtf, dts, dga, dgb, dof, dos, delta = branch_out_bwd(dh, s["o_fox"], s["o_swa"], s["proj"], w["wbf"],
                                                                w["wbs"], w["wo"], cols, f"branch_out_bwd_{l}")
        gw["w_out", l] = matmul_tn(y[None], dh[None], f"dw_out_{l}").reshape(N_DEV, d // N_DEV, d)
        to_shards = lambda a: a.reshape(512, N_DEV, d // N_DEV).transpose(1, 0, 2)
        gw["w_branch_fox", l] = to_shards(matmul_tn(s["o_fox"][None], dtf[None], f"dw_branch_fox_{l}")[0]
                                          .reshape(8, LANES, d)[:, :HEAD_DIM].reshape(512, d))
        gw["w_branch_swa", l] = to_shards(_permute_heads(
            matmul_tn(s["o_swa"][None], dts[None], f"dw_branch_swa_{l}")[0], 0, inverse=True))

        stages = [("ffn2_in", l)]
        (dkf, dvf, dqf, dc_rows), got = fox_bwd(s["qf"], s["kf"], s["vf"], s["kt"], dof, s["lse_fox"],
                                         delta[:, :8].T.reshape(8, 1, m_rows), f"fox_bwd_{l}", scatter_rider(stages))
        parts.update(zip(keys_of(stages), got))
        dc = _lane_pad(dc_rows.reshape(8, m_rows).T)
        dqb, dkb, dvb, dbias, dsink = swa_bwd(s["qb"], s["kb"], s["vb"], bias, swa_sinks[l], s["o_swa"], s["lse_swa"],
                                              dos, f"swa_bwd_{l}")
        dbias_total = dbias if dbias_total is None else dbias_total + dbias
        gs["swa_sinks"][l] = dsink[0, :8]
        dproj, ggqa, ggka, ggqb, ggkb, gfb = qk_post_bwd(s["proj"], s["gains"], s["fbias"], dqf, dkf, dvf, dqb, dkb,
                                                         dvb, dc, dga, dgb, cols, f"qk_post_bwd_{l}")
        gs["fox_q_norm"][l] = ggqa[0, :HEAD_DIM] * scale
        gs["fox_k_norm"][l] = ggka[0, :HEAD_DIM]
        gs["swa_q_norm"][l] = ggqb.reshape(8, HEAD_DIM).sum(0) * scale
        gs["swa_k_norm"][l] = ggkb.reshape(2, HEAD_DIM).sum(0)
        gs["forget_bias"][l] = gfb[0, :8]
        dwp = matmul_tn(dproj[None], s["nm"][None], f"dw_in_{l}", tk=1024 if cols.np % 1024 == 0 else cols.np)[0]
        gw["w_in", l] = _restore_w_in(dwp, cols, din_shard)
        dh, gs["mix_norm"][l] = dproj_bwd(dh, s["h1"], mix_norm[l:l + 1], dproj, w["wp"], f"dproj_bwd_{l}")

        dh, gs["ffn1_norm"][l] = ffn_back(dh, "ffn1", s["h0"], ffn1_norm[l:l + 1], s["n1"], s["a1"], s["fg1"],
                                          s["fu1"], [("mix", l)])

    grad_x = dh[LANES:][None]
    dmeta = dh[PAD_FRONT:LANES]
    dtable = bias_reduce(dbias_total, "swa_dbias")[:, :8]

    last = ("ffn1_w_in", 0)
    send_sems, recv_sems, src_thru, land_thru, token = scatter_start(gw[last], "scatter_last_start")
    big_out = [{}, {}, {}, {}]
    for n in BIG:
        if n != last[0]:
            update = adamw_sum_cols if n == "w_in" else adamw_sum
            outs = update([parts[n, l] for l in range(depth)], wts[n], mom1[n], mom2[n], f"adamw_{n}", after=token)
            for k in range(4):
                big_out[k][n] = outs[k]
    sent, landed = scatter_wait(send_sems, recv_sems, src_thru, land_thru,
                                [big_out[1][n] for n in BIG if n != last[0]], "scatter_last_wait")
    parts[last] = lax.dynamic_update_slice_in_dim(landed, lax.dynamic_slice_in_dim(sent, dev, 1, axis=0), dev, axis=0)
    outs = adamw_sum([parts[last[0], l] for l in range(depth)], wts[last[0]], mom1[last[0]], mom2[last[0]],
                     f"adamw_{last[0]}")
    for k in range(4):
        big_out[k][last[0]] = outs[k]

    small_g = {n: (jnp.stack(gs[n]) if n != "rel_bias_table" else None) for n in SMALL}
    small_g["rel_bias_table"] = dtable
    pieces = [loss_part[0:1, 0:1].reshape(1, 1)] + [small_g[n].reshape(1, -1) for n in SMALL] + [dmeta.reshape(1, -1)]
    small_shapes = [(1,)] + [wts[n].shape for n in SMALL] + [(N_META, d)]
    total = allsum_small(_pack(pieces, LANES, 8, F32), "allsum_small")
    summed = _unpack(total, small_shapes)
    loss = summed[0][0]
    g_small = dict(zip(SMALL, summed[1:1 + len(SMALL)]))
    g_meta = lax.dynamic_slice_in_dim(summed[-1], dev * (d // N_DEV), d // N_DEV, axis=1)
    names = SMALL + ("meta_tokens",)
    g_small["meta_tokens"] = g_meta
    pk = lambda src: _pack([src[n].reshape(1, -1) for n in names], LANES, 8, F32)[0]
    small_out = [dict(zip(names, _unpack(o, [wts[n].shape for n in names])))
                 for o in adamw_small(pk(g_small), pk(wts), pk(mom1), pk(mom2), "adamw_small")]

    for out in big_out:
        for n in flipped:
            out[n] = jnp.swapaxes(out[n], 1, 2)
        out["w_in"] = from_rows(out["w_in"])
    grads = {**big_out[0], **g_small}
    delta = {**big_out[1], **small_out[0]}
    new_m = {**big_out[2], **small_out[1]}
    new_v = {**big_out[3], **small_out[2]}
    return (loss, grad_x, *[grads[n] for n in WEIGHTS], *[delta[n] for n in WEIGHTS],
            *[new_m[n] for n in WEIGHTS], *[new_v[n] for n in WEIGHTS])
```

```python
import math

import numpy as np
import jax
import jax.numpy as jnp
from jax import lax
from jax.experimental import pallas as pl
from jax.experimental.pallas import tpu as pltpu

F32 = jnp.float32
BF16 = jnp.bfloat16
EPS = 1e-6
NEG = -1e30
HEAD_DIM = 64
LANES = 128
N_META = 16
PAD_FRONT = LANES - N_META
N_BUCKETS = 32
MAX_DISTANCE = 128
N_DEV = 8
ADAM_LR, ADAM_B1, ADAM_B2, ADAM_EPS, ADAM_WD, ADAM_STEP = 0.001, 0.9, 0.999, 1e-08, 0.01, 10
VMEM_LIMIT = 56 * 1024 * 1024
MESH = pl.DeviceIdType.MESH


def _params(n_grid):
    return pltpu.CompilerParams(dimension_semantics=("arbitrary",) * n_grid,
                                vmem_limit_bytes=VMEM_LIMIT)


def _dot(a, b):
    return jnp.dot(a, b, preferred_element_type=F32)


def _dot_nt(a, b):
    return lax.dot_general(a, b, (((1,), (1,)), ((), ())), preferred_element_type=F32)


def _dot_tn(a, b):
    return lax.dot_general(a, b, (((0,), (0,)), ((), ())), preferred_element_type=F32)


def _rms(x):
    r = lax.rsqrt(jnp.mean(x * x, axis=-1, keepdims=True) + EPS)
    return x * r, r


def _rms_bwd(x, g, dn):
    xh, r = _rms(x)
    dxh = dn * g
    dx = r * (dxh - xh * jnp.mean(dxh * xh, axis=-1, keepdims=True))
    return dx, jnp.sum(dn * xh, axis=0, keepdims=True)


def _split2(v):
    hi = v.astype(BF16)
    return hi, (v - hi.astype(F32)).astype(BF16)


def _split3(v):
    hi = v.astype(BF16)
    r1 = v - hi.astype(F32)
    mid = r1.astype(BF16)
    return hi, mid, (r1 - mid.astype(F32)).astype(BF16)


def _group_ones():
    r = lax.broadcasted_iota(jnp.int32, (LANES, LANES), 0) // HEAD_DIM
    c = lax.broadcasted_iota(jnp.int32, (LANES, LANES), 1) // HEAD_DIM
    return jnp.where(r == c, 1.0, 0.0).astype(BF16)


def _group_mean(v, ones):
    hi, lo = _split2(v)
    return (_dot(hi, ones) + _dot(lo, ones)) * (1.0 / HEAD_DIM)


def _row_tile(m):
    return 384 if m % 384 == 0 else LANES


def _tile(m, cap):
    return max(t for t in range(16, cap + 1, 16) if m % t == 0)


def _peer(k):
    x, y, c = lax.axis_index("x"), lax.axis_index("y"), lax.axis_index("c")
    px = 1 - x if k & 4 else x
    py = 1 - y if k & 2 else y
    pc = 1 - c if k & 1 else c
    return (px, py, pc), 4 * px + 2 * py + pc


def _exchange_body(src_ref, dst_ref, send_sems, recv_sems, local_sem, bcast):
    x, y, c = lax.axis_index("x"), lax.axis_index("y"), lax.axis_index("c")
    me = 4 * x + 2 * y + c
    mine = pltpu.make_async_copy(src_ref.at[0 if bcast else me], dst_ref.at[me], local_sem)
    mine.start()
    sends = []
    for k in range(1, N_DEV):
        dev, idx = _peer(k)
        cp = pltpu.make_async_remote_copy(
            src_ref=src_ref.at[0 if bcast else idx], dst_ref=dst_ref.at[me],
            send_sem=send_sems.at[k - 1], recv_sem=recv_sems.at[k - 1],
            device_id=dev, device_id_type=MESH)
        cp.start()
        sends.append(cp)
    for k in range(1, N_DEV):
        dev, idx = _peer(k)
        pltpu.make_async_remote_copy(
            src_ref=src_ref.at[0], dst_ref=dst_ref.at[idx],
            send_sem=send_sems.at[k - 1], recv_sem=recv_sems.at[k - 1],
            device_id=dev, device_id_type=MESH).wait_recv()
    for cp in sends:
        cp.wait_send()
    mine.wait()


class Rider:
    FIRST = (1, 2, 4, 6)
    RELAYED = (2, 4, 6)

    def __init__(self, srcs=(), bcast=True):
        self.srcs, self.bcast, self.n = list(srcs), bcast, len(srcs)

    def out_shapes(self):
        return [jax.ShapeDtypeStruct(((N_DEV,) + s.shape) if self.bcast else s.shape, s.dtype) for s in self.srcs]

    def specs(self):
        return [pl.BlockSpec(memory_space=pl.ANY)] * self.n

    def scratch(self):
        if not self.n:
            return []
        return [pltpu.SemaphoreType.DMA((self.n * (N_DEV - 1),)), pltpu.SemaphoreType.DMA((self.n * (N_DEV - 1),)),
                pltpu.SemaphoreType.DMA((self.n,))]

    @staticmethod
    def _copy(src, dst, a, pair, dev, send_sems, recv_sems):
        sem = a * (N_DEV - 1) + pair - 1
        return pltpu.make_async_remote_copy(src_ref=src, dst_ref=dst, send_sem=send_sems.at[sem],
                                            recv_sem=recv_sems.at[sem], device_id=dev, device_id_type=MESH)

    def _first(self):
        return self.FIRST if self.bcast else range(1, N_DEV)

    def _own(self, s, d, a, local_sems):
        me = 4 * lax.axis_index("x") + 2 * lax.axis_index("y") + lax.axis_index("c")
        return pltpu.make_async_copy(s if self.bcast else s.at[me], d.at[me], local_sems.at[a]), me

    def start(self, src_refs, dst_refs, send_sems, recv_sems, local_sems):
        for a, (s, d) in enumerate(zip(src_refs, dst_refs)):
            own, me = self._own(s, d, a, local_sems)
            own.start()
            for k in self._first():
                dev, idx = _peer(k)
                self._copy(s if self.bcast else s.at[idx], d.at[me], a, k, dev, send_sems, recv_sems).start()

    def relay(self, src_refs, dst_refs, send_sems, recv_sems, local_sems):
        if not self.bcast:
            return
        sibling, _ = _peer(1)
        for a, d in enumerate(dst_refs):
            for k in self.RELAYED:
                dev, idx = _peer(k)
                self._copy(d.at[idx], d.at[idx], a, k, dev, send_sems, recv_sems).wait_recv()
                self._copy(d.at[idx], d.at[idx], a, k + 1, sibling, send_sems, recv_sems).start()

    def wait(self, src_refs, dst_refs, send_sems, recv_sems, local_sems):
        sibling, _ = _peer(1)
        for a, (s, d) in enumerate(zip(src_refs, dst_refs)):
            own, me = self._own(s, d, a, local_sems)
            for k in range(1, N_DEV):
                if not (self.bcast and k in self.RELAYED):
                    dev, idx = _peer(k)
                    self._copy(d.at[idx], d.at[idx], a, k, dev, send_sems, recv_sems).wait_recv()
            for k in self._first():
                dev, idx = _peer(k)
                self._copy(s if self.bcast else s.at[idx], d.at[me], a, k, dev, send_sems, recv_sems).wait_send()
            if self.bcast:
                for k in self.RELAYED:
                    dev, idx = _peer(k)
                    self._copy(d.at[idx], d.at[idx], a, k + 1, sibling, send_sems, recv_sems).wait_send()
            own.wait()


def rider_call(core, name, grid, in_specs, out_specs, out_shape, scratch_shapes, args, rider=None):
    rider = rider or Rider()
    n_in, n_out, n_scr, nr = len(in_specs), len(out_specs), len(scratch_shapes), rider.n

    def body(*refs):
        ins, r_src = refs[:n_in], refs[n_in:n_in + nr]
        outs = refs[n_in + nr:n_in + nr + n_out]
        r_dst = refs[n_in + nr + n_out:n_in + 2 * nr + n_out]
        scr = refs[n_in + 2 * nr + n_out:n_in + 2 * nr + n_out + n_scr]
        sems = refs[n_in + 2 * nr + n_out + n_scr:]
        if nr:
            first, relay, last = True, True, True
            for ax, size in enumerate(grid):
                first = first & (pl.program_id(ax) == 0)
                relay = relay & (pl.program_id(ax) == (3 * size // 4 if ax == 0 else 0))
                last = last & (pl.program_id(ax) == size - 1)
            if not grid:
                rider.start(r_src, r_dst, *sems)
                rider.relay(r_src, r_dst, *sems)
            else:
                pl.when(first)(lambda: rider.start(r_src, r_dst, *sems))
                if rider.bcast:
                    pl.when(relay)(lambda: rider.relay(r_src, r_dst, *sems))
        core(*ins, *outs, *scr)
        if nr:
            if not grid:
                rider.wait(r_src, r_dst, *sems)
            else:
                pl.when(last)(lambda: rider.wait(r_src, r_dst, *sems))

    res = pl.pallas_call(
        body, name=name, grid=grid,
        in_specs=list(in_specs) + rider.specs(),
        out_specs=list(out_specs) + rider.specs(),
        out_shape=list(out_shape) + rider.out_shapes(),
        scratch_shapes=list(scratch_shapes) + rider.scratch(),
        compiler_params=_params(len(grid)),
    )(*args, *rider.srcs)
    return res[:n_out], res[n_out:]


def exchange_hbm(srcs, bcast, name):
    return rider_call(lambda: None, name, (), [], [], [], [], [], Rider(srcs, bcast))[1]


_HBM = pl.BlockSpec(memory_space=pltpu.HBM)
_SEM = pl.BlockSpec(memory_space=pltpu.SEMAPHORE)
_EFFECT = pltpu.CompilerParams(has_side_effects=pltpu.SideEffectType.DATAFLOW_SIDE_EFFECTING)


def scatter_start(src, name):
    def body(src_ref, land_ref, send_sems, recv_sems, src_thru, land_thru, token):
        me = 4 * lax.axis_index("x") + 2 * lax.axis_index("y") + lax.axis_index("c")
        for k in range(1, N_DEV):
            dev, idx = _peer(k)
            pltpu.make_async_remote_copy(src_ref=src_ref.at[idx], dst_ref=land_ref.at[me], send_sem=send_sems.at[k - 1],
                                         recv_sem=recv_sems.at[k - 1], device_id=dev, device_id_type=MESH).start()
        token[...] = jnp.zeros_like(token)

    return pl.pallas_call(
        body, name=name,
        out_shape=(pltpu.SemaphoreType.DMA((N_DEV - 1,)), pltpu.SemaphoreType.DMA((N_DEV - 1,)),
                   pltpu.HBM(src.shape, src.dtype), pltpu.HBM(src.shape, src.dtype), jax.ShapeDtypeStruct((8, LANES), F32)),
        in_specs=(_HBM, _HBM), out_specs=(_SEM, _SEM, _HBM, _HBM, pl.BlockSpec(memory_space=pltpu.VMEM)),
        input_output_aliases={0: 2, 1: 3}, compiler_params=_EFFECT,
    )(pltpu.with_memory_space_constraint(src, pltpu.HBM),
      pltpu.with_memory_space_constraint(lax.empty(src.shape, src.dtype), pltpu.HBM))


def scatter_wait(send_sems, recv_sems, src_thru, land_thru, after, name):
    n_after = len(after)

    def body(*refs):
        src_ref, land_ref, send_sems, recv_sems = refs[:4]
        for k in range(1, N_DEV):
            dev, idx = _peer(k)
            copy = pltpu.make_async_remote_copy(src_ref=src_ref.at[idx], dst_ref=land_ref.at[idx],
                                                send_sem=send_sems.at[k - 1], recv_sem=recv_sems.at[k - 1],
                                                device_id=dev, device_id_type=MESH)
            copy.wait_send()
            copy.wait_recv()

    return pl.pallas_call(
        body, name=name,
        out_shape=(pltpu.HBM(src_thru.shape, src_thru.dtype), pltpu.HBM(land_thru.shape, land_thru.dtype)),
        in_specs=(_HBM, _HBM, _SEM, _SEM) + (pl.BlockSpec(memory_space=pl.ANY),) * n_after, out_specs=(_HBM, _HBM),
        input_output_aliases={0: 0, 1: 1}, compiler_params=_EFFECT,
    )(src_thru, land_thru, send_sems, recv_sems, *after)


def allsum_small(vec, name):
    def body(src_ref, out_ref, dst_ref, send_sems, recv_sems, local_sem):
        _exchange_body(src_ref, dst_ref, send_sems, recv_sems, local_sem, True)
        acc = dst_ref[0]
        for j in range(1, N_DEV):
            acc = acc + dst_ref[j]
        out_ref[...] = acc

    return pl.pallas_call(
        body, name=name,
        out_shape=jax.ShapeDtypeStruct(vec.shape[1:], F32),
        in_specs=[pl.BlockSpec(memory_space=pltpu.VMEM)],
        out_specs=pl.BlockSpec(memory_space=pltpu.VMEM),
        scratch_shapes=[pltpu.VMEM((N_DEV,) + vec.shape[1:], F32),
                        pltpu.SemaphoreType.DMA((N_DEV - 1,)), pltpu.SemaphoreType.DMA((N_DEV - 1,)),
                        pltpu.SemaphoreType.DMA],
    )(vec)


def gather_small(vec, name):
    def body(src_ref, dst_ref, send_sems, recv_sems, local_sem):
        _exchange_body(src_ref, dst_ref, send_sems, recv_sems, local_sem, True)

    return pl.pallas_call(
        body, name=name,
        out_shape=jax.ShapeDtypeStruct((N_DEV,) + vec.shape[1:], F32),
        in_specs=[pl.BlockSpec(memory_space=pltpu.VMEM)],
        out_specs=pl.BlockSpec(memory_space=pltpu.VMEM),
        scratch_shapes=[pltpu.SemaphoreType.DMA((N_DEV - 1,)), pltpu.SemaphoreType.DMA((N_DEV - 1,)),
                        pltpu.SemaphoreType.DMA],
    )(vec)


FFN_FWD_ROWS = 1056
FFN_BWD_ROWS = 704
FFN_BWD_CHUNKS = 4
DW_ROWS = 1408

def ffn_fwd(h, g, w_in8, w_out4, name, rider=None):
    m, d = h.shape
    fb = w_in8.shape[1]
    tm = _tile(m, FFN_FWD_ROWS)

    def body(h_ref, g_ref, wg_ref, wu_ref, wo_ref, hn_ref, n_ref, a_ref, fg_ref, fu_ref, acc_ref):
        i = pl.program_id(1)

        @pl.when(i == 0)
        def _():
            xh, _ = _rms(h_ref[...])
            n_ref[...] = (xh * g_ref[...]).astype(BF16)
            acc_ref[...] = jnp.zeros_like(acc_ref)

        n = n_ref[...]
        gate = _dot_nt(n, wg_ref[0])
        up = _dot_nt(n, wu_ref[0])
        sg = jax.nn.sigmoid(gate)
        silu = gate * sg
        a = (silu * up).astype(BF16)
        a_ref[0] = a
        fg_ref[0] = (up * (sg * (1.0 + gate * (1.0 - sg)))).astype(BF16)
        fu_ref[0] = silu.astype(BF16)
        acc_ref[...] += _dot(a, wo_ref[0])

        @pl.when(i == 3)
        def _():
            hn_ref[...] = h_ref[...] + 0.5 * acc_ref[...]

    return rider_call(
        body, name, (m // tm, 4),
        in_specs=[pl.BlockSpec((tm, d), lambda r, i: (r, 0)),
                  pl.BlockSpec((1, d), lambda r, i: (0, 0)),
                  pl.BlockSpec((1, fb, d), lambda r, i: (i, 0, 0)),
                  pl.BlockSpec((1, fb, d), lambda r, i: (i + 4, 0, 0)),
                  pl.BlockSpec((1, fb, d), lambda r, i: (i, 0, 0))],
        out_specs=[pl.BlockSpec((tm, d), lambda r, i: (r, 0)),
                   pl.BlockSpec((tm, d), lambda r, i: (r, 0))] + [pl.BlockSpec((1, tm, fb), lambda r, i: (i, r, 0))] * 3,
        out_shape=[jax.ShapeDtypeStruct((m, d), F32), jax.ShapeDtypeStruct((m, d), BF16)]
                  + [jax.ShapeDtypeStruct((4, m, fb), BF16)] * 3,
        scratch_shapes=[pltpu.VMEM((tm, d), F32)],
        args=(h, g, w_in8, w_in8, w_out4), rider=rider)


def ffn_bwd(dh, h, g, f_gate, f_up, w_in8, w_out4, name, rider=None):
    m, d = h.shape
    fb = w_in8.shape[1]
    tm = _tile(m, FFN_BWD_ROWS)

    def body(dh_ref, h_ref, g_ref, fg_ref, fu_ref, wg_ref, wu_ref, wo_ref,
             dhin_ref, dg_ref, du_ref, dgn_ref, dhs_ref, acc_ref):
        r = pl.program_id(0)
        i = pl.program_id(1)

        @pl.when(i == 0)
        def _():
            dhs_ref[...] = (0.5 * dh_ref[...]).astype(BF16)
            acc_ref[...] = jnp.zeros_like(acc_ref)

        @pl.when((r == 0) & (i == 0))
        def _():
            dgn_ref[...] = jnp.zeros_like(dgn_ref)

        chunks = FFN_BWD_CHUNKS if tm % (16 * FFN_BWD_CHUNKS) == 0 else 1
        for c in range(chunks):
            rows = slice(c * tm // chunks, (c + 1) * tm // chunks)
            da = _dot_nt(dhs_ref[rows, :], wo_ref[0])
            dub = (da * fu_ref[0, rows, :]).astype(BF16)
            dgb = (da * fg_ref[0, rows, :]).astype(BF16)
            dg_ref[0, rows, :] = dgb
            du_ref[0, rows, :] = dub
            acc_ref[rows, :] += _dot(dgb, wg_ref[0]) + _dot(dub, wu_ref[0])

        @pl.when(i == 3)
        def _():
            dx, dgain = _rms_bwd(h_ref[...], g_ref[...], acc_ref[...])
            dgn_ref[...] += dgain
            dhin_ref[...] = dh_ref[...] + dx

    row = lambda r, i: (r, 0)
    blk = lambda r, i: (i, r, 0)
    return rider_call(
        body, name, (m // tm, 4),
        in_specs=[pl.BlockSpec((tm, d), row), pl.BlockSpec((tm, d), row),
                  pl.BlockSpec((1, d), lambda r, i: (0, 0)),
                  pl.BlockSpec((1, tm, fb), blk), pl.BlockSpec((1, tm, fb), blk),
                  pl.BlockSpec((1, fb, d), lambda r, i: (i, 0, 0)),
                  pl.BlockSpec((1, fb, d), lambda r, i: (i + 4, 0, 0)),
                  pl.BlockSpec((1, fb, d), lambda r, i: (i, 0, 0))],
        out_specs=[pl.BlockSpec((tm, d), row),
                   pl.BlockSpec((1, tm, fb), blk), pl.BlockSpec((1, tm, fb), blk),
                   pl.BlockSpec((1, d), lambda r, i: (0, 0)),
                   pl.BlockSpec((tm, d), row)],
        out_shape=[jax.ShapeDtypeStruct((m, d), F32),
                   jax.ShapeDtypeStruct((4, m, fb), BF16), jax.ShapeDtypeStruct((4, m, fb), BF16),
                   jax.ShapeDtypeStruct((1, d), F32), jax.ShapeDtypeStruct((m, d), BF16)],
        scratch_shapes=[pltpu.VMEM((tm, d), F32)],
        args=(dh, h, g, f_gate, f_up, w_in8, w_in8, w_out4), rider=rider)


def matmul_tn(x, y, name, tn=None, tk=None, x2=None, rider=None):
    if tk is not None:
        assert x.shape[0] == y.shape[0] == 1 and x2 is None and rider is None
        bk, m, kf = x.shape[2] // tk, x.shape[1], x.shape[2]
        tm = _tile(m, DW_ROWS)
        tn_ = y.shape[2] if tn is None else tn

        def tiled(x_ref, y_ref, o_ref, acc_ref):
            r = pl.program_id(2)

            @pl.when(r == 0)
            def _():
                acc_ref[...] = jnp.zeros_like(acc_ref)

            acc_ref[...] += _dot_tn(x_ref[0].astype(BF16), y_ref[0].astype(BF16))

            @pl.when(r == m // tm - 1)
            def _():
                o_ref[0] = acc_ref[...].astype(BF16)

        return pl.pallas_call(
            tiled, name=name, grid=(bk, y.shape[2] // tn_, m // tm),
            in_specs=[pl.BlockSpec((1, tm, tk), lambda i, j, r: (0, r, i)),
                      pl.BlockSpec((1, tm, tn_), lambda i, j, r: (0, r, j))],
            out_specs=pl.BlockSpec((1, tk, tn_), lambda i, j, r: (0, i, j)),
            out_shape=jax.ShapeDtypeStruct((1, kf, y.shape[2]), BF16),
            scratch_shapes=[pltpu.VMEM((tk, tn_), F32)],
            compiler_params=_params(3),
        )(x, y)
    bx, m, k = x.shape
    by, _, n = y.shape
    b = max(bx, by) * (2 if x2 is not None else 1)
    tm = _tile(m, DW_ROWS)
    tn = n if tn is None else tn
    nt = n // tn
    nr = m // tm

    def body(*refs):
        x_ref, y_ref = refs[0], refs[-3]
        o_ref, acc_ref = refs[-2], refs[-1]
        r = pl.program_id(2)

        @pl.when(r == 0)
        def _():
            acc_ref[...] = jnp.zeros_like(acc_ref)

        if x2 is None:
            acc_ref[...] += _dot_tn(x_ref[0].astype(BF16), y_ref[0].astype(BF16))
        else:
            @pl.when(pl.program_id(0) < bx)
            def _():
                acc_ref[...] += _dot_tn(x_ref[0].astype(BF16), y_ref[0].astype(BF16))

            @pl.when(pl.program_id(0) >= bx)
            def _():
                acc_ref[...] += _dot_tn(refs[1][0].astype(BF16), y_ref[0].astype(BF16))

        @pl.when(r == nr - 1)
        def _():
            o_ref[0] = acc_ref[...].astype(BF16)

    if x2 is None:
        x_specs = [pl.BlockSpec((1, tm, k), (lambda i, j, r: (i, r, 0)) if bx > 1 else (lambda i, j, r: (0, r, 0)))]
    else:
        x_specs = [pl.BlockSpec((1, tm, k), lambda i, j, r: (jnp.minimum(i, bx - 1), jnp.where(i < bx, r, nr - 1), 0)),
                   pl.BlockSpec((1, tm, k), lambda i, j, r: (jnp.maximum(i - bx, 0), jnp.where(i < bx, 0, r), 0))]
    y_map = (lambda i, j, r: (i, r, j)) if by > 1 else (lambda i, j, r: (0, r, j))
    (out,), carried = rider_call(
        body, name, (b, nt, nr),
        in_specs=x_specs + [pl.BlockSpec((1, tm, tn), y_map)],
        out_specs=[pl.BlockSpec((1, k, tn), lambda i, j, r: (i, 0, j))],
        out_shape=[jax.ShapeDtypeStruct((b, k, n), BF16)],
        scratch_shapes=[pltpu.VMEM((k, tn), F32)],
        args=[x] + ([x2] if x2 is not None else []) + [y], rider=rider)
    return (out, carried) if rider is not None else out


AUG = HEAD_DIM


class _Cols:
    def __init__(self, d):
        self.d = d
        self.ga, self.gb = 0, d
        self.qa, self.ka, self.va = 2 * d, 2 * d + 1024, 2 * d + 2048
        self.qb = 2 * d + 3072
        self.kb, self.vb, self.fa = self.qb + 512, self.qb + 640, self.qb + 768
        self.np = self.qb + 1024


def mixer_proj(h, g, wp, name):
    m, d = h.shape
    npad = wp.shape[0]
    tm = _row_tile(m)

    def body(h_ref, g_ref, w_ref, n_ref, p_ref):
        xh, _ = _rms(h_ref[...])
        n = (xh * g_ref[...]).astype(BF16)
        n_ref[...] = n
        p_ref[...] = _dot_nt(n, w_ref[...])

    return pl.pallas_call(
        body, name=name, grid=(m // tm,),
        in_specs=[pl.BlockSpec((tm, d), lambda r: (r, 0)), pl.BlockSpec((1, d), lambda r: (0, 0)),
                  pl.BlockSpec((npad, d), lambda r: (0, 0))],
        out_specs=[pl.BlockSpec((tm, d), lambda r: (r, 0)), pl.BlockSpec((tm, npad), lambda r: (r, 0))],
        out_shape=[jax.ShapeDtypeStruct((m, d), BF16), jax.ShapeDtypeStruct((m, npad), F32)],
        compiler_params=_params(1),
    )(h, g, wp)


def _head_norm(x, gain, ones):
    outs = []
    for b in range(x.shape[1] // LANES):
        xb = x[:, b * LANES:(b + 1) * LANES]
        r = lax.rsqrt(_group_mean(xb * xb, ones) + EPS)
        outs.append(xb * r * gain[:, b * LANES:(b + 1) * LANES])
    return outs


def _head_norm_bwd(x, gain, dn, ones):
    dxs, dgs = [], []
    for b in range(x.shape[1] // LANES):
        sl = slice(b * LANES, (b + 1) * LANES)
        xb, dnb = x[:, sl], dn[:, sl]
        r = lax.rsqrt(_group_mean(xb * xb, ones) + EPS)
        xh = xb * r
        dxh = dnb * gain[:, sl]
        dxs.append(r * (dxh - xh * _group_mean(dxh * xh, ones)))
        dgs.append(jnp.sum(dnb * xh, axis=0, keepdims=True))
    return dxs, dgs


def _lane_col(v, lane_iota, idx):
    return jnp.sum(jnp.where(lane_iota == idx, v, 0.0), axis=1, keepdims=True)


def _aug(base, lane, vals):
    for i, v in enumerate(vals):
        base = jnp.where(lane == AUG + i, v, base)
    return base


def qk_post(proj, gains, fbias, cols, name):
    m = proj.shape[0]
    tm = _row_tile(m)
    gqa, gka, gqb, gkb = gains

    def body(qa_ref, ka_ref, va_ref, qb_ref, kb_ref, vb_ref, fa_ref, gqa_ref, gka_ref, gqb_ref, gkb_ref, fb_ref,
             qf_o, kf_o, vf_o, kt_o, vt_o, qb_o, kb_o, vb_o, carry_ref):
        r0 = pl.program_id(0)

        @pl.when(r0 == 0)
        def _():
            carry_ref[...] = jnp.zeros_like(carry_ref)

        z = fa_ref[...] + fb_ref[...]
        logf = jnp.minimum(z, 0.0) - jnp.log(1.0 + jnp.exp(-jnp.abs(z)))
        rr = lax.broadcasted_iota(jnp.int32, (tm, tm), 0)
        cc = lax.broadcasted_iota(jnp.int32, (tm, tm), 1)
        tril = jnp.where(cc <= rr, 1.0, 0.0).astype(BF16)
        p0, p1, p2 = _split3(logf)
        c = _dot(tril, p0) + _dot(tril, p1) + _dot(tril, p2) + carry_ref[...]
        carry_ref[...] += jnp.sum(logf, axis=0, keepdims=True)

        lane = lax.broadcasted_iota(jnp.int32, (tm, LANES), 1)
        is_pad = (r0 * tm + lax.broadcasted_iota(jnp.int32, (tm, 1), 0)) < PAD_FRONT
        ones = jnp.ones((LANES, LANES), BF16)
        for hd in range(8):
            sl = slice(hd * LANES, (hd + 1) * LANES)
            ch = _lane_col(c, lane, hd)
            ct = [p.astype(F32) for p in _split3(ch)]
            cs = [p.astype(F32) for p in _split3(-jnp.where(is_pad, -NEG, ch))]
            xq = qa_ref[:, sl]
            qn = xq * lax.rsqrt(_group_mean(xq * xq, ones) + EPS) * gqa_ref[...]
            qf_o[:, sl] = _aug(qn, lane, ct + [1.0, 1.0, 1.0]).astype(BF16)
            xk = ka_ref[:, sl]
            kn = xk * lax.rsqrt(_group_mean(xk * xk, ones) + EPS) * gka_ref[...]
            kf = _aug(kn, lane, [1.0, 1.0, 1.0] + cs)
            vf = _aug(va_ref[:, sl], lane, [1.0, 1.0, 1.0])
            kf_o[:, sl] = kf.astype(BF16)
            vf_o[:, sl] = vf.astype(BF16)
            kt_o[sl, :] = kf.T.astype(BF16)
            vt_o[sl, :] = vf.T.astype(BF16)

        gones = _group_ones()
        for src, gn, dst in ((qb_ref, gqb_ref, qb_o), (kb_ref, gkb_ref, kb_o)):
            for b, blk in enumerate(_head_norm(src[...], gn[...], gones)):
                dst[:, b * LANES:(b + 1) * LANES] = blk.astype(BF16)
        vb_o[...] = vb_ref[...].astype(BF16)

    w1024 = lambda off: pl.BlockSpec((tm, 1024), lambda r, o=off // 1024: (r, o))
    w512 = lambda off: pl.BlockSpec((tm, 512), lambda r, o=off // 512: (r, o))
    w128 = lambda off: pl.BlockSpec((tm, LANES), lambda r, o=off // LANES: (r, o))
    vec = lambda w: pl.BlockSpec((1, w), lambda r: (0, 0))
    row = lambda w: pl.BlockSpec((tm, w), lambda r: (r, 0))
    return pl.pallas_call(
        body, name=name, grid=(m // tm,),
        in_specs=[w1024(cols.qa), w1024(cols.ka), w1024(cols.va), w512(cols.qb), w128(cols.kb), w128(cols.vb),
                  w128(cols.fa), vec(LANES), vec(LANES), vec(512), vec(LANES), vec(LANES)],
        out_specs=[row(1024), row(1024), row(1024)] + [pl.BlockSpec((1024, tm), lambda r: (0, r))] * 2
                  + [row(512), row(LANES), row(LANES)],
        out_shape=[jax.ShapeDtypeStruct((m, 1024), BF16)] * 3 + [jax.ShapeDtypeStruct((1024, m), BF16)] * 2
                  + [jax.ShapeDtypeStruct((m, 512), BF16)] + [jax.ShapeDtypeStruct((m, LANES), BF16)] * 2,
        scratch_shapes=[pltpu.VMEM((1, LANES), F32)],
        compiler_params=_params(1),
    )(proj, proj, proj, proj, proj, proj, proj, gqa, gka, gqb, gkb, fbias)


def qk_post_bwd(proj, gains, fbias, dqf, dkf, dvf, dqb, dkb, dvb, dc, dga, dgb, cols, name):
    m = proj.shape[0]
    d = cols.d
    tm = _row_tile(m)
    nt = m // tm
    gqa, gka, gqb, gkb = gains

    def body(qa_ref, ka_ref, qb_ref, kb_ref, fa_ref, gqa_ref, gka_ref, gqb_ref, gkb_ref, fb_ref,
             dqf_ref, dkf_ref, dvf_ref, dqb_ref, dkb_ref, dvb_ref, dc_ref, dga_ref, dgb_ref,
             dp_o, ggqa_o, ggka_o, ggqb_o, ggkb_o, gfb_o, carry_ref):
        @pl.when(pl.program_id(0) == 0)
        def _():
            carry_ref[...] = jnp.zeros_like(carry_ref)
            for o in (ggqa_o, ggka_o, ggqb_o, ggkb_o, gfb_o):
                o[...] = jnp.zeros_like(o)

        dp_o[:, cols.ga:cols.ga + d] = dga_ref[...].astype(BF16)
        dp_o[:, cols.gb:cols.gb + d] = dgb_ref[...].astype(BF16)
        dp_o[:, cols.fa + LANES:cols.np] = jnp.zeros((tm, cols.np - cols.fa - LANES), BF16)
        lane = lax.broadcasted_iota(jnp.int32, (tm, LANES), 1)
        data = lane < HEAD_DIM
        ones = jnp.ones((LANES, LANES), BF16)
        for hd in range(8):
            sl = slice(hd * LANES, (hd + 1) * LANES)
            for src, gn, dn_ref, off, gout in ((qa_ref, gqa_ref, dqf_ref, cols.qa, ggqa_o),
                                               (ka_ref, gka_ref, dkf_ref, cols.ka, ggka_o)):
                x = src[:, sl]
                dn = jnp.where(data, dn_ref[:, sl], 0.0)
                r = lax.rsqrt(_group_mean(x * x, ones) + EPS)
                xh = x * r
                dxh = dn * gn[...]
                dp_o[:, off + hd * LANES:off + (hd + 1) * LANES] = (
                    r * (dxh - xh * _group_mean(dxh * xh, ones))).astype(BF16)
                gout[...] += jnp.sum(dn * xh, axis=0, keepdims=True)
            dp_o[:, cols.va + hd * LANES:cols.va + (hd + 1) * LANES] = jnp.where(data, dvf_ref[:, sl], 0.0).astype(BF16)
        dp_o[:, cols.vb:cols.vb + LANES] = dvb_ref[...].astype(BF16)
        gones = _group_ones()
        for src, gn, dn, off, gout in ((qb_ref, gqb_ref, dqb_ref, cols.qb, ggqb_o),
                                       (kb_ref, gkb_ref, dkb_ref, cols.kb, ggkb_o)):
            dxs, dgs = _head_norm_bwd(src[...], gn[...], dn[...], gones)
            for b, (dx, dg) in enumerate(zip(dxs, dgs)):
                dp_o[:, off + b * LANES:off + (b + 1) * LANES] = dx.astype(BF16)
                gout[:, b * LANES:(b + 1) * LANES] += dg
        dcv = dc_ref[...]
        rr = lax.broadcasted_iota(jnp.int32, (tm, tm), 0)
        cc = lax.broadcasted_iota(jnp.int32, (tm, tm), 1)
        triu = jnp.where(cc >= rr, 1.0, 0.0).astype(BF16)
        p0, p1, p2 = _split3(dcv)
        dlogf = _dot(triu, p0) + _dot(triu, p1) + _dot(triu, p2) + carry_ref[...]
        carry_ref[...] += jnp.sum(dcv, axis=0, keepdims=True)
        z = fa_ref[...] + fb_ref[...]
        row = (nt - 1 - pl.program_id(0)) * tm + lax.broadcasted_iota(jnp.int32, (tm, LANES), 0)
        dfa = jnp.where(row >= PAD_FRONT, dlogf * jax.nn.sigmoid(-z), 0.0)
        dp_o[:, cols.fa:cols.fa + LANES] = dfa.astype(BF16)
        gfb_o[...] += jnp.sum(dfa, axis=0, keepdims=True)

    rev = lambda r: nt - 1 - r
    w1024 = lambda off: pl.BlockSpec((tm, 1024), lambda r, o=off // 1024: (rev(r), o))
    w512 = lambda off: pl.BlockSpec((tm, 512), lambda r, o=off // 512: (rev(r), o))
    w128 = lambda off: pl.BlockSpec((tm, LANES), lambda r, o=off // LANES: (rev(r), o))
    vec = lambda w: pl.BlockSpec((1, w), lambda r: (0, 0))
    row = lambda w: pl.BlockSpec((tm, w), lambda r: (rev(r), 0))
    return pl.pallas_call(
        body, name=name, grid=(nt,),
        in_specs=[w1024(cols.qa), w1024(cols.ka), w512(cols.qb), w128(cols.kb), w128(cols.fa),
                  vec(LANES), vec(LANES), vec(512), vec(LANES), vec(LANES),
                  row(1024), row(1024), row(1024), row(512), row(LANES), row(LANES), row(LANES), row(d), row(d)],
        out_specs=[row(cols.np), vec(LANES), vec(LANES), vec(512), vec(LANES), vec(LANES)],
        out_shape=[jax.ShapeDtypeStruct((m, cols.np), BF16), jax.ShapeDtypeStruct((1, LANES), F32),
                   jax.ShapeDtypeStruct((1, LANES), F32), jax.ShapeDtypeStruct((1, 512), F32),
                   jax.ShapeDtypeStruct((1, LANES), F32), jax.ShapeDtypeStruct((1, LANES), F32)],
        scratch_shapes=[pltpu.VMEM((1, LANES), F32)],
        compiler_params=_params(1),
    )(proj, proj, proj, proj, proj, gqa, gka, gqb, gkb, fbias, dqf, dkf, dvf, dqb, dkb, dvb, dc, dga, dgb)


def dproj_bwd(dh, h, g, dproj, wp, name):
    m, d = h.shape
    npad = wp.shape[0]
    tm = _row_tile(m)

    def body(dh_ref, h_ref, g_ref, dp_ref, w_ref, dhin_ref, dgn_ref):
        @pl.when(pl.program_id(0) == 0)
        def _():
            dgn_ref[...] = jnp.zeros_like(dgn_ref)

        dn = _dot(dp_ref[...], w_ref[...])
        dx, dgain = _rms_bwd(h_ref[...], g_ref[...], dn)
        dgn_ref[...] += dgain
        dhin_ref[...] = dh_ref[...] + dx

    row = lambda w: pl.BlockSpec((tm, w), lambda r: (r, 0))
    return pl.pallas_call(
        body, name=name, grid=(m // tm,),
        in_specs=[row(d), row(d), pl.BlockSpec((1, d), lambda r: (0, 0)), row(npad),
                  pl.BlockSpec((npad, d), lambda r: (0, 0))],
        out_specs=[row(d), pl.BlockSpec((1, d), lambda r: (0, 0))],
        out_shape=[jax.ShapeDtypeStruct((m, d), F32), jax.ShapeDtypeStruct((1, d), F32)],
        compiler_params=_params(1),
    )(dh, h, g, dproj, wp)


def _causal_t(t):
    return lax.broadcasted_iota(jnp.int32, (t, t), 0) <= lax.broadcasted_iota(jnp.int32, (t, t), 1)


HEADS_PER_STEP = 4


def fox_fwd(qf, kf, vt, name, rider=None):
    m = qf.shape[0]
    t = _row_tile(m)
    nq = m // t
    hp = HEADS_PER_STEP
    w = hp * LANES

    def body(q_ref, k_ref, vt_ref, o_ref, lse_ref, acc_ref, m_ref, p_ref, a_ref):
        qi = pl.program_id(1)
        acc_ref[...] = jnp.zeros_like(acc_ref)
        m_ref[...] = jnp.full_like(m_ref, NEG)

        def scores(ki, slot, mask):
            off = pl.multiple_of(ki * t, t)
            for e in range(hp):
                sl = slice(e * LANES, (e + 1) * LANES)
                s = _dot_nt(k_ref[pl.ds(off, t), sl], q_ref[:, sl])
                if mask is not None:
                    s = jnp.where(mask, s, NEG)
                m_old = m_ref[e]
                m_new = jnp.maximum(m_old, jnp.max(s, axis=0, keepdims=True))
                p_ref[slot, e] = jnp.exp(s - m_new).astype(BF16)
                a_ref[slot, e] = jnp.exp(m_old - m_new)
                m_ref[e] = m_new

        def values(ki, slot):
            off = pl.multiple_of(ki * t, t)
            for e in range(hp):
                sl = slice(e * LANES, (e + 1) * LANES)
                acc_ref[e] = acc_ref[e] * a_ref[slot, e] + _dot(vt_ref[sl, pl.ds(off, t)], p_ref[slot, e])

        causal = _causal_t(t)
        scores(0, 0, causal | (jnp.full((t, t), qi, jnp.int32) > 0))

        def step(ki, carry):
            values(ki - 1, (ki - 1) % 2)
            scores(ki, ki % 2, None)
            return carry

        lax.fori_loop(1, qi, step, 0)

        @pl.when(qi >= 1)
        def _():
            values(qi - 1, (qi - 1) % 2)
            scores(qi, qi % 2, causal)

        values(qi, qi % 2)
        row = lax.broadcasted_iota(jnp.int32, (LANES, t), 0)
        for e in range(hp):
            l = jnp.max(acc_ref[e, AUG:AUG + 8, :], axis=0, keepdims=True)
            o_ref[:, e * LANES:(e + 1) * LANES] = jnp.where(row < HEAD_DIM, acc_ref[e] * (1.0 / l), 0.0).T
            lse_ref[e] = m_ref[e] + jnp.log(l)

    return rider_call(
        body, name, (8 // hp, nq),
        in_specs=[pl.BlockSpec((t, w), lambda hd, i: (i, hd)),
                  pl.BlockSpec((m, w), lambda hd, i: (0, hd)),
                  pl.BlockSpec((w, m), lambda hd, i: (hd, 0))],
        out_specs=[pl.BlockSpec((t, w), lambda hd, i: (i, hd)),
                   pl.BlockSpec((hp, 1, t), lambda hd, i: (hd, 0, i))],
        out_shape=[jax.ShapeDtypeStruct((m, 8 * LANES), F32), jax.ShapeDtypeStruct((8, 1, m), F32)],
        scratch_shapes=[pltpu.VMEM((hp, LANES, t), F32), pltpu.VMEM((hp, 1, t), F32),
                        pltpu.VMEM((2, hp, t, t), BF16), pltpu.VMEM((2, hp, 1, t), F32)],
        args=(qf, kf, vt), rider=rider)


def fox_bwd(qf, kf, vf, kt, dof, lse, delta, name, rider=None):
    m = qf.shape[0]
    t = _row_tile(m)
    nq = m // t
    hp = HEADS_PER_STEP
    w = hp * LANES

    def body(k_ref, v_ref, kt_ref, q_ref, do_ref, lse_ref, delta_ref, dk_ref, dv_ref, dq_ref, dck_ref, dcq_ref,
             dka_ref, dva_ref, dqt_ref):
        ki = pl.program_id(1)

        @pl.when(ki == 0)
        def _():
            dqt_ref[...] = jnp.zeros_like(dqt_ref)

        dka_ref[...] = jnp.zeros_like(dka_ref)
        dva_ref[...] = jnp.zeros_like(dva_ref)

        def tile(qi, diagonal):
            off = pl.multiple_of(qi * t, t)
            for e in range(hp):
                sl = slice(e * LANES, (e + 1) * LANES)
                q = q_ref[pl.ds(off, t), sl]
                do = do_ref[pl.ds(off, t), sl]
                s = _dot_nt(k_ref[:, sl], q)
                if diagonal:
                    s = jnp.where(_causal_t(t), s, NEG)
                p = jnp.exp(s - lse_ref[e, :, pl.ds(off, t)])
                ds = (p * (_dot_nt(v_ref[:, sl], do) - delta_ref[e, :, pl.ds(off, t)])).astype(BF16)
                dva_ref[:, sl] += _dot(p.astype(BF16), do)
                dka_ref[:, sl] += _dot(ds, q)
                dqt_ref[sl, pl.ds(off, t)] += _dot(kt_ref[sl, :], ds)

        def step(qi, carry):
            tile(qi, False)
            return carry

        tile(ki, True)
        lax.fori_loop(ki + 1, nq, step, 0)
        dk_ref[...] = dka_ref[...]
        dv_ref[...] = dva_ref[...]
        row8 = lax.broadcasted_iota(jnp.int32, (8, 1), 0)
        for e in range(hp):
            slab = dka_ref[:, e * LANES:(e + 1) * LANES].T[AUG:AUG + 8, :]
            dck_ref[e] = -jnp.sum(jnp.where(row8 == 3, slab, 0.0), axis=0, keepdims=True)

        @pl.when(ki == nq - 1)
        def _():
            for e in range(hp):
                sl = slice(e * LANES, (e + 1) * LANES)
                slab = dqt_ref[e * LANES + AUG:e * LANES + AUG + 8, :]
                dcq_ref[e] = jnp.sum(jnp.where(row8 == 0, slab, 0.0), axis=0, keepdims=True)
                for j in range(nq):
                    dq_ref[j * t:(j + 1) * t, sl] = dqt_ref[sl, j * t:(j + 1) * t].T

    tile_spec = pl.BlockSpec((t, w), lambda hd, i: (i, hd))
    full = pl.BlockSpec((m, w), lambda hd, i: (0, hd))
    stat = pl.BlockSpec((hp, 1, m), lambda hd, i: (hd, 0, 0))
    (dkf, dvf, dqf, dck, dcq), carried = rider_call(
        body, name, (8 // hp, nq),
        in_specs=[tile_spec, tile_spec, pl.BlockSpec((w, t), lambda hd, i: (hd, i)), full, full, stat, stat],
        out_specs=[tile_spec, tile_spec, full, pl.BlockSpec((hp, 1, t), lambda hd, i: (hd, 0, i)), stat],
        out_shape=[jax.ShapeDtypeStruct((m, 8 * LANES), F32), jax.ShapeDtypeStruct((m, 8 * LANES), F32),
                   jax.ShapeDtypeStruct((m, 8 * LANES), F32), jax.ShapeDtypeStruct((8, 1, m), F32),
                   jax.ShapeDtypeStruct((8, 1, m), F32)],
        scratch_shapes=[pltpu.VMEM((t, w), F32), pltpu.VMEM((t, w), F32), pltpu.VMEM((w, m), F32)],
        args=(kf, vf, kt, qf, dof, lse, delta), rider=rider)
    return (dkf, dvf, dqf, dcq + dck), carried


def _bucket_ids():
    def bucket(dist):
        n = np.maximum(dist, 0)
        max_exact = N_BUCKETS // 2
        nf = np.maximum(n, 1).astype(np.float32)
        large = max_exact + (np.log(nf / max_exact) / math.log(MAX_DISTANCE / max_exact)
                             * (N_BUCKETS - max_exact)).astype(np.int32)
        return np.where(n < max_exact, n, np.minimum(large, N_BUCKETS - 1))

    tl = np.arange(LANES)[:, None]
    sl = np.arange(LANES)[None, :]
    prev = bucket(LANES + tl - sl)
    cur = bucket(tl - sl)
    meta = np.full((LANES, LANES), N_BUCKETS - 1)
    return np.concatenate([prev, cur, meta], axis=1).astype(np.int32)


def bias_build(table, name):
    ids = jnp.asarray(_bucket_ids())

    def body(t_ref, id_ref, o_ref):
        idv = id_ref[...]
        for h in range(8):
            acc = jnp.zeros((LANES, 3 * LANES), F32)
            for b in range(N_BUCKETS):
                acc = jnp.where(idv == b, t_ref[b, h], acc)
            o_ref[h] = acc

    return pl.pallas_call(
        body, name=name,
        in_specs=[pl.BlockSpec(memory_space=pltpu.SMEM), pl.BlockSpec(memory_space=pltpu.VMEM)],
        out_specs=pl.BlockSpec(memory_space=pltpu.VMEM),
        out_shape=jax.ShapeDtypeStruct((8, LANES, 3 * LANES), F32),
    )(table, ids)


def bias_reduce(dbias, name):
    ids = jnp.asarray(_bucket_ids())

    def body(d_ref, id_ref, o_ref):
        idv = id_ref[...]
        rr = lax.broadcasted_iota(jnp.int32, (N_BUCKETS, LANES), 0)
        cc = lax.broadcasted_iota(jnp.int32, (N_BUCKETS, LANES), 1)
        acc = jnp.zeros((N_BUCKETS, LANES), F32)
        for h in range(8):
            dv = d_ref[h]
            for b in range(N_BUCKETS):
                val = jnp.sum(jnp.where(idv == b, dv, 0.0), keepdims=True)
                acc = jnp.where((rr == b) & (cc == h), val, acc)
        o_ref[...] = acc

    return pl.pallas_call(
        body, name=name,
        in_specs=[pl.BlockSpec(memory_space=pltpu.VMEM), pl.BlockSpec(memory_space=pltpu.VMEM)],
        out_specs=pl.BlockSpec(memory_space=pltpu.VMEM),
        out_shape=jax.ShapeDtypeStruct((N_BUCKETS, LANES), F32),
    )(dbias, ids)


def _swa_penalty(n):
    shape = (LANES, 3 * LANES)
    tl = lax.broadcasted_iota(jnp.int32, shape, 0)
    col = lax.broadcasted_iota(jnp.int32, shape, 1)
    sl = col & (LANES - 1)
    nv = jnp.full(shape, n, jnp.int32)
    is_meta = sl >= PAD_FRONT
    prev = (col < LANES) & (sl > tl) & (nv >= 1) & ((nv >= 2) | is_meta)
    cur = (col >= LANES) & (col < 2 * LANES) & (sl <= tl) & ((nv >= 1) | is_meta)
    meta = (col >= 2 * LANES) & is_meta & ((nv >= 2) | ((nv == 1) & (sl <= tl)))
    return jnp.where(prev | cur | meta, 0.0, NEG)


def _swa_keys(ref, n):
    off_prev = pl.multiple_of(jnp.maximum(n - 1, 0) * LANES, LANES)
    off_cur = pl.multiple_of(n * LANES, LANES)
    return jnp.concatenate([ref[pl.ds(off_prev, LANES), :], ref[pl.ds(off_cur, LANES), :], ref[0:LANES, :]], axis=0)


def swa_fwd(q, k, v, bias, sinks, name):
    m = q.shape[0]

    def body(q_ref, k_ref, v_ref, bias_ref, sink_ref, o_ref, lse_ref):
        n = pl.program_id(0)
        lane1 = lax.broadcasted_iota(jnp.int32, (1, LANES), 1)
        lane_t = lax.broadcasted_iota(jnp.int32, (LANES, LANES), 1)
        in_head = [lane1 < HEAD_DIM, lane1 >= HEAD_DIM]
        kall = _swa_keys(k_ref, n)
        vall = _swa_keys(v_ref, n)
        vs = [jnp.where(in_head[g], vall, jnp.zeros_like(vall)) for g in (0, 1)]
        penalty = _swa_penalty(n)
        lse = jnp.zeros((LANES, LANES), F32)
        for b in range(4):
            qb = q_ref[:, b * LANES:(b + 1) * LANES]
            ob = jnp.zeros((LANES, LANES), F32)
            for g in (0, 1):
                h = 4 * g + b
                qe = jnp.where(in_head[g], qb, jnp.zeros_like(qb))
                s = _dot_nt(qe, kall) + bias_ref[h] + penalty
                sink = sink_ref[h]
                mx = jnp.maximum(jnp.max(s, axis=1, keepdims=True), sink)
                p = jnp.exp(s - mx)
                den = jnp.sum(p, axis=1, keepdims=True) + jnp.exp(sink - mx)
                ob = ob + _dot((p / den).astype(BF16), vs[g])
                lse = jnp.where(lane_t == h, mx + jnp.log(den), lse)
            o_ref[:, b * LANES:(b + 1) * LANES] = ob
        lse_ref[...] = lse

    return pl.pallas_call(
        body, name=name, grid=(m // LANES,),
        in_specs=[pl.BlockSpec((LANES, 512), lambda n: (n, 0)),
                  pl.BlockSpec((m, LANES), lambda n: (0, 0)), pl.BlockSpec((m, LANES), lambda n: (0, 0)),
                  pl.BlockSpec((8, LANES, 3 * LANES), lambda n: (0, 0, 0)),
                  pl.BlockSpec(memory_space=pltpu.SMEM)],
        out_specs=[pl.BlockSpec((LANES, 512), lambda n: (n, 0)), pl.BlockSpec((LANES, LANES), lambda n: (n, 0))],
        out_shape=[jax.ShapeDtypeStruct((m, 512), F32), jax.ShapeDtypeStruct((m, LANES), F32)],
        compiler_params=_params(1),
    )(q, k, v, bias, sinks)


def swa_bwd(q, k, v, bias, sinks, o, lse, do, name):
    m = q.shape[0]

    def body(q_ref, do_ref, o_ref, lse_ref, k_ref, v_ref, bias_ref, sink_ref,
             dq_ref, dk_ref, dv_ref, dbias_ref, dsink_ref):
        n = pl.program_id(0)

        @pl.when(n == 0)
        def _():
            for r in (dk_ref, dv_ref, dbias_ref, dsink_ref):
                r[...] = jnp.zeros_like(r)

        lane1 = lax.broadcasted_iota(jnp.int32, (1, LANES), 1)
        lane_t = lax.broadcasted_iota(jnp.int32, (LANES, LANES), 1)
        in_head = [lane1 < HEAD_DIM, lane1 >= HEAD_DIM]
        off_prev = pl.multiple_of(jnp.maximum(n - 1, 0) * LANES, LANES)
        off_cur = pl.multiple_of(n * LANES, LANES)
        kall = _swa_keys(k_ref, n)
        vall = _swa_keys(v_ref, n)
        ks = [jnp.where(in_head[g], kall, jnp.zeros_like(kall)) for g in (0, 1)]
        penalty = _swa_penalty(n)
        lsev = lse_ref[...]
        dsink = dsink_ref[...]
        dkall = jnp.zeros((3 * LANES, LANES), F32)
        dvall = jnp.zeros((3 * LANES, LANES), F32)
        for b in range(4):
            sl = slice(b * LANES, (b + 1) * LANES)
            qb = q_ref[:, sl]
            dob = do_ref[:, sl]
            prod = dob * o_ref[:, sl]
            dqb = jnp.zeros((LANES, LANES), F32)
            for g in (0, 1):
                h = 4 * g + b
                qe = jnp.where(in_head[g], qb, jnp.zeros_like(qb))
                doe = jnp.where(in_head[g], dob, 0.0).astype(BF16)
                delta = jnp.sum(jnp.where(in_head[g], prod, 0.0), axis=1, keepdims=True)
                lse_h = _lane_col(lsev, lane_t, h)
                s = _dot_nt(qe, kall) + bias_ref[h] + penalty
                p = jnp.exp(s - lse_h)
                ds = p * (_dot_nt(doe, vall) - delta)
                dbias_ref[h] += ds
                sink_part = jnp.sum(-jnp.exp(sink_ref[h] - lse_h) * delta, keepdims=True)
                dsink = jnp.where(lane1 == h, dsink + sink_part, dsink)
                dsb = ds.astype(BF16)
                dqb = dqb + _dot(dsb, ks[g])
                dkall = dkall + _dot_tn(dsb, qe)
                dvall = dvall + _dot_tn(p.astype(BF16), doe)
            dq_ref[:, sl] = dqb
        dsink_ref[...] = dsink
        for ref, val in ((dk_ref, dkall), (dv_ref, dvall)):
            ref[pl.ds(off_prev, LANES), :] += val[0:LANES]
            ref[pl.ds(off_cur, LANES), :] += val[LANES:2 * LANES]
            ref[0:LANES, :] += val[2 * LANES:3 * LANES]

    blk = pl.BlockSpec((LANES, 512), lambda n: (n, 0))
    full = pl.BlockSpec((m, LANES), lambda n: (0, 0))
    return pl.pallas_call(
        body, name=name, grid=(m // LANES,),
        in_specs=[blk, blk, blk, pl.BlockSpec((LANES, LANES), lambda n: (n, 0)), full, full,
                  pl.BlockSpec((8, LANES, 3 * LANES), lambda n: (0, 0, 0)),
                  pl.BlockSpec(memory_space=pltpu.SMEM)],
        out_specs=[blk, full, full, pl.BlockSpec((8, LANES, 3 * LANES), lambda n: (0, 0, 0)),
                   pl.BlockSpec((1, LANES), lambda n: (0, 0))],
        out_shape=[jax.ShapeDtypeStruct((m, 512), F32), jax.ShapeDtypeStruct((m, LANES), F32),
                   jax.ShapeDtypeStruct((m, LANES), F32), jax.ShapeDtypeStruct((8, LANES, 3 * LANES), F32),
                   jax.ShapeDtypeStruct((1, LANES), F32)],
        compiler_params=_params(1),
    )(q, do, o, lse, k, v, bias, sinks)


def branch_out(h, o_fox, o_swa, proj, wbf, wbs, wo, cols, name):
    m, d = h.shape
    tm = _row_tile(m)

    def body(h_ref, of_ref, os_ref, ga_ref, gb_ref, wbf_ref, wbs_ref, wo_ref, hn_ref):
        tf = _dot(of_ref[...].astype(BF16), wbf_ref[...])
        ts = _dot(os_ref[...].astype(BF16), wbs_ref[...])
        y = jax.nn.sigmoid(ga_ref[...]) * tf + jax.nn.sigmoid(gb_ref[...]) * ts
        hn_ref[...] = h_ref[...] + _dot(y.astype(BF16), wo_ref[...])

    row = lambda w, o=0: pl.BlockSpec((tm, w), lambda r, o=o: (r, o))
    res = lambda a: pl.BlockSpec(a.shape, lambda r: (0, 0))
    return pl.pallas_call(
        body, name=name, grid=(m // tm,),
        in_specs=[row(d), row(1024), row(512), row(d, cols.ga // d), row(d, cols.gb // d), res(wbf), res(wbs), res(wo)],
        out_specs=row(d),
        out_shape=jax.ShapeDtypeStruct((m, d), F32),
        compiler_params=_params(1),
    )(h, o_fox, o_swa, proj, proj, wbf, wbs, wo)


def branch_out_bwd(dh, o_fox, o_swa, proj, wbf, wbs, wo, cols, name):
    m, d = dh.shape
    tm = _row_tile(m)

    def body(dh_ref, of_ref, os_ref, ga_ref, gb_ref, wbf_ref, wbs_ref, wo_ref,
             y_ref, dtf_ref, dts_ref, dga_ref, dgb_ref, dof_ref, dos_ref, delta_ref):
        dy = _dot_nt(dh_ref[...].astype(BF16), wo_ref[...])
        tf = _dot(of_ref[...].astype(BF16), wbf_ref[...])
        ts = _dot(os_ref[...].astype(BF16), wbs_ref[...])
        sa = jax.nn.sigmoid(ga_ref[...])
        sb = jax.nn.sigmoid(gb_ref[...])
        y_ref[...] = (sa * tf + sb * ts).astype(BF16)
        dtf = (dy * sa).astype(BF16)
        dts = (dy * sb).astype(BF16)
        dtf_ref[...] = dtf
        dts_ref[...] = dts
        dga_ref[...] = (dy * tf * sa * (1.0 - sa)).astype(BF16)
        dgb_ref[...] = (dy * ts * sb * (1.0 - sb)).astype(BF16)
        dof = _dot_nt(dtf, wbf_ref[...])
        dof_ref[...] = dof.astype(BF16)
        dos_ref[...] = _dot_nt(dts, wbs_ref[...])
        lane = lax.broadcasted_iota(jnp.int32, (tm, LANES), 1)
        delta = jnp.zeros((tm, LANES), F32)
        for hd in range(8):
            sl = slice(hd * LANES, (hd + 1) * LANES)
            delta = jnp.where(lane == hd, jnp.sum(dof[:, sl] * of_ref[:, sl], axis=1, keepdims=True), delta)
        delta_ref[...] = delta

    row = lambda w, o=0: pl.BlockSpec((tm, w), lambda r, o=o: (r, o))
    res = lambda a: pl.BlockSpec(a.shape, lambda r: (0, 0))
    return pl.pallas_call(
        body, name=name, grid=(m // tm,),
        in_specs=[row(d), row(1024), row(512), row(d, cols.ga // d), row(d, cols.gb // d), res(wbf), res(wbs), res(wo)],
        out_specs=[row(d)] * 5 + [row(1024), row(512), row(LANES)],
        out_shape=[jax.ShapeDtypeStruct((m, d), BF16)] * 5 + [jax.ShapeDtypeStruct((m, 1024), BF16),
                   jax.ShapeDtypeStruct((m, 512), F32), jax.ShapeDtypeStruct((m, LANES), F32)],
        compiler_params=_params(1),
    )(dh, o_fox, o_swa, proj, proj, wbf, wbs, wo)


def loss_head(h, target, name):
    m, d = h.shape

    def body(h_ref, t_ref, dh_ref, loss_ref):
        n = pl.program_id(0)

        @pl.when(n == 0)
        def _():
            loss_ref[...] = jnp.zeros_like(loss_ref)
            dh_ref[...] = jnp.zeros_like(dh_ref)

        @pl.when(n > 0)
        def _():
            err = h_ref[...] - t_ref[...]
            dh_ref[...] = err * (1.0 / d)
            loss_ref[...] += jnp.sum(err * err, keepdims=True) * (0.5 / d)

    return pl.pallas_call(
        body, name=name, grid=(m // LANES,),
        in_specs=[pl.BlockSpec((LANES, d), lambda n: (n, 0)),
                  pl.BlockSpec((LANES, d), lambda n: (jnp.maximum(n - 1, 0), 0))],
        out_specs=[pl.BlockSpec((LANES, d), lambda n: (n, 0)), pl.BlockSpec((8, LANES), lambda n: (0, 0))],
        out_shape=[jax.ShapeDtypeStruct((m, d), F32), jax.ShapeDtypeStruct((8, LANES), F32)],
        compiler_params=_params(1),
    )(h, target)


def _adamw_math(w, g, m, v):
    m = ADAM_B1 * m + (1.0 - ADAM_B1) * g
    v = ADAM_B2 * v + (1.0 - ADAM_B2) * (g * g)
    m_hat = m / (1.0 - ADAM_B1 ** ADAM_STEP)
    v_hat = v / (1.0 - ADAM_B2 ** ADAM_STEP)
    delta = -ADAM_LR * (m_hat / (jnp.sqrt(v_hat) + ADAM_EPS) + ADAM_WD * w)
    return delta, m, v


def adamw_sum(parts, w, m, v, name, after=None):
    n_layers, a, b = w.shape
    ta = next(t for t in (256, 176, 128, a) if a % t == 0)
    nr = a // ta

    def body(*refs):
        p_refs = refs[:n_layers]
        w_ref, m_ref, v_ref = refs[n_layers:n_layers + 3]
        g_o, d_o, m_o, v_o = refs[-4:]
        for l in range(n_layers):
            @pl.when(pl.program_id(0) == l)
            def _(l=l):
                g = p_refs[l][0].astype(F32)
                for j in range(1, N_DEV):
                    g = g + p_refs[l][j].astype(F32)
                g_o[0] = g
                d_o[0], m_o[0], v_o[0] = _adamw_math(w_ref[0], g, m_ref[0], v_ref[0])

    def part_spec(l):
        return pl.BlockSpec((N_DEV, ta, b), lambda i, r, l=l: (0, jnp.where(i == l, r, jnp.where(i < l, 0, nr - 1)), 0))

    row = pl.BlockSpec((1, ta, b), lambda i, r: (i, r, 0))
    return pl.pallas_call(
        body, name=name, grid=(n_layers, nr),
        in_specs=[part_spec(l) for l in range(n_layers)] + [row, row, row]
                 + ([pl.BlockSpec(memory_space=pl.ANY)] if after is not None else []),
        out_specs=[row] * 4,
        out_shape=[jax.ShapeDtypeStruct(w.shape, F32)] * 4,
        compiler_params=_params(2),
    )(*parts, w, m, v, *([after] if after is not None else []))


def adamw_sum_cols(parts, w, m, v, name, after=None):
    n_layers = len(parts)
    a, b = parts[0].shape[1:]
    tc = 512 if b % 512 == 0 else b
    nc = b // tc

    def body(*refs):
        p_refs = refs[:n_layers]
        w_ref, m_ref, v_ref = refs[n_layers:n_layers + 3]
        g_o, d_o, m_o, v_o = refs[-4:]
        for l in range(n_layers):
            @pl.when(pl.program_id(0) == l)
            def _(l=l):
                g = p_refs[l][0].astype(F32)
                for j in range(1, N_DEV):
                    g = g + p_refs[l][j].astype(F32)
                g_o[...] = g
                d_o[...], m_o[...], v_o[...] = _adamw_math(w_ref[...], g, m_ref[...], v_ref[...])

    def part_spec(l):
        return pl.BlockSpec((N_DEV, a, tc), lambda i, c, l=l: (0, 0, jnp.where(i == l, c, jnp.where(i < l, 0, nc - 1))))

    col = pl.BlockSpec((a, tc), lambda i, c: (0, i * nc + c))
    return pl.pallas_call(
        body, name=name, grid=(n_layers, nc),
        in_specs=[part_spec(l) for l in range(n_layers)] + [col, col, col]
                 + ([pl.BlockSpec(memory_space=pl.ANY)] if after is not None else []),
        out_specs=[col] * 4,
        out_shape=[jax.ShapeDtypeStruct(w.shape, F32)] * 4,
        compiler_params=_params(2),
    )(*parts, w, m, v, *([after] if after is not None else []))


def adamw_small(g, w, m, v, name):
    def body(g_ref, w_ref, m_ref, v_ref, d_o, m_o, v_o):
        d_o[...], m_o[...], v_o[...] = _adamw_math(w_ref[...], g_ref[...], m_ref[...], v_ref[...])

    spec = pl.BlockSpec(memory_space=pltpu.VMEM)
    return pl.pallas_call(
        body, name=name, in_specs=[spec] * 4, out_specs=[spec] * 3,
        out_shape=[jax.ShapeDtypeStruct(w.shape, F32)] * 3,
    )(g, w, m, v)


BIG = ("ffn1_w_in", "ffn1_w_out", "w_in", "w_branch_fox", "w_branch_swa", "w_out", "ffn2_w_in", "ffn2_w_out")
SMALL = ("rel_bias_table", "ffn1_norm", "mix_norm", "forget_bias", "fox_q_norm", "fox_k_norm",
         "swa_q_norm", "swa_k_norm", "swa_sinks", "ffn2_norm")
WEIGHTS = ("meta_tokens", "rel_bias_table", "ffn1_norm", "ffn1_w_in", "ffn1_w_out", "mix_norm", "w_in",
           "forget_bias", "fox_q_norm", "fox_k_norm", "swa_q_norm", "swa_k_norm", "swa_sinks", "w_branch_fox",
           "w_branch_swa", "w_out", "ffn2_norm", "ffn2_w_in", "ffn2_w_out")


def _pack(arrs, width, row_multiple, dtype):
    lead = arrs[0].shape[:-1]
    flat = jnp.concatenate([a.astype(dtype) for a in arrs], axis=-1)
    n = flat.shape[-1]
    rows = -(-n // width)
    rows = -(-rows // row_multiple) * row_multiple
    flat = jnp.pad(flat, [(0, 0)] * len(lead) + [(0, rows * width - n)])
    return flat.reshape(lead + (rows, width))


def _unpack(flat, shapes):
    flat = flat.reshape(-1)
    out, off = [], 0
    for s in shapes:
        n = int(np.prod(s))
        out.append(flat[off:off + n].reshape(s))
        off += n
    return out


def _swa_head_order():
    return [4 * (j % 2) + j // 2 for j in range(8)]


def _permute_heads(a, axis, inverse=False):
    order = _swa_head_order()
    if inverse:
        order = [order.index(hd) for hd in range(8)]
    parts = [lax.slice_in_dim(a, hd * HEAD_DIM, (hd + 1) * HEAD_DIM, axis=axis) for hd in order]
    return jnp.concatenate(parts, axis=axis)


def _pad_heads(a):
    return jnp.pad(a.reshape(8, HEAD_DIM, -1), ((0, 0), (0, LANES - HEAD_DIM), (0, 0))).reshape(8 * LANES, -1)


def _unpad_heads(a):
    return a.reshape(8, LANES, -1)[:, :HEAD_DIM].reshape(8 * HEAD_DIM, -1)


def _swa_rows(a, inverse=False):
    shape = (4, 2) if inverse else (2, 4)
    return a.reshape(shape + (HEAD_DIM, -1)).transpose(1, 0, 2, 3).reshape(a.shape)


def _w_in_rows(d):
    return np.cumsum([0, 512, 512, 512, 8, 512, 128, 128, d, d])


def _reorder_w_in(wt, cols):
    o = _w_in_rows(cols.d)
    qa, ka, va, fa, qb, kb, vb, ga, gb = [wt[o[i]:o[i + 1]] for i in range(9)]
    zeros = jnp.zeros((cols.np - cols.fa - 8, wt.shape[1]), wt.dtype)
    return jnp.concatenate([ga, gb, _pad_heads(qa), _pad_heads(ka), _pad_heads(va), _swa_rows(qb), kb, vb, fa, zeros],
                           axis=0)


def _restore_w_in(wpt, cols, width):
    seg = lambda off, n: wpt[off:off + n]
    rows = jnp.concatenate([_unpad_heads(seg(cols.qa, 1024)), _unpad_heads(seg(cols.ka, 1024)),
                            _unpad_heads(seg(cols.va, 1024)), seg(cols.fa, 8), _swa_rows(seg(cols.qb, 512), True),
                            seg(cols.kb, 128), seg(cols.vb, 128), seg(cols.ga, cols.d), seg(cols.gb, cols.d)], axis=0)
    return rows.reshape(N_DEV, width, -1)


def _lane_pad(v):
    return jnp.pad(v, ((0, 0), (0, LANES - v.shape[1])))


def kernel(x, meta_tokens, rel_bias_table, ffn1_norm, ffn1_w_in, ffn1_w_out, mix_norm, w_in, forget_bias, fox_q_norm, fox_k_norm, swa_q_norm, swa_k_norm, swa_sinks, w_branch_fox, w_branch_swa, w_out, ffn2_norm, ffn2_w_in, ffn2_w_out, loss_target, m_meta_tokens, m_rel_bias_table, m_ffn1_norm, m_ffn1_w_in, m_ffn1_w_out, m_mix_norm, m_w_in, m_forget_bias, m_fox_q_norm, m_fox_k_norm, m_swa_q_norm, m_swa_k_norm, m_swa_sinks, m_w_branch_fox, m_w_branch_swa, m_w_out, m_ffn2_norm, m_ffn2_w_in, m_ffn2_w_out, v_meta_tokens, v_rel_bias_table, v_ffn1_norm, v_ffn1_w_in, v_ffn1_w_out, v_mix_norm, v_w_in, v_forget_bias, v_fox_q_norm, v_fox_k_norm, v_swa_q_norm, v_swa_k_norm, v_swa_sinks, v_w_branch_fox, v_w_branch_swa, v_w_out, v_ffn2_norm, v_ffn2_w_in, v_ffn2_w_out):
    args = dict(locals())
    wts = {n: args[n] for n in WEIGHTS}
    mom1 = {n: args["m_" + n] for n in WEIGHTS}
    mom2 = {n: args["v_" + n] for n in WEIGHTS}

    seq, d = x.shape[1], x.shape[2]
    m_rows = seq + LANES
    depth = ffn1_norm.shape[0]
    fb = ffn1_w_in.shape[2]
    fo = ffn1_w_out.shape[1]
    din_shard = w_in.shape[2]
    cols = _Cols(d)
    scale = HEAD_DIM ** -0.5
    dev = 4 * lax.axis_index("x") + 2 * lax.axis_index("y") + lax.axis_index("c")

    groups = {"ffn1": ("ffn1_w_in", "ffn1_w_out"), "mix": ("w_in", "w_branch_fox", "w_branch_swa", "w_out"),
              "ffn2": ("ffn2_w_in", "ffn2_w_out"), "ffn1_in": ("ffn1_w_in",), "ffn1_out": ("ffn1_w_out",),
              "ffn2_in": ("ffn2_w_in",), "ffn2_out": ("ffn2_w_out",)}
    flipped = ("ffn1_w_in", "ffn2_w_in")
    for n in flipped:
        wts[n], mom1[n], mom2[n] = (jnp.swapaxes(a, 1, 2) for a in (wts[n], mom1[n], mom2[n]))
    to_rows = lambda a: jnp.transpose(a, (2, 0, 1)).reshape(din_shard, depth * d)
    from_rows = lambda a: jnp.transpose(a.reshape(din_shard, depth, d), (1, 2, 0))
    wts["w_in"], mom1["w_in"], mom2["w_in"] = (to_rows(a) for a in (wts["w_in"], mom1["w_in"], mom2["w_in"]))
    shard = {n: wts[n].astype(BF16) for n in BIG}
    w_in_rows = shard.pop("w_in")
    shard["w_in"] = [w_in_rows[:, l * d:(l + 1) * d] for l in range(depth)]
    full, parts, gw = {}, {}, {}

    def keys_of(stages):
        return [(n, l) for g, l in stages if l < depth for n in groups[g]]

    def gather_rider(stages):
        return Rider([shard[n][l] for n, l in keys_of(stages)], True)

    def scatter_rider(stages):
        return Rider([gw[k] for k in keys_of(stages)], False)

    def ffn_weights(tag, l):
        return full[tag + "_w_in", l], full[tag + "_w_out", l].reshape(4, fb, d)

    def mixer_weights(l):
        wp = _reorder_w_in(full["w_in", l].reshape(N_DEV * din_shard, d), cols)
        wbf = jnp.concatenate([full["w_branch_fox", l][j] for j in range(N_DEV)], axis=1)
        wbf = jnp.pad(wbf.reshape(8, HEAD_DIM, d), ((0, 0), (0, LANES - HEAD_DIM), (0, 0))).reshape(8 * LANES, d)
        wbs = _permute_heads(jnp.concatenate([full["w_branch_swa", l][j] for j in range(N_DEV)], axis=1), 0)
        return wp, wbf, wbs, full["w_out", l].reshape(d, d)

    full.update(zip(keys_of([("ffn1", 0)]), exchange_hbm(gather_rider([("ffn1", 0)]).srcs, True, "gather_first")))
    meta_all = gather_small(meta_tokens.reshape(1, N_META, -1), "gather_meta")
    meta_full = meta_all.transpose(1, 0, 2).reshape(N_META, d)
    tile8 = lambda g, s=1.0: jnp.tile(g.reshape(1, HEAD_DIM) * s, (1, 8))
    tile2 = lambda g: jnp.tile(g.reshape(1, HEAD_DIM), (1, 2))
    data_lanes = lambda g, s=1.0: _lane_pad(g.reshape(1, HEAD_DIM) * s)
    bias = bias_build(rel_bias_table, "swa_bias")

    first = jnp.concatenate([jnp.zeros((PAD_FRONT, d), F32), meta_full], axis=0)
    h = jnp.concatenate([first, x[0]], axis=0)
    saved, lw = [], []
    for l in range(depth):
        s, w = {"h0": h}, {}
        w["ffn1_in"], w["ffn1_out"] = ffn_weights("ffn1", l)
        stages = [("mix", l)]
        (h, s["n1"], s["a1"], s["fg1"], s["fu1"]), got = ffn_fwd(h, ffn1_norm[l:l + 1], w["ffn1_in"], w["ffn1_out"],
                                                          f"ffn1_fwd_{l}", gather_rider(stages))
        full.update(zip(keys_of(stages), got))
        s["h1"] = h
        w["wp"], w["wbf"], w["wbs"], w["wo"] = mixer_weights(l)
        s["nm"], s["proj"] = mixer_proj(h, mix_norm[l:l + 1], w["wp"], f"mixer_proj_{l}")
        s["gains"] = (data_lanes(fox_q_norm[l], scale), data_lanes(fox_k_norm[l]), tile8(swa_q_norm[l], scale),
                      tile2(swa_k_norm[l]))
        s["fbias"] = _lane_pad(forget_bias[l:l + 1])
        qf, kf, vf, kt, vt, qb, kb, vb = qk_post(s["proj"], s["gains"], s["fbias"], cols, f"qk_post_{l}")
        s.update(qf=qf, kf=kf, vf=vf, kt=kt, qb=qb, kb=kb, vb=vb)
        stages = [("ffn2", l)]
        (s["o_fox"], s["lse_fox"]), got = fox_fwd(qf, kf, vt, f"fox_fwd_{l}", gather_rider(stages))
        full.update(zip(keys_of(stages), got))
        s["o_swa"], s["lse_swa"] = swa_fwd(qb, kb, vb, bias, swa_sinks[l], f"swa_fwd_{l}")
        h = branch_out(h, s["o_fox"], s["o_swa"], s["proj"], w["wbf"], w["wbs"], w["wo"], cols, f"branch_out_{l}")
        s["h2"] = h
        w["ffn2_in"], w["ffn2_out"] = ffn_weights("ffn2", l)
        stages = [("ffn1", l + 1)]
        (h, s["n2"], s["a2"], s["fg2"], s["fu2"]), got = ffn_fwd(h, ffn2_norm[l:l + 1], w["ffn2_in"], w["ffn2_out"],
                                                          f"ffn2_fwd_{l}", gather_rider(stages))
        full.update(zip(keys_of(stages), got))
        saved.append(s)
        lw.append(w)

    dh, loss_part = loss_head(h, loss_target[0], "loss_head")

    gs = {n: [None] * depth for n in SMALL}
    dbias_total = None
    for l in reversed(range(depth)):
        w, s = lw[l], saved[l]

        def ffn_back(dh, tag, hin, norm, n_in, a, f_gate, f_up, stages):
            (dh_in, dg, du, dgn, dhs), got = ffn_bwd(dh, hin, norm, f_gate, f_up, w[tag + "_in"], w[tag + "_out"],
                                                     f"{tag}_bwd_{l}", scatter_rider(stages))
            parts.update(zip(keys_of(stages), got))
            gw[tag + "_w_out", l] = matmul_tn(a, dhs[None], f"{tag}_dwo_{l}").reshape(N_DEV, fo, d)
            stages = [(tag + "_out", l)]
            gw[tag + "_w_in", l], got = matmul_tn(dg, n_in[None], f"{tag}_dwi_{l}", x2=du,
                                                  rider=scatter_rider(stages))
            parts.update(zip(keys_of(stages), got))
            return dh_in, dgn

        dh, gs["ffn2_norm"][l] = ffn_back(dh, "ffn2", s["h2"], ffn2_norm[l:l + 1], s["n2"], s["a2"], s["fg2"],
                                          s["fu2"], [("ffn1_in", l + 1)])

        y, dtf, dts, dga, dgb, dof, dos, delta = branch_out_bwd(dh, s["o_fox"], s["o_swa"], s["proj"], w["wbf"],
                                                                w["wbs"], w["wo"], cols, f"branch_out_bwd_{l}")
        gw["w_out", l] = matmul_tn(y[None], dh[None], f"dw_out_{l}").reshape(N_DEV, d // N_DEV, d)
        to_shards = lambda a: a.reshape(512, N_DEV, d // N_DEV).transpose(1, 0, 2)
        gw["w_branch_fox", l] = to_shards(matmul_tn(s["o_fox"][None], dtf[None], f"dw_branch_fox_{l}")[0]
                                          .reshape(8, LANES, d)[:, :HEAD_DIM].reshape(512, d))
        gw["w_branch_swa", l] = to_shards(_permute_heads(
            matmul_tn(s["o_swa"][None], dts[None], f"dw_branch_swa_{l}")[0], 0, inverse=True))

        stages = [("ffn2_in", l)]
        (dkf, dvf, dqf, dc_rows), got = fox_bwd(s["qf"], s["kf"], s["vf"], s["kt"], dof, s["lse_fox"],
                                         delta[:, :8].T.reshape(8, 1, m_rows), f"fox_bwd_{l}", scatter_rider(stages))
        parts.update(zip(keys_of(stages), got))
        dc = _lane_pad(dc_rows.reshape(8, m_rows).T)
        dqb, dkb, dvb, dbias, dsink = swa_bwd(s["qb"], s["kb"], s["vb"], bias, swa_sinks[l], s["o_swa"], s["lse_swa"],
                                              dos, f"swa_bwd_{l}")
        dbias_total = dbias if dbias_total is None else dbias_total + dbias
        gs["swa_sinks"][l] = dsink[0, :8]
        dproj, ggqa, ggka, ggqb, ggkb, gfb = qk_post_bwd(s["proj"], s["gains"], s["fbias"], dqf, dkf, dvf, dqb, dkb,
                                                         dvb, dc, dga, dgb, cols, f"qk_post_bwd_{l}")
        gs["fox_q_norm"][l] = ggqa[0, :HEAD_DIM] * scale
        gs["fox_k_norm"][l] = ggka[0, :HEAD_DIM]
        gs["swa_q_norm"][l] = ggqb.reshape(8, HEAD_DIM).sum(0) * scale
        gs["swa_k_norm"][l] = ggkb.reshape(2, HEAD_DIM).sum(0)
        gs["forget_bias"][l] = gfb[0, :8]
        dwp = matmul_tn(dproj[None], s["nm"][None], f"dw_in_{l}", tk=1024 if cols.np % 1024 == 0 else cols.np)[0]
        gw["w_in", l] = _restore_w_in(dwp, cols, din_shard)
        dh, gs["mix_norm"][l] = dproj_bwd(dh, s["h1"], mix_norm[l:l + 1], dproj, w["wp"], f"dproj_bwd_{l}")

        dh, gs["ffn1_norm"][l] = ffn_back(dh, "ffn1", s["h0"], ffn1_norm[l:l + 1], s["n1"], s["a1"], s["fg1"],
                                          s["fu1"], [("mix", l)])

    grad_x = dh[LANES:][None]
    dmeta = dh[PAD_FRONT:LANES]
    dtable = bias_reduce(dbias_total, "swa_dbias")[:, :8]

    last = ("ffn1_w_in", 0)
    send_sems, recv_sems, src_thru, land_thru, token = scatter_start(gw[last], "scatter_last_start")
    big_out = [{}, {}, {}, {}]
    for n in BIG:
        if n != last[0]:
            update = adamw_sum_cols if n == "w_in" else adamw_sum
            outs = update([parts[n, l] for l in range(depth)], wts[n], mom1[n], mom2[n], f"adamw_{n}", after=token)
            for k in range(4):
                big_out[k][n] = outs[k]
    sent, landed = scatter_wait(send_sems, recv_sems, src_thru, land_thru,
                                [big_out[1][n] for n in BIG if n != last[0]], "scatter_last_wait")
    parts[last] = lax.dynamic_update_slice_in_dim(landed, lax.dynamic_slice_in_dim(sent, dev, 1, axis=0), dev, axis=0)
    outs = adamw_sum([parts[last[0], l] for l in range(depth)], wts[last[0]], mom1[last[0]], mom2[last[0]],
                     f"adamw_{last[0]}")
    for k in range(4):
        big_out[k][last[0]] = outs[k]

    small_g = {n: (jnp.stack(gs[n]) if n != "rel_bias_table" else None) for n in SMALL}
    small_g["rel_bias_table"] = dtable
    pieces = [loss_part[0:1, 0:1].reshape(1, 1)] + [small_g[n].reshape(1, -1) for n in SMALL] + [dmeta.reshape(1, -1)]
    small_shapes = [(1,)] + [wts[n].shape for n in SMALL] + [(N_META, d)]
    total = allsum_small(_pack(pieces, LANES, 8, F32), "allsum_small")
    summed = _unpack(total, small_shapes)
    loss = summed[0][0]
    g_small = dict(zip(SMALL, summed[1:1 + len(SMALL)]))
    g_meta = lax.dynamic_slice_in_dim(summed[-1], dev * (d // N_DEV), d // N_DEV, axis=1)
    names = SMALL + ("meta_tokens",)
    g_small["meta_tokens"] = g_meta
    pk = lambda src: _pack([src[n].reshape(1, -1) for n in names], LANES, 8, F32)[0]
    small_out = [dict(zip(names, _unpack(o, [wts[n].shape for n in names])))
                 for o in adamw_small(pk(g_small), pk(wts), pk(mom1), pk(mom2), "adamw_small")]

    for out in big_out:
        for n in flipped:
            out[n] = jnp.swapaxes(out[n], 1, 2)
        out["w_in"] = from_rows(out["w_in"])
    grads = {**big_out[0], **g_small}
    delta = {**big_out[1], **small_out[0]}
    new_m = {**big_out[2], **small_out[1]}
    new_v = {**big_out[3], **small_out[2]}
    return (loss, grad_x, *[grads[n] for n in WEIGHTS], *[delta[n] for n in WEIGHTS],
            *[new_m[n] for n in WEIGHTS], *[new_v[n] for n in WEIGHTS])
```

```python
import math

import numpy as np
import jax
import jax.numpy as jnp
from jax import lax
from jax.experimental import pallas as pl
from jax.experimental.pallas import tpu as pltpu

F32 = jnp.float32
BF16 = jnp.bfloat16
EPS = 1e-6
NEG = -1e30
HEAD_DIM = 64
LANES = 128
N_META = 16
PAD_FRONT = LANES - N_META
N_BUCKETS = 32
MAX_DISTANCE = 128
N_DEV = 8
ADAM_LR, ADAM_B1, ADAM_B2, ADAM_EPS, ADAM_WD, ADAM_STEP = 0.001, 0.9, 0.999, 1e-08, 0.01, 10
VMEM_LIMIT = 56 * 1024 * 1024
MESH = pl.DeviceIdType.MESH


def _params(n_grid):
    return pltpu.CompilerParams(dimension_semantics=("arbitrary",) * n_grid,
                                vmem_limit_bytes=VMEM_LIMIT)


def _dot(a, b):
    return jnp.dot(a, b, preferred_element_type=F32)


def _dot_nt(a, b):
    return lax.dot_general(a, b, (((1,), (1,)), ((), ())), preferred_element_type=F32)


def _dot_tn(a, b):
    return lax.dot_general(a, b, (((0,), (0,)), ((), ())), preferred_element_type=F32)


def _rms(x):
    r = lax.rsqrt(jnp.mean(x * x, axis=-1, keepdims=True) + EPS)
    return x * r, r


def _rms_bwd(x, g, dn):
    xh, r = _rms(x)
    dxh = dn * g
    dx = r * (dxh - xh * jnp.mean(dxh * xh, axis=-1, keepdims=True))
    return dx, jnp.sum(dn * xh, axis=0, keepdims=True)


def _split2(v):
    hi = v.astype(BF16)
    return hi, (v - hi.astype(F32)).astype(BF16)


def _split3(v):
    hi = v.astype(BF16)
    r1 = v - hi.astype(F32)
    mid = r1.astype(BF16)
    return hi, mid, (r1 - mid.astype(F32)).astype(BF16)


def _group_ones():
    r = lax.broadcasted_iota(jnp.int32, (LANES, LANES), 0) // HEAD_DIM
    c = lax.broadcasted_iota(jnp.int32, (LANES, LANES), 1) // HEAD_DIM
    return jnp.where(r == c, 1.0, 0.0).astype(BF16)


def _group_mean(v, ones):
    hi, lo = _split2(v)
    return (_dot(hi, ones) + _dot(lo, ones)) * (1.0 / HEAD_DIM)


def _row_tile(m):
    return 384 if m % 384 == 0 else LANES


def _tile(m, cap):
    return max(t for t in range(16, cap + 1, 16) if m % t == 0)


def _peer(k):
    x, y, c = lax.axis_index("x"), lax.axis_index("y"), lax.axis_index("c")
    px = 1 - x if k & 4 else x
    py = 1 - y if k & 2 else y
    pc = 1 - c if k & 1 else c
    return (px, py, pc), 4 * px + 2 * py + pc


def _exchange_body(src_ref, dst_ref, send_sems, recv_sems, local_sem, bcast):
    x, y, c = lax.axis_index("x"), lax.axis_index("y"), lax.axis_index("c")
    me = 4 * x + 2 * y + c
    mine = pltpu.make_async_copy(src_ref.at[0 if bcast else me], dst_ref.at[me], local_sem)
    mine.start()
    sends = []
    for k in range(1, N_DEV):
        dev, idx = _peer(k)
        cp = pltpu.make_async_remote_copy(
            src_ref=src_ref.at[0 if bcast else idx], dst_ref=dst_ref.at[me],
            send_sem=send_sems.at[k - 1], recv_sem=recv_sems.at[k - 1],
            device_id=dev, device_id_type=MESH)
        cp.start()
        sends.append(cp)
    for k in range(1, N_DEV):
        dev, idx = _peer(k)
        pltpu.make_async_remote_copy(
            src_ref=src_ref.at[0], dst_ref=dst_ref.at[idx],
            send_sem=send_sems.at[k - 1], recv_sem=recv_sems.at[k - 1],
            device_id=dev, device_id_type=MESH).wait_recv()
    for cp in sends:
        cp.wait_send()
    mine.wait()


class Rider:
    FIRST = (1, 2, 4, 6)
    RELAYED = (2, 4, 6)

    def __init__(self, srcs=(), bcast=True):
        self.srcs, self.bcast, self.n = list(srcs), bcast, len(srcs)

    def out_shapes(self):
        return [jax.ShapeDtypeStruct(((N_DEV,) + s.shape) if self.bcast else s.shape, s.dtype) for s in self.srcs]

    def specs(self):
        return [pl.BlockSpec(memory_space=pl.ANY)] * self.n

    def scratch(self):
        if not self.n:
            return []
        return [pltpu.SemaphoreType.DMA((self.n * (N_DEV - 1),)), pltpu.SemaphoreType.DMA((self.n * (N_DEV - 1),)),
                pltpu.SemaphoreType.DMA((self.n,))]

    @staticmethod
    def _copy(src, dst, a, pair, dev, send_sems, recv_sems):
        sem = a * (N_DEV - 1) + pair - 1
        return pltpu.make_async_remote_copy(src_ref=src, dst_ref=dst, send_sem=send_sems.at[sem],
                                            recv_sem=recv_sems.at[sem], device_id=dev, device_id_type=MESH)

    def _first(self):
        return self.FIRST if self.bcast else range(1, N_DEV)

    def _own(self, s, d, a, local_sems):
        me = 4 * lax.axis_index("x") + 2 * lax.axis_index("y") + lax.axis_index("c")
        return pltpu.make_async_copy(s if self.bcast else s.at[me], d.at[me], local_sems.at[a]), me

    def start(self, src_refs, dst_refs, send_sems, recv_sems, local_sems):
        for a, (s, d) in enumerate(zip(src_refs, dst_refs)):
            own, me = self._own(s, d, a, local_sems)
            own.start()
            for k in self._first():
                dev, idx = _peer(k)
                self._copy(s if self.bcast else s.at[idx], d.at[me], a, k, dev, send_sems, recv_sems).start()

    def relay(self, src_refs, dst_refs, send_sems, recv_sems, local_sems):
        if not self.bcast:
            return
        sibling, _ = _peer(1)
        for a, d in enumerate(dst_refs):
            for k in self.RELAYED:
                dev, idx = _peer(k)
                self._copy(d.at[idx], d.at[idx], a, k, dev, send_sems, recv_sems).wait_recv()
                self._copy(d.at[idx], d.at[idx], a, k + 1, sibling, send_sems, recv_sems).start()

    def wait(self, src_refs, dst_refs, send_sems, recv_sems, local_sems):
        sibling, _ = _peer(1)
        for a, (s, d) in enumerate(zip(src_refs, dst_refs)):
            own, me = self._own(s, d, a, local_sems)
            for k in range(1, N_DEV):
                if not (self.bcast and k in self.RELAYED):
                    dev, idx = _peer(k)
                    self._copy(d.at[idx], d.at[idx], a, k, dev, send_sems, recv_sems).wait_recv()
            for k in self._first():
                dev, idx = _peer(k)
                self._copy(s if self.bcast else s.at[idx], d.at[me], a, k, dev, send_sems, recv_sems).wait_send()
            if self.bcast:
                for k in self.RELAYED:
                    dev, idx = _peer(k)
                    self._copy(d.at[idx], d.at[idx], a, k + 1, sibling, send_sems, recv_sems).wait_send()
            own.wait()


def rider_call(core, name, grid, in_specs, out_specs, out_shape, scratch_shapes, args, rider=None):
    rider = rider or Rider()
    n_in, n_out, n_scr, nr = len(in_specs), len(out_specs), len(scratch_shapes), rider.n

    def body(*refs):
        ins, r_src = refs[:n_in], refs[n_in:n_in + nr]
        outs = refs[n_in + nr:n_in + nr + n_out]
        r_dst = refs[n_in + nr + n_out:n_in + 2 * nr + n_out]
        scr = refs[n_in + 2 * nr + n_out:n_in + 2 * nr + n_out + n_scr]
        sems = refs[n_in + 2 * nr + n_out + n_scr:]
        if nr:
            first, relay, last = True, True, True
            for ax, size in enumerate(grid):
                first = first & (pl.program_id(ax) == 0)
                relay = relay & (pl.program_id(ax) == (3 * size // 4 if ax == 0 else 0))
                last = last & (pl.program_id(ax) == size - 1)
            if not grid:
                rider.start(r_src, r_dst, *sems)
                rider.relay(r_src, r_dst, *sems)
            else:
                pl.when(first)(lambda: rider.start(r_src, r_dst, *sems))
                if rider.bcast:
                    pl.when(relay)(lambda: rider.relay(r_src, r_dst, *sems))
        core(*ins, *outs, *scr)
        if nr:
            if not grid:
                rider.wait(r_src, r_dst, *sems)
            else:
                pl.when(last)(lambda: rider.wait(r_src, r_dst, *sems))

    res = pl.pallas_call(
        body, name=name, grid=grid,
        in_specs=list(in_specs) + rider.specs(),
        out_specs=list(out_specs) + rider.specs(),
        out_shape=list(out_shape) + rider.out_shapes(),
        scratch_shapes=list(scratch_shapes) + rider.scratch(),
        compiler_params=_params(len(grid)),
    )(*args, *rider.srcs)
    return res[:n_out], res[n_out:]


def exchange_hbm(srcs, bcast, name):
    return rider_call(lambda: None, name, (), [], [], [], [], [], Rider(srcs, bcast))[1]


_HBM = pl.BlockSpec(memory_space=pltpu.HBM)
_SEM = pl.BlockSpec(memory_space=pltpu.SEMAPHORE)
_EFFECT = pltpu.CompilerParams(has_side_effects=pltpu.SideEffectType.DATAFLOW_SIDE_EFFECTING)


def scatter_start(src, name):
    def body(src_ref, land_ref, send_sems, recv_sems, src_thru, land_thru, token):
        me = 4 * lax.axis_index("x") + 2 * lax.axis_index("y") + lax.axis_index("c")
        for k in range(1, N_DEV):
            dev, idx = _peer(k)
            pltpu.make_async_remote_copy(src_ref=src_ref.at[idx], dst_ref=land_ref.at[me], send_sem=send_sems.at[k - 1],
                                         recv_sem=recv_sems.at[k - 1], device_id=dev, device_id_type=MESH).start()
        token[...] = jnp.zeros_like(token)

    return pl.pallas_call(
        body, name=name,
        out_shape=(pltpu.SemaphoreType.DMA((N_DEV - 1,)), pltpu.SemaphoreType.DMA((N_DEV - 1,)),
                   pltpu.HBM(src.shape, src.dtype), pltpu.HBM(src.shape, src.dtype), jax.ShapeDtypeStruct((8, LANES), F32)),
        in_specs=(_HBM, _HBM), out_specs=(_SEM, _SEM, _HBM, _HBM, pl.BlockSpec(memory_space=pltpu.VMEM)),
        input_output_aliases={0: 2, 1: 3}, compiler_params=_EFFECT,
    )(pltpu.with_memory_space_constraint(src, pltpu.HBM),
      pltpu.with_memory_space_constraint(lax.empty(src.shape, src.dtype), pltpu.HBM))


def scatter_wait(send_sems, recv_sems, src_thru, land_thru, after, name):
    n_after = len(after)

    def body(*refs):
        src_ref, land_ref, send_sems, recv_sems = refs[:4]
        for k in range(1, N_DEV):
            dev, idx = _peer(k)
            copy = pltpu.make_async_remote_copy(src_ref=src_ref.at[idx], dst_ref=land_ref.at[idx],
                                                send_sem=send_sems.at[k - 1], recv_sem=recv_sems.at[k - 1],
                                                device_id=dev, device_id_type=MESH)
            copy.wait_send()
            copy.wait_recv()

    return pl.pallas_call(
        body, name=name,
        out_shape=(pltpu.HBM(src_thru.shape, src_thru.dtype), pltpu.HBM(land_thru.shape, land_thru.dtype)),
        in_specs=(_HBM, _HBM, _SEM, _SEM) + (pl.BlockSpec(memory_space=pl.ANY),) * n_after, out_specs=(_HBM, _HBM),
        input_output_aliases={0: 0, 1: 1}, compiler_params=_EFFECT,
    )(src_thru, land_thru, send_sems, recv_sems, *after)


def allsum_small(vec, name):
    def body(src_ref, out_ref, dst_ref, send_sems, recv_sems, local_sem):
        _exchange_body(src_ref, dst_ref, send_sems, recv_sems, local_sem, True)
        acc = dst_ref[0]
        for j in range(1, N_DEV):
            acc = acc + dst_ref[j]
        out_ref[...] = acc

    return pl.pallas_call(
        body, name=name,
        out_shape=jax.ShapeDtypeStruct(vec.shape[1:], F32),
        in_specs=[pl.BlockSpec(memory_space=pltpu.VMEM)],
        out_specs=pl.BlockSpec(memory_space=pltpu.VMEM),
        scratch_shapes=[pltpu.VMEM((N_DEV,) + vec.shape[1:], F32),
                        pltpu.SemaphoreType.DMA((N_DEV - 1,)), pltpu.SemaphoreType.DMA((N_DEV - 1,)),
                        pltpu.SemaphoreType.DMA],
    )(vec)


def gather_small(vec, name):
    def body(src_ref, dst_ref, send_sems, recv_sems, local_sem):
        _exchange_body(src_ref, dst_ref, send_sems, recv_sems, local_sem, True)

    return pl.pallas_call(
        body, name=name,
        out_shape=jax.ShapeDtypeStruct((N_DEV,) + vec.shape[1:], F32),
        in_specs=[pl.BlockSpec(memory_space=pltpu.VMEM)],
        out_specs=pl.BlockSpec(memory_space=pltpu.VMEM),
        scratch_shapes=[pltpu.SemaphoreType.DMA((N_DEV - 1,)), pltpu.SemaphoreType.DMA((N_DEV - 1,)),
                        pltpu.SemaphoreType.DMA],
    )(vec)


FFN_FWD_ROWS = 1056
FFN_BWD_ROWS = 704
FFN_BWD_CHUNKS = 4
DW_ROWS = 1408

def ffn_fwd(h, g, w_in8, w_out4, name, rider=None):
    m, d = h.shape
    fb = w_in8.shape[1]
    tm = _tile(m, FFN_FWD_ROWS)

    def body(h_ref, g_ref, wg_ref, wu_ref, wo_ref, hn_ref, n_ref, a_ref, fg_ref, fu_ref, acc_ref):
        i = pl.program_id(1)

        @pl.when(i == 0)
        def _():
            xh, _ = _rms(h_ref[...])
            n_ref[...] = (xh * g_ref[...]).astype(BF16)
            acc_ref[...] = jnp.zeros_like(acc_ref)

        n = n_ref[...]
        gate = _dot_nt(n, wg_ref[0])
        up = _dot_nt(n, wu_ref[0])
        sg = jax.nn.sigmoid(gate)
        silu = gate * sg
        a = (silu * up).astype(BF16)
        a_ref[0] = a
        fg_ref[0] = (up * (sg * (1.0 + gate * (1.0 - sg)))).astype(BF16)
        fu_ref[0] = silu.astype(BF16)
        acc_ref[...] += _dot(a, wo_ref[0])

        @pl.when(i == 3)
        def _():
            hn_ref[...] = h_ref[...] + 0.5 * acc_ref[...]

    return rider_call(
        body, name, (m // tm, 4),
        in_specs=[pl.BlockSpec((tm, d), lambda r, i: (r, 0)),
                  pl.BlockSpec((1, d), lambda r, i: (0, 0)),
                  pl.BlockSpec((1, fb, d), lambda r, i: (i, 0, 0)),
                  pl.BlockSpec((1, fb, d), lambda r, i: (i + 4, 0, 0)),
                  pl.BlockSpec((1, fb, d), lambda r, i: (i, 0, 0))],
        out_specs=[pl.BlockSpec((tm, d), lambda r, i: (r, 0)),
                   pl.BlockSpec((tm, d), lambda r, i: (r, 0))] + [pl.BlockSpec((1, tm, fb), lambda r, i: (i, r, 0))] * 3,
        out_shape=[jax.ShapeDtypeStruct((m, d), F32), jax.ShapeDtypeStruct((m, d), BF16)]
                  + [jax.ShapeDtypeStruct((4, m, fb), BF16)] * 3,
        scratch_shapes=[pltpu.VMEM((tm, d), F32)],
        args=(h, g, w_in8, w_in8, w_out4), rider=rider)


def ffn_bwd(dh, h, g, f_gate, f_up, w_in8, w_out4, name, rider=None):
    m, d = h.shape
    fb = w_in8.shape[1]
    tm = _tile(m, FFN_BWD_ROWS)

    def body(dh_ref, h_ref, g_ref, fg_ref, fu_ref, wg_ref, wu_ref, wo_ref,
             dhin_ref, dg_ref, du_ref, dgn_ref, dhs_ref, acc_ref):
        r = pl.program_id(0)
        i = pl.program_id(1)

        @pl.when(i == 0)
        def _():
            dhs_ref[...] = (0.5 * dh_ref[...]).astype(BF16)
            acc_ref[...] = jnp.zeros_like(acc_ref)

        @pl.when((r == 0) & (i == 0))
        def _():
            dgn_ref[...] = jnp.zeros_like(dgn_ref)

        chunks = FFN_BWD_CHUNKS if tm % (16 * FFN_BWD_CHUNKS) == 0 else 1
        for c in range(chunks):
            rows = slice(c * tm // chunks, (c + 1) * tm // chunks)
            da = _dot_nt(dhs_ref[rows, :], wo_ref[0])
            dub = (da * fu_ref[0, rows, :]).astype(BF16)
            dgb = (da * fg_ref[0, rows, :]).astype(BF16)
            dg_ref[0, rows, :] = dgb
            du_ref[0, rows, :] = dub
            acc_ref[rows, :] += _dot(dgb, wg_ref[0]) + _dot(dub, wu_ref[0])

        @pl.when(i == 3)
        def _():
            dx, dgain = _rms_bwd(h_ref[...], g_ref[...], acc_ref[...])
            dgn_ref[...] += dgain
            dhin_ref[...] = dh_ref[...] + dx

    row = lambda r, i: (r, 0)
    blk = lambda r, i: (i, r, 0)
    return rider_call(
        body, name, (m // tm, 4),
        in_specs=[pl.BlockSpec((tm, d), row), pl.BlockSpec((tm, d), row),
                  pl.BlockSpec((1, d), lambda r, i: (0, 0)),
                  pl.BlockSpec((1, tm, fb), blk), pl.BlockSpec((1, tm, fb), blk),
                  pl.BlockSpec((1, fb, d), lambda r, i: (i, 0, 0)),
                  pl.BlockSpec((1, fb, d), lambda r, i: (i + 4, 0, 0)),
                  pl.BlockSpec((1, fb, d), lambda r, i: (i, 0, 0))],
        out_specs=[pl.BlockSpec((tm, d), row),
                   pl.BlockSpec((1, tm, fb), blk), pl.BlockSpec((1, tm, fb), blk),
                   pl.BlockSpec((1, d), lambda r, i: (0, 0)),
                   pl.BlockSpec((tm, d), row)],
        out_shape=[jax.ShapeDtypeStruct((m, d), F32),
                   jax.ShapeDtypeStruct((4, m, fb), BF16), jax.ShapeDtypeStruct((4, m, fb), BF16),
                   jax.ShapeDtypeStruct((1, d), F32), jax.ShapeDtypeStruct((m, d), BF16)],
        scratch_shapes=[pltpu.VMEM((tm, d), F32)],
        args=(dh, h, g, f_gate, f_up, w_in8, w_in8, w_out4), rider=rider)


def matmul_tn(x, y, name, tn=None, tk=None, x2=None, rider=None):
    if tk is not None:
        assert x.shape[0] == y.shape[0] == 1 and x2 is None and rider is None
        bk, m, kf = x.shape[2] // tk, x.shape[1], x.shape[2]
        tm = _tile(m, DW_ROWS)
        tn_ = y.shape[2] if tn is None else tn

        def tiled(x_ref, y_ref, o_ref, acc_ref):
            r = pl.program_id(2)

            @pl.when(r == 0)
            def _():
                acc_ref[...] = jnp.zeros_like(acc_ref)

            acc_ref[...] += _dot_tn(x_ref[0].astype(BF16), y_ref[0].astype(BF16))

            @pl.when(r == m // tm - 1)
            def _():
                o_ref[0] = acc_ref[...].astype(BF16)

        return pl.pallas_call(
            tiled, name=name, grid=(bk, y.shape[2] // tn_, m // tm),
            in_specs=[pl.BlockSpec((1, tm, tk), lambda i, j, r: (0, r, i)),
                      pl.BlockSpec((1, tm, tn_), lambda i, j, r: (0, r, j))],
            out_specs=pl.BlockSpec((1, tk, tn_), lambda i, j, r: (0, i, j)),
            out_shape=jax.ShapeDtypeStruct((1, kf, y.shape[2]), BF16),
            scratch_shapes=[pltpu.VMEM((tk, tn_), F32)],
            compiler_params=_params(3),
        )(x, y)
    bx, m, k = x.shape
    by, _, n = y.shape
    b = max(bx, by) * (2 if x2 is not None else 1)
    tm = _tile(m, DW_ROWS)
    tn = n if tn is None else tn
    nt = n // tn
    nr = m // tm

    def body(*refs):
        x_ref, y_ref = refs[0], refs[-3]
        o_ref, acc_ref = refs[-2], refs[-1]
        r = pl.program_id(2)

        @pl.when(r == 0)
        def _():
            acc_ref[...] = jnp.zeros_like(acc_ref)

        if x2 is None:
            acc_ref[...] += _dot_tn(x_ref[0].astype(BF16), y_ref[0].astype(BF16))
        else:
            @pl.when(pl.program_id(0) < bx)
            def _():
                acc_ref[...] += _dot_tn(x_ref[0].astype(BF16), y_ref[0].astype(BF16))

            @pl.when(pl.program_id(0) >= bx)
            def _():
                acc_ref[...] += _dot_tn(refs[1][0].astype(BF16), y_ref[0].astype(BF16))

        @pl.when(r == nr - 1)
        def _():
            o_ref[0] = acc_ref[...].astype(BF16)

    if x2 is None:
        x_specs = [pl.BlockSpec((1, tm, k), (lambda i, j, r: (i, r, 0)) if bx > 1 else (lambda i, j, r: (0, r, 0)))]
    else:
        x_specs = [pl.BlockSpec((1, tm, k), lambda i, j, r: (jnp.minimum(i, bx - 1), jnp.where(i < bx, r, nr - 1), 0)),
                   pl.BlockSpec((1, tm, k), lambda i, j, r: (jnp.maximum(i - bx, 0), jnp.where(i < bx, 0, r), 0))]
    y_map = (lambda i, j, r: (i, r, j)) if by > 1 else (lambda i, j, r: (0, r, j))
    (out,), carried = rider_call(
        body, name, (b, nt, nr),
        in_specs=x_specs + [pl.BlockSpec((1, tm, tn), y_map)],
        out_specs=[pl.BlockSpec((1, k, tn), lambda i, j, r: (i, 0, j))],
        out_shape=[jax.ShapeDtypeStruct((b, k, n), BF16)],
        scratch_shapes=[pltpu.VMEM((k, tn), F32)],
        args=[x] + ([x2] if x2 is not None else []) + [y], rider=rider)
    return (out, carried) if rider is not None else out


AUG = HEAD_DIM


class _Cols:
    def __init__(self, d):
        self.d = d
        self.ga, self.gb = 0, d
        self.qa, self.ka, self.va = 2 * d, 2 * d + 1024, 2 * d + 2048
        self.qb = 2 * d + 3072
        self.kb, self.vb, self.fa = self.qb + 512, self.qb + 640, self.qb + 768
        self.np = self.qb + 1024


def mixer_proj(h, g, wp, name):
    m, d = h.shape
    npad = wp.shape[0]
    tm = _row_tile(m)

    def body(h_ref, g_ref, w_ref, n_ref, p_ref):
        xh, _ = _rms(h_ref[...])
        n = (xh * g_ref[...]).astype(BF16)
        n_ref[...] = n
        p_ref[...] = _dot_nt(n, w_ref[...])

    return pl.pallas_call(
        body, name=name, grid=(m // tm,),
        in_specs=[pl.BlockSpec((tm, d), lambda r: (r, 0)), pl.BlockSpec((1, d), lambda r: (0, 0)),
                  pl.BlockSpec((npad, d), lambda r: (0, 0))],
        out_specs=[pl.BlockSpec((tm, d), lambda r: (r, 0)), pl.BlockSpec((tm, npad), lambda r: (r, 0))],
        out_shape=[jax.ShapeDtypeStruct((m, d), BF16), jax.ShapeDtypeStruct((m, npad), F32)],
        compiler_params=_params(1),
    )(h, g, wp)


def _head_norm(x, gain, ones):
    outs = []
    for b in range(x.shape[1] // LANES):
        xb = x[:, b * LANES:(b + 1) * LANES]
        r = lax.rsqrt(_group_mean(xb * xb, ones) + EPS)
        outs.append(xb * r * gain[:, b * LANES:(b + 1) * LANES])
    return outs


def _head_norm_bwd(x, gain, dn, ones):
    dxs, dgs = [], []
    for b in range(x.shape[1] // LANES):
        sl = slice(b * LANES, (b + 1) * LANES)
        xb, dnb = x[:, sl], dn[:, sl]
        r = lax.rsqrt(_group_mean(xb * xb, ones) + EPS)
        xh = xb * r
        dxh = dnb * gain[:, sl]
        dxs.append(r * (dxh - xh * _group_mean(dxh * xh, ones)))
        dgs.append(jnp.sum(dnb * xh, axis=0, keepdims=True))
    return dxs, dgs


def _lane_col(v, lane_iota, idx):
    return jnp.sum(jnp.where(lane_iota == idx, v, 0.0), axis=1, keepdims=True)


def _aug(base, lane, vals):
    for i, v in enumerate(vals):
        base = jnp.where(lane == AUG + i, v, base)
    return base


def qk_post(proj, gains, fbias, cols, name):
    m = proj.shape[0]
    tm = _row_tile(m)
    gqa, gka, gqb, gkb = gains

    def body(qa_ref, ka_ref, va_ref, qb_ref, kb_ref, vb_ref, fa_ref, gqa_ref, gka_ref, gqb_ref, gkb_ref, fb_ref,
             qf_o, kf_o, vf_o, kt_o, vt_o, qb_o, kb_o, vb_o, carry_ref):
        r0 = pl.program_id(0)

        @pl.when(r0 == 0)
        def _():
            carry_ref[...] = jnp.zeros_like(carry_ref)

        z = fa_ref[...] + fb_ref[...]
        logf = jnp.minimum(z, 0.0) - jnp.log(1.0 + jnp.exp(-jnp.abs(z)))
        rr = lax.broadcasted_iota(jnp.int32, (tm, tm), 0)
        cc = lax.broadcasted_iota(jnp.int32, (tm, tm), 1)
        tril = jnp.where(cc <= rr, 1.0, 0.0).astype(BF16)
        p0, p1, p2 = _split3(logf)
        c = _dot(tril, p0) + _dot(tril, p1) + _dot(tril, p2) + carry_ref[...]
        carry_ref[...] += jnp.sum(logf, axis=0, keepdims=True)

        lane = lax.broadcasted_iota(jnp.int32, (tm, LANES), 1)
        is_pad = (r0 * tm + lax.broadcasted_iota(jnp.int32, (tm, 1), 0)) < PAD_FRONT
        ones = jnp.ones((LANES, LANES), BF16)
        for hd in range(8):
            sl = slice(hd * LANES, (hd + 1) * LANES)
            ch = _lane_col(c, lane, hd)
            ct = [p.astype(F32) for p in _split3(ch)]
            cs = [p.astype(F32) for p in _split3(-jnp.where(is_pad, -NEG, ch))]
            xq = qa_ref[:, sl]
            qn = xq * lax.rsqrt(_group_mean(xq * xq, ones) + EPS) * gqa_ref[...]
            qf_o[:, sl] = _aug(qn, lane, ct + [1.0, 1.0, 1.0]).astype(BF16)
            xk = ka_ref[:, sl]
            kn = xk * lax.rsqrt(_group_mean(xk * xk, ones) + EPS) * gka_ref[...]
            kf = _aug(kn, lane, [1.0, 1.0, 1.0] + cs)
            vf = _aug(va_ref[:, sl], lane, [1.0, 1.0, 1.0])
            kf_o[:, sl] = kf.astype(BF16)
            vf_o[:, sl] = vf.astype(BF16)
            kt_o[sl, :] = kf.T.astype(BF16)
            vt_o[sl, :] = vf.T.astype(BF16)

        gones = _group_ones()
        for src, gn, dst in ((qb_ref, gqb_ref, qb_o), (kb_ref, gkb_ref, kb_o)):
            for b, blk in enumerate(_head_norm(src[...], gn[...], gones)):
                dst[:, b * LANES:(b + 1) * LANES] = blk.astype(BF16)
        vb_o[...] = vb_ref[...].astype(BF16)

    w1024 = lambda off: pl.BlockSpec((tm, 1024), lambda r, o=off // 1024: (r, o))
    w512 = lambda off: pl.BlockSpec((tm, 512), lambda r, o=off // 512: (r, o))
    w128 = lambda off: pl.BlockSpec((tm, LANES), lambda r, o=off // LANES: (r, o))
    vec = lambda w: pl.BlockSpec((1, w), lambda r: (0, 0))
    row = lambda w: pl.BlockSpec((tm, w), lambda r: (r, 0))
    return pl.pallas_call(
        body, name=name, grid=(m // tm,),
        in_specs=[w1024(cols.qa), w1024(cols.ka), w1024(cols.va), w512(cols.qb), w128(cols.kb), w128(cols.vb),
                  w128(cols.fa), vec(LANES), vec(LANES), vec(512), vec(LANES), vec(LANES)],
        out_specs=[row(1024), row(1024), row(1024)] + [pl.BlockSpec((1024, tm), lambda r: (0, r))] * 2
                  + [row(512), row(LANES), row(LANES)],
        out_shape=[jax.ShapeDtypeStruct((m, 1024), BF16)] * 3 + [jax.ShapeDtypeStruct((1024, m), BF16)] * 2
                  + [jax.ShapeDtypeStruct((m, 512), BF16)] + [jax.ShapeDtypeStruct((m, LANES), BF16)] * 2,
        scratch_shapes=[pltpu.VMEM((1, LANES), F32)],
        compiler_params=_params(1),
    )(proj, proj, proj, proj, proj, proj, proj, gqa, gka, gqb, gkb, fbias)


def qk_post_bwd(proj, gains, fbias, dqf, dkf, dvf, dqb, dkb, dvb, dc, dga, dgb, cols, name):
    m = proj.shape[0]
    d = cols.d
    tm = _row_tile(m)
    nt = m // tm
    gqa, gka, gqb, gkb = gains

    def body(qa_ref, ka_ref, qb_ref, kb_ref, fa_ref, gqa_ref, gka_ref, gqb_ref, gkb_ref, fb_ref,
             dqf_ref, dkf_ref, dvf_ref, dqb_ref, dkb_ref, dvb_ref, dc_ref, dga_ref, dgb_ref,
             dp_o, ggqa_o, ggka_o, ggqb_o, ggkb_o, gfb_o, carry_ref):
        @pl.when(pl.program_id(0) == 0)
        def _():
            carry_ref[...] = jnp.zeros_like(carry_ref)
            for o in (ggqa_o, ggka_o, ggqb_o, ggkb_o, gfb_o):
                o[...] = jnp.zeros_like(o)

        dp_o[:, cols.ga:cols.ga + d] = dga_ref[...].astype(BF16)
        dp_o[:, cols.gb:cols.gb + d] = dgb_ref[...].astype(BF16)
        dp_o[:, cols.fa + LANES:cols.np] = jnp.zeros((tm, cols.np - cols.fa - LANES), BF16)
        lane = lax.broadcasted_iota(jnp.int32, (tm, LANES), 1)
        data = lane < HEAD_DIM
        ones = jnp.ones((LANES, LANES), BF16)
        for hd in range(8):
            sl = slice(hd * LANES, (hd + 1) * LANES)
            for src, gn, dn_ref, off, gout in ((qa_ref, gqa_ref, dqf_ref, cols.qa, ggqa_o),
                                               (ka_ref, gka_ref, dkf_ref, cols.ka, ggka_o)):
                x = src[:, sl]
                dn = jnp.where(data, dn_ref[:, sl], 0.0)
                r = lax.rsqrt(_group_mean(x * x, ones) + EPS)
                xh = x * r
                dxh = dn * gn[...]
                dp_o[:, off + hd * LANES:off + (hd + 1) * LANES] = (
                    r * (dxh - xh * _group_mean(dxh * xh, ones))).astype(BF16)
                gout[...] += jnp.sum(dn * xh, axis=0, keepdims=True)
            dp_o[:, cols.va + hd * LANES:cols.va + (hd + 1) * LANES] = jnp.where(data, dvf_ref[:, sl], 0.0).astype(BF16)
        dp_o[:, cols.vb:cols.vb + LANES] = dvb_ref[...].astype(BF16)
        gones = _group_ones()
        for src, gn, dn, off, gout in ((qb_ref, gqb_ref, dqb_ref, cols.qb, ggqb_o),
                                       (kb_ref, gkb_ref, dkb_ref, cols.kb, ggkb_o)):
            dxs, dgs = _head_norm_bwd(src[...], gn[...], dn[...], gones)
            for b, (dx, dg) in enumerate(zip(dxs, dgs)):
                dp_o[:, off + b * LANES:off + (b + 1) * LANES] = dx.astype(BF16)
                gout[:, b * LANES:(b + 1) * LANES] += dg
        dcv = dc_ref[...]
        rr = lax.broadcasted_iota(jnp.int32, (tm, tm), 0)
        cc = lax.broadcasted_iota(jnp.int32, (tm, tm), 1)
        triu = jnp.where(cc >= rr, 1.0, 0.0).astype(BF16)
        p0, p1, p2 = _split3(dcv)
        dlogf = _dot(triu, p0) + _dot(triu, p1) + _dot(triu, p2) + carry_ref[...]
        carry_ref[...] += jnp.sum(dcv, axis=0, keepdims=True)
        z = fa_ref[...] + fb_ref[...]
        row = (nt - 1 - pl.program_id(0)) * tm + lax.broadcasted_iota(jnp.int32, (tm, LANES), 0)
        dfa = jnp.where(row >= PAD_FRONT, dlogf * jax.nn.sigmoid(-z), 0.0)
        dp_o[:, cols.fa:cols.fa + LANES] = dfa.astype(BF16)
        gfb_o[...] += jnp.sum(dfa, axis=0, keepdims=True)

    rev = lambda r: nt - 1 - r
    w1024 = lambda off: pl.BlockSpec((tm, 1024), lambda r, o=off // 1024: (rev(r), o))
    w512 = lambda off: pl.BlockSpec((tm, 512), lambda r, o=off // 512: (rev(r), o))
    w128 = lambda off: pl.BlockSpec((tm, LANES), lambda r, o=off // LANES: (rev(r), o))
    vec = lambda w: pl.BlockSpec((1, w), lambda r: (0, 0))
    row = lambda w: pl.BlockSpec((tm, w), lambda r: (rev(r), 0))
    return pl.pallas_call(
        body, name=name, grid=(nt,),
        in_specs=[w1024(cols.qa), w1024(cols.ka), w512(cols.qb), w128(cols.kb), w128(cols.fa),
                  vec(LANES), vec(LANES), vec(512), vec(LANES), vec(LANES),
                  row(1024), row(1024), row(1024), row(512), row(LANES), row(LANES), row(LANES), row(d), row(d)],
        out_specs=[row(cols.np), vec(LANES), vec(LANES), vec(512), vec(LANES), vec(LANES)],
        out_shape=[jax.ShapeDtypeStruct((m, cols.np), BF16), jax.ShapeDtypeStruct((1, LANES), F32),
                   jax.ShapeDtypeStruct((1, LANES), F32), jax.ShapeDtypeStruct((1, 512), F32),
                   jax.ShapeDtypeStruct((1, LANES), F32), jax.ShapeDtypeStruct((1, LANES), F32)],
        scratch_shapes=[pltpu.VMEM((1, LANES), F32)],
        compiler_params=_params(1),
    )(proj, proj, proj, proj, proj, gqa, gka, gqb, gkb, fbias, dqf, dkf, dvf, dqb, dkb, dvb, dc, dga, dgb)


def dproj_bwd(dh, h, g, dproj, wp, name):
    m, d = h.shape
    npad = wp.shape[0]
    tm = _row_tile(m)

    def body(dh_ref, h_ref, g_ref, dp_ref, w_ref, dhin_ref, dgn_ref):
        @pl.when(pl.program_id(0) == 0)
        def _():
            dgn_ref[...] = jnp.zeros_like(dgn_ref)

        dn = _dot(dp_ref[...], w_ref[...])
        dx, dgain = _rms_bwd(h_ref[...], g_ref[...], dn)
        dgn_ref[...] += dgain
        dhin_ref[...] = dh_ref[...] + dx

    row = lambda w: pl.BlockSpec((tm, w), lambda r: (r, 0))
    return pl.pallas_call(
        body, name=name, grid=(m // tm,),
        in_specs=[row(d), row(d), pl.BlockSpec((1, d), lambda r: (0, 0)), row(npad),
                  pl.BlockSpec((npad, d), lambda r: (0, 0))],
        out_specs=[row(d), pl.BlockSpec((1, d), lambda r: (0, 0))],
        out_shape=[jax.ShapeDtypeStruct((m, d), F32), jax.ShapeDtypeStruct((1, d), F32)],
        compiler_params=_params(1),
    )(dh, h, g, dproj, wp)


def _causal_t(t):
    return lax.broadcasted_iota(jnp.int32, (t, t), 0) <= lax.broadcasted_iota(jnp.int32, (t, t), 1)


HEADS_PER_STEP = 4


def fox_fwd(qf, kf, vt, name, rider=None):
    m = qf.shape[0]
    t = _row_tile(m)
    nq = m // t
    hp = HEADS_PER_STEP
    w = hp * LANES

    def body(q_ref, k_ref, vt_ref, o_ref, lse_ref, acc_ref, m_ref, p_ref, a_ref):
        qi = pl.program_id(1)
        acc_ref[...] = jnp.zeros_like(acc_ref)
        m_ref[...] = jnp.full_like(m_ref, NEG)

        def scores(ki, slot, mask):
            off = pl.multiple_of(ki * t, t)
            for e in range(hp):
                sl = slice(e * LANES, (e + 1) * LANES)
                s = _dot_nt(k_ref[pl.ds(off, t), sl], q_ref[:, sl])
                if mask is not None:
                    s = jnp.where(mask, s, NEG)
                m_old = m_ref[e]
                m_new = jnp.maximum(m_old, jnp.max(s, axis=0, keepdims=True))
                p_ref[slot, e] = jnp.exp(s - m_new).astype(BF16)
                a_ref[slot, e] = jnp.exp(m_old - m_new)
                m_ref[e] = m_new

        def values(ki, slot):
            off = pl.multiple_of(ki * t, t)
            for e in range(hp):
                sl = slice(e * LANES, (e + 1) * LANES)
                acc_ref[e] = acc_ref[e] * a_ref[slot, e] + _dot(vt_ref[sl, pl.ds(off, t)], p_ref[slot, e])

        causal = _causal_t(t)
        scores(0, 0, causal | (jnp.full((t, t), qi, jnp.int32) > 0))

        def step(ki, carry):
            values(ki - 1, (ki - 1) % 2)
            scores(ki, ki % 2, None)
            return carry

        lax.fori_loop(1, qi, step, 0)

        @pl.when(qi >= 1)
        def _():
            values(qi - 1, (qi - 1) % 2)
            scores(qi, qi % 2, causal)

        values(qi, qi % 2)
        row = lax.broadcasted_iota(jnp.int32, (LANES, t), 0)
        for e in range(hp):
            l = jnp.max(acc_ref[e, AUG:AUG + 8, :], axis=0, keepdims=True)
            o_ref[:, e * LANES:(e + 1) * LANES] = jnp.where(row < HEAD_DIM, acc_ref[e] * (1.0 / l), 0.0).T
            lse_ref[e] = m_ref[e] + jnp.log(l)

    return rider_call(
        body, name, (8 // hp, nq),
        in_specs=[pl.BlockSpec((t, w), lambda hd, i: (i, hd)),
                  pl.BlockSpec((m, w), lambda hd, i: (0, hd)),
                  pl.BlockSpec((w, m), lambda hd, i: (hd, 0))],
        out_specs=[pl.BlockSpec((t, w), lambda hd, i: (i, hd)),
                   pl.BlockSpec((hp, 1, t), lambda hd, i: (hd, 0, i))],
        out_shape=[jax.ShapeDtypeStruct((m, 8 * LANES), F32), jax.ShapeDtypeStruct((8, 1, m), F32)],
        scratch_shapes=[pltpu.VMEM((hp, LANES, t), F32), pltpu.VMEM((hp, 1, t), F32),
                        pltpu.VMEM((2, hp, t, t), BF16), pltpu.VMEM((2, hp, 1, t), F32)],
        args=(qf, kf, vt), rider=rider)


def fox_bwd(qf, kf, vf, kt, dof, lse, delta, name, rider=None):
    m = qf.shape[0]
    t = _row_tile(m)
    nq = m // t
    hp = HEADS_PER_STEP
    w = hp * LANES

    def body(k_ref, v_ref, kt_ref, q_ref, do_ref, lse_ref, delta_ref, dk_ref, dv_ref, dq_ref, dck_ref, dcq_ref,
             dka_ref, dva_ref, dqt_ref):
        ki = pl.program_id(1)

        @pl.when(ki == 0)
        def _():
            dqt_ref[...] = jnp.zeros_like(dqt_ref)

        dka_ref[...] = jnp.zeros_like(dka_ref)
        dva_ref[...] = jnp.zeros_like(dva_ref)

        def tile(qi, diagonal):
            off = pl.multiple_of(qi * t, t)
            for e in range(hp):
                sl = slice(e * LANES, (e + 1) * LANES)
                q = q_ref[pl.ds(off, t), sl]
                do = do_ref[pl.ds(off, t), sl]
                s = _dot_nt(k_ref[:, sl], q)
                if diagonal:
                    s = jnp.where(_causal_t(t), s, NEG)
                p = jnp.exp(s - lse_ref[e, :, pl.ds(off, t)])
                ds = (p * (_dot_nt(v_ref[:, sl], do) - delta_ref[e, :, pl.ds(off, t)])).astype(BF16)
                dva_ref[:, sl] += _dot(p.astype(BF16), do)
                dka_ref[:, sl] += _dot(ds, q)
                dqt_ref[sl, pl.ds(off, t)] += _dot(kt_ref[sl, :], ds)

        def step(qi, carry):
            tile(qi, False)
            return carry

        tile(ki, True)
        lax.fori_loop(ki + 1, nq, step, 0)
        dk_ref[...] = dka_ref[...]
        dv_ref[...] = dva_ref[...]
        row8 = lax.broadcasted_iota(jnp.int32, (8, 1), 0)
        for e in range(hp):
            slab = dka_ref[:, e * LANES:(e + 1) * LANES].T[AUG:AUG + 8, :]
            dck_ref[e] = -jnp.sum(jnp.where(row8 == 3, slab, 0.0), axis=0, keepdims=True)

        @pl.when(ki == nq - 1)
        def _():
            for e in range(hp):
                sl = slice(e * LANES, (e + 1) * LANES)
                slab = dqt_ref[e * LANES + AUG:e * LANES + AUG + 8, :]
                dcq_ref[e] = jnp.sum(jnp.where(row8 == 0, slab, 0.0), axis=0, keepdims=True)
                for j in range(nq):
                    dq_ref[j * t:(j + 1) * t, sl] = dqt_ref[sl, j * t:(j + 1) * t].T

    tile_spec = pl.BlockSpec((t, w), lambda hd, i: (i, hd))
    full = pl.BlockSpec((m, w), lambda hd, i: (0, hd))
    stat = pl.BlockSpec((hp, 1, m), lambda hd, i: (hd, 0, 0))
    (dkf, dvf, dqf, dck, dcq), carried = rider_call(
        body, name, (8 // hp, nq),
        in_specs=[tile_spec, tile_spec, pl.BlockSpec((w, t), lambda hd, i: (hd, i)), full, full, stat, stat],
        out_specs=[tile_spec, tile_spec, full, pl.BlockSpec((hp, 1, t), lambda hd, i: (hd, 0, i)), stat],
        out_shape=[jax.ShapeDtypeStruct((m, 8 * LANES), F32), jax.ShapeDtypeStruct((m, 8 * LANES), F32),
                   jax.ShapeDtypeStruct((m, 8 * LANES), F32), jax.ShapeDtypeStruct((8, 1, m), F32),
                   jax.ShapeDtypeStruct((8, 1, m), F32)],
        scratch_shapes=[pltpu.VMEM((t, w), F32), pltpu.VMEM((t, w), F32), pltpu.VMEM((w, m), F32)],
        args=(kf, vf, kt, qf, dof, lse, delta), rider=rider)
    return (dkf, dvf, dqf, dcq + dck), carried


def _bucket_ids():
    def bucket(dist):
        n = np.maximum(dist, 0)
        max_exact = N_BUCKETS // 2
        nf = np.maximum(n, 1).astype(np.float32)
        large = max_exact + (np.log(nf / max_exact) / math.log(MAX_DISTANCE / max_exact)
                             * (N_BUCKETS - max_exact)).astype(np.int32)
        return np.where(n < max_exact, n, np.minimum(large, N_BUCKETS - 1))

    tl = np.arange(LANES)[:, None]
    sl = np.arange(LANES)[None, :]
    prev = bucket(LANES + tl - sl)
    cur = bucket(tl - sl)
    meta = np.full((LANES, LANES), N_BUCKETS - 1)
    return np.concatenate([prev, cur, meta], axis=1).astype(np.int32)


def bias_build(table, name):
    ids = jnp.asarray(_bucket_ids())

    def body(t_ref, id_ref, o_ref):
        idv = id_ref[...]
        for h in range(8):
            acc = jnp.zeros((LANES, 3 * LANES), F32)
            for b in range(N_BUCKETS):
                acc = jnp.where(idv == b, t_ref[b, h], acc)
            o_ref[h] = acc

    return pl.pallas_call(
        body, name=name,
        in_specs=[pl.BlockSpec(memory_space=pltpu.SMEM), pl.BlockSpec(memory_space=pltpu.VMEM)],
        out_specs=pl.BlockSpec(memory_space=pltpu.VMEM),
        out_shape=jax.ShapeDtypeStruct((8, LANES, 3 * LANES), F32),
    )(table, ids)


def bias_reduce(dbias, name):
    ids = jnp.asarray(_bucket_ids())

    def body(d_ref, id_ref, o_ref):
        idv = id_ref[...]
        rr = lax.broadcasted_iota(jnp.int32, (N_BUCKETS, LANES), 0)
        cc = lax.broadcasted_iota(jnp.int32, (N_BUCKETS, LANES), 1)
        acc = jnp.zeros((N_BUCKETS, LANES), F32)
        for h in range(8):
            dv = d_ref[h]
            for b in range(N_BUCKETS):
                val = jnp.sum(jnp.where(idv == b, dv, 0.0), keepdims=True)
                acc = jnp.where((rr == b) & (cc == h), val, acc)
        o_ref[...] = acc

    return pl.pallas_call(
        body, name=name,
        in_specs=[pl.BlockSpec(memory_space=pltpu.VMEM), pl.BlockSpec(memory_space=pltpu.VMEM)],
        out_specs=pl.BlockSpec(memory_space=pltpu.VMEM),
        out_shape=jax.ShapeDtypeStruct((N_BUCKETS, LANES), F32),
    )(dbias, ids)


def _swa_penalty(n):
    shape = (LANES, 3 * LANES)
    tl = lax.broadcasted_iota(jnp.int32, shape, 0)
    col = lax.broadcasted_iota(jnp.int32, shape, 1)
    sl = col & (LANES - 1)
    nv = jnp.full(shape, n, jnp.int32)
    is_meta = sl >= PAD_FRONT
    prev = (col < LANES) & (sl > tl) & (nv >= 1) & ((nv >= 2) | is_meta)
    cur = (col >= LANES) & (col < 2 * LANES) & (sl <= tl) & ((nv >= 1) | is_meta)
    meta = (col >= 2 * LANES) & is_meta & ((nv >= 2) | ((nv == 1) & (sl <= tl)))
    return jnp.where(prev | cur | meta, 0.0, NEG)


def _swa_keys(ref, n):
    off_prev = pl.multiple_of(jnp.maximum(n - 1, 0) * LANES, LANES)
    off_cur = pl.multiple_of(n * LANES, LANES)
    return jnp.concatenate([ref[pl.ds(off_prev, LANES), :], ref[pl.ds(off_cur, LANES), :], ref[0:LANES, :]], axis=0)


def swa_fwd(q, k, v, bias, sinks, name, rider=None):
    m = q.shape[0]

    def body(q_ref, k_ref, v_ref, bias_ref, sink_ref, o_ref, lse_ref):
        n = pl.program_id(0)
        lane1 = lax.broadcasted_iota(jnp.int32, (1, LANES), 1)
        lane_t = lax.broadcasted_iota(jnp.int32, (LANES, LANES), 1)
        in_head = [lane1 < HEAD_DIM, lane1 >= HEAD_DIM]
        kall = _swa_keys(k_ref, n)
        vall = _swa_keys(v_ref, n)
        vs = [jnp.where(in_head[g], vall, jnp.zeros_like(vall)) for g in (0, 1)]
        penalty = _swa_penalty(n)
        lse = jnp.zeros((LANES, LANES), F32)
        for b in range(4):
            qb = q_ref[:, b * LANES:(b + 1) * LANES]
            ob = jnp.zeros((LANES, LANES), F32)
            for g in (0, 1):
                h = 4 * g + b
                qe = jnp.where(in_head[g], qb, jnp.zeros_like(qb))
                s = _dot_nt(qe, kall) + bias_ref[h] + penalty
                sink = sink_ref[h]
                mx = jnp.maximum(jnp.max(s, axis=1, keepdims=True), sink)
                p = jnp.exp(s - mx)
                den = jnp.sum(p, axis=1, keepdims=True) + jnp.exp(sink - mx)
                ob = ob + _dot((p / den).astype(BF16), vs[g])
                lse = jnp.where(lane_t == h, mx + jnp.log(den), lse)
            o_ref[:, b * LANES:(b + 1) * LANES] = ob
        lse_ref[...] = lse

    return rider_call(
        body, name, (m // LANES,),
        in_specs=[pl.BlockSpec((LANES, 512), lambda n: (n, 0)),
                  pl.BlockSpec((m, LANES), lambda n: (0, 0)), pl.BlockSpec((m, LANES), lambda n: (0, 0)),
                  pl.BlockSpec((8, LANES, 3 * LANES), lambda n: (0, 0, 0)),
                  pl.BlockSpec(memory_space=pltpu.SMEM)],
        out_specs=[pl.BlockSpec((LANES, 512), lambda n: (n, 0)), pl.BlockSpec((LANES, LANES), lambda n: (n, 0))],
        out_shape=[jax.ShapeDtypeStruct((m, 512), F32), jax.ShapeDtypeStruct((m, LANES), F32)],
        scratch_shapes=[], args=(q, k, v, bias, sinks), rider=rider)


def swa_bwd(q, k, v, bias, sinks, o, lse, do, name):
    m = q.shape[0]

    def body(q_ref, do_ref, o_ref, lse_ref, k_ref, v_ref, bias_ref, sink_ref,
             dq_ref, dk_ref, dv_ref, dbias_ref, dsink_ref):
        n = pl.program_id(0)

        @pl.when(n == 0)
        def _():
            for r in (dk_ref, dv_ref, dbias_ref, dsink_ref):
                r[...] = jnp.zeros_like(r)

        lane1 = lax.broadcasted_iota(jnp.int32, (1, LANES), 1)
        lane_t = lax.broadcasted_iota(jnp.int32, (LANES, LANES), 1)
        in_head = [lane1 < HEAD_DIM, lane1 >= HEAD_DIM]
        off_prev = pl.multiple_of(jnp.maximum(n - 1, 0) * LANES, LANES)
        off_cur = pl.multiple_of(n * LANES, LANES)
        kall = _swa_keys(k_ref, n)
        vall = _swa_keys(v_ref, n)
        ks = [jnp.where(in_head[g], kall, jnp.zeros_like(kall)) for g in (0, 1)]
        penalty = _swa_penalty(n)
        lsev = lse_ref[...]
        dsink = dsink_ref[...]
        dkall = jnp.zeros((3 * LANES, LANES), F32)
        dvall = jnp.zeros((3 * LANES, LANES), F32)
        for b in range(4):
            sl = slice(b * LANES, (b + 1) * LANES)
            qb = q_ref[:, sl]
            dob = do_ref[:, sl]
            prod = dob * o_ref[:, sl]
            dqb = jnp.zeros((LANES, LANES), F32)
            for g in (0, 1):
                h = 4 * g + b
                qe = jnp.where(in_head[g], qb, jnp.zeros_like(qb))
                doe = jnp.where(in_head[g], dob, 0.0).astype(BF16)
                delta = jnp.sum(jnp.where(in_head[g], prod, 0.0), axis=1, keepdims=True)
                lse_h = _lane_col(lsev, lane_t, h)
                s = _dot_nt(qe, kall) + bias_ref[h] + penalty
                p = jnp.exp(s - lse_h)
                ds = p * (_dot_nt(doe, vall) - delta)
                dbias_ref[h] += ds
                sink_part = jnp.sum(-jnp.exp(sink_ref[h] - lse_h) * delta, keepdims=True)
                dsink = jnp.where(lane1 == h, dsink + sink_part, dsink)
                dsb = ds.astype(BF16)
                dqb = dqb + _dot(dsb, ks[g])
                dkall = dkall + _dot_tn(dsb, qe)
                dvall = dvall + _dot_tn(p.astype(BF16), doe)
            dq_ref[:, sl] = dqb
        dsink_ref[...] = dsink
        for ref, val in ((dk_ref, dkall), (dv_ref, dvall)):
            ref[pl.ds(off_prev, LANES), :] += val[0:LANES]
            ref[pl.ds(off_cur, LANES), :] += val[LANES:2 * LANES]
            ref[0:LANES, :] += val[2 * LANES:3 * LANES]

    blk = pl.BlockSpec((LANES, 512), lambda n: (n, 0))
    full = pl.BlockSpec((m, LANES), lambda n: (0, 0))
    return pl.pallas_call(
        body, name=name, grid=(m // LANES,),
        in_specs=[blk, blk, blk, pl.BlockSpec((LANES, LANES), lambda n: (n, 0)), full, full,
                  pl.BlockSpec((8, LANES, 3 * LANES), lambda n: (0, 0, 0)),
                  pl.BlockSpec(memory_space=pltpu.SMEM)],
        out_specs=[blk, full, full, pl.BlockSpec((8, LANES, 3 * LANES), lambda n: (0, 0, 0)),
                   pl.BlockSpec((1, LANES), lambda n: (0, 0))],
        out_shape=[jax.ShapeDtypeStruct((m, 512), F32), jax.ShapeDtypeStruct((m, LANES), F32),
                   jax.ShapeDtypeStruct((m, LANES), F32), jax.ShapeDtypeStruct((8, LANES, 3 * LANES), F32),
                   jax.ShapeDtypeStruct((1, LANES), F32)],
        compiler_params=_params(1),
    )(q, do, o, lse, k, v, bias, sinks)


def branch_out(h, o_fox, o_swa, proj, wbf, wbs, wo, cols, name):
    m, d = h.shape
    tm = _row_tile(m)

    def body(h_ref, of_ref, os_ref, ga_ref, gb_ref, wbf_ref, wbs_ref, wo_ref, hn_ref):
        tf = _dot(of_ref[...].astype(BF16), wbf_ref[...])
        ts = _dot(os_ref[...].astype(BF16), wbs_ref[...])
        y = jax.nn.sigmoid(ga_ref[...]) * tf + jax.nn.sigmoid(gb_ref[...]) * ts
        hn_ref[...] = h_ref[...] + _dot(y.astype(BF16), wo_ref[...])

    row = lambda w, o=0: pl.BlockSpec((tm, w), lambda r, o=o: (r, o))
    res = lambda a: pl.BlockSpec(a.shape, lambda r: (0, 0))
    return pl.pallas_call(
        body, name=name, grid=(m // tm,),
        in_specs=[row(d), row(1024), row(512), row(d, cols.ga // d), row(d, cols.gb // d), res(wbf), res(wbs), res(wo)],
        out_specs=row(d),
        out_shape=jax.ShapeDtypeStruct((m, d), F32),
        compiler_params=_params(1),
    )(h, o_fox, o_swa, proj, proj, wbf, wbs, wo)


def branch_out_bwd(dh, o_fox, o_swa, proj, wbf, wbs, wo, cols, name):
    m, d = dh.shape
    tm = _row_tile(m)

    def body(dh_ref, of_ref, os_ref, ga_ref, gb_ref, wbf_ref, wbs_ref, wo_ref,
             y_ref, dtf_ref, dts_ref, dga_ref, dgb_ref, dof_ref, dos_ref, delta_ref):
        dy = _dot_nt(dh_ref[...].astype(BF16), wo_ref[...])
        tf = _dot(of_ref[...].astype(BF16), wbf_ref[...])
        ts = _dot(os_ref[...].astype(BF16), wbs_ref[...])
        sa = jax.nn.sigmoid(ga_ref[...])
        sb = jax.nn.sigmoid(gb_ref[...])
        y_ref[...] = (sa * tf + sb * ts).astype(BF16)
        dtf = (dy * sa).astype(BF16)
        dts = (dy * sb).astype(BF16)
        dtf_ref[...] = dtf
        dts_ref[...] = dts
        dga_ref[...] = (dy * tf * sa * (1.0 - sa)).astype(BF16)
        dgb_ref[...] = (dy * ts * sb * (1.0 - sb)).astype(BF16)
        dof = _dot_nt(dtf, wbf_ref[...])
        dof_ref[...] = dof.astype(BF16)
        dos_ref[...] = _dot_nt(dts, wbs_ref[...])
        lane = lax.broadcasted_iota(jnp.int32, (tm, LANES), 1)
        delta = jnp.zeros((tm, LANES), F32)
        for hd in range(8):
            sl = slice(hd * LANES, (hd + 1) * LANES)
            delta = jnp.where(lane == hd, jnp.sum(dof[:, sl] * of_ref[:, sl], axis=1, keepdims=True), delta)
        delta_ref[...] = delta

    row = lambda w, o=0: pl.BlockSpec((tm, w), lambda r, o=o: (r, o))
    res = lambda a: pl.BlockSpec(a.shape, lambda r: (0, 0))
    return pl.pallas_call(
        body, name=name, grid=(m // tm,),
        in_specs=[row(d), row(1024), row(512), row(d, cols.ga // d), row(d, cols.gb // d), res(wbf), res(wbs), res(wo)],
        out_specs=[row(d)] * 5 + [row(1024), row(512), row(LANES)],
        out_shape=[jax.ShapeDtypeStruct((m, d), BF16)] * 5 + [jax.ShapeDtypeStruct((m, 1024), BF16),
                   jax.ShapeDtypeStruct((m, 512), F32), jax.ShapeDtypeStruct((m, LANES), F32)],
        compiler_params=_params(1),
    )(dh, o_fox, o_swa, proj, proj, wbf, wbs, wo)


def loss_head(h, target, name):
    m, d = h.shape

    def body(h_ref, t_ref, dh_ref, loss_ref):
        n = pl.program_id(0)

        @pl.when(n == 0)
        def _():
            loss_ref[...] = jnp.zeros_like(loss_ref)
            dh_ref[...] = jnp.zeros_like(dh_ref)

        @pl.when(n > 0)
        def _():
            err = h_ref[...] - t_ref[...]
            dh_ref[...] = err * (1.0 / d)
            loss_ref[...] += jnp.sum(err * err, keepdims=True) * (0.5 / d)

    return pl.pallas_call(
        body, name=name, grid=(m // LANES,),
        in_specs=[pl.BlockSpec((LANES, d), lambda n: (n, 0)),
                  pl.BlockSpec((LANES, d), lambda n: (jnp.maximum(n - 1, 0), 0))],
        out_specs=[pl.BlockSpec((LANES, d), lambda n: (n, 0)), pl.BlockSpec((8, LANES), lambda n: (0, 0))],
        out_shape=[jax.ShapeDtypeStruct((m, d), F32), jax.ShapeDtypeStruct((8, LANES), F32)],
        compiler_params=_params(1),
    )(h, target)


def _adamw_math(w, g, m, v):
    m = ADAM_B1 * m + (1.0 - ADAM_B1) * g
    v = ADAM_B2 * v + (1.0 - ADAM_B2) * (g * g)
    m_hat = m / (1.0 - ADAM_B1 ** ADAM_STEP)
    v_hat = v / (1.0 - ADAM_B2 ** ADAM_STEP)
    delta = -ADAM_LR * (m_hat / (jnp.sqrt(v_hat) + ADAM_EPS) + ADAM_WD * w)
    return delta, m, v


def adamw_sum(parts, w, m, v, name, after=None):
    n_layers, a, b = w.shape
    ta = next(t for t in (256, 176, 128, a) if a % t == 0)
    nr = a // ta

    def body(*refs):
        p_refs = refs[:n_layers]
        w_ref, m_ref, v_ref = refs[n_layers:n_layers + 3]
        g_o, d_o, m_o, v_o = refs[-4:]
        for l in range(n_layers):
            @pl.when(pl.program_id(0) == l)
            def _(l=l):
                g = p_refs[l][0].astype(F32)
                for j in range(1, N_DEV):
                    g = g + p_refs[l][j].astype(F32)
                g_o[0] = g
                d_o[0], m_o[0], v_o[0] = _adamw_math(w_ref[0], g, m_ref[0], v_ref[0])

    def part_spec(l):
        return pl.BlockSpec((N_DEV, ta, b), lambda i, r, l=l: (0, jnp.where(i == l, r, jnp.where(i < l, 0, nr - 1)), 0))

    row = pl.BlockSpec((1, ta, b), lambda i, r: (i, r, 0))
    return pl.pallas_call(
        body, name=name, grid=(n_layers, nr),
        in_specs=[part_spec(l) for l in range(n_layers)] + [row, row, row]
                 + ([pl.BlockSpec(memory_space=pl.ANY)] if after is not None else []),
        out_specs=[row] * 4,
        out_shape=[jax.ShapeDtypeStruct(w.shape, F32)] * 4,
        compiler_params=_params(2),
    )(*parts, w, m, v, *([after] if after is not None else []))


def adamw_sum_cols(parts, w, m, v, name, after=None):
    n_layers = len(parts)
    a, b = parts[0].shape[1:]
    tc = 512 if b % 512 == 0 else b
    nc = b // tc

    def body(*refs):
        p_refs = refs[:n_layers]
        w_ref, m_ref, v_ref = refs[n_layers:n_layers + 3]
        g_o, d_o, m_o, v_o = refs[-4:]
        for l in range(n_layers):
            @pl.when(pl.program_id(0) == l)
            def _(l=l):
                g = p_refs[l][0].astype(F32)
                for j in range(1, N_DEV):
                    g = g + p_refs[l][j].astype(F32)
                g_o[...] = g
                d_o[...], m_o[...], v_o[...] = _adamw_math(w_ref[...], g, m_ref[...], v_ref[...])

    def part_spec(l):
        return pl.BlockSpec((N_DEV, a, tc), lambda i, c, l=l: (0, 0, jnp.where(i == l, c, jnp.where(i < l, 0, nc - 1))))

    col = pl.BlockSpec((a, tc), lambda i, c: (0, i * nc + c))
    return pl.pallas_call(
        body, name=name, grid=(n_layers, nc),
        in_specs=[part_spec(l) for l in range(n_layers)] + [col, col, col]
                 + ([pl.BlockSpec(memory_space=pl.ANY)] if after is not None else []),
        out_specs=[col] * 4,
        out_shape=[jax.ShapeDtypeStruct(w.shape, F32)] * 4,
        compiler_params=_params(2),
    )(*parts, w, m, v, *([after] if after is not None else []))


def adamw_small(g, w, m, v, name):
    def body(g_ref, w_ref, m_ref, v_ref, d_o, m_o, v_o):
        d_o[...], m_o[...], v_o[...] = _adamw_math(w_ref[...], g_ref[...], m_ref[...], v_ref[...])

    spec = pl.BlockSpec(memory_space=pltpu.VMEM)
    return pl.pallas_call(
        body, name=name, in_specs=[spec] * 4, out_specs=[spec] * 3,
        out_shape=[jax.ShapeDtypeStruct(w.shape, F32)] * 3,
    )(g, w, m, v)


BIG = ("ffn1_w_in", "ffn1_w_out", "w_in", "w_branch_fox", "w_branch_swa", "w_out", "ffn2_w_in", "ffn2_w_out")
SMALL = ("rel_bias_table", "ffn1_norm", "mix_norm", "forget_bias", "fox_q_norm", "fox_k_norm",
         "swa_q_norm", "swa_k_norm", "swa_sinks", "ffn2_norm")
WEIGHTS = ("meta_tokens", "rel_bias_table", "ffn1_norm", "ffn1_w_in", "ffn1_w_out", "mix_norm", "w_in",
           "forget_bias", "fox_q_norm", "fox_k_norm", "swa_q_norm", "swa_k_norm", "swa_sinks", "w_branch_fox",
           "w_branch_swa", "w_out", "ffn2_norm", "ffn2_w_in", "ffn2_w_out")


def _pack(arrs, width, row_multiple, dtype):
    lead = arrs[0].shape[:-1]
    flat = jnp.concatenate([a.astype(dtype) for a in arrs], axis=-1)
    n = flat.shape[-1]
    rows = -(-n // width)
    rows = -(-rows // row_multiple) * row_multiple
    flat = jnp.pad(flat, [(0, 0)] * len(lead) + [(0, rows * width - n)])
    return flat.reshape(lead + (rows, width))


def _unpack(flat, shapes):
    flat = flat.reshape(-1)
    out, off = [], 0
    for s in shapes:
        n = int(np.prod(s))
        out.append(flat[off:off + n].reshape(s))
        off += n
    return out


def _swa_head_order():
    return [4 * (j % 2) + j // 2 for j in range(8)]


def _permute_heads(a, axis, inverse=False):
    order = _swa_head_order()
    if inverse:
        order = [order.index(hd) for hd in range(8)]
    parts = [lax.slice_in_dim(a, hd * HEAD_DIM, (hd + 1) * HEAD_DIM, axis=axis) for hd in order]
    return jnp.concatenate(parts, axis=axis)


def _pad_heads(a):
    return jnp.pad(a.reshape(8, HEAD_DIM, -1), ((0, 0), (0, LANES - HEAD_DIM), (0, 0))).reshape(8 * LANES, -1)


def _unpad_heads(a):
    return a.reshape(8, LANES, -1)[:, :HEAD_DIM].reshape(8 * HEAD_DIM, -1)


def _swa_rows(a, inverse=False):
    shape = (4, 2) if inverse else (2, 4)
    return a.reshape(shape + (HEAD_DIM, -1)).transpose(1, 0, 2, 3).reshape(a.shape)


def _w_in_rows(d):
    return np.cumsum([0, 512, 512, 512, 8, 512, 128, 128, d, d])


def _reorder_w_in(wt, cols):
    o = _w_in_rows(cols.d)
    qa, ka, va, fa, qb, kb, vb, ga, gb = [wt[o[i]:o[i + 1]] for i in range(9)]
    zeros = jnp.zeros((cols.np - cols.fa - 8, wt.shape[1]), wt.dtype)
    return jnp.concatenate([ga, gb, _pad_heads(qa), _pad_heads(ka), _pad_heads(va), _swa_rows(qb), kb, vb, fa, zeros],
                           axis=0)


def _restore_w_in(wpt, cols, width):
    seg = lambda off, n: wpt[off:off + n]
    rows = jnp.concatenate([_unpad_heads(seg(cols.qa, 1024)), _unpad_heads(seg(cols.ka, 1024)),
                            _unpad_heads(seg(cols.va, 1024)), seg(cols.fa, 8), _swa_rows(seg(cols.qb, 512), True),
                            seg(cols.kb, 128), seg(cols.vb, 128), seg(cols.ga, cols.d), seg(cols.gb, cols.d)], axis=0)
    return rows.reshape(N_DEV, width, -1)


def _lane_pad(v):
    return jnp.pad(v, ((0, 0), (0, LANES - v.shape[1])))


def kernel(x, meta_tokens, rel_bias_table, ffn1_norm, ffn1_w_in, ffn1_w_out, mix_norm, w_in, forget_bias, fox_q_norm, fox_k_norm, swa_q_norm, swa_k_norm, swa_sinks, w_branch_fox, w_branch_swa, w_out, ffn2_norm, ffn2_w_in, ffn2_w_out, loss_target, m_meta_tokens, m_rel_bias_table, m_ffn1_norm, m_ffn1_w_in, m_ffn1_w_out, m_mix_norm, m_w_in, m_forget_bias, m_fox_q_norm, m_fox_k_norm, m_swa_q_norm, m_swa_k_norm, m_swa_sinks, m_w_branch_fox, m_w_branch_swa, m_w_out, m_ffn2_norm, m_ffn2_w_in, m_ffn2_w_out, v_meta_tokens, v_rel_bias_table, v_ffn1_norm, v_ffn1_w_in, v_ffn1_w_out, v_mix_norm, v_w_in, v_forget_bias, v_fox_q_norm, v_fox_k_norm, v_swa_q_norm, v_swa_k_norm, v_swa_sinks, v_w_branch_fox, v_w_branch_swa, v_w_out, v_ffn2_norm, v_ffn2_w_in, v_ffn2_w_out):
    args = dict(locals())
    wts = {n: args[n] for n in WEIGHTS}
    mom1 = {n: args["m_" + n] for n in WEIGHTS}
    mom2 = {n: args["v_" + n] for n in WEIGHTS}

    seq, d = x.shape[1], x.shape[2]
    m_rows = seq + LANES
    depth = ffn1_norm.shape[0]
    fb = ffn1_w_in.shape[2]
    fo = ffn1_w_out.shape[1]
    din_shard = w_in.shape[2]
    cols = _Cols(d)
    scale = HEAD_DIM ** -0.5
    dev = 4 * lax.axis_index("x") + 2 * lax.axis_index("y") + lax.axis_index("c")

    groups = {"ffn1": ("ffn1_w_in", "ffn1_w_out"), "mix": ("w_in", "w_branch_fox", "w_branch_swa", "w_out"),
              "ffn2": ("ffn2_w_in", "ffn2_w_out"), "ffn1_in": ("ffn1_w_in",), "ffn1_out": ("ffn1_w_out",),
              "ffn2_in": ("ffn2_w_in",), "ffn2_out": ("ffn2_w_out",)}
    flipped = ("ffn1_w_in", "ffn2_w_in")
    for n in flipped:
        wts[n], mom1[n], mom2[n] = (jnp.swapaxes(a, 1, 2) for a in (wts[n], mom1[n], mom2[n]))
    to_rows = lambda a: jnp.transpose(a, (2, 0, 1)).reshape(din_shard, depth * d)
    from_rows = lambda a: jnp.transpose(a.reshape(din_shard, depth, d), (1, 2, 0))
    wts["w_in"], mom1["w_in"], mom2["w_in"] = (to_rows(a) for a in (wts["w_in"], mom1["w_in"], mom2["w_in"]))
    shard = {n: wts[n].astype(BF16) for n in BIG}
    w_in_rows = shard.pop("w_in")
    shard["w_in"] = [w_in_rows[:, l * d:(l + 1) * d] for l in range(depth)]
    full, parts, gw = {}, {}, {}

    def keys_of(stages):
        return [(n, l) for g, l in stages if l < depth for n in groups[g]]

    def gather_rider(stages):
        return Rider([shard[n][l] for n, l in keys_of(stages)], True)

    def scatter_rider(stages):
        return Rider([gw[k] for k in keys_of(stages)], False)

    def ffn_weights(tag, l):
        return full[tag + "_w_in", l], full[tag + "_w_out", l].reshape(4, fb, d)

    def mixer_weights(l):
        wp = _reorder_w_in(full["w_in", l].reshape(N_DEV * din_shard, d), cols)
        wbf = jnp.concatenate([full["w_branch_fox", l][j] for j in range(N_DEV)], axis=1)
        wbf = jnp.pad(wbf.reshape(8, HEAD_DIM, d), ((0, 0), (0, LANES - HEAD_DIM), (0, 0))).reshape(8 * LANES, d)
        wbs = _permute_heads(jnp.concatenate([full["w_branch_swa", l][j] for j in range(N_DEV)], axis=1), 0)
        return wp, wbf, wbs, full["w_out", l].reshape(d, d)

    full.update(zip(keys_of([("ffn1", 0)]), exchange_hbm(gather_rider([("ffn1", 0)]).srcs, True, "gather_first")))
    meta_all = gather_small(meta_tokens.reshape(1, N_META, -1), "gather_meta")
    meta_full = meta_all.transpose(1, 0, 2).reshape(N_META, d)
    tile8 = lambda g, s=1.0: jnp.tile(g.reshape(1, HEAD_DIM) * s, (1, 8))
    tile2 = lambda g: jnp.tile(g.reshape(1, HEAD_DIM), (1, 2))
    data_lanes = lambda g, s=1.0: _lane_pad(g.reshape(1, HEAD_DIM) * s)
    bias = bias_build(rel_bias_table, "swa_bias")

    first = jnp.concatenate([jnp.zeros((PAD_FRONT, d), F32), meta_full], axis=0)
    h = jnp.concatenate([first, x[0]], axis=0)
    saved, lw = [], []
    for l in range(depth):
        s, w = {"h0": h}, {}
        w["ffn1_in"], w["ffn1_out"] = ffn_weights("ffn1", l)
        stages = [("mix", l)]
        (h, s["n1"], s["a1"], s["fg1"], s["fu1"]), got = ffn_fwd(h, ffn1_norm[l:l + 1], w["ffn1_in"], w["ffn1_out"],
                                                          f"ffn1_fwd_{l}", gather_rider(stages))
        full.update(zip(keys_of(stages), got))
        s["h1"] = h
        w["wp"], w["wbf"], w["wbs"], w["wo"] = mixer_weights(l)
        s["nm"], s["proj"] = mixer_proj(h, mix_norm[l:l + 1], w["wp"], f"mixer_proj_{l}")
        s["gains"] = (data_lanes(fox_q_norm[l], scale), data_lanes(fox_k_norm[l]), tile8(swa_q_norm[l], scale),
                      tile2(swa_k_norm[l]))
        s["fbias"] = _lane_pad(forget_bias[l:l + 1])
        qf, kf, vf, kt, vt, qb, kb, vb = qk_post(s["proj"], s["gains"], s["fbias"], cols, f"qk_post_{l}")
        s.update(qf=qf, kf=kf, vf=vf, kt=kt, qb=qb, kb=kb, vb=vb)
        stages = [("ffn2", l)]
        (s["o_fox"], s["lse_fox"]), got = fox_fwd(qf, kf, vt, f"fox_fwd_{l}", gather_rider(stages))
        full.update(zip(keys_of(stages), got))
        stages = [("ffn1_out", l + 1)]
        (s["o_swa"], s["lse_swa"]), got = swa_fwd(qb, kb, vb, bias, swa_sinks[l], f"swa_fwd_{l}", gather_rider(stages))
        full.update(zip(keys_of(stages), got))
        h = branch_out(h, s["o_fox"], s["o_swa"], s["proj"], w["wbf"], w["wbs"], w["wo"], cols, f"branch_out_{l}")
        s["h2"] = h
        w["ffn2_in"], w["ffn2_out"] = ffn_weights("ffn2", l)
        stages = [("ffn1_in", l + 1)]
        (h, s["n2"], s["a2"], s["fg2"], s["fu2"]), got = ffn_fwd(h, ffn2_norm[l:l + 1], w["ffn2_in"], w["ffn2_out"],
                                                          f"ffn2_fwd_{l}", gather_rider(stages))
        full.update(zip(keys_of(stages), got))
        saved.append(s)
        lw.append(w)

    dh, loss_part = loss_head(h, loss_target[0], "loss_head")

    gs = {n: [None] * depth for n in SMALL}
    dbias_total = None
    for l in reversed(range(depth)):
        w, s = lw[l], saved[l]

        def ffn_back(dh, tag, hin, norm, n_in, a, f_gate, f_up, stages):
            (dh_in, dg, du, dgn, dhs), got = ffn_bwd(dh, hin, norm, f_gate, f_up, w[tag + "_in"], w[tag + "_out"],
                                                     f"{tag}_bwd_{l}", scatter_rider(stages))
            parts.update(zip(keys_of(stages), got))
            gw[tag + "_w_out", l] = matmul_tn(a, dhs[None], f"{tag}_dwo_{l}").reshape(N_DEV, fo, d)
            stages = [(tag + "_out", l)]
            gw[tag + "_w_in", l], got = matmul_tn(dg, n_in[None], f"{tag}_dwi_{l}", x2=du,
                                                  rider=scatter_rider(stages))
            parts.update(zip(keys_of(stages), got))
            return dh_in, dgn

        dh, gs["ffn2_norm"][l] = ffn_back(dh, "ffn2", s["h2"], ffn2_norm[l:l + 1], s["n2"], s["a2"], s["fg2"],
                                          s["fu2"], [("ffn1_in", l + 1)])

        y, dtf, dts, dga, dgb, dof, dos, delta = branch_out_bwd(dh, s["o_fox"], s["o_swa"], s["proj"], w["wbf"],
                                                                w["wbs"], w["wo"], cols, f"branch_out_bwd_{l}")
        gw["w_out", l] = matmul_tn(y[None], dh[None], f"dw_out_{l}").reshape(N_DEV, d // N_DEV, d)
        to_shards = lambda a: a.reshape(512, N_DEV, d // N_DEV).transpose(1, 0, 2)
        gw["w_branch_fox", l] = to_shards(matmul_tn(s["o_fox"][None], dtf[None], f"dw_branch_fox_{l}")[0]
                                          .reshape(8, LANES, d)[:, :HEAD_DIM].reshape(512, d))
        gw["w_branch_swa", l] = to_shards(_permute_heads(
            matmul_tn(s["o_swa"][None], dts[None], f"dw_branch_swa_{l}")[0], 0, inverse=True))

        stages = [("ffn2_in", l)]
        (dkf, dvf, dqf, dc_rows), got = fox_bwd(s["qf"], s["kf"], s["vf"], s["kt"], dof, s["lse_fox"],
                                         delta[:, :8].T.reshape(8, 1, m_rows), f"fox_bwd_{l}", scatter_rider(stages))
        parts.update(zip(keys_of(stages), got))
        dc = _lane_pad(dc_rows.reshape(8, m_rows).T)
        dqb, dkb, dvb, dbias, dsink = swa_bwd(s["qb"], s["kb"], s["vb"], bias, swa_sinks[l], s["o_swa"], s["lse_swa"],
                                              dos, f"swa_bwd_{l}")
        dbias_total = dbias if dbias_total is None else dbias_total + dbias
        gs["swa_sinks"][l] = dsink[0, :8]
        dproj, ggqa, ggka, ggqb, ggkb, gfb = qk_post_bwd(s["proj"], s["gains"], s["fbias"], dqf, dkf, dvf, dqb, dkb,
                                                         dvb, dc, dga, dgb, cols, f"qk_post_bwd_{l}")
        gs["fox_q_norm"][l] = ggqa[0, :HEAD_DIM] * scale
        gs["fox_k_norm"][l] = ggka[0, :HEAD_DIM]
        gs["swa_q_norm"][l] = ggqb.reshape(8, HEAD_DIM).sum(0) * scale
        gs["swa_k_norm"][l] = ggkb.reshape(2, HEAD_DIM).sum(0)
        gs["forget_bias"][l] = gfb[0, :8]
        dwp = matmul_tn(dproj[None], s["nm"][None], f"dw_in_{l}", tk=1024 if cols.np % 1024 == 0 else cols.np)[0]
        gw["w_in", l] = _restore_w_in(dwp, cols, din_shard)
        dh, gs["mix_norm"][l] = dproj_bwd(dh, s["h1"], mix_norm[l:l + 1], dproj, w["wp"], f"dproj_bwd_{l}")

        dh, gs["ffn1_norm"][l] = ffn_back(dh, "ffn1", s["h0"], ffn1_norm[l:l + 1], s["n1"], s["a1"], s["fg1"],
                                          s["fu1"], [("mix", l)])

    grad_x = dh[LANES:][None]
    dmeta = dh[PAD_FRONT:LANES]
    dtable = bias_reduce(dbias_total, "swa_dbias")[:, :8]

    last = ("ffn1_w_in", 0)
    send_sems, recv_sems, src_thru, land_thru, token = scatter_start(gw[last], "scatter_last_start")
    big_out = [{}, {}, {}, {}]
    for n in BIG:
        if n != last[0]:
            update = adamw_sum_cols if n == "w_in" else adamw_sum
            outs = update([parts[n, l] for l in range(depth)], wts[n], mom1[n], mom2[n], f"adamw_{n}", after=token)
            for k in range(4):
                big_out[k][n] = outs[k]
    sent, landed = scatter_wait(send_sems, recv_sems, src_thru, land_thru,
                                [big_out[1][n] for n in BIG if n != last[0]], "scatter_last_wait")
    parts[last] = lax.dynamic_update_slice_in_dim(landed, lax.dynamic_slice_in_dim(sent, dev, 1, axis=0), dev, axis=0)
    outs = adamw_sum([parts[last[0], l] for l in range(depth)], wts[last[0]], mom1[last[0]], mom2[last[0]],
                     f"adamw_{last[0]}")
    for k in range(4):
        big_out[k][last[0]] = outs[k]

    small_g = {n: (jnp.stack(gs[n]) if n != "rel_bias_table" else None) for n in SMALL}
    small_g["rel_bias_table"] = dtable
    pieces = [loss_part[0:1, 0:1].reshape(1, 1)] + [small_g[n].reshape(1, -1) for n in SMALL] + [dmeta.reshape(1, -1)]
    small_shapes = [(1,)] + [wts[n].shape for n in SMALL] + [(N_META, d)]
    total = allsum_small(_pack(pieces, LANES, 8, F32), "allsum_small")
    summed = _unpack(total, small_shapes)
    loss = summed[0][0]
    g_small = dict(zip(SMALL, summed[1:1 + len(SMALL)]))
    g_meta = lax.dynamic_slice_in_dim(summed[-1], dev * (d // N_DEV), d // N_DEV, axis=1)
    names = SMALL + ("meta_tokens",)
    g_small["meta_tokens"] = g_meta
    pk = lambda src: _pack([src[n].reshape(1, -1) for n in names], LANES, 8, F32)[0]
    small_out = [dict(zip(names, _unpack(o, [wts[n].shape for n in names])))
                 for o in adamw_small(pk(g_small), pk(wts), pk(mom1), pk(mom2), "adamw_small")]

    for out in big_out:
        for n in flipped:
            out[n] = jnp.swapaxes(out[n], 1, 2)
        out["w_in"] = from_rows(out["w_in"])
    grads = {**big_out[0], **g_small}
    delta = {**big_out[1], **small_out[0]}
    new_m = {**big_out[2], **small_out[1]}
    new_v = {**big_out[3], **small_out[2]}
    return (loss, grad_x, *[grads[n] for n in WEIGHTS], *[delta[n] for n in WEIGHTS],
            *[new_m[n] for n in WEIGHTS], *[new_v[n] for n in WEIGHTS])
```

```python
import math

import numpy as np
import jax
import jax.numpy as jnp
from jax import lax
from jax.experimental import pallas as pl
from jax.experimental.pallas import tpu as pltpu

F32 = jnp.float32
BF16 = jnp.bfloat16
EPS = 1e-6
NEG = -1e30
HEAD_DIM = 64
LANES = 128
N_META = 16
PAD_FRONT = LANES - N_META
N_BUCKETS = 32
MAX_DISTANCE = 128
N_DEV = 8
ADAM_LR, ADAM_B1, ADAM_B2, ADAM_EPS, ADAM_WD, ADAM_STEP = 0.001, 0.9, 0.999, 1e-08, 0.01, 10
VMEM_LIMIT = 56 * 1024 * 1024
MESH = pl.DeviceIdType.MESH


def _params(n_grid):
    return pltpu.CompilerParams(dimension_semantics=("arbitrary",) * n_grid,
                                vmem_limit_bytes=VMEM_LIMIT)


def _dot(a, b):
    return jnp.dot(a, b, preferred_element_type=F32)


def _dot_nt(a, b):
    return lax.dot_general(a, b, (((1,), (1,)), ((), ())), preferred_element_type=F32)


def _dot_tn(a, b):
    return lax.dot_general(a, b, (((0,), (0,)), ((), ())), preferred_element_type=F32)


def _rms(x):
    r = lax.rsqrt(jnp.mean(x * x, axis=-1, keepdims=True) + EPS)
    return x * r, r


def _rms_bwd(x, g, dn):
    xh, r = _rms(x)
    dxh = dn * g
    dx = r * (dxh - xh * jnp.mean(dxh * xh, axis=-1, keepdims=True))
    return dx, jnp.sum(dn * xh, axis=0, keepdims=True)


def _split2(v):
    hi = v.astype(BF16)
    return hi, (v - hi.astype(F32)).astype(BF16)


def _split3(v):
    hi = v.astype(BF16)
    r1 = v - hi.astype(F32)
    mid = r1.astype(BF16)
    return hi, mid, (r1 - mid.astype(F32)).astype(BF16)


def _group_ones():
    r = lax.broadcasted_iota(jnp.int32, (LANES, LANES), 0) // HEAD_DIM
    c = lax.broadcasted_iota(jnp.int32, (LANES, LANES), 1) // HEAD_DIM
    return jnp.where(r == c, 1.0, 0.0).astype(BF16)


def _group_mean(v, ones):
    hi, lo = _split2(v)
    return (_dot(hi, ones) + _dot(lo, ones)) * (1.0 / HEAD_DIM)


def _row_tile(m):
    return 384 if m % 384 == 0 else LANES


def _tile(m, cap):
    return max(t for t in range(16, cap + 1, 16) if m % t == 0)


def _peer(k):
    x, y, c = lax.axis_index("x"), lax.axis_index("y"), lax.axis_index("c")
    px = 1 - x if k & 4 else x
    py = 1 - y if k & 2 else y
    pc = 1 - c if k & 1 else c
    return (px, py, pc), 4 * px + 2 * py + pc


def _exchange_body(src_ref, dst_ref, send_sems, recv_sems, local_sem, bcast):
    x, y, c = lax.axis_index("x"), lax.axis_index("y"), lax.axis_index("c")
    me = 4 * x + 2 * y + c
    mine = pltpu.make_async_copy(src_ref.at[0 if bcast else me], dst_ref.at[me], local_sem)
    mine.start()
    sends = []
    for k in range(1, N_DEV):
        dev, idx = _peer(k)
        cp = pltpu.make_async_remote_copy(
            src_ref=src_ref.at[0 if bcast else idx], dst_ref=dst_ref.at[me],
            send_sem=send_sems.at[k - 1], recv_sem=recv_sems.at[k - 1],
            device_id=dev, device_id_type=MESH)
        cp.start()
        sends.append(cp)
    for k in range(1, N_DEV):
        dev, idx = _peer(k)
        pltpu.make_async_remote_copy(
            src_ref=src_ref.at[0], dst_ref=dst_ref.at[idx],
            send_sem=send_sems.at[k - 1], recv_sem=recv_sems.at[k - 1],
            device_id=dev, device_id_type=MESH).wait_recv()
    for cp in sends:
        cp.wait_send()
    mine.wait()


class Rider:
    FIRST = (1, 2, 4, 6)
    RELAYED = (2, 4, 6)

    def __init__(self, srcs=(), bcast=True):
        self.srcs, self.bcast, self.n = list(srcs), bcast, len(srcs)

    def out_shapes(self):
        return [jax.ShapeDtypeStruct(((N_DEV,) + s.shape) if self.bcast else s.shape, s.dtype) for s in self.srcs]

    def specs(self):
        return [pl.BlockSpec(memory_space=pl.ANY)] * self.n

    def scratch(self):
        if not self.n:
            return []
        return [pltpu.SemaphoreType.DMA((self.n * (N_DEV - 1),)), pltpu.SemaphoreType.DMA((self.n * (N_DEV - 1),)),
                pltpu.SemaphoreType.DMA((self.n,))]

    @staticmethod
    def _copy(src, dst, a, pair, dev, send_sems, recv_sems):
        sem = a * (N_DEV - 1) + pair - 1
        return pltpu.make_async_remote_copy(src_ref=src, dst_ref=dst, send_sem=send_sems.at[sem],
                                            recv_sem=recv_sems.at[sem], device_id=dev, device_id_type=MESH)

    def _first(self):
        return self.FIRST if self.bcast else range(1, N_DEV)

    def _own(self, s, d, a, local_sems):
        me = 4 * lax.axis_index("x") + 2 * lax.axis_index("y") + lax.axis_index("c")
        return pltpu.make_async_copy(s if self.bcast else s.at[me], d.at[me], local_sems.at[a]), me

    def start(self, src_refs, dst_refs, send_sems, recv_sems, local_sems):
        for a, (s, d) in enumerate(zip(src_refs, dst_refs)):
            own, me = self._own(s, d, a, local_sems)
            own.start()
            for k in self._first():
                dev, idx = _peer(k)
                self._copy(s if self.bcast else s.at[idx], d.at[me], a, k, dev, send_sems, recv_sems).start()

    def relay(self, src_refs, dst_refs, send_sems, recv_sems, local_sems):
        if not self.bcast:
            return
        sibling, _ = _peer(1)
        for a, d in enumerate(dst_refs):
            for k in self.RELAYED:
                dev, idx = _peer(k)
                self._copy(d.at[idx], d.at[idx], a, k, dev, send_sems, recv_sems).wait_recv()
                self._copy(d.at[idx], d.at[idx], a, k + 1, sibling, send_sems, recv_sems).start()

    def wait(self, src_refs, dst_refs, send_sems, recv_sems, local_sems):
        sibling, _ = _peer(1)
        for a, (s, d) in enumerate(zip(src_refs, dst_refs)):
            own, me = self._own(s, d, a, local_sems)
            for k in range(1, N_DEV):
                if not (self.bcast and k in self.RELAYED):
                    dev, idx = _peer(k)
                    self._copy(d.at[idx], d.at[idx], a, k, dev, send_sems, recv_sems).wait_recv()
            for k in self._first():
                dev, idx = _peer(k)
                self._copy(s if self.bcast else s.at[idx], d.at[me], a, k, dev, send_sems, recv_sems).wait_send()
            if self.bcast:
                for k in self.RELAYED:
                    dev, idx = _peer(k)
                    self._copy(d.at[idx], d.at[idx], a, k + 1, sibling, send_sems, recv_sems).wait_send()
            own.wait()


RELAY_AT = 7


def rider_call(core, name, grid, in_specs, out_specs, out_shape, scratch_shapes, args, rider=None):
    rider = rider or Rider()
    n_in, n_out, n_scr, nr = len(in_specs), len(out_specs), len(scratch_shapes), rider.n

    def body(*refs):
        ins, r_src = refs[:n_in], refs[n_in:n_in + nr]
        outs = refs[n_in + nr:n_in + nr + n_out]
        r_dst = refs[n_in + nr + n_out:n_in + 2 * nr + n_out]
        scr = refs[n_in + 2 * nr + n_out:n_in + 2 * nr + n_out + n_scr]
        sems = refs[n_in + 2 * nr + n_out + n_scr:]
        if nr:
            first, last, step, steps = True, True, 0, 1
            for ax, size in enumerate(grid):
                first = first & (pl.program_id(ax) == 0)
                last = last & (pl.program_id(ax) == size - 1)
                step = step * size + pl.program_id(ax)
                steps *= size
            relay = step == RELAY_AT * steps // 8
            if not grid:
                rider.start(r_src, r_dst, *sems)
                rider.relay(r_src, r_dst, *sems)
            else:
                pl.when(first)(lambda: rider.start(r_src, r_dst, *sems))
                if rider.bcast:
                    pl.when(relay)(lambda: rider.relay(r_src, r_dst, *sems))
        core(*ins, *outs, *scr)
        if nr:
            if not grid:
                rider.wait(r_src, r_dst, *sems)
            else:
                pl.when(last)(lambda: rider.wait(r_src, r_dst, *sems))

    res = pl.pallas_call(
        body, name=name, grid=grid,
        in_specs=list(in_specs) + rider.specs(),
        out_specs=list(out_specs) + rider.specs(),
        out_shape=list(out_shape) + rider.out_shapes(),
        scratch_shapes=list(scratch_shapes) + rider.scratch(),
        compiler_params=_params(len(grid)),
    )(*args, *rider.srcs)
    return res[:n_out], res[n_out:]


def exchange_hbm(srcs, bcast, name):
    return rider_call(lambda: None, name, (), [], [], [], [], [], Rider(srcs, bcast))[1]


_HBM = pl.BlockSpec(memory_space=pltpu.HBM)
_SEM = pl.BlockSpec(memory_space=pltpu.SEMAPHORE)
_EFFECT = pltpu.CompilerParams(has_side_effects=pltpu.SideEffectType.DATAFLOW_SIDE_EFFECTING)


def scatter_start(src, name):
    def body(src_ref, land_ref, send_sems, recv_sems, src_thru, land_thru, token):
        me = 4 * lax.axis_index("x") + 2 * lax.axis_index("y") + lax.axis_index("c")
        for k in range(1, N_DEV):
            dev, idx = _peer(k)
            pltpu.make_async_remote_copy(src_ref=src_ref.at[idx], dst_ref=land_ref.at[me], send_sem=send_sems.at[k - 1],
                                         recv_sem=recv_sems.at[k - 1], device_id=dev, device_id_type=MESH).start()
        token[...] = jnp.zeros_like(token)

    return pl.pallas_call(
        body, name=name,
        out_shape=(pltpu.SemaphoreType.DMA((N_DEV - 1,)), pltpu.SemaphoreType.DMA((N_DEV - 1,)),
                   pltpu.HBM(src.shape, src.dtype), pltpu.HBM(src.shape, src.dtype), jax.ShapeDtypeStruct((8, LANES), F32)),
        in_specs=(_HBM, _HBM), out_specs=(_SEM, _SEM, _HBM, _HBM, pl.BlockSpec(memory_space=pltpu.VMEM)),
        input_output_aliases={0: 2, 1: 3}, compiler_params=_EFFECT,
    )(pltpu.with_memory_space_constraint(src, pltpu.HBM),
      pltpu.with_memory_space_constraint(lax.empty(src.shape, src.dtype), pltpu.HBM))


def scatter_wait(send_sems, recv_sems, src_thru, land_thru, after, name):
    n_after = len(after)

    def body(*refs):
        src_ref, land_ref, send_sems, recv_sems = refs[:4]
        for k in range(1, N_DEV):
            dev, idx = _peer(k)
            copy = pltpu.make_async_remote_copy(src_ref=src_ref.at[idx], dst_ref=land_ref.at[idx],
                                                send_sem=send_sems.at[k - 1], recv_sem=recv_sems.at[k - 1],
                                                device_id=dev, device_id_type=MESH)
            copy.wait_send()
            copy.wait_recv()

    return pl.pallas_call(
        body, name=name,
        out_shape=(pltpu.HBM(src_thru.shape, src_thru.dtype), pltpu.HBM(land_thru.shape, land_thru.dtype)),
        in_specs=(_HBM, _HBM, _SEM, _SEM) + (pl.BlockSpec(memory_space=pl.ANY),) * n_after, out_specs=(_HBM, _HBM),
        input_output_aliases={0: 0, 1: 1}, compiler_params=_EFFECT,
    )(src_thru, land_thru, send_sems, recv_sems, *after)


def allsum_small(vec, name):
    def body(src_ref, out_ref, dst_ref, send_sems, recv_sems, local_sem):
        _exchange_body(src_ref, dst_ref, send_sems, recv_sems, local_sem, True)
        acc = dst_ref[0]
        for j in range(1, N_DEV):
            acc = acc + dst_ref[j]
        out_ref[...] = acc

    return pl.pallas_call(
        body, name=name,
        out_shape=jax.ShapeDtypeStruct(vec.shape[1:], F32),
        in_specs=[pl.BlockSpec(memory_space=pltpu.VMEM)],
        out_specs=pl.BlockSpec(memory_space=pltpu.VMEM),
        scratch_shapes=[pltpu.VMEM((N_DEV,) + vec.shape[1:], F32),
                        pltpu.SemaphoreType.DMA((N_DEV - 1,)), pltpu.SemaphoreType.DMA((N_DEV - 1,)),
                        pltpu.SemaphoreType.DMA],
    )(vec)


def gather_small(vec, name):
    def body(src_ref, dst_ref, send_sems, recv_sems, local_sem):
        _exchange_body(src_ref, dst_ref, send_sems, recv_sems, local_sem, True)

    return pl.pallas_call(
        body, name=name,
        out_shape=jax.ShapeDtypeStruct((N_DEV,) + vec.shape[1:], F32),
        in_specs=[pl.BlockSpec(memory_space=pltpu.VMEM)],
        out_specs=pl.BlockSpec(memory_space=pltpu.VMEM),
        scratch_shapes=[pltpu.SemaphoreType.DMA((N_DEV - 1,)), pltpu.SemaphoreType.DMA((N_DEV - 1,)),
                        pltpu.SemaphoreType.DMA],
    )(vec)


FFN_FWD_ROWS = 1056
FFN_BWD_ROWS = 704
FFN_BWD_CHUNKS = 4
DW_ROWS = 1408

def ffn_fwd(h, g, w_in8, w_out4, name, rider=None):
    m, d = h.shape
    fb = w_in8.shape[1]
    tm = _tile(m, FFN_FWD_ROWS)

    def body(h_ref, g_ref, wg_ref, wu_ref, wo_ref, hn_ref, n_ref, a_ref, fg_ref, fu_ref, acc_ref):
        i = pl.program_id(1)

        @pl.when(i == 0)
        def _():
            xh, _ = _rms(h_ref[...])
            n_ref[...] = (xh * g_ref[...]).astype(BF16)
            acc_ref[...] = jnp.zeros_like(acc_ref)

        n = n_ref[...]
        gate = _dot_nt(n, wg_ref[0])
        up = _dot_nt(n, wu_ref[0])
        sg = jax.nn.sigmoid(gate)
        silu = gate * sg
        a = (silu * up).astype(BF16)
        a_ref[0] = a
        fg_ref[0] = (up * (sg * (1.0 + gate * (1.0 - sg)))).astype(BF16)
        fu_ref[0] = silu.astype(BF16)
        acc_ref[...] += _dot(a, wo_ref[0])

        @pl.when(i == 3)
        def _():
            hn_ref[...] = h_ref[...] + 0.5 * acc_ref[...]

    return rider_call(
        body, name, (m // tm, 4),
        in_specs=[pl.BlockSpec((tm, d), lambda r, i: (r, 0)),
                  pl.BlockSpec((1, d), lambda r, i: (0, 0)),
                  pl.BlockSpec((1, fb, d), lambda r, i: (i, 0, 0)),
                  pl.BlockSpec((1, fb, d), lambda r, i: (i + 4, 0, 0)),
                  pl.BlockSpec((1, fb, d), lambda r, i: (i, 0, 0))],
        out_specs=[pl.BlockSpec((tm, d), lambda r, i: (r, 0)),
                   pl.BlockSpec((tm, d), lambda r, i: (r, 0))] + [pl.BlockSpec((1, tm, fb), lambda r, i: (i, r, 0))] * 3,
        out_shape=[jax.ShapeDtypeStruct((m, d), F32), jax.ShapeDtypeStruct((m, d), BF16)]
                  + [jax.ShapeDtypeStruct((4, m, fb), BF16)] * 3,
        scratch_shapes=[pltpu.VMEM((tm, d), F32)],
        args=(h, g, w_in8, w_in8, w_out4), rider=rider)


def ffn_bwd(dh, h, g, f_gate, f_up, w_in8, w_out4, name, rider=None):
    m, d = h.shape
    fb = w_in8.shape[1]
    tm = _tile(m, FFN_BWD_ROWS)

    def body(dh_ref, h_ref, g_ref, fg_ref, fu_ref, wg_ref, wu_ref, wo_ref,
             dhin_ref, dg_ref, du_ref, dgn_ref, dhs_ref, acc_ref):
        r = pl.program_id(0)
        i = pl.program_id(1)

        @pl.when(i == 0)
        def _():
            dhs_ref[...] = (0.5 * dh_ref[...]).astype(BF16)
            acc_ref[...] = jnp.zeros_like(acc_ref)

        @pl.when((r == 0) & (i == 0))
        def _():
            dgn_ref[...] = jnp.zeros_like(dgn_ref)

        chunks = FFN_BWD_CHUNKS if tm % (16 * FFN_BWD_CHUNKS) == 0 else 1
        for c in range(chunks):
            rows = slice(c * tm // chunks, (c + 1) * tm // chunks)
            da = _dot_nt(dhs_ref[rows, :], wo_ref[0])
            dub = (da * fu_ref[0, rows, :]).astype(BF16)
            dgb = (da * fg_ref[0, rows, :]).astype(BF16)
            dg_ref[0, rows, :] = dgb
            du_ref[0, rows, :] = dub
            acc_ref[rows, :] += _dot(dgb, wg_ref[0]) + _dot(dub, wu_ref[0])

        @pl.when(i == 3)
        def _():
            dx, dgain = _rms_bwd(h_ref[...], g_ref[...], acc_ref[...])
            dgn_ref[...] += dgain
            dhin_ref[...] = dh_ref[...] + dx

    row = lambda r, i: (r, 0)
    blk = lambda r, i: (i, r, 0)
    return rider_call(
        body, name, (m // tm, 4),
        in_specs=[pl.BlockSpec((tm, d), row), pl.BlockSpec((tm, d), row),
                  pl.BlockSpec((1, d), lambda r, i: (0, 0)),
                  pl.BlockSpec((1, tm, fb), blk), pl.BlockSpec((1, tm, fb), blk),
                  pl.BlockSpec((1, fb, d), lambda r, i: (i, 0, 0)),
                  pl.BlockSpec((1, fb, d), lambda r, i: (i + 4, 0, 0)),
                  pl.BlockSpec((1, fb, d), lambda r, i: (i, 0, 0))],
        out_specs=[pl.BlockSpec((tm, d), row),
                   pl.BlockSpec((1, tm, fb), blk), pl.BlockSpec((1, tm, fb), blk),
                   pl.BlockSpec((1, d), lambda r, i: (0, 0)),
                   pl.BlockSpec((tm, d), row)],
        out_shape=[jax.ShapeDtypeStruct((m, d), F32),
                   jax.ShapeDtypeStruct((4, m, fb), BF16), jax.ShapeDtypeStruct((4, m, fb), BF16),
                   jax.ShapeDtypeStruct((1, d), F32), jax.ShapeDtypeStruct((m, d), BF16)],
        scratch_shapes=[pltpu.VMEM((tm, d), F32)],
        args=(dh, h, g, f_gate, f_up, w_in8, w_in8, w_out4), rider=rider)


def matmul_tn(x, y, name, tn=None, tk=None, x2=None, rider=None):
    if tk is not None:
        assert x.shape[0] == y.shape[0] == 1 and x2 is None and rider is None
        bk, m, kf = x.shape[2] // tk, x.shape[1], x.shape[2]
        tm = _tile(m, DW_ROWS)
        tn_ = y.shape[2] if tn is None else tn

        def tiled(x_ref, y_ref, o_ref, acc_ref):
            r = pl.program_id(2)

            @pl.when(r == 0)
            def _():
                acc_ref[...] = jnp.zeros_like(acc_ref)

            acc_ref[...] += _dot_tn(x_ref[0].astype(BF16), y_ref[0].astype(BF16))

            @pl.when(r == m // tm - 1)
            def _():
                o_ref[0] = acc_ref[...].astype(BF16)

        return pl.pallas_call(
            tiled, name=name, grid=(bk, y.shape[2] // tn_, m // tm),
            in_specs=[pl.BlockSpec((1, tm, tk), lambda i, j, r: (0, r, i)),
                      pl.BlockSpec((1, tm, tn_), lambda i, j, r: (0, r, j))],
            out_specs=pl.BlockSpec((1, tk, tn_), lambda i, j, r: (0, i, j)),
            out_shape=jax.ShapeDtypeStruct((1, kf, y.shape[2]), BF16),
            scratch_shapes=[pltpu.VMEM((tk, tn_), F32)],
            compiler_params=_params(3),
        )(x, y)
    bx, m, k = x.shape
    by, _, n = y.shape
    b = max(bx, by) * (2 if x2 is not None else 1)
    tm = _tile(m, DW_ROWS)
    tn = n if tn is None else tn
    nt = n // tn
    nr = m // tm

    def body(*refs):
        x_ref, y_ref = refs[0], refs[-3]
        o_ref, acc_ref = refs[-2], refs[-1]
        r = pl.program_id(2)

        @pl.when(r == 0)
        def _():
            acc_ref[...] = jnp.zeros_like(acc_ref)

        if x2 is None:
            acc_ref[...] += _dot_tn(x_ref[0].astype(BF16), y_ref[0].astype(BF16))
        else:
            @pl.when(pl.program_id(0) < bx)
            def _():
                acc_ref[...] += _dot_tn(x_ref[0].astype(BF16), y_ref[0].astype(BF16))

            @pl.when(pl.program_id(0) >= bx)
            def _():
                acc_ref[...] += _dot_tn(refs[1][0].astype(BF16), y_ref[0].astype(BF16))

        @pl.when(r == nr - 1)
        def _():
            o_ref[0] = acc_ref[...].astype(BF16)

    if x2 is None:
        x_specs = [pl.BlockSpec((1, tm, k), (lambda i, j, r: (i, r, 0)) if bx > 1 else (lambda i, j, r: (0, r, 0)))]
    else:
        x_specs = [pl.BlockSpec((1, tm, k), lambda i, j, r: (jnp.minimum(i, bx - 1), jnp.where(i < bx, r, nr - 1), 0)),
                   pl.BlockSpec((1, tm, k), lambda i, j, r: (jnp.maximum(i - bx, 0), jnp.where(i < bx, 0, r), 0))]
    y_map = (lambda i, j, r: (i, r, j)) if by > 1 else (lambda i, j, r: (0, r, j))
    (out,), carried = rider_call(
        body, name, (b, nt, nr),
        in_specs=x_specs + [pl.BlockSpec((1, tm, tn), y_map)],
        out_specs=[pl.BlockSpec((1, k, tn), lambda i, j, r: (i, 0, j))],
        out_shape=[jax.ShapeDtypeStruct((b, k, n), BF16)],
        scratch_shapes=[pltpu.VMEM((k, tn), F32)],
        args=[x] + ([x2] if x2 is not None else []) + [y], rider=rider)
    return (out, carried) if rider is not None else out


AUG = HEAD_DIM


class _Cols:
    def __init__(self, d):
        self.d = d
        self.ga, self.gb = 0, d
        self.qa, self.ka, self.va = 2 * d, 2 * d + 1024, 2 * d + 2048
        self.qb = 2 * d + 3072
        self.kb, self.vb, self.fa = self.qb + 512, self.qb + 640, self.qb + 768
        self.np = self.qb + 1024


def mixer_proj(h, g, wp, name):
    m, d = h.shape
    npad = wp.shape[0]
    tm = _row_tile(m)

    def body(h_ref, g_ref, w_ref, n_ref, p_ref):
        xh, _ = _rms(h_ref[...])
        n = (xh * g_ref[...]).astype(BF16)
        n_ref[...] = n
        p_ref[...] = _dot_nt(n, w_ref[...])

    return pl.pallas_call(
        body, name=name, grid=(m // tm,),
        in_specs=[pl.BlockSpec((tm, d), lambda r: (r, 0)), pl.BlockSpec((1, d), lambda r: (0, 0)),
                  pl.BlockSpec((npad, d), lambda r: (0, 0))],
        out_specs=[pl.BlockSpec((tm, d), lambda r: (r, 0)), pl.BlockSpec((tm, npad), lambda r: (r, 0))],
        out_shape=[jax.ShapeDtypeStruct((m, d), BF16), jax.ShapeDtypeStruct((m, npad), F32)],
        compiler_params=_params(1),
    )(h, g, wp)


def _head_norm(x, gain, ones):
    outs = []
    for b in range(x.shape[1] // LANES):
        xb = x[:, b * LANES:(b + 1) * LANES]
        r = lax.rsqrt(_group_mean(xb * xb, ones) + EPS)
        outs.append(xb * r * gain[:, b * LANES:(b + 1) * LANES])
    return outs


def _head_norm_bwd(x, gain, dn, ones):
    dxs, dgs = [], []
    for b in range(x.shape[1] // LANES):
        sl = slice(b * LANES, (b + 1) * LANES)
        xb, dnb = x[:, sl], dn[:, sl]
        r = lax.rsqrt(_group_mean(xb * xb, ones) + EPS)
        xh = xb * r
        dxh = dnb * gain[:, sl]
        dxs.append(r * (dxh - xh * _group_mean(dxh * xh, ones)))
        dgs.append(jnp.sum(dnb * xh, axis=0, keepdims=True))
    return dxs, dgs


def _lane_col(v, lane_iota, idx):
    return jnp.sum(jnp.where(lane_iota == idx, v, 0.0), axis=1, keepdims=True)


def _aug(base, lane, vals):
    for i, v in enumerate(vals):
        base = jnp.where(lane == AUG + i, v, base)
    return base


def qk_post(proj, gains, fbias, cols, name):
    m = proj.shape[0]
    tm = _row_tile(m)
    gqa, gka, gqb, gkb = gains

    def body(qa_ref, ka_ref, va_ref, qb_ref, kb_ref, vb_ref, fa_ref, gqa_ref, gka_ref, gqb_ref, gkb_ref, fb_ref,
             qf_o, kf_o, vf_o, kt_o, vt_o, qb_o, kb_o, vb_o, carry_ref):
        r0 = pl.program_id(0)

        @pl.when(r0 == 0)
        def _():
            carry_ref[...] = jnp.zeros_like(carry_ref)

        z = fa_ref[...] + fb_ref[...]
        logf = jnp.minimum(z, 0.0) - jnp.log(1.0 + jnp.exp(-jnp.abs(z)))
        rr = lax.broadcasted_iota(jnp.int32, (tm, tm), 0)
        cc = lax.broadcasted_iota(jnp.int32, (tm, tm), 1)
        tril = jnp.where(cc <= rr, 1.0, 0.0).astype(BF16)
        p0, p1, p2 = _split3(logf)
        c = _dot(tril, p0) + _dot(tril, p1) + _dot(tril, p2) + carry_ref[...]
        carry_ref[...] += jnp.sum(logf, axis=0, keepdims=True)

        lane = lax.broadcasted_iota(jnp.int32, (tm, LANES), 1)
        is_pad = (r0 * tm + lax.broadcasted_iota(jnp.int32, (tm, 1), 0)) < PAD_FRONT
        ones = jnp.ones((LANES, LANES), BF16)
        for hd in range(8):
            sl = slice(hd * LANES, (hd + 1) * LANES)
            ch = _lane_col(c, lane, hd)
            ct = [p.astype(F32) for p in _split3(ch)]
            cs = [p.astype(F32) for p in _split3(-jnp.where(is_pad, -NEG, ch))]
            xq = qa_ref[:, sl]
            qn = xq * lax.rsqrt(_group_mean(xq * xq, ones) + EPS) * gqa_ref[...]
            qf_o[:, sl] = _aug(qn, lane, ct + [1.0, 1.0, 1.0]).astype(BF16)
            xk = ka_ref[:, sl]
            kn = xk * lax.rsqrt(_group_mean(xk * xk, ones) + EPS) * gka_ref[...]
            kf = _aug(kn, lane, [1.0, 1.0, 1.0] + cs)
            vf = _aug(va_ref[:, sl], lane, [1.0, 1.0, 1.0])
            kf_o[:, sl] = kf.astype(BF16)
            vf_o[:, sl] = vf.astype(BF16)
            kt_o[sl, :] = kf.T.astype(BF16)
            vt_o[sl, :] = vf.T.astype(BF16)

        gones = _group_ones()
        for src, gn, dst in ((qb_ref, gqb_ref, qb_o), (kb_ref, gkb_ref, kb_o)):
            for b, blk in enumerate(_head_norm(src[...], gn[...], gones)):
                dst[:, b * LANES:(b + 1) * LANES] = blk.astype(BF16)
        vb_o[...] = vb_ref[...].astype(BF16)

    w1024 = lambda off: pl.BlockSpec((tm, 1024), lambda r, o=off // 1024: (r, o))
    w512 = lambda off: pl.BlockSpec((tm, 512), lambda r, o=off // 512: (r, o))
    w128 = lambda off: pl.BlockSpec((tm, LANES), lambda r, o=off // LANES: (r, o))
    vec = lambda w: pl.BlockSpec((1, w), lambda r: (0, 0))
    row = lambda w: pl.BlockSpec((tm, w), lambda r: (r, 0))
    return pl.pallas_call(
        body, name=name, grid=(m // tm,),
        in_specs=[w1024(cols.qa), w1024(cols.ka), w1024(cols.va), w512(cols.qb), w128(cols.kb), w128(cols.vb),
                  w128(cols.fa), vec(LANES), vec(LANES), vec(512), vec(LANES), vec(LANES)],
        out_specs=[row(1024), row(1024), row(1024)] + [pl.BlockSpec((1024, tm), lambda r: (0, r))] * 2
                  + [row(512), row(LANES), row(LANES)],
        out_shape=[jax.ShapeDtypeStruct((m, 1024), BF16)] * 3 + [jax.ShapeDtypeStruct((1024, m), BF16)] * 2
                  + [jax.ShapeDtypeStruct((m, 512), BF16)] + [jax.ShapeDtypeStruct((m, LANES), BF16)] * 2,
        scratch_shapes=[pltpu.VMEM((1, LANES), F32)],
        compiler_params=_params(1),
    )(proj, proj, proj, proj, proj, proj, proj, gqa, gka, gqb, gkb, fbias)


def qk_post_bwd(proj, gains, fbias, dqf, dkf, dvf, dqb, dkb, dvb, dc, dga, dgb, cols, name):
    m = proj.shape[0]
    d = cols.d
    tm = _row_tile(m)
    nt = m // tm
    gqa, gka, gqb, gkb = gains

    def body(qa_ref, ka_ref, qb_ref, kb_ref, fa_ref, gqa_ref, gka_ref, gqb_ref, gkb_ref, fb_ref,
             dqf_ref, dkf_ref, dvf_ref, dqb_ref, dkb_ref, dvb_ref, dc_ref, dga_ref, dgb_ref,
             dp_o, ggqa_o, ggka_o, ggqb_o, ggkb_o, gfb_o, carry_ref):
        @pl.when(pl.program_id(0) == 0)
        def _():
            carry_ref[...] = jnp.zeros_like(carry_ref)
            for o in (ggqa_o, ggka_o, ggqb_o, ggkb_o, gfb_o):
                o[...] = jnp.zeros_like(o)

        dp_o[:, cols.ga:cols.ga + d] = dga_ref[...].astype(BF16)
        dp_o[:, cols.gb:cols.gb + d] = dgb_ref[...].astype(BF16)
        dp_o[:, cols.fa + LANES:cols.np] = jnp.zeros((tm, cols.np - cols.fa - LANES), BF16)
        lane = lax.broadcasted_iota(jnp.int32, (tm, LANES), 1)
        data = lane < HEAD_DIM
        ones = jnp.ones((LANES, LANES), BF16)
        for hd in range(8):
            sl = slice(hd * LANES, (hd + 1) * LANES)
            for src, gn, dn_ref, off, gout in ((qa_ref, gqa_ref, dqf_ref, cols.qa, ggqa_o),
                                               (ka_ref, gka_ref, dkf_ref, cols.ka, ggka_o)):
                x = src[:, sl]
                dn = jnp.where(data, dn_ref[:, sl], 0.0)
                r = lax.rsqrt(_group_mean(x * x, ones) + EPS)
                xh = x * r
                dxh = dn * gn[...]
                dp_o[:, off + hd * LANES:off + (hd + 1) * LANES] = (
                    r * (dxh - xh * _group_mean(dxh * xh, ones))).astype(BF16)
                gout[...] += jnp.sum(dn * xh, axis=0, keepdims=True)
            dp_o[:, cols.va + hd * LANES:cols.va + (hd + 1) * LANES] = jnp.where(data, dvf_ref[:, sl], 0.0).astype(BF16)
        dp_o[:, cols.vb:cols.vb + LANES] = dvb_ref[...].astype(BF16)
        gones = _group_ones()
        for src, gn, dn, off, gout in ((qb_ref, gqb_ref, dqb_ref, cols.qb, ggqb_o),
                                       (kb_ref, gkb_ref, dkb_ref, cols.kb, ggkb_o)):
            dxs, dgs = _head_norm_bwd(src[...], gn[...], dn[...], gones)
            for b, (dx, dg) in enumerate(zip(dxs, dgs)):
                dp_o[:, off + b * LANES:off + (b + 1) * LANES] = dx.astype(BF16)
                gout[:, b * LANES:(b + 1) * LANES] += dg
        dcv = dc_ref[...]
        rr = lax.broadcasted_iota(jnp.int32, (tm, tm), 0)
        cc = lax.broadcasted_iota(jnp.int32, (tm, tm), 1)
        triu = jnp.where(cc >= rr, 1.0, 0.0).astype(BF16)
        p0, p1, p2 = _split3(dcv)
        dlogf = _dot(triu, p0) + _dot(triu, p1) + _dot(triu, p2) + carry_ref[...]
        carry_ref[...] += jnp.sum(dcv, axis=0, keepdims=True)
        z = fa_ref[...] + fb_ref[...]
        row = (nt - 1 - pl.program_id(0)) * tm + lax.broadcasted_iota(jnp.int32, (tm, LANES), 0)
        dfa = jnp.where(row >= PAD_FRONT, dlogf * jax.nn.sigmoid(-z), 0.0)
        dp_o[:, cols.fa:cols.fa + LANES] = dfa.astype(BF16)
        gfb_o[...] += jnp.sum(dfa, axis=0, keepdims=True)

    rev = lambda r: nt - 1 - r
    w1024 = lambda off: pl.BlockSpec((tm, 1024), lambda r, o=off // 1024: (rev(r), o))
    w512 = lambda off: pl.BlockSpec((tm, 512), lambda r, o=off // 512: (rev(r), o))
    w128 = lambda off: pl.BlockSpec((tm, LANES), lambda r, o=off // LANES: (rev(r), o))
    vec = lambda w: pl.BlockSpec((1, w), lambda r: (0, 0))
    row = lambda w: pl.BlockSpec((tm, w), lambda r: (rev(r), 0))
    return pl.pallas_call(
        body, name=name, grid=(nt,),
        in_specs=[w1024(cols.qa), w1024(cols.ka), w512(cols.qb), w128(cols.kb), w128(cols.fa),
                  vec(LANES), vec(LANES), vec(512), vec(LANES), vec(LANES),
                  row(1024), row(1024), row(1024), row(512), row(LANES), row(LANES), row(LANES), row(d), row(d)],
        out_specs=[row(cols.np), vec(LANES), vec(LANES), vec(512), vec(LANES), vec(LANES)],
        out_shape=[jax.ShapeDtypeStruct((m, cols.np), BF16), jax.ShapeDtypeStruct((1, LANES), F32),
                   jax.ShapeDtypeStruct((1, LANES), F32), jax.ShapeDtypeStruct((1, 512), F32),
                   jax.ShapeDtypeStruct((1, LANES), F32), jax.ShapeDtypeStruct((1, LANES), F32)],
        scratch_shapes=[pltpu.VMEM((1, LANES), F32)],
        compiler_params=_params(1),
    )(proj, proj, proj, proj, proj, gqa, gka, gqb, gkb, fbias, dqf, dkf, dvf, dqb, dkb, dvb, dc, dga, dgb)


def dproj_bwd(dh, h, g, dproj, wp, name):
    m, d = h.shape
    npad = wp.shape[0]
    tm = _row_tile(m)

    def body(dh_ref, h_ref, g_ref, dp_ref, w_ref, dhin_ref, dgn_ref):
        @pl.when(pl.program_id(0) == 0)
        def _():
            dgn_ref[...] = jnp.zeros_like(dgn_ref)

        dn = _dot(dp_ref[...], w_ref[...])
        dx, dgain = _rms_bwd(h_ref[...], g_ref[...], dn)
        dgn_ref[...] += dgain
        dhin_ref[...] = dh_ref[...] + dx

    row = lambda w: pl.BlockSpec((tm, w), lambda r: (r, 0))
    return pl.pallas_call(
        body, name=name, grid=(m // tm,),
        in_specs=[row(d), row(d), pl.BlockSpec((1, d), lambda r: (0, 0)), row(npad),
                  pl.BlockSpec((npad, d), lambda r: (0, 0))],
        out_specs=[row(d), pl.BlockSpec((1, d), lambda r: (0, 0))],
        out_shape=[jax.ShapeDtypeStruct((m, d), F32), jax.ShapeDtypeStruct((1, d), F32)],
        compiler_params=_params(1),
    )(dh, h, g, dproj, wp)


def _causal_t(t):
    return lax.broadcasted_iota(jnp.int32, (t, t), 0) <= lax.broadcasted_iota(jnp.int32, (t, t), 1)


HEADS_PER_STEP = 4


def fox_fwd(qf, kf, vt, name, rider=None):
    m = qf.shape[0]
    t = _row_tile(m)
    nq = m // t
    hp = HEADS_PER_STEP
    w = hp * LANES

    def body(q_ref, k_ref, vt_ref, o_ref, lse_ref, acc_ref, m_ref, p_ref, a_ref):
        qi = pl.program_id(1)
        acc_ref[...] = jnp.zeros_like(acc_ref)
        m_ref[...] = jnp.full_like(m_ref, NEG)

        def scores(ki, slot, mask):
            off = pl.multiple_of(ki * t, t)
            for e in range(hp):
                sl = slice(e * LANES, (e + 1) * LANES)
                s = _dot_nt(k_ref[pl.ds(off, t), sl], q_ref[:, sl])
                if mask is not None:
                    s = jnp.where(mask, s, NEG)
                m_old = m_ref[e]
                m_new = jnp.maximum(m_old, jnp.max(s, axis=0, keepdims=True))
                p_ref[slot, e] = jnp.exp(s - m_new).astype(BF16)
                a_ref[slot, e] = jnp.exp(m_old - m_new)
                m_ref[e] = m_new

        def values(ki, slot):
            off = pl.multiple_of(ki * t, t)
            for e in range(hp):
                sl = slice(e * LANES, (e + 1) * LANES)
                acc_ref[e] = acc_ref[e] * a_ref[slot, e] + _dot(vt_ref[sl, pl.ds(off, t)], p_ref[slot, e])

        causal = _causal_t(t)
        scores(0, 0, causal | (jnp.full((t, t), qi, jnp.int32) > 0))

        def step(ki, carry):
            values(ki - 1, (ki - 1) % 2)
            scores(ki, ki % 2, None)
            return carry

        lax.fori_loop(1, qi, step, 0)

        @pl.when(qi >= 1)
        def _():
            values(qi - 1, (qi - 1) % 2)
            scores(qi, qi % 2, causal)

        values(qi, qi % 2)
        row = lax.broadcasted_iota(jnp.int32, (LANES, t), 0)
        for e in range(hp):
            l = jnp.max(acc_ref[e, AUG:AUG + 8, :], axis=0, keepdims=True)
            o_ref[:, e * LANES:(e + 1) * LANES] = jnp.where(row < HEAD_DIM, acc_ref[e] * (1.0 / l), 0.0).T
            lse_ref[e] = m_ref[e] + jnp.log(l)

    return rider_call(
        body, name, (8 // hp, nq),
        in_specs=[pl.BlockSpec((t, w), lambda hd, i: (i, hd)),
                  pl.BlockSpec((m, w), lambda hd, i: (0, hd)),
                  pl.BlockSpec((w, m), lambda hd, i: (hd, 0))],
        out_specs=[pl.BlockSpec((t, w), lambda hd, i: (i, hd)),
                   pl.BlockSpec((hp, 1, t), lambda hd, i: (hd, 0, i))],
        out_shape=[jax.ShapeDtypeStruct((m, 8 * LANES), F32), jax.ShapeDtypeStruct((8, 1, m), F32)],
        scratch_shapes=[pltpu.VMEM((hp, LANES, t), F32), pltpu.VMEM((hp, 1, t), F32),
                        pltpu.VMEM((2, hp, t, t), BF16), pltpu.VMEM((2, hp, 1, t), F32)],
        args=(qf, kf, vt), rider=rider)


def fox_bwd(qf, kf, vf, kt, dof, lse, delta, name, rider=None):
    m = qf.shape[0]
    t = _row_tile(m)
    nq = m // t
    hp = HEADS_PER_STEP
    w = hp * LANES

    def body(k_ref, v_ref, kt_ref, q_ref, do_ref, lse_ref, delta_ref, dk_ref, dv_ref, dq_ref, dck_ref, dcq_ref,
             dka_ref, dva_ref, dqt_ref):
        ki = pl.program_id(1)

        @pl.when(ki == 0)
        def _():
            dqt_ref[...] = jnp.zeros_like(dqt_ref)

        dka_ref[...] = jnp.zeros_like(dka_ref)
        dva_ref[...] = jnp.zeros_like(dva_ref)

        def tile(qi, diagonal):
            off = pl.multiple_of(qi * t, t)
            for e in range(hp):
                sl = slice(e * LANES, (e + 1) * LANES)
                q = q_ref[pl.ds(off, t), sl]
                do = do_ref[pl.ds(off, t), sl]
                s = _dot_nt(k_ref[:, sl], q)
                if diagonal:
                    s = jnp.where(_causal_t(t), s, NEG)
                p = jnp.exp(s - lse_ref[e, :, pl.ds(off, t)])
                ds = (p * (_dot_nt(v_ref[:, sl], do) - delta_ref[e, :, pl.ds(off, t)])).astype(BF16)
                dva_ref[:, sl] += _dot(p.astype(BF16), do)
                dka_ref[:, sl] += _dot(ds, q)
                dqt_ref[sl, pl.ds(off, t)] += _dot(kt_ref[sl, :], ds)

        def step(qi, carry):
            tile(qi, False)
            return carry

        tile(ki, True)
        lax.fori_loop(ki + 1, nq, step, 0)
        dk_ref[...] = dka_ref[...]
        dv_ref[...] = dva_ref[...]
        row8 = lax.broadcasted_iota(jnp.int32, (8, 1), 0)
        for e in range(hp):
            slab = dka_ref[:, e * LANES:(e + 1) * LANES].T[AUG:AUG + 8, :]
            dck_ref[e] = -jnp.sum(jnp.where(row8 == 3, slab, 0.0), axis=0, keepdims=True)

        @pl.when(ki == nq - 1)
        def _():
            for e in range(hp):
                sl = slice(e * LANES, (e + 1) * LANES)
                slab = dqt_ref[e * LANES + AUG:e * LANES + AUG + 8, :]
                dcq_ref[e] = jnp.sum(jnp.where(row8 == 0, slab, 0.0), axis=0, keepdims=True)
                for j in range(nq):
                    dq_ref[j * t:(j + 1) * t, sl] = dqt_ref[sl, j * t:(j + 1) * t].T

    tile_spec = pl.BlockSpec((t, w), lambda hd, i: (i, hd))
    full = pl.BlockSpec((m, w), lambda hd, i: (0, hd))
    stat = pl.BlockSpec((hp, 1, m), lambda hd, i: (hd, 0, 0))
    (dkf, dvf, dqf, dck, dcq), carried = rider_call(
        body, name, (8 // hp, nq),
        in_specs=[tile_spec, tile_spec, pl.BlockSpec((w, t), lambda hd, i: (hd, i)), full, full, stat, stat],
        out_specs=[tile_spec, tile_spec, full, pl.BlockSpec((hp, 1, t), lambda hd, i: (hd, 0, i)), stat],
        out_shape=[jax.ShapeDtypeStruct((m, 8 * LANES), F32), jax.ShapeDtypeStruct((m, 8 * LANES), F32),
                   jax.ShapeDtypeStruct((m, 8 * LANES), F32), jax.ShapeDtypeStruct((8, 1, m), F32),
                   jax.ShapeDtypeStruct((8, 1, m), F32)],
        scratch_shapes=[pltpu.VMEM((t, w), F32), pltpu.VMEM((t, w), F32), pltpu.VMEM((w, m), F32)],
        args=(kf, vf, kt, qf, dof, lse, delta), rider=rider)
    return (dkf, dvf, dqf, dcq + dck), carried


def _bucket_ids():
    def bucket(dist):
        n = np.maximum(dist, 0)
        max_exact = N_BUCKETS // 2
        nf = np.maximum(n, 1).astype(np.float32)
        large = max_exact + (np.log(nf / max_exact) / math.log(MAX_DISTANCE / max_exact)
                             * (N_BUCKETS - max_exact)).astype(np.int32)
        return np.where(n < max_exact, n, np.minimum(large, N_BUCKETS - 1))

    tl = np.arange(LANES)[:, None]
    sl = np.arange(LANES)[None, :]
    prev = bucket(LANES + tl - sl)
    cur = bucket(tl - sl)
    meta = np.full((LANES, LANES), N_BUCKETS - 1)
    return np.concatenate([prev, cur, meta], axis=1).astype(np.int32)


def bias_build(table, name):
    ids = jnp.asarray(_bucket_ids())

    def body(t_ref, id_ref, o_ref):
        idv = id_ref[...]
        for h in range(8):
            acc = jnp.zeros((LANES, 3 * LANES), F32)
            for b in range(N_BUCKETS):
                acc = jnp.where(idv == b, t_ref[b, h], acc)
            o_ref[h] = acc

    return pl.pallas_call(
        body, name=name,
        in_specs=[pl.BlockSpec(memory_space=pltpu.SMEM), pl.BlockSpec(memory_space=pltpu.VMEM)],
        out_specs=pl.BlockSpec(memory_space=pltpu.VMEM),
        out_shape=jax.ShapeDtypeStruct((8, LANES, 3 * LANES), F32),
    )(table, ids)


def bias_reduce(dbias, name):
    ids = jnp.asarray(_bucket_ids())

    def body(d_ref, id_ref, o_ref):
        idv = id_ref[...]
        rr = lax.broadcasted_iota(jnp.int32, (N_BUCKETS, LANES), 0)
        cc = lax.broadcasted_iota(jnp.int32, (N_BUCKETS, LANES), 1)
        acc = jnp.zeros((N_BUCKETS, LANES), F32)
        for h in range(8):
            dv = d_ref[h]
            for b in range(N_BUCKETS):
                val = jnp.sum(jnp.where(idv == b, dv, 0.0), keepdims=True)
                acc = jnp.where((rr == b) & (cc == h), val, acc)
        o_ref[...] = acc

    return pl.pallas_call(
        body, name=name,
        in_specs=[pl.BlockSpec(memory_space=pltpu.VMEM), pl.BlockSpec(memory_space=pltpu.VMEM)],
        out_specs=pl.BlockSpec(memory_space=pltpu.VMEM),
        out_shape=jax.ShapeDtypeStruct((N_BUCKETS, LANES), F32),
    )(dbias, ids)


def _swa_penalty(n):
    shape = (LANES, 3 * LANES)
    tl = lax.broadcasted_iota(jnp.int32, shape, 0)
    col = lax.broadcasted_iota(jnp.int32, shape, 1)
    sl = col & (LANES - 1)
    nv = jnp.full(shape, n, jnp.int32)
    is_meta = sl >= PAD_FRONT
    prev = (col < LANES) & (sl > tl) & (nv >= 1) & ((nv >= 2) | is_meta)
    cur = (col >= LANES) & (col < 2 * LANES) & (sl <= tl) & ((nv >= 1) | is_meta)
    meta = (col >= 2 * LANES) & is_meta & ((nv >= 2) | ((nv == 1) & (sl <= tl)))
    return jnp.where(prev | cur | meta, 0.0, NEG)


def _swa_keys(ref, n):
    off_prev = pl.multiple_of(jnp.maximum(n - 1, 0) * LANES, LANES)
    off_cur = pl.multiple_of(n * LANES, LANES)
    return jnp.concatenate([ref[pl.ds(off_prev, LANES), :], ref[pl.ds(off_cur, LANES), :], ref[0:LANES, :]], axis=0)


def swa_fwd(q, k, v, bias, sinks, name, rider=None):
    m = q.shape[0]

    def body(q_ref, k_ref, v_ref, bias_ref, sink_ref, o_ref, lse_ref):
        n = pl.program_id(0)
        lane1 = lax.broadcasted_iota(jnp.int32, (1, LANES), 1)
        lane_t = lax.broadcasted_iota(jnp.int32, (LANES, LANES), 1)
        in_head = [lane1 < HEAD_DIM, lane1 >= HEAD_DIM]
        kall = _swa_keys(k_ref, n)
        vall = _swa_keys(v_ref, n)
        vs = [jnp.where(in_head[g], vall, jnp.zeros_like(vall)) for g in (0, 1)]
        penalty = _swa_penalty(n)
        lse = jnp.zeros((LANES, LANES), F32)
        for b in range(4):
            qb = q_ref[:, b * LANES:(b + 1) * LANES]
            ob = jnp.zeros((LANES, LANES), F32)
            for g in (0, 1):
                h = 4 * g + b
                qe = jnp.where(in_head[g], qb, jnp.zeros_like(qb))
                s = _dot_nt(qe, kall) + bias_ref[h] + penalty
                sink = sink_ref[h]
                mx = jnp.maximum(jnp.max(s, axis=1, keepdims=True), sink)
                p = jnp.exp(s - mx)
                den = jnp.sum(p, axis=1, keepdims=True) + jnp.exp(sink - mx)
                ob = ob + _dot((p / den).astype(BF16), vs[g])
                lse = jnp.where(lane_t == h, mx + jnp.log(den), lse)
            o_ref[:, b * LANES:(b + 1) * LANES] = ob
        lse_ref[...] = lse

    return rider_call(
        body, name, (m // LANES,),
        in_specs=[pl.BlockSpec((LANES, 512), lambda n: (n, 0)),
                  pl.BlockSpec((m, LANES), lambda n: (0, 0)), pl.BlockSpec((m, LANES), lambda n: (0, 0)),
                  pl.BlockSpec((8, LANES, 3 * LANES), lambda n: (0, 0, 0)),
                  pl.BlockSpec(memory_space=pltpu.SMEM)],
        out_specs=[pl.BlockSpec((LANES, 512), lambda n: (n, 0)), pl.BlockSpec((LANES, LANES), lambda n: (n, 0))],
        out_shape=[jax.ShapeDtypeStruct((m, 512), F32), jax.ShapeDtypeStruct((m, LANES), F32)],
        scratch_shapes=[], args=(q, k, v, bias, sinks), rider=rider)


def swa_bwd(q, k, v, bias, sinks, o, lse, do, name):
    m = q.shape[0]

    def body(q_ref, do_ref, o_ref, lse_ref, k_ref, v_ref, bias_ref, sink_ref,
             dq_ref, dk_ref, dv_ref, dbias_ref, dsink_ref):
        n = pl.program_id(0)

        @pl.when(n == 0)
        def _():
            for r in (dk_ref, dv_ref, dbias_ref, dsink_ref):
                r[...] = jnp.zeros_like(r)

        lane1 = lax.broadcasted_iota(jnp.int32, (1, LANES), 1)
        lane_t = lax.broadcasted_iota(jnp.int32, (LANES, LANES), 1)
        in_head = [lane1 < HEAD_DIM, lane1 >= HEAD_DIM]
        off_prev = pl.multiple_of(jnp.maximum(n - 1, 0) * LANES, LANES)
        off_cur = pl.multiple_of(n * LANES, LANES)
        kall = _swa_keys(k_ref, n)
        vall = _swa_keys(v_ref, n)
        ks = [jnp.where(in_head[g], kall, jnp.zeros_like(kall)) for g in (0, 1)]
        penalty = _swa_penalty(n)
        lsev = lse_ref[...]
        dsink = dsink_ref[...]
        dkall = jnp.zeros((3 * LANES, LANES), F32)
        dvall = jnp.zeros((3 * LANES, LANES), F32)
        for b in range(4):
            sl = slice(b * LANES, (b + 1) * LANES)
            qb = q_ref[:, sl]
            dob = do_ref[:, sl]
            prod = dob * o_ref[:, sl]
            dqb = jnp.zeros((LANES, LANES), F32)
            for g in (0, 1):
                h = 4 * g + b
                qe = jnp.where(in_head[g], qb, jnp.zeros_like(qb))
                doe = jnp.where(in_head[g], dob, 0.0).astype(BF16)
                delta = jnp.sum(jnp.where(in_head[g], prod, 0.0), axis=1, keepdims=True)
                lse_h = _lane_col(lsev, lane_t, h)
                s = _dot_nt(qe, kall) + bias_ref[h] + penalty
                p = jnp.exp(s - lse_h)
                ds = p * (_dot_nt(doe, vall) - delta)
                dbias_ref[h] += ds
                sink_part = jnp.sum(-jnp.exp(sink_ref[h] - lse_h) * delta, keepdims=True)
                dsink = jnp.where(lane1 == h, dsink + sink_part, dsink)
                dsb = ds.astype(BF16)
                dqb = dqb + _dot(dsb, ks[g])
                dkall = dkall + _dot_tn(dsb, qe)
                dvall = dvall + _dot_tn(p.astype(BF16), doe)
            dq_ref[:, sl] = dqb
        dsink_ref[...] = dsink
        for ref, val in ((dk_ref, dkall), (dv_ref, dvall)):
            ref[pl.ds(off_prev, LANES), :] += val[0:LANES]
            ref[pl.ds(off_cur, LANES), :] += val[LANES:2 * LANES]
            ref[0:LANES, :] += val[2 * LANES:3 * LANES]

    blk = pl.BlockSpec((LANES, 512), lambda n: (n, 0))
    full = pl.BlockSpec((m, LANES), lambda n: (0, 0))
    return pl.pallas_call(
        body, name=name, grid=(m // LANES,),
        in_specs=[blk, blk, blk, pl.BlockSpec((LANES, LANES), lambda n: (n, 0)), full, full,
                  pl.BlockSpec((8, LANES, 3 * LANES), lambda n: (0, 0, 0)),
                  pl.BlockSpec(memory_space=pltpu.SMEM)],
        out_specs=[blk, full, full, pl.BlockSpec((8, LANES, 3 * LANES), lambda n: (0, 0, 0)),
                   pl.BlockSpec((1, LANES), lambda n: (0, 0))],
        out_shape=[jax.ShapeDtypeStruct((m, 512), F32), jax.ShapeDtypeStruct((m, LANES), F32),
                   jax.ShapeDtypeStruct((m, LANES), F32), jax.ShapeDtypeStruct((8, LANES, 3 * LANES), F32),
                   jax.ShapeDtypeStruct((1, LANES), F32)],
        compiler_params=_params(1),
    )(q, do, o, lse, k, v, bias, sinks)


def branch_out(h, o_fox, o_swa, proj, wbf, wbs, wo, cols, name):
    m, d = h.shape
    tm = _row_tile(m)

    def body(h_ref, of_ref, os_ref, ga_ref, gb_ref, wbf_ref, wbs_ref, wo_ref, hn_ref):
        tf = _dot(of_ref[...].astype(BF16), wbf_ref[...])
        ts = _dot(os_ref[...].astype(BF16), wbs_ref[...])
        y = jax.nn.sigmoid(ga_ref[...]) * tf + jax.nn.sigmoid(gb_ref[...]) * ts
        hn_ref[...] = h_ref[...] + _dot(y.astype(BF16), wo_ref[...])

    row = lambda w, o=0: pl.BlockSpec((tm, w), lambda r, o=o: (r, o))
    res = lambda a: pl.BlockSpec(a.shape, lambda r: (0, 0))
    return pl.pallas_call(
        body, name=name, grid=(m // tm,),
        in_specs=[row(d), row(1024), row(512), row(d, cols.ga // d), row(d, cols.gb // d), res(wbf), res(wbs), res(wo)],
        out_specs=row(d),
        out_shape=jax.ShapeDtypeStruct((m, d), F32),
        compiler_params=_params(1),
    )(h, o_fox, o_swa, proj, proj, wbf, wbs, wo)


def branch_out_bwd(dh, o_fox, o_swa, proj, wbf, wbs, wo, cols, name):
    m, d = dh.shape
    tm = _row_tile(m)

    def body(dh_ref, of_ref, os_ref, ga_ref, gb_ref, wbf_ref, wbs_ref, wo_ref,
             y_ref, dtf_ref, dts_ref, dga_ref, dgb_ref, dof_ref, dos_ref, delta_ref):
        dy = _dot_nt(dh_ref[...].astype(BF16), wo_ref[...])
        tf = _dot(of_ref[...].astype(BF16), wbf_ref[...])
        ts = _dot(os_ref[...].astype(BF16), wbs_ref[...])
        sa = jax.nn.sigmoid(ga_ref[...])
        sb = jax.nn.sigmoid(gb_ref[...])
        y_ref[...] = (sa * tf + sb * ts).astype(BF16)
        dtf = (dy * sa).astype(BF16)
        dts = (dy * sb).astype(BF16)
        dtf_ref[...] = dtf
        dts_ref[...] = dts
        dga_ref[...] = (dy * tf * sa * (1.0 - sa)).astype(BF16)
        dgb_ref[...] = (dy * ts * sb * (1.0 - sb)).astype(BF16)
        dof = _dot_nt(dtf, wbf_ref[...])
        dof_ref[...] = dof.astype(BF16)
        dos_ref[...] = _dot_nt(dts, wbs_ref[...])
        lane = lax.broadcasted_iota(jnp.int32, (tm, LANES), 1)
        delta = jnp.zeros((tm, LANES), F32)
        for hd in range(8):
            sl = slice(hd * LANES, (hd + 1) * LANES)
            delta = jnp.where(lane == hd, jnp.sum(dof[:, sl] * of_ref[:, sl], axis=1, keepdims=True), delta)
        delta_ref[...] = delta

    row = lambda w, o=0: pl.BlockSpec((tm, w), lambda r, o=o: (r, o))
    res = lambda a: pl.BlockSpec(a.shape, lambda r: (0, 0))
    return pl.pallas_call(
        body, name=name, grid=(m // tm,),
        in_specs=[row(d), row(1024), row(512), row(d, cols.ga // d), row(d, cols.gb // d), res(wbf), res(wbs), res(wo)],
        out_specs=[row(d)] * 5 + [row(1024), row(512), row(LANES)],
        out_shape=[jax.ShapeDtypeStruct((m, d), BF16)] * 5 + [jax.ShapeDtypeStruct((m, 1024), BF16),
                   jax.ShapeDtypeStruct((m, 512), F32), jax.ShapeDtypeStruct((m, LANES), F32)],
        compiler_params=_params(1),
    )(dh, o_fox, o_swa, proj, proj, wbf, wbs, wo)


def loss_head(h, target, name):
    m, d = h.shape

    def body(h_ref, t_ref, dh_ref, loss_ref):
        n = pl.program_id(0)

        @pl.when(n == 0)
        def _():
            loss_ref[...] = jnp.zeros_like(loss_ref)
            dh_ref[...] = jnp.zeros_like(dh_ref)

        @pl.when(n > 0)
        def _():
            err = h_ref[...] - t_ref[...]
            dh_ref[...] = err * (1.0 / d)
            loss_ref[...] += jnp.sum(err * err, keepdims=True) * (0.5 / d)

    return pl.pallas_call(
        body, name=name, grid=(m // LANES,),
        in_specs=[pl.BlockSpec((LANES, d), lambda n: (n, 0)),
                  pl.BlockSpec((LANES, d), lambda n: (jnp.maximum(n - 1, 0), 0))],
        out_specs=[pl.BlockSpec((LANES, d), lambda n: (n, 0)), pl.BlockSpec((8, LANES), lambda n: (0, 0))],
        out_shape=[jax.ShapeDtypeStruct((m, d), F32), jax.ShapeDtypeStruct((8, LANES), F32)],
        compiler_params=_params(1),
    )(h, target)


def _adamw_math(w, g, m, v):
    m = ADAM_B1 * m + (1.0 - ADAM_B1) * g
    v = ADAM_B2 * v + (1.0 - ADAM_B2) * (g * g)
    m_hat = m / (1.0 - ADAM_B1 ** ADAM_STEP)
    v_hat = v / (1.0 - ADAM_B2 ** ADAM_STEP)
    delta = -ADAM_LR * (m_hat / (jnp.sqrt(v_hat) + ADAM_EPS) + ADAM_WD * w)
    return delta, m, v


def adamw_sum(parts, w, m, v, name, after=None):
    n_layers, a, b = w.shape
    ta = next(t for t in (256, 176, 128, a) if a % t == 0)
    nr = a // ta

    def body(*refs):
        p_refs = refs[:n_layers]
        w_ref, m_ref, v_ref = refs[n_layers:n_layers + 3]
        g_o, d_o, m_o, v_o = refs[-4:]
        for l in range(n_layers):
            @pl.when(pl.program_id(0) == l)
            def _(l=l):
                g = p_refs[l][0].astype(F32)
                for j in range(1, N_DEV):
                    g = g + p_refs[l][j].astype(F32)
                g_o[0] = g
                d_o[0], m_o[0], v_o[0] = _adamw_math(w_ref[0], g, m_ref[0], v_ref[0])

    def part_spec(l):
        return pl.BlockSpec((N_DEV, ta, b), lambda i, r, l=l: (0, jnp.where(i == l, r, jnp.where(i < l, 0, nr - 1)), 0))

    row = pl.BlockSpec((1, ta, b), lambda i, r: (i, r, 0))
    return pl.pallas_call(
        body, name=name, grid=(n_layers, nr),
        in_specs=[part_spec(l) for l in range(n_layers)] + [row, row, row]
                 + ([pl.BlockSpec(memory_space=pl.ANY)] if after is not None else []),
        out_specs=[row] * 4,
        out_shape=[jax.ShapeDtypeStruct(w.shape, F32)] * 4,
        compiler_params=_params(2),
    )(*parts, w, m, v, *([after] if after is not None else []))


def adamw_sum_cols(parts, w, m, v, name, after=None):
    n_layers = len(parts)
    a, b = parts[0].shape[1:]
    tc = 512 if b % 512 == 0 else b
    nc = b // tc

    def body(*refs):
        p_refs = refs[:n_layers]
        w_ref, m_ref, v_ref = refs[n_layers:n_layers + 3]
        g_o, d_o, m_o, v_o = refs[-4:]
        for l in range(n_layers):
            @pl.when(pl.program_id(0) == l)
            def _(l=l):
                g = p_refs[l][0].astype(F32)
                for j in range(1, N_DEV):
                    g = g + p_refs[l][j].astype(F32)
                g_o[...] = g
                d_o[...], m_o[...], v_o[...] = _adamw_math(w_ref[...], g, m_ref[...], v_ref[...])

    def part_spec(l):
        return pl.BlockSpec((N_DEV, a, tc), lambda i, c, l=l: (0, 0, jnp.where(i == l, c, jnp.where(i < l, 0, nc - 1))))

    col = pl.BlockSpec((a, tc), lambda i, c: (0, i * nc + c))
    return pl.pallas_call(
        body, name=name, grid=(n_layers, nc),
        in_specs=[part_spec(l) for l in range(n_layers)] + [col, col, col]
                 + ([pl.BlockSpec(memory_space=pl.ANY)] if after is not None else []),
        out_specs=[col] * 4,
        out_shape=[jax.ShapeDtypeStruct(w.shape, F32)] * 4,
        compiler_params=_params(2),
    )(*parts, w, m, v, *([after] if after is not None else []))


def adamw_small(g, w, m, v, name):
    def body(g_ref, w_ref, m_ref, v_ref, d_o, m_o, v_o):
        d_o[...], m_o[...], v_o[...] = _adamw_math(w_ref[...], g_ref[...], m_ref[...], v_ref[...])

    spec = pl.BlockSpec(memory_space=pltpu.VMEM)
    return pl.pallas_call(
        body, name=name, in_specs=[spec] * 4, out_specs=[spec] * 3,
        out_shape=[jax.ShapeDtypeStruct(w.shape, F32)] * 3,
    )(g, w, m, v)


BIG = ("ffn1_w_in", "ffn1_w_out", "w_in", "w_branch_fox", "w_branch_swa", "w_out", "ffn2_w_in", "ffn2_w_out")
SMALL = ("rel_bias_table", "ffn1_norm", "mix_norm", "forget_bias", "fox_q_norm", "fox_k_norm",
         "swa_q_norm", "swa_k_norm", "swa_sinks", "ffn2_norm")
WEIGHTS = ("meta_tokens", "rel_bias_table", "ffn1_norm", "ffn1_w_in", "ffn1_w_out", "mix_norm", "w_in",
           "forget_bias", "fox_q_norm", "fox_k_norm", "swa_q_norm", "swa_k_norm", "swa_sinks", "w_branch_fox",
           "w_branch_swa", "w_out", "ffn2_norm", "ffn2_w_in", "ffn2_w_out")


def _pack(arrs, width, row_multiple, dtype):
    lead = arrs[0].shape[:-1]
    flat = jnp.concatenate([a.astype(dtype) for a in arrs], axis=-1)
    n = flat.shape[-1]
    rows = -(-n // width)
    rows = -(-rows // row_multiple) * row_multiple
    flat = jnp.pad(flat, [(0, 0)] * len(lead) + [(0, rows * width - n)])
    return flat.reshape(lead + (rows, width))


def _unpack(flat, shapes):
    flat = flat.reshape(-1)
    out, off = [], 0
    for s in shapes:
        n = int(np.prod(s))
        out.append(flat[off:off + n].reshape(s))
        off += n
    return out


def _swa_head_order():
    return [4 * (j % 2) + j // 2 for j in range(8)]


def _permute_heads(a, axis, inverse=False):
    order = _swa_head_order()
    if inverse:
        order = [order.index(hd) for hd in range(8)]
    parts = [lax.slice_in_dim(a, hd * HEAD_DIM, (hd + 1) * HEAD_DIM, axis=axis) for hd in order]
    return jnp.concatenate(parts, axis=axis)


def _pad_heads(a):
    return jnp.pad(a.reshape(8, HEAD_DIM, -1), ((0, 0), (0, LANES - HEAD_DIM), (0, 0))).reshape(8 * LANES, -1)


def _unpad_heads(a):
    return a.reshape(8, LANES, -1)[:, :HEAD_DIM].reshape(8 * HEAD_DIM, -1)


def _swa_rows(a, inverse=False):
    shape = (4, 2) if inverse else (2, 4)
    return a.reshape(shape + (HEAD_DIM, -1)).transpose(1, 0, 2, 3).reshape(a.shape)


def _w_in_rows(d):
    return np.cumsum([0, 512, 512, 512, 8, 512, 128, 128, d, d])


def _reorder_w_in(wt, cols):
    o = _w_in_rows(cols.d)
    qa, ka, va, fa, qb, kb, vb, ga, gb = [wt[o[i]:o[i + 1]] for i in range(9)]
    zeros = jnp.zeros((cols.np - cols.fa - 8, wt.shape[1]), wt.dtype)
    return jnp.concatenate([ga, gb, _pad_heads(qa), _pad_heads(ka), _pad_heads(va), _swa_rows(qb), kb, vb, fa, zeros],
                           axis=0)


def _restore_w_in(wpt, cols, width):
    seg = lambda off, n: wpt[off:off + n]
    rows = jnp.concatenate([_unpad_heads(seg(cols.qa, 1024)), _unpad_heads(seg(cols.ka, 1024)),
                            _unpad_heads(seg(cols.va, 1024)), seg(cols.fa, 8), _swa_rows(seg(cols.qb, 512), True),
                            seg(cols.kb, 128), seg(cols.vb, 128), seg(cols.ga, cols.d), seg(cols.gb, cols.d)], axis=0)
    return rows.reshape(N_DEV, width, -1)


def _lane_pad(v):
    return jnp.pad(v, ((0, 0), (0, LANES - v.shape[1])))


def kernel(x, meta_tokens, rel_bias_table, ffn1_norm, ffn1_w_in, ffn1_w_out, mix_norm, w_in, forget_bias, fox_q_norm, fox_k_norm, swa_q_norm, swa_k_norm, swa_sinks, w_branch_fox, w_branch_swa, w_out, ffn2_norm, ffn2_w_in, ffn2_w_out, loss_target, m_meta_tokens, m_rel_bias_table, m_ffn1_norm, m_ffn1_w_in, m_ffn1_w_out, m_mix_norm, m_w_in, m_forget_bias, m_fox_q_norm, m_fox_k_norm, m_swa_q_norm, m_swa_k_norm, m_swa_sinks, m_w_branch_fox, m_w_branch_swa, m_w_out, m_ffn2_norm, m_ffn2_w_in, m_ffn2_w_out, v_meta_tokens, v_rel_bias_table, v_ffn1_norm, v_ffn1_w_in, v_ffn1_w_out, v_mix_norm, v_w_in, v_forget_bias, v_fox_q_norm, v_fox_k_norm, v_swa_q_norm, v_swa_k_norm, v_swa_sinks, v_w_branch_fox, v_w_branch_swa, v_w_out, v_ffn2_norm, v_ffn2_w_in, v_ffn2_w_out):
    args = dict(locals())
    wts = {n: args[n] for n in WEIGHTS}
    mom1 = {n: args["m_" + n] for n in WEIGHTS}
    mom2 = {n: args["v_" + n] for n in WEIGHTS}

    seq, d = x.shape[1], x.shape[2]
    m_rows = seq + LANES
    depth = ffn1_norm.shape[0]
    fb = ffn1_w_in.shape[2]
    fo = ffn1_w_out.shape[1]
    din_shard = w_in.shape[2]
    cols = _Cols(d)
    scale = HEAD_DIM ** -0.5
    dev = 4 * lax.axis_index("x") + 2 * lax.axis_index("y") + lax.axis_index("c")

    groups = {"ffn1": ("ffn1_w_in", "ffn1_w_out"), "mix": ("w_in", "w_branch_fox", "w_branch_swa", "w_out"),
              "ffn2": ("ffn2_w_in", "ffn2_w_out"), "ffn1_in": ("ffn1_w_in",), "ffn1_out": ("ffn1_w_out",),
              "ffn2_in": ("ffn2_w_in",), "ffn2_out": ("ffn2_w_out",)}
    flipped = ("ffn1_w_in", "ffn2_w_in")
    for n in flipped:
        wts[n], mom1[n], mom2[n] = (jnp.swapaxes(a, 1, 2) for a in (wts[n], mom1[n], mom2[n]))
    to_rows = lambda a: jnp.transpose(a, (2, 0, 1)).reshape(din_shard, depth * d)
    from_rows = lambda a: jnp.transpose(a.reshape(din_shard, depth, d), (1, 2, 0))
    wts["w_in"], mom1["w_in"], mom2["w_in"] = (to_rows(a) for a in (wts["w_in"], mom1["w_in"], mom2["w_in"]))
    shard = {n: wts[n].astype(BF16) for n in BIG}
    w_in_rows = shard.pop("w_in")
    shard["w_in"] = [w_in_rows[:, l * d:(l + 1) * d] for l in range(depth)]
    full, parts, gw = {}, {}, {}

    def keys_of(stages):
        return [(n, l) for g, l in stages if l < depth for n in groups[g]]

    def gather_rider(stages):
        return Rider([shard[n][l] for n, l in keys_of(stages)], True)

    def scatter_rider(stages):
        return Rider([gw[k] for k in keys_of(stages)], False)

    def ffn_weights(tag, l):
        return full[tag + "_w_in", l], full[tag + "_w_out", l].reshape(4, fb, d)

    def mixer_weights(l):
        wp = _reorder_w_in(full["w_in", l].reshape(N_DEV * din_shard, d), cols)
        wbf = jnp.concatenate([full["w_branch_fox", l][j] for j in range(N_DEV)], axis=1)
        wbf = jnp.pad(wbf.reshape(8, HEAD_DIM, d), ((0, 0), (0, LANES - HEAD_DIM), (0, 0))).reshape(8 * LANES, d)
        wbs = _permute_heads(jnp.concatenate([full["w_branch_swa", l][j] for j in range(N_DEV)], axis=1), 0)
        return wp, wbf, wbs, full["w_out", l].reshape(d, d)

    full.update(zip(keys_of([("ffn1", 0)]), exchange_hbm(gather_rider([("ffn1", 0)]).srcs, True, "gather_first")))
    meta_all = gather_small(meta_tokens.reshape(1, N_META, -1), "gather_meta")
    meta_full = meta_all.transpose(1, 0, 2).reshape(N_META, d)
    tile8 = lambda g, s=1.0: jnp.tile(g.reshape(1, HEAD_DIM) * s, (1, 8))
    tile2 = lambda g: jnp.tile(g.reshape(1, HEAD_DIM), (1, 2))
    data_lanes = lambda g, s=1.0: _lane_pad(g.reshape(1, HEAD_DIM) * s)
    bias = bias_build(rel_bias_table, "swa_bias")

    first = jnp.concatenate([jnp.zeros((PAD_FRONT, d), F32), meta_full], axis=0)
    h = jnp.concatenate([first, x[0]], axis=0)
    saved, lw = [], []
    for l in range(depth):
        s, w = {"h0": h}, {}
        w["ffn1_in"], w["ffn1_out"] = ffn_weights("ffn1", l)
        stages = [("mix", l)]
        (h, s["n1"], s["a1"], s["fg1"], s["fu1"]), got = ffn_fwd(h, ffn1_norm[l:l + 1], w["ffn1_in"], w["ffn1_out"],
                                                          f"ffn1_fwd_{l}", gather_rider(stages))
        full.update(zip(keys_of(stages), got))
        s["h1"] = h
        w["wp"], w["wbf"], w["wbs"], w["wo"] = mixer_weights(l)
        s["nm"], s["proj"] = mixer_proj(h, mix_norm[l:l + 1], w["wp"], f"mixer_proj_{l}")
        s["gains"] = (data_lanes(fox_q_norm[l], scale), data_lanes(fox_k_norm[l]), tile8(swa_q_norm[l], scale),
                      tile2(swa_k_norm[l]))
        s["fbias"] = _lane_pad(forget_bias[l:l + 1])
        qf, kf, vf, kt, vt, qb, kb, vb = qk_post(s["proj"], s["gains"], s["fbias"], cols, f"qk_post_{l}")
        s.update(qf=qf, kf=kf, vf=vf, kt=kt, qb=qb, kb=kb, vb=vb)
        stages = [("ffn2", l)]
        (s["o_fox"], s["lse_fox"]), got = fox_fwd(qf, kf, vt, f"fox_fwd_{l}", gather_rider(stages))
        full.update(zip(keys_of(stages), got))
        stages = [("ffn1_out", l + 1)]
        (s["o_swa"], s["lse_swa"]), got = swa_fwd(qb, kb, vb, bias, swa_sinks[l], f"swa_fwd_{l}", gather_rider(stages))
        full.update(zip(keys_of(stages), got))
        h = branch_out(h, s["o_fox"], s["o_swa"], s["proj"], w["wbf"], w["wbs"], w["wo"], cols, f"branch_out_{l}")
        s["h2"] = h
        w["ffn2_in"], w["ffn2_out"] = ffn_weights("ffn2", l)
        stages = [("ffn1_in", l + 1)]
        (h, s["n2"], s["a2"], s["fg2"], s["fu2"]), got = ffn_fwd(h, ffn2_norm[l:l + 1], w["ffn2_in"], w["ffn2_out"],
                                                          f"ffn2_fwd_{l}", gather_rider(stages))
        full.update(zip(keys_of(stages), got))
        saved.append(s)
        lw.append(w)

    dh, loss_part = loss_head(h, loss_target[0], "loss_head")

    gs = {n: [None] * depth for n in SMALL}
    dbias_total = None
    for l in reversed(range(depth)):
        w, s = lw[l], saved[l]

        def ffn_back(dh, tag, hin, norm, n_in, a, f_gate, f_up, stages):
            (dh_in, dg, du, dgn, dhs), got = ffn_bwd(dh, hin, norm, f_gate, f_up, w[tag + "_in"], w[tag + "_out"],
                                                     f"{tag}_bwd_{l}", scatter_rider(stages))
            parts.update(zip(keys_of(stages), got))
            gw[tag + "_w_out", l] = matmul_tn(a, dhs[None], f"{tag}_dwo_{l}").reshape(N_DEV, fo, d)
            stages = [(tag + "_out", l)]
            gw[tag + "_w_in", l], got = matmul_tn(dg, n_in[None], f"{tag}_dwi_{l}", x2=du,
                                                  rider=scatter_rider(stages))
            parts.update(zip(keys_of(stages), got))
            return dh_in, dgn

        dh, gs["ffn2_norm"][l] = ffn_back(dh, "ffn2", s["h2"], ffn2_norm[l:l + 1], s["n2"], s["a2"], s["fg2"],
                                          s["fu2"], [("ffn1_in", l + 1)])

        y, dtf, dts, dga, dgb, dof, dos, delta = branch_out_bwd(dh, s["o_fox"], s["o_swa"], s["proj"], w["wbf"],
                                                                w["wbs"], w["wo"], cols, f"branch_out_bwd_{l}")
        gw["w_out", l] = matmul_tn(y[None], dh[None], f"dw_out_{l}").reshape(N_DEV, d // N_DEV, d)
        to_shards = lambda a: a.reshape(512, N_DEV, d // N_DEV).transpose(1, 0, 2)
        gw["w_branch_fox", l] = to_shards(matmul_tn(s["o_fox"][None], dtf[None], f"dw_branch_fox_{l}")[0]
                                          .reshape(8, LANES, d)[:, :HEAD_DIM].reshape(512, d))
        gw["w_branch_swa", l] = to_shards(_permute_heads(
            matmul_tn(s["o_swa"][None], dts[None], f"dw_branch_swa_{l}")[0], 0, inverse=True))

        stages = [("ffn2_in", l)]
        (dkf, dvf, dqf, dc_rows), got = fox_bwd(s["qf"], s["kf"], s["vf"], s["kt"], dof, s["lse_fox"],
                                         delta[:, :8].T.reshape(8, 1, m_rows), f"fox_bwd_{l}", scatter_rider(stages))
        parts.update(zip(keys_of(stages), got))
        dc = _lane_pad(dc_rows.reshape(8, m_rows).T)
        dqb, dkb, dvb, dbias, dsink = swa_bwd(s["qb"], s["kb"], s["vb"], bias, swa_sinks[l], s["o_swa"], s["lse_swa"],
                                              dos, f"swa_bwd_{l}")
        dbias_total = dbias if dbias_total is None else dbias_total + dbias
        gs["swa_sinks"][l] = dsink[0, :8]
        dproj, ggqa, ggka, ggqb, ggkb, gfb = qk_post_bwd(s["proj"], s["gains"], s["fbias"], dqf, dkf, dvf, dqb, dkb,
                                                         dvb, dc, dga, dgb, cols, f"qk_post_bwd_{l}")
        gs["fox_q_norm"][l] = ggqa[0, :HEAD_DIM] * scale
        gs["fox_k_norm"][l] = ggka[0, :HEAD_DIM]
        gs["swa_q_norm"][l] = ggqb.reshape(8, HEAD_DIM).sum(0) * scale
        gs["swa_k_norm"][l] = ggkb.reshape(2, HEAD_DIM).sum(0)
        gs["forget_bias"][l] = gfb[0, :8]
        dwp = matmul_tn(dproj[None], s["nm"][None], f"dw_in_{l}", tk=1024 if cols.np % 1024 == 0 else cols.np)[0]
        gw["w_in", l] = _restore_w_in(dwp, cols, din_shard)
        dh, gs["mix_norm"][l] = dproj_bwd(dh, s["h1"], mix_norm[l:l + 1], dproj, w["wp"], f"dproj_bwd_{l}")

        dh, gs["ffn1_norm"][l] = ffn_back(dh, "ffn1", s["h0"], ffn1_norm[l:l + 1], s["n1"], s["a1"], s["fg1"],
                                          s["fu1"], [("mix", l)])

    grad_x = dh[LANES:][None]
    dmeta = dh[PAD_FRONT:LANES]
    dtable = bias_reduce(dbias_total, "swa_dbias")[:, :8]

    last = ("ffn1_w_in", 0)
    send_sems, recv_sems, src_thru, land_thru, token = scatter_start(gw[last], "scatter_last_start")
    big_out = [{}, {}, {}, {}]
    for n in BIG:
        if n != last[0]:
            update = adamw_sum_cols if n == "w_in" else adamw_sum
            outs = update([parts[n, l] for l in range(depth)], wts[n], mom1[n], mom2[n], f"adamw_{n}", after=token)
            for k in range(4):
                big_out[k][n] = outs[k]
    sent, landed = scatter_wait(send_sems, recv_sems, src_thru, land_thru,
                                [big_out[1][n] for n in BIG if n != last[0]], "scatter_last_wait")
    parts[last] = lax.dynamic_update_slice_in_dim(landed, lax.dynamic_slice_in_dim(sent, dev, 1, axis=0), dev, axis=0)
    outs = adamw_sum([parts[last[0], l] for l in range(depth)], wts[last[0]], mom1[last[0]], mom2[last[0]],
                     f"adamw_{last[0]}")
    for k in range(4):
        big_out[k][last[0]] = outs[k]

    small_g = {n: (jnp.stack(gs[n]) if n != "rel_bias_table" else None) for n in SMALL}
    small_g["rel_bias_table"] = dtable
    pieces = [loss_part[0:1, 0:1].reshape(1, 1)] + [small_g[n].reshape(1, -1) for n in SMALL] + [dmeta.reshape(1, -1)]
    small_shapes = [(1,)] + [wts[n].shape for n in SMALL] + [(N_META, d)]
    total = allsum_small(_pack(pieces, LANES, 8, F32), "allsum_small")
    summed = _unpack(total, small_shapes)
    loss = summed[0][0]
    g_small = dict(zip(SMALL, summed[1:1 + len(SMALL)]))
    g_meta = lax.dynamic_slice_in_dim(summed[-1], dev * (d // N_DEV), d // N_DEV, axis=1)
    names = SMALL + ("meta_tokens",)
    g_small["meta_tokens"] = g_meta
    pk = lambda src: _pack([src[n].reshape(1, -1) for n in names], LANES, 8, F32)[0]
    small_out = [dict(zip(names, _unpack(o, [wts[n].shape for n in names])))
                 for o in adamw_small(pk(g_small), pk(wts), pk(mom1), pk(mom2), "adamw_small")]

    for out in big_out:
        for n in flipped:
            out[n] = jnp.swapaxes(out[n], 1, 2)
        out["w_in"] = from_rows(out["w_in"])
    grads = {**big_out[0], **g_small}
    delta = {**big_out[1], **small_out[0]}
    new_m = {**big_out[2], **small_out[1]}
    new_v = {**big_out[3], **small_out[2]}
    return (loss, grad_x, *[grads[n] for n in WEIGHTS], *[delta[n] for n in WEIGHTS],
            *[new_m[n] for n in WEIGHTS], *[new_v[n] for n in WEIGHTS])
```

```python
import math

import numpy as np
import jax
import jax.numpy as jnp
from jax import lax
from jax.experimental import pallas as pl
from jax.experimental.pallas import tpu as pltpu

F32 = jnp.float32
BF16 = jnp.bfloat16
EPS = 1e-6
NEG = -1e30
HEAD_DIM = 64
LANES = 128
N_META = 16
PAD_FRONT = LANES - N_META
N_BUCKETS = 32
MAX_DISTANCE = 128
N_DEV = 8
ADAM_LR, ADAM_B1, ADAM_B2, ADAM_EPS, ADAM_WD, ADAM_STEP = 0.001, 0.9, 0.999, 1e-08, 0.01, 10
VMEM_LIMIT = 56 * 1024 * 1024
MESH = pl.DeviceIdType.MESH


def _params(n_grid):
    return pltpu.CompilerParams(dimension_semantics=("arbitrary",) * n_grid,
                                vmem_limit_bytes=VMEM_LIMIT)


def _dot(a, b):
    return jnp.dot(a, b, preferred_element_type=F32)


def _dot_nt(a, b):
    return lax.dot_general(a, b, (((1,), (1,)), ((), ())), preferred_element_type=F32)


def _dot_tn(a, b):
    return lax.dot_general(a, b, (((0,), (0,)), ((), ())), preferred_element_type=F32)


def _rms(x):
    r = lax.rsqrt(jnp.mean(x * x, axis=-1, keepdims=True) + EPS)
    return x * r, r


def _rms_bwd(x, g, dn):
    xh, r = _rms(x)
    dxh = dn * g
    dx = r * (dxh - xh * jnp.mean(dxh * xh, axis=-1, keepdims=True))
    return dx, jnp.sum(dn * xh, axis=0, keepdims=True)


def _split2(v):
    hi = v.astype(BF16)
    return hi, (v - hi.astype(F32)).astype(BF16)


def _split3(v):
    hi = v.astype(BF16)
    r1 = v - hi.astype(F32)
    mid = r1.astype(BF16)
    return hi, mid, (r1 - mid.astype(F32)).astype(BF16)


def _group_ones():
    r = lax.broadcasted_iota(jnp.int32, (LANES, LANES), 0) // HEAD_DIM
    c = lax.broadcasted_iota(jnp.int32, (LANES, LANES), 1) // HEAD_DIM
    return jnp.where(r == c, 1.0, 0.0).astype(BF16)


def _group_mean(v, ones):
    hi, lo = _split2(v)
    return (_dot(hi, ones) + _dot(lo, ones)) * (1.0 / HEAD_DIM)


def _row_tile(m):
    return 384 if m % 384 == 0 else LANES


def _tile(m, cap):
    return max(t for t in range(16, cap + 1, 16) if m % t == 0)


def _peer(k):
    x, y, c = lax.axis_index("x"), lax.axis_index("y"), lax.axis_index("c")
    px = 1 - x if k & 4 else x
    py = 1 - y if k & 2 else y
    pc = 1 - c if k & 1 else c
    return (px, py, pc), 4 * px + 2 * py + pc


def _exchange_body(src_ref, dst_ref, send_sems, recv_sems, local_sem, bcast):
    x, y, c = lax.axis_index("x"), lax.axis_index("y"), lax.axis_index("c")
    me = 4 * x + 2 * y + c
    mine = pltpu.make_async_copy(src_ref.at[0 if bcast else me], dst_ref.at[me], local_sem)
    mine.start()
    sends = []
    for k in range(1, N_DEV):
        dev, idx = _peer(k)
        cp = pltpu.make_async_remote_copy(
            src_ref=src_ref.at[0 if bcast else idx], dst_ref=dst_ref.at[me],
            send_sem=send_sems.at[k - 1], recv_sem=recv_sems.at[k - 1],
            device_id=dev, device_id_type=MESH)
        cp.start()
        sends.append(cp)
    for k in range(1, N_DEV):
        dev, idx = _peer(k)
        pltpu.make_async_remote_copy(
            src_ref=src_ref.at[0], dst_ref=dst_ref.at[idx],
            send_sem=send_sems.at[k - 1], recv_sem=recv_sems.at[k - 1],
            device_id=dev, device_id_type=MESH).wait_recv()
    for cp in sends:
        cp.wait_send()
    mine.wait()


class Rider:
    FIRST = (1, 2, 4, 6)
    RELAYED = (2, 4, 6)

    def __init__(self, srcs=(), bcast=True):
        self.srcs, self.bcast, self.n = list(srcs), bcast, len(srcs)

    def out_shapes(self):
        return [jax.ShapeDtypeStruct(((N_DEV,) + s.shape) if self.bcast else s.shape, s.dtype) for s in self.srcs]

    def specs(self):
        return [pl.BlockSpec(memory_space=pl.ANY)] * self.n

    def scratch(self):
        if not self.n:
            return []
        return [pltpu.SemaphoreType.DMA((self.n * (N_DEV - 1),)), pltpu.SemaphoreType.DMA((self.n * (N_DEV - 1),)),
                pltpu.SemaphoreType.DMA((self.n,))]

    @staticmethod
    def _copy(src, dst, a, pair, dev, send_sems, recv_sems):
        sem = a * (N_DEV - 1) + pair - 1
        return pltpu.make_async_remote_copy(src_ref=src, dst_ref=dst, send_sem=send_sems.at[sem],
                                            recv_sem=recv_sems.at[sem], device_id=dev, device_id_type=MESH)

    def _first(self):
        return self.FIRST if self.bcast else range(1, N_DEV)

    def _own(self, s, d, a, local_sems):
        me = 4 * lax.axis_index("x") + 2 * lax.axis_index("y") + lax.axis_index("c")
        return pltpu.make_async_copy(s if self.bcast else s.at[me], d.at[me], local_sems.at[a]), me

    def start(self, src_refs, dst_refs, send_sems, recv_sems, local_sems):
        for a, (s, d) in enumerate(zip(src_refs, dst_refs)):
            own, me = self._own(s, d, a, local_sems)
            own.start()
            for k in self._first():
                dev, idx = _peer(k)
                self._copy(s if self.bcast else s.at[idx], d.at[me], a, k, dev, send_sems, recv_sems).start()

    def relay(self, src_refs, dst_refs, send_sems, recv_sems, local_sems):
        if not self.bcast:
            return
        sibling, _ = _peer(1)
        for a, d in enumerate(dst_refs):
            for k in self.RELAYED:
                dev, idx = _peer(k)
                self._copy(d.at[idx], d.at[idx], a, k, dev, send_sems, recv_sems).wait_recv()
                self._copy(d.at[idx], d.at[idx], a, k + 1, sibling, send_sems, recv_sems).start()

    def wait(self, src_refs, dst_refs, send_sems, recv_sems, local_sems):
        sibling, _ = _peer(1)
        for a, (s, d) in enumerate(zip(src_refs, dst_refs)):
            own, me = self._own(s, d, a, local_sems)
            for k in range(1, N_DEV):
                if not (self.bcast and k in self.RELAYED):
                    dev, idx = _peer(k)
                    self._copy(d.at[idx], d.at[idx], a, k, dev, send_sems, recv_sems).wait_recv()
            for k in self._first():
                dev, idx = _peer(k)
                self._copy(s if self.bcast else s.at[idx], d.at[me], a, k, dev, send_sems, recv_sems).wait_send()
            if self.bcast:
                for k in self.RELAYED:
                    dev, idx = _peer(k)
                    self._copy(d.at[idx], d.at[idx], a, k + 1, sibling, send_sems, recv_sems).wait_send()
            own.wait()


RELAY_AT = 7


def rider_call(core, name, grid, in_specs, out_specs, out_shape, scratch_shapes, args, rider=None):
    rider = rider or Rider()
    n_in, n_out, n_scr, nr = len(in_specs), len(out_specs), len(scratch_shapes), rider.n

    def body(*refs):
        ins, r_src = refs[:n_in], refs[n_in:n_in + nr]
        outs = refs[n_in + nr:n_in + nr + n_out]
        r_dst = refs[n_in + nr + n_out:n_in + 2 * nr + n_out]
        scr = refs[n_in + 2 * nr + n_out:n_in + 2 * nr + n_out + n_scr]
        sems = refs[n_in + 2 * nr + n_out + n_scr:]
        if nr:
            first, last, step, steps = True, True, 0, 1
            for ax, size in enumerate(grid):
                first = first & (pl.program_id(ax) == 0)
                last = last & (pl.program_id(ax) == size - 1)
                step = step * size + pl.program_id(ax)
                steps *= size
            relay = step == RELAY_AT * steps // 8
            if not grid:
                rider.start(r_src, r_dst, *sems)
                rider.relay(r_src, r_dst, *sems)
            else:
                pl.when(first)(lambda: rider.start(r_src, r_dst, *sems))
                if rider.bcast:
                    pl.when(relay)(lambda: rider.relay(r_src, r_dst, *sems))
        core(*ins, *outs, *scr)
        if nr:
            if not grid:
                rider.wait(r_src, r_dst, *sems)
            else:
                pl.when(last)(lambda: rider.wait(r_src, r_dst, *sems))

    res = pl.pallas_call(
        body, name=name, grid=grid,
        in_specs=list(in_specs) + rider.specs(),
        out_specs=list(out_specs) + rider.specs(),
        out_shape=list(out_shape) + rider.out_shapes(),
        scratch_shapes=list(scratch_shapes) + rider.scratch(),
        compiler_params=_params(len(grid)),
    )(*args, *rider.srcs)
    return res[:n_out], res[n_out:]


def exchange_hbm(srcs, bcast, name):
    return rider_call(lambda: None, name, (), [], [], [], [], [], Rider(srcs, bcast))[1]


_HBM = pl.BlockSpec(memory_space=pltpu.HBM)
_SEM = pl.BlockSpec(memory_space=pltpu.SEMAPHORE)
_EFFECT = pltpu.CompilerParams(has_side_effects=pltpu.SideEffectType.DATAFLOW_SIDE_EFFECTING)


def scatter_start(src, name):
    def body(src_ref, land_ref, send_sems, recv_sems, src_thru, land_thru, token):
        me = 4 * lax.axis_index("x") + 2 * lax.axis_index("y") + lax.axis_index("c")
        for k in range(1, N_DEV):
            dev, idx = _peer(k)
            pltpu.make_async_remote_copy(src_ref=src_ref.at[idx], dst_ref=land_ref.at[me], send_sem=send_sems.at[k - 1],
                                         recv_sem=recv_sems.at[k - 1], device_id=dev, device_id_type=MESH).start()
        token[...] = jnp.zeros_like(token)

    return pl.pallas_call(
        body, name=name,
        out_shape=(pltpu.SemaphoreType.DMA((N_DEV - 1,)), pltpu.SemaphoreType.DMA((N_DEV - 1,)),
                   pltpu.HBM(src.shape, src.dtype), pltpu.HBM(src.shape, src.dtype), jax.ShapeDtypeStruct((8, LANES), F32)),
        in_specs=(_HBM, _HBM), out_specs=(_SEM, _SEM, _HBM, _HBM, pl.BlockSpec(memory_space=pltpu.VMEM)),
        input_output_aliases={0: 2, 1: 3}, compiler_params=_EFFECT,
    )(pltpu.with_memory_space_constraint(src, pltpu.HBM),
      pltpu.with_memory_space_constraint(lax.empty(src.shape, src.dtype), pltpu.HBM))


def scatter_wait(send_sems, recv_sems, src_thru, land_thru, after, name):
    n_after = len(after)

    def body(*refs):
        src_ref, land_ref, send_sems, recv_sems = refs[:4]
        for k in range(1, N_DEV):
            dev, idx = _peer(k)
            copy = pltpu.make_async_remote_copy(src_ref=src_ref.at[idx], dst_ref=land_ref.at[idx],
                                                send_sem=send_sems.at[k - 1], recv_sem=recv_sems.at[k - 1],
                                                device_id=dev, device_id_type=MESH)
            copy.wait_send()
            copy.wait_recv()

    return pl.pallas_call(
        body, name=name,
        out_shape=(pltpu.HBM(src_thru.shape, src_thru.dtype), pltpu.HBM(land_thru.shape, land_thru.dtype)),
        in_specs=(_HBM, _HBM, _SEM, _SEM) + (pl.BlockSpec(memory_space=pl.ANY),) * n_after, out_specs=(_HBM, _HBM),
        input_output_aliases={0: 0, 1: 1}, compiler_params=_EFFECT,
    )(src_thru, land_thru, send_sems, recv_sems, *after)


def allsum_small(vec, name):
    def body(src_ref, out_ref, dst_ref, send_sems, recv_sems, local_sem):
        _exchange_body(src_ref, dst_ref, send_sems, recv_sems, local_sem, True)
        acc = dst_ref[0]
        for j in range(1, N_DEV):
            acc = acc + dst_ref[j]
        out_ref[...] = acc

    return pl.pallas_call(
        body, name=name,
        out_shape=jax.ShapeDtypeStruct(vec.shape[1:], F32),
        in_specs=[pl.BlockSpec(memory_space=pltpu.VMEM)],
        out_specs=pl.BlockSpec(memory_space=pltpu.VMEM),
        scratch_shapes=[pltpu.VMEM((N_DEV,) + vec.shape[1:], F32),
                        pltpu.SemaphoreType.DMA((N_DEV - 1,)), pltpu.SemaphoreType.DMA((N_DEV - 1,)),
                        pltpu.SemaphoreType.DMA],
    )(vec)


def gather_small(vec, name):
    def body(src_ref, dst_ref, send_sems, recv_sems, local_sem):
        _exchange_body(src_ref, dst_ref, send_sems, recv_sems, local_sem, True)

    return pl.pallas_call(
        body, name=name,
        out_shape=jax.ShapeDtypeStruct((N_DEV,) + vec.shape[1:], F32),
        in_specs=[pl.BlockSpec(memory_space=pltpu.VMEM)],
        out_specs=pl.BlockSpec(memory_space=pltpu.VMEM),
        scratch_shapes=[pltpu.SemaphoreType.DMA((N_DEV - 1,)), pltpu.SemaphoreType.DMA((N_DEV - 1,)),
                        pltpu.SemaphoreType.DMA],
    )(vec)


FFN_FWD_ROWS = 1056
FFN_BWD_ROWS = 704
FFN_BWD_CHUNKS = 4
DW_ROWS = 1408

def ffn_fwd(h, g, w_in8, w_out4, name, rider=None):
    m, d = h.shape
    fb = w_in8.shape[1]
    tm = _tile(m, FFN_FWD_ROWS)

    def body(h_ref, g_ref, wg_ref, wu_ref, wo_ref, hn_ref, n_ref, a_ref, fg_ref, fu_ref, acc_ref):
        i = pl.program_id(1)

        @pl.when(i == 0)
        def _():
            xh, _ = _rms(h_ref[...])
            n_ref[...] = (xh * g_ref[...]).astype(BF16)
            acc_ref[...] = jnp.zeros_like(acc_ref)

        n = n_ref[...]
        gate = _dot_nt(n, wg_ref[0])
        up = _dot_nt(n, wu_ref[0])
        sg = jax.nn.sigmoid(gate)
        silu = gate * sg
        a = (silu * up).astype(BF16)
        a_ref[0] = a
        fg_ref[0] = (up * (sg * (1.0 + gate * (1.0 - sg)))).astype(BF16)
        fu_ref[0] = silu.astype(BF16)
        acc_ref[...] += _dot(a, wo_ref[0])

        @pl.when(i == 3)
        def _():
            hn_ref[...] = h_ref[...] + 0.5 * acc_ref[...]

    return rider_call(
        body, name, (m // tm, 4),
        in_specs=[pl.BlockSpec((tm, d), lambda r, i: (r, 0)),
                  pl.BlockSpec((1, d), lambda r, i: (0, 0)),
                  pl.BlockSpec((1, fb, d), lambda r, i: (i, 0, 0)),
                  pl.BlockSpec((1, fb, d), lambda r, i: (i + 4, 0, 0)),
                  pl.BlockSpec((1, fb, d), lambda r, i: (i, 0, 0))],
        out_specs=[pl.BlockSpec((tm, d), lambda r, i: (r, 0)),
                   pl.BlockSpec((tm, d), lambda r, i: (r, 0))] + [pl.BlockSpec((1, tm, fb), lambda r, i: (i, r, 0))] * 3,
        out_shape=[jax.ShapeDtypeStruct((m, d), F32), jax.ShapeDtypeStruct((m, d), BF16)]
                  + [jax.ShapeDtypeStruct((4, m, fb), BF16)] * 3,
        scratch_shapes=[pltpu.VMEM((tm, d), F32)],
        args=(h, g, w_in8, w_in8, w_out4), rider=rider)


def ffn_bwd(dh, h, g, f_gate, f_up, w_in8, w_out4, name, rider=None):
    m, d = h.shape
    fb = w_in8.shape[1]
    tm = _tile(m, FFN_BWD_ROWS)

    def body(dh_ref, h_ref, g_ref, fg_ref, fu_ref, wg_ref, wu_ref, wo_ref,
             dhin_ref, dg_ref, du_ref, dgn_ref, dhs_ref, acc_ref):
        r = pl.program_id(0)
        i = pl.program_id(1)

        @pl.when(i == 0)
        def _():
            dhs_ref[...] = (0.5 * dh_ref[...]).astype(BF16)
            acc_ref[...] = jnp.zeros_like(acc_ref)

        @pl.when((r == 0) & (i == 0))
        def _():
            dgn_ref[...] = jnp.zeros_like(dgn_ref)

        chunks = FFN_BWD_CHUNKS if tm % (16 * FFN_BWD_CHUNKS) == 0 else 1
        for c in range(chunks):
            rows = slice(c * tm // chunks, (c + 1) * tm // chunks)
            da = _dot_nt(dhs_ref[rows, :], wo_ref[0])
            dub = (da * fu_ref[0, rows, :]).astype(BF16)
            dgb = (da * fg_ref[0, rows, :]).astype(BF16)
            dg_ref[0, rows, :] = dgb
            du_ref[0, rows, :] = dub
            acc_ref[rows, :] += _dot(dgb, wg_ref[0]) + _dot(dub, wu_ref[0])

        @pl.when(i == 3)
        def _():
            dx, dgain = _rms_bwd(h_ref[...], g_ref[...], acc_ref[...])
            dgn_ref[...] += dgain
            dhin_ref[...] = dh_ref[...] + dx

    row = lambda r, i: (r, 0)
    blk = lambda r, i: (i, r, 0)
    return rider_call(
        body, name, (m // tm, 4),
        in_specs=[pl.BlockSpec((tm, d), row), pl.BlockSpec((tm, d), row),
                  pl.BlockSpec((1, d), lambda r, i: (0, 0)),
                  pl.BlockSpec((1, tm, fb), blk), pl.BlockSpec((1, tm, fb), blk),
                  pl.BlockSpec((1, fb, d), lambda r, i: (i, 0, 0)),
                  pl.BlockSpec((1, fb, d), lambda r, i: (i + 4, 0, 0)),
                  pl.BlockSpec((1, fb, d), lambda r, i: (i, 0, 0))],
        out_specs=[pl.BlockSpec((tm, d), row),
                   pl.BlockSpec((1, tm, fb), blk), pl.BlockSpec((1, tm, fb), blk),
                   pl.BlockSpec((1, d), lambda r, i: (0, 0)),
                   pl.BlockSpec((tm, d), row)],
        out_shape=[jax.ShapeDtypeStruct((m, d), F32),
                   jax.ShapeDtypeStruct((4, m, fb), BF16), jax.ShapeDtypeStruct((4, m, fb), BF16),
                   jax.ShapeDtypeStruct((1, d), F32), jax.ShapeDtypeStruct((m, d), BF16)],
        scratch_shapes=[pltpu.VMEM((tm, d), F32)],
        args=(dh, h, g, f_gate, f_up, w_in8, w_in8, w_out4), rider=rider)


def matmul_tn(x, y, name, tn=None, tk=None, x2=None, rider=None):
    if tk is not None:
        assert x.shape[0] == y.shape[0] == 1 and x2 is None and rider is None
        bk, m, kf = x.shape[2] // tk, x.shape[1], x.shape[2]
        tm = _tile(m, DW_ROWS)
        tn_ = y.shape[2] if tn is None else tn

        def tiled(x_ref, y_ref, o_ref, acc_ref):
            r = pl.program_id(2)

            @pl.when(r == 0)
            def _():
                acc_ref[...] = jnp.zeros_like(acc_ref)

            acc_ref[...] += _dot_tn(x_ref[0].astype(BF16), y_ref[0].astype(BF16))

            @pl.when(r == m // tm - 1)
            def _():
                o_ref[0] = acc_ref[...].astype(BF16)

        return pl.pallas_call(
            tiled, name=name, grid=(bk, y.shape[2] // tn_, m // tm),
            in_specs=[pl.BlockSpec((1, tm, tk), lambda i, j, r: (0, r, i)),
                      pl.BlockSpec((1, tm, tn_), lambda i, j, r: (0, r, j))],
            out_specs=pl.BlockSpec((1, tk, tn_), lambda i, j, r: (0, i, j)),
            out_shape=jax.ShapeDtypeStruct((1, kf, y.shape[2]), BF16),
            scratch_shapes=[pltpu.VMEM((tk, tn_), F32)],
            compiler_params=_params(3),
        )(x, y)
    bx, m, k = x.shape
    by, _, n = y.shape
    b = max(bx, by) * (2 if x2 is not None else 1)
    tm = _tile(m, DW_ROWS)
    tn = n if tn is None else tn
    nt = n // tn
    nr = m // tm

    def body(*refs):
        x_ref, y_ref = refs[0], refs[-3]
        o_ref, acc_ref = refs[-2], refs[-1]
        r = pl.program_id(2)

        @pl.when(r == 0)
        def _():
            acc_ref[...] = jnp.zeros_like(acc_ref)

        if x2 is None:
            acc_ref[...] += _dot_tn(x_ref[0].astype(BF16), y_ref[0].astype(BF16))
        else:
            @pl.when(pl.program_id(0) < bx)
            def _():
                acc_ref[...] += _dot_tn(x_ref[0].astype(BF16), y_ref[0].astype(BF16))

            @pl.when(pl.program_id(0) >= bx)
            def _():
                acc_ref[...] += _dot_tn(refs[1][0].astype(BF16), y_ref[0].astype(BF16))

        @pl.when(r == nr - 1)
        def _():
            o_ref[0] = acc_ref[...].astype(BF16)

    if x2 is None:
        x_specs = [pl.BlockSpec((1, tm, k), (lambda i, j, r: (i, r, 0)) if bx > 1 else (lambda i, j, r: (0, r, 0)))]
    else:
        x_specs = [pl.BlockSpec((1, tm, k), lambda i, j, r: (jnp.minimum(i, bx - 1), jnp.where(i < bx, r, nr - 1), 0)),
                   pl.BlockSpec((1, tm, k), lambda i, j, r: (jnp.maximum(i - bx, 0), jnp.where(i < bx, 0, r), 0))]
    y_map = (lambda i, j, r: (i, r, j)) if by > 1 else (lambda i, j, r: (0, r, j))
    (out,), carried = rider_call(
        body, name, (b, nt, nr),
        in_specs=x_specs + [pl.BlockSpec((1, tm, tn), y_map)],
        out_specs=[pl.BlockSpec((1, k, tn), lambda i, j, r: (i, 0, j))],
        out_shape=[jax.ShapeDtypeStruct((b, k, n), BF16)],
        scratch_shapes=[pltpu.VMEM((k, tn), F32)],
        args=[x] + ([x2] if x2 is not None else []) + [y], rider=rider)
    return (out, carried) if rider is not None else out


AUG = HEAD_DIM


class _Cols:
    def __init__(self, d):
        self.d = d
        self.ga, self.gb = 0, d
        self.qa, self.ka, self.va = 2 * d, 2 * d + 1024, 2 * d + 2048
        self.qb = 2 * d + 3072
        self.kb, self.vb, self.fa = self.qb + 512, self.qb + 640, self.qb + 768
        self.np = self.qb + 1024


def mixer_proj(h, g, wp, name):
    m, d = h.shape
    npad = wp.shape[0]
    tm = _row_tile(m)

    def body(h_ref, g_ref, w_ref, n_ref, p_ref):
        xh, _ = _rms(h_ref[...])
        n = (xh * g_ref[...]).astype(BF16)
        n_ref[...] = n
        p_ref[...] = _dot_nt(n, w_ref[...])

    return pl.pallas_call(
        body, name=name, grid=(m // tm,),
        in_specs=[pl.BlockSpec((tm, d), lambda r: (r, 0)), pl.BlockSpec((1, d), lambda r: (0, 0)),
                  pl.BlockSpec((npad, d), lambda r: (0, 0))],
        out_specs=[pl.BlockSpec((tm, d), lambda r: (r, 0)), pl.BlockSpec((tm, npad), lambda r: (r, 0))],
        out_shape=[jax.ShapeDtypeStruct((m, d), BF16), jax.ShapeDtypeStruct((m, npad), F32)],
        compiler_params=_params(1),
    )(h, g, wp)


def _head_norm(x, gain, ones):
    outs = []
    for b in range(x.shape[1] // LANES):
        xb = x[:, b * LANES:(b + 1) * LANES]
        r = lax.rsqrt(_group_mean(xb * xb, ones) + EPS)
        outs.append(xb * r * gain[:, b * LANES:(b + 1) * LANES])
    return outs


def _head_norm_bwd(x, gain, dn, ones):
    dxs, dgs = [], []
    for b in range(x.shape[1] // LANES):
        sl = slice(b * LANES, (b + 1) * LANES)
        xb, dnb = x[:, sl], dn[:, sl]
        r = lax.rsqrt(_group_mean(xb * xb, ones) + EPS)
        xh = xb * r
        dxh = dnb * gain[:, sl]
        dxs.append(r * (dxh - xh * _group_mean(dxh * xh, ones)))
        dgs.append(jnp.sum(dnb * xh, axis=0, keepdims=True))
    return dxs, dgs


def _lane_col(v, lane_iota, idx):
    return jnp.sum(jnp.where(lane_iota == idx, v, 0.0), axis=1, keepdims=True)


def _aug(base, lane, vals):
    for i, v in enumerate(vals):
        base = jnp.where(lane == AUG + i, v, base)
    return base


def qk_post(proj, gains, fbias, cols, name):
    m = proj.shape[0]
    tm = _row_tile(m)
    gqa, gka, gqb, gkb = gains

    def body(qa_ref, ka_ref, va_ref, qb_ref, kb_ref, vb_ref, fa_ref, gqa_ref, gka_ref, gqb_ref, gkb_ref, fb_ref,
             qf_o, kf_o, vf_o, kt_o, vt_o, qb_o, kb_o, vb_o, carry_ref):
        r0 = pl.program_id(0)

        @pl.when(r0 == 0)
        def _():
            carry_ref[...] = jnp.zeros_like(carry_ref)

        z = fa_ref[...] + fb_ref[...]
        logf = jnp.minimum(z, 0.0) - jnp.log(1.0 + jnp.exp(-jnp.abs(z)))
        rr = lax.broadcasted_iota(jnp.int32, (tm, tm), 0)
        cc = lax.broadcasted_iota(jnp.int32, (tm, tm), 1)
        tril = jnp.where(cc <= rr, 1.0, 0.0).astype(BF16)
        p0, p1, p2 = _split3(logf)
        c = _dot(tril, p0) + _dot(tril, p1) + _dot(tril, p2) + carry_ref[...]
        carry_ref[...] += jnp.sum(logf, axis=0, keepdims=True)

        lane = lax.broadcasted_iota(jnp.int32, (tm, LANES), 1)
        is_pad = (r0 * tm + lax.broadcasted_iota(jnp.int32, (tm, 1), 0)) < PAD_FRONT
        ones = jnp.ones((LANES, LANES), BF16)
        for hd in range(8):
            sl = slice(hd * LANES, (hd + 1) * LANES)
            ch = _lane_col(c, lane, hd)
            ct = [p.astype(F32) for p in _split3(ch)]
            cs = [p.astype(F32) for p in _split3(-jnp.where(is_pad, -NEG, ch))]
            xq = qa_ref[:, sl]
            qn = xq * lax.rsqrt(_group_mean(xq * xq, ones) + EPS) * gqa_ref[...]
            qf_o[:, sl] = _aug(qn, lane, ct + [1.0, 1.0, 1.0]).astype(BF16)
            xk = ka_ref[:, sl]
            kn = xk * lax.rsqrt(_group_mean(xk * xk, ones) + EPS) * gka_ref[...]
            kf = _aug(kn, lane, [1.0, 1.0, 1.0] + cs)
            vf = _aug(va_ref[:, sl], lane, [1.0, 1.0, 1.0])
            kf_o[:, sl] = kf.astype(BF16)
            vf_o[:, sl] = vf.astype(BF16)
            kt_o[sl, :] = kf.T.astype(BF16)
            vt_o[sl, :] = vf.T.astype(BF16)

        gones = _group_ones()
        for src, gn, dst in ((qb_ref, gqb_ref, qb_o), (kb_ref, gkb_ref, kb_o)):
            for b, blk in enumerate(_head_norm(src[...], gn[...], gones)):
                dst[:, b * LANES:(b + 1) * LANES] = blk.astype(BF16)
        vb_o[...] = vb_ref[...].astype(BF16)

    w1024 = lambda off: pl.BlockSpec((tm, 1024), lambda r, o=off // 1024: (r, o))
    w512 = lambda off: pl.BlockSpec((tm, 512), lambda r, o=off // 512: (r, o))
    w128 = lambda off: pl.BlockSpec((tm, LANES), lambda r, o=off // LANES: (r, o))
    vec = lambda w: pl.BlockSpec((1, w), lambda r: (0, 0))
    row = lambda w: pl.BlockSpec((tm, w), lambda r: (r, 0))
    return pl.pallas_call(
        body, name=name, grid=(m // tm,),
        in_specs=[w1024(cols.qa), w1024(cols.ka), w1024(cols.va), w512(cols.qb), w128(cols.kb), w128(cols.vb),
                  w128(cols.fa), vec(LANES), vec(LANES), vec(512), vec(LANES), vec(LANES)],
        out_specs=[row(1024), row(1024), row(1024)] + [pl.BlockSpec((1024, tm), lambda r: (0, r))] * 2
                  + [row(512), row(LANES), row(LANES)],
        out_shape=[jax.ShapeDtypeStruct((m, 1024), BF16)] * 3 + [jax.ShapeDtypeStruct((1024, m), BF16)] * 2
                  + [jax.ShapeDtypeStruct((m, 512), BF16)] + [jax.ShapeDtypeStruct((m, LANES), BF16)] * 2,
        scratch_shapes=[pltpu.VMEM((1, LANES), F32)],
        compiler_params=_params(1),
    )(proj, proj, proj, proj, proj, proj, proj, gqa, gka, gqb, gkb, fbias)


def qk_post_bwd(proj, gains, fbias, dqf, dkf, dvf, dqb, dkb, dvb, dc, dga, dgb, cols, name):
    m = proj.shape[0]
    d = cols.d
    tm = _row_tile(m)
    nt = m // tm
    gqa, gka, gqb, gkb = gains

    def body(qa_ref, ka_ref, qb_ref, kb_ref, fa_ref, gqa_ref, gka_ref, gqb_ref, gkb_ref, fb_ref,
             dqf_ref, dkf_ref, dvf_ref, dqb_ref, dkb_ref, dvb_ref, dc_ref, dga_ref, dgb_ref,
             dp_o, ggqa_o, ggka_o, ggqb_o, ggkb_o, gfb_o, carry_ref):
        @pl.when(pl.program_id(0) == 0)
        def _():
            carry_ref[...] = jnp.zeros_like(carry_ref)
            for o in (ggqa_o, ggka_o, ggqb_o, ggkb_o, gfb_o):
                o[...] = jnp.zeros_like(o)

        dp_o[:, cols.ga:cols.ga + d] = dga_ref[...].astype(BF16)
        dp_o[:, cols.gb:cols.gb + d] = dgb_ref[...].astype(BF16)
        dp_o[:, cols.fa + LANES:cols.np] = jnp.zeros((tm, cols.np - cols.fa - LANES), BF16)
        lane = lax.broadcasted_iota(jnp.int32, (tm, LANES), 1)
        data = lane < HEAD_DIM
        ones = jnp.ones((LANES, LANES), BF16)
        for hd in range(8):
            sl = slice(hd * LANES, (hd + 1) * LANES)
            for src, gn, dn_ref, off, gout in ((qa_ref, gqa_ref, dqf_ref, cols.qa, ggqa_o),
                                               (ka_ref, gka_ref, dkf_ref, cols.ka, ggka_o)):
                x = src[:, sl]
                dn = jnp.where(data, dn_ref[:, sl], 0.0)
                r = lax.rsqrt(_group_mean(x * x, ones) + EPS)
                xh = x * r
                dxh = dn * gn[...]
                dp_o[:, off + hd * LANES:off + (hd + 1) * LANES] = (
                    r * (dxh - xh * _group_mean(dxh * xh, ones))).astype(BF16)
                gout[...] += jnp.sum(dn * xh, axis=0, keepdims=True)
            dp_o[:, cols.va + hd * LANES:cols.va + (hd + 1) * LANES] = jnp.where(data, dvf_ref[:, sl], 0.0).astype(BF16)
        dp_o[:, cols.vb:cols.vb + LANES] = dvb_ref[...].astype(BF16)
        gones = _group_ones()
        for src, gn, dn, off, gout in ((qb_ref, gqb_ref, dqb_ref, cols.qb, ggqb_o),
                                       (kb_ref, gkb_ref, dkb_ref, cols.kb, ggkb_o)):
            dxs, dgs = _head_norm_bwd(src[...], gn[...], dn[...], gones)
            for b, (dx, dg) in enumerate(zip(dxs, dgs)):
                dp_o[:, off + b * LANES:off + (b + 1) * LANES] = dx.astype(BF16)
                gout[:, b * LANES:(b + 1) * LANES] += dg
        dcv = dc_ref[...]
        rr = lax.broadcasted_iota(jnp.int32, (tm, tm), 0)
        cc = lax.broadcasted_iota(jnp.int32, (tm, tm), 1)
        triu = jnp.where(cc >= rr, 1.0, 0.0).astype(BF16)
        p0, p1, p2 = _split3(dcv)
        dlogf = _dot(triu, p0) + _dot(triu, p1) + _dot(triu, p2) + carry_ref[...]
        carry_ref[...] += jnp.sum(dcv, axis=0, keepdims=True)
        z = fa_ref[...] + fb_ref[...]
        row = (nt - 1 - pl.program_id(0)) * tm + lax.broadcasted_iota(jnp.int32, (tm, LANES), 0)
        dfa = jnp.where(row >= PAD_FRONT, dlogf * jax.nn.sigmoid(-z), 0.0)
        dp_o[:, cols.fa:cols.fa + LANES] = dfa.astype(BF16)
        gfb_o[...] += jnp.sum(dfa, axis=0, keepdims=True)

    rev = lambda r: nt - 1 - r
    w1024 = lambda off: pl.BlockSpec((tm, 1024), lambda r, o=off // 1024: (rev(r), o))
    w512 = lambda off: pl.BlockSpec((tm, 512), lambda r, o=off // 512: (rev(r), o))
    w128 = lambda off: pl.BlockSpec((tm, LANES), lambda r, o=off // LANES: (rev(r), o))
    vec = lambda w: pl.BlockSpec((1, w), lambda r: (0, 0))
    row = lambda w: pl.BlockSpec((tm, w), lambda r: (rev(r), 0))
    return pl.pallas_call(
        body, name=name, grid=(nt,),
        in_specs=[w1024(cols.qa), w1024(cols.ka), w512(cols.qb), w128(cols.kb), w128(cols.fa),
                  vec(LANES), vec(LANES), vec(512), vec(LANES), vec(LANES),
                  row(1024), row(1024), row(1024), row(512), row(LANES), row(LANES), row(LANES), row(d), row(d)],
        out_specs=[row(cols.np), vec(LANES), vec(LANES), vec(512), vec(LANES), vec(LANES)],
        out_shape=[jax.ShapeDtypeStruct((m, cols.np), BF16), jax.ShapeDtypeStruct((1, LANES), F32),
                   jax.ShapeDtypeStruct((1, LANES), F32), jax.ShapeDtypeStruct((1, 512), F32),
                   jax.ShapeDtypeStruct((1, LANES), F32), jax.ShapeDtypeStruct((1, LANES), F32)],
        scratch_shapes=[pltpu.VMEM((1, LANES), F32)],
        compiler_params=_params(1),
    )(proj, proj, proj, proj, proj, gqa, gka, gqb, gkb, fbias, dqf, dkf, dvf, dqb, dkb, dvb, dc, dga, dgb)


def dproj_bwd(dh, h, g, dproj, wp, name):
    m, d = h.shape
    npad = wp.shape[0]
    tm = _row_tile(m)

    def body(dh_ref, h_ref, g_ref, dp_ref, w_ref, dhin_ref, dgn_ref):
        @pl.when(pl.program_id(0) == 0)
        def _():
            dgn_ref[...] = jnp.zeros_like(dgn_ref)

        dn = _dot(dp_ref[...], w_ref[...])
        dx, dgain = _rms_bwd(h_ref[...], g_ref[...], dn)
        dgn_ref[...] += dgain
        dhin_ref[...] = dh_ref[...] + dx

    row = lambda w: pl.BlockSpec((tm, w), lambda r: (r, 0))
    return pl.pallas_call(
        body, name=name, grid=(m // tm,),
        in_specs=[row(d), row(d), pl.BlockSpec((1, d), lambda r: (0, 0)), row(npad),
                  pl.BlockSpec((npad, d), lambda r: (0, 0))],
        out_specs=[row(d), pl.BlockSpec((1, d), lambda r: (0, 0))],
        out_shape=[jax.ShapeDtypeStruct((m, d), F32), jax.ShapeDtypeStruct((1, d), F32)],
        compiler_params=_params(1),
    )(dh, h, g, dproj, wp)


def _causal_t(t):
    return lax.broadcasted_iota(jnp.int32, (t, t), 0) <= lax.broadcasted_iota(jnp.int32, (t, t), 1)


HEADS_PER_STEP = 4
HEADS_PER_STEP_FWD = 8


def fox_fwd(qf, kf, vt, name, rider=None):
    m = qf.shape[0]
    t = _row_tile(m)
    nq = m // t
    hp = HEADS_PER_STEP_FWD
    w = hp * LANES

    def body(q_ref, k_ref, vt_ref, o_ref, lse_ref, acc_ref, m_ref, p_ref, a_ref):
        qi = pl.program_id(1)
        acc_ref[...] = jnp.zeros_like(acc_ref)
        m_ref[...] = jnp.full_like(m_ref, NEG)

        def scores(ki, slot, mask):
            off = pl.multiple_of(ki * t, t)
            for e in range(hp):
                sl = slice(e * LANES, (e + 1) * LANES)
                s = _dot_nt(k_ref[pl.ds(off, t), sl], q_ref[:, sl])
                if mask is not None:
                    s = jnp.where(mask, s, NEG)
                m_old = m_ref[e]
                m_new = jnp.maximum(m_old, jnp.max(s, axis=0, keepdims=True))
                p_ref[slot, e] = jnp.exp(s - m_new).astype(BF16)
                a_ref[slot, e] = jnp.exp(m_old - m_new)
                m_ref[e] = m_new

        def values(ki, slot):
            off = pl.multiple_of(ki * t, t)
            for e in range(hp):
                sl = slice(e * LANES, (e + 1) * LANES)
                acc_ref[e] = acc_ref[e] * a_ref[slot, e] + _dot(vt_ref[sl, pl.ds(off, t)], p_ref[slot, e])

        causal = _causal_t(t)
        scores(0, 0, causal | (jnp.full((t, t), qi, jnp.int32) > 0))

        def step(ki, carry):
            values(ki - 1, (ki - 1) % 2)
            scores(ki, ki % 2, None)
            return carry

        lax.fori_loop(1, qi, step, 0)

        @pl.when(qi >= 1)
        def _():
            values(qi - 1, (qi - 1) % 2)
            scores(qi, qi % 2, causal)

        values(qi, qi % 2)
        row = lax.broadcasted_iota(jnp.int32, (LANES, t), 0)
        for e in range(hp):
            l = jnp.max(acc_ref[e, AUG:AUG + 8, :], axis=0, keepdims=True)
            o_ref[:, e * LANES:(e + 1) * LANES] = jnp.where(row < HEAD_DIM, acc_ref[e] * (1.0 / l), 0.0).T
            lse_ref[e] = m_ref[e] + jnp.log(l)

    return rider_call(
        body, name, (8 // hp, nq),
        in_specs=[pl.BlockSpec((t, w), lambda hd, i: (i, hd)),
                  pl.BlockSpec((m, w), lambda hd, i: (0, hd)),
                  pl.BlockSpec((w, m), lambda hd, i: (hd, 0))],
        out_specs=[pl.BlockSpec((t, w), lambda hd, i: (i, hd)),
                   pl.BlockSpec((hp, 1, t), lambda hd, i: (hd, 0, i))],
        out_shape=[jax.ShapeDtypeStruct((m, 8 * LANES), F32), jax.ShapeDtypeStruct((8, 1, m), F32)],
        scratch_shapes=[pltpu.VMEM((hp, LANES, t), F32), pltpu.VMEM((hp, 1, t), F32),
                        pltpu.VMEM((2, hp, t, t), BF16), pltpu.VMEM((2, hp, 1, t), F32)],
        args=(qf, kf, vt), rider=rider)


def fox_bwd(qf, kf, vf, kt, dof, lse, delta, name, rider=None):
    m = qf.shape[0]
    t = _row_tile(m)
    nq = m // t
    hp = HEADS_PER_STEP
    w = hp * LANES

    def body(k_ref, v_ref, kt_ref, q_ref, do_ref, lse_ref, delta_ref, dk_ref, dv_ref, dq_ref, dck_ref, dcq_ref,
             dka_ref, dva_ref, dqt_ref):
        ki = pl.program_id(1)

        @pl.when(ki == 0)
        def _():
            dqt_ref[...] = jnp.zeros_like(dqt_ref)

        dka_ref[...] = jnp.zeros_like(dka_ref)
        dva_ref[...] = jnp.zeros_like(dva_ref)

        def tile(qi, diagonal):
            off = pl.multiple_of(qi * t, t)
            for e in range(hp):
                sl = slice(e * LANES, (e + 1) * LANES)
                q = q_ref[pl.ds(off, t), sl]
                do = do_ref[pl.ds(off, t), sl]
                s = _dot_nt(k_ref[:, sl], q)
                if diagonal:
                    s = jnp.where(_causal_t(t), s, NEG)
                p = jnp.exp(s - lse_ref[e, :, pl.ds(off, t)])
                ds = (p * (_dot_nt(v_ref[:, sl], do) - delta_ref[e, :, pl.ds(off, t)])).astype(BF16)
                dva_ref[:, sl] += _dot(p.astype(BF16), do)
                dka_ref[:, sl] += _dot(ds, q)
                dqt_ref[sl, pl.ds(off, t)] += _dot(kt_ref[sl, :], ds)

        def step(qi, carry):
            tile(qi, False)
            return carry

        tile(ki, True)
        lax.fori_loop(ki + 1, nq, step, 0)
        dk_ref[...] = dka_ref[...]
        dv_ref[...] = dva_ref[...]
        row8 = lax.broadcasted_iota(jnp.int32, (8, 1), 0)
        for e in range(hp):
            slab = dka_ref[:, e * LANES:(e + 1) * LANES].T[AUG:AUG + 8, :]
            dck_ref[e] = -jnp.sum(jnp.where(row8 == 3, slab, 0.0), axis=0, keepdims=True)

        @pl.when(ki == nq - 1)
        def _():
            for e in range(hp):
                sl = slice(e * LANES, (e + 1) * LANES)
                slab = dqt_ref[e * LANES + AUG:e * LANES + AUG + 8, :]
                dcq_ref[e] = jnp.sum(jnp.where(row8 == 0, slab, 0.0), axis=0, keepdims=True)
                for j in range(nq):
                    dq_ref[j * t:(j + 1) * t, sl] = dqt_ref[sl, j * t:(j + 1) * t].T

    tile_spec = pl.BlockSpec((t, w), lambda hd, i: (i, hd))
    full = pl.BlockSpec((m, w), lambda hd, i: (0, hd))
    stat = pl.BlockSpec((hp, 1, m), lambda hd, i: (hd, 0, 0))
    (dkf, dvf, dqf, dck, dcq), carried = rider_call(
        body, name, (8 // hp, nq),
        in_specs=[tile_spec, tile_spec, pl.BlockSpec((w, t), lambda hd, i: (hd, i)), full, full, stat, stat],
        out_specs=[tile_spec, tile_spec, full, pl.BlockSpec((hp, 1, t), lambda hd, i: (hd, 0, i)), stat],
        out_shape=[jax.ShapeDtypeStruct((m, 8 * LANES), F32), jax.ShapeDtypeStruct((m, 8 * LANES), F32),
                   jax.ShapeDtypeStruct((m, 8 * LANES), F32), jax.ShapeDtypeStruct((8, 1, m), F32),
                   jax.ShapeDtypeStruct((8, 1, m), F32)],
        scratch_shapes=[pltpu.VMEM((t, w), F32), pltpu.VMEM((t, w), F32), pltpu.VMEM((w, m), F32)],
        args=(kf, vf, kt, qf, dof, lse, delta), rider=rider)
    return (dkf, dvf, dqf, dcq + dck), carried


def _bucket_ids():
    def bucket(dist):
        n = np.maximum(dist, 0)
        max_exact = N_BUCKETS // 2
        nf = np.maximum(n, 1).astype(np.float32)
        large = max_exact + (np.log(nf / max_exact) / math.log(MAX_DISTANCE / max_exact)
                             * (N_BUCKETS - max_exact)).astype(np.int32)
        return np.where(n < max_exact, n, np.minimum(large, N_BUCKETS - 1))

    tl = np.arange(LANES)[:, None]
    sl = np.arange(LANES)[None, :]
    prev = bucket(LANES + tl - sl)
    cur = bucket(tl - sl)
    meta = np.full((LANES, LANES), N_BUCKETS - 1)
    return np.concatenate([prev, cur, meta], axis=1).astype(np.int32)


def bias_build(table, name):
    ids = jnp.asarray(_bucket_ids())

    def body(t_ref, id_ref, o_ref):
        idv = id_ref[...]
        for h in range(8):
            acc = jnp.zeros((LANES, 3 * LANES), F32)
            for b in range(N_BUCKETS):
                acc = jnp.where(idv == b, t_ref[b, h], acc)
            o_ref[h] = acc

    return pl.pallas_call(
        body, name=name,
        in_specs=[pl.BlockSpec(memory_space=pltpu.SMEM), pl.BlockSpec(memory_space=pltpu.VMEM)],
        out_specs=pl.BlockSpec(memory_space=pltpu.VMEM),
        out_shape=jax.ShapeDtypeStruct((8, LANES, 3 * LANES), F32),
    )(table, ids)


def bias_reduce(dbias, name):
    ids = jnp.asarray(_bucket_ids())

    def body(d_ref, id_ref, o_ref):
        idv = id_ref[...]
        rr = lax.broadcasted_iota(jnp.int32, (N_BUCKETS, LANES), 0)
        cc = lax.broadcasted_iota(jnp.int32, (N_BUCKETS, LANES), 1)
        acc = jnp.zeros((N_BUCKETS, LANES), F32)
        for h in range(8):
            dv = d_ref[h]
            for b in range(N_BUCKETS):
                val = jnp.sum(jnp.where(idv == b, dv, 0.0), keepdims=True)
                acc = jnp.where((rr == b) & (cc == h), val, acc)
        o_ref[...] = acc

    return pl.pallas_call(
        body, name=name,
        in_specs=[pl.BlockSpec(memory_space=pltpu.VMEM), pl.BlockSpec(memory_space=pltpu.VMEM)],
        out_specs=pl.BlockSpec(memory_space=pltpu.VMEM),
        out_shape=jax.ShapeDtypeStruct((N_BUCKETS, LANES), F32),
    )(dbias, ids)


def _swa_penalty(n):
    shape = (LANES, 3 * LANES)
    tl = lax.broadcasted_iota(jnp.int32, shape, 0)
    col = lax.broadcasted_iota(jnp.int32, shape, 1)
    sl = col & (LANES - 1)
    nv = jnp.full(shape, n, jnp.int32)
    is_meta = sl >= PAD_FRONT
    prev = (col < LANES) & (sl > tl) & (nv >= 1) & ((nv >= 2) | is_meta)
    cur = (col >= LANES) & (col < 2 * LANES) & (sl <= tl) & ((nv >= 1) | is_meta)
    meta = (col >= 2 * LANES) & is_meta & ((nv >= 2) | ((nv == 1) & (sl <= tl)))
    return jnp.where(prev | cur | meta, 0.0, NEG)


def _swa_keys(ref, n):
    off_prev = pl.multiple_of(jnp.maximum(n - 1, 0) * LANES, LANES)
    off_cur = pl.multiple_of(n * LANES, LANES)
    return jnp.concatenate([ref[pl.ds(off_prev, LANES), :], ref[pl.ds(off_cur, LANES), :], ref[0:LANES, :]], axis=0)


def swa_fwd(q, k, v, bias, sinks, name, rider=None):
    m = q.shape[0]

    def body(q_ref, k_ref, v_ref, bias_ref, sink_ref, o_ref, lse_ref):
        n = pl.program_id(0)
        lane1 = lax.broadcasted_iota(jnp.int32, (1, LANES), 1)
        lane_t = lax.broadcasted_iota(jnp.int32, (LANES, LANES), 1)
        in_head = [lane1 < HEAD_DIM, lane1 >= HEAD_DIM]
        kall = _swa_keys(k_ref, n)
        vall = _swa_keys(v_ref, n)
        vs = [jnp.where(in_head[g], vall, jnp.zeros_like(vall)) for g in (0, 1)]
        penalty = _swa_penalty(n)
        lse = jnp.zeros((LANES, LANES), F32)
        for b in range(4):
            qb = q_ref[:, b * LANES:(b + 1) * LANES]
            ob = jnp.zeros((LANES, LANES), F32)
            for g in (0, 1):
                h = 4 * g + b
                qe = jnp.where(in_head[g], qb, jnp.zeros_like(qb))
                s = _dot_nt(qe, kall) + bias_ref[h] + penalty
                sink = sink_ref[h]
                mx = jnp.maximum(jnp.max(s, axis=1, keepdims=True), sink)
                p = jnp.exp(s - mx)
                den = jnp.sum(p, axis=1, keepdims=True) + jnp.exp(sink - mx)
                ob = ob + _dot((p / den).astype(BF16), vs[g])
                lse = jnp.where(lane_t == h, mx + jnp.log(den), lse)
            o_ref[:, b * LANES:(b + 1) * LANES] = ob
        lse_ref[...] = lse

    return rider_call(
        body, name, (m // LANES,),
        in_specs=[pl.BlockSpec((LANES, 512), lambda n: (n, 0)),
                  pl.BlockSpec((m, LANES), lambda n: (0, 0)), pl.BlockSpec((m, LANES), lambda n: (0, 0)),
                  pl.BlockSpec((8, LANES, 3 * LANES), lambda n: (0, 0, 0)),
                  pl.BlockSpec(memory_space=pltpu.SMEM)],
        out_specs=[pl.BlockSpec((LANES, 512), lambda n: (n, 0)), pl.BlockSpec((LANES, LANES), lambda n: (n, 0))],
        out_shape=[jax.ShapeDtypeStruct((m, 512), F32), jax.ShapeDtypeStruct((m, LANES), F32)],
        scratch_shapes=[], args=(q, k, v, bias, sinks), rider=rider)


def swa_bwd(q, k, v, bias, sinks, o, lse, do, name):
    m = q.shape[0]

    def body(q_ref, do_ref, o_ref, lse_ref, k_ref, v_ref, bias_ref, sink_ref,
             dq_ref, dk_ref, dv_ref, dbias_ref, dsink_ref):
        n = pl.program_id(0)

        @pl.when(n == 0)
        def _():
            for r in (dk_ref, dv_ref, dbias_ref, dsink_ref):
                r[...] = jnp.zeros_like(r)

        lane1 = lax.broadcasted_iota(jnp.int32, (1, LANES), 1)
        lane_t = lax.broadcasted_iota(jnp.int32, (LANES, LANES), 1)
        in_head = [lane1 < HEAD_DIM, lane1 >= HEAD_DIM]
        off_prev = pl.multiple_of(jnp.maximum(n - 1, 0) * LANES, LANES)
        off_cur = pl.multiple_of(n * LANES, LANES)
        kall = _swa_keys(k_ref, n)
        vall = _swa_keys(v_ref, n)
        ks = [jnp.where(in_head[g], kall, jnp.zeros_like(kall)) for g in (0, 1)]
        penalty = _swa_penalty(n)
        lsev = lse_ref[...]
        dsink = dsink_ref[...]
        dkall = jnp.zeros((3 * LANES, LANES), F32)
        dvall = jnp.zeros((3 * LANES, LANES), F32)
        for b in range(4):
            sl = slice(b * LANES, (b + 1) * LANES)
            qb = q_ref[:, sl]
            dob = do_ref[:, sl]
            prod = dob * o_ref[:, sl]
            dqb = jnp.zeros((LANES, LANES), F32)
            for g in (0, 1):
                h = 4 * g + b
                qe = jnp.where(in_head[g], qb, jnp.zeros_like(qb))
                doe = jnp.where(in_head[g], dob, 0.0).astype(BF16)
                delta = jnp.sum(jnp.where(in_head[g], prod, 0.0), axis=1, keepdims=True)
                lse_h = _lane_col(lsev, lane_t, h)
                s = _dot_nt(qe, kall) + bias_ref[h] + penalty
                p = jnp.exp(s - lse_h)
                ds = p * (_dot_nt(doe, vall) - delta)
                dbias_ref[h] += ds
                sink_part = jnp.sum(-jnp.exp(sink_ref[h] - lse_h) * delta, keepdims=True)
                dsink = jnp.where(lane1 == h, dsink + sink_part, dsink)
                dsb = ds.astype(BF16)
                dqb = dqb + _dot(dsb, ks[g])
                dkall = dkall + _dot_tn(dsb, qe)
                dvall = dvall + _dot_tn(p.astype(BF16), doe)
            dq_ref[:, sl] = dqb
        dsink_ref[...] = dsink
        for ref, val in ((dk_ref, dkall), (dv_ref, dvall)):
            ref[pl.ds(off_prev, LANES), :] += val[0:LANES]
            ref[pl.ds(off_cur, LANES), :] += val[LANES:2 * LANES]
            ref[0:LANES, :] += val[2 * LANES:3 * LANES]

    blk = pl.BlockSpec((LANES, 512), lambda n: (n, 0))
    full = pl.BlockSpec((m, LANES), lambda n: (0, 0))
    return pl.pallas_call(
        body, name=name, grid=(m // LANES,),
        in_specs=[blk, blk, blk, pl.BlockSpec((LANES, LANES), lambda n: (n, 0)), full, full,
                  pl.BlockSpec((8, LANES, 3 * LANES), lambda n: (0, 0, 0)),
                  pl.BlockSpec(memory_space=pltpu.SMEM)],
        out_specs=[blk, full, full, pl.BlockSpec((8, LANES, 3 * LANES), lambda n: (0, 0, 0)),
                   pl.BlockSpec((1, LANES), lambda n: (0, 0))],
        out_shape=[jax.ShapeDtypeStruct((m, 512), F32), jax.ShapeDtypeStruct((m, LANES), F32),
                   jax.ShapeDtypeStruct((m, LANES), F32), jax.ShapeDtypeStruct((8, LANES, 3 * LANES), F32),
                   jax.ShapeDtypeStruct((1, LANES), F32)],
        compiler_params=_params(1),
    )(q, do, o, lse, k, v, bias, sinks)


def branch_out(h, o_fox, o_swa, proj, wbf, wbs, wo, cols, name):
    m, d = h.shape
    tm = _row_tile(m)

    def body(h_ref, of_ref, os_ref, ga_ref, gb_ref, wbf_ref, wbs_ref, wo_ref, hn_ref):
        tf = _dot(of_ref[...].astype(BF16), wbf_ref[...])
        ts = _dot(os_ref[...].astype(BF16), wbs_ref[...])
        y = jax.nn.sigmoid(ga_ref[...]) * tf + jax.nn.sigmoid(gb_ref[...]) * ts
        hn_ref[...] = h_ref[...] + _dot(y.astype(BF16), wo_ref[...])

    row = lambda w, o=0: pl.BlockSpec((tm, w), lambda r, o=o: (r, o))
    res = lambda a: pl.BlockSpec(a.shape, lambda r: (0, 0))
    return pl.pallas_call(
        body, name=name, grid=(m // tm,),
        in_specs=[row(d), row(1024), row(512), row(d, cols.ga // d), row(d, cols.gb // d), res(wbf), res(wbs), res(wo)],
        out_specs=row(d),
        out_shape=jax.ShapeDtypeStruct((m, d), F32),
        compiler_params=_params(1),
    )(h, o_fox, o_swa, proj, proj, wbf, wbs, wo)


def branch_out_bwd(dh, o_fox, o_swa, proj, wbf, wbs, wo, cols, name):
    m, d = dh.shape
    tm = _row_tile(m)

    def body(dh_ref, of_ref, os_ref, ga_ref, gb_ref, wbf_ref, wbs_ref, wo_ref,
             y_ref, dtf_ref, dts_ref, dga_ref, dgb_ref, dof_ref, dos_ref, delta_ref):
        dy = _dot_nt(dh_ref[...].astype(BF16), wo_ref[...])
        tf = _dot(of_ref[...].astype(BF16), wbf_ref[...])
        ts = _dot(os_ref[...].astype(BF16), wbs_ref[...])
        sa = jax.nn.sigmoid(ga_ref[...])
        sb = jax.nn.sigmoid(gb_ref[...])
        y_ref[...] = (sa * tf + sb * ts).astype(BF16)
        dtf = (dy * sa).astype(BF16)
        dts = (dy * sb).astype(BF16)
        dtf_ref[...] = dtf
        dts_ref[...] = dts
        dga_ref[...] = (dy * tf * sa * (1.0 - sa)).astype(BF16)
        dgb_ref[...] = (dy * ts * sb * (1.0 - sb)).astype(BF16)
        dof = _dot_nt(dtf, wbf_ref[...])
        dof_ref[...] = dof.astype(BF16)
        dos_ref[...] = _dot_nt(dts, wbs_ref[...])
        lane = lax.broadcasted_iota(jnp.int32, (tm, LANES), 1)
        delta = jnp.zeros((tm, LANES), F32)
        for hd in range(8):
            sl = slice(hd * LANES, (hd + 1) * LANES)
            delta = jnp.where(lane == hd, jnp.sum(dof[:, sl] * of_ref[:, sl], axis=1, keepdims=True), delta)
        delta_ref[...] = delta

    row = lambda w, o=0: pl.BlockSpec((tm, w), lambda r, o=o: (r, o))
    res = lambda a: pl.BlockSpec(a.shape, lambda r: (0, 0))
    return pl.pallas_call(
        body, name=name, grid=(m // tm,),
        in_specs=[row(d), row(1024), row(512), row(d, cols.ga // d), row(d, cols.gb // d), res(wbf), res(wbs), res(wo)],
        out_specs=[row(d)] * 5 + [row(1024), row(512), row(LANES)],
        out_shape=[jax.ShapeDtypeStruct((m, d), BF16)] * 5 + [jax.ShapeDtypeStruct((m, 1024), BF16),
                   jax.ShapeDtypeStruct((m, 512), F32), jax.ShapeDtypeStruct((m, LANES), F32)],
        compiler_params=_params(1),
    )(dh, o_fox, o_swa, proj, proj, wbf, wbs, wo)


def loss_head(h, target, name):
    m, d = h.shape

    def body(h_ref, t_ref, dh_ref, loss_ref):
        n = pl.program_id(0)

        @pl.when(n == 0)
        def _():
            loss_ref[...] = jnp.zeros_like(loss_ref)
            dh_ref[...] = jnp.zeros_like(dh_ref)

        @pl.when(n > 0)
        def _():
            err = h_ref[...] - t_ref[...]
            dh_ref[...] = err * (1.0 / d)
            loss_ref[...] += jnp.sum(err * err, keepdims=True) * (0.5 / d)

    return pl.pallas_call(
        body, name=name, grid=(m // LANES,),
        in_specs=[pl.BlockSpec((LANES, d), lambda n: (n, 0)),
                  pl.BlockSpec((LANES, d), lambda n: (jnp.maximum(n - 1, 0), 0))],
        out_specs=[pl.BlockSpec((LANES, d), lambda n: (n, 0)), pl.BlockSpec((8, LANES), lambda n: (0, 0))],
        out_shape=[jax.ShapeDtypeStruct((m, d), F32), jax.ShapeDtypeStruct((8, LANES), F32)],
        compiler_params=_params(1),
    )(h, target)


def _adamw_math(w, g, m, v):
    m = ADAM_B1 * m + (1.0 - ADAM_B1) * g
    v = ADAM_B2 * v + (1.0 - ADAM_B2) * (g * g)
    m_hat = m / (1.0 - ADAM_B1 ** ADAM_STEP)
    v_hat = v / (1.0 - ADAM_B2 ** ADAM_STEP)
    delta = -ADAM_LR * (m_hat / (jnp.sqrt(v_hat) + ADAM_EPS) + ADAM_WD * w)
    return delta, m, v


def adamw_sum(parts, w, m, v, name, after=None):
    n_layers, a, b = w.shape
    ta = next(t for t in (256, 176, 128, a) if a % t == 0)
    nr = a // ta

    def body(*refs):
        p_refs = refs[:n_layers]
        w_ref, m_ref, v_ref = refs[n_layers:n_layers + 3]
        g_o, d_o, m_o, v_o = refs[-4:]
        for l in range(n_layers):
            @pl.when(pl.program_id(0) == l)
            def _(l=l):
                g = p_refs[l][0].astype(F32)
                for j in range(1, N_DEV):
                    g = g + p_refs[l][j].astype(F32)
                g_o[0] = g
                d_o[0], m_o[0], v_o[0] = _adamw_math(w_ref[0], g, m_ref[0], v_ref[0])

    def part_spec(l):
        return pl.BlockSpec((N_DEV, ta, b), lambda i, r, l=l: (0, jnp.where(i == l, r, jnp.where(i < l, 0, nr - 1)), 0))

    row = pl.BlockSpec((1, ta, b), lambda i, r: (i, r, 0))
    return pl.pallas_call(
        body, name=name, grid=(n_layers, nr),
        in_specs=[part_spec(l) for l in range(n_layers)] + [row, row, row]
                 + ([pl.BlockSpec(memory_space=pl.ANY)] if after is not None else []),
        out_specs=[row] * 4,
        out_shape=[jax.ShapeDtypeStruct(w.shape, F32)] * 4,
        compiler_params=_params(2),
    )(*parts, w, m, v, *([after] if after is not None else []))


def adamw_sum_cols(parts, w, m, v, name, after=None):
    n_layers = len(parts)
    a, b = parts[0].shape[1:]
    tc = 512 if b % 512 == 0 else b
    nc = b // tc

    def body(*refs):
        p_refs = refs[:n_layers]
        w_ref, m_ref, v_ref = refs[n_layers:n_layers + 3]
        g_o, d_o, m_o, v_o = refs[-4:]
        for l in range(n_layers):
            @pl.when(pl.program_id(0) == l)
            def _(l=l):
                g = p_refs[l][0].astype(F32)
                for j in range(1, N_DEV):
                    g = g + p_refs[l][j].astype(F32)
                g_o[...] = g
                d_o[...], m_o[...], v_o[...] = _adamw_math(w_ref[...], g, m_ref[...], v_ref[...])

    def part_spec(l):
        return pl.BlockSpec((N_DEV, a, tc), lambda i, c, l=l: (0, 0, jnp.where(i == l, c, jnp.where(i < l, 0, nc - 1))))

    col = pl.BlockSpec((a, tc), lambda i, c: (0, i * nc + c))
    return pl.pallas_call(
        body, name=name, grid=(n_layers, nc),
        in_specs=[part_spec(l) for l in range(n_layers)] + [col, col, col]
                 + ([pl.BlockSpec(memory_space=pl.ANY)] if after is not None else []),
        out_specs=[col] * 4,
        out_shape=[jax.ShapeDtypeStruct(w.shape, F32)] * 4,
        compiler_params=_params(2),
    )(*parts, w, m, v, *([after] if after is not None else []))


def adamw_small(g, w, m, v, name):
    def body(g_ref, w_ref, m_ref, v_ref, d_o, m_o, v_o):
        d_o[...], m_o[...], v_o[...] = _adamw_math(w_ref[...], g_ref[...], m_ref[...], v_ref[...])

    spec = pl.BlockSpec(memory_space=pltpu.VMEM)
    return pl.pallas_call(
        body, name=name, in_specs=[spec] * 4, out_specs=[spec] * 3,
        out_shape=[jax.ShapeDtypeStruct(w.shape, F32)] * 3,
    )(g, w, m, v)


BIG = ("ffn1_w_in", "ffn1_w_out", "w_in", "w_branch_fox", "w_branch_swa", "w_out", "ffn2_w_in", "ffn2_w_out")
SMALL = ("rel_bias_table", "ffn1_norm", "mix_norm", "forget_bias", "fox_q_norm", "fox_k_norm",
         "swa_q_norm", "swa_k_norm", "swa_sinks", "ffn2_norm")
WEIGHTS = ("meta_tokens", "rel_bias_table", "ffn1_norm", "ffn1_w_in", "ffn1_w_out", "mix_norm", "w_in",
           "forget_bias", "fox_q_norm", "fox_k_norm", "swa_q_norm", "swa_k_norm", "swa_sinks", "w_branch_fox",
           "w_branch_swa", "w_out", "ffn2_norm", "ffn2_w_in", "ffn2_w_out")


def _pack(arrs, width, row_multiple, dtype):
    lead = arrs[0].shape[:-1]
    flat = jnp.concatenate([a.astype(dtype) for a in arrs], axis=-1)
    n = flat.shape[-1]
    rows = -(-n // width)
    rows = -(-rows // row_multiple) * row_multiple
    flat = jnp.pad(flat, [(0, 0)] * len(lead) + [(0, rows * width - n)])
    return flat.reshape(lead + (rows, width))


def _unpack(flat, shapes):
    flat = flat.reshape(-1)
    out, off = [], 0
    for s in shapes:
        n = int(np.prod(s))
        out.append(flat[off:off + n].reshape(s))
        off += n
    return out


def _swa_head_order():
    return [4 * (j % 2) + j // 2 for j in range(8)]


def _permute_heads(a, axis, inverse=False):
    order = _swa_head_order()
    if inverse:
        order = [order.index(hd) for hd in range(8)]
    parts = [lax.slice_in_dim(a, hd * HEAD_DIM, (hd + 1) * HEAD_DIM, axis=axis) for hd in order]
    return jnp.concatenate(parts, axis=axis)


def _pad_heads(a):
    return jnp.pad(a.reshape(8, HEAD_DIM, -1), ((0, 0), (0, LANES - HEAD_DIM), (0, 0))).reshape(8 * LANES, -1)


def _unpad_heads(a):
    return a.reshape(8, LANES, -1)[:, :HEAD_DIM].reshape(8 * HEAD_DIM, -1)


def _swa_rows(a, inverse=False):
    shape = (4, 2) if inverse else (2, 4)
    return a.reshape(shape + (HEAD_DIM, -1)).transpose(1, 0, 2, 3).reshape(a.shape)


def _w_in_rows(d):
    return np.cumsum([0, 512, 512, 512, 8, 512, 128, 128, d, d])


def _reorder_w_in(wt, cols):
    o = _w_in_rows(cols.d)
    qa, ka, va, fa, qb, kb, vb, ga, gb = [wt[o[i]:o[i + 1]] for i in range(9)]
    zeros = jnp.zeros((cols.np - cols.fa - 8, wt.shape[1]), wt.dtype)
    return jnp.concatenate([ga, gb, _pad_heads(qa), _pad_heads(ka), _pad_heads(va), _swa_rows(qb), kb, vb, fa, zeros],
                           axis=0)


def _restore_w_in(wpt, cols, width):
    seg = lambda off, n: wpt[off:off + n]
    rows = jnp.concatenate([_unpad_heads(seg(cols.qa, 1024)), _unpad_heads(seg(cols.ka, 1024)),
                            _unpad_heads(seg(cols.va, 1024)), seg(cols.fa, 8), _swa_rows(seg(cols.qb, 512), True),
                            seg(cols.kb, 128), seg(cols.vb, 128), seg(cols.ga, cols.d), seg(cols.gb, cols.d)], axis=0)
    return rows.reshape(N_DEV, width, -1)


def _lane_pad(v):
    return jnp.pad(v, ((0, 0), (0, LANES - v.shape[1])))


def kernel(x, meta_tokens, rel_bias_table, ffn1_norm, ffn1_w_in, ffn1_w_out, mix_norm, w_in, forget_bias, fox_q_norm, fox_k_norm, swa_q_norm, swa_k_norm, swa_sinks, w_branch_fox, w_branch_swa, w_out, ffn2_norm, ffn2_w_in, ffn2_w_out, loss_target, m_meta_tokens, m_rel_bias_table, m_ffn1_norm, m_ffn1_w_in, m_ffn1_w_out, m_mix_norm, m_w_in, m_forget_bias, m_fox_q_norm, m_fox_k_norm, m_swa_q_norm, m_swa_k_norm, m_swa_sinks, m_w_branch_fox, m_w_branch_swa, m_w_out, m_ffn2_norm, m_ffn2_w_in, m_ffn2_w_out, v_meta_tokens, v_rel_bias_table, v_ffn1_norm, v_ffn1_w_in, v_ffn1_w_out, v_mix_norm, v_w_in, v_forget_bias, v_fox_q_norm, v_fox_k_norm, v_swa_q_norm, v_swa_k_norm, v_swa_sinks, v_w_branch_fox, v_w_branch_swa, v_w_out, v_ffn2_norm, v_ffn2_w_in, v_ffn2_w_out):
    args = dict(locals())
    wts = {n: args[n] for n in WEIGHTS}
    mom1 = {n: args["m_" + n] for n in WEIGHTS}
    mom2 = {n: args["v_" + n] for n in WEIGHTS}

    seq, d = x.shape[1], x.shape[2]
    m_rows = seq + LANES
    depth = ffn1_norm.shape[0]
    fb = ffn1_w_in.shape[2]
    fo = ffn1_w_out.shape[1]
    din_shard = w_in.shape[2]
    cols = _Cols(d)
    scale = HEAD_DIM ** -0.5
    dev = 4 * lax.axis_index("x") + 2 * lax.axis_index("y") + lax.axis_index("c")

    groups = {"ffn1": ("ffn1_w_in", "ffn1_w_out"), "mix": ("w_in", "w_branch_fox", "w_branch_swa", "w_out"),
              "ffn2": ("ffn2_w_in", "ffn2_w_out"), "ffn1_in": ("ffn1_w_in",), "ffn1_out": ("ffn1_w_out",),
              "ffn2_in": ("ffn2_w_in",), "ffn2_out": ("ffn2_w_out",)}
    flipped = ("ffn1_w_in", "ffn2_w_in")
    for n in flipped:
        wts[n], mom1[n], mom2[n] = (jnp.swapaxes(a, 1, 2) for a in (wts[n], mom1[n], mom2[n]))
    to_rows = lambda a: jnp.transpose(a, (2, 0, 1)).reshape(din_shard, depth * d)
    from_rows = lambda a: jnp.transpose(a.reshape(din_shard, depth, d), (1, 2, 0))
    wts["w_in"], mom1["w_in"], mom2["w_in"] = (to_rows(a) for a in (wts["w_in"], mom1["w_in"], mom2["w_in"]))
    shard = {n: wts[n].astype(BF16) for n in BIG}
    w_in_rows = shard.pop("w_in")
    shard["w_in"] = [w_in_rows[:, l * d:(l + 1) * d] for l in range(depth)]
    full, parts, gw = {}, {}, {}

    def keys_of(stages):
        return [(n, l) for g, l in stages if l < depth for n in groups[g]]

    def gather_rider(stages):
        return Rider([shard[n][l] for n, l in keys_of(stages)], True)

    def scatter_rider(stages):
        return Rider([gw[k] for k in keys_of(stages)], False)

    def ffn_weights(tag, l):
        return full[tag + "_w_in", l], full[tag + "_w_out", l].reshape(4, fb, d)

    def mixer_weights(l):
        wp = _reorder_w_in(full["w_in", l].reshape(N_DEV * din_shard, d), cols)
        wbf = jnp.concatenate([full["w_branch_fox", l][j] for j in range(N_DEV)], axis=1)
        wbf = jnp.pad(wbf.reshape(8, HEAD_DIM, d), ((0, 0), (0, LANES - HEAD_DIM), (0, 0))).reshape(8 * LANES, d)
        wbs = _permute_heads(jnp.concatenate([full["w_branch_swa", l][j] for j in range(N_DEV)], axis=1), 0)
        return wp, wbf, wbs, full["w_out", l].reshape(d, d)

    full.update(zip(keys_of([("ffn1", 0)]), exchange_hbm(gather_rider([("ffn1", 0)]).srcs, True, "gather_first")))
    meta_all = gather_small(meta_tokens.reshape(1, N_META, -1), "gather_meta")
    meta_full = meta_all.transpose(1, 0, 2).reshape(N_META, d)
    tile8 = lambda g, s=1.0: jnp.tile(g.reshape(1, HEAD_DIM) * s, (1, 8))
    tile2 = lambda g: jnp.tile(g.reshape(1, HEAD_DIM), (1, 2))
    data_lanes = lambda g, s=1.0: _lane_pad(g.reshape(1, HEAD_DIM) * s)
    bias = bias_build(rel_bias_table, "swa_bias")

    first = jnp.concatenate([jnp.zeros((PAD_FRONT, d), F32), meta_full], axis=0)
    h = jnp.concatenate([first, x[0]], axis=0)
    saved, lw = [], []
    for l in range(depth):
        s, w = {"h0": h}, {}
        w["ffn1_in"], w["ffn1_out"] = ffn_weights("ffn1", l)
        stages = [("mix", l)]
        (h, s["n1"], s["a1"], s["fg1"], s["fu1"]), got = ffn_fwd(h, ffn1_norm[l:l + 1], w["ffn1_in"], w["ffn1_out"],
                                                          f"ffn1_fwd_{l}", gather_rider(stages))
        full.update(zip(keys_of(stages), got))
        s["h1"] = h
        w["wp"], w["wbf"], w["wbs"], w["wo"] = mixer_weights(l)
        s["nm"], s["proj"] = mixer_proj(h, mix_norm[l:l + 1], w["wp"], f"mixer_proj_{l}")
        s["gains"] = (data_lanes(fox_q_norm[l], scale), data_lanes(fox_k_norm[l]), tile8(swa_q_norm[l], scale),
                      tile2(swa_k_norm[l]))
        s["fbias"] = _lane_pad(forget_bias[l:l + 1])
        qf, kf, vf, kt, vt, qb, kb, vb = qk_post(s["proj"], s["gains"], s["fbias"], cols, f"qk_post_{l}")
        s.update(qf=qf, kf=kf, vf=vf, kt=kt, qb=qb, kb=kb, vb=vb)
        stages = [("ffn2", l)]
        (s["o_fox"], s["lse_fox"]), got = fox_fwd(qf, kf, vt, f"fox_fwd_{l}", gather_rider(stages))
        full.update(zip(keys_of(stages), got))
        stages = [("ffn1_out", l + 1)]
        (s["o_swa"], s["lse_swa"]), got = swa_fwd(qb, kb, vb, bias, swa_sinks[l], f"swa_fwd_{l}", gather_rider(stages))
        full.update(zip(keys_of(stages), got))
        h = branch_out(h, s["o_fox"], s["o_swa"], s["proj"], w["wbf"], w["wbs"], w["wo"], cols, f"branch_out_{l}")
        s["h2"] = h
        w["ffn2_in"], w["ffn2_out"] = ffn_weights("ffn2", l)
        stages = [("ffn1_in", l + 1)]
        (h, s["n2"], s["a2"], s["fg2"], s["fu2"]), got = ffn_fwd(h, ffn2_norm[l:l + 1], w["ffn2_in"], w["ffn2_out"],
                                                          f"ffn2_fwd_{l}", gather_rider(stages))
        full.update(zip(keys_of(stages), got))
        saved.append(s)
        lw.append(w)

    dh, loss_part = loss_head(h, loss_target[0], "loss_head")

    gs = {n: [None] * depth for n in SMALL}
    dbias_total = None
    for l in reversed(range(depth)):
        w, s = lw[l], saved[l]

        def ffn_back(dh, tag, hin, norm, n_in, a, f_gate, f_up, stages):
            (dh_in, dg, du, dgn, dhs), got = ffn_bwd(dh, hin, norm, f_gate, f_up, w[tag + "_in"], w[tag + "_out"],
                                                     f"{tag}_bwd_{l}", scatter_rider(stages))
            parts.update(zip(keys_of(stages), got))
            gw[tag + "_w_out", l] = matmul_tn(a, dhs[None], f"{tag}_dwo_{l}").reshape(N_DEV, fo, d)
            stages = [(tag + "_out", l)]
            gw[tag + "_w_in", l], got = matmul_tn(dg, n_in[None], f"{tag}_dwi_{l}", x2=du,
                                                  rider=scatter_rider(stages))
            parts.update(zip(keys_of(stages), got))
            return dh_in, dgn

        dh, gs["ffn2_norm"][l] = ffn_back(dh, "ffn2", s["h2"], ffn2_norm[l:l + 1], s["n2"], s["a2"], s["fg2"],
                                          s["fu2"], [("ffn1_in", l + 1)])

        y, dtf, dts, dga, dgb, dof, dos, delta = branch_out_bwd(dh, s["o_fox"], s["o_swa"], s["proj"], w["wbf"],
                                                                w["wbs"], w["wo"], cols, f"branch_out_bwd_{l}")
        gw["w_out", l] = matmul_tn(y[None], dh[None], f"dw_out_{l}").reshape(N_DEV, d // N_DEV, d)
        to_shards = lambda a: a.reshape(512, N_DEV, d // N_DEV).transpose(1, 0, 2)
        gw["w_branch_fox", l] = to_shards(matmul_tn(s["o_fox"][None], dtf[None], f"dw_branch_fox_{l}")[0]
                                          .reshape(8, LANES, d)[:, :HEAD_DIM].reshape(512, d))
        gw["w_branch_swa", l] = to_shards(_permute_heads(
            matmul_tn(s["o_swa"][None], dts[None], f"dw_branch_swa_{l}")[0], 0, inverse=True))

        stages = [("ffn2_in", l)]
        (dkf, dvf, dqf, dc_rows), got = fox_bwd(s["qf"], s["kf"], s["vf"], s["kt"], dof, s["lse_fox"],
                                         delta[:, :8].T.reshape(8, 1, m_rows), f"fox_bwd_{l}", scatter_rider(stages))
        parts.update(zip(keys_of(stages), got))
        dc = _lane_pad(dc_rows.reshape(8, m_rows).T)
        dqb, dkb, dvb, dbias, dsink = swa_bwd(s["qb"], s["kb"], s["vb"], bias, swa_sinks[l], s["o_swa"], s["lse_swa"],
                                              dos, f"swa_bwd_{l}")
        dbias_total = dbias if dbias_total is None else dbias_total + dbias
        gs["swa_sinks"][l] = dsink[0, :8]
        dproj, ggqa, ggka, ggqb, ggkb, gfb = qk_post_bwd(s["proj"], s["gains"], s["fbias"], dqf, dkf, dvf, dqb, dkb,
                                                         dvb, dc, dga, dgb, cols, f"qk_post_bwd_{l}")
        gs["fox_q_norm"][l] = ggqa[0, :HEAD_DIM] * scale
        gs["fox_k_norm"][l] = ggka[0, :HEAD_DIM]
        gs["swa_q_norm"][l] = ggqb.reshape(8, HEAD_DIM).sum(0) * scale
        gs["swa_k_norm"][l] = ggkb.reshape(2, HEAD_DIM).sum(0)
        gs["forget_bias"][l] = gfb[0, :8]
        dwp = matmul_tn(dproj[None], s["nm"][None], f"dw_in_{l}", tk=1024 if cols.np % 1024 == 0 else cols.np)[0]
        gw["w_in", l] = _restore_w_in(dwp, cols, din_shard)
        dh, gs["mix_norm"][l] = dproj_bwd(dh, s["h1"], mix_norm[l:l + 1], dproj, w["wp"], f"dproj_bwd_{l}")

        dh, gs["ffn1_norm"][l] = ffn_back(dh, "ffn1", s["h0"], ffn1_norm[l:l + 1], s["n1"], s["a1"], s["fg1"],
                                          s["fu1"], [("mix", l)])

    grad_x = dh[LANES:][None]
    dmeta = dh[PAD_FRONT:LANES]
    dtable = bias_reduce(dbias_total, "swa_dbias")[:, :8]

    last = ("ffn1_w_in", 0)
    send_sems, recv_sems, src_thru, land_thru, token = scatter_start(gw[last], "scatter_last_start")
    big_out = [{}, {}, {}, {}]
    for n in BIG:
        if n != last[0]:
            update = adamw_sum_cols if n == "w_in" else adamw_sum
            outs = update([parts[n, l] for l in range(depth)], wts[n], mom1[n], mom2[n], f"adamw_{n}", after=token)
            for k in range(4):
                big_out[k][n] = outs[k]
    sent, landed = scatter_wait(send_sems, recv_sems, src_thru, land_thru,
                                [big_out[1][n] for n in BIG if n != last[0]], "scatter_last_wait")
    parts[last] = lax.dynamic_update_slice_in_dim(landed, lax.dynamic_slice_in_dim(sent, dev, 1, axis=0), dev, axis=0)
    outs = adamw_sum([parts[last[0], l] for l in range(depth)], wts[last[0]], mom1[last[0]], mom2[last[0]],
                     f"adamw_{last[0]}")
    for k in range(4):
        big_out[k][last[0]] = outs[k]

    small_g = {n: (jnp.stack(gs[n]) if n != "rel_bias_table" else None) for n in SMALL}
    small_g["rel_bias_table"] = dtable
    pieces = [loss_part[0:1, 0:1].reshape(1, 1)] + [small_g[n].reshape(1, -1) for n in SMALL] + [dmeta.reshape(1, -1)]
    small_shapes = [(1,)] + [wts[n].shape for n in SMALL] + [(N_META, d)]
    total = allsum_small(_pack(pieces, LANES, 8, F32), "allsum_small")
    summed = _unpack(total, small_shapes)
    loss = summed[0][0]
    g_small = dict(zip(SMALL, summed[1:1 + len(SMALL)]))
    g_meta = lax.dynamic_slice_in_dim(summed[-1], dev * (d // N_DEV), d // N_DEV, axis=1)
    names = SMALL + ("meta_tokens",)
    g_small["meta_tokens"] = g_meta
    pk = lambda src: _pack([src[n].reshape(1, -1) for n in names], LANES, 8, F32)[0]
    small_out = [dict(zip(names, _unpack(o, [wts[n].shape for n in names])))
                 for o in adamw_small(pk(g_small), pk(wts), pk(mom1), pk(mom2), "adamw_small")]

    for out in big_out:
        for n in flipped:
            out[n] = jnp.swapaxes(out[n], 1, 2)
        out["w_in"] = from_rows(out["w_in"])
    grads = {**big_out[0], **g_small}
    delta = {**big_out[1], **small_out[0]}
    new_m = {**big_out[2], **small_out[1]}
    new_v = {**big_out[3], **small_out[2]}
    return (loss, grad_x, *[grads[n] for n in WEIGHTS], *[delta[n] for n in WEIGHTS],
            *[new_m[n] for n in WEIGHTS], *[new_v[n] for n in WEIGHTS])
```

```python
import math

import numpy as np
import jax
import jax.numpy as jnp
from jax import lax
from jax.experimental import pallas as pl
from jax.experimental.pallas import tpu as pltpu

F32 = jnp.float32
BF16 = jnp.bfloat16
EPS = 1e-6
NEG = -1e30
HEAD_DIM = 64
LANES = 128
N_META = 16
PAD_FRONT = LANES - N_META
N_BUCKETS = 32
MAX_DISTANCE = 128
N_DEV = 8
ADAM_LR, ADAM_B1, ADAM_B2, ADAM_EPS, ADAM_WD, ADAM_STEP = 0.001, 0.9, 0.999, 1e-08, 0.01, 10
VMEM_LIMIT = 56 * 1024 * 1024
MESH = pl.DeviceIdType.MESH


def _params(n_grid):
    return pltpu.CompilerParams(dimension_semantics=("arbitrary",) * n_grid,
                                vmem_limit_bytes=VMEM_LIMIT)


def _dot(a, b):
    return jnp.dot(a, b, preferred_element_type=F32)


def _dot_nt(a, b):
    return lax.dot_general(a, b, (((1,), (1,)), ((), ())), preferred_element_type=F32)


def _dot_tn(a, b):
    return lax.dot_general(a, b, (((0,), (0,)), ((), ())), preferred_element_type=F32)


def _rms(x):
    r = lax.rsqrt(jnp.mean(x * x, axis=-1, keepdims=True) + EPS)
    return x * r, r


def _rms_bwd(x, g, dn):
    xh, r = _rms(x)
    dxh = dn * g
    dx = r * (dxh - xh * jnp.mean(dxh * xh, axis=-1, keepdims=True))
    return dx, jnp.sum(dn * xh, axis=0, keepdims=True)


def _split2(v):
    hi = v.astype(BF16)
    return hi, (v - hi.astype(F32)).astype(BF16)


def _split3(v):
    hi = v.astype(BF16)
    r1 = v - hi.astype(F32)
    mid = r1.astype(BF16)
    return hi, mid, (r1 - mid.astype(F32)).astype(BF16)


def _group_ones():
    r = lax.broadcasted_iota(jnp.int32, (LANES, LANES), 0) // HEAD_DIM
    c = lax.broadcasted_iota(jnp.int32, (LANES, LANES), 1) // HEAD_DIM
    return jnp.where(r == c, 1.0, 0.0).astype(BF16)


def _group_mean(v, ones):
    hi, lo = _split2(v)
    return (_dot(hi, ones) + _dot(lo, ones)) * (1.0 / HEAD_DIM)


def _row_tile(m):
    return 384 if m % 384 == 0 else LANES


def _tile(m, cap):
    return max(t for t in range(16, cap + 1, 16) if m % t == 0)


def _peer(k):
    x, y, c = lax.axis_index("x"), lax.axis_index("y"), lax.axis_index("c")
    px = 1 - x if k & 4 else x
    py = 1 - y if k & 2 else y
    pc = 1 - c if k & 1 else c
    return (px, py, pc), 4 * px + 2 * py + pc


def _exchange_body(src_ref, dst_ref, send_sems, recv_sems, local_sem, bcast):
    x, y, c = lax.axis_index("x"), lax.axis_index("y"), lax.axis_index("c")
    me = 4 * x + 2 * y + c
    mine = pltpu.make_async_copy(src_ref.at[0 if bcast else me], dst_ref.at[me], local_sem)
    mine.start()
    sends = []
    for k in range(1, N_DEV):
        dev, idx = _peer(k)
        cp = pltpu.make_async_remote_copy(
            src_ref=src_ref.at[0 if bcast else idx], dst_ref=dst_ref.at[me],
            send_sem=send_sems.at[k - 1], recv_sem=recv_sems.at[k - 1],
            device_id=dev, device_id_type=MESH)
        cp.start()
        sends.append(cp)
    for k in range(1, N_DEV):
        dev, idx = _peer(k)
        pltpu.make_async_remote_copy(
            src_ref=src_ref.at[0], dst_ref=dst_ref.at[idx],
            send_sem=send_sems.at[k - 1], recv_sem=recv_sems.at[k - 1],
            device_id=dev, device_id_type=MESH).wait_recv()
    for cp in sends:
        cp.wait_send()
    mine.wait()


class Rider:
    FIRST = (1, 2, 4, 6)
    RELAYED = (2, 4, 6)

    def __init__(self, srcs=(), bcast=True):
        self.srcs, self.bcast, self.n = list(srcs), bcast, len(srcs)

    def out_shapes(self):
        return [jax.ShapeDtypeStruct(((N_DEV,) + s.shape) if self.bcast else s.shape, s.dtype) for s in self.srcs]

    def specs(self):
        return [pl.BlockSpec(memory_space=pl.ANY)] * self.n

    def scratch(self):
        if not self.n:
            return []
        return [pltpu.SemaphoreType.DMA((self.n * (N_DEV - 1),)), pltpu.SemaphoreType.DMA((self.n * (N_DEV - 1),)),
                pltpu.SemaphoreType.DMA((self.n,))]

    @staticmethod
    def _copy(src, dst, a, pair, dev, send_sems, recv_sems):
        sem = a * (N_DEV - 1) + pair - 1
        return pltpu.make_async_remote_copy(src_ref=src, dst_ref=dst, send_sem=send_sems.at[sem],
                                            recv_sem=recv_sems.at[sem], device_id=dev, device_id_type=MESH)

    def _first(self):
        return self.FIRST if self.bcast else range(1, N_DEV)

    def _own(self, s, d, a, local_sems):
        me = 4 * lax.axis_index("x") + 2 * lax.axis_index("y") + lax.axis_index("c")
        return pltpu.make_async_copy(s if self.bcast else s.at[me], d.at[me], local_sems.at[a]), me

    def start(self, src_refs, dst_refs, send_sems, recv_sems, local_sems):
        for a, (s, d) in enumerate(zip(src_refs, dst_refs)):
            own, me = self._own(s, d, a, local_sems)
            own.start()
            for k in self._first():
                dev, idx = _peer(k)
                self._copy(s if self.bcast else s.at[idx], d.at[me], a, k, dev, send_sems, recv_sems).start()

    def relay(self, src_refs, dst_refs, send_sems, recv_sems, local_sems):
        if not self.bcast:
            return
        sibling, _ = _peer(1)
        for a, d in enumerate(dst_refs):
            for k in self.RELAYED:
                dev, idx = _peer(k)
                self._copy(d.at[idx], d.at[idx], a, k, dev, send_sems, recv_sems).wait_recv()
                self._copy(d.at[idx], d.at[idx], a, k + 1, sibling, send_sems, recv_sems).start()

    def wait(self, src_refs, dst_refs, send_sems, recv_sems, local_sems):
        sibling, _ = _peer(1)
        for a, (s, d) in enumerate(zip(src_refs, dst_refs)):
            own, me = self._own(s, d, a, local_sems)
            for k in range(1, N_DEV):
                if not (self.bcast and k in self.RELAYED):
                    dev, idx = _peer(k)
                    self._copy(d.at[idx], d.at[idx], a, k, dev, send_sems, recv_sems).wait_recv()
            for k in self._first():
                dev, idx = _peer(k)
                self._copy(s if self.bcast else s.at[idx], d.at[me], a, k, dev, send_sems, recv_sems).wait_send()
            if self.bcast:
                for k in self.RELAYED:
                    dev, idx = _peer(k)
                    self._copy(d.at[idx], d.at[idx], a, k + 1, sibling, send_sems, recv_sems).wait_send()
            own.wait()


RELAY_AT = 7


def rider_call(core, name, grid, in_specs, out_specs, out_shape, scratch_shapes, args, rider=None):
    rider = rider or Rider()
    n_in, n_out, n_scr, nr = len(in_specs), len(out_specs), len(scratch_shapes), rider.n

    def body(*refs):
        ins, r_src = refs[:n_in], refs[n_in:n_in + nr]
        outs = refs[n_in + nr:n_in + nr + n_out]
        r_dst = refs[n_in + nr + n_out:n_in + 2 * nr + n_out]
        scr = refs[n_in + 2 * nr + n_out:n_in + 2 * nr + n_out + n_scr]
        sems = refs[n_in + 2 * nr + n_out + n_scr:]
        if nr:
            first, last, step, steps = True, True, 0, 1
            for ax, size in enumerate(grid):
                first = first & (pl.program_id(ax) == 0)
                last = last & (pl.program_id(ax) == size - 1)
                step = step * size + pl.program_id(ax)
                steps *= size
            relay = step == RELAY_AT * steps // 8
            if not grid:
                rider.start(r_src, r_dst, *sems)
                rider.relay(r_src, r_dst, *sems)
            else:
                pl.when(first)(lambda: rider.start(r_src, r_dst, *sems))
                if rider.bcast:
                    pl.when(relay)(lambda: rider.relay(r_src, r_dst, *sems))
        core(*ins, *outs, *scr)
        if nr:
            if not grid:
                rider.wait(r_src, r_dst, *sems)
            else:
                pl.when(last)(lambda: rider.wait(r_src, r_dst, *sems))

    res = pl.pallas_call(
        body, name=name, grid=grid,
        in_specs=list(in_specs) + rider.specs(),
        out_specs=list(out_specs) + rider.specs(),
        out_shape=list(out_shape) + rider.out_shapes(),
        scratch_shapes=list(scratch_shapes) + rider.scratch(),
        compiler_params=_params(len(grid)),
    )(*args, *rider.srcs)
    return res[:n_out], res[n_out:]


def exchange_hbm(srcs, bcast, name):
    return rider_call(lambda: None, name, (), [], [], [], [], [], Rider(srcs, bcast))[1]


_HBM = pl.BlockSpec(memory_space=pltpu.HBM)
_SEM = pl.BlockSpec(memory_space=pltpu.SEMAPHORE)
_EFFECT = pltpu.CompilerParams(has_side_effects=pltpu.SideEffectType.DATAFLOW_SIDE_EFFECTING)


def scatter_start(src, name):
    def body(src_ref, land_ref, send_sems, recv_sems, src_thru, land_thru, token):
        me = 4 * lax.axis_index("x") + 2 * lax.axis_index("y") + lax.axis_index("c")
        for k in range(1, N_DEV):
            dev, idx = _peer(k)
            pltpu.make_async_remote_copy(src_ref=src_ref.at[idx], dst_ref=land_ref.at[me], send_sem=send_sems.at[k - 1],
                                         recv_sem=recv_sems.at[k - 1], device_id=dev, device_id_type=MESH).start()
        token[...] = jnp.zeros_like(token)

    return pl.pallas_call(
        body, name=name,
        out_shape=(pltpu.SemaphoreType.DMA((N_DEV - 1,)), pltpu.SemaphoreType.DMA((N_DEV - 1,)),
                   pltpu.HBM(src.shape, src.dtype), pltpu.HBM(src.shape, src.dtype), jax.ShapeDtypeStruct((8, LANES), F32)),
        in_specs=(_HBM, _HBM), out_specs=(_SEM, _SEM, _HBM, _HBM, pl.BlockSpec(memory_space=pltpu.VMEM)),
        input_output_aliases={0: 2, 1: 3}, compiler_params=_EFFECT,
    )(pltpu.with_memory_space_constraint(src, pltpu.HBM),
      pltpu.with_memory_space_constraint(lax.empty(src.shape, src.dtype), pltpu.HBM))


def scatter_wait(send_sems, recv_sems, src_thru, land_thru, after, name):
    n_after = len(after)

    def body(*refs):
        src_ref, land_ref, send_sems, recv_sems = refs[:4]
        for k in range(1, N_DEV):
            dev, idx = _peer(k)
            copy = pltpu.make_async_remote_copy(src_ref=src_ref.at[idx], dst_ref=land_ref.at[idx],
                                                send_sem=send_sems.at[k - 1], recv_sem=recv_sems.at[k - 1],
                                                device_id=dev, device_id_type=MESH)
            copy.wait_send()
            copy.wait_recv()

    return pl.pallas_call(
        body, name=name,
        out_shape=(pltpu.HBM(src_thru.shape, src_thru.dtype), pltpu.HBM(land_thru.shape, land_thru.dtype)),
        in_specs=(_HBM, _HBM, _SEM, _SEM) + (pl.BlockSpec(memory_space=pl.ANY),) * n_after, out_specs=(_HBM, _HBM),
        input_output_aliases={0: 0, 1: 1}, compiler_params=_EFFECT,
    )(src_thru, land_thru, send_sems, recv_sems, *after)


def allsum_small(vec, name):
    def body(src_ref, out_ref, dst_ref, send_sems, recv_sems, local_sem):
        _exchange_body(src_ref, dst_ref, send_sems, recv_sems, local_sem, True)
        acc = dst_ref[0]
        for j in range(1, N_DEV):
            acc = acc + dst_ref[j]
        out_ref[...] = acc

    return pl.pallas_call(
        body, name=name,
        out_shape=jax.ShapeDtypeStruct(vec.shape[1:], F32),
        in_specs=[pl.BlockSpec(memory_space=pltpu.VMEM)],
        out_specs=pl.BlockSpec(memory_space=pltpu.VMEM),
        scratch_shapes=[pltpu.VMEM((N_DEV,) + vec.shape[1:], F32),
                        pltpu.SemaphoreType.DMA((N_DEV - 1,)), pltpu.SemaphoreType.DMA((N_DEV - 1,)),
                        pltpu.SemaphoreType.DMA],
    )(vec)


def gather_small(vec, name):
    def body(src_ref, dst_ref, send_sems, recv_sems, local_sem):
        _exchange_body(src_ref, dst_ref, send_sems, recv_sems, local_sem, True)

    return pl.pallas_call(
        body, name=name,
        out_shape=jax.ShapeDtypeStruct((N_DEV,) + vec.shape[1:], F32),
        in_specs=[pl.BlockSpec(memory_space=pltpu.VMEM)],
        out_specs=pl.BlockSpec(memory_space=pltpu.VMEM),
        scratch_shapes=[pltpu.SemaphoreType.DMA((N_DEV - 1,)), pltpu.SemaphoreType.DMA((N_DEV - 1,)),
                        pltpu.SemaphoreType.DMA],
    )(vec)


FFN_FWD_ROWS = 1056
FFN_BWD_ROWS = 704
FFN_BWD_CHUNKS = 4
DW_ROWS = 1408

def ffn_fwd(h, g, w_in8, w_out4, name, rider=None):
    m, d = h.shape
    fb = w_in8.shape[1]
    tm = _tile(m, FFN_FWD_ROWS)

    def body(h_ref, g_ref, wg_ref, wu_ref, wo_ref, hn_ref, n_ref, a_ref, fg_ref, fu_ref, acc_ref):
        i = pl.program_id(1)

        @pl.when(i == 0)
        def _():
            xh, _ = _rms(h_ref[...])
            n_ref[...] = (xh * g_ref[...]).astype(BF16)
            acc_ref[...] = jnp.zeros_like(acc_ref)

        n = n_ref[...]
        gate = _dot_nt(n, wg_ref[0])
        up = _dot_nt(n, wu_ref[0])
        sg = jax.nn.sigmoid(gate)
        silu = gate * sg
        a = (silu * up).astype(BF16)
        a_ref[0] = a
        fg_ref[0] = (up * (sg * (1.0 + gate * (1.0 - sg)))).astype(BF16)
        fu_ref[0] = silu.astype(BF16)
        acc_ref[...] += _dot(a, wo_ref[0])

        @pl.when(i == 3)
        def _():
            hn_ref[...] = h_ref[...] + 0.5 * acc_ref[...]

    return rider_call(
        body, name, (m // tm, 4),
        in_specs=[pl.BlockSpec((tm, d), lambda r, i: (r, 0)),
                  pl.BlockSpec((1, d), lambda r, i: (0, 0)),
                  pl.BlockSpec((1, fb, d), lambda r, i: (i, 0, 0)),
                  pl.BlockSpec((1, fb, d), lambda r, i: (i + 4, 0, 0)),
                  pl.BlockSpec((1, fb, d), lambda r, i: (i, 0, 0))],
        out_specs=[pl.BlockSpec((tm, d), lambda r, i: (r, 0)),
                   pl.BlockSpec((tm, d), lambda r, i: (r, 0))] + [pl.BlockSpec((1, tm, fb), lambda r, i: (i, r, 0))] * 3,
        out_shape=[jax.ShapeDtypeStruct((m, d), F32), jax.ShapeDtypeStruct((m, d), BF16)]
                  + [jax.ShapeDtypeStruct((4, m, fb), BF16)] * 3,
        scratch_shapes=[pltpu.VMEM((tm, d), F32)],
        args=(h, g, w_in8, w_in8, w_out4), rider=rider)


def ffn_bwd(dh, h, g, f_gate, f_up, w_in8, w_out4, name, rider=None):
    m, d = h.shape
    fb = w_in8.shape[1]
    tm = _tile(m, FFN_BWD_ROWS)

    def body(dh_ref, h_ref, g_ref, fg_ref, fu_ref, wg_ref, wu_ref, wo_ref,
             dhin_ref, dg_ref, du_ref, dgn_ref, dhs_ref, acc_ref):
        r = pl.program_id(0)
        i = pl.program_id(1)

        @pl.when(i == 0)
        def _():
            dhs_ref[...] = (0.5 * dh_ref[...]).astype(BF16)
            acc_ref[...] = jnp.zeros_like(acc_ref)

        @pl.when((r == 0) & (i == 0))
        def _():
            dgn_ref[...] = jnp.zeros_like(dgn_ref)

        chunks = FFN_BWD_CHUNKS if tm % (16 * FFN_BWD_CHUNKS) == 0 else 1
        for c in range(chunks):
            rows = slice(c * tm // chunks, (c + 1) * tm // chunks)
            da = _dot_nt(dhs_ref[rows, :], wo_ref[0])
            dub = (da * fu_ref[0, rows, :]).astype(BF16)
            dgb = (da * fg_ref[0, rows, :]).astype(BF16)
            dg_ref[0, rows, :] = dgb
            du_ref[0, rows, :] = dub
            acc_ref[rows, :] += _dot(dgb, wg_ref[0]) + _dot(dub, wu_ref[0])

        @pl.when(i == 3)
        def _():
            dx, dgain = _rms_bwd(h_ref[...], g_ref[...], acc_ref[...])
            dgn_ref[...] += dgain
            dhin_ref[...] = dh_ref[...] + dx

    row = lambda r, i: (r, 0)
    blk = lambda r, i: (i, r, 0)
    return rider_call(
        body, name, (m // tm, 4),
        in_specs=[pl.BlockSpec((tm, d), row), pl.BlockSpec((tm, d), row),
                  pl.BlockSpec((1, d), lambda r, i: (0, 0)),
                  pl.BlockSpec((1, tm, fb), blk), pl.BlockSpec((1, tm, fb), blk),
                  pl.BlockSpec((1, fb, d), lambda r, i: (i, 0, 0)),
                  pl.BlockSpec((1, fb, d), lambda r, i: (i + 4, 0, 0)),
                  pl.BlockSpec((1, fb, d), lambda r, i: (i, 0, 0))],
        out_specs=[pl.BlockSpec((tm, d), row),
                   pl.BlockSpec((1, tm, fb), blk), pl.BlockSpec((1, tm, fb), blk),
                   pl.BlockSpec((1, d), lambda r, i: (0, 0)),
                   pl.BlockSpec((tm, d), row)],
        out_shape=[jax.ShapeDtypeStruct((m, d), F32),
                   jax.ShapeDtypeStruct((4, m, fb), BF16), jax.ShapeDtypeStruct((4, m, fb), BF16),
                   jax.ShapeDtypeStruct((1, d), F32), jax.ShapeDtypeStruct((m, d), BF16)],
        scratch_shapes=[pltpu.VMEM((tm, d), F32)],
        args=(dh, h, g, f_gate, f_up, w_in8, w_in8, w_out4), rider=rider)


def matmul_tn(x, y, name, tn=None, tk=None, x2=None, rider=None):
    if tk is not None:
        assert x.shape[0] == y.shape[0] == 1 and x2 is None and rider is None
        bk, m, kf = x.shape[2] // tk, x.shape[1], x.shape[2]
        tm = _tile(m, DW_ROWS)
        tn_ = y.shape[2] if tn is None else tn

        def tiled(x_ref, y_ref, o_ref, acc_ref):
            r = pl.program_id(2)

            @pl.when(r == 0)
            def _():
                acc_ref[...] = jnp.zeros_like(acc_ref)

            acc_ref[...] += _dot_tn(x_ref[0].astype(BF16), y_ref[0].astype(BF16))

            @pl.when(r == m // tm - 1)
            def _():
                o_ref[0] = acc_ref[...].astype(BF16)

        return pl.pallas_call(
            tiled, name=name, grid=(bk, y.shape[2] // tn_, m // tm),
            in_specs=[pl.BlockSpec((1, tm, tk), lambda i, j, r: (0, r, i)),
                      pl.BlockSpec((1, tm, tn_), lambda i, j, r: (0, r, j))],
            out_specs=pl.BlockSpec((1, tk, tn_), lambda i, j, r: (0, i, j)),
            out_shape=jax.ShapeDtypeStruct((1, kf, y.shape[2]), BF16),
            scratch_shapes=[pltpu.VMEM((tk, tn_), F32)],
            compiler_params=_params(3),
        )(x, y)
    bx, m, k = x.shape
    by, _, n = y.shape
    b = max(bx, by) * (2 if x2 is not None else 1)
    tm = _tile(m, DW_ROWS)
    tn = n if tn is None else tn
    nt = n // tn
    nr = m // tm

    def body(*refs):
        x_ref, y_ref = refs[0], refs[-3]
        o_ref, acc_ref = refs[-2], refs[-1]
        r = pl.program_id(2)

        @pl.when(r == 0)
        def _():
            acc_ref[...] = jnp.zeros_like(acc_ref)

        if x2 is None:
            acc_ref[...] += _dot_tn(x_ref[0].astype(BF16), y_ref[0].astype(BF16))
        else:
            @pl.when(pl.program_id(0) < bx)
            def _():
                acc_ref[...] += _dot_tn(x_ref[0].astype(BF16), y_ref[0].astype(BF16))

            @pl.when(pl.program_id(0) >= bx)
            def _():
                acc_ref[...] += _dot_tn(refs[1][0].astype(BF16), y_ref[0].astype(BF16))

        @pl.when(r == nr - 1)
        def _():
            o_ref[0] = acc_ref[...].astype(BF16)

    if x2 is None:
        x_specs = [pl.BlockSpec((1, tm, k), (lambda i, j, r: (i, r, 0)) if bx > 1 else (lambda i, j, r: (0, r, 0)))]
    else:
        x_specs = [pl.BlockSpec((1, tm, k), lambda i, j, r: (jnp.minimum(i, bx - 1), jnp.where(i < bx, r, nr - 1), 0)),
                   pl.BlockSpec((1, tm, k), lambda i, j, r: (jnp.maximum(i - bx, 0), jnp.where(i < bx, 0, r), 0))]
    y_map = (lambda i, j, r: (i, r, j)) if by > 1 else (lambda i, j, r: (0, r, j))
    (out,), carried = rider_call(
        body, name, (b, nt, nr),
        in_specs=x_specs + [pl.BlockSpec((1, tm, tn), y_map)],
        out_specs=[pl.BlockSpec((1, k, tn), lambda i, j, r: (i, 0, j))],
        out_shape=[jax.ShapeDtypeStruct((b, k, n), BF16)],
        scratch_shapes=[pltpu.VMEM((k, tn), F32)],
        args=[x] + ([x2] if x2 is not None else []) + [y], rider=rider)
    return (out, carried) if rider is not None else out


AUG = HEAD_DIM


class _Cols:
    def __init__(self, d):
        self.d = d
        self.ga, self.gb = 0, d
        self.qa, self.ka, self.va = 2 * d, 2 * d + 1024, 2 * d + 2048
        self.qb = 2 * d + 3072
        self.kb, self.vb, self.fa = self.qb + 512, self.qb + 640, self.qb + 768
        self.np = self.qb + 1024


def mixer_proj(h, g, wp, name):
    m, d = h.shape
    npad = wp.shape[0]
    tm = _row_tile(m)

    def body(h_ref, g_ref, w_ref, n_ref, p_ref):
        xh, _ = _rms(h_ref[...])
        n = (xh * g_ref[...]).astype(BF16)
        n_ref[...] = n
        p_ref[...] = _dot_nt(n, w_ref[...])

    return pl.pallas_call(
        body, name=name, grid=(m // tm,),
        in_specs=[pl.BlockSpec((tm, d), lambda r: (r, 0)), pl.BlockSpec((1, d), lambda r: (0, 0)),
                  pl.BlockSpec((npad, d), lambda r: (0, 0))],
        out_specs=[pl.BlockSpec((tm, d), lambda r: (r, 0)), pl.BlockSpec((tm, npad), lambda r: (r, 0))],
        out_shape=[jax.ShapeDtypeStruct((m, d), BF16), jax.ShapeDtypeStruct((m, npad), F32)],
        compiler_params=_params(1),
    )(h, g, wp)


def _head_norm(x, gain, ones):
    outs = []
    for b in range(x.shape[1] // LANES):
        xb = x[:, b * LANES:(b + 1) * LANES]
        r = lax.rsqrt(_group_mean(xb * xb, ones) + EPS)
        outs.append(xb * r * gain[:, b * LANES:(b + 1) * LANES])
    return outs


def _head_norm_bwd(x, gain, dn, ones):
    dxs, dgs = [], []
    for b in range(x.shape[1] // LANES):
        sl = slice(b * LANES, (b + 1) * LANES)
        xb, dnb = x[:, sl], dn[:, sl]
        r = lax.rsqrt(_group_mean(xb * xb, ones) + EPS)
        xh = xb * r
        dxh = dnb * gain[:, sl]
        dxs.append(r * (dxh - xh * _group_mean(dxh * xh, ones)))
        dgs.append(jnp.sum(dnb * xh, axis=0, keepdims=True))
    return dxs, dgs


def _lane_col(v, lane_iota, idx):
    return jnp.sum(jnp.where(lane_iota == idx, v, 0.0), axis=1, keepdims=True)


def _aug(base, lane, vals):
    for i, v in enumerate(vals):
        base = jnp.where(lane == AUG + i, v, base)
    return base


def qk_post(proj, gains, fbias, cols, name):
    m = proj.shape[0]
    tm = _row_tile(m)
    gqa, gka, gqb, gkb = gains

    def body(qa_ref, ka_ref, va_ref, qb_ref, kb_ref, vb_ref, fa_ref, gqa_ref, gka_ref, gqb_ref, gkb_ref, fb_ref,
             qf_o, kf_o, vf_o, kt_o, vt_o, qb_o, kb_o, vb_o, carry_ref):
        r0 = pl.program_id(0)

        @pl.when(r0 == 0)
        def _():
            carry_ref[...] = jnp.zeros_like(carry_ref)

        z = fa_ref[...] + fb_ref[...]
        logf = jnp.minimum(z, 0.0) - jnp.log(1.0 + jnp.exp(-jnp.abs(z)))
        rr = lax.broadcasted_iota(jnp.int32, (tm, tm), 0)
        cc = lax.broadcasted_iota(jnp.int32, (tm, tm), 1)
        tril = jnp.where(cc <= rr, 1.0, 0.0).astype(BF16)
        p0, p1, p2 = _split3(logf)
        c = _dot(tril, p0) + _dot(tril, p1) + _dot(tril, p2) + carry_ref[...]
        carry_ref[...] += jnp.sum(logf, axis=0, keepdims=True)

        lane = lax.broadcasted_iota(jnp.int32, (tm, LANES), 1)
        is_pad = (r0 * tm + lax.broadcasted_iota(jnp.int32, (tm, 1), 0)) < PAD_FRONT
        ones = jnp.ones((LANES, LANES), BF16)
        for hd in range(8):
            sl = slice(hd * LANES, (hd + 1) * LANES)
            ch = _lane_col(c, lane, hd)
            ct = [p.astype(F32) for p in _split3(ch)]
            cs = [p.astype(F32) for p in _split3(-jnp.where(is_pad, -NEG, ch))]
            xq = qa_ref[:, sl]
            qn = xq * lax.rsqrt(_group_mean(xq * xq, ones) + EPS) * gqa_ref[...]
            qf_o[:, sl] = _aug(qn, lane, ct + [1.0, 1.0, 1.0]).astype(BF16)
            xk = ka_ref[:, sl]
            kn = xk * lax.rsqrt(_group_mean(xk * xk, ones) + EPS) * gka_ref[...]
            kf = _aug(kn, lane, [1.0, 1.0, 1.0] + cs)
            vf = _aug(va_ref[:, sl], lane, [1.0, 1.0, 1.0])
            kf_o[:, sl] = kf.astype(BF16)
            vf_o[:, sl] = vf.astype(BF16)
            kt_o[sl, :] = kf.T.astype(BF16)
            vt_o[sl, :] = vf.T.astype(BF16)

        gones = _group_ones()
        for src, gn, dst in ((qb_ref, gqb_ref, qb_o), (kb_ref, gkb_ref, kb_o)):
            for b, blk in enumerate(_head_norm(src[...], gn[...], gones)):
                dst[:, b * LANES:(b + 1) * LANES] = blk.astype(BF16)
        vb_o[...] = vb_ref[...].astype(BF16)

    w1024 = lambda off: pl.BlockSpec((tm, 1024), lambda r, o=off // 1024: (r, o))
    w512 = lambda off: pl.BlockSpec((tm, 512), lambda r, o=off // 512: (r, o))
    w128 = lambda off: pl.BlockSpec((tm, LANES), lambda r, o=off // LANES: (r, o))
    vec = lambda w: pl.BlockSpec((1, w), lambda r: (0, 0))
    row = lambda w: pl.BlockSpec((tm, w), lambda r: (r, 0))
    return pl.pallas_call(
        body, name=name, grid=(m // tm,),
        in_specs=[w1024(cols.qa), w1024(cols.ka), w1024(cols.va), w512(cols.qb), w128(cols.kb), w128(cols.vb),
                  w128(cols.fa), vec(LANES), vec(LANES), vec(512), vec(LANES), vec(LANES)],
        out_specs=[row(1024), row(1024), row(1024)] + [pl.BlockSpec((1024, tm), lambda r: (0, r))] * 2
                  + [row(512), row(LANES), row(LANES)],
        out_shape=[jax.ShapeDtypeStruct((m, 1024), BF16)] * 3 + [jax.ShapeDtypeStruct((1024, m), BF16)] * 2
                  + [jax.ShapeDtypeStruct((m, 512), BF16)] + [jax.ShapeDtypeStruct((m, LANES), BF16)] * 2,
        scratch_shapes=[pltpu.VMEM((1, LANES), F32)],
        compiler_params=_params(1),
    )(proj, proj, proj, proj, proj, proj, proj, gqa, gka, gqb, gkb, fbias)


def qk_post_bwd(proj, gains, fbias, dqf, dkf, dvf, dqb, dkb, dvb, dc, dga, dgb, cols, name):
    m = proj.shape[0]
    d = cols.d
    tm = _row_tile(m)
    nt = m // tm
    gqa, gka, gqb, gkb = gains

    def body(qa_ref, ka_ref, qb_ref, kb_ref, fa_ref, gqa_ref, gka_ref, gqb_ref, gkb_ref, fb_ref,
             dqf_ref, dkf_ref, dvf_ref, dqb_ref, dkb_ref, dvb_ref, dc_ref, dga_ref, dgb_ref,
             dp_o, ggqa_o, ggka_o, ggqb_o, ggkb_o, gfb_o, carry_ref):
        @pl.when(pl.program_id(0) == 0)
        def _():
            carry_ref[...] = jnp.zeros_like(carry_ref)
            for o in (ggqa_o, ggka_o, ggqb_o, ggkb_o, gfb_o):
                o[...] = jnp.zeros_like(o)

        dp_o[:, cols.ga:cols.ga + d] = dga_ref[...].astype(BF16)
        dp_o[:, cols.gb:cols.gb + d] = dgb_ref[...].astype(BF16)
        dp_o[:, cols.fa + LANES:cols.np] = jnp.zeros((tm, cols.np - cols.fa - LANES), BF16)
        lane = lax.broadcasted_iota(jnp.int32, (tm, LANES), 1)
        data = lane < HEAD_DIM
        ones = jnp.ones((LANES, LANES), BF16)
        for hd in range(8):
            sl = slice(hd * LANES, (hd + 1) * LANES)
            for src, gn, dn_ref, off, gout in ((qa_ref, gqa_ref, dqf_ref, cols.qa, ggqa_o),
                                               (ka_ref, gka_ref, dkf_ref, cols.ka, ggka_o)):
                x = src[:, sl]
                dn = jnp.where(data, dn_ref[:, sl], 0.0)
                r = lax.rsqrt(_group_mean(x * x, ones) + EPS)
                xh = x * r
                dxh = dn * gn[...]
                dp_o[:, off + hd * LANES:off + (hd + 1) * LANES] = (
                    r * (dxh - xh * _group_mean(dxh * xh, ones))).astype(BF16)
                gout[...] += jnp.sum(dn * xh, axis=0, keepdims=True)
            dp_o[:, cols.va + hd * LANES:cols.va + (hd + 1) * LANES] = jnp.where(data, dvf_ref[:, sl], 0.0).astype(BF16)
        dp_o[:, cols.vb:cols.vb + LANES] = dvb_ref[...].astype(BF16)
        gones = _group_ones()
        for src, gn, dn, off, gout in ((qb_ref, gqb_ref, dqb_ref, cols.qb, ggqb_o),
                                       (kb_ref, gkb_ref, dkb_ref, cols.kb, ggkb_o)):
            dxs, dgs = _head_norm_bwd(src[...], gn[...], dn[...], gones)
            for b, (dx, dg) in enumerate(zip(dxs, dgs)):
                dp_o[:, off + b * LANES:off + (b + 1) * LANES] = dx.astype(BF16)
                gout[:, b * LANES:(b + 1) * LANES] += dg
        dcv = dc_ref[...]
        rr = lax.broadcasted_iota(jnp.int32, (tm, tm), 0)
        cc = lax.broadcasted_iota(jnp.int32, (tm, tm), 1)
        triu = jnp.where(cc >= rr, 1.0, 0.0).astype(BF16)
        p0, p1, p2 = _split3(dcv)
        dlogf = _dot(triu, p0) + _dot(triu, p1) + _dot(triu, p2) + carry_ref[...]
        carry_ref[...] += jnp.sum(dcv, axis=0, keepdims=True)
        z = fa_ref[...] + fb_ref[...]
        row = (nt - 1 - pl.program_id(0)) * tm + lax.broadcasted_iota(jnp.int32, (tm, LANES), 0)
        dfa = jnp.where(row >= PAD_FRONT, dlogf * jax.nn.sigmoid(-z), 0.0)
        dp_o[:, cols.fa:cols.fa + LANES] = dfa.astype(BF16)
        gfb_o[...] += jnp.sum(dfa, axis=0, keepdims=True)

    rev = lambda r: nt - 1 - r
    w1024 = lambda off: pl.BlockSpec((tm, 1024), lambda r, o=off // 1024: (rev(r), o))
    w512 = lambda off: pl.BlockSpec((tm, 512), lambda r, o=off // 512: (rev(r), o))
    w128 = lambda off: pl.BlockSpec((tm, LANES), lambda r, o=off // LANES: (rev(r), o))
    vec = lambda w: pl.BlockSpec((1, w), lambda r: (0, 0))
    row = lambda w: pl.BlockSpec((tm, w), lambda r: (rev(r), 0))
    return pl.pallas_call(
        body, name=name, grid=(nt,),
        in_specs=[w1024(cols.qa), w1024(cols.ka), w512(cols.qb), w128(cols.kb), w128(cols.fa),
                  vec(LANES), vec(LANES), vec(512), vec(LANES), vec(LANES),
                  row(1024), row(1024), row(1024), row(512), row(LANES), row(LANES), row(LANES), row(d), row(d)],
        out_specs=[row(cols.np), vec(LANES), vec(LANES), vec(512), vec(LANES), vec(LANES)],
        out_shape=[jax.ShapeDtypeStruct((m, cols.np), BF16), jax.ShapeDtypeStruct((1, LANES), F32),
                   jax.ShapeDtypeStruct((1, LANES), F32), jax.ShapeDtypeStruct((1, 512), F32),
                   jax.ShapeDtypeStruct((1, LANES), F32), jax.ShapeDtypeStruct((1, LANES), F32)],
        scratch_shapes=[pltpu.VMEM((1, LANES), F32)],
        compiler_params=_params(1),
    )(proj, proj, proj, proj, proj, gqa, gka, gqb, gkb, fbias, dqf, dkf, dvf, dqb, dkb, dvb, dc, dga, dgb)


def dproj_bwd(dh, h, g, dproj, wp, name):
    m, d = h.shape
    npad = wp.shape[0]
    tm = _row_tile(m)

    def body(dh_ref, h_ref, g_ref, dp_ref, w_ref, dhin_ref, dgn_ref):
        @pl.when(pl.program_id(0) == 0)
        def _():
            dgn_ref[...] = jnp.zeros_like(dgn_ref)

        dn = _dot(dp_ref[...], w_ref[...])
        dx, dgain = _rms_bwd(h_ref[...], g_ref[...], dn)
        dgn_ref[...] += dgain
        dhin_ref[...] = dh_ref[...] + dx

    row = lambda w: pl.BlockSpec((tm, w), lambda r: (r, 0))
    return pl.pallas_call(
        body, name=name, grid=(m // tm,),
        in_specs=[row(d), row(d), pl.BlockSpec((1, d), lambda r: (0, 0)), row(npad),
                  pl.BlockSpec((npad, d), lambda r: (0, 0))],
        out_specs=[row(d), pl.BlockSpec((1, d), lambda r: (0, 0))],
        out_shape=[jax.ShapeDtypeStruct((m, d), F32), jax.ShapeDtypeStruct((1, d), F32)],
        compiler_params=_params(1),
    )(dh, h, g, dproj, wp)


def _causal_t(t):
    return lax.broadcasted_iota(jnp.int32, (t, t), 0) <= lax.broadcasted_iota(jnp.int32, (t, t), 1)


HEADS_PER_STEP = 4
HEADS_PER_STEP_FWD = 8


def fox_fwd(qf, kf, vt, name, rider=None):
    m = qf.shape[0]
    t = _row_tile(m)
    nq = m // t
    hp = HEADS_PER_STEP_FWD
    w = hp * LANES

    def body(q_ref, k_ref, vt_ref, o_ref, lse_ref, acc_ref, m_ref, p_ref, a_ref):
        qi = pl.program_id(1)
        acc_ref[...] = jnp.zeros_like(acc_ref)
        m_ref[...] = jnp.full_like(m_ref, NEG)

        def scores(ki, slot, mask):
            off = pl.multiple_of(ki * t, t)
            for e in range(hp):
                sl = slice(e * LANES, (e + 1) * LANES)
                s = _dot_nt(k_ref[pl.ds(off, t), sl], q_ref[:, sl])
                if mask is not None:
                    s = jnp.where(mask, s, NEG)
                m_old = m_ref[e]
                m_new = jnp.maximum(m_old, jnp.max(s, axis=0, keepdims=True))
                p_ref[slot, e] = jnp.exp(s - m_new).astype(BF16)
                a_ref[slot, e] = jnp.exp(m_old - m_new)
                m_ref[e] = m_new

        def values(ki, slot):
            off = pl.multiple_of(ki * t, t)
            for e in range(hp):
                sl = slice(e * LANES, (e + 1) * LANES)
                acc_ref[e] = acc_ref[e] * a_ref[slot, e] + _dot(vt_ref[sl, pl.ds(off, t)], p_ref[slot, e])

        causal = _causal_t(t)
        scores(0, 0, causal | (jnp.full((t, t), qi, jnp.int32) > 0))

        def step(ki, carry):
            values(ki - 1, (ki - 1) % 2)
            scores(ki, ki % 2, None)
            return carry

        lax.fori_loop(1, qi, step, 0)

        @pl.when(qi >= 1)
        def _():
            values(qi - 1, (qi - 1) % 2)
            scores(qi, qi % 2, causal)

        values(qi, qi % 2)
        row = lax.broadcasted_iota(jnp.int32, (LANES, t), 0)
        for e in range(hp):
            l = jnp.max(acc_ref[e, AUG:AUG + 8, :], axis=0, keepdims=True)
            o_ref[:, e * LANES:(e + 1) * LANES] = jnp.where(row < HEAD_DIM, acc_ref[e] * (1.0 / l), 0.0).T
            lse_ref[e] = m_ref[e] + jnp.log(l)

    return rider_call(
        body, name, (8 // hp, nq),
        in_specs=[pl.BlockSpec((t, w), lambda hd, i: (i, hd)),
                  pl.BlockSpec((m, w), lambda hd, i: (0, hd)),
                  pl.BlockSpec((w, m), lambda hd, i: (hd, 0))],
        out_specs=[pl.BlockSpec((t, w), lambda hd, i: (i, hd)),
                   pl.BlockSpec((hp, 1, t), lambda hd, i: (hd, 0, i))],
        out_shape=[jax.ShapeDtypeStruct((m, 8 * LANES), F32), jax.ShapeDtypeStruct((8, 1, m), F32)],
        scratch_shapes=[pltpu.VMEM((hp, LANES, t), F32), pltpu.VMEM((hp, 1, t), F32),
                        pltpu.VMEM((2, hp, t, t), BF16), pltpu.VMEM((2, hp, 1, t), F32)],
        args=(qf, kf, vt), rider=rider)


def fox_bwd(qf, kf, vf, kt, dof, lse, delta, name, rider=None):
    m = qf.shape[0]
    t = _row_tile(m)
    nq = m // t
    hp = HEADS_PER_STEP
    w = hp * LANES

    def body(k_ref, v_ref, kt_ref, q_ref, do_ref, lse_ref, delta_ref, dk_ref, dv_ref, dq_ref, dck_ref, dcq_ref,
             dka_ref, dva_ref, dqt_ref):
        ki = pl.program_id(1)

        @pl.when(ki == 0)
        def _():
            dqt_ref[...] = jnp.zeros_like(dqt_ref)

        dka_ref[...] = jnp.zeros_like(dka_ref)
        dva_ref[...] = jnp.zeros_like(dva_ref)

        def tile(qi, diagonal):
            off = pl.multiple_of(qi * t, t)
            for e in range(hp):
                sl = slice(e * LANES, (e + 1) * LANES)
                q = q_ref[pl.ds(off, t), sl]
                do = do_ref[pl.ds(off, t), sl]
                s = _dot_nt(k_ref[:, sl], q)
                if diagonal:
                    s = jnp.where(_causal_t(t), s, NEG)
                p = jnp.exp(s - lse_ref[e, :, pl.ds(off, t)])
                ds = (p * (_dot_nt(v_ref[:, sl], do) - delta_ref[e, :, pl.ds(off, t)])).astype(BF16)
                dva_ref[:, sl] += _dot(p.astype(BF16), do)
                dka_ref[:, sl] += _dot(ds, q)
                dqt_ref[sl, pl.ds(off, t)] += _dot(kt_ref[sl, :], ds)

        def step(qi, carry):
            tile(qi, False)
            return carry

        tile(ki, True)
        lax.fori_loop(ki + 1, nq, step, 0)
        dk_ref[...] = dka_ref[...]
        dv_ref[...] = dva_ref[...]
        row8 = lax.broadcasted_iota(jnp.int32, (8, 1), 0)
        for e in range(hp):
            slab = dka_ref[:, e * LANES:(e + 1) * LANES].T[AUG:AUG + 8, :]
            dck_ref[e] = -jnp.sum(jnp.where(row8 == 3, slab, 0.0), axis=0, keepdims=True)

        @pl.when(ki == nq - 1)
        def _():
            for e in range(hp):
                sl = slice(e * LANES, (e + 1) * LANES)
                slab = dqt_ref[e * LANES + AUG:e * LANES + AUG + 8, :]
                dcq_ref[e] = jnp.sum(jnp.where(row8 == 0, slab, 0.0), axis=0, keepdims=True)
                for j in range(nq):
                    dq_ref[j * t:(j + 1) * t, sl] = dqt_ref[sl, j * t:(j + 1) * t].T

    tile_spec = pl.BlockSpec((t, w), lambda hd, i: (i, hd))
    full = pl.BlockSpec((m, w), lambda hd, i: (0, hd))
    stat = pl.BlockSpec((hp, 1, m), lambda hd, i: (hd, 0, 0))
    (dkf, dvf, dqf, dck, dcq), carried = rider_call(
        body, name, (8 // hp, nq),
        in_specs=[tile_spec, tile_spec, pl.BlockSpec((w, t), lambda hd, i: (hd, i)), full, full, stat, stat],
        out_specs=[tile_spec, tile_spec, full, pl.BlockSpec((hp, 1, t), lambda hd, i: (hd, 0, i)), stat],
        out_shape=[jax.ShapeDtypeStruct((m, 8 * LANES), F32), jax.ShapeDtypeStruct((m, 8 * LANES), F32),
                   jax.ShapeDtypeStruct((m, 8 * LANES), F32), jax.ShapeDtypeStruct((8, 1, m), F32),
                   jax.ShapeDtypeStruct((8, 1, m), F32)],
        scratch_shapes=[pltpu.VMEM((t, w), F32), pltpu.VMEM((t, w), F32), pltpu.VMEM((w, m), F32)],
        args=(kf, vf, kt, qf, dof, lse, delta), rider=rider)
    return (dkf, dvf, dqf, dcq + dck), carried


def _bucket_ids():
    def bucket(dist):
        n = np.maximum(dist, 0)
        max_exact = N_BUCKETS // 2
        nf = np.maximum(n, 1).astype(np.float32)
        large = max_exact + (np.log(nf / max_exact) / math.log(MAX_DISTANCE / max_exact)
                             * (N_BUCKETS - max_exact)).astype(np.int32)
        return np.where(n < max_exact, n, np.minimum(large, N_BUCKETS - 1))

    tl = np.arange(LANES)[:, None]
    sl = np.arange(LANES)[None, :]
    prev = bucket(LANES + tl - sl)
    cur = bucket(tl - sl)
    meta = np.full((LANES, LANES), N_BUCKETS - 1)
    return np.concatenate([prev, cur, meta], axis=1).astype(np.int32)


def bias_build(table, name):
    ids = jnp.asarray(_bucket_ids())

    def body(t_ref, id_ref, o_ref):
        idv = id_ref[...]
        for h in range(8):
            acc = jnp.zeros((LANES, 3 * LANES), F32)
            for b in range(N_BUCKETS):
                acc = jnp.where(idv == b, t_ref[b, h], acc)
            o_ref[h] = acc

    return pl.pallas_call(
        body, name=name,
        in_specs=[pl.BlockSpec(memory_space=pltpu.SMEM), pl.BlockSpec(memory_space=pltpu.VMEM)],
        out_specs=pl.BlockSpec(memory_space=pltpu.VMEM),
        out_shape=jax.ShapeDtypeStruct((8, LANES, 3 * LANES), F32),
    )(table, ids)


def bias_reduce(dbias, name):
    ids = jnp.asarray(_bucket_ids())

    def body(d_ref, id_ref, o_ref):
        idv = id_ref[...]
        rr = lax.broadcasted_iota(jnp.int32, (N_BUCKETS, LANES), 0)
        cc = lax.broadcasted_iota(jnp.int32, (N_BUCKETS, LANES), 1)
        acc = jnp.zeros((N_BUCKETS, LANES), F32)
        for h in range(8):
            dv = d_ref[h]
            for b in range(N_BUCKETS):
                val = jnp.sum(jnp.where(idv == b, dv, 0.0), keepdims=True)
                acc = jnp.where((rr == b) & (cc == h), val, acc)
        o_ref[...] = acc

    return pl.pallas_call(
        body, name=name,
        in_specs=[pl.BlockSpec(memory_space=pltpu.VMEM), pl.BlockSpec(memory_space=pltpu.VMEM)],
        out_specs=pl.BlockSpec(memory_space=pltpu.VMEM),
        out_shape=jax.ShapeDtypeStruct((N_BUCKETS, LANES), F32),
    )(dbias, ids)


def _swa_penalty(n):
    shape = (LANES, 3 * LANES)
    tl = lax.broadcasted_iota(jnp.int32, shape, 0)
    col = lax.broadcasted_iota(jnp.int32, shape, 1)
    sl = col & (LANES - 1)
    nv = jnp.full(shape, n, jnp.int32)
    is_meta = sl >= PAD_FRONT
    prev = (col < LANES) & (sl > tl) & (nv >= 1) & ((nv >= 2) | is_meta)
    cur = (col >= LANES) & (col < 2 * LANES) & (sl <= tl) & ((nv >= 1) | is_meta)
    meta = (col >= 2 * LANES) & is_meta & ((nv >= 2) | ((nv == 1) & (sl <= tl)))
    return jnp.where(prev | cur | meta, 0.0, NEG)


def _swa_keys(ref, n):
    off_prev = pl.multiple_of(jnp.maximum(n - 1, 0) * LANES, LANES)
    off_cur = pl.multiple_of(n * LANES, LANES)
    return jnp.concatenate([ref[pl.ds(off_prev, LANES), :], ref[pl.ds(off_cur, LANES), :], ref[0:LANES, :]], axis=0)


def swa_fwd(q, k, v, bias, sinks, name, rider=None):
    m = q.shape[0]

    def body(q_ref, k_ref, v_ref, bias_ref, sink_ref, o_ref, lse_ref):
        n = pl.program_id(0)
        lane1 = lax.broadcasted_iota(jnp.int32, (1, LANES), 1)
        lane_t = lax.broadcasted_iota(jnp.int32, (LANES, LANES), 1)
        in_head = [lane1 < HEAD_DIM, lane1 >= HEAD_DIM]
        kall = _swa_keys(k_ref, n)
        vall = _swa_keys(v_ref, n)
        vs = [jnp.where(in_head[g], vall, jnp.zeros_like(vall)) for g in (0, 1)]
        penalty = _swa_penalty(n)
        lse = jnp.zeros((LANES, LANES), F32)
        for b in range(4):
            qb = q_ref[:, b * LANES:(b + 1) * LANES]
            ob = jnp.zeros((LANES, LANES), F32)
            for g in (0, 1):
                h = 4 * g + b
                qe = jnp.where(in_head[g], qb, jnp.zeros_like(qb))
                s = _dot_nt(qe, kall) + bias_ref[h] + penalty
                sink = sink_ref[h]
                mx = jnp.maximum(jnp.max(s, axis=1, keepdims=True), sink)
                p = jnp.exp(s - mx)
                den = jnp.sum(p, axis=1, keepdims=True) + jnp.exp(sink - mx)
                ob = ob + _dot((p / den).astype(BF16), vs[g])
                lse = jnp.where(lane_t == h, mx + jnp.log(den), lse)
            o_ref[:, b * LANES:(b + 1) * LANES] = ob
        lse_ref[...] = lse

    return rider_call(
        body, name, (m // LANES,),
        in_specs=[pl.BlockSpec((LANES, 512), lambda n: (n, 0)),
                  pl.BlockSpec((m, LANES), lambda n: (0, 0)), pl.BlockSpec((m, LANES), lambda n: (0, 0)),
                  pl.BlockSpec((8, LANES, 3 * LANES), lambda n: (0, 0, 0)),
                  pl.BlockSpec(memory_space=pltpu.SMEM)],
        out_specs=[pl.BlockSpec((LANES, 512), lambda n: (n, 0)), pl.BlockSpec((LANES, LANES), lambda n: (n, 0))],
        out_shape=[jax.ShapeDtypeStruct((m, 512), F32), jax.ShapeDtypeStruct((m, LANES), F32)],
        scratch_shapes=[], args=(q, k, v, bias, sinks), rider=rider)


def swa_bwd(q, k, v, bias, sinks, o, lse, do, name):
    m = q.shape[0]

    def body(q_ref, do_ref, o_ref, lse_ref, k_ref, v_ref, bias_ref, sink_ref,
             dq_ref, dk_ref, dv_ref, dbias_ref, dsink_ref):
        n = pl.program_id(0)

        @pl.when(n == 0)
        def _():
            for r in (dk_ref, dv_ref, dbias_ref, dsink_ref):
                r[...] = jnp.zeros_like(r)

        lane1 = lax.broadcasted_iota(jnp.int32, (1, LANES), 1)
        lane_t = lax.broadcasted_iota(jnp.int32, (LANES, LANES), 1)
        in_head = [lane1 < HEAD_DIM, lane1 >= HEAD_DIM]
        off_prev = pl.multiple_of(jnp.maximum(n - 1, 0) * LANES, LANES)
        off_cur = pl.multiple_of(n * LANES, LANES)
        kall = _swa_keys(k_ref, n)
        vall = _swa_keys(v_ref, n)
        ks = [jnp.where(in_head[g], kall, jnp.zeros_like(kall)) for g in (0, 1)]
        penalty = _swa_penalty(n)
        lsev = lse_ref[...]
        dsink = dsink_ref[...]
        dkall = jnp.zeros((3 * LANES, LANES), F32)
        dvall = jnp.zeros((3 * LANES, LANES), F32)
        for b in range(4):
            sl = slice(b * LANES, (b + 1) * LANES)
            qb = q_ref[:, sl]
            dob = do_ref[:, sl]
            prod = dob * o_ref[:, sl]
            dqb = jnp.zeros((LANES, LANES), F32)
            for g in (0, 1):
                h = 4 * g + b
                qe = jnp.where(in_head[g], qb, jnp.zeros_like(qb))
                doe = jnp.where(in_head[g], dob, 0.0).astype(BF16)
                delta = jnp.sum(jnp.where(in_head[g], prod, 0.0), axis=1, keepdims=True)
                lse_h = _lane_col(lsev, lane_t, h)
                s = _dot_nt(qe, kall) + bias_ref[h] + penalty
                p = jnp.exp(s - lse_h)
                ds = p * (_dot_nt(doe, vall) - delta)
                dbias_ref[h] += ds
                sink_part = jnp.sum(-jnp.exp(sink_ref[h] - lse_h) * delta, keepdims=True)
                dsink = jnp.where(lane1 == h, dsink + sink_part, dsink)
                dsb = ds.astype(BF16)
                dqb = dqb + _dot(dsb, ks[g])
                dkall = dkall + _dot_tn(dsb, qe)
                dvall = dvall + _dot_tn(p.astype(BF16), doe)
            dq_ref[:, sl] = dqb
        dsink_ref[...] = dsink
        for ref, val in ((dk_ref, dkall), (dv_ref, dvall)):
            ref[pl.ds(off_prev, LANES), :] += val[0:LANES]
            ref[pl.ds(off_cur, LANES), :] += val[LANES:2 * LANES]
            ref[0:LANES, :] += val[2 * LANES:3 * LANES]

    blk = pl.BlockSpec((LANES, 512), lambda n: (n, 0))
    full = pl.BlockSpec((m, LANES), lambda n: (0, 0))
    return pl.pallas_call(
        body, name=name, grid=(m // LANES,),
        in_specs=[blk, blk, blk, pl.BlockSpec((LANES, LANES), lambda n: (n, 0)), full, full,
                  pl.BlockSpec((8, LANES, 3 * LANES), lambda n: (0, 0, 0)),
                  pl.BlockSpec(memory_space=pltpu.SMEM)],
        out_specs=[blk, full, full, pl.BlockSpec((8, LANES, 3 * LANES), lambda n: (0, 0, 0)),
                   pl.BlockSpec((1, LANES), lambda n: (0, 0))],
        out_shape=[jax.ShapeDtypeStruct((m, 512), F32), jax.ShapeDtypeStruct((m, LANES), F32),
                   jax.ShapeDtypeStruct((m, LANES), F32), jax.ShapeDtypeStruct((8, LANES, 3 * LANES), F32),
                   jax.ShapeDtypeStruct((1, LANES), F32)],
        compiler_params=_params(1),
    )(q, do, o, lse, k, v, bias, sinks)


def branch_out(h, o_fox, o_swa, proj, wbf, wbs, wo, cols, name):
    m, d = h.shape
    tm = _row_tile(m)

    def body(h_ref, of_ref, os_ref, ga_ref, gb_ref, wbf_ref, wbs_ref, wo_ref, hn_ref, tf_ref, ts_ref):
        tf = _dot(of_ref[...].astype(BF16), wbf_ref[...])
        ts = _dot(os_ref[...].astype(BF16), wbs_ref[...])
        tf_ref[...] = tf.astype(BF16)
        ts_ref[...] = ts.astype(BF16)
        y = jax.nn.sigmoid(ga_ref[...]) * tf + jax.nn.sigmoid(gb_ref[...]) * ts
        hn_ref[...] = h_ref[...] + _dot(y.astype(BF16), wo_ref[...])

    row = lambda w, o=0: pl.BlockSpec((tm, w), lambda r, o=o: (r, o))
    res = lambda a: pl.BlockSpec(a.shape, lambda r: (0, 0))
    return pl.pallas_call(
        body, name=name, grid=(m // tm,),
        in_specs=[row(d), row(1024), row(512), row(d, cols.ga // d), row(d, cols.gb // d), res(wbf), res(wbs), res(wo)],
        out_specs=[row(d)] * 3,
        out_shape=[jax.ShapeDtypeStruct((m, d), F32)] + [jax.ShapeDtypeStruct((m, d), BF16)] * 2,
        compiler_params=_params(1),
    )(h, o_fox, o_swa, proj, proj, wbf, wbs, wo)


def branch_out_bwd(dh, o_fox, t_fox, t_swa, proj, wbf, wbs, wo, cols, name):
    m, d = dh.shape
    tm = _row_tile(m)

    def body(dh_ref, of_ref, tf_ref, ts_ref, ga_ref, gb_ref, wbf_ref, wbs_ref, wo_ref,
             y_ref, dtf_ref, dts_ref, dga_ref, dgb_ref, dof_ref, dos_ref, delta_ref):
        dy = _dot_nt(dh_ref[...].astype(BF16), wo_ref[...])
        tf = tf_ref[...].astype(F32)
        ts = ts_ref[...].astype(F32)
        sa = jax.nn.sigmoid(ga_ref[...])
        sb = jax.nn.sigmoid(gb_ref[...])
        y_ref[...] = (sa * tf + sb * ts).astype(BF16)
        dtf = (dy * sa).astype(BF16)
        dts = (dy * sb).astype(BF16)
        dtf_ref[...] = dtf
        dts_ref[...] = dts
        dga_ref[...] = (dy * tf * sa * (1.0 - sa)).astype(BF16)
        dgb_ref[...] = (dy * ts * sb * (1.0 - sb)).astype(BF16)
        dof = _dot_nt(dtf, wbf_ref[...])
        dof_ref[...] = dof.astype(BF16)
        dos_ref[...] = _dot_nt(dts, wbs_ref[...])
        lane = lax.broadcasted_iota(jnp.int32, (tm, LANES), 1)
        delta = jnp.zeros((tm, LANES), F32)
        for hd in range(8):
            sl = slice(hd * LANES, (hd + 1) * LANES)
            delta = jnp.where(lane == hd, jnp.sum(dof[:, sl] * of_ref[:, sl], axis=1, keepdims=True), delta)
        delta_ref[...] = delta

    row = lambda w, o=0: pl.BlockSpec((tm, w), lambda r, o=o: (r, o))
    res = lambda a: pl.BlockSpec(a.shape, lambda r: (0, 0))
    return pl.pallas_call(
        body, name=name, grid=(m // tm,),
        in_specs=[row(d), row(1024), row(d), row(d), row(d, cols.ga // d), row(d, cols.gb // d), res(wbf), res(wbs),
                  res(wo)],
        out_specs=[row(d)] * 5 + [row(1024), row(512), row(LANES)],
        out_shape=[jax.ShapeDtypeStruct((m, d), BF16)] * 5 + [jax.ShapeDtypeStruct((m, 1024), BF16),
                   jax.ShapeDtypeStruct((m, 512), F32), jax.ShapeDtypeStruct((m, LANES), F32)],
        compiler_params=_params(1),
    )(dh, o_fox, t_fox, t_swa, proj, proj, wbf, wbs, wo)


def loss_head(h, target, name):
    m, d = h.shape

    def body(h_ref, t_ref, dh_ref, loss_ref):
        n = pl.program_id(0)

        @pl.when(n == 0)
        def _():
            loss_ref[...] = jnp.zeros_like(loss_ref)
            dh_ref[...] = jnp.zeros_like(dh_ref)

        @pl.when(n > 0)
        def _():
            err = h_ref[...] - t_ref[...]
            dh_ref[...] = err * (1.0 / d)
            loss_ref[...] += jnp.sum(err * err, keepdims=True) * (0.5 / d)

    return pl.pallas_call(
        body, name=name, grid=(m // LANES,),
        in_specs=[pl.BlockSpec((LANES, d), lambda n: (n, 0)),
                  pl.BlockSpec((LANES, d), lambda n: (jnp.maximum(n - 1, 0), 0))],
        out_specs=[pl.BlockSpec((LANES, d), lambda n: (n, 0)), pl.BlockSpec((8, LANES), lambda n: (0, 0))],
        out_shape=[jax.ShapeDtypeStruct((m, d), F32), jax.ShapeDtypeStruct((8, LANES), F32)],
        compiler_params=_params(1),
    )(h, target)


def _adamw_math(w, g, m, v):
    m = ADAM_B1 * m + (1.0 - ADAM_B1) * g
    v = ADAM_B2 * v + (1.0 - ADAM_B2) * (g * g)
    m_hat = m / (1.0 - ADAM_B1 ** ADAM_STEP)
    v_hat = v / (1.0 - ADAM_B2 ** ADAM_STEP)
    delta = -ADAM_LR * (m_hat / (jnp.sqrt(v_hat) + ADAM_EPS) + ADAM_WD * w)
    return delta, m, v


def adamw_sum(parts, w, m, v, name, after=None):
    n_layers, a, b = w.shape
    ta = next(t for t in (256, 176, 128, a) if a % t == 0)
    nr = a // ta

    def body(*refs):
        p_refs = refs[:n_layers]
        w_ref, m_ref, v_ref = refs[n_layers:n_layers + 3]
        g_o, d_o, m_o, v_o = refs[-4:]
        for l in range(n_layers):
            @pl.when(pl.program_id(0) == l)
            def _(l=l):
                g = p_refs[l][0].astype(F32)
                for j in range(1, N_DEV):
                    g = g + p_refs[l][j].astype(F32)
                g_o[0] = g
                d_o[0], m_o[0], v_o[0] = _adamw_math(w_ref[0], g, m_ref[0], v_ref[0])

    def part_spec(l):
        return pl.BlockSpec((N_DEV, ta, b), lambda i, r, l=l: (0, jnp.where(i == l, r, jnp.where(i < l, 0, nr - 1)), 0))

    row = pl.BlockSpec((1, ta, b), lambda i, r: (i, r, 0))
    return pl.pallas_call(
        body, name=name, grid=(n_layers, nr),
        in_specs=[part_spec(l) for l in range(n_layers)] + [row, row, row]
                 + ([pl.BlockSpec(memory_space=pl.ANY)] if after is not None else []),
        out_specs=[row] * 4,
        out_shape=[jax.ShapeDtypeStruct(w.shape, F32)] * 4,
        compiler_params=_params(2),
    )(*parts, w, m, v, *([after] if after is not None else []))


def adamw_sum_cols(parts, w, m, v, name, after=None):
    n_layers = len(parts)
    a, b = parts[0].shape[1:]
    tc = 512 if b % 512 == 0 else b
    nc = b // tc

    def body(*refs):
        p_refs = refs[:n_layers]
        w_ref, m_ref, v_ref = refs[n_layers:n_layers + 3]
        g_o, d_o, m_o, v_o = refs[-4:]
        for l in range(n_layers):
            @pl.when(pl.program_id(0) == l)
            def _(l=l):
                g = p_refs[l][0].astype(F32)
                for j in range(1, N_DEV):
                    g = g + p_refs[l][j].astype(F32)
                g_o[...] = g
                d_o[...], m_o[...], v_o[...] = _adamw_math(w_ref[...], g, m_ref[...], v_ref[...])

    def part_spec(l):
        return pl.BlockSpec((N_DEV, a, tc), lambda i, c, l=l: (0, 0, jnp.where(i == l, c, jnp.where(i < l, 0, nc - 1))))

    col = pl.BlockSpec((a, tc), lambda i, c: (0, i * nc + c))
    return pl.pallas_call(
        body, name=name, grid=(n_layers, nc),
        in_specs=[part_spec(l) for l in range(n_layers)] + [col, col, col]
                 + ([pl.BlockSpec(memory_space=pl.ANY)] if after is not None else []),
        out_specs=[col] * 4,
        out_shape=[jax.ShapeDtypeStruct(w.shape, F32)] * 4,
        compiler_params=_params(2),
    )(*parts, w, m, v, *([after] if after is not None else []))


def adamw_small(g, w, m, v, name):
    def body(g_ref, w_ref, m_ref, v_ref, d_o, m_o, v_o):
        d_o[...], m_o[...], v_o[...] = _adamw_math(w_ref[...], g_ref[...], m_ref[...], v_ref[...])

    spec = pl.BlockSpec(memory_space=pltpu.VMEM)
    return pl.pallas_call(
        body, name=name, in_specs=[spec] * 4, out_specs=[spec] * 3,
        out_shape=[jax.ShapeDtypeStruct(w.shape, F32)] * 3,
    )(g, w, m, v)


BIG = ("ffn1_w_in", "ffn1_w_out", "w_in", "w_branch_fox", "w_branch_swa", "w_out", "ffn2_w_in", "ffn2_w_out")
SMALL = ("rel_bias_table", "ffn1_norm", "mix_norm", "forget_bias", "fox_q_norm", "fox_k_norm",
         "swa_q_norm", "swa_k_norm", "swa_sinks", "ffn2_norm")
WEIGHTS = ("meta_tokens", "rel_bias_table", "ffn1_norm", "ffn1_w_in", "ffn1_w_out", "mix_norm", "w_in",
           "forget_bias", "fox_q_norm", "fox_k_norm", "swa_q_norm", "swa_k_norm", "swa_sinks", "w_branch_fox",
           "w_branch_swa", "w_out", "ffn2_norm", "ffn2_w_in", "ffn2_w_out")


def _pack(arrs, width, row_multiple, dtype):
    lead = arrs[0].shape[:-1]
    flat = jnp.concatenate([a.astype(dtype) for a in arrs], axis=-1)
    n = flat.shape[-1]
    rows = -(-n // width)
    rows = -(-rows // row_multiple) * row_multiple
    flat = jnp.pad(flat, [(0, 0)] * len(lead) + [(0, rows * width - n)])
    return flat.reshape(lead + (rows, width))


def _unpack(flat, shapes):
    flat = flat.reshape(-1)
    out, off = [], 0
    for s in shapes:
        n = int(np.prod(s))
        out.append(flat[off:off + n].reshape(s))
        off += n
    return out


def _swa_head_order():
    return [4 * (j % 2) + j // 2 for j in range(8)]


def _permute_heads(a, axis, inverse=False):
    order = _swa_head_order()
    if inverse:
        order = [order.index(hd) for hd in range(8)]
    parts = [lax.slice_in_dim(a, hd * HEAD_DIM, (hd + 1) * HEAD_DIM, axis=axis) for hd in order]
    return jnp.concatenate(parts, axis=axis)


def _pad_heads(a):
    return jnp.pad(a.reshape(8, HEAD_DIM, -1), ((0, 0), (0, LANES - HEAD_DIM), (0, 0))).reshape(8 * LANES, -1)


def _unpad_heads(a):
    return a.reshape(8, LANES, -1)[:, :HEAD_DIM].reshape(8 * HEAD_DIM, -1)


def _swa_rows(a, inverse=False):
    shape = (4, 2) if inverse else (2, 4)
    return a.reshape(shape + (HEAD_DIM, -1)).transpose(1, 0, 2, 3).reshape(a.shape)


def _w_in_rows(d):
    return np.cumsum([0, 512, 512, 512, 8, 512, 128, 128, d, d])


def _reorder_w_in(wt, cols):
    o = _w_in_rows(cols.d)
    qa, ka, va, fa, qb, kb, vb, ga, gb = [wt[o[i]:o[i + 1]] for i in range(9)]
    zeros = jnp.zeros((cols.np - cols.fa - 8, wt.shape[1]), wt.dtype)
    return jnp.concatenate([ga, gb, _pad_heads(qa), _pad_heads(ka), _pad_heads(va), _swa_rows(qb), kb, vb, fa, zeros],
                           axis=0)


def _restore_w_in(wpt, cols, width):
    seg = lambda off, n: wpt[off:off + n]
    rows = jnp.concatenate([_unpad_heads(seg(cols.qa, 1024)), _unpad_heads(seg(cols.ka, 1024)),
                            _unpad_heads(seg(cols.va, 1024)), seg(cols.fa, 8), _swa_rows(seg(cols.qb, 512), True),
                            seg(cols.kb, 128), seg(cols.vb, 128), seg(cols.ga, cols.d), seg(cols.gb, cols.d)], axis=0)
    return rows.reshape(N_DEV, width, -1)


def _lane_pad(v):
    return jnp.pad(v, ((0, 0), (0, LANES - v.shape[1])))


def kernel(x, meta_tokens, rel_bias_table, ffn1_norm, ffn1_w_in, ffn1_w_out, mix_norm, w_in, forget_bias, fox_q_norm, fox_k_norm, swa_q_norm, swa_k_norm, swa_sinks, w_branch_fox, w_branch_swa, w_out, ffn2_norm, ffn2_w_in, ffn2_w_out, loss_target, m_meta_tokens, m_rel_bias_table, m_ffn1_norm, m_ffn1_w_in, m_ffn1_w_out, m_mix_norm, m_w_in, m_forget_bias, m_fox_q_norm, m_fox_k_norm, m_swa_q_norm, m_swa_k_norm, m_swa_sinks, m_w_branch_fox, m_w_branch_swa, m_w_out, m_ffn2_norm, m_ffn2_w_in, m_ffn2_w_out, v_meta_tokens, v_rel_bias_table, v_ffn1_norm, v_ffn1_w_in, v_ffn1_w_out, v_mix_norm, v_w_in, v_forget_bias, v_fox_q_norm, v_fox_k_norm, v_swa_q_norm, v_swa_k_norm, v_swa_sinks, v_w_branch_fox, v_w_branch_swa, v_w_out, v_ffn2_norm, v_ffn2_w_in, v_ffn2_w_out):
    args = dict(locals())
    wts = {n: args[n] for n in WEIGHTS}
    mom1 = {n: args["m_" + n] for n in WEIGHTS}
    mom2 = {n: args["v_" + n] for n in WEIGHTS}

    seq, d = x.shape[1], x.shape[2]
    m_rows = seq + LANES
    depth = ffn1_norm.shape[0]
    fb = ffn1_w_in.shape[2]
    fo = ffn1_w_out.shape[1]
    din_shard = w_in.shape[2]
    cols = _Cols(d)
    scale = HEAD_DIM ** -0.5
    dev = 4 * lax.axis_index("x") + 2 * lax.axis_index("y") + lax.axis_index("c")

    groups = {"ffn1": ("ffn1_w_in", "ffn1_w_out"), "mix": ("w_in", "w_branch_fox", "w_branch_swa", "w_out"),
              "ffn2": ("ffn2_w_in", "ffn2_w_out"), "ffn1_in": ("ffn1_w_in",), "ffn1_out": ("ffn1_w_out",),
              "ffn2_in": ("ffn2_w_in",), "ffn2_out": ("ffn2_w_out",)}
    flipped = ("ffn1_w_in", "ffn2_w_in")
    for n in flipped:
        wts[n], mom1[n], mom2[n] = (jnp.swapaxes(a, 1, 2) for a in (wts[n], mom1[n], mom2[n]))
    to_rows = lambda a: jnp.transpose(a, (2, 0, 1)).reshape(din_shard, depth * d)
    from_rows = lambda a: jnp.transpose(a.reshape(din_shard, depth, d), (1, 2, 0))
    wts["w_in"], mom1["w_in"], mom2["w_in"] = (to_rows(a) for a in (wts["w_in"], mom1["w_in"], mom2["w_in"]))
    shard = {n: wts[n].astype(BF16) for n in BIG}
    w_in_rows = shard.pop("w_in")
    shard["w_in"] = [w_in_rows[:, l * d:(l + 1) * d] for l in range(depth)]
    full, parts, gw = {}, {}, {}

    def keys_of(stages):
        return [(n, l) for g, l in stages if l < depth for n in groups[g]]

    def gather_rider(stages):
        return Rider([shard[n][l] for n, l in keys_of(stages)], True)

    def scatter_rider(stages):
        return Rider([gw[k] for k in keys_of(stages)], False)

    def ffn_weights(tag, l):
        return full[tag + "_w_in", l], full[tag + "_w_out", l].reshape(4, fb, d)

    def mixer_weights(l):
        wp = _reorder_w_in(full["w_in", l].reshape(N_DEV * din_shard, d), cols)
        wbf = jnp.concatenate([full["w_branch_fox", l][j] for j in range(N_DEV)], axis=1)
        wbf = jnp.pad(wbf.reshape(8, HEAD_DIM, d), ((0, 0), (0, LANES - HEAD_DIM), (0, 0))).reshape(8 * LANES, d)
        wbs = _permute_heads(jnp.concatenate([full["w_branch_swa", l][j] for j in range(N_DEV)], axis=1), 0)
        return wp, wbf, wbs, full["w_out", l].reshape(d, d)

    full.update(zip(keys_of([("ffn1", 0)]), exchange_hbm(gather_rider([("ffn1", 0)]).srcs, True, "gather_first")))
    meta_all = gather_small(meta_tokens.reshape(1, N_META, -1), "gather_meta")
    meta_full = meta_all.transpose(1, 0, 2).reshape(N_META, d)
    tile8 = lambda g, s=1.0: jnp.tile(g.reshape(1, HEAD_DIM) * s, (1, 8))
    tile2 = lambda g: jnp.tile(g.reshape(1, HEAD_DIM), (1, 2))
    data_lanes = lambda g, s=1.0: _lane_pad(g.reshape(1, HEAD_DIM) * s)
    bias = bias_build(rel_bias_table, "swa_bias")

    first = jnp.concatenate([jnp.zeros((PAD_FRONT, d), F32), meta_full], axis=0)
    h = jnp.concatenate([first, x[0]], axis=0)
    saved, lw = [], []
    for l in range(depth):
        s, w = {"h0": h}, {}
        w["ffn1_in"], w["ffn1_out"] = ffn_weights("ffn1", l)
        stages = [("mix", l)]
        (h, s["n1"], s["a1"], s["fg1"], s["fu1"]), got = ffn_fwd(h, ffn1_norm[l:l + 1], w["ffn1_in"], w["ffn1_out"],
                                                          f"ffn1_fwd_{l}", gather_rider(stages))
        full.update(zip(keys_of(stages), got))
        s["h1"] = h
        w["wp"], w["wbf"], w["wbs"], w["wo"] = mixer_weights(l)
        s["nm"], s["proj"] = mixer_proj(h, mix_norm[l:l + 1], w["wp"], f"mixer_proj_{l}")
        s["gains"] = (data_lanes(fox_q_norm[l], scale), data_lanes(fox_k_norm[l]), tile8(swa_q_norm[l], scale),
                      tile2(swa_k_norm[l]))
        s["fbias"] = _lane_pad(forget_bias[l:l + 1])
        qf, kf, vf, kt, vt, qb, kb, vb = qk_post(s["proj"], s["gains"], s["fbias"], cols, f"qk_post_{l}")
        s.update(qf=qf, kf=kf, vf=vf, kt=kt, qb=qb, kb=kb, vb=vb)
        stages = [("ffn2", l)]
        (s["o_fox"], s["lse_fox"]), got = fox_fwd(qf, kf, vt, f"fox_fwd_{l}", gather_rider(stages))
        full.update(zip(keys_of(stages), got))
        stages = [("ffn1_out", l + 1)]
        (s["o_swa"], s["lse_swa"]), got = swa_fwd(qb, kb, vb, bias, swa_sinks[l], f"swa_fwd_{l}", gather_rider(stages))
        full.update(zip(keys_of(stages), got))
        h, s["t_fox"], s["t_swa"] = branch_out(h, s["o_fox"], s["o_swa"], s["proj"], w["wbf"], w["wbs"], w["wo"], cols,
                                               f"branch_out_{l}")
        s["h2"] = h
        w["ffn2_in"], w["ffn2_out"] = ffn_weights("ffn2", l)
        stages = [("ffn1_in", l + 1)]
        (h, s["n2"], s["a2"], s["fg2"], s["fu2"]), got = ffn_fwd(h, ffn2_norm[l:l + 1], w["ffn2_in"], w["ffn2_out"],
                                                          f"ffn2_fwd_{l}", gather_rider(stages))
        full.update(zip(keys_of(stages), got))
        saved.append(s)
        lw.append(w)

    dh, loss_part = loss_head(h, loss_target[0], "loss_head")

    gs = {n: [None] * depth for n in SMALL}
    dbias_total = None
    for l in reversed(range(depth)):
        w, s = lw[l], saved[l]

        def ffn_back(dh, tag, hin, norm, n_in, a, f_gate, f_up, stages):
            (dh_in, dg, du, dgn, dhs), got = ffn_bwd(dh, hin, norm, f_gate, f_up, w[tag + "_in"], w[tag + "_out"],
                                                     f"{tag}_bwd_{l}", scatter_rider(stages))
            parts.update(zip(keys_of(stages), got))
            gw[tag + "_w_out", l] = matmul_tn(a, dhs[None], f"{tag}_dwo_{l}").reshape(N_DEV, fo, d)
            stages = [(tag + "_out", l)]
            gw[tag + "_w_in", l], got = matmul_tn(dg, n_in[None], f"{tag}_dwi_{l}", x2=du,
                                                  rider=scatter_rider(stages))
            parts.update(zip(keys_of(stages), got))
            return dh_in, dgn

        dh, gs["ffn2_norm"][l] = ffn_back(dh, "ffn2", s["h2"], ffn2_norm[l:l + 1], s["n2"], s["a2"], s["fg2"],
                                          s["fu2"], [("ffn1_in", l + 1)])

        y, dtf, dts, dga, dgb, dof, dos, delta = branch_out_bwd(dh, s["o_fox"], s["t_fox"], s["t_swa"], s["proj"],
                                                                w["wbf"], w["wbs"], w["wo"], cols,
                                                                f"branch_out_bwd_{l}")
        gw["w_out", l] = matmul_tn(y[None], dh[None], f"dw_out_{l}").reshape(N_DEV, d // N_DEV, d)
        to_shards = lambda a: a.reshape(512, N_DEV, d // N_DEV).transpose(1, 0, 2)
        gw["w_branch_fox", l] = to_shards(matmul_tn(s["o_fox"][None], dtf[None], f"dw_branch_fox_{l}")[0]
                                          .reshape(8, LANES, d)[:, :HEAD_DIM].reshape(512, d))
        gw["w_branch_swa", l] = to_shards(_permute_heads(
            matmul_tn(s["o_swa"][None], dts[None], f"dw_branch_swa_{l}")[0], 0, inverse=True))

        stages = [("ffn2_in", l)]
        (dkf, dvf, dqf, dc_rows), got = fox_bwd(s["qf"], s["kf"], s["vf"], s["kt"], dof, s["lse_fox"],
                                         delta[:, :8].T.reshape(8, 1, m_rows), f"fox_bwd_{l}", scatter_rider(stages))
        parts.update(zip(keys_of(stages), got))
        dc = _lane_pad(dc_rows.reshape(8, m_rows).T)
        dqb, dkb, dvb, dbias, dsink = swa_bwd(s["qb"], s["kb"], s["vb"], bias, swa_sinks[l], s["o_swa"], s["lse_swa"],
                                              dos, f"swa_bwd_{l}")
        dbias_total = dbias if dbias_total is None else dbias_total + dbias
        gs["swa_sinks"][l] = dsink[0, :8]
        dproj, ggqa, ggka, ggqb, ggkb, gfb = qk_post_bwd(s["proj"], s["gains"], s["fbias"], dqf, dkf, dvf, dqb, dkb,
                                                         dvb, dc, dga, dgb, cols, f"qk_post_bwd_{l}")
        gs["fox_q_norm"][l] = ggqa[0, :HEAD_DIM] * scale
        gs["fox_k_norm"][l] = ggka[0, :HEAD_DIM]
        gs["swa_q_norm"][l] = ggqb.reshape(8, HEAD_DIM).sum(0) * scale
        gs["swa_k_norm"][l] = ggkb.reshape(2, HEAD_DIM).sum(0)
        gs["forget_bias"][l] = gfb[0, :8]
        dwp = matmul_tn(dproj[None], s["nm"][None], f"dw_in_{l}", tk=1024 if cols.np % 1024 == 0 else cols.np)[0]
        gw["w_in", l] = _restore_w_in(dwp, cols, din_shard)
        dh, gs["mix_norm"][l] = dproj_bwd(dh, s["h1"], mix_norm[l:l + 1], dproj, w["wp"], f"dproj_bwd_{l}")

        dh, gs["ffn1_norm"][l] = ffn_back(dh, "ffn1", s["h0"], ffn1_norm[l:l + 1], s["n1"], s["a1"], s["fg1"],
                                          s["fu1"], [("mix", l)])

    grad_x = dh[LANES:][None]
    dmeta = dh[PAD_FRONT:LANES]
    dtable = bias_reduce(dbias_total, "swa_dbias")[:, :8]

    last = ("ffn1_w_in", 0)
    send_sems, recv_sems, src_thru, land_thru, token = scatter_start(gw[last], "scatter_last_start")
    big_out = [{}, {}, {}, {}]
    for n in BIG:
        if n != last[0]:
            update = adamw_sum_cols if n == "w_in" else adamw_sum
            outs = update([parts[n, l] for l in range(depth)], wts[n], mom1[n], mom2[n], f"adamw_{n}", after=token)
            for k in range(4):
                big_out[k][n] = outs[k]
    sent, landed = scatter_wait(send_sems, recv_sems, src_thru, land_thru,
                                [big_out[1][n] for n in BIG if n != last[0]], "scatter_last_wait")
    parts[last] = lax.dynamic_update_slice_in_dim(landed, lax.dynamic_slice_in_dim(sent, dev, 1, axis=0), dev, axis=0)
    outs = adamw_sum([parts[last[0], l] for l in range(depth)], wts[last[0]], mom1[last[0]], mom2[last[0]],
                     f"adamw_{last[0]}")
    for k in range(4):
        big_out[k][last[0]] = outs[k]

    small_g = {n: (jnp.stack(gs[n]) if n != "rel_bias_table" else None) for n in SMALL}
    small_g["rel_bias_table"] = dtable
    pieces = [loss_part[0:1, 0:1].reshape(1, 1)] + [small_g[n].reshape(1, -1) for n in SMALL] + [dmeta.reshape(1, -1)]
    small_shapes = [(1,)] + [wts[n].shape for n in SMALL] + [(N_META, d)]
    total = allsum_small(_pack(pieces, LANES, 8, F32), "allsum_small")
    summed = _unpack(total, small_shapes)
    loss = summed[0][0]
    g_small = dict(zip(SMALL, summed[1:1 + len(SMALL)]))
    g_meta = lax.dynamic_slice_in_dim(summed[-1], dev * (d // N_DEV), d // N_DEV, axis=1)
    names = SMALL + ("meta_tokens",)
    g_small["meta_tokens"] = g_meta
    pk = lambda src: _pack([src[n].reshape(1, -1) for n in names], LANES, 8, F32)[0]
    small_out = [dict(zip(names, _unpack(o, [wts[n].shape for n in names])))
                 for o in adamw_small(pk(g_small), pk(wts), pk(mom1), pk(mom2), "adamw_small")]

    for out in big_out:
        for n in flipped:
            out[n] = jnp.swapaxes(out[n], 1, 2)
        out["w_in"] = from_rows(out["w_in"])
    grads = {**big_out[0], **g_small}
    delta = {**big_out[1], **small_out[0]}
    new_m = {**big_out[2], **small_out[1]}
    new_v = {**big_out[3], **small_out[2]}
    return (loss, grad_x, *[grads[n] for n in WEIGHTS], *[delta[n] for n in WEIGHTS],
            *[new_m[n] for n in WEIGHTS], *[new_v[n] for n in WEIGHTS])
```
